```python
import math
import jax, jax.numpy as jnp
from jax import lax
import numpy as np

D_MODEL = 1024
BATCH = 32
SEQ = 256
DEPTH = 2
DEC_BATCH = 2
DEC_SEQ = 2048
PAST_LEN = 512

GRID_W = 64
N_EVEN = (DEPTH + 1) // 2
N_ODD = DEPTH // 2
H_A = 4
DK_A = 64
DV_A = 128
GATE_RANK = 16
GATE_TAU = 16.0
CHUNK = 64
H_B = 4
DH_B = 64
DV_B = 2 * DH_B
H_C = 8
HKV_C = 2
DH_C = 128
D_FF = 2816
N_EXPERTS = 8
TOP_K = 2
D_FF_E = 3584
Q_BLOCK = 128
ROPE_THETA = 10000.0
EPS = 1e-6
A_QK = H_A * DK_A
A_V = H_A * DV_A
B_QK = H_B * 2 * DH_B
B_V = H_B * DV_B
EVEN_IN = 2 * A_QK + 2 * A_V + 2 * GATE_RANK + 2 * B_QK + B_V
EVEN_MIX = A_V + B_V
ODD_IN = (H_C + 2 * HKV_C) * DH_C

kernel_name = 'hybrid_diffusion_gla_diffattn_gqa_moe_step'


def rmsnorm(x, g):
    xf = x.astype(jnp.float32)
    y = xf * lax.rsqrt(jnp.mean(xf * xf, axis=-1, keepdims=True) + EPS)
    return (y * g.astype(jnp.float32)).astype(x.dtype)


def grid_angles(n, rot_dim):
    n_rows = n // GRID_W
    rows = jnp.repeat(jnp.arange(n_rows, dtype=jnp.float32), GRID_W)
    cols = jnp.tile(jnp.arange(GRID_W, dtype=jnp.float32), n_rows)
    half = rot_dim // 2
    freqs = ROPE_THETA ** (-jnp.arange(0, half, 2, dtype=jnp.float32) / half)
    return rows[:, None] * freqs, cols[:, None] * freqs


def rope_half(x, ang):
    d2 = x.shape[-1] // 2
    shape = (1, ang.shape[0]) + (1,) * (x.ndim - 3) + (d2,)
    cos = jnp.cos(ang).reshape(shape).astype(x.dtype)
    sin = jnp.sin(ang).reshape(shape).astype(x.dtype)
    x1, x2 = x[..., :d2], x[..., d2:]
    return jnp.concatenate([x1 * cos - x2 * sin, x2 * cos + x1 * sin], axis=-1)


def axial_rope(x):
    d = x.shape[-1]
    ang_r, ang_c = grid_angles(x.shape[1], d)
    h = d // 2
    return jnp.concatenate([rope_half(x[..., :h], ang_r), rope_half(x[..., h:], ang_c)], axis=-1)


def sweep_query_blocks(fn, q):
    b, n = q.shape[:2]
    nb = n // Q_BLOCK
    qb = jnp.moveaxis(q.reshape((b, nb, Q_BLOCK) + q.shape[2:]), 1, 0)
    out = jnp.moveaxis(lax.map(fn, qb), 0, 1)
    return out.reshape((b, n) + out.shape[3:])


def gla_chunked(q, k, v, log_a, s0):
    b, n, h, dk = q.shape
    dv = v.shape[-1]
    nc = n // CHUNK

    def chunks(t):
        return jnp.moveaxis(t.astype(jnp.float32).reshape((b, nc, CHUNK) + t.shape[2:]), 1, 0)

    tril = jnp.tril(jnp.ones((CHUNK, CHUNK), dtype=bool))[None, :, :, None, None]

    def step(S, inp):
        qc, kc, vc, gc = inp
        bcum = jnp.cumsum(gc, axis=1)
        o_inter = jnp.einsum('bthd,bhde->bthe', qc * jnp.exp(bcum), S)
        rel = jnp.where(tril, bcum[:, :, None] - bcum[:, None, :], -jnp.inf)
        scores = jnp.einsum('bthd,bshd,btshd->bhts', qc, kc, jnp.exp(rel))
        o_intra = jnp.einsum('bhts,bshe->bthe', scores, vc)
        b_last = bcum[:, -1]
        S = jnp.exp(b_last)[..., None] * S + jnp.einsum(
            'bshd,bshe->bhde', kc * jnp.exp(b_last[:, None] - bcum), vc)
        return S, o_inter + o_intra

    s_fin, o = lax.scan(step, s0.astype(jnp.float32),
                        (chunks(q), chunks(k), chunks(v), chunks(log_a)))
    o = jnp.moveaxis(o, 0, 1).reshape(b, n, h, dv)
    return o, s_fin


def diff_attention(q, k, v, lam):
    scale = DH_B ** -0.5

    def block(qb):
        s = jnp.einsum('bqhid,bkhid->bihqk', qb, k).astype(jnp.float32) * scale
        p = jax.nn.softmax(s, axis=-1)
        w = p[:, 0] - lam * p[:, 1]
        return jnp.einsum('bhqk,bkhe->bqhe', w.astype(v.dtype), v)

    return sweep_query_blocks(block, q)


def gqa_attention(q, k, v):
    b, nq = q.shape[:2]
    qg = q.reshape(b, nq, HKV_C, H_C // HKV_C, DH_C)
    scale = DH_C ** -0.5

    def block(qb):
        s = jnp.einsum('bqhgd,bkhd->bhgqk', qb, k).astype(jnp.float32) * scale
        p = jax.nn.softmax(s, axis=-1)
        return jnp.einsum('bhgqk,bkhd->bqhgd', p.astype(v.dtype), v)

    return sweep_query_blocks(block, qg).reshape(b, nq, H_C * DH_C)


def lambda_init(layer):
    return 0.8 - 0.6 * math.exp(-0.3 * layer)


def even_mixer(h, w_in, w_gate2, b_gate, g_gla, lq1, lk1, lq2, lk2, g_sub, w_out, lam_init, ctx):
    b, n, _ = h.shape
    z = h @ w_in
    cuts = list(np.cumsum([A_QK, A_QK, A_V, A_V, GATE_RANK, GATE_RANK, B_QK, B_QK]))
    qa, ka, va, ra, gf, gb, qb, kb, vb = jnp.split(z, cuts, axis=-1)
    qa = qa.reshape(b, n, H_A, DK_A) * (DK_A ** -0.5)
    ka = ka.reshape(b, n, H_A, DK_A)
    va = va.reshape(b, n, H_A, DV_A)
    log_a_f = (jax.nn.log_sigmoid((gf @ w_gate2[0] + b_gate[0]).astype(jnp.float32)) / GATE_TAU
               ).reshape(b, n, H_A, DK_A)
    log_a_b = (jax.nn.log_sigmoid((gb @ w_gate2[1] + b_gate[1]).astype(jnp.float32)) / GATE_TAU
               ).reshape(b, n, H_A, DK_A)
    qb = qb.reshape(b, n, H_B, 2, DH_B)
    kb = kb.reshape(b, n, H_B, 2, DH_B)
    vb = vb.reshape(b, n, H_B, DV_B)
    lam = (jnp.exp(jnp.sum(lq1 * lk1).astype(jnp.float32))
           - jnp.exp(jnp.sum(lq2 * lk2).astype(jnp.float32)) + lam_init)
    if ctx is None:
        s0f = jnp.zeros((b, H_A, DK_A, DV_A), jnp.float32)
        s0b = s0f
        q_b, k_all, v_all = qb, kb, vb
    else:
        state_a, k_ctx, v_ctx = ctx
        s0f, s0b = state_a[:, 0], state_a[:, 1]
        q_b = axial_rope(qb)
        k_all = jnp.concatenate([axial_rope(kb), k_ctx.astype(kb.dtype)], axis=1)
        v_all = jnp.concatenate([vb, v_ctx.astype(vb.dtype)], axis=1)
    o_f, s_f = gla_chunked(qa, ka, va, log_a_f, s0f)
    o_b, s_b = gla_chunked(jnp.flip(qa, 1), jnp.flip(ka, 1), jnp.flip(va, 1), jnp.flip(log_a_b, 1), s0b)
    o_a = rmsnorm(o_f + jnp.flip(o_b, 1), g_gla).astype(h.dtype) * jax.nn.silu(ra.reshape(b, n, H_A, DV_A))
    o_b_attn = rmsnorm(diff_attention(q_b, k_all, v_all, lam), g_sub) * (1.0 - lam_init)
    y = jnp.concatenate([o_a.reshape(b, n, A_V), o_b_attn.reshape(b, n, B_V).astype(h.dtype)], axis=-1) @ w_out
    if ctx is None:
        return y, (jnp.stack([s_f, s_b], axis=1).astype(h.dtype), kb, vb)
    return y, None


def odd_mixer(h, w_in, g_q, g_k, w_out, ctx):
    b, n, _ = h.shape
    q, k, v = jnp.split(h @ w_in, [H_C * DH_C, (H_C + HKV_C) * DH_C], axis=-1)
    q = rmsnorm(q.reshape(b, n, H_C, DH_C), g_q)
    k = rmsnorm(k.reshape(b, n, HKV_C, DH_C), g_k)
    v = v.reshape(b, n, HKV_C, DH_C)
    if ctx is None:
        y = gqa_attention(q, k, v) @ w_out
        return y, (k, v)
    k_ctx, v_ctx = ctx
    k_all = jnp.concatenate([axial_rope(k), k_ctx.astype(k.dtype)], axis=1)
    v_all = jnp.concatenate([v, v_ctx.astype(v.dtype)], axis=1)
    return gqa_attention(axial_rope(q), k_all, v_all) @ w_out, None


def swiglu(h, w_gate, w_up, w_down):
    return (jax.nn.silu(h @ w_gate) * (h @ w_up)) @ w_down


def moe_swiglu(h, w_router, w_gate, w_up, w_down):
    logits = (h @ w_router).astype(jnp.float32)
    top_v, top_i = lax.top_k(logits, TOP_K)
    wts = jax.nn.softmax(top_v, axis=-1)
    combine = jnp.sum(jax.nn.one_hot(top_i, N_EXPERTS, dtype=jnp.float32) * wts[..., None], axis=-2)
    out = jnp.zeros_like(h)
    for e in range(N_EXPERTS):
        out = out + combine[..., e:e + 1].astype(h.dtype) * swiglu(h, w_gate[e], w_up[e], w_down[e])
    return out


def setup_inputs(seed: int = 0) -> dict:
    key = jax.random.key(seed)
    ks = jax.random.split(key, 40)
    D = D_MODEL

    def nrm(k, shape, scale):
        return jax.random.normal(k, shape, jnp.float32) * scale

    return {
        'x_prompt': nrm(ks[0], (BATCH, SEQ, D), 1.0),
        'x_sample': nrm(ks[1], (DEC_BATCH, DEC_SEQ, D), 1.0),
        'state_a': nrm(ks[2], (DEC_BATCH, N_EVEN, 2, H_A, DK_A, DV_A), 0.5),
        'cache_b_k': nrm(ks[3], (DEC_BATCH, N_EVEN, PAST_LEN, H_B, 2, DH_B), 1.0),
        'cache_b_v': nrm(ks[4], (DEC_BATCH, N_EVEN, PAST_LEN, H_B, DV_B), 1.0),
        'cache_c_k': nrm(ks[5], (DEC_BATCH, N_ODD, PAST_LEN, HKV_C, DH_C), 1.0),
        'cache_c_v': nrm(ks[6], (DEC_BATCH, N_ODD, PAST_LEN, HKV_C, DH_C), 1.0),
        'c': nrm(ks[7], (DEC_BATCH, D), 1.0),
        'c_ctx': nrm(ks[8], (D,), 1.0),
        'w_mod': nrm(ks[9], (DEPTH, D, 6 * D), 0.5 * D ** -0.5),
        'b_mod': nrm(ks[10], (DEPTH, 6 * D), 0.02),
        'norm_mix': 1.0 + nrm(ks[11], (DEPTH, D), 0.02),
        'norm_ffn': 1.0 + nrm(ks[12], (DEPTH, D), 0.02),
        'w_in_even': nrm(ks[13], (N_EVEN, D, EVEN_IN), D ** -0.5),
        'w_gate2_a': nrm(ks[14], (N_EVEN, 2, GATE_RANK, A_QK), GATE_RANK ** -0.5),
        'b_gate_a': nrm(ks[15], (N_EVEN, 2, A_QK), 0.1),
        'g_gla': 1.0 + nrm(ks[16], (N_EVEN, DV_A), 0.02),
        'lam_q1': nrm(ks[17], (N_EVEN, DH_B), 0.1),
        'lam_k1': nrm(ks[18], (N_EVEN, DH_B), 0.1),
        'lam_q2': nrm(ks[19], (N_EVEN, DH_B), 0.1),
        'lam_k2': nrm(ks[20], (N_EVEN, DH_B), 0.1),
        'g_sub_b': 1.0 + nrm(ks[21], (N_EVEN, DV_B), 0.02),
        'w_out_even': nrm(ks[22], (N_EVEN, EVEN_MIX, D), EVEN_MIX ** -0.5),
        'w_in_odd': nrm(ks[23], (N_ODD, D, ODD_IN), D ** -0.5),
        'g_q_c': 1.0 + nrm(ks[24], (N_ODD, DH_C), 0.02),
        'g_k_c': 1.0 + nrm(ks[25], (N_ODD, DH_C), 0.02),
        'w_out_odd': nrm(ks[26], (N_ODD, H_C * DH_C, D), (H_C * DH_C) ** -0.5),
        'ffn_gate': nrm(ks[27], (N_EVEN, D, D_FF), D ** -0.5),
        'ffn_up': nrm(ks[28], (N_EVEN, D, D_FF), D ** -0.5),
        'ffn_down': nrm(ks[29], (N_EVEN, D_FF, D), D_FF ** -0.5),
        'w_router': nrm(ks[30], (N_ODD, D, N_EXPERTS), D ** -0.5),
        'exp_gate': nrm(ks[31], (N_ODD, N_EXPERTS, D, D_FF_E), D ** -0.5),
        'exp_up': nrm(ks[32], (N_ODD, N_EXPERTS, D, D_FF_E), D ** -0.5),
        'exp_down': nrm(ks[33], (N_ODD, N_EXPERTS, D_FF_E, D), D_FF_E ** -0.5),
        'norm_final': 1.0 + nrm(ks[34], (D,), 0.02),
    }


def reference(x_prompt, x_sample, state_a, cache_b_k, cache_b_v, cache_c_k, cache_c_v, c, c_ctx,
              w_mod, b_mod, norm_mix, norm_ffn, w_in_even, w_gate2_a, b_gate_a, g_gla,
              lam_q1, lam_k1, lam_q2, lam_k2, g_sub_b, w_out_even, w_in_odd, g_q_c, g_k_c,
              w_out_odd, ffn_gate, ffn_up, ffn_down, w_router, exp_gate, exp_up, exp_down,
              norm_final):
    def layer(l, x, cond, ctx):
        mod = jax.nn.silu(cond) @ w_mod[l] + b_mod[l]
        sh1, sc1, g1, sh2, sc2, g2 = [m[:, None, :] for m in jnp.split(mod, 6, axis=-1)]
        i = l // 2
        h = rmsnorm(x, norm_mix[l]) * (1.0 + sc1) + sh1
        if l % 2 == 0:
            y, new = even_mixer(h, w_in_even[i], w_gate2_a[i], b_gate_a[i], g_gla[i],
                                lam_q1[i], lam_k1[i], lam_q2[i], lam_k2[i], g_sub_b[i],
                                w_out_even[i], lambda_init(l), ctx)
        else:
            y, new = odd_mixer(h, w_in_odd[i], g_q_c[i], g_k_c[i], w_out_odd[i], ctx)
        x = x + g1 * y
        h = rmsnorm(x, norm_ffn[l]) * (1.0 + sc2) + sh2
        if l % 2 == 0:
            f = swiglu(h, ffn_gate[i], ffn_up[i], ffn_down[i])
        else:
            f = moe_swiglu(h, w_router[i], exp_gate[i], exp_up[i], exp_down[i])
        return x + g2 * f, new

    xp = x_prompt
    sa, bk, bv, ck, cv = [], [], [], [], []
    for l in range(DEPTH):
        xp, new = layer(l, xp, c_ctx[None, :], None)
        if l % 2 == 0:
            sa.append(new[0])
            bk.append(new[1])
            bv.append(new[2])
        else:
            ck.append(new[0])
            cv.append(new[1])

    xs = x_sample
    for l in range(DEPTH):
        i = l // 2
        if l % 2 == 0:
            ctx = (state_a[:, i], cache_b_k[:, i], cache_b_v[:, i])
        else:
            ctx = (cache_c_k[:, i], cache_c_v[:, i])
        xs, _ = layer(l, xs, c, ctx)

    y_prompt = rmsnorm(xp, norm_final)
    y_sample = rmsnorm(xs, norm_final)
    return (y_prompt, y_sample, jnp.stack(sa, axis=1), jnp.stack(bk, axis=1), jnp.stack(bv, axis=1),
            jnp.stack(ck, axis=1), jnp.stack(cv, axis=1))
```

```python
import functools
import math

import numpy as np
import jax
import jax.numpy as jnp
from jax import lax
from jax.experimental import pallas as pl
from jax.experimental.pallas import tpu as pltpu

D_MODEL = 1024
BATCH = 32
SEQ = 256
DEPTH = 2
DEC_BATCH = 2
DEC_SEQ = 2048
PAST_LEN = 512
GRID_W = 64
H_A, DK_A, DV_A = 4, 64, 128
GATE_RANK = 16
GATE_TAU = 16.0
H_B, DH_B = 4, 64
DV_B = 2 * DH_B
H_C, HKV_C, DH_C = 8, 2, 128
D_FF = 2816
N_EXPERTS = 8
D_FF_E = 3584
ROPE_THETA = 10000.0
EPS = 1e-6
A_QK = H_A * DK_A
A_V = H_A * DV_A
B_QK = H_B * 2 * DH_B
B_V = H_B * DV_B

F32 = jnp.float32
BF16 = jnp.bfloat16

V7X_VMEM_BYTES = 64 * 1024 * 1024
VMEM_LIMIT = V7X_VMEM_BYTES * 7 // 8
LANES = 128
TM = 256
N_CTX = BATCH * SEQ
N_LAT = DEC_BATCH * DEC_SEQ
M_TOK = N_CTX + N_LAT
NB = M_TOK // TM
NCB = N_CTX // TM
LBB = DEC_SEQ // TM
N_COND = 8
GLA_C = 64
GLA_SB = 16
TME = 512
NT_E = 2 * M_TOK // TME + N_EXPERTS
TF_E = D_FF_E // 2
TQ_LAT = 256

assert SEQ == TM and DEC_SEQ % TM == 0 and DEPTH == 2


def _cparams(sem):
    return pltpu.CompilerParams(dimension_semantics=sem, vmem_limit_bytes=VMEM_LIMIT)


def _group(i):
    return jnp.where(i < NCB, 0, 1 + (i - NCB) // LBB)


def _pos_block(i):
    return jnp.where(i < NCB, 0, (i - NCB) % LBB)


def _mod_spec(chunk):
    return pl.BlockSpec((None, 1, D_MODEL), lambda i, c=chunk: (_group(i), 0, c))


def _row_spec(width, col=0):
    return pl.BlockSpec((TM, width), lambda i, c=col: (i, c))


def _const_spec(shape):
    nd = len(shape)
    return pl.BlockSpec(shape, lambda i, nd=nd: (0,) * nd)


def _dot(a, b):
    return jnp.dot(a, b, preferred_element_type=F32)


def _dot_nt(a, b):
    return lax.dot_general(a, b, (((1,), (1,)), ((), ())), preferred_element_type=F32)


def _split2(a):
    hi = a.astype(BF16)
    lo = (a - hi.astype(F32)).astype(BF16)
    return hi, lo


def _dot_hi(a, b):
    a_hi, a_lo = _split2(a)
    b_hi, b_lo = _split2(b)
    return _dot(a_hi, b_hi) + (_dot(a_hi, b_lo) + _dot(a_lo, b_hi))


def _silu(x):
    return x * (1.0 / (1.0 + jnp.exp(-x)))


def _log_sigmoid(x):
    return jnp.minimum(x, 0.0) - jnp.log(1.0 + jnp.exp(-jnp.abs(x)))


def _rms(x, g):
    return x * lax.rsqrt(jnp.mean(x * x, axis=-1, keepdims=True) + EPS) * g


def _norm_mod(x, gamma, sc, sh):
    return _rms(x, gamma) * (1.0 + sc) + sh


def _rope(x, cos, sin, half):
    lane = lax.broadcasted_iota(jnp.int32, x.shape, 1)
    first = (lane % (2 * half)) < half
    swapped = jnp.where(first, pltpu.roll(x, LANES - half, 1), pltpu.roll(x, half, 1))
    return x * cos + swapped * sin


def _mod_kernel(cond_ref, w_ref, b_ref, o_ref):
    o_ref[...] = _dot_hi(_silu(cond_ref[...]), w_ref[...]) + b_ref[...]


def _modulation(cond, w_mod, b_mod):
    tn = 1536
    return pl.pallas_call(
        _mod_kernel,
        grid=(DEPTH, 6 * D_MODEL // tn),
        in_specs=[
            pl.BlockSpec((N_COND, D_MODEL), lambda l, j: (0, 0)),
            pl.BlockSpec((None, D_MODEL, tn), lambda l, j: (l, 0, j)),
            pl.BlockSpec((None, 1, tn), lambda l, j: (l, 0, j)),
        ],
        out_specs=pl.BlockSpec((None, N_COND, tn), lambda l, j: (l, 0, j)),
        out_shape=jax.ShapeDtypeStruct((DEPTH, N_COND, 6 * D_MODEL), F32),
        compiler_params=_cparams(("arbitrary", "arbitrary")),
        name="modulation",
    )(cond, w_mod, b_mod.reshape(DEPTH, 1, 6 * D_MODEL))


def _inproj_even_kernel(x_ref, gam_ref, sh_ref, sc_ref, w_ref, wg_ref, g2_ref, bg_ref, cos_ref, sin_ref,
                        a_ref, glog_ref, qb_ref, kb_ref, vb_ref, kv_ref):
    is_lat = pl.program_id(0) >= NCB
    hb = _norm_mod(x_ref[...], gam_ref[...], sc_ref[...], sh_ref[...]).astype(BF16)
    z = _dot(hb, w_ref[...])
    na = 2 * A_QK + 2 * A_V
    a_ref[...] = z[:, :na]
    gates = _dot(hb, wg_ref[...])
    xg = _dot(gates.astype(BF16), g2_ref[...]) + bg_ref[...]
    glog_ref[...] = _log_sigmoid(xg) * (1.0 / GATE_TAU)
    cos = jnp.where(is_lat, cos_ref[...], 1.0)
    sin = jnp.where(is_lat, sin_ref[...], 0.0)
    kv_ref[...] = z[:, na + B_QK:]
    for j in range(B_QK // LANES):
        lo = na + j * LANES
        qb_ref[:, j * LANES:(j + 1) * LANES] = _rope(z[:, lo:lo + LANES], cos, sin, DH_B // 4).astype(BF16)
        lo = na + B_QK + j * LANES
        kb_ref[:, j * LANES:(j + 1) * LANES] = _rope(z[:, lo:lo + LANES], cos, sin, DH_B // 4).astype(BF16)
    vb_ref[...] = z[:, na + 2 * B_QK:].astype(BF16)


def _inproj_even(x, gamma, mod, w_main, w_gates, g2, bg, cos, sin):
    na = 2 * A_QK + 2 * A_V
    nz = na + 2 * B_QK + B_V
    return pl.pallas_call(
        _inproj_even_kernel,
        grid=(NB,),
        in_specs=[
            _row_spec(D_MODEL), _const_spec((1, D_MODEL)), _mod_spec(0), _mod_spec(1),
            _const_spec((D_MODEL, nz)), _const_spec((D_MODEL, LANES)), _const_spec((LANES, 2 * A_QK)),
            _const_spec((1, 2 * A_QK)),
            pl.BlockSpec((TM, LANES), lambda i: (_pos_block(i), 0)),
            pl.BlockSpec((TM, LANES), lambda i: (_pos_block(i), 0)),
        ],
        out_specs=[_row_spec(na), _row_spec(2 * A_QK), _row_spec(B_QK), _row_spec(B_QK), _row_spec(B_V),
                   _row_spec(B_QK + B_V)],
        out_shape=[
            jax.ShapeDtypeStruct((M_TOK, na), F32), jax.ShapeDtypeStruct((M_TOK, 2 * A_QK), F32),
            jax.ShapeDtypeStruct((M_TOK, B_QK), BF16), jax.ShapeDtypeStruct((M_TOK, B_QK), BF16),
            jax.ShapeDtypeStruct((M_TOK, B_V), BF16), jax.ShapeDtypeStruct((M_TOK, B_QK + B_V), F32),
        ],
        compiler_params=_cparams(("arbitrary",)),
        name="inproj_even",
    )(x, gamma, mod, mod, w_main, w_gates, g2, bg, cos, sin)


def _gla_chunk(q_ref, k_ref, v_ref, g_ref, o_ref, row0, s_ref, a_ref, b_ref, m1_ref, head_mask, rev):
    nsb = GLA_C // GLA_SB
    rows = pl.ds(row0, GLA_C)
    q, k, v, g = q_ref[rows, :], k_ref[rows, :], v_ref[rows, :], g_ref[rows, :]
    v_b = v.astype(BF16)
    r_i = lax.broadcasted_iota(jnp.int32, (GLA_C, GLA_C), 0)
    c_i = lax.broadcasted_iota(jnp.int32, (GLA_C, GLA_C), 1)
    tri = jnp.where((c_i >= r_i) if rev else (c_i <= r_i), 1.0, 0.0).astype(BF16)
    g_hi = g.astype(BF16)
    g_r = g - g_hi.astype(F32)
    g_mid = g_r.astype(BF16)
    g_lo = (g_r - g_mid.astype(F32)).astype(BF16)
    bc = _dot(tri, g_hi) + (_dot(tri, g_mid) + _dot(tri, g_lo))
    b_ref[...] = bc
    ends = [bc[GLA_SB * j:GLA_SB * j + 1] if rev else bc[GLA_SB * j + GLA_SB - 1:GLA_SB * (j + 1)]
            for j in range(nsb)]
    last = ends[0] if rev else ends[nsb - 1]
    e_rows = jnp.concatenate([jnp.broadcast_to(e, (GLA_SB, A_QK)) for e in ends], axis=0)
    khat = k * jnp.exp(e_rows - bc)
    qc = q * jnp.exp(bc)
    kl = k * jnp.exp(last - bc)
    row = lax.broadcasted_iota(jnp.int32, (GLA_C, A_QK), 0)

    def expand(x):
        return jnp.where(head_mask, jnp.concatenate([x] * H_A, axis=0), 0.0).astype(BF16)

    scores = None
    for j in (range(1, nsb) if rev else range(nsb - 1)):
        q_rows = (row < GLA_SB * j) if rev else (row >= GLA_SB * (j + 1))
        qj = jnp.where(q_rows, q * jnp.exp(bc - ends[j]), 0.0)
        kj = jnp.where((row // GLA_SB) == j, khat, 0.0).astype(BF16)
        sj = _dot_nt(expand(qj), kj)
        scores = sj if scores is None else scores + sj
    s_old = s_ref[...]
    inter = _dot(expand(qc), s_old.astype(BF16))
    o_heads = []
    for h in range(H_A):
        hs = slice(h * GLA_C, (h + 1) * GLA_C)
        o_heads.append(inter[hs] + _dot(scores[hs].astype(BF16), v_b[:, h * DV_A:(h + 1) * DV_A]))
    o = jnp.concatenate(o_heads, axis=1)

    t_i = lax.broadcasted_iota(jnp.int32, (GLA_SB, A_QK), 0)
    diag = []
    for i in range(nsb):
        q_i = q_ref[pl.ds(row0 + GLA_SB * i, GLA_SB), :]
        b_i = b_ref[GLA_SB * i:GLA_SB * (i + 1), :]
        base = i * GLA_SB * GLA_SB
        for j in range(GLA_SB):
            s = GLA_SB * i + j
            k_s = k_ref[pl.ds(row0 + s, 1), :]
            b_s = b_ref[s:s + 1, :]
            keep = (t_i <= j) if rev else (t_i >= j)
            a = jnp.where(keep, q_i * k_s * jnp.exp(b_i - b_s), 0.0)
            a_ref[base + j * GLA_SB:base + (j + 1) * GLA_SB, :] = a.astype(BF16)
        w = _dot(a_ref[base:base + GLA_SB * GLA_SB, :], m1_ref[...])
        acc = None
        for j in range(GLA_SB):
            term = w[j * GLA_SB:(j + 1) * GLA_SB] * v_ref[pl.ds(row0 + GLA_SB * i + j, 1), :]
            acc = term if acc is None else acc + term
        diag.append(acc)
    o_ref[rows, :] = o + jnp.concatenate(diag, axis=0)

    t = jnp.transpose(jnp.concatenate([kl, jnp.broadcast_to(last, (GLA_C, A_QK))], axis=0))
    kv = _dot(t[:, :GLA_C].astype(BF16), v_b)
    a_col = jnp.exp(t[:, GLA_C:GLA_C + 1])
    for h in range(H_A):
        hs = slice(h * DK_A, (h + 1) * DK_A)
        s_ref[hs, :] = a_col[hs] * s_old[hs] + kv[hs, h * DV_A:(h + 1) * DV_A]


def _gla_kernel(fblk_ref, bblk_ref, first_ref, init_ref, seq_ref,
                qf_ref, kf_ref, vf_ref, gf_ref, qr_ref, kr_ref, vr_ref, gr_ref, s0_ref, m1_ref,
                of_ref, ob_ref, sfin_ref, sf_ref, sb_ref, af_ref, ab_ref, bf_ref, bb_ref):
    step = pl.program_id(0)

    @pl.when(first_ref[step] == 1)
    def _():
        sf_ref[...] = s0_ref[0]
        sb_ref[...] = s0_ref[1]

    r_h = lax.broadcasted_iota(jnp.int32, (H_A * GLA_C, A_QK), 0) // GLA_C
    c_h = lax.broadcasted_iota(jnp.int32, (H_A * GLA_C, A_QK), 1) // DK_A
    head_mask = r_h == c_h
    nch = TM // GLA_C

    def body(c, carry):
        _gla_chunk(qf_ref, kf_ref, vf_ref, gf_ref, of_ref, pl.multiple_of(c * GLA_C, GLA_C),
                   sf_ref, af_ref, bf_ref, m1_ref, head_mask, False)
        _gla_chunk(qr_ref, kr_ref, vr_ref, gr_ref, ob_ref, pl.multiple_of((nch - 1 - c) * GLA_C, GLA_C),
                   sb_ref, ab_ref, bb_ref, m1_ref, head_mask, True)
        return carry

    lax.fori_loop(0, nch, body, 0)
    sfin_ref[0] = sf_ref[...]
    sfin_ref[1] = sb_ref[...]


def _gla_tables():
    fblk, bblk, first, init, seq = [], [], [], [], []
    for b in range(BATCH):
        fblk.append(b), bblk.append(b), first.append(1), init.append(0), seq.append(b)
    for s in range(DEC_BATCH):
        for j in range(LBB):
            fblk.append(NCB + s * LBB + j)
            bblk.append(NCB + s * LBB + LBB - 1 - j)
            first.append(1 if j == 0 else 0)
            init.append(1 + s)
            seq.append(BATCH + s)
    return [jnp.asarray(np.array(t, np.int32)) for t in (fblk, bblk, first, init, seq)]


def _gla(a_proj, glog, s0_all, m1):
    tables = _gla_tables()
    nsteps = int(tables[0].shape[0])
    nseq = BATCH + DEC_BATCH

    def fmap(col):
        return lambda i, fb, bb, fi, ini, sq: (fb[i], col)

    def rmap(col):
        return lambda i, fb, bb, fi, ini, sq: (bb[i], col)

    gs = pltpu.PrefetchScalarGridSpec(
        num_scalar_prefetch=5,
        grid=(nsteps,),
        in_specs=[
            pl.BlockSpec((TM, A_QK), fmap(0)), pl.BlockSpec((TM, A_QK), fmap(1)),
            pl.BlockSpec((TM, A_V), fmap(1)), pl.BlockSpec((TM, A_QK), fmap(0)),
            pl.BlockSpec((TM, A_QK), rmap(0)), pl.BlockSpec((TM, A_QK), rmap(1)),
            pl.BlockSpec((TM, A_V), rmap(1)), pl.BlockSpec((TM, A_QK), rmap(1)),
            pl.BlockSpec((None, 2, A_QK, DV_A), lambda i, fb, bb, fi, ini, sq: (ini[i], 0, 0, 0)),
            pl.BlockSpec((A_QK, A_V), lambda i, fb, bb, fi, ini, sq: (0, 0)),
        ],
        out_specs=[
            pl.BlockSpec((TM, A_V), fmap(0)), pl.BlockSpec((TM, A_V), rmap(0)),
            pl.BlockSpec((None, 2, A_QK, DV_A), lambda i, fb, bb, fi, ini, sq: (sq[i], 0, 0, 0)),
        ],
        scratch_shapes=[
            pltpu.VMEM((A_QK, DV_A), F32), pltpu.VMEM((A_QK, DV_A), F32),
            pltpu.VMEM((GLA_C * GLA_SB, A_QK), BF16), pltpu.VMEM((GLA_C * GLA_SB, A_QK), BF16),
            pltpu.VMEM((GLA_C, A_QK), F32), pltpu.VMEM((GLA_C, A_QK), F32),
        ],
    )
    return pl.pallas_call(
        _gla_kernel,
        grid_spec=gs,
        out_shape=[
            jax.ShapeDtypeStruct((M_TOK, A_V), F32), jax.ShapeDtypeStruct((M_TOK, A_V), F32),
            jax.ShapeDtypeStruct((nseq, 2, A_QK, DV_A), F32),
        ],
        compiler_params=_cparams(("arbitrary",)),
        name="gla",
    )(*tables, a_proj, a_proj, a_proj, glog, a_proj, a_proj, a_proj, glog, s0_all, m1)


def _softmax_parts(s):
    e = jnp.exp(s - jnp.max(s, axis=-1, keepdims=True))
    return e, 1.0 / jnp.sum(e, axis=-1, keepdims=True)


def _diffattn_body(q_ref, k_ref, v_ref, lq1_ref, lk1_ref, lq2_ref, lk2_ref, gsub_ref, o_ref, lam_init):
    lam = (jnp.exp(jnp.sum(lq1_ref[...] * lk1_ref[...], axis=-1, keepdims=True))
           - jnp.exp(jnp.sum(lq2_ref[...] * lk2_ref[...], axis=-1, keepdims=True)) + lam_init)
    lane = lax.broadcasted_iota(jnp.int32, (q_ref.shape[0], LANES), 1)
    for h in range(H_B):
        hs = slice(h * LANES, (h + 1) * LANES)
        q_h = q_ref[:, hs]
        k_h = k_ref[:, hs]
        zero = jnp.zeros_like(q_h)
        e1, r1 = _softmax_parts(_dot_nt(jnp.where(lane < DH_B, q_h, zero), k_h))
        e2, r2 = _softmax_parts(_dot_nt(jnp.where(lane >= DH_B, q_h, zero), k_h))
        w = e1 * r1 - e2 * (lam * r2)
        o = _dot(w.astype(BF16), v_ref[:, hs])
        o_ref[:, hs] = (_rms(o, gsub_ref[...]) * (1.0 - lam_init)).astype(BF16)


def _diffattn_ctx_kernel(q_ref, k_ref, v_ref, lq1, lk1, lq2, lk2, gsub, o_ref, *, lam_init):
    _diffattn_body(q_ref, k_ref, v_ref, lq1, lk1, lq2, lk2, gsub, o_ref, lam_init)


def _diffattn_lat_kernel(q_ref, k_ref, v_ref, lq1, lk1, lq2, lk2, gsub, prev_ref, o_ref, *, lam_init):
    del prev_ref
    _diffattn_body(q_ref, k_ref, v_ref, lq1, lk1, lq2, lk2, gsub, o_ref, lam_init)


def _diffattn(qb, kb, vb, k_lat, v_lat, lam_params, gsub, lam_init):
    small = [pl.BlockSpec((1, DH_B), lambda *_: (0, 0))] * 4 + [pl.BlockSpec((1, DV_B), lambda *_: (0, 0))]
    out_sds = jax.ShapeDtypeStruct((M_TOK, B_V), BF16)
    ctx = pl.pallas_call(
        functools.partial(_diffattn_ctx_kernel, lam_init=lam_init),
        grid=(NCB,),
        in_specs=[_row_spec(B_QK), _row_spec(B_QK), _row_spec(B_V)] + small,
        out_specs=_row_spec(B_V),
        out_shape=out_sds,
        compiler_params=_cparams(("arbitrary",)),
        name="diffattn_ctx",
    )(qb, kb, vb, *lam_params, gsub)
    nk = DEC_SEQ + PAST_LEN
    nqb = DEC_SEQ // TQ_LAT
    off = N_CTX // TQ_LAT
    return pl.pallas_call(
        functools.partial(_diffattn_lat_kernel, lam_init=lam_init),
        grid=(DEC_BATCH, nqb),
        in_specs=[
            pl.BlockSpec((TQ_LAT, B_QK), lambda b, j: (off + b * nqb + j, 0)),
            pl.BlockSpec((None, nk, B_QK), lambda b, j: (b, 0, 0)),
            pl.BlockSpec((None, nk, B_V), lambda b, j: (b, 0, 0)),
        ] + small + [pl.BlockSpec(memory_space=pl.ANY)],
        out_specs=pl.BlockSpec((TQ_LAT, B_V), lambda b, j: (off + b * nqb + j, 0)),
        out_shape=out_sds,
        input_output_aliases={8: 0},
        compiler_params=_cparams(("arbitrary", "arbitrary")),
        name="diffattn_lat",
    )(qb, k_lat, v_lat, *lam_params, gsub, ctx)


def _postmix_even_kernel(of_ref, ob_ref, ra_ref, attn_ref, x_ref, g1_ref, ggla_ref, wo_ref, o_ref):
    heads = []
    for h in range(H_A):
        hs = slice(h * DV_A, (h + 1) * DV_A)
        heads.append((_rms(of_ref[:, hs] + ob_ref[:, hs], ggla_ref[...]) * _silu(ra_ref[:, hs])).astype(BF16))
    mix = jnp.concatenate(heads + [attn_ref[...]], axis=1)
    o_ref[...] = x_ref[...] + g1_ref[...] * _dot(mix, wo_ref[...])


def _postmix_even(o_f, o_b, a_proj, attn, x, mod, g_gla, w_out):
    return pl.pallas_call(
        _postmix_even_kernel,
        grid=(NB,),
        in_specs=[_row_spec(A_V), _row_spec(A_V), _row_spec(A_V, 2), _row_spec(B_V), _row_spec(D_MODEL),
                  _mod_spec(2), _const_spec((1, DV_A)), _const_spec((A_V + B_V, D_MODEL))],
        out_specs=_row_spec(D_MODEL),
        out_shape=jax.ShapeDtypeStruct((M_TOK, D_MODEL), F32),
        compiler_params=_cparams(("arbitrary",)),
        name="postmix_even",
    )(o_f, o_b, a_proj, attn, x, mod, g_gla, w_out)


def _ffn_even_kernel(x_ref, gam_ref, sh_ref, sc_ref, g2_ref, wg_ref, wu_ref, wd_ref, o_ref):
    x = x_ref[...]
    hb = _norm_mod(x, gam_ref[...], sc_ref[...], sh_ref[...]).astype(BF16)
    acc = None
    nsub = 2
    w = D_FF // nsub
    for c in range(nsub):
        cs = slice(c * w, (c + 1) * w)
        act = (_silu(_dot(hb, wg_ref[:, cs])) * _dot(hb, wu_ref[:, cs])).astype(BF16)
        y = _dot(act, wd_ref[cs, :])
        acc = y if acc is None else acc + y
    o_ref[...] = x + g2_ref[...] * acc


def _ffn_even(x, gamma, mod, w_gate, w_up, w_down):
    return pl.pallas_call(
        _ffn_even_kernel,
        grid=(NB,),
        in_specs=[_row_spec(D_MODEL), _const_spec((1, D_MODEL)), _mod_spec(3), _mod_spec(4), _mod_spec(5),
                  _const_spec((D_MODEL, D_FF)), _const_spec((D_MODEL, D_FF)), _const_spec((D_FF, D_MODEL))],
        out_specs=_row_spec(D_MODEL),
        out_shape=jax.ShapeDtypeStruct((M_TOK, D_MODEL), F32),
        compiler_params=_cparams(("arbitrary",)),
        name="ffn_even",
    )(x, gamma, mod, mod, mod, w_gate, w_up, w_down)


def _inproj_odd_kernel(x_ref, gam_ref, sh_ref, sc_ref, w_ref, gq_ref, gk_ref, cos_ref, sin_ref,
                       q_ref, k_ref, v_ref, kv_ref):
    is_lat = pl.program_id(0) >= NCB
    hb = _norm_mod(x_ref[...], gam_ref[...], sc_ref[...], sh_ref[...]).astype(BF16)
    z = _dot(hb, w_ref[...])
    cos = jnp.where(is_lat, cos_ref[...], 1.0)
    sin = jnp.where(is_lat, sin_ref[...], 0.0)
    scale = DH_C ** -0.5
    for h in range(H_C):
        hs = slice(h * DH_C, (h + 1) * DH_C)
        q_ref[:, hs] = (_rope(_rms(z[:, hs], gq_ref[...]), cos, sin, DH_C // 4) * scale).astype(BF16)
    nq = H_C * DH_C
    for h in range(HKV_C):
        hs = slice(h * DH_C, (h + 1) * DH_C)
        k_n = _rms(z[:, nq + h * DH_C:nq + (h + 1) * DH_C], gk_ref[...])
        kv_ref[:, hs] = k_n
        k_ref[:, hs] = _rope(k_n, cos, sin, DH_C // 4).astype(BF16)
    nkv = HKV_C * DH_C
    v = z[:, nq + nkv:]
    kv_ref[:, nkv:] = v
    v_ref[...] = v.astype(BF16)


def _inproj_odd(x, gamma, mod, w_in, g_q, g_k, cos, sin):
    nq, nkv = H_C * DH_C, HKV_C * DH_C
    return pl.pallas_call(
        _inproj_odd_kernel,
        grid=(NB,),
        in_specs=[
            _row_spec(D_MODEL), _const_spec((1, D_MODEL)), _mod_spec(0), _mod_spec(1),
            _const_spec((D_MODEL, nq + 2 * nkv)), _const_spec((1, DH_C)), _const_spec((1, DH_C)),
            pl.BlockSpec((TM, LANES), lambda i: (_pos_block(i), 0)),
            pl.BlockSpec((TM, LANES), lambda i: (_pos_block(i), 0)),
        ],
        out_specs=[_row_spec(nq), _row_spec(nkv), _row_spec(nkv), _row_spec(2 * nkv)],
        out_shape=[
            jax.ShapeDtypeStruct((M_TOK, nq), BF16), jax.ShapeDtypeStruct((M_TOK, nkv), BF16),
            jax.ShapeDtypeStruct((M_TOK, nkv), BF16), jax.ShapeDtypeStruct((M_TOK, 2 * nkv), F32),
        ],
        compiler_params=_cparams(("arbitrary",)),
        name="inproj_odd",
    )(x, gamma, mod, mod, w_in, g_q, g_k, cos, sin)


def _gqa_body(q_ref, k_ref, v_ref, o_ref):
    rep = H_C // HKV_C
    tq = q_ref.shape[0]
    for hk in range(HKV_C):
        ks = slice(hk * DH_C, (hk + 1) * DH_C)
        q_g = jnp.concatenate([q_ref[:, (hk * rep + g) * DH_C:(hk * rep + g + 1) * DH_C] for g in range(rep)],
                              axis=0)
        e, r = _softmax_parts(_dot_nt(q_g, k_ref[:, ks]))
        o = _dot((e * r).astype(BF16), v_ref[:, ks])
        for g in range(rep):
            o_ref[:, (hk * rep + g) * DH_C:(hk * rep + g + 1) * DH_C] = o[g * tq:(g + 1) * tq].astype(BF16)


def _gqa_ctx_kernel(q_ref, k_ref, v_ref, o_ref):
    _gqa_body(q_ref, k_ref, v_ref, o_ref)


def _gqa_lat_kernel(q_ref, k_ref, v_ref, prev_ref, o_ref):
    del prev_ref
    _gqa_body(q_ref, k_ref, v_ref, o_ref)


def _gqa(q, k, v, k_lat, v_lat):
    nq, nkv = H_C * DH_C, HKV_C * DH_C
    out_sds = jax.ShapeDtypeStruct((M_TOK, nq), BF16)
    ctx = pl.pallas_call(
        _gqa_ctx_kernel,
        grid=(NCB,),
        in_specs=[_row_spec(nq), _row_spec(nkv), _row_spec(nkv)],
        out_specs=_row_spec(nq),
        out_shape=out_sds,
        compiler_params=_cparams(("arbitrary",)),
        name="gqa_ctx",
    )(q, k, v)
    nk = DEC_SEQ + PAST_LEN
    nqb = DEC_SEQ // TQ_LAT
    off = N_CTX // TQ_LAT
    return pl.pallas_call(
        _gqa_lat_kernel,
        grid=(DEC_BATCH, nqb),
        in_specs=[
            pl.BlockSpec((TQ_LAT, nq), lambda b, j: (off + b * nqb + j, 0)),
            pl.BlockSpec((None, nk, nkv), lambda b, j: (b, 0, 0)),
            pl.BlockSpec((None, nk, nkv), lambda b, j: (b, 0, 0)),
            pl.BlockSpec(memory_space=pl.ANY),
        ],
        out_specs=pl.BlockSpec((TQ_LAT, nq), lambda b, j: (off + b * nqb + j, 0)),
        out_shape=out_sds,
        input_output_aliases={3: 0},
        compiler_params=_cparams(("arbitrary", "arbitrary")),
        name="gqa_lat",
    )(q, k_lat, v_lat, ctx)


def _postmix_odd_kernel(attn_ref, x_ref, g1_ref, wo_ref, gam_ref, sh_ref, sc_ref, wr_ref,
                        x1_ref, h_ref, w1_ref, w2_ref, idx_ref):
    x1 = x_ref[...] + g1_ref[...] * _dot(attn_ref[...], wo_ref[...])
    x1_ref[...] = x1
    h = _norm_mod(x1, gam_ref[...], sc_ref[...], sh_ref[...])
    h_ref[...] = h.astype(BF16)
    logits = _dot_hi(h, wr_ref[...])
    lane = lax.broadcasted_iota(jnp.int32, logits.shape, 1)
    lane_f = lane.astype(F32)
    lg = jnp.where(lane < N_EXPERTS, logits, -jnp.inf)
    m1 = jnp.max(lg, axis=-1, keepdims=True)
    i1 = jnp.min(jnp.where(lg == m1, lane_f, float(LANES)), axis=-1, keepdims=True)
    lg2 = jnp.where(lane_f == i1, -jnp.inf, lg)
    m2 = jnp.max(lg2, axis=-1, keepdims=True)
    i2 = jnp.min(jnp.where(lg2 == m2, lane_f, float(LANES)), axis=-1, keepdims=True)
    e = jnp.exp(m2 - m1)
    w1 = 1.0 / (1.0 + e)
    w1_ref[...] = jnp.broadcast_to(w1, logits.shape)
    w2_ref[...] = jnp.broadcast_to(e * w1, logits.shape)
    idx_ref[...] = jnp.where(lane < LANES // 2, i1, i2).astype(jnp.int32)


def _postmix_odd(attn, x, mod, w_out, gamma, w_router):
    return pl.pallas_call(
        _postmix_odd_kernel,
        grid=(NB,),
        in_specs=[_row_spec(H_C * DH_C), _row_spec(D_MODEL), _mod_spec(2), _const_spec((H_C * DH_C, D_MODEL)),
                  _const_spec((1, D_MODEL)), _mod_spec(3), _mod_spec(4), _const_spec((D_MODEL, LANES))],
        out_specs=[_row_spec(D_MODEL), _row_spec(D_MODEL), _row_spec(LANES), _row_spec(LANES), _row_spec(LANES)],
        out_shape=[
            jax.ShapeDtypeStruct((M_TOK, D_MODEL), F32), jax.ShapeDtypeStruct((M_TOK, D_MODEL), BF16),
            jax.ShapeDtypeStruct((M_TOK, LANES), F32), jax.ShapeDtypeStruct((M_TOK, LANES), F32),
            jax.ShapeDtypeStruct((M_TOK, LANES), jnp.int32),
        ],
        compiler_params=_cparams(("arbitrary",)),
        name="postmix_odd",
    )(attn, x, mod, w_out, gamma, mod, mod, w_router)


def _experts_kernel(te_ref, tv_ref, x_ref, wg_ref, wu_ref, wd_ref, o_ref):
    i, f = pl.program_id(0), pl.program_id(1)

    @pl.when(tv_ref[i] == 1)
    def _():
        x = x_ref[...]
        acc = None
        nsub = 2
        w = TF_E // nsub
        for c in range(nsub):
            cs = slice(c * w, (c + 1) * w)
            act = (_silu(_dot(x, wg_ref[:, cs])) * _dot(x, wu_ref[:, cs])).astype(BF16)
            y = _dot(act, wd_ref[cs, :])
            acc = y if acc is None else acc + y

        @pl.when(f == 0)
        def _():
            o_ref[...] = acc

        @pl.when(f > 0)
        def _():
            o_ref[...] += acc

    @pl.when(jnp.logical_and(tv_ref[i] == 0, f == 0))
    def _():
        o_ref[...] = jnp.zeros_like(o_ref)


def _experts(tile_expert, tile_valid, xs, w_gate, w_up, w_down):
    gs = pltpu.PrefetchScalarGridSpec(
        num_scalar_prefetch=2,
        grid=(NT_E, D_FF_E // TF_E),
        in_specs=[
            pl.BlockSpec((TME, D_MODEL), lambda i, f, te, tv: (i, 0)),
            pl.BlockSpec((None, D_MODEL, TF_E), lambda i, f, te, tv: (te[i], 0, f)),
            pl.BlockSpec((None, D_MODEL, TF_E), lambda i, f, te, tv: (te[i], 0, f)),
            pl.BlockSpec((None, TF_E, D_MODEL), lambda i, f, te, tv: (te[i], f, 0)),
        ],
        out_specs=pl.BlockSpec((TME, D_MODEL), lambda i, f, te, tv: (i, 0)),
    )
    return pl.pallas_call(
        _experts_kernel,
        grid_spec=gs,
        out_shape=jax.ShapeDtypeStruct((NT_E * TME, D_MODEL), F32),
        compiler_params=_cparams(("arbitrary", "arbitrary")),
        name="experts",
    )(tile_expert, tile_valid, xs, w_gate, w_up, w_down)


def _combine_kernel(x_ref, y1_ref, y2_ref, w1_ref, w2_ref, g2_ref, gam_ref, o_ref):
    rep = D_MODEL // LANES
    w1 = jnp.concatenate([w1_ref[...]] * rep, axis=1)
    w2 = jnp.concatenate([w2_ref[...]] * rep, axis=1)
    x2 = x_ref[...] + g2_ref[...] * (w1 * y1_ref[...] + w2 * y2_ref[...])
    o_ref[...] = _rms(x2, gam_ref[...])


def _combine(x1, y1, y2, w1, w2, mod, gamma):
    return pl.pallas_call(
        _combine_kernel,
        grid=(NB,),
        in_specs=[_row_spec(D_MODEL), _row_spec(D_MODEL), _row_spec(D_MODEL), _row_spec(LANES), _row_spec(LANES),
                  _mod_spec(5), _const_spec((1, D_MODEL))],
        out_specs=_row_spec(D_MODEL),
        out_shape=jax.ShapeDtypeStruct((M_TOK, D_MODEL), F32),
        compiler_params=_cparams(("arbitrary",)),
        name="combine",
    )(x1, y1, y2, w1, w2, mod, gamma)


def _rope_tables(rot_dim):
    n = DEC_SEQ
    rows = jnp.repeat(jnp.arange(n // GRID_W, dtype=F32), GRID_W)
    cols = jnp.tile(jnp.arange(GRID_W, dtype=F32), n // GRID_W)
    half = rot_dim // 2
    freqs = ROPE_THETA ** (-jnp.arange(0, half, 2, dtype=F32) / half)
    ang_r, ang_c = rows[:, None] * freqs, cols[:, None] * freqs
    cos = jnp.concatenate([jnp.cos(ang_r)] * 2 + [jnp.cos(ang_c)] * 2, axis=-1)
    sin = jnp.concatenate([-jnp.sin(ang_r), jnp.sin(ang_r), -jnp.sin(ang_c), jnp.sin(ang_c)], axis=-1)
    rep = LANES // rot_dim
    return jnp.tile(cos, (1, rep)), jnp.tile(sin, (1, rep))


def _route(idx):
    e_flat = jnp.stack([idx[:, 0], idx[:, LANES // 2]], axis=1).reshape(-1)
    n_assign = e_flat.shape[0]
    onehot = (e_flat[:, None] == jnp.arange(N_EXPERTS, dtype=jnp.int32)[None, :]).astype(jnp.int32)
    csum = jnp.cumsum(onehot, axis=0)
    rank = jnp.sum(csum * onehot, axis=1) - 1
    counts = csum[-1]
    tiles = (counts + TME - 1) // TME
    tile_end = jnp.cumsum(tiles)
    tile_start = tile_end - tiles
    pos = tile_start[e_flat] * TME + rank
    tile_id = jnp.arange(NT_E, dtype=jnp.int32)
    tile_expert = jnp.minimum(jnp.sum((tile_id[:, None] >= tile_end[None, :]).astype(jnp.int32), axis=1),
                              N_EXPERTS - 1).astype(jnp.int32)
    tile_valid = (tile_id < tile_end[-1]).astype(jnp.int32)
    order = jnp.argsort(e_flat, stable=True).astype(jnp.int32)
    start = jnp.cumsum(counts) - counts
    slot = jnp.arange(NT_E * TME, dtype=jnp.int32)
    slot_e = tile_expert[slot // TME]
    within = slot - tile_start[slot_e] * TME
    live = jnp.logical_and(within < counts[slot_e], tile_valid[slot // TME] == 1)
    src = order[jnp.clip(start[slot_e] + within, 0, n_assign - 1)] // 2
    src_tok = jnp.where(live, src, 0).astype(jnp.int32)
    return src_tok, pos.reshape(-1, 2), tile_expert, tile_valid


def lambda_init(layer):
    return 0.8 - 0.6 * math.exp(-0.3 * layer)


def kernel(x_prompt, x_sample, state_a, cache_b_k, cache_b_v, cache_c_k, cache_c_v, c, c_ctx, w_mod, b_mod, norm_mix, norm_ffn, w_in_even, w_gate2_a, b_gate_a, g_gla, lam_q1, lam_k1, lam_q2, lam_k2, g_sub_b, w_out_even, w_in_odd, g_q_c, g_k_c, w_out_odd, ffn_gate, ffn_up, ffn_down, w_router, exp_gate, exp_up, exp_down, norm_final):
    x = jnp.concatenate([x_prompt.reshape(N_CTX, D_MODEL), x_sample.reshape(N_LAT, D_MODEL)], axis=0)
    cond = jnp.concatenate([c_ctx[None, :], c, jnp.zeros((N_COND - 1 - DEC_BATCH, D_MODEL), F32)], axis=0)
    mod = _modulation(cond, w_mod, b_mod).reshape(DEPTH, N_COND, 1, 6 * D_MODEL)

    w = w_in_even[0]
    na = 2 * A_QK + 2 * A_V
    gate_lo = na
    gate_hi = na + 2 * GATE_RANK
    col_scale = jnp.concatenate([jnp.full((A_QK,), DK_A ** -0.5, F32), jnp.ones((na - A_QK,), F32),
                                 jnp.full((B_QK,), DH_B ** -0.5, F32), jnp.ones((B_QK + B_V,), F32)])
    w_main = (jnp.concatenate([w[:, :gate_lo], w[:, gate_hi:]], axis=1) * col_scale).astype(BF16)
    w_gates = jnp.pad(w[:, gate_lo:gate_hi], ((0, 0), (0, LANES - 2 * GATE_RANK))).astype(BF16)
    g2 = jnp.zeros((LANES, 2 * A_QK), F32)
    g2 = g2.at[:GATE_RANK, :A_QK].set(w_gate2_a[0, 0]).at[GATE_RANK:2 * GATE_RANK, A_QK:].set(w_gate2_a[0, 1])
    bg = b_gate_a[0].reshape(1, 2 * A_QK)
    cos_b, sin_b = _rope_tables(DH_B)
    a_proj, glog, qb, kb, vb, kvb = _inproj_even(
        x, norm_mix[0][None, :], mod[0], w_main, w_gates, g2.astype(BF16), bg, cos_b, sin_b)

    s0_all = jnp.concatenate([jnp.zeros((1, 2, A_QK, DV_A), F32),
                              state_a[:, 0].reshape(DEC_BATCH, 2, A_QK, DV_A)], axis=0)
    hd = np.arange(A_QK)[:, None] // DK_A == np.arange(A_V)[None, :] // DV_A
    m1 = jnp.asarray(hd.astype(np.float32)).astype(BF16)
    o_f, o_b, s_fin = _gla(a_proj, glog, s0_all, m1)

    k_lat = jnp.concatenate([kb[N_CTX:].reshape(DEC_BATCH, DEC_SEQ, B_QK),
                             cache_b_k[:, 0].reshape(DEC_BATCH, PAST_LEN, B_QK).astype(BF16)], axis=1)
    v_lat = jnp.concatenate([vb[N_CTX:].reshape(DEC_BATCH, DEC_SEQ, B_V),
                             cache_b_v[:, 0].reshape(DEC_BATCH, PAST_LEN, B_V).astype(BF16)], axis=1)
    lam_params = [p[0][None, :] for p in (lam_q1, lam_k1, lam_q2, lam_k2)]
    attn_b = _diffattn(qb, kb, vb, k_lat, v_lat, lam_params, g_sub_b[0][None, :], lambda_init(0))

    x = _postmix_even(o_f, o_b, a_proj, attn_b, x, mod[0], g_gla[0][None, :], w_out_even[0].astype(BF16))
    x = _ffn_even(x, norm_ffn[0][None, :], mod[0], ffn_gate[0].astype(BF16), ffn_up[0].astype(BF16),
                  ffn_down[0].astype(BF16))

    cos_c, sin_c = _rope_tables(DH_C)
    q_c, k_c, v_c, kvc = _inproj_odd(x, norm_mix[1][None, :], mod[1], w_in_odd[0].astype(BF16),
                                     g_q_c[0][None, :], g_k_c[0][None, :], cos_c, sin_c)
    nkv = HKV_C * DH_C
    k_lat = jnp.concatenate([k_c[N_CTX:].reshape(DEC_BATCH, DEC_SEQ, nkv),
                             cache_c_k[:, 0].reshape(DEC_BATCH, PAST_LEN, nkv).astype(BF16)], axis=1)
    v_lat = jnp.concatenate([v_c[N_CTX:].reshape(DEC_BATCH, DEC_SEQ, nkv),
                             cache_c_v[:, 0].reshape(DEC_BATCH, PAST_LEN, nkv).astype(BF16)], axis=1)
    attn_c = _gqa(q_c, k_c, v_c, k_lat, v_lat)
    w_r = jnp.pad(w_router[0], ((0, 0), (0, LANES - N_EXPERTS)))
    x1, h_moe, w1, w2, idx = _postmix_odd(attn_c, x, mod[1], w_out_odd[0].astype(BF16), norm_ffn[1][None, :], w_r)

    src_tok, pos, tile_expert, tile_valid = _route(idx)
    xs = jnp.take(h_moe, src_tok, axis=0)
    ys = _experts(tile_expert, tile_valid, xs, exp_gate[0].astype(BF16), exp_up[0].astype(BF16),
                  exp_down[0].astype(BF16))
    y1 = jnp.take(ys, pos[:, 0], axis=0)
    y2 = jnp.take(ys, pos[:, 1], axis=0)
    y = _combine(x1, y1, y2, w1, w2, mod[1], norm_final[None, :])

    y_prompt = y[:N_CTX].reshape(BATCH, SEQ, D_MODEL)
    y_sample = y[N_CTX:].reshape(DEC_BATCH, DEC_SEQ, D_MODEL)
    new_state_a = s_fin[:BATCH].reshape(BATCH, 1, 2, H_A, DK_A, DV_A)
    new_b_k = kvb[:N_CTX, :B_QK].reshape(BATCH, 1, SEQ, H_B, 2, DH_B)
    new_b_v = kvb[:N_CTX, B_QK:].reshape(BATCH, 1, SEQ, H_B, DV_B)
    new_c_k = kvc[:N_CTX, :nkv].reshape(BATCH, 1, SEQ, HKV_C, DH_C)
    new_c_v = kvc[:N_CTX, nkv:].reshape(BATCH, 1, SEQ, HKV_C, DH_C)
    return (y_prompt, y_sample, new_state_a, new_b_k, new_b_v, new_c_k, new_c_v)
```

```python
import functools
import math

import numpy as np
import jax
import jax.numpy as jnp
from jax import lax
from jax.experimental import pallas as pl
from jax.experimental.pallas import tpu as pltpu

D_MODEL = 1024
BATCH = 32
SEQ = 256
DEPTH = 2
DEC_BATCH = 2
DEC_SEQ = 2048
PAST_LEN = 512
GRID_W = 64
H_A, DK_A, DV_A = 4, 64, 128
GATE_RANK = 16
GATE_TAU = 16.0
H_B, DH_B = 4, 64
DV_B = 2 * DH_B
H_C, HKV_C, DH_C = 8, 2, 128
D_FF = 2816
N_EXPERTS = 8
D_FF_E = 3584
ROPE_THETA = 10000.0
EPS = 1e-6
A_QK = H_A * DK_A
A_V = H_A * DV_A
B_QK = H_B * 2 * DH_B
B_V = H_B * DV_B

F32 = jnp.float32
BF16 = jnp.bfloat16

V7X_VMEM_BYTES = 64 * 1024 * 1024
VMEM_LIMIT = V7X_VMEM_BYTES * 7 // 8
LANES = 128
TM = 256
N_CTX = BATCH * SEQ
N_LAT = DEC_BATCH * DEC_SEQ
M_TOK = N_CTX + N_LAT
NB = M_TOK // TM
NCB = N_CTX // TM
LBB = DEC_SEQ // TM
N_COND = 8
GLA_C = 64
GLA_SB = 16
TME = 512
NT_E = 2 * M_TOK // TME + N_EXPERTS
TF_E = D_FF_E // 2
TQ_LAT = 256

assert SEQ == TM and DEC_SEQ % TM == 0 and DEPTH == 2


def _cparams(sem):
    return pltpu.CompilerParams(dimension_semantics=sem, vmem_limit_bytes=VMEM_LIMIT)


def _group(i):
    return jnp.where(i < NCB, 0, 1 + (i - NCB) // LBB)


def _pos_block(i):
    return jnp.where(i < NCB, 0, (i - NCB) % LBB)


def _mod_spec(chunk):
    return pl.BlockSpec((None, 1, D_MODEL), lambda i, c=chunk: (_group(i), 0, c))


def _row_spec(width, col=0):
    return pl.BlockSpec((TM, width), lambda i, c=col: (i, c))


def _ctx_spec(width):
    return pl.BlockSpec((TM, width), lambda i: (jnp.minimum(i, NCB - 1), 0))


def _lat_spec(width):
    return pl.BlockSpec((TM, width), lambda i: (jnp.maximum(i - NCB, 0), 0))


def _const_spec(shape):
    nd = len(shape)
    return pl.BlockSpec(shape, lambda i, nd=nd: (0,) * nd)


def _dot(a, b):
    return jnp.dot(a, b, preferred_element_type=F32)


def _dot_nt(a, b):
    return lax.dot_general(a, b, (((1,), (1,)), ((), ())), preferred_element_type=F32)


def _split2(a):
    hi = a.astype(BF16)
    lo = (a - hi.astype(F32)).astype(BF16)
    return hi, lo


def _dot_hi(a, b):
    a_hi, a_lo = _split2(a)
    b_hi, b_lo = _split2(b)
    return _dot(a_hi, b_hi) + (_dot(a_hi, b_lo) + _dot(a_lo, b_hi))


def _silu(x):
    return x * (1.0 / (1.0 + jnp.exp(-x)))


def _log_sigmoid(x):
    return jnp.minimum(x, 0.0) - jnp.log(1.0 + jnp.exp(-jnp.abs(x)))


def _rms(x, g):
    return x * lax.rsqrt(jnp.mean(x * x, axis=-1, keepdims=True) + EPS) * g


def _norm_mod(x, gamma, sc, sh):
    return _rms(x, gamma) * (1.0 + sc) + sh


def _rope(x, cos, sin, half):
    lane = lax.broadcasted_iota(jnp.int32, x.shape, 1)
    first = (lane % (2 * half)) < half
    swapped = jnp.where(first, pltpu.roll(x, LANES - half, 1), pltpu.roll(x, half, 1))
    return x * cos + swapped * sin


def _mod_kernel(cond_ref, w_ref, b_ref, o_ref):
    o_ref[...] = _dot_hi(_silu(cond_ref[...]), w_ref[...]) + b_ref[...]


def _modulation(cond, w_mod, b_mod):
    tn = 1536
    return pl.pallas_call(
        _mod_kernel,
        grid=(DEPTH, 6 * D_MODEL // tn),
        in_specs=[
            pl.BlockSpec((N_COND, D_MODEL), lambda l, j: (0, 0)),
            pl.BlockSpec((None, D_MODEL, tn), lambda l, j: (l, 0, j)),
            pl.BlockSpec((None, 1, tn), lambda l, j: (l, 0, j)),
        ],
        out_specs=pl.BlockSpec((None, N_COND, tn), lambda l, j: (l, 0, j)),
        out_shape=jax.ShapeDtypeStruct((DEPTH, N_COND, 6 * D_MODEL), F32),
        compiler_params=_cparams(("arbitrary", "arbitrary")),
        name="modulation",
    )(cond, w_mod, b_mod.reshape(DEPTH, 1, 6 * D_MODEL))


def _inproj_even_kernel(x_ref, gam_ref, sh_ref, sc_ref, w_ref, wg_ref, g2_ref, bg_ref, cos_ref, sin_ref,
                        a_ref, glog_ref, qb_ref, kb_ref, vb_ref, kctx_ref, vctx_ref):
    is_lat = pl.program_id(0) >= NCB
    hb = _norm_mod(x_ref[...], gam_ref[...], sc_ref[...], sh_ref[...]).astype(BF16)
    z = _dot(hb, w_ref[...])
    na = 2 * A_QK + 2 * A_V
    a_ref[...] = z[:, :na]
    gates = _dot(hb, wg_ref[...])
    xg = _dot(gates.astype(BF16), g2_ref[...]) + bg_ref[...]
    glog_ref[...] = _log_sigmoid(xg) * (1.0 / GATE_TAU)
    cos = jnp.where(is_lat, cos_ref[...], 1.0)
    sin = jnp.where(is_lat, sin_ref[...], 0.0)

    @pl.when(jnp.logical_not(is_lat))
    def _():
        kctx_ref[...] = z[:, na + B_QK:na + 2 * B_QK]
        vctx_ref[...] = z[:, na + 2 * B_QK:]

    for j in range(B_QK // LANES):
        lo = na + j * LANES
        qb_ref[:, j * LANES:(j + 1) * LANES] = _rope(z[:, lo:lo + LANES], cos, sin, DH_B // 4).astype(BF16)
        lo = na + B_QK + j * LANES
        kb_ref[:, j * LANES:(j + 1) * LANES] = _rope(z[:, lo:lo + LANES], cos, sin, DH_B // 4).astype(BF16)
    vb_ref[...] = z[:, na + 2 * B_QK:].astype(BF16)


def _inproj_even(x, gamma, mod, w_main, w_gates, g2, bg, cos, sin):
    na = 2 * A_QK + 2 * A_V
    nz = na + 2 * B_QK + B_V
    return pl.pallas_call(
        _inproj_even_kernel,
        grid=(NB,),
        in_specs=[
            _row_spec(D_MODEL), _const_spec((1, D_MODEL)), _mod_spec(0), _mod_spec(1),
            _const_spec((D_MODEL, nz)), _const_spec((D_MODEL, LANES)), _const_spec((LANES, 2 * A_QK)),
            _const_spec((1, 2 * A_QK)),
            pl.BlockSpec((TM, LANES), lambda i: (_pos_block(i), 0)),
            pl.BlockSpec((TM, LANES), lambda i: (_pos_block(i), 0)),
        ],
        out_specs=[_row_spec(na), _row_spec(2 * A_QK), _row_spec(B_QK), _row_spec(B_QK), _row_spec(B_V),
                   _ctx_spec(B_QK), _ctx_spec(B_V)],
        out_shape=[
            jax.ShapeDtypeStruct((M_TOK, na), F32), jax.ShapeDtypeStruct((M_TOK, 2 * A_QK), F32),
            jax.ShapeDtypeStruct((M_TOK, B_QK), BF16), jax.ShapeDtypeStruct((M_TOK, B_QK), BF16),
            jax.ShapeDtypeStruct((M_TOK, B_V), BF16), jax.ShapeDtypeStruct((N_CTX, B_QK), F32),
            jax.ShapeDtypeStruct((N_CTX, B_V), F32),
        ],
        compiler_params=_cparams(("arbitrary",)),
        name="inproj_even",
    )(x, gamma, mod, mod, w_main, w_gates, g2, bg, cos, sin)


def _gla_chunk(q_ref, k_ref, v_ref, g_ref, o_ref, row0, s_ref, a_ref, b_ref, m1_ref, head_mask, rev):
    nsb = GLA_C // GLA_SB
    rows = pl.ds(row0, GLA_C)
    q, k, v, g = q_ref[rows, :], k_ref[rows, :], v_ref[rows, :], g_ref[rows, :]
    v_b = v.astype(BF16)
    r_i = lax.broadcasted_iota(jnp.int32, (GLA_C, GLA_C), 0)
    c_i = lax.broadcasted_iota(jnp.int32, (GLA_C, GLA_C), 1)
    tri = jnp.where((c_i >= r_i) if rev else (c_i <= r_i), 1.0, 0.0).astype(BF16)
    g_hi = g.astype(BF16)
    g_r = g - g_hi.astype(F32)
    g_mid = g_r.astype(BF16)
    g_lo = (g_r - g_mid.astype(F32)).astype(BF16)
    bc = _dot(tri, g_hi) + (_dot(tri, g_mid) + _dot(tri, g_lo))
    b_ref[...] = bc
    ends = [bc[GLA_SB * j:GLA_SB * j + 1] if rev else bc[GLA_SB * j + GLA_SB - 1:GLA_SB * (j + 1)]
            for j in range(nsb)]
    last = ends[0] if rev else ends[nsb - 1]
    e_rows = jnp.concatenate([jnp.broadcast_to(e, (GLA_SB, A_QK)) for e in ends], axis=0)
    khat = k * jnp.exp(e_rows - bc)
    qc = q * jnp.exp(bc)
    kl = k * jnp.exp(last - bc)
    row = lax.broadcasted_iota(jnp.int32, (GLA_C, A_QK), 0)

    def expand(x):
        return jnp.where(head_mask, jnp.concatenate([x] * H_A, axis=0), 0.0).astype(BF16)

    scores = None
    for j in (range(1, nsb) if rev else range(nsb - 1)):
        q_rows = (row < GLA_SB * j) if rev else (row >= GLA_SB * (j + 1))
        qj = jnp.where(q_rows, q * jnp.exp(bc - ends[j]), 0.0)
        kj = jnp.where((row // GLA_SB) == j, khat, 0.0).astype(BF16)
        sj = _dot_nt(expand(qj), kj)
        scores = sj if scores is None else scores + sj
    s_old = s_ref[...]
    inter = _dot(expand(qc), s_old.astype(BF16))
    o_heads = []
    for h in range(H_A):
        hs = slice(h * GLA_C, (h + 1) * GLA_C)
        o_heads.append(inter[hs] + _dot(scores[hs].astype(BF16), v_b[:, h * DV_A:(h + 1) * DV_A]))
    o = jnp.concatenate(o_heads, axis=1)

    t_i = lax.broadcasted_iota(jnp.int32, (GLA_SB, A_QK), 0)
    diag = []
    for i in range(nsb):
        q_i = q_ref[pl.ds(row0 + GLA_SB * i, GLA_SB), :]
        b_i = b_ref[GLA_SB * i:GLA_SB * (i + 1), :]
        base = i * GLA_SB * GLA_SB
        for j in range(GLA_SB):
            s = GLA_SB * i + j
            k_s = k_ref[pl.ds(row0 + s, 1), :]
            b_s = b_ref[s:s + 1, :]
            keep = (t_i <= j) if rev else (t_i >= j)
            a = jnp.where(keep, q_i * k_s * jnp.exp(b_i - b_s), 0.0)
            a_ref[base + j * GLA_SB:base + (j + 1) * GLA_SB, :] = a.astype(BF16)
        w = _dot(a_ref[base:base + GLA_SB * GLA_SB, :], m1_ref[...])
        acc = None
        for j in range(GLA_SB):
            term = w[j * GLA_SB:(j + 1) * GLA_SB] * v_ref[pl.ds(row0 + GLA_SB * i + j, 1), :]
            acc = term if acc is None else acc + term
        diag.append(acc)
    o_ref[rows, :] = o + jnp.concatenate(diag, axis=0)

    t = jnp.transpose(jnp.concatenate([kl, jnp.broadcast_to(last, (GLA_C, A_QK))], axis=0))
    kv = _dot(t[:, :GLA_C].astype(BF16), v_b)
    a_col = jnp.exp(t[:, GLA_C:GLA_C + 1])
    for h in range(H_A):
        hs = slice(h * DK_A, (h + 1) * DK_A)
        s_ref[hs, :] = a_col[hs] * s_old[hs] + kv[hs, h * DV_A:(h + 1) * DV_A]


def _gla_kernel(fblk_ref, bblk_ref, first_ref, init_ref, seq_ref,
                qf_ref, kf_ref, vf_ref, gf_ref, qr_ref, kr_ref, vr_ref, gr_ref, s0_ref, m1_ref,
                of_ref, ob_ref, sfin_ref, sf_ref, sb_ref, af_ref, ab_ref, bf_ref, bb_ref):
    step = pl.program_id(0)

    @pl.when(first_ref[step] == 1)
    def _():
        sf_ref[...] = s0_ref[0]
        sb_ref[...] = s0_ref[1]

    r_h = lax.broadcasted_iota(jnp.int32, (H_A * GLA_C, A_QK), 0) // GLA_C
    c_h = lax.broadcasted_iota(jnp.int32, (H_A * GLA_C, A_QK), 1) // DK_A
    head_mask = r_h == c_h
    nch = TM // GLA_C

    def body(c, carry):
        _gla_chunk(qf_ref, kf_ref, vf_ref, gf_ref, of_ref, pl.multiple_of(c * GLA_C, GLA_C),
                   sf_ref, af_ref, bf_ref, m1_ref, head_mask, False)
        _gla_chunk(qr_ref, kr_ref, vr_ref, gr_ref, ob_ref, pl.multiple_of((nch - 1 - c) * GLA_C, GLA_C),
                   sb_ref, ab_ref, bb_ref, m1_ref, head_mask, True)
        return carry

    lax.fori_loop(0, nch, body, 0)
    sfin_ref[0] = sf_ref[...]
    sfin_ref[1] = sb_ref[...]


def _gla_tables():
    fblk, bblk, first, init, seq = [], [], [], [], []
    for b in range(BATCH):
        fblk.append(b), bblk.append(b), first.append(1), init.append(0), seq.append(b)
    for s in range(DEC_BATCH):
        for j in range(LBB):
            fblk.append(NCB + s * LBB + j)
            bblk.append(NCB + s * LBB + LBB - 1 - j)
            first.append(1 if j == 0 else 0)
            init.append(1 + s)
            seq.append(BATCH + s)
    return [jnp.asarray(np.array(t, np.int32)) for t in (fblk, bblk, first, init, seq)]


def _gla(a_proj, glog, s0_all, m1):
    tables = _gla_tables()
    nsteps = int(tables[0].shape[0])
    nseq = BATCH + DEC_BATCH

    def fmap(col):
        return lambda i, fb, bb, fi, ini, sq: (fb[i], col)

    def rmap(col):
        return lambda i, fb, bb, fi, ini, sq: (bb[i], col)

    gs = pltpu.PrefetchScalarGridSpec(
        num_scalar_prefetch=5,
        grid=(nsteps,),
        in_specs=[
            pl.BlockSpec((TM, A_QK), fmap(0)), pl.BlockSpec((TM, A_QK), fmap(1)),
            pl.BlockSpec((TM, A_V), fmap(1)), pl.BlockSpec((TM, A_QK), fmap(0)),
            pl.BlockSpec((TM, A_QK), rmap(0)), pl.BlockSpec((TM, A_QK), rmap(1)),
            pl.BlockSpec((TM, A_V), rmap(1)), pl.BlockSpec((TM, A_QK), rmap(1)),
            pl.BlockSpec((None, 2, A_QK, DV_A), lambda i, fb, bb, fi, ini, sq: (ini[i], 0, 0, 0)),
            pl.BlockSpec((A_QK, A_V), lambda i, fb, bb, fi, ini, sq: (0, 0)),
        ],
        out_specs=[
            pl.BlockSpec((TM, A_V), fmap(0)), pl.BlockSpec((TM, A_V), rmap(0)),
            pl.BlockSpec((None, 2, A_QK, DV_A), lambda i, fb, bb, fi, ini, sq: (sq[i], 0, 0, 0)),
        ],
        scratch_shapes=[
            pltpu.VMEM((A_QK, DV_A), F32), pltpu.VMEM((A_QK, DV_A), F32),
            pltpu.VMEM((GLA_C * GLA_SB, A_QK), BF16), pltpu.VMEM((GLA_C * GLA_SB, A_QK), BF16),
            pltpu.VMEM((GLA_C, A_QK), F32), pltpu.VMEM((GLA_C, A_QK), F32),
        ],
    )
    return pl.pallas_call(
        _gla_kernel,
        grid_spec=gs,
        out_shape=[
            jax.ShapeDtypeStruct((M_TOK, A_V), F32), jax.ShapeDtypeStruct((M_TOK, A_V), F32),
            jax.ShapeDtypeStruct((nseq, 2, A_QK, DV_A), F32),
        ],
        compiler_params=_cparams(("arbitrary",)),
        name="gla",
    )(*tables, a_proj, a_proj, a_proj, glog, a_proj, a_proj, a_proj, glog, s0_all, m1)


def _softmax_parts(s):
    e = jnp.exp(s - jnp.max(s, axis=-1, keepdims=True))
    return e, 1.0 / jnp.sum(e, axis=-1, keepdims=True)


def _diffattn_body(q_ref, k_ref, v_ref, lq1_ref, lk1_ref, lq2_ref, lk2_ref, gsub_ref, o_ref, lam_init):
    lam = (jnp.exp(jnp.sum(lq1_ref[...] * lk1_ref[...], axis=-1, keepdims=True))
           - jnp.exp(jnp.sum(lq2_ref[...] * lk2_ref[...], axis=-1, keepdims=True)) + lam_init)
    lane = lax.broadcasted_iota(jnp.int32, (q_ref.shape[0], LANES), 1)
    for h in range(H_B):
        hs = slice(h * LANES, (h + 1) * LANES)
        q_h = q_ref[:, hs]
        k_h = k_ref[:, hs]
        zero = jnp.zeros_like(q_h)
        e1, r1 = _softmax_parts(_dot_nt(jnp.where(lane < DH_B, q_h, zero), k_h))
        e2, r2 = _softmax_parts(_dot_nt(jnp.where(lane >= DH_B, q_h, zero), k_h))
        w = e1 * r1 - e2 * (lam * r2)
        o = _dot(w.astype(BF16), v_ref[:, hs])
        o_ref[:, hs] = (_rms(o, gsub_ref[...]) * (1.0 - lam_init)).astype(BF16)


def _diffattn_ctx_kernel(q_ref, k_ref, v_ref, lq1, lk1, lq2, lk2, gsub, o_ref, *, lam_init):
    _diffattn_body(q_ref, k_ref, v_ref, lq1, lk1, lq2, lk2, gsub, o_ref, lam_init)


def _diffattn_lat_kernel(q_ref, k_ref, v_ref, lq1, lk1, lq2, lk2, gsub, prev_ref, o_ref, *, lam_init):
    del prev_ref
    _diffattn_body(q_ref, k_ref, v_ref, lq1, lk1, lq2, lk2, gsub, o_ref, lam_init)


def _diffattn(qb, kb, vb, k_lat, v_lat, lam_params, gsub, lam_init):
    small = [pl.BlockSpec((1, DH_B), lambda *_: (0, 0))] * 4 + [pl.BlockSpec((1, DV_B), lambda *_: (0, 0))]
    out_sds = jax.ShapeDtypeStruct((M_TOK, B_V), BF16)
    ctx = pl.pallas_call(
        functools.partial(_diffattn_ctx_kernel, lam_init=lam_init),
        grid=(NCB,),
        in_specs=[_row_spec(B_QK), _row_spec(B_QK), _row_spec(B_V)] + small,
        out_specs=_row_spec(B_V),
        out_shape=out_sds,
        compiler_params=_cparams(("arbitrary",)),
        name="diffattn_ctx",
    )(qb, kb, vb, *lam_params, gsub)
    nk = DEC_SEQ + PAST_LEN
    nqb = DEC_SEQ // TQ_LAT
    off = N_CTX // TQ_LAT
    return pl.pallas_call(
        functools.partial(_diffattn_lat_kernel, lam_init=lam_init),
        grid=(DEC_BATCH, nqb),
        in_specs=[
            pl.BlockSpec((TQ_LAT, B_QK), lambda b, j: (off + b * nqb + j, 0)),
            pl.BlockSpec((None, nk, B_QK), lambda b, j: (b, 0, 0)),
            pl.BlockSpec((None, nk, B_V), lambda b, j: (b, 0, 0)),
        ] + small + [pl.BlockSpec(memory_space=pl.ANY)],
        out_specs=pl.BlockSpec((TQ_LAT, B_V), lambda b, j: (off + b * nqb + j, 0)),
        out_shape=out_sds,
        input_output_aliases={8: 0},
        compiler_params=_cparams(("arbitrary", "arbitrary")),
        name="diffattn_lat",
    )(qb, k_lat, v_lat, *lam_params, gsub, ctx)


def _postmix_even_kernel(of_ref, ob_ref, ra_ref, attn_ref, x_ref, g1_ref, ggla_ref, wo_ref, o_ref):
    heads = []
    for h in range(H_A):
        hs = slice(h * DV_A, (h + 1) * DV_A)
        heads.append((_rms(of_ref[:, hs] + ob_ref[:, hs], ggla_ref[...]) * _silu(ra_ref[:, hs])).astype(BF16))
    mix = jnp.concatenate(heads + [attn_ref[...]], axis=1)
    o_ref[...] = x_ref[...] + g1_ref[...] * _dot(mix, wo_ref[...])


def _postmix_even(o_f, o_b, a_proj, attn, x, mod, g_gla, w_out):
    return pl.pallas_call(
        _postmix_even_kernel,
        grid=(NB,),
        in_specs=[_row_spec(A_V), _row_spec(A_V), _row_spec(A_V, 2), _row_spec(B_V), _row_spec(D_MODEL),
                  _mod_spec(2), _const_spec((1, DV_A)), _const_spec((A_V + B_V, D_MODEL))],
        out_specs=_row_spec(D_MODEL),
        out_shape=jax.ShapeDtypeStruct((M_TOK, D_MODEL), F32),
        compiler_params=_cparams(("arbitrary",)),
        name="postmix_even",
    )(o_f, o_b, a_proj, attn, x, mod, g_gla, w_out)


def _ffn_even_kernel(x_ref, gam_ref, sh_ref, sc_ref, g2_ref, wg_ref, wu_ref, wd_ref, o_ref):
    x = x_ref[...]
    hb = _norm_mod(x, gam_ref[...], sc_ref[...], sh_ref[...]).astype(BF16)
    acc = None
    nsub = 2
    w = D_FF // nsub
    for c in range(nsub):
        cs = slice(c * w, (c + 1) * w)
        act = (_silu(_dot(hb, wg_ref[:, cs])) * _dot(hb, wu_ref[:, cs])).astype(BF16)
        y = _dot(act, wd_ref[cs, :])
        acc = y if acc is None else acc + y
    o_ref[...] = x + g2_ref[...] * acc


def _ffn_even(x, gamma, mod, w_gate, w_up, w_down):
    return pl.pallas_call(
        _ffn_even_kernel,
        grid=(NB,),
        in_specs=[_row_spec(D_MODEL), _const_spec((1, D_MODEL)), _mod_spec(3), _mod_spec(4), _mod_spec(5),
                  _const_spec((D_MODEL, D_FF)), _const_spec((D_MODEL, D_FF)), _const_spec((D_FF, D_MODEL))],
        out_specs=_row_spec(D_MODEL),
        out_shape=jax.ShapeDtypeStruct((M_TOK, D_MODEL), F32),
        compiler_params=_cparams(("arbitrary",)),
        name="ffn_even",
    )(x, gamma, mod, mod, mod, w_gate, w_up, w_down)


def _inproj_odd_kernel(x_ref, gam_ref, sh_ref, sc_ref, w_ref, gq_ref, gk_ref, cos_ref, sin_ref,
                       q_ref, k_ref, v_ref, kctx_ref, vctx_ref):
    is_lat = pl.program_id(0) >= NCB
    is_ctx = jnp.logical_not(is_lat)
    hb = _norm_mod(x_ref[...], gam_ref[...], sc_ref[...], sh_ref[...]).astype(BF16)
    z = _dot(hb, w_ref[...])
    cos = jnp.where(is_lat, cos_ref[...], 1.0)
    sin = jnp.where(is_lat, sin_ref[...], 0.0)
    scale = DH_C ** -0.5
    for h in range(H_C):
        hs = slice(h * DH_C, (h + 1) * DH_C)
        q_ref[:, hs] = (_rope(_rms(z[:, hs], gq_ref[...]), cos, sin, DH_C // 4) * scale).astype(BF16)
    nq = H_C * DH_C
    for h in range(HKV_C):
        hs = slice(h * DH_C, (h + 1) * DH_C)
        k_n = _rms(z[:, nq + h * DH_C:nq + (h + 1) * DH_C], gk_ref[...])

        @pl.when(is_ctx)
        def _(k_n=k_n, hs=hs):
            kctx_ref[:, hs] = k_n

        k_ref[:, hs] = _rope(k_n, cos, sin, DH_C // 4).astype(BF16)
    nkv = HKV_C * DH_C
    v = z[:, nq + nkv:]

    @pl.when(is_ctx)
    def _():
        vctx_ref[...] = v

    v_ref[...] = v.astype(BF16)


def _inproj_odd(x, gamma, mod, w_in, g_q, g_k, cos, sin):
    nq, nkv = H_C * DH_C, HKV_C * DH_C
    return pl.pallas_call(
        _inproj_odd_kernel,
        grid=(NB,),
        in_specs=[
            _row_spec(D_MODEL), _const_spec((1, D_MODEL)), _mod_spec(0), _mod_spec(1),
            _const_spec((D_MODEL, nq + 2 * nkv)), _const_spec((1, DH_C)), _const_spec((1, DH_C)),
            pl.BlockSpec((TM, LANES), lambda i: (_pos_block(i), 0)),
            pl.BlockSpec((TM, LANES), lambda i: (_pos_block(i), 0)),
        ],
        out_specs=[_row_spec(nq), _row_spec(nkv), _row_spec(nkv), _ctx_spec(nkv), _ctx_spec(nkv)],
        out_shape=[
            jax.ShapeDtypeStruct((M_TOK, nq), BF16), jax.ShapeDtypeStruct((M_TOK, nkv), BF16),
            jax.ShapeDtypeStruct((M_TOK, nkv), BF16), jax.ShapeDtypeStruct((N_CTX, nkv), F32),
            jax.ShapeDtypeStruct((N_CTX, nkv), F32),
        ],
        compiler_params=_cparams(("arbitrary",)),
        name="inproj_odd",
    )(x, gamma, mod, mod, w_in, g_q, g_k, cos, sin)


def _gqa_body(q_ref, k_ref, v_ref, o_ref):
    rep = H_C // HKV_C
    tq = q_ref.shape[0]
    for hk in range(HKV_C):
        ks = slice(hk * DH_C, (hk + 1) * DH_C)
        q_g = jnp.concatenate([q_ref[:, (hk * rep + g) * DH_C:(hk * rep + g + 1) * DH_C] for g in range(rep)],
                              axis=0)
        e, r = _softmax_parts(_dot_nt(q_g, k_ref[:, ks]))
        o = _dot((e * r).astype(BF16), v_ref[:, ks])
        for g in range(rep):
            o_ref[:, (hk * rep + g) * DH_C:(hk * rep + g + 1) * DH_C] = o[g * tq:(g + 1) * tq].astype(BF16)


def _gqa_ctx_kernel(q_ref, k_ref, v_ref, o_ref):
    _gqa_body(q_ref, k_ref, v_ref, o_ref)


def _gqa_lat_kernel(q_ref, k_ref, v_ref, prev_ref, o_ref):
    del prev_ref
    _gqa_body(q_ref, k_ref, v_ref, o_ref)


def _gqa(q, k, v, k_lat, v_lat):
    nq, nkv = H_C * DH_C, HKV_C * DH_C
    out_sds = jax.ShapeDtypeStruct((M_TOK, nq), BF16)
    ctx = pl.pallas_call(
        _gqa_ctx_kernel,
        grid=(NCB,),
        in_specs=[_row_spec(nq), _row_spec(nkv), _row_spec(nkv)],
        out_specs=_row_spec(nq),
        out_shape=out_sds,
        compiler_params=_cparams(("arbitrary",)),
        name="gqa_ctx",
    )(q, k, v)
    nk = DEC_SEQ + PAST_LEN
    nqb = DEC_SEQ // TQ_LAT
    off = N_CTX // TQ_LAT
    return pl.pallas_call(
        _gqa_lat_kernel,
        grid=(DEC_BATCH, nqb),
        in_specs=[
            pl.BlockSpec((TQ_LAT, nq), lambda b, j: (off + b * nqb + j, 0)),
            pl.BlockSpec((None, nk, nkv), lambda b, j: (b, 0, 0)),
            pl.BlockSpec((None, nk, nkv), lambda b, j: (b, 0, 0)),
            pl.BlockSpec(memory_space=pl.ANY),
        ],
        out_specs=pl.BlockSpec((TQ_LAT, nq), lambda b, j: (off + b * nqb + j, 0)),
        out_shape=out_sds,
        input_output_aliases={3: 0},
        compiler_params=_cparams(("arbitrary", "arbitrary")),
        name="gqa_lat",
    )(q, k_lat, v_lat, ctx)


def _postmix_odd_kernel(attn_ref, x_ref, g1_ref, wo_ref, gam_ref, sh_ref, sc_ref, wr_ref,
                        x1_ref, h_ref, w1_ref, w2_ref, idx_ref, cnt_ref, run_ref):
    @pl.when(pl.program_id(0) == 0)
    def _():
        run_ref[...] = jnp.zeros_like(run_ref)

    x1 = x_ref[...] + g1_ref[...] * _dot(attn_ref[...], wo_ref[...])
    x1_ref[...] = x1
    h = _norm_mod(x1, gam_ref[...], sc_ref[...], sh_ref[...])
    h_ref[...] = h.astype(BF16)
    logits = _dot_hi(h, wr_ref[...])
    lane = lax.broadcasted_iota(jnp.int32, logits.shape, 1)
    lane_f = lane.astype(F32)
    lg = jnp.where(lane < N_EXPERTS, logits, -jnp.inf)
    m1 = jnp.max(lg, axis=-1, keepdims=True)
    i1 = jnp.min(jnp.where(lg == m1, lane_f, float(LANES)), axis=-1, keepdims=True)
    lg2 = jnp.where(lane_f == i1, -jnp.inf, lg)
    m2 = jnp.max(lg2, axis=-1, keepdims=True)
    i2 = jnp.min(jnp.where(lg2 == m2, lane_f, float(LANES)), axis=-1, keepdims=True)
    e = jnp.exp(m2 - m1)
    w1 = 1.0 / (1.0 + e)
    w1_ref[...] = jnp.broadcast_to(w1, logits.shape)
    w2_ref[...] = jnp.broadcast_to(e * w1, logits.shape)
    oh1 = jnp.where(lane_f == i1, 1.0, 0.0)
    oh2 = jnp.where(lane_f == i2, 1.0, 0.0)
    both = oh1 + oh2
    r_i = lax.broadcasted_iota(jnp.int32, (TM, TM), 0)
    c_i = lax.broadcasted_iota(jnp.int32, (TM, TM), 1)
    before = jnp.where(c_i < r_i, 1.0, 0.0).astype(BF16)
    pref = _dot(before, both.astype(BF16)) + run_ref[0:1, :]
    rank1 = jnp.sum(pref * oh1, axis=-1, keepdims=True)
    rank2 = jnp.sum(pref * oh2, axis=-1, keepdims=True)
    run_ref[...] = run_ref[...] + jnp.sum(both, axis=0, keepdims=True)
    cnt_ref[...] = run_ref[...]
    quarter = LANES // 4
    idx_ref[...] = jnp.where(lane < quarter, i1, jnp.where(lane < 2 * quarter, i2, jnp.where(
        lane < 3 * quarter, rank1, rank2))).astype(jnp.int32)


def _postmix_odd(attn, x, mod, w_out, gamma, w_router):
    return pl.pallas_call(
        _postmix_odd_kernel,
        grid=(NB,),
        in_specs=[_row_spec(H_C * DH_C), _row_spec(D_MODEL), _mod_spec(2), _const_spec((H_C * DH_C, D_MODEL)),
                  _const_spec((1, D_MODEL)), _mod_spec(3), _mod_spec(4), _const_spec((D_MODEL, LANES))],
        out_specs=[_row_spec(D_MODEL), _row_spec(D_MODEL), _row_spec(LANES), _row_spec(LANES), _row_spec(LANES),
                   _const_spec((8, LANES))],
        out_shape=[
            jax.ShapeDtypeStruct((M_TOK, D_MODEL), F32), jax.ShapeDtypeStruct((M_TOK, D_MODEL), BF16),
            jax.ShapeDtypeStruct((M_TOK, LANES), F32), jax.ShapeDtypeStruct((M_TOK, LANES), F32),
            jax.ShapeDtypeStruct((M_TOK, LANES), jnp.int32), jax.ShapeDtypeStruct((8, LANES), F32),
        ],
        scratch_shapes=[pltpu.VMEM((8, LANES), F32)],
        compiler_params=_cparams(("arbitrary",)),
        name="postmix_odd",
    )(attn, x, mod, w_out, gamma, mod, mod, w_router)


def _experts_kernel(te_ref, tv_ref, x_ref, wg_ref, wu_ref, wd_ref, o_ref):
    i, f = pl.program_id(0), pl.program_id(1)

    @pl.when(tv_ref[i] == 1)
    def _():
        x = x_ref[...]
        acc = None
        nsub = 2
        w = TF_E // nsub
        for c in range(nsub):
            cs = slice(c * w, (c + 1) * w)
            act = (_silu(_dot(x, wg_ref[:, cs])) * _dot(x, wu_ref[:, cs])).astype(BF16)
            y = _dot(act, wd_ref[cs, :])
            acc = y if acc is None else acc + y

        @pl.when(f == 0)
        def _():
            o_ref[...] = acc

        @pl.when(f > 0)
        def _():
            o_ref[...] += acc

    @pl.when(jnp.logical_and(tv_ref[i] == 0, f == 0))
    def _():
        o_ref[...] = jnp.zeros_like(o_ref)


def _experts(tile_expert, tile_valid, xs, w_gate, w_up, w_down):
    gs = pltpu.PrefetchScalarGridSpec(
        num_scalar_prefetch=2,
        grid=(NT_E, D_FF_E // TF_E),
        in_specs=[
            pl.BlockSpec((TME, D_MODEL), lambda i, f, te, tv: (i, 0)),
            pl.BlockSpec((None, D_MODEL, TF_E), lambda i, f, te, tv: (te[i], 0, f)),
            pl.BlockSpec((None, D_MODEL, TF_E), lambda i, f, te, tv: (te[i], 0, f)),
            pl.BlockSpec((None, TF_E, D_MODEL), lambda i, f, te, tv: (te[i], f, 0)),
        ],
        out_specs=pl.BlockSpec((TME, D_MODEL), lambda i, f, te, tv: (i, 0)),
    )
    return pl.pallas_call(
        _experts_kernel,
        grid_spec=gs,
        out_shape=jax.ShapeDtypeStruct((NT_E * TME, D_MODEL), F32),
        compiler_params=_cparams(("arbitrary", "arbitrary")),
        name="experts",
    )(tile_expert, tile_valid, xs, w_gate, w_up, w_down)


def _combine_kernel(x_ref, y1_ref, y2_ref, w1_ref, w2_ref, g2_ref, gam_ref, octx_ref, olat_ref):
    rep = D_MODEL // LANES
    w1 = jnp.concatenate([w1_ref[...]] * rep, axis=1)
    w2 = jnp.concatenate([w2_ref[...]] * rep, axis=1)
    x2 = x_ref[...] + g2_ref[...] * (w1 * y1_ref[...] + w2 * y2_ref[...])
    y = _rms(x2, gam_ref[...])
    is_lat = pl.program_id(0) >= NCB

    @pl.when(jnp.logical_not(is_lat))
    def _():
        octx_ref[...] = y

    @pl.when(is_lat)
    def _():
        olat_ref[...] = y


def _combine(x1, y1, y2, w1, w2, mod, gamma):
    return pl.pallas_call(
        _combine_kernel,
        grid=(NB,),
        in_specs=[_row_spec(D_MODEL), _row_spec(D_MODEL), _row_spec(D_MODEL), _row_spec(LANES), _row_spec(LANES),
                  _mod_spec(5), _const_spec((1, D_MODEL))],
        out_specs=[_ctx_spec(D_MODEL), _lat_spec(D_MODEL)],
        out_shape=[jax.ShapeDtypeStruct((N_CTX, D_MODEL), F32), jax.ShapeDtypeStruct((N_LAT, D_MODEL), F32)],
        compiler_params=_cparams(("arbitrary",)),
        name="combine",
    )(x1, y1, y2, w1, w2, mod, gamma)


def _rope_tables(rot_dim):
    n = DEC_SEQ
    rows = jnp.repeat(jnp.arange(n // GRID_W, dtype=F32), GRID_W)
    cols = jnp.tile(jnp.arange(GRID_W, dtype=F32), n // GRID_W)
    half = rot_dim // 2
    freqs = ROPE_THETA ** (-jnp.arange(0, half, 2, dtype=F32) / half)
    ang_r, ang_c = rows[:, None] * freqs, cols[:, None] * freqs
    cos = jnp.concatenate([jnp.cos(ang_r)] * 2 + [jnp.cos(ang_c)] * 2, axis=-1)
    sin = jnp.concatenate([-jnp.sin(ang_r), jnp.sin(ang_r), -jnp.sin(ang_c), jnp.sin(ang_c)], axis=-1)
    rep = LANES // rot_dim
    return jnp.tile(cos, (1, rep)), jnp.tile(sin, (1, rep))


def _rows(a, idx):
    return a.at[idx].get(mode="promise_in_bounds")


def _route(idx, counts_f):
    quarter = LANES // 4
    experts = jnp.arange(N_EXPERTS, dtype=jnp.int32)
    e2 = jnp.stack([idx[:, 0], idx[:, quarter]], axis=1)
    rank2 = jnp.stack([idx[:, 2 * quarter], idx[:, 3 * quarter]], axis=1)
    counts = counts_f[0, :N_EXPERTS].astype(jnp.int32)
    tiles = (counts + TME - 1) // TME
    tile_end = jnp.cumsum(tiles)
    tile_start = tile_end - tiles
    start = jnp.cumsum(counts) - counts
    onehot = e2[:, :, None] == experts[None, None, :]
    pos = jnp.sum(jnp.where(onehot, tile_start[None, None, :] * TME, 0), axis=-1) + rank2
    tile_id = jnp.arange(NT_E, dtype=jnp.int32)
    tile_expert = jnp.minimum(jnp.sum((tile_id[:, None] >= tile_end[None, :]).astype(jnp.int32), axis=1),
                              N_EXPERTS - 1).astype(jnp.int32)
    tile_valid = (tile_id < tile_end[-1]).astype(jnp.int32)
    t_oh = tile_expert[:, None] == experts[None, :]
    t_first = jnp.sum(jnp.where(t_oh, (start - tile_start * TME)[None, :], 0), axis=1) + tile_id * TME
    t_last = jnp.sum(jnp.where(t_oh, (start + counts)[None, :], 0), axis=1)
    order = jnp.argsort(e2.reshape(-1), stable=True).astype(jnp.int32)
    n_assign = order.shape[0]
    g_idx = t_first[:, None] + jnp.arange(TME, dtype=jnp.int32)[None, :]
    live = jnp.logical_and(g_idx < t_last[:, None], tile_valid[:, None] == 1)
    src = _rows(order, jnp.clip(g_idx, 0, n_assign - 1).reshape(-1)) // 2
    src_tok = jnp.where(live.reshape(-1), src, 0).astype(jnp.int32)
    return src_tok, pos, tile_expert, tile_valid


def lambda_init(layer):
    return 0.8 - 0.6 * math.exp(-0.3 * layer)


def kernel(x_prompt, x_sample, state_a, cache_b_k, cache_b_v, cache_c_k, cache_c_v, c, c_ctx, w_mod, b_mod, norm_mix, norm_ffn, w_in_even, w_gate2_a, b_gate_a, g_gla, lam_q1, lam_k1, lam_q2, lam_k2, g_sub_b, w_out_even, w_in_odd, g_q_c, g_k_c, w_out_odd, ffn_gate, ffn_up, ffn_down, w_router, exp_gate, exp_up, exp_down, norm_final):
    x = jnp.concatenate([x_prompt.reshape(N_CTX, D_MODEL), x_sample.reshape(N_LAT, D_MODEL)], axis=0)
    cond = jnp.concatenate([c_ctx[None, :], c, jnp.zeros((N_COND - 1 - DEC_BATCH, D_MODEL), F32)], axis=0)
    mod = _modulation(cond, w_mod, b_mod).reshape(DEPTH, N_COND, 1, 6 * D_MODEL)

    w = w_in_even[0]
    na = 2 * A_QK + 2 * A_V
    gate_lo = na
    gate_hi = na + 2 * GATE_RANK
    col_scale = jnp.concatenate([jnp.full((A_QK,), DK_A ** -0.5, F32), jnp.ones((na - A_QK,), F32),
                                 jnp.full((B_QK,), DH_B ** -0.5, F32), jnp.ones((B_QK + B_V,), F32)])
    w_main = (jnp.concatenate([w[:, :gate_lo], w[:, gate_hi:]], axis=1) * col_scale).astype(BF16)
    w_gates = jnp.pad(w[:, gate_lo:gate_hi], ((0, 0), (0, LANES - 2 * GATE_RANK))).astype(BF16)
    g2 = jnp.zeros((LANES, 2 * A_QK), F32)
    g2 = g2.at[:GATE_RANK, :A_QK].set(w_gate2_a[0, 0]).at[GATE_RANK:2 * GATE_RANK, A_QK:].set(w_gate2_a[0, 1])
    bg = b_gate_a[0].reshape(1, 2 * A_QK)
    cos_b, sin_b = _rope_tables(DH_B)
    a_proj, glog, qb, kb, vb, kb_ctx, vb_ctx = _inproj_even(
        x, norm_mix[0][None, :], mod[0], w_main, w_gates, g2.astype(BF16), bg, cos_b, sin_b)

    s0_all = jnp.concatenate([jnp.zeros((1, 2, A_QK, DV_A), F32),
                              state_a[:, 0].reshape(DEC_BATCH, 2, A_QK, DV_A)], axis=0)
    hd = np.arange(A_QK)[:, None] // DK_A == np.arange(A_V)[None, :] // DV_A
    m1 = jnp.asarray(hd.astype(np.float32)).astype(BF16)
    o_f, o_b, s_fin = _gla(a_proj, glog, s0_all, m1)

    k_lat = jnp.concatenate([kb[N_CTX:].reshape(DEC_BATCH, DEC_SEQ, B_QK),
                             cache_b_k[:, 0].reshape(DEC_BATCH, PAST_LEN, B_QK).astype(BF16)], axis=1)
    v_lat = jnp.concatenate([vb[N_CTX:].reshape(DEC_BATCH, DEC_SEQ, B_V),
                             cache_b_v[:, 0].reshape(DEC_BATCH, PAST_LEN, B_V).astype(BF16)], axis=1)
    lam_params = [p[0][None, :] for p in (lam_q1, lam_k1, lam_q2, lam_k2)]
    attn_b = _diffattn(qb, kb, vb, k_lat, v_lat, lam_params, g_sub_b[0][None, :], lambda_init(0))

    x = _postmix_even(o_f, o_b, a_proj, attn_b, x, mod[0], g_gla[0][None, :], w_out_even[0].astype(BF16))
    x = _ffn_even(x, norm_ffn[0][None, :], mod[0], ffn_gate[0].astype(BF16), ffn_up[0].astype(BF16),
                  ffn_down[0].astype(BF16))

    cos_c, sin_c = _rope_tables(DH_C)
    q_c, k_c, v_c, kc_ctx, vc_ctx = _inproj_odd(x, norm_mix[1][None, :], mod[1], w_in_odd[0].astype(BF16),
                                     g_q_c[0][None, :], g_k_c[0][None, :], cos_c, sin_c)
    nkv = HKV_C * DH_C
    k_lat = jnp.concatenate([k_c[N_CTX:].reshape(DEC_BATCH, DEC_SEQ, nkv),
                             cache_c_k[:, 0].reshape(DEC_BATCH, PAST_LEN, nkv).astype(BF16)], axis=1)
    v_lat = jnp.concatenate([v_c[N_CTX:].reshape(DEC_BATCH, DEC_SEQ, nkv),
                             cache_c_v[:, 0].reshape(DEC_BATCH, PAST_LEN, nkv).astype(BF16)], axis=1)
    attn_c = _gqa(q_c, k_c, v_c, k_lat, v_lat)
    w_r = jnp.pad(w_router[0], ((0, 0), (0, LANES - N_EXPERTS)))
    x1, h_moe, w1, w2, idx, counts = _postmix_odd(attn_c, x, mod[1], w_out_odd[0].astype(BF16),
                                                  norm_ffn[1][None, :], w_r)

    src_tok, pos, tile_expert, tile_valid = _route(idx, counts)
    xs = _rows(h_moe, src_tok)
    ys = _experts(tile_expert, tile_valid, xs, exp_gate[0].astype(BF16), exp_up[0].astype(BF16),
                  exp_down[0].astype(BF16))
    y1 = _rows(ys, pos[:, 0])
    y2 = _rows(ys, pos[:, 1])
    y_ctx, y_lat = _combine(x1, y1, y2, w1, w2, mod[1], norm_final[None, :])

    y_prompt = y_ctx.reshape(BATCH, SEQ, D_MODEL)
    y_sample = y_lat.reshape(DEC_BATCH, DEC_SEQ, D_MODEL)
    new_state_a = s_fin[:BATCH].reshape(BATCH, 1, 2, H_A, DK_A, DV_A)
    new_b_k = kb_ctx.reshape(BATCH, 1, SEQ, H_B, 2, DH_B)
    new_b_v = vb_ctx.reshape(BATCH, 1, SEQ, H_B, DV_B)
    new_c_k = kc_ctx.reshape(BATCH, 1, SEQ, HKV_C, DH_C)
    new_c_v = vc_ctx.reshape(BATCH, 1, SEQ, HKV_C, DH_C)
    return (y_prompt, y_sample, new_state_a, new_b_k, new_b_v, new_c_k, new_c_v)
```

```python
import functools
import math

import numpy as np
import jax
import jax.numpy as jnp
from jax import lax
from jax.experimental import pallas as pl
from jax.experimental.pallas import tpu as pltpu

D_MODEL = 1024
BATCH = 32
SEQ = 256
DEPTH = 2
DEC_BATCH = 2
DEC_SEQ = 2048
PAST_LEN = 512
GRID_W = 64
H_A, DK_A, DV_A = 4, 64, 128
GATE_RANK = 16
GATE_TAU = 16.0
H_B, DH_B = 4, 64
DV_B = 2 * DH_B
H_C, HKV_C, DH_C = 8, 2, 128
D_FF = 2816
N_EXPERTS = 8
D_FF_E = 3584
ROPE_THETA = 10000.0
EPS = 1e-6
A_QK = H_A * DK_A
A_V = H_A * DV_A
B_QK = H_B * 2 * DH_B
B_V = H_B * DV_B

F32 = jnp.float32
BF16 = jnp.bfloat16

V7X_VMEM_BYTES = 64 * 1024 * 1024
VMEM_LIMIT = V7X_VMEM_BYTES * 7 // 8
LANES = 128
TM = 256
N_CTX = BATCH * SEQ
N_LAT = DEC_BATCH * DEC_SEQ
M_TOK = N_CTX + N_LAT
NB = M_TOK // TM
NCB = N_CTX // TM
LBB = DEC_SEQ // TM
N_COND = 8
GLA_C = 64
GLA_SB = 16
TME = 512
NT_E = 2 * M_TOK // TME + N_EXPERTS
NC_E = 2
TF_E = D_FF_E // NC_E
TQ_LAT = 256

assert SEQ == TM and DEC_SEQ % TM == 0 and DEPTH == 2


def _cparams(sem):
    return pltpu.CompilerParams(dimension_semantics=sem, vmem_limit_bytes=VMEM_LIMIT)


def _group(i):
    return jnp.where(i < NCB, 0, 1 + (i - NCB) // LBB)


def _pos_block(i):
    return jnp.where(i < NCB, 0, (i - NCB) % LBB)


def _mod_spec(chunk):
    return pl.BlockSpec((None, 1, D_MODEL), lambda i, c=chunk: (_group(i), 0, c))


def _row_spec(width, col=0):
    return pl.BlockSpec((TM, width), lambda i, c=col: (i, c))


def _ctx_spec(width):
    return pl.BlockSpec((TM, width), lambda i: (jnp.minimum(i, NCB - 1), 0))


def _lat_spec(width):
    return pl.BlockSpec((TM, width), lambda i: (jnp.maximum(i - NCB, 0), 0))


def _const_spec(shape):
    nd = len(shape)
    return pl.BlockSpec(shape, lambda i, nd=nd: (0,) * nd)


def _dot(a, b):
    return jnp.dot(a, b, preferred_element_type=F32)


def _dot_nt(a, b):
    return lax.dot_general(a, b, (((1,), (1,)), ((), ())), preferred_element_type=F32)


def _split2(a):
    hi = a.astype(BF16)
    lo = (a - hi.astype(F32)).astype(BF16)
    return hi, lo


def _dot_hi(a, b):
    a_hi, a_lo = _split2(a)
    b_hi, b_lo = _split2(b)
    return _dot(a_hi, b_hi) + (_dot(a_hi, b_lo) + _dot(a_lo, b_hi))


def _silu(x):
    return x * (1.0 / (1.0 + jnp.exp(-x)))


def _log_sigmoid(x):
    return jnp.minimum(x, 0.0) - jnp.log(1.0 + jnp.exp(-jnp.abs(x)))


def _rms(x, g):
    return x * lax.rsqrt(jnp.mean(x * x, axis=-1, keepdims=True) + EPS) * g


def _norm_mod(x, gamma, sc, sh):
    return _rms(x, gamma) * (1.0 + sc) + sh


def _rope(x, cos, sin, half):
    lane = lax.broadcasted_iota(jnp.int32, x.shape, 1)
    first = (lane % (2 * half)) < half
    swapped = jnp.where(first, pltpu.roll(x, LANES - half, 1), pltpu.roll(x, half, 1))
    return x * cos + swapped * sin


def _mod_kernel(cond_ref, w_ref, b_ref, o_ref):
    o_ref[...] = _dot_hi(_silu(cond_ref[...]), w_ref[...]) + b_ref[...]


def _modulation(cond, w_mod, b_mod):
    tn = 1536
    return pl.pallas_call(
        _mod_kernel,
        grid=(DEPTH, 6 * D_MODEL // tn),
        in_specs=[
            pl.BlockSpec((N_COND, D_MODEL), lambda l, j: (0, 0)),
            pl.BlockSpec((None, D_MODEL, tn), lambda l, j: (l, 0, j)),
            pl.BlockSpec((None, 1, tn), lambda l, j: (l, 0, j)),
        ],
        out_specs=pl.BlockSpec((None, N_COND, tn), lambda l, j: (l, 0, j)),
        out_shape=jax.ShapeDtypeStruct((DEPTH, N_COND, 6 * D_MODEL), F32),
        compiler_params=_cparams(("arbitrary", "arbitrary")),
        name="modulation",
    )(cond, w_mod, b_mod.reshape(DEPTH, 1, 6 * D_MODEL))


def _inproj_even_kernel(x_ref, gam_ref, sh_ref, sc_ref, w_ref, wg_ref, g2_ref, bg_ref, cos_ref, sin_ref,
                        a_ref, glog_ref, qb_ref, kb_ref, vb_ref, kctx_ref, vctx_ref):
    is_lat = pl.program_id(0) >= NCB
    hb = _norm_mod(x_ref[...], gam_ref[...], sc_ref[...], sh_ref[...]).astype(BF16)
    z = _dot(hb, w_ref[...])
    na = 2 * A_QK + 2 * A_V
    a_ref[...] = z[:, :na]
    gates = _dot(hb, wg_ref[...])
    xg = _dot(gates.astype(BF16), g2_ref[...]) + bg_ref[...]
    glog_ref[...] = _log_sigmoid(xg) * (1.0 / GATE_TAU)
    cos = jnp.where(is_lat, cos_ref[...], 1.0)
    sin = jnp.where(is_lat, sin_ref[...], 0.0)

    @pl.when(jnp.logical_not(is_lat))
    def _():
        kctx_ref[...] = z[:, na + B_QK:na + 2 * B_QK]
        vctx_ref[...] = z[:, na + 2 * B_QK:]

    for j in range(B_QK // LANES):
        lo = na + j * LANES
        qb_ref[:, j * LANES:(j + 1) * LANES] = _rope(z[:, lo:lo + LANES], cos, sin, DH_B // 4).astype(BF16)
        lo = na + B_QK + j * LANES
        kb_ref[:, j * LANES:(j + 1) * LANES] = _rope(z[:, lo:lo + LANES], cos, sin, DH_B // 4).astype(BF16)
    vb_ref[...] = z[:, na + 2 * B_QK:].astype(BF16)


def _inproj_even(x, gamma, mod, w_main, w_gates, g2, bg, cos, sin):
    na = 2 * A_QK + 2 * A_V
    nz = na + 2 * B_QK + B_V
    return pl.pallas_call(
        _inproj_even_kernel,
        grid=(NB,),
        in_specs=[
            _row_spec(D_MODEL), _const_spec((1, D_MODEL)), _mod_spec(0), _mod_spec(1),
            _const_spec((D_MODEL, nz)), _const_spec((D_MODEL, LANES)), _const_spec((LANES, 2 * A_QK)),
            _const_spec((1, 2 * A_QK)),
            pl.BlockSpec((TM, LANES), lambda i: (_pos_block(i), 0)),
            pl.BlockSpec((TM, LANES), lambda i: (_pos_block(i), 0)),
        ],
        out_specs=[_row_spec(na), _row_spec(2 * A_QK), _row_spec(B_QK), _row_spec(B_QK), _row_spec(B_V),
                   _ctx_spec(B_QK), _ctx_spec(B_V)],
        out_shape=[
            jax.ShapeDtypeStruct((M_TOK, na), F32), jax.ShapeDtypeStruct((M_TOK, 2 * A_QK), F32),
            jax.ShapeDtypeStruct((M_TOK, B_QK), BF16), jax.ShapeDtypeStruct((M_TOK, B_QK), BF16),
            jax.ShapeDtypeStruct((M_TOK, B_V), BF16), jax.ShapeDtypeStruct((N_CTX, B_QK), F32),
            jax.ShapeDtypeStruct((N_CTX, B_V), F32),
        ],
        compiler_params=_cparams(("arbitrary",)),
        name="inproj_even",
    )(x, gamma, mod, mod, w_main, w_gates, g2, bg, cos, sin)


def _gla_chunk(q_ref, k_ref, v_ref, g_ref, o_ref, row0, s_ref, a_ref, b_ref, m1_ref, head_mask, rev):
    nsb = GLA_C // GLA_SB
    rows = pl.ds(row0, GLA_C)
    q, k, v, g = q_ref[rows, :], k_ref[rows, :], v_ref[rows, :], g_ref[rows, :]
    v_b = v.astype(BF16)
    r_i = lax.broadcasted_iota(jnp.int32, (GLA_C, GLA_C), 0)
    c_i = lax.broadcasted_iota(jnp.int32, (GLA_C, GLA_C), 1)
    tri = jnp.where((c_i >= r_i) if rev else (c_i <= r_i), 1.0, 0.0).astype(BF16)
    g_hi = g.astype(BF16)
    g_r = g - g_hi.astype(F32)
    g_mid = g_r.astype(BF16)
    g_lo = (g_r - g_mid.astype(F32)).astype(BF16)
    bc = _dot(tri, g_hi) + (_dot(tri, g_mid) + _dot(tri, g_lo))
    b_ref[...] = bc
    ends = [bc[GLA_SB * j:GLA_SB * j + 1] if rev else bc[GLA_SB * j + GLA_SB - 1:GLA_SB * (j + 1)]
            for j in range(nsb)]
    last = ends[0] if rev else ends[nsb - 1]
    e_rows = jnp.concatenate([jnp.broadcast_to(e, (GLA_SB, A_QK)) for e in ends], axis=0)
    khat = k * jnp.exp(e_rows - bc)
    qc = q * jnp.exp(bc)
    kl = k * jnp.exp(last - bc)
    row = lax.broadcasted_iota(jnp.int32, (GLA_C, A_QK), 0)

    def expand(x):
        return jnp.where(head_mask, jnp.concatenate([x] * H_A, axis=0), 0.0).astype(BF16)

    scores = None
    for j in (range(1, nsb) if rev else range(nsb - 1)):
        q_rows = (row < GLA_SB * j) if rev else (row >= GLA_SB * (j + 1))
        qj = jnp.where(q_rows, q * jnp.exp(bc - ends[j]), 0.0)
        kj = jnp.where((row // GLA_SB) == j, khat, 0.0).astype(BF16)
        sj = _dot_nt(expand(qj), kj)
        scores = sj if scores is None else scores + sj
    s_old = s_ref[...]
    inter = _dot(expand(qc), s_old.astype(BF16))
    o_heads = []
    for h in range(H_A):
        hs = slice(h * GLA_C, (h + 1) * GLA_C)
        o_heads.append(inter[hs] + _dot(scores[hs].astype(BF16), v_b[:, h * DV_A:(h + 1) * DV_A]))
    o = jnp.concatenate(o_heads, axis=1)

    t_i = lax.broadcasted_iota(jnp.int32, (GLA_SB, A_QK), 0)
    diag = []
    for i in range(nsb):
        q_i = q_ref[pl.ds(row0 + GLA_SB * i, GLA_SB), :]
        b_i = b_ref[GLA_SB * i:GLA_SB * (i + 1), :]
        base = i * GLA_SB * GLA_SB
        for j in range(GLA_SB):
            s = GLA_SB * i + j
            k_s = k_ref[pl.ds(row0 + s, 1), :]
            b_s = b_ref[s:s + 1, :]
            keep = (t_i <= j) if rev else (t_i >= j)
            a = jnp.where(keep, q_i * k_s * jnp.exp(b_i - b_s), 0.0)
            a_ref[base + j * GLA_SB:base + (j + 1) * GLA_SB, :] = a.astype(BF16)
        w = _dot(a_ref[base:base + GLA_SB * GLA_SB, :], m1_ref[...])
        acc = None
        for j in range(GLA_SB):
            term = w[j * GLA_SB:(j + 1) * GLA_SB] * v_ref[pl.ds(row0 + GLA_SB * i + j, 1), :]
            acc = term if acc is None else acc + term
        diag.append(acc)
    o_ref[rows, :] = o + jnp.concatenate(diag, axis=0)

    t = jnp.transpose(jnp.concatenate([kl, jnp.broadcast_to(last, (GLA_C, A_QK))], axis=0))
    kv = _dot(t[:, :GLA_C].astype(BF16), v_b)
    a_col = jnp.exp(t[:, GLA_C:GLA_C + 1])
    for h in range(H_A):
        hs = slice(h * DK_A, (h + 1) * DK_A)
        s_ref[hs, :] = a_col[hs] * s_old[hs] + kv[hs, h * DV_A:(h + 1) * DV_A]


def _gla_kernel(fblk_ref, bblk_ref, first_ref, init_ref, seq_ref,
                qf_ref, kf_ref, vf_ref, gf_ref, qr_ref, kr_ref, vr_ref, gr_ref, s0_ref, m1_ref,
                of_ref, ob_ref, sfin_ref, sf_ref, sb_ref, af_ref, ab_ref, bf_ref, bb_ref):
    step = pl.program_id(0)

    @pl.when(first_ref[step] == 1)
    def _():
        sf_ref[...] = s0_ref[0]
        sb_ref[...] = s0_ref[1]

    r_h = lax.broadcasted_iota(jnp.int32, (H_A * GLA_C, A_QK), 0) // GLA_C
    c_h = lax.broadcasted_iota(jnp.int32, (H_A * GLA_C, A_QK), 1) // DK_A
    head_mask = r_h == c_h
    nch = TM // GLA_C

    def body(c, carry):
        _gla_chunk(qf_ref, kf_ref, vf_ref, gf_ref, of_ref, pl.multiple_of(c * GLA_C, GLA_C),
                   sf_ref, af_ref, bf_ref, m1_ref, head_mask, False)
        _gla_chunk(qr_ref, kr_ref, vr_ref, gr_ref, ob_ref, pl.multiple_of((nch - 1 - c) * GLA_C, GLA_C),
                   sb_ref, ab_ref, bb_ref, m1_ref, head_mask, True)
        return carry

    lax.fori_loop(0, nch, body, 0)
    sfin_ref[0] = sf_ref[...]
    sfin_ref[1] = sb_ref[...]


def _gla_tables():
    fblk, bblk, first, init, seq = [], [], [], [], []
    for b in range(BATCH):
        fblk.append(b), bblk.append(b), first.append(1), init.append(0), seq.append(b)
    for s in range(DEC_BATCH):
        for j in range(LBB):
            fblk.append(NCB + s * LBB + j)
            bblk.append(NCB + s * LBB + LBB - 1 - j)
            first.append(1 if j == 0 else 0)
            init.append(1 + s)
            seq.append(BATCH + s)
    return [jnp.asarray(np.array(t, np.int32)) for t in (fblk, bblk, first, init, seq)]


def _gla(a_proj, glog, s0_all, m1):
    tables = _gla_tables()
    nsteps = int(tables[0].shape[0])
    nseq = BATCH + DEC_BATCH

    def fmap(col):
        return lambda i, fb, bb, fi, ini, sq: (fb[i], col)

    def rmap(col):
        return lambda i, fb, bb, fi, ini, sq: (bb[i], col)

    gs = pltpu.PrefetchScalarGridSpec(
        num_scalar_prefetch=5,
        grid=(nsteps,),
        in_specs=[
            pl.BlockSpec((TM, A_QK), fmap(0)), pl.BlockSpec((TM, A_QK), fmap(1)),
            pl.BlockSpec((TM, A_V), fmap(1)), pl.BlockSpec((TM, A_QK), fmap(0)),
            pl.BlockSpec((TM, A_QK), rmap(0)), pl.BlockSpec((TM, A_QK), rmap(1)),
            pl.BlockSpec((TM, A_V), rmap(1)), pl.BlockSpec((TM, A_QK), rmap(1)),
            pl.BlockSpec((None, 2, A_QK, DV_A), lambda i, fb, bb, fi, ini, sq: (ini[i], 0, 0, 0)),
            pl.BlockSpec((A_QK, A_V), lambda i, fb, bb, fi, ini, sq: (0, 0)),
        ],
        out_specs=[
            pl.BlockSpec((TM, A_V), fmap(0)), pl.BlockSpec((TM, A_V), rmap(0)),
            pl.BlockSpec((None, 2, A_QK, DV_A), lambda i, fb, bb, fi, ini, sq: (sq[i], 0, 0, 0)),
        ],
        scratch_shapes=[
            pltpu.VMEM((A_QK, DV_A), F32), pltpu.VMEM((A_QK, DV_A), F32),
            pltpu.VMEM((GLA_C * GLA_SB, A_QK), BF16), pltpu.VMEM((GLA_C * GLA_SB, A_QK), BF16),
            pltpu.VMEM((GLA_C, A_QK), F32), pltpu.VMEM((GLA_C, A_QK), F32),
        ],
    )
    return pl.pallas_call(
        _gla_kernel,
        grid_spec=gs,
        out_shape=[
            jax.ShapeDtypeStruct((M_TOK, A_V), F32), jax.ShapeDtypeStruct((M_TOK, A_V), F32),
            jax.ShapeDtypeStruct((nseq, 2, A_QK, DV_A), F32),
        ],
        compiler_params=_cparams(("arbitrary",)),
        name="gla",
    )(*tables, a_proj, a_proj, a_proj, glog, a_proj, a_proj, a_proj, glog, s0_all, m1)


def _softmax_parts(s):
    e = jnp.exp(s - jnp.max(s, axis=-1, keepdims=True))
    return e, 1.0 / jnp.sum(e, axis=-1, keepdims=True)


def _diffattn_body(q_ref, k_ref, v_ref, lq1_ref, lk1_ref, lq2_ref, lk2_ref, gsub_ref, o_ref, lam_init):
    lam = (jnp.exp(jnp.sum(lq1_ref[...] * lk1_ref[...], axis=-1, keepdims=True))
           - jnp.exp(jnp.sum(lq2_ref[...] * lk2_ref[...], axis=-1, keepdims=True)) + lam_init)
    lane = lax.broadcasted_iota(jnp.int32, (q_ref.shape[0], LANES), 1)
    for h in range(H_B):
        hs = slice(h * LANES, (h + 1) * LANES)
        q_h = q_ref[:, hs]
        k_h = k_ref[:, hs]
        zero = jnp.zeros_like(q_h)
        e1, r1 = _softmax_parts(_dot_nt(jnp.where(lane < DH_B, q_h, zero), k_h))
        e2, r2 = _softmax_parts(_dot_nt(jnp.where(lane >= DH_B, q_h, zero), k_h))
        w = e1 * r1 - e2 * (lam * r2)
        o = _dot(w.astype(BF16), v_ref[:, hs])
        o_ref[:, hs] = (_rms(o, gsub_ref[...]) * (1.0 - lam_init)).astype(BF16)


def _diffattn_kernel(q_ref, k_ref, v_ref, lq1, lk1, lq2, lk2, gsub, o_ref, *, lam_init):
    _diffattn_body(q_ref, k_ref, v_ref, lq1, lk1, lq2, lk2, gsub, o_ref, lam_init)


def _diffattn(qb, kb, vb, k_lat, v_lat, lam_params, gsub, lam_init):
    small = [pl.BlockSpec((1, DH_B), lambda *_: (0, 0))] * 4 + [pl.BlockSpec((1, DV_B), lambda *_: (0, 0))]
    body = functools.partial(_diffattn_kernel, lam_init=lam_init)
    ctx = pl.pallas_call(
        body,
        grid=(NCB,),
        in_specs=[_row_spec(B_QK), _row_spec(B_QK), _row_spec(B_V)] + small,
        out_specs=_row_spec(B_V),
        out_shape=jax.ShapeDtypeStruct((N_CTX, B_V), BF16),
        compiler_params=_cparams(("arbitrary",)),
        name="diffattn_ctx",
    )(qb, kb, vb, *lam_params, gsub)
    nk = DEC_SEQ + PAST_LEN
    nqb = DEC_SEQ // TQ_LAT
    off = N_CTX // TQ_LAT
    lat = pl.pallas_call(
        body,
        grid=(DEC_BATCH, nqb),
        in_specs=[
            pl.BlockSpec((TQ_LAT, B_QK), lambda b, j: (off + b * nqb + j, 0)),
            pl.BlockSpec((None, nk, B_QK), lambda b, j: (b, 0, 0)),
            pl.BlockSpec((None, nk, B_V), lambda b, j: (b, 0, 0)),
        ] + small,
        out_specs=pl.BlockSpec((TQ_LAT, B_V), lambda b, j: (b * nqb + j, 0)),
        out_shape=jax.ShapeDtypeStruct((N_LAT, B_V), BF16),
        compiler_params=_cparams(("arbitrary", "arbitrary")),
        name="diffattn_lat",
    )(qb, k_lat, v_lat, *lam_params, gsub)
    return ctx, lat


def _pick_rows(ctx_ref, lat_ref):
    return jnp.where(pl.program_id(0) >= NCB, lat_ref[...], ctx_ref[...])


def _postmix_even_kernel(of_ref, ob_ref, ra_ref, actx_ref, alat_ref, x_ref, g1_ref, ggla_ref, wo_ref, o_ref):
    heads = []
    for h in range(H_A):
        hs = slice(h * DV_A, (h + 1) * DV_A)
        heads.append((_rms(of_ref[:, hs] + ob_ref[:, hs], ggla_ref[...]) * _silu(ra_ref[:, hs])).astype(BF16))
    mix = jnp.concatenate(heads + [_pick_rows(actx_ref, alat_ref)], axis=1)
    o_ref[...] = x_ref[...] + g1_ref[...] * _dot(mix, wo_ref[...])


def _postmix_even(o_f, o_b, a_proj, attn_ctx, attn_lat, x, mod, g_gla, w_out):
    return pl.pallas_call(
        _postmix_even_kernel,
        grid=(NB,),
        in_specs=[_row_spec(A_V), _row_spec(A_V), _row_spec(A_V, 2), _ctx_spec(B_V), _lat_spec(B_V),
                  _row_spec(D_MODEL), _mod_spec(2), _const_spec((1, DV_A)), _const_spec((A_V + B_V, D_MODEL))],
        out_specs=_row_spec(D_MODEL),
        out_shape=jax.ShapeDtypeStruct((M_TOK, D_MODEL), F32),
        compiler_params=_cparams(("arbitrary",)),
        name="postmix_even",
    )(o_f, o_b, a_proj, attn_ctx, attn_lat, x, mod, g_gla, w_out)


def _ffn_even_kernel(x_ref, gam_ref, sh_ref, sc_ref, g2_ref, wg_ref, wu_ref, wd_ref, o_ref):
    x = x_ref[...]
    hb = _norm_mod(x, gam_ref[...], sc_ref[...], sh_ref[...]).astype(BF16)
    acc = None
    nsub = 2
    w = D_FF // nsub
    for c in range(nsub):
        cs = slice(c * w, (c + 1) * w)
        act = (_silu(_dot(hb, wg_ref[:, cs])) * _dot(hb, wu_ref[:, cs])).astype(BF16)
        y = _dot(act, wd_ref[cs, :])
        acc = y if acc is None else acc + y
    o_ref[...] = x + g2_ref[...] * acc


def _ffn_even(x, gamma, mod, w_gate, w_up, w_down):
    return pl.pallas_call(
        _ffn_even_kernel,
        grid=(NB,),
        in_specs=[_row_spec(D_MODEL), _const_spec((1, D_MODEL)), _mod_spec(3), _mod_spec(4), _mod_spec(5),
                  _const_spec((D_MODEL, D_FF)), _const_spec((D_MODEL, D_FF)), _const_spec((D_FF, D_MODEL))],
        out_specs=_row_spec(D_MODEL),
        out_shape=jax.ShapeDtypeStruct((M_TOK, D_MODEL), F32),
        compiler_params=_cparams(("arbitrary",)),
        name="ffn_even",
    )(x, gamma, mod, mod, mod, w_gate, w_up, w_down)


def _inproj_odd_kernel(x_ref, gam_ref, sh_ref, sc_ref, w_ref, gq_ref, gk_ref, cos_ref, sin_ref,
                       q_ref, k_ref, v_ref, kctx_ref, vctx_ref):
    is_lat = pl.program_id(0) >= NCB
    is_ctx = jnp.logical_not(is_lat)
    hb = _norm_mod(x_ref[...], gam_ref[...], sc_ref[...], sh_ref[...]).astype(BF16)
    z = _dot(hb, w_ref[...])
    cos = jnp.where(is_lat, cos_ref[...], 1.0)
    sin = jnp.where(is_lat, sin_ref[...], 0.0)
    scale = DH_C ** -0.5
    for h in range(H_C):
        hs = slice(h * DH_C, (h + 1) * DH_C)
        q_ref[:, hs] = (_rope(_rms(z[:, hs], gq_ref[...]), cos, sin, DH_C // 4) * scale).astype(BF16)
    nq = H_C * DH_C
    for h in range(HKV_C):
        hs = slice(h * DH_C, (h + 1) * DH_C)
        k_n = _rms(z[:, nq + h * DH_C:nq + (h + 1) * DH_C], gk_ref[...])

        @pl.when(is_ctx)
        def _(k_n=k_n, hs=hs):
            kctx_ref[:, hs] = k_n

        k_ref[:, hs] = _rope(k_n, cos, sin, DH_C // 4).astype(BF16)
    nkv = HKV_C * DH_C
    v = z[:, nq + nkv:]

    @pl.when(is_ctx)
    def _():
        vctx_ref[...] = v

    v_ref[...] = v.astype(BF16)


def _inproj_odd(x, gamma, mod, w_in, g_q, g_k, cos, sin):
    nq, nkv = H_C * DH_C, HKV_C * DH_C
    return pl.pallas_call(
        _inproj_odd_kernel,
        grid=(NB,),
        in_specs=[
            _row_spec(D_MODEL), _const_spec((1, D_MODEL)), _mod_spec(0), _mod_spec(1),
            _const_spec((D_MODEL, nq + 2 * nkv)), _const_spec((1, DH_C)), _const_spec((1, DH_C)),
            pl.BlockSpec((TM, LANES), lambda i: (_pos_block(i), 0)),
            pl.BlockSpec((TM, LANES), lambda i: (_pos_block(i), 0)),
        ],
        out_specs=[_row_spec(nq), _row_spec(nkv), _row_spec(nkv), _ctx_spec(nkv), _ctx_spec(nkv)],
        out_shape=[
            jax.ShapeDtypeStruct((M_TOK, nq), BF16), jax.ShapeDtypeStruct((M_TOK, nkv), BF16),
            jax.ShapeDtypeStruct((M_TOK, nkv), BF16), jax.ShapeDtypeStruct((N_CTX, nkv), F32),
            jax.ShapeDtypeStruct((N_CTX, nkv), F32),
        ],
        compiler_params=_cparams(("arbitrary",)),
        name="inproj_odd",
    )(x, gamma, mod, mod, w_in, g_q, g_k, cos, sin)


def _gqa_body(q_ref, k_ref, v_ref, o_ref):
    rep = H_C // HKV_C
    tq = q_ref.shape[0]
    for hk in range(HKV_C):
        ks = slice(hk * DH_C, (hk + 1) * DH_C)
        q_g = jnp.concatenate([q_ref[:, (hk * rep + g) * DH_C:(hk * rep + g + 1) * DH_C] for g in range(rep)],
                              axis=0)
        e, r = _softmax_parts(_dot_nt(q_g, k_ref[:, ks]))
        o = _dot((e * r).astype(BF16), v_ref[:, ks])
        for g in range(rep):
            o_ref[:, (hk * rep + g) * DH_C:(hk * rep + g + 1) * DH_C] = o[g * tq:(g + 1) * tq].astype(BF16)


def _gqa(q, k, v, k_lat, v_lat):
    nq, nkv = H_C * DH_C, HKV_C * DH_C
    ctx = pl.pallas_call(
        _gqa_body,
        grid=(NCB,),
        in_specs=[_row_spec(nq), _row_spec(nkv), _row_spec(nkv)],
        out_specs=_row_spec(nq),
        out_shape=jax.ShapeDtypeStruct((N_CTX, nq), BF16),
        compiler_params=_cparams(("arbitrary",)),
        name="gqa_ctx",
    )(q, k, v)
    nk = DEC_SEQ + PAST_LEN
    nqb = DEC_SEQ // TQ_LAT
    off = N_CTX // TQ_LAT
    lat = pl.pallas_call(
        _gqa_body,
        grid=(DEC_BATCH, nqb),
        in_specs=[
            pl.BlockSpec((TQ_LAT, nq), lambda b, j: (off + b * nqb + j, 0)),
            pl.BlockSpec((None, nk, nkv), lambda b, j: (b, 0, 0)),
            pl.BlockSpec((None, nk, nkv), lambda b, j: (b, 0, 0)),
        ],
        out_specs=pl.BlockSpec((TQ_LAT, nq), lambda b, j: (b * nqb + j, 0)),
        out_shape=jax.ShapeDtypeStruct((N_LAT, nq), BF16),
        compiler_params=_cparams(("arbitrary", "arbitrary")),
        name="gqa_lat",
    )(q, k_lat, v_lat)
    return ctx, lat


def _postmix_odd_kernel(actx_ref, alat_ref, x_ref, g1_ref, wo_ref, gam_ref, sh_ref, sc_ref, wr_ref,
                        x1_ref, h_ref, w1_ref, w2_ref, idx_ref, cnt_ref, run_ref):
    @pl.when(pl.program_id(0) == 0)
    def _():
        run_ref[...] = jnp.zeros_like(run_ref)

    x1 = x_ref[...] + g1_ref[...] * _dot(_pick_rows(actx_ref, alat_ref), wo_ref[...])
    x1_ref[...] = x1
    h = _norm_mod(x1, gam_ref[...], sc_ref[...], sh_ref[...])
    h_ref[...] = h
    logits = _dot_hi(h, wr_ref[...])
    lane = lax.broadcasted_iota(jnp.int32, logits.shape, 1)
    lane_f = lane.astype(F32)
    lg = jnp.where(lane < N_EXPERTS, logits, -jnp.inf)
    m1 = jnp.max(lg, axis=-1, keepdims=True)
    i1 = jnp.min(jnp.where(lg == m1, lane_f, float(LANES)), axis=-1, keepdims=True)
    lg2 = jnp.where(lane_f == i1, -jnp.inf, lg)
    m2 = jnp.max(lg2, axis=-1, keepdims=True)
    i2 = jnp.min(jnp.where(lg2 == m2, lane_f, float(LANES)), axis=-1, keepdims=True)
    e = jnp.exp(m2 - m1)
    w1 = 1.0 / (1.0 + e)
    w1_ref[...] = jnp.broadcast_to(w1, logits.shape)
    w2_ref[...] = jnp.broadcast_to(e * w1, logits.shape)
    oh1 = jnp.where(lane_f == i1, 1.0, 0.0)
    oh2 = jnp.where(lane_f == i2, 1.0, 0.0)
    both = oh1 + oh2
    r_i = lax.broadcasted_iota(jnp.int32, (TM, TM), 0)
    c_i = lax.broadcasted_iota(jnp.int32, (TM, TM), 1)
    before = jnp.where(c_i < r_i, 1.0, 0.0).astype(BF16)
    pref = _dot(before, both.astype(BF16)) + run_ref[0:1, :]
    rank1 = jnp.sum(pref * oh1, axis=-1, keepdims=True)
    rank2 = jnp.sum(pref * oh2, axis=-1, keepdims=True)
    run_ref[...] = run_ref[...] + jnp.sum(both, axis=0, keepdims=True)
    cnt_ref[...] = run_ref[...]
    quarter = LANES // 4
    idx_ref[...] = jnp.where(lane < quarter, i1, jnp.where(lane < 2 * quarter, i2, jnp.where(
        lane < 3 * quarter, rank1, rank2))).astype(jnp.int32)


def _postmix_odd(attn_ctx, attn_lat, x, mod, w_out, gamma, w_router):
    return pl.pallas_call(
        _postmix_odd_kernel,
        grid=(NB,),
        in_specs=[_ctx_spec(H_C * DH_C), _lat_spec(H_C * DH_C), _row_spec(D_MODEL), _mod_spec(2),
                  _const_spec((H_C * DH_C, D_MODEL)),
                  _const_spec((1, D_MODEL)), _mod_spec(3), _mod_spec(4), _const_spec((D_MODEL, LANES))],
        out_specs=[_row_spec(D_MODEL), _row_spec(D_MODEL), _row_spec(LANES), _row_spec(LANES), _row_spec(LANES),
                   _const_spec((8, LANES))],
        out_shape=[
            jax.ShapeDtypeStruct((M_TOK, D_MODEL), F32), jax.ShapeDtypeStruct((M_TOK, D_MODEL), F32),
            jax.ShapeDtypeStruct((M_TOK, LANES), F32), jax.ShapeDtypeStruct((M_TOK, LANES), F32),
            jax.ShapeDtypeStruct((M_TOK, LANES), jnp.int32), jax.ShapeDtypeStruct((8, LANES), F32),
        ],
        scratch_shapes=[pltpu.VMEM((8, LANES), F32)],
        compiler_params=_cparams(("arbitrary",)),
        name="postmix_odd",
    )(attn_ctx, attn_lat, x, mod, w_out, gamma, mod, mod, w_router)


def _cast_rows(src_ref, dst_ref, piece):
    for r in range(0, src_ref.shape[0], piece):
        dst_ref[r:r + piece, :] = src_ref[r:r + piece, :].astype(BF16)


def _experts_up_kernel(tile_ref, chunk_ref, exp_ref, first_ref, valid_ref, x_ref, wg_ref, wu_ref, act_ref,
                       wgb_ref, wub_ref):
    s = pl.program_id(0)

    @pl.when(first_ref[s] == 1)
    def _():
        _cast_rows(wg_ref, wgb_ref, 256)
        _cast_rows(wu_ref, wub_ref, 256)

    @pl.when(valid_ref[s] == 1)
    def _():
        x = x_ref[...].astype(BF16)
        act_ref[...] = (_silu(_dot(x, wgb_ref[...])) * _dot(x, wub_ref[...])).astype(BF16)


def _experts_down_kernel(act_tile_ref, exp_ref, first_ref, valid_ref, act_ref, wd_ref, o_ref, wdb_ref):
    i = pl.program_id(0)

    @pl.when(first_ref[i] == 1)
    def _():
        _cast_rows(wd_ref, wdb_ref, 512)

    @pl.when(valid_ref[i] == 1)
    def _():
        o_ref[...] = _dot(act_ref[...], wdb_ref[...])

    @pl.when(valid_ref[i] == 0)
    def _():
        o_ref[...] = jnp.zeros_like(o_ref)


def _experts(up_tables, down_tables, xs, w_gate, w_up, w_down):
    up = pl.pallas_call(
        _experts_up_kernel,
        grid_spec=pltpu.PrefetchScalarGridSpec(
            num_scalar_prefetch=5,
            grid=(NC_E * NT_E,),
            in_specs=[
                pl.BlockSpec((TME, D_MODEL), lambda s, t, c, e, f, v: (t[s], 0)),
                pl.BlockSpec((None, D_MODEL, TF_E), lambda s, t, c, e, f, v: (e[s], 0, c[s])),
                pl.BlockSpec((None, D_MODEL, TF_E), lambda s, t, c, e, f, v: (e[s], 0, c[s])),
            ],
            out_specs=pl.BlockSpec((TME, TF_E), lambda s, t, c, e, f, v: (t[s], c[s])),
            scratch_shapes=[pltpu.VMEM((D_MODEL, TF_E), BF16), pltpu.VMEM((D_MODEL, TF_E), BF16)],
        ),
        out_shape=jax.ShapeDtypeStruct((NT_E * TME, D_FF_E), BF16),
        compiler_params=_cparams(("arbitrary",)),
        name="experts_up",
    )(*up_tables, xs, w_gate, w_up)
    return pl.pallas_call(
        _experts_down_kernel,
        grid_spec=pltpu.PrefetchScalarGridSpec(
            num_scalar_prefetch=4,
            grid=(NT_E,),
            in_specs=[
                pl.BlockSpec((TME, D_FF_E), lambda i, a, e, f, v: (a[i], 0)),
                pl.BlockSpec((None, D_FF_E, D_MODEL), lambda i, a, e, f, v: (e[i], 0, 0)),
            ],
            out_specs=pl.BlockSpec((TME, D_MODEL), lambda i, a, e, f, v: (i, 0)),
            scratch_shapes=[pltpu.VMEM((D_FF_E, D_MODEL), BF16)],
        ),
        out_shape=jax.ShapeDtypeStruct((NT_E * TME, D_MODEL), F32),
        compiler_params=_cparams(("arbitrary",)),
        name="experts_down",
    )(*down_tables, up, w_down)


def _combine_kernel(x_ref, y1_ref, y2_ref, w1_ref, w2_ref, g2_ref, gam_ref, octx_ref, olat_ref):
    rep = D_MODEL // LANES
    w1 = jnp.concatenate([w1_ref[...]] * rep, axis=1)
    w2 = jnp.concatenate([w2_ref[...]] * rep, axis=1)
    x2 = x_ref[...] + g2_ref[...] * (w1 * y1_ref[...] + w2 * y2_ref[...])
    y = _rms(x2, gam_ref[...])
    is_lat = pl.program_id(0) >= NCB

    @pl.when(jnp.logical_not(is_lat))
    def _():
        octx_ref[...] = y

    @pl.when(is_lat)
    def _():
        olat_ref[...] = y


def _combine(x1, y1, y2, w1, w2, mod, gamma):
    return pl.pallas_call(
        _combine_kernel,
        grid=(NB,),
        in_specs=[_row_spec(D_MODEL), _row_spec(D_MODEL), _row_spec(D_MODEL), _row_spec(LANES), _row_spec(LANES),
                  _mod_spec(5), _const_spec((1, D_MODEL))],
        out_specs=[_ctx_spec(D_MODEL), _lat_spec(D_MODEL)],
        out_shape=[jax.ShapeDtypeStruct((N_CTX, D_MODEL), F32), jax.ShapeDtypeStruct((N_LAT, D_MODEL), F32)],
        compiler_params=_cparams(("arbitrary",)),
        name="combine",
    )(x1, y1, y2, w1, w2, mod, gamma)


def _rope_tables(rot_dim):
    n = DEC_SEQ
    rows = jnp.repeat(jnp.arange(n // GRID_W, dtype=F32), GRID_W)
    cols = jnp.tile(jnp.arange(GRID_W, dtype=F32), n // GRID_W)
    half = rot_dim // 2
    freqs = ROPE_THETA ** (-jnp.arange(0, half, 2, dtype=F32) / half)
    ang_r, ang_c = rows[:, None] * freqs, cols[:, None] * freqs
    cos = jnp.concatenate([jnp.cos(ang_r)] * 2 + [jnp.cos(ang_c)] * 2, axis=-1)
    sin = jnp.concatenate([-jnp.sin(ang_r), jnp.sin(ang_r), -jnp.sin(ang_c), jnp.sin(ang_c)], axis=-1)
    rep = LANES // rot_dim
    return jnp.tile(cos, (1, rep)), jnp.tile(sin, (1, rep))


def _rows(a, idx):
    return a.at[idx].get(mode="promise_in_bounds")


def _route(idx, counts_f):
    quarter = LANES // 4
    experts = jnp.arange(N_EXPERTS, dtype=jnp.int32)
    e2 = jnp.stack([idx[:, 0], idx[:, quarter]], axis=1)
    rank2 = jnp.stack([idx[:, 2 * quarter], idx[:, 3 * quarter]], axis=1)
    counts = counts_f[0, :N_EXPERTS].astype(jnp.int32)
    tiles = (counts + TME - 1) // TME
    tile_end = jnp.cumsum(tiles)
    tile_start = tile_end - tiles
    start = jnp.cumsum(counts) - counts
    onehot = e2[:, :, None] == experts[None, None, :]
    pos = jnp.sum(jnp.where(onehot, tile_start[None, None, :] * TME, 0), axis=-1) + rank2
    tile_id = jnp.arange(NT_E, dtype=jnp.int32)
    tile_expert = jnp.minimum(jnp.sum((tile_id[:, None] >= tile_end[None, :]).astype(jnp.int32), axis=1),
                              N_EXPERTS - 1).astype(jnp.int32)
    tile_valid = (tile_id < tile_end[-1]).astype(jnp.int32)
    t_oh = tile_expert[:, None] == experts[None, :]
    t_first = jnp.sum(jnp.where(t_oh, (start - tile_start * TME)[None, :], 0), axis=1) + tile_id * TME
    t_last = jnp.sum(jnp.where(t_oh, (start + counts)[None, :], 0), axis=1)
    order = jnp.argsort(e2.reshape(-1), stable=True).astype(jnp.int32)
    n_assign = order.shape[0]
    g_idx = t_first[:, None] + jnp.arange(TME, dtype=jnp.int32)[None, :]
    live = jnp.logical_and(g_idx < t_last[:, None], tile_valid[:, None] == 1)
    src = _rows(order, jnp.clip(g_idx, 0, n_assign - 1).reshape(-1)) // 2
    src_tok = jnp.where(live.reshape(-1), src, 0).astype(jnp.int32)

    n_tiles = tile_end[-1]
    last_tile = jnp.maximum(n_tiles - 1, 0)
    last_expert = jnp.sum(jnp.where(tile_id == last_tile, tile_expert, 0))
    t_tstart = jnp.sum(jnp.where(t_oh, tile_start[None, :], 0), axis=1)
    down_tables = (
        jnp.where(tile_valid == 1, tile_id, last_tile).astype(jnp.int32),
        tile_expert,
        jnp.logical_and(tile_valid == 1, tile_id == t_tstart).astype(jnp.int32),
        tile_valid,
    )
    s_id = jnp.arange(NC_E * NT_E, dtype=jnp.int32)
    s_exp = jnp.minimum(jnp.sum((s_id[:, None] >= NC_E * tile_end[None, :]).astype(jnp.int32), axis=1),
                        N_EXPERTS - 1)
    s_oh = s_exp[:, None] == experts[None, :]
    s_tstart = jnp.sum(jnp.where(s_oh, tile_start[None, :], 0), axis=1)
    s_ntile = jnp.maximum(jnp.sum(jnp.where(s_oh, tiles[None, :], 0), axis=1), 1)
    rel = s_id - NC_E * s_tstart
    s_valid = s_id < NC_E * n_tiles
    up_tables = (
        jnp.where(s_valid, s_tstart + rel % s_ntile, last_tile).astype(jnp.int32),
        jnp.where(s_valid, rel // s_ntile, NC_E - 1).astype(jnp.int32),
        jnp.where(s_valid, s_exp, last_expert).astype(jnp.int32),
        jnp.logical_and(s_valid, rel % s_ntile == 0).astype(jnp.int32),
        s_valid.astype(jnp.int32),
    )
    return src_tok, pos, up_tables, down_tables


def lambda_init(layer):
    return 0.8 - 0.6 * math.exp(-0.3 * layer)


def kernel(x_prompt, x_sample, state_a, cache_b_k, cache_b_v, cache_c_k, cache_c_v, c, c_ctx, w_mod, b_mod, norm_mix, norm_ffn, w_in_even, w_gate2_a, b_gate_a, g_gla, lam_q1, lam_k1, lam_q2, lam_k2, g_sub_b, w_out_even, w_in_odd, g_q_c, g_k_c, w_out_odd, ffn_gate, ffn_up, ffn_down, w_router, exp_gate, exp_up, exp_down, norm_final):
    x = jnp.concatenate([x_prompt.reshape(N_CTX, D_MODEL), x_sample.reshape(N_LAT, D_MODEL)], axis=0)
    cond = jnp.concatenate([c_ctx[None, :], c, jnp.zeros((N_COND - 1 - DEC_BATCH, D_MODEL), F32)], axis=0)
    mod = _modulation(cond, w_mod, b_mod).reshape(DEPTH, N_COND, 1, 6 * D_MODEL)

    w = w_in_even[0]
    na = 2 * A_QK + 2 * A_V
    gate_lo = na
    gate_hi = na + 2 * GATE_RANK
    col_scale = jnp.concatenate([jnp.full((A_QK,), DK_A ** -0.5, F32), jnp.ones((na - A_QK,), F32),
                                 jnp.full((B_QK,), DH_B ** -0.5, F32), jnp.ones((B_QK + B_V,), F32)])
    w_main = (jnp.concatenate([w[:, :gate_lo], w[:, gate_hi:]], axis=1) * col_scale).astype(BF16)
    w_gates = jnp.pad(w[:, gate_lo:gate_hi], ((0, 0), (0, LANES - 2 * GATE_RANK))).astype(BF16)
    g2 = jnp.zeros((LANES, 2 * A_QK), F32)
    g2 = g2.at[:GATE_RANK, :A_QK].set(w_gate2_a[0, 0]).at[GATE_RANK:2 * GATE_RANK, A_QK:].set(w_gate2_a[0, 1])
    bg = b_gate_a[0].reshape(1, 2 * A_QK)
    cos_b, sin_b = _rope_tables(DH_B)
    a_proj, glog, qb, kb, vb, kb_ctx, vb_ctx = _inproj_even(
        x, norm_mix[0][None, :], mod[0], w_main, w_gates, g2.astype(BF16), bg, cos_b, sin_b)

    s0_all = jnp.concatenate([jnp.zeros((1, 2, A_QK, DV_A), F32),
                              state_a[:, 0].reshape(DEC_BATCH, 2, A_QK, DV_A)], axis=0)
    hd = np.arange(A_QK)[:, None] // DK_A == np.arange(A_V)[None, :] // DV_A
    m1 = jnp.asarray(hd.astype(np.float32)).astype(BF16)
    o_f, o_b, s_fin = _gla(a_proj, glog, s0_all, m1)

    k_lat = jnp.concatenate([kb[N_CTX:].reshape(DEC_BATCH, DEC_SEQ, B_QK),
                             cache_b_k[:, 0].reshape(DEC_BATCH, PAST_LEN, B_QK).astype(BF16)], axis=1)
    v_lat = jnp.concatenate([vb[N_CTX:].reshape(DEC_BATCH, DEC_SEQ, B_V),
                             cache_b_v[:, 0].reshape(DEC_BATCH, PAST_LEN, B_V).astype(BF16)], axis=1)
    lam_params = [p[0][None, :] for p in (lam_q1, lam_k1, lam_q2, lam_k2)]
    attn_b = _diffattn(qb, kb, vb, k_lat, v_lat, lam_params, g_sub_b[0][None, :], lambda_init(0))

    x = _postmix_even(o_f, o_b, a_proj, *attn_b, x, mod[0], g_gla[0][None, :], w_out_even[0].astype(BF16))
    x = _ffn_even(x, norm_ffn[0][None, :], mod[0], ffn_gate[0].astype(BF16), ffn_up[0].astype(BF16),
                  ffn_down[0].astype(BF16))

    cos_c, sin_c = _rope_tables(DH_C)
    q_c, k_c, v_c, kc_ctx, vc_ctx = _inproj_odd(x, norm_mix[1][None, :], mod[1], w_in_odd[0].astype(BF16),
                                     g_q_c[0][None, :], g_k_c[0][None, :], cos_c, sin_c)
    nkv = HKV_C * DH_C
    k_lat = jnp.concatenate([k_c[N_CTX:].reshape(DEC_BATCH, DEC_SEQ, nkv),
                             cache_c_k[:, 0].reshape(DEC_BATCH, PAST_LEN, nkv).astype(BF16)], axis=1)
    v_lat = jnp.concatenate([v_c[N_CTX:].reshape(DEC_BATCH, DEC_SEQ, nkv),
                             cache_c_v[:, 0].reshape(DEC_BATCH, PAST_LEN, nkv).astype(BF16)], axis=1)
    attn_c = _gqa(q_c, k_c, v_c, k_lat, v_lat)
    w_r = jnp.pad(w_router[0], ((0, 0), (0, LANES - N_EXPERTS)))
    x1, h_moe, w1, w2, idx, counts = _postmix_odd(*attn_c, x, mod[1], w_out_odd[0].astype(BF16),
                                                  norm_ffn[1][None, :], w_r)

    src_tok, pos, up_tables, down_tables = _route(idx, counts)
    xs = _rows(h_moe, src_tok)
    ys = _experts(up_tables, down_tables, xs, exp_gate[0], exp_up[0], exp_down[0])
    y1 = _rows(ys, pos[:, 0])
    y2 = _rows(ys, pos[:, 1])
    y_ctx, y_lat = _combine(x1, y1, y2, w1, w2, mod[1], norm_final[None, :])

    y_prompt = y_ctx.reshape(BATCH, SEQ, D_MODEL)
    y_sample = y_lat.reshape(DEC_BATCH, DEC_SEQ, D_MODEL)
    new_state_a = s_fin[:BATCH].reshape(BATCH, 1, 2, H_A, DK_A, DV_A)
    new_b_k = kb_ctx.reshape(BATCH, 1, SEQ, H_B, 2, DH_B)
    new_b_v = vb_ctx.reshape(BATCH, 1, SEQ, H_B, DV_B)
    new_c_k = kc_ctx.reshape(BATCH, 1, SEQ, HKV_C, DH_C)
    new_c_v = vc_ctx.reshape(BATCH, 1, SEQ, HKV_C, DH_C)
    return (y_prompt, y_sample, new_state_a, new_b_k, new_b_v, new_c_k, new_c_v)
```

```python
import functools
import math

import numpy as np
import jax
import jax.numpy as jnp
from jax import lax
from jax.experimental import pallas as pl
from jax.experimental.pallas import tpu as pltpu

D_MODEL = 1024
BATCH = 32
SEQ = 256
DEPTH = 2
DEC_BATCH = 2
DEC_SEQ = 2048
PAST_LEN = 512
GRID_W = 64
H_A, DK_A, DV_A = 4, 64, 128
GATE_RANK = 16
GATE_TAU = 16.0
H_B, DH_B = 4, 64
DV_B = 2 * DH_B
H_C, HKV_C, DH_C = 8, 2, 128
D_FF = 2816
N_EXPERTS = 8
D_FF_E = 3584
ROPE_THETA = 10000.0
EPS = 1e-6
A_QK = H_A * DK_A
A_V = H_A * DV_A
B_QK = H_B * 2 * DH_B
B_V = H_B * DV_B

F32 = jnp.float32
BF16 = jnp.bfloat16

V7X_VMEM_BYTES = 64 * 1024 * 1024
VMEM_LIMIT = V7X_VMEM_BYTES * 7 // 8
LANES = 128
TM = 256
N_CTX = BATCH * SEQ
N_LAT = DEC_BATCH * DEC_SEQ
M_TOK = N_CTX + N_LAT
NB = M_TOK // TM
NCB = N_CTX // TM
LBB = DEC_SEQ // TM
N_COND = 8
GLA_C = 64
GLA_SB = 8
TME = 512
NT_E = 2 * M_TOK // TME + N_EXPERTS
NC_E = 2
TF_E = D_FF_E // NC_E
TQ_LAT = 256

assert SEQ == TM and DEC_SEQ % TM == 0 and DEPTH == 2


def _cparams(sem):
    return pltpu.CompilerParams(dimension_semantics=sem, vmem_limit_bytes=VMEM_LIMIT)


def _group(i):
    return jnp.where(i < NCB, 0, 1 + (i - NCB) // LBB)


def _pos_block(i):
    return jnp.where(i < NCB, 0, (i - NCB) % LBB)


def _mod_spec(chunk):
    return pl.BlockSpec((None, 1, D_MODEL), lambda i, c=chunk: (_group(i), 0, c))


def _row_spec(width, col=0):
    return pl.BlockSpec((TM, width), lambda i, c=col: (i, c))


def _ctx_spec(width):
    return pl.BlockSpec((TM, width), lambda i: (jnp.minimum(i, NCB - 1), 0))


def _lat_spec(width):
    return pl.BlockSpec((TM, width), lambda i: (jnp.maximum(i - NCB, 0), 0))


def _const_spec(shape):
    nd = len(shape)
    return pl.BlockSpec(shape, lambda i, nd=nd: (0,) * nd)


def _dot(a, b):
    return jnp.dot(a, b, preferred_element_type=F32)


def _dot_nt(a, b):
    return lax.dot_general(a, b, (((1,), (1,)), ((), ())), preferred_element_type=F32)


def _split2(a):
    hi = a.astype(BF16)
    lo = (a - hi.astype(F32)).astype(BF16)
    return hi, lo


def _dot_hi(a, b):
    a_hi, a_lo = _split2(a)
    b_hi, b_lo = _split2(b)
    return _dot(a_hi, b_hi) + (_dot(a_hi, b_lo) + _dot(a_lo, b_hi))


def _silu(x):
    return x * (1.0 / (1.0 + jnp.exp(-x)))


def _log_sigmoid(x):
    return jnp.minimum(x, 0.0) - jnp.log(1.0 + jnp.exp(-jnp.abs(x)))


def _rms(x, g):
    return x * lax.rsqrt(jnp.mean(x * x, axis=-1, keepdims=True) + EPS) * g


def _norm_mod(x, gamma, sc, sh):
    return _rms(x, gamma) * (1.0 + sc) + sh


def _rope(x, cos, sin, half):
    lane = lax.broadcasted_iota(jnp.int32, x.shape, 1)
    first = (lane % (2 * half)) < half
    swapped = jnp.where(first, pltpu.roll(x, LANES - half, 1), pltpu.roll(x, half, 1))
    return x * cos + swapped * sin


def _mod_kernel(cond_ref, w_ref, b_ref, o_ref):
    o_ref[...] = _dot_hi(_silu(cond_ref[...]), w_ref[...]) + b_ref[...]


def _modulation(cond, w_mod, b_mod):
    tn = 1536
    return pl.pallas_call(
        _mod_kernel,
        grid=(DEPTH, 6 * D_MODEL // tn),
        in_specs=[
            pl.BlockSpec((N_COND, D_MODEL), lambda l, j: (0, 0)),
            pl.BlockSpec((None, D_MODEL, tn), lambda l, j: (l, 0, j)),
            pl.BlockSpec((None, 1, tn), lambda l, j: (l, 0, j)),
        ],
        out_specs=pl.BlockSpec((None, N_COND, tn), lambda l, j: (l, 0, j)),
        out_shape=jax.ShapeDtypeStruct((DEPTH, N_COND, 6 * D_MODEL), F32),
        compiler_params=_cparams(("arbitrary", "arbitrary")),
        name="modulation",
    )(cond, w_mod, b_mod.reshape(DEPTH, 1, 6 * D_MODEL))


def _inproj_even_kernel(x_ref, gam_ref, sh_ref, sc_ref, w_ref, wg_ref, g2_ref, bg_ref, cos_ref, sin_ref,
                        a_ref, glog_ref, qb_ref, kb_ref, vb_ref, kctx_ref, vctx_ref):
    is_lat = pl.program_id(0) >= NCB
    hb = _norm_mod(x_ref[...], gam_ref[...], sc_ref[...], sh_ref[...]).astype(BF16)
    z = _dot(hb, w_ref[...])
    na = 2 * A_QK + 2 * A_V
    a_ref[...] = z[:, :na]
    gates = _dot(hb, wg_ref[...])
    xg = _dot(gates.astype(BF16), g2_ref[...]) + bg_ref[...]
    glog_ref[...] = _log_sigmoid(xg) * (1.0 / GATE_TAU)
    cos = jnp.where(is_lat, cos_ref[...], 1.0)
    sin = jnp.where(is_lat, sin_ref[...], 0.0)

    @pl.when(jnp.logical_not(is_lat))
    def _():
        kctx_ref[...] = z[:, na + B_QK:na + 2 * B_QK]
        vctx_ref[...] = z[:, na + 2 * B_QK:]

    for j in range(B_QK // LANES):
        lo = na + j * LANES
        qb_ref[:, j * LANES:(j + 1) * LANES] = _rope(z[:, lo:lo + LANES], cos, sin, DH_B // 4).astype(BF16)
        lo = na + B_QK + j * LANES
        kb_ref[:, j * LANES:(j + 1) * LANES] = _rope(z[:, lo:lo + LANES], cos, sin, DH_B // 4).astype(BF16)
    vb_ref[...] = z[:, na + 2 * B_QK:].astype(BF16)


def _inproj_even(x, gamma, mod, w_main, w_gates, g2, bg, cos, sin):
    na = 2 * A_QK + 2 * A_V
    nz = na + 2 * B_QK + B_V
    return pl.pallas_call(
        _inproj_even_kernel,
        grid=(NB,),
        in_specs=[
            _row_spec(D_MODEL), _const_spec((1, D_MODEL)), _mod_spec(0), _mod_spec(1),
            _const_spec((D_MODEL, nz)), _const_spec((D_MODEL, LANES)), _const_spec((LANES, 2 * A_QK)),
            _const_spec((1, 2 * A_QK)),
            pl.BlockSpec((TM, LANES), lambda i: (_pos_block(i), 0)),
            pl.BlockSpec((TM, LANES), lambda i: (_pos_block(i), 0)),
        ],
        out_specs=[_row_spec(na), _row_spec(2 * A_QK), _row_spec(B_QK), _row_spec(B_QK), _row_spec(B_V),
                   _ctx_spec(B_QK), _ctx_spec(B_V)],
        out_shape=[
            jax.ShapeDtypeStruct((M_TOK, na), F32), jax.ShapeDtypeStruct((M_TOK, 2 * A_QK), F32),
            jax.ShapeDtypeStruct((M_TOK, B_QK), BF16), jax.ShapeDtypeStruct((M_TOK, B_QK), BF16),
            jax.ShapeDtypeStruct((M_TOK, B_V), BF16), jax.ShapeDtypeStruct((N_CTX, B_QK), F32),
            jax.ShapeDtypeStruct((N_CTX, B_V), F32),
        ],
        compiler_params=_cparams(("arbitrary",)),
        name="inproj_even",
    )(x, gamma, mod, mod, w_main, w_gates, g2, bg, cos, sin)


def _gla_chunk(q_ref, k_ref, v_ref, g_ref, o_ref, row0, s_ref, a_ref, b_ref, m1_ref, head_mask, same_block, rev):
    rows = pl.ds(row0, GLA_C)
    q, k, v, g = q_ref[rows, :], k_ref[rows, :], v_ref[rows, :], g_ref[rows, :]
    v_b = v.astype(BF16)
    r_i = lax.broadcasted_iota(jnp.int32, (GLA_C, GLA_C), 0)
    c_i = lax.broadcasted_iota(jnp.int32, (GLA_C, GLA_C), 1)
    tri = jnp.where((c_i >= r_i) if rev else (c_i <= r_i), 1.0, 0.0).astype(BF16)
    g_hi = g.astype(BF16)
    g_r = g - g_hi.astype(F32)
    g_mid = g_r.astype(BF16)
    g_lo = (g_r - g_mid.astype(F32)).astype(BF16)
    bc = _dot(tri, g_hi) + (_dot(tri, g_mid) + _dot(tri, g_lo))
    b_ref[...] = bc
    last = bc[0:1] if rev else bc[GLA_C - 1:GLA_C]
    qc = q * jnp.exp(bc)
    kl = k * jnp.exp(last - bc)
    row = lax.broadcasted_iota(jnp.int32, (GLA_C, A_QK), 0)

    def expand(x):
        return jnp.where(head_mask, jnp.concatenate([x] * H_A, axis=0), 0.0).astype(BF16)

    scores = None
    half = GLA_C // 2
    while half >= GLA_SB:
        mids = [bc[b + half:b + half + 1] if rev else bc[b + half - 1:b + half] for b in range(0, GLA_C, 2 * half)]
        ref = mids[0] if len(mids) == 1 else jnp.concatenate(
            [jnp.broadcast_to(m, (2 * half, A_QK)) for m in mids], axis=0)
        upper = (row // half) % 2 == 1
        later, earlier = (jnp.logical_not(upper), upper) if rev else (upper, jnp.logical_not(upper))
        q_l = jnp.where(later, q * jnp.exp(bc - ref), 0.0)
        k_e = jnp.where(earlier, k * jnp.exp(ref - bc), 0.0).astype(BF16)
        term = _dot_nt(expand(q_l), k_e)
        if 2 * half < GLA_C:
            term = jnp.where(same_block[2 * half], term, 0.0)
        scores = term if scores is None else scores + term
        half //= 2
    s_old = s_ref[...]
    inter = _dot(expand(qc), s_old.astype(BF16))
    o_heads = []
    for h in range(H_A):
        hs = slice(h * GLA_C, (h + 1) * GLA_C)
        o_heads.append(inter[hs] + _dot(scores[hs].astype(BF16), v_b[:, h * DV_A:(h + 1) * DV_A]))
    o = jnp.concatenate(o_heads, axis=1)

    t_i = lax.broadcasted_iota(jnp.int32, (GLA_SB, A_QK), 0)
    group = 2 * GLA_SB
    diag = []
    for i0 in range(0, GLA_C, group):
        base = i0 * GLA_SB
        for i in range(i0, i0 + group, GLA_SB):
            q_i = q_ref[pl.ds(row0 + i, GLA_SB), :]
            b_i = b_ref[i:i + GLA_SB, :]
            for j in range(0, GLA_SB, 2):
                pair = []
                for jj in (j, j + 1):
                    k_s = k_ref[pl.ds(row0 + i + jj, 1), :]
                    b_s = b_ref[i + jj:i + jj + 1, :]
                    keep = (t_i <= jj) if rev else (t_i >= jj)
                    pair.append(jnp.where(keep, q_i * k_s * jnp.exp(b_i - b_s), 0.0))
                lo = (i + j) * GLA_SB
                a_ref[lo:lo + 2 * GLA_SB, :] = jnp.concatenate(pair, axis=0).astype(BF16)
        w = _dot(a_ref[base:base + group * GLA_SB, :], m1_ref[...])
        for i in range(i0, i0 + group, GLA_SB):
            acc = None
            for j in range(GLA_SB):
                lo = (i + j) * GLA_SB - base
                term = w[lo:lo + GLA_SB] * v_ref[pl.ds(row0 + i + j, 1), :]
                acc = term if acc is None else acc + term
            diag.append(acc)
    o_ref[rows, :] = o + jnp.concatenate(diag, axis=0)

    t = jnp.transpose(jnp.concatenate([kl, jnp.broadcast_to(last, (GLA_C, A_QK))], axis=0))
    kv = _dot(t[:, :GLA_C].astype(BF16), v_b)
    a_col = jnp.exp(t[:, GLA_C:GLA_C + 1])
    for h in range(H_A):
        hs = slice(h * DK_A, (h + 1) * DK_A)
        s_ref[hs, :] = a_col[hs] * s_old[hs] + kv[hs, h * DV_A:(h + 1) * DV_A]


def _gla_kernel(fblk_ref, bblk_ref, first_ref, init_ref, seq_ref,
                qf_ref, kf_ref, vf_ref, gf_ref, qr_ref, kr_ref, vr_ref, gr_ref, s0_ref, m1_ref,
                of_ref, ob_ref, sfin_ref, sf_ref, sb_ref, af_ref, ab_ref, bf_ref, bb_ref):
    step = pl.program_id(0)

    @pl.when(first_ref[step] == 1)
    def _():
        sf_ref[...] = s0_ref[0]
        sb_ref[...] = s0_ref[1]

    r_h = lax.broadcasted_iota(jnp.int32, (H_A * GLA_C, A_QK), 0) // GLA_C
    c_h = lax.broadcasted_iota(jnp.int32, (H_A * GLA_C, A_QK), 1) // DK_A
    head_mask = r_h == c_h
    t_s = lax.broadcasted_iota(jnp.int32, (H_A * GLA_C, GLA_C), 0) % GLA_C
    s_s = lax.broadcasted_iota(jnp.int32, (H_A * GLA_C, GLA_C), 1)
    same_block = {}
    size = 2 * GLA_SB
    while size < GLA_C:
        same_block[size] = (t_s // size) == (s_s // size)
        size *= 2
    nch = TM // GLA_C

    def body(c, carry):
        _gla_chunk(qf_ref, kf_ref, vf_ref, gf_ref, of_ref, pl.multiple_of(c * GLA_C, GLA_C),
                   sf_ref, af_ref, bf_ref, m1_ref, head_mask, same_block, False)
        _gla_chunk(qr_ref, kr_ref, vr_ref, gr_ref, ob_ref, pl.multiple_of((nch - 1 - c) * GLA_C, GLA_C),
                   sb_ref, ab_ref, bb_ref, m1_ref, head_mask, same_block, True)
        return carry

    lax.fori_loop(0, nch, body, 0)
    sfin_ref[0] = sf_ref[...]
    sfin_ref[1] = sb_ref[...]


def _gla_tables():
    fblk, bblk, first, init, seq = [], [], [], [], []
    for b in range(BATCH):
        fblk.append(b), bblk.append(b), first.append(1), init.append(0), seq.append(b)
    for s in range(DEC_BATCH):
        for j in range(LBB):
            fblk.append(NCB + s * LBB + j)
            bblk.append(NCB + s * LBB + LBB - 1 - j)
            first.append(1 if j == 0 else 0)
            init.append(1 + s)
            seq.append(BATCH + s)
    return [jnp.asarray(np.array(t, np.int32)) for t in (fblk, bblk, first, init, seq)]


def _gla(a_proj, glog, s0_all, m1):
    tables = _gla_tables()
    nsteps = int(tables[0].shape[0])
    nseq = BATCH + DEC_BATCH

    def fmap(col):
        return lambda i, fb, bb, fi, ini, sq: (fb[i], col)

    def rmap(col):
        return lambda i, fb, bb, fi, ini, sq: (bb[i], col)

    gs = pltpu.PrefetchScalarGridSpec(
        num_scalar_prefetch=5,
        grid=(nsteps,),
        in_specs=[
            pl.BlockSpec((TM, A_QK), fmap(0)), pl.BlockSpec((TM, A_QK), fmap(1)),
            pl.BlockSpec((TM, A_V), fmap(1)), pl.BlockSpec((TM, A_QK), fmap(0)),
            pl.BlockSpec((TM, A_QK), rmap(0)), pl.BlockSpec((TM, A_QK), rmap(1)),
            pl.BlockSpec((TM, A_V), rmap(1)), pl.BlockSpec((TM, A_QK), rmap(1)),
            pl.BlockSpec((None, 2, A_QK, DV_A), lambda i, fb, bb, fi, ini, sq: (ini[i], 0, 0, 0)),
            pl.BlockSpec((A_QK, A_V), lambda i, fb, bb, fi, ini, sq: (0, 0)),
        ],
        out_specs=[
            pl.BlockSpec((TM, A_V), fmap(0)), pl.BlockSpec((TM, A_V), rmap(0)),
            pl.BlockSpec((None, 2, A_QK, DV_A), lambda i, fb, bb, fi, ini, sq: (sq[i], 0, 0, 0)),
        ],
        scratch_shapes=[
            pltpu.VMEM((A_QK, DV_A), F32), pltpu.VMEM((A_QK, DV_A), F32),
            pltpu.VMEM((GLA_C * GLA_SB, A_QK), BF16), pltpu.VMEM((GLA_C * GLA_SB, A_QK), BF16),
            pltpu.VMEM((GLA_C, A_QK), F32), pltpu.VMEM((GLA_C, A_QK), F32),
        ],
    )
    return pl.pallas_call(
        _gla_kernel,
        grid_spec=gs,
        out_shape=[
            jax.ShapeDtypeStruct((M_TOK, A_V), F32), jax.ShapeDtypeStruct((M_TOK, A_V), F32),
            jax.ShapeDtypeStruct((nseq, 2, A_QK, DV_A), F32),
        ],
        compiler_params=_cparams(("arbitrary",)),
        name="gla",
    )(*tables, a_proj, a_proj, a_proj, glog, a_proj, a_proj, a_proj, glog, s0_all, m1)


def _softmax_parts(s):
    e = jnp.exp(s - jnp.max(s, axis=-1, keepdims=True))
    return e, 1.0 / jnp.sum(e, axis=-1, keepdims=True)


def _diffattn_body(q_ref, k_ref, v_ref, lq1_ref, lk1_ref, lq2_ref, lk2_ref, gsub_ref, o_ref, lam_init):
    lam = (jnp.exp(jnp.sum(lq1_ref[...] * lk1_ref[...], axis=-1, keepdims=True))
           - jnp.exp(jnp.sum(lq2_ref[...] * lk2_ref[...], axis=-1, keepdims=True)) + lam_init)
    lane = lax.broadcasted_iota(jnp.int32, (q_ref.shape[0], LANES), 1)
    for h in range(H_B):
        hs = slice(h * LANES, (h + 1) * LANES)
        q_h = q_ref[:, hs]
        k_h = k_ref[:, hs]
        zero = jnp.zeros_like(q_h)
        e1, r1 = _softmax_parts(_dot_nt(jnp.where(lane < DH_B, q_h, zero), k_h))
        e2, r2 = _softmax_parts(_dot_nt(jnp.where(lane >= DH_B, q_h, zero), k_h))
        w = e1 * r1 - e2 * (lam * r2)
        o = _dot(w.astype(BF16), v_ref[:, hs])
        o_ref[:, hs] = (_rms(o, gsub_ref[...]) * (1.0 - lam_init)).astype(BF16)


def _diffattn_kernel(q_ref, k_ref, v_ref, lq1, lk1, lq2, lk2, gsub, o_ref, *, lam_init):
    _diffattn_body(q_ref, k_ref, v_ref, lq1, lk1, lq2, lk2, gsub, o_ref, lam_init)


def _diffattn(qb, kb, vb, k_lat, v_lat, lam_params, gsub, lam_init):
    small = [pl.BlockSpec((1, DH_B), lambda *_: (0, 0))] * 4 + [pl.BlockSpec((1, DV_B), lambda *_: (0, 0))]
    body = functools.partial(_diffattn_kernel, lam_init=lam_init)
    ctx = pl.pallas_call(
        body,
        grid=(NCB,),
        in_specs=[_row_spec(B_QK), _row_spec(B_QK), _row_spec(B_V)] + small,
        out_specs=_row_spec(B_V),
        out_shape=jax.ShapeDtypeStruct((N_CTX, B_V), BF16),
        compiler_params=_cparams(("arbitrary",)),
        name="diffattn_ctx",
    )(qb, kb, vb, *lam_params, gsub)
    nk = DEC_SEQ + PAST_LEN
    nqb = DEC_SEQ // TQ_LAT
    off = N_CTX // TQ_LAT
    lat = pl.pallas_call(
        body,
        grid=(DEC_BATCH, nqb),
        in_specs=[
            pl.BlockSpec((TQ_LAT, B_QK), lambda b, j: (off + b * nqb + j, 0)),
            pl.BlockSpec((None, nk, B_QK), lambda b, j: (b, 0, 0)),
            pl.BlockSpec((None, nk, B_V), lambda b, j: (b, 0, 0)),
        ] + small,
        out_specs=pl.BlockSpec((TQ_LAT, B_V), lambda b, j: (b * nqb + j, 0)),
        out_shape=jax.ShapeDtypeStruct((N_LAT, B_V), BF16),
        compiler_params=_cparams(("arbitrary", "arbitrary")),
        name="diffattn_lat",
    )(qb, k_lat, v_lat, *lam_params, gsub)
    return ctx, lat


def _pick_rows(ctx_ref, lat_ref):
    return jnp.where(pl.program_id(0) >= NCB, lat_ref[...], ctx_ref[...])


def _postmix_even_kernel(of_ref, ob_ref, ra_ref, actx_ref, alat_ref, x_ref, g1_ref, ggla_ref, wo_ref, o_ref):
    heads = []
    for h in range(H_A):
        hs = slice(h * DV_A, (h + 1) * DV_A)
        heads.append((_rms(of_ref[:, hs] + ob_ref[:, hs], ggla_ref[...]) * _silu(ra_ref[:, hs])).astype(BF16))
    mix = jnp.concatenate(heads + [_pick_rows(actx_ref, alat_ref)], axis=1)
    o_ref[...] = x_ref[...] + g1_ref[...] * _dot(mix, wo_ref[...])


def _postmix_even(o_f, o_b, a_proj, attn_ctx, attn_lat, x, mod, g_gla, w_out):
    return pl.pallas_call(
        _postmix_even_kernel,
        grid=(NB,),
        in_specs=[_row_spec(A_V), _row_spec(A_V), _row_spec(A_V, 2), _ctx_spec(B_V), _lat_spec(B_V),
                  _row_spec(D_MODEL), _mod_spec(2), _const_spec((1, DV_A)), _const_spec((A_V + B_V, D_MODEL))],
        out_specs=_row_spec(D_MODEL),
        out_shape=jax.ShapeDtypeStruct((M_TOK, D_MODEL), F32),
        compiler_params=_cparams(("arbitrary",)),
        name="postmix_even",
    )(o_f, o_b, a_proj, attn_ctx, attn_lat, x, mod, g_gla, w_out)


def _ffn_even_kernel(x_ref, gam_ref, sh_ref, sc_ref, g2_ref, wg_ref, wu_ref, wd_ref, o_ref):
    x = x_ref[...]
    hb = _norm_mod(x, gam_ref[...], sc_ref[...], sh_ref[...]).astype(BF16)
    acc = None
    nsub = 2
    w = D_FF // nsub
    for c in range(nsub):
        cs = slice(c * w, (c + 1) * w)
        act = (_silu(_dot(hb, wg_ref[:, cs])) * _dot(hb, wu_ref[:, cs])).astype(BF16)
        y = _dot(act, wd_ref[cs, :])
        acc = y if acc is None else acc + y
    o_ref[...] = x + g2_ref[...] * acc


def _ffn_even(x, gamma, mod, w_gate, w_up, w_down):
    return pl.pallas_call(
        _ffn_even_kernel,
        grid=(NB,),
        in_specs=[_row_spec(D_MODEL), _const_spec((1, D_MODEL)), _mod_spec(3), _mod_spec(4), _mod_spec(5),
                  _const_spec((D_MODEL, D_FF)), _const_spec((D_MODEL, D_FF)), _const_spec((D_FF, D_MODEL))],
        out_specs=_row_spec(D_MODEL),
        out_shape=jax.ShapeDtypeStruct((M_TOK, D_MODEL), F32),
        compiler_params=_cparams(("arbitrary",)),
        name="ffn_even",
    )(x, gamma, mod, mod, mod, w_gate, w_up, w_down)


def _inproj_odd_kernel(x_ref, gam_ref, sh_ref, sc_ref, w_ref, gq_ref, gk_ref, cos_ref, sin_ref,
                       q_ref, k_ref, v_ref, kctx_ref, vctx_ref):
    is_lat = pl.program_id(0) >= NCB
    is_ctx = jnp.logical_not(is_lat)
    hb = _norm_mod(x_ref[...], gam_ref[...], sc_ref[...], sh_ref[...]).astype(BF16)
    z = _dot(hb, w_ref[...])
    cos = jnp.where(is_lat, cos_ref[...], 1.0)
    sin = jnp.where(is_lat, sin_ref[...], 0.0)
    scale = DH_C ** -0.5
    for h in range(H_C):
        hs = slice(h * DH_C, (h + 1) * DH_C)
        q_ref[:, hs] = (_rope(_rms(z[:, hs], gq_ref[...]), cos, sin, DH_C // 4) * scale).astype(BF16)
    nq = H_C * DH_C
    for h in range(HKV_C):
        hs = slice(h * DH_C, (h + 1) * DH_C)
        k_n = _rms(z[:, nq + h * DH_C:nq + (h + 1) * DH_C], gk_ref[...])

        @pl.when(is_ctx)
        def _(k_n=k_n, hs=hs):
            kctx_ref[:, hs] = k_n

        k_ref[:, hs] = _rope(k_n, cos, sin, DH_C // 4).astype(BF16)
    nkv = HKV_C * DH_C
    v = z[:, nq + nkv:]

    @pl.when(is_ctx)
    def _():
        vctx_ref[...] = v

    v_ref[...] = v.astype(BF16)


def _inproj_odd(x, gamma, mod, w_in, g_q, g_k, cos, sin):
    nq, nkv = H_C * DH_C, HKV_C * DH_C
    return pl.pallas_call(
        _inproj_odd_kernel,
        grid=(NB,),
        in_specs=[
            _row_spec(D_MODEL), _const_spec((1, D_MODEL)), _mod_spec(0), _mod_spec(1),
            _const_spec((D_MODEL, nq + 2 * nkv)), _const_spec((1, DH_C)), _const_spec((1, DH_C)),
            pl.BlockSpec((TM, LANES), lambda i: (_pos_block(i), 0)),
            pl.BlockSpec((TM, LANES), lambda i: (_pos_block(i), 0)),
        ],
        out_specs=[_row_spec(nq), _row_spec(nkv), _row_spec(nkv), _ctx_spec(nkv), _ctx_spec(nkv)],
        out_shape=[
            jax.ShapeDtypeStruct((M_TOK, nq), BF16), jax.ShapeDtypeStruct((M_TOK, nkv), BF16),
            jax.ShapeDtypeStruct((M_TOK, nkv), BF16), jax.ShapeDtypeStruct((N_CTX, nkv), F32),
            jax.ShapeDtypeStruct((N_CTX, nkv), F32),
        ],
        compiler_params=_cparams(("arbitrary",)),
        name="inproj_odd",
    )(x, gamma, mod, mod, w_in, g_q, g_k, cos, sin)


def _gqa_body(q_ref, k_ref, v_ref, o_ref):
    rep = H_C // HKV_C
    tq = q_ref.shape[0]
    for hk in range(HKV_C):
        ks = slice(hk * DH_C, (hk + 1) * DH_C)
        q_g = jnp.concatenate([q_ref[:, (hk * rep + g) * DH_C:(hk * rep + g + 1) * DH_C] for g in range(rep)],
                              axis=0)
        e, r = _softmax_parts(_dot_nt(q_g, k_ref[:, ks]))
        o = _dot((e * r).astype(BF16), v_ref[:, ks])
        for g in range(rep):
            o_ref[:, (hk * rep + g) * DH_C:(hk * rep + g + 1) * DH_C] = o[g * tq:(g + 1) * tq].astype(BF16)


def _gqa(q, k, v, k_lat, v_lat):
    nq, nkv = H_C * DH_C, HKV_C * DH_C
    ctx = pl.pallas_call(
        _gqa_body,
        grid=(NCB,),
        in_specs=[_row_spec(nq), _row_spec(nkv), _row_spec(nkv)],
        out_specs=_row_spec(nq),
        out_shape=jax.ShapeDtypeStruct((N_CTX, nq), BF16),
        compiler_params=_cparams(("arbitrary",)),
        name="gqa_ctx",
    )(q, k, v)
    nk = DEC_SEQ + PAST_LEN
    nqb = DEC_SEQ // TQ_LAT
    off = N_CTX // TQ_LAT
    lat = pl.pallas_call(
        _gqa_body,
        grid=(DEC_BATCH, nqb),
        in_specs=[
            pl.BlockSpec((TQ_LAT, nq), lambda b, j: (off + b * nqb + j, 0)),
            pl.BlockSpec((None, nk, nkv), lambda b, j: (b, 0, 0)),
            pl.BlockSpec((None, nk, nkv), lambda b, j: (b, 0, 0)),
        ],
        out_specs=pl.BlockSpec((TQ_LAT, nq), lambda b, j: (b * nqb + j, 0)),
        out_shape=jax.ShapeDtypeStruct((N_LAT, nq), BF16),
        compiler_params=_cparams(("arbitrary", "arbitrary")),
        name="gqa_lat",
    )(q, k_lat, v_lat)
    return ctx, lat


def _postmix_odd_kernel(actx_ref, alat_ref, x_ref, g1_ref, wo_ref, gam_ref, sh_ref, sc_ref, wr_ref,
                        x1_ref, h_ref, w1_ref, w2_ref, idx_ref, cnt_ref, run_ref):
    @pl.when(pl.program_id(0) == 0)
    def _():
        run_ref[...] = jnp.zeros_like(run_ref)

    x1 = x_ref[...] + g1_ref[...] * _dot(_pick_rows(actx_ref, alat_ref), wo_ref[...])
    x1_ref[...] = x1
    h = _norm_mod(x1, gam_ref[...], sc_ref[...], sh_ref[...])
    h_ref[...] = h
    logits = _dot_hi(h, wr_ref[...])
    lane = lax.broadcasted_iota(jnp.int32, logits.shape, 1)
    lane_f = lane.astype(F32)
    lg = jnp.where(lane < N_EXPERTS, logits, -jnp.inf)
    m1 = jnp.max(lg, axis=-1, keepdims=True)
    i1 = jnp.min(jnp.where(lg == m1, lane_f, float(LANES)), axis=-1, keepdims=True)
    lg2 = jnp.where(lane_f == i1, -jnp.inf, lg)
    m2 = jnp.max(lg2, axis=-1, keepdims=True)
    i2 = jnp.min(jnp.where(lg2 == m2, lane_f, float(LANES)), axis=-1, keepdims=True)
    e = jnp.exp(m2 - m1)
    w1 = 1.0 / (1.0 + e)
    w1_ref[...] = jnp.broadcast_to(w1, logits.shape)
    w2_ref[...] = jnp.broadcast_to(e * w1, logits.shape)
    oh1 = jnp.where(lane_f == i1, 1.0, 0.0)
    oh2 = jnp.where(lane_f == i2, 1.0, 0.0)
    both = oh1 + oh2
    r_i = lax.broadcasted_iota(jnp.int32, (TM, TM), 0)
    c_i = lax.broadcasted_iota(jnp.int32, (TM, TM), 1)
    before = jnp.where(c_i < r_i, 1.0, 0.0).astype(BF16)
    pref = _dot(before, both.astype(BF16)) + run_ref[0:1, :]
    rank1 = jnp.sum(pref * oh1, axis=-1, keepdims=True)
    rank2 = jnp.sum(pref * oh2, axis=-1, keepdims=True)
    run_ref[...] = run_ref[...] + jnp.sum(both, axis=0, keepdims=True)
    cnt_ref[...] = run_ref[...]
    quarter = LANES // 4
    idx_ref[...] = jnp.where(lane < quarter, i1, jnp.where(lane < 2 * quarter, i2, jnp.where(
        lane < 3 * quarter, rank1, rank2))).astype(jnp.int32)


def _postmix_odd(attn_ctx, attn_lat, x, mod, w_out, gamma, w_router):
    return pl.pallas_call(
        _postmix_odd_kernel,
        grid=(NB,),
        in_specs=[_ctx_spec(H_C * DH_C), _lat_spec(H_C * DH_C), _row_spec(D_MODEL), _mod_spec(2),
                  _const_spec((H_C * DH_C, D_MODEL)),
                  _const_spec((1, D_MODEL)), _mod_spec(3), _mod_spec(4), _const_spec((D_MODEL, LANES))],
        out_specs=[_row_spec(D_MODEL), _row_spec(D_MODEL), _row_spec(LANES), _row_spec(LANES), _row_spec(LANES),
                   _const_spec((8, LANES))],
        out_shape=[
            jax.ShapeDtypeStruct((M_TOK, D_MODEL), F32), jax.ShapeDtypeStruct((M_TOK, D_MODEL), F32),
            jax.ShapeDtypeStruct((M_TOK, LANES), F32), jax.ShapeDtypeStruct((M_TOK, LANES), F32),
            jax.ShapeDtypeStruct((M_TOK, LANES), jnp.int32), jax.ShapeDtypeStruct((8, LANES), F32),
        ],
        scratch_shapes=[pltpu.VMEM((8, LANES), F32)],
        compiler_params=_cparams(("arbitrary",)),
        name="postmix_odd",
    )(attn_ctx, attn_lat, x, mod, w_out, gamma, mod, mod, w_router)


def _cast_rows(src_ref, dst_ref, piece):
    for r in range(0, src_ref.shape[0], piece):
        dst_ref[r:r + piece, :] = src_ref[r:r + piece, :].astype(BF16)


def _experts_up_kernel(tile_ref, chunk_ref, exp_ref, first_ref, valid_ref, xa_ref, xb_ref, wg_ref, wu_ref, act_ref,
                       wgb_ref, wub_ref):
    s = pl.program_id(0)

    @pl.when(first_ref[s] == 1)
    def _():
        _cast_rows(wg_ref, wgb_ref, 256)
        _cast_rows(wu_ref, wub_ref, 256)

    @pl.when(valid_ref[s] == 1)
    def _():
        x = jnp.where(tile_ref[s] >= NT_E // 2, xb_ref[...], xa_ref[...]).astype(BF16)
        act_ref[...] = (_silu(_dot(x, wgb_ref[...])) * _dot(x, wub_ref[...])).astype(BF16)


def _experts_down_kernel(act_tile_ref, exp_ref, first_ref, valid_ref, act_ref, wd_ref, o_ref, wdb_ref):
    i = pl.program_id(0)

    @pl.when(first_ref[i] == 1)
    def _():
        _cast_rows(wd_ref, wdb_ref, 512)

    @pl.when(valid_ref[i] == 1)
    def _():
        o_ref[...] = _dot(act_ref[...], wdb_ref[...])

    @pl.when(valid_ref[i] == 0)
    def _():
        o_ref[...] = jnp.zeros_like(o_ref)


def _experts(up_tables, down_tables, xs, w_gate, w_up, w_down):
    up = pl.pallas_call(
        _experts_up_kernel,
        grid_spec=pltpu.PrefetchScalarGridSpec(
            num_scalar_prefetch=5,
            grid=(NC_E * NT_E,),
            in_specs=[
                pl.BlockSpec((TME, D_MODEL), lambda s, t, c, e, f, v: (jnp.minimum(t[s], NT_E // 2 - 1), 0)),
                pl.BlockSpec((TME, D_MODEL), lambda s, t, c, e, f, v: (jnp.maximum(t[s] - NT_E // 2, 0), 0)),
                pl.BlockSpec((None, D_MODEL, TF_E), lambda s, t, c, e, f, v: (e[s], 0, c[s])),
                pl.BlockSpec((None, D_MODEL, TF_E), lambda s, t, c, e, f, v: (e[s], 0, c[s])),
            ],
            out_specs=pl.BlockSpec((TME, TF_E), lambda s, t, c, e, f, v: (t[s], c[s])),
            scratch_shapes=[pltpu.VMEM((D_MODEL, TF_E), BF16), pltpu.VMEM((D_MODEL, TF_E), BF16)],
        ),
        out_shape=jax.ShapeDtypeStruct((NT_E * TME, D_FF_E), BF16),
        compiler_params=_cparams(("arbitrary",)),
        name="experts_up",
    )(*up_tables, *xs, w_gate, w_up)
    return pl.pallas_call(
        _experts_down_kernel,
        grid_spec=pltpu.PrefetchScalarGridSpec(
            num_scalar_prefetch=4,
            grid=(NT_E,),
            in_specs=[
                pl.BlockSpec((TME, D_FF_E), lambda i, a, e, f, v: (a[i], 0)),
                pl.BlockSpec((None, D_FF_E, D_MODEL), lambda i, a, e, f, v: (e[i], 0, 0)),
            ],
            out_specs=pl.BlockSpec((TME, D_MODEL), lambda i, a, e, f, v: (i, 0)),
            scratch_shapes=[pltpu.VMEM((D_FF_E, D_MODEL), BF16)],
        ),
        out_shape=jax.ShapeDtypeStruct((NT_E * TME, D_MODEL), F32),
        compiler_params=_cparams(("arbitrary",)),
        name="experts_down",
    )(*down_tables, up, w_down)


def _combine_kernel(x_ref, y1_ref, y2_ref, w1_ref, w2_ref, g2_ref, gam_ref, octx_ref, olat_ref):
    rep = D_MODEL // LANES
    w1 = jnp.concatenate([w1_ref[...]] * rep, axis=1)
    w2 = jnp.concatenate([w2_ref[...]] * rep, axis=1)
    x2 = x_ref[...] + g2_ref[...] * (w1 * y1_ref[...] + w2 * y2_ref[...])
    y = _rms(x2, gam_ref[...])
    is_lat = pl.program_id(0) >= NCB

    @pl.when(jnp.logical_not(is_lat))
    def _():
        octx_ref[...] = y

    @pl.when(is_lat)
    def _():
        olat_ref[...] = y


def _combine(x1, y1, y2, w1, w2, mod, gamma):
    return pl.pallas_call(
        _combine_kernel,
        grid=(NB,),
        in_specs=[_row_spec(D_MODEL), _row_spec(D_MODEL), _row_spec(D_MODEL), _row_spec(LANES), _row_spec(LANES),
                  _mod_spec(5), _const_spec((1, D_MODEL))],
        out_specs=[_ctx_spec(D_MODEL), _lat_spec(D_MODEL)],
        out_shape=[jax.ShapeDtypeStruct((N_CTX, D_MODEL), F32), jax.ShapeDtypeStruct((N_LAT, D_MODEL), F32)],
        compiler_params=_cparams(("arbitrary",)),
        name="combine",
    )(x1, y1, y2, w1, w2, mod, gamma)


def _rope_tables(rot_dim):
    n = DEC_SEQ
    rows = jnp.repeat(jnp.arange(n // GRID_W, dtype=F32), GRID_W)
    cols = jnp.tile(jnp.arange(GRID_W, dtype=F32), n // GRID_W)
    half = rot_dim // 2
    freqs = ROPE_THETA ** (-jnp.arange(0, half, 2, dtype=F32) / half)
    ang_r, ang_c = rows[:, None] * freqs, cols[:, None] * freqs
    cos = jnp.concatenate([jnp.cos(ang_r)] * 2 + [jnp.cos(ang_c)] * 2, axis=-1)
    sin = jnp.concatenate([-jnp.sin(ang_r), jnp.sin(ang_r), -jnp.sin(ang_c), jnp.sin(ang_c)], axis=-1)
    rep = LANES // rot_dim
    return jnp.tile(cos, (1, rep)), jnp.tile(sin, (1, rep))


def _rows(a, idx):
    return a.at[idx].get(mode="promise_in_bounds")


def _route(idx, counts_f):
    quarter = LANES // 4
    experts = jnp.arange(N_EXPERTS, dtype=jnp.int32)
    e2 = jnp.stack([idx[:, 0], idx[:, quarter]], axis=1)
    rank2 = jnp.stack([idx[:, 2 * quarter], idx[:, 3 * quarter]], axis=1)
    counts = counts_f[0, :N_EXPERTS].astype(jnp.int32)
    tiles = (counts + TME - 1) // TME
    tile_end = jnp.cumsum(tiles)
    tile_start = tile_end - tiles
    start = jnp.cumsum(counts) - counts
    onehot = e2[:, :, None] == experts[None, None, :]
    pos = jnp.sum(jnp.where(onehot, tile_start[None, None, :] * TME, 0), axis=-1) + rank2
    tile_id = jnp.arange(NT_E, dtype=jnp.int32)
    tile_expert = jnp.minimum(jnp.sum((tile_id[:, None] >= tile_end[None, :]).astype(jnp.int32), axis=1),
                              N_EXPERTS - 1).astype(jnp.int32)
    tile_valid = (tile_id < tile_end[-1]).astype(jnp.int32)
    t_oh = tile_expert[:, None] == experts[None, :]
    t_first = jnp.sum(jnp.where(t_oh, (start - tile_start * TME)[None, :], 0), axis=1) + tile_id * TME
    t_last = jnp.sum(jnp.where(t_oh, (start + counts)[None, :], 0), axis=1)
    order = jnp.argsort(e2.reshape(-1), stable=True).astype(jnp.int32)
    n_assign = order.shape[0]
    g_idx = t_first[:, None] + jnp.arange(TME, dtype=jnp.int32)[None, :]
    live = jnp.logical_and(g_idx < t_last[:, None], tile_valid[:, None] == 1)
    src = _rows(order, jnp.clip(g_idx, 0, n_assign - 1).reshape(-1)) // 2
    src_tok = jnp.where(live.reshape(-1), src, 0).astype(jnp.int32)

    n_tiles = tile_end[-1]
    last_tile = jnp.maximum(n_tiles - 1, 0)
    last_expert = jnp.sum(jnp.where(tile_id == last_tile, tile_expert, 0))
    t_tstart = jnp.sum(jnp.where(t_oh, tile_start[None, :], 0), axis=1)
    down_tables = (
        jnp.where(tile_valid == 1, tile_id, last_tile).astype(jnp.int32),
        tile_expert,
        jnp.logical_and(tile_valid == 1, tile_id == t_tstart).astype(jnp.int32),
        tile_valid,
    )
    s_id = jnp.arange(NC_E * NT_E, dtype=jnp.int32)
    s_exp = jnp.minimum(jnp.sum((s_id[:, None] >= NC_E * tile_end[None, :]).astype(jnp.int32), axis=1),
                        N_EXPERTS - 1)
    s_oh = s_exp[:, None] == experts[None, :]
    s_tstart = jnp.sum(jnp.where(s_oh, tile_start[None, :], 0), axis=1)
    s_ntile = jnp.maximum(jnp.sum(jnp.where(s_oh, tiles[None, :], 0), axis=1), 1)
    rel = s_id - NC_E * s_tstart
    s_valid = s_id < NC_E * n_tiles
    up_tables = (
        jnp.where(s_valid, s_tstart + rel % s_ntile, last_tile).astype(jnp.int32),
        jnp.where(s_valid, rel // s_ntile, NC_E - 1).astype(jnp.int32),
        jnp.where(s_valid, s_exp, last_expert).astype(jnp.int32),
        jnp.logical_and(s_valid, rel % s_ntile == 0).astype(jnp.int32),
        s_valid.astype(jnp.int32),
    )
    return src_tok, pos, up_tables, down_tables


def lambda_init(layer):
    return 0.8 - 0.6 * math.exp(-0.3 * layer)


def kernel(x_prompt, x_sample, state_a, cache_b_k, cache_b_v, cache_c_k, cache_c_v, c, c_ctx, w_mod, b_mod, norm_mix, norm_ffn, w_in_even, w_gate2_a, b_gate_a, g_gla, lam_q1, lam_k1, lam_q2, lam_k2, g_sub_b, w_out_even, w_in_odd, g_q_c, g_k_c, w_out_odd, ffn_gate, ffn_up, ffn_down, w_router, exp_gate, exp_up, exp_down, norm_final):
    x = jnp.concatenate([x_prompt.reshape(N_CTX, D_MODEL), x_sample.reshape(N_LAT, D_MODEL)], axis=0)
    cond = jnp.concatenate([c_ctx[None, :], c, jnp.zeros((N_COND - 1 - DEC_BATCH, D_MODEL), F32)], axis=0)
    mod = _modulation(cond, w_mod, b_mod).reshape(DEPTH, N_COND, 1, 6 * D_MODEL)

    w = w_in_even[0]
    na = 2 * A_QK + 2 * A_V
    gate_lo = na
    gate_hi = na + 2 * GATE_RANK
    col_scale = jnp.concatenate([jnp.full((A_QK,), DK_A ** -0.5, F32), jnp.ones((na - A_QK,), F32),
                                 jnp.full((B_QK,), DH_B ** -0.5, F32), jnp.ones((B_QK + B_V,), F32)])
    w_main = (jnp.concatenate([w[:, :gate_lo], w[:, gate_hi:]], axis=1) * col_scale).astype(BF16)
    w_gates = jnp.pad(w[:, gate_lo:gate_hi], ((0, 0), (0, LANES - 2 * GATE_RANK))).astype(BF16)
    g2 = jnp.zeros((LANES, 2 * A_QK), F32)
    g2 = g2.at[:GATE_RANK, :A_QK].set(w_gate2_a[0, 0]).at[GATE_RANK:2 * GATE_RANK, A_QK:].set(w_gate2_a[0, 1])
    bg = b_gate_a[0].reshape(1, 2 * A_QK)
    cos_b, sin_b = _rope_tables(DH_B)
    a_proj, glog, qb, kb, vb, kb_ctx, vb_ctx = _inproj_even(
        x, norm_mix[0][None, :], mod[0], w_main, w_gates, g2.astype(BF16), bg, cos_b, sin_b)

    s0_all = jnp.concatenate([jnp.zeros((1, 2, A_QK, DV_A), F32),
                              state_a[:, 0].reshape(DEC_BATCH, 2, A_QK, DV_A)], axis=0)
    hd = np.arange(A_QK)[:, None] // DK_A == np.arange(A_V)[None, :] // DV_A
    m1 = jnp.asarray(hd.astype(np.float32)).astype(BF16)
    o_f, o_b, s_fin = _gla(a_proj, glog, s0_all, m1)

    k_lat = jnp.concatenate([kb[N_CTX:].reshape(DEC_BATCH, DEC_SEQ, B_QK),
                             cache_b_k[:, 0].reshape(DEC_BATCH, PAST_LEN, B_QK).astype(BF16)], axis=1)
    v_lat = jnp.concatenate([vb[N_CTX:].reshape(DEC_BATCH, DEC_SEQ, B_V),
                             cache_b_v[:, 0].reshape(DEC_BATCH, PAST_LEN, B_V).astype(BF16)], axis=1)
    lam_params = [p[0][None, :] for p in (lam_q1, lam_k1, lam_q2, lam_k2)]
    attn_b = _diffattn(qb, kb, vb, k_lat, v_lat, lam_params, g_sub_b[0][None, :], lambda_init(0))

    x = _postmix_even(o_f, o_b, a_proj, *attn_b, x, mod[0], g_gla[0][None, :], w_out_even[0].astype(BF16))
    x = _ffn_even(x, norm_ffn[0][None, :], mod[0], ffn_gate[0].astype(BF16), ffn_up[0].astype(BF16),
                  ffn_down[0].astype(BF16))

    cos_c, sin_c = _rope_tables(DH_C)
    q_c, k_c, v_c, kc_ctx, vc_ctx = _inproj_odd(x, norm_mix[1][None, :], mod[1], w_in_odd[0].astype(BF16),
                                     g_q_c[0][None, :], g_k_c[0][None, :], cos_c, sin_c)
    nkv = HKV_C * DH_C
    k_lat = jnp.concatenate([k_c[N_CTX:].reshape(DEC_BATCH, DEC_SEQ, nkv),
                             cache_c_k[:, 0].reshape(DEC_BATCH, PAST_LEN, nkv).astype(BF16)], axis=1)
    v_lat = jnp.concatenate([v_c[N_CTX:].reshape(DEC_BATCH, DEC_SEQ, nkv),
                             cache_c_v[:, 0].reshape(DEC_BATCH, PAST_LEN, nkv).astype(BF16)], axis=1)
    attn_c = _gqa(q_c, k_c, v_c, k_lat, v_lat)
    w_r = jnp.pad(w_router[0], ((0, 0), (0, LANES - N_EXPERTS)))
    x1, h_moe, w1, w2, idx, counts = _postmix_odd(*attn_c, x, mod[1], w_out_odd[0].astype(BF16),
                                                  norm_ffn[1][None, :], w_r)

    src_tok, pos, up_tables, down_tables = _route(idx, counts)
    half = NT_E * TME // 2
    xs = (_rows(h_moe, src_tok[:half]), _rows(h_moe, src_tok[half:]))
    ys = _experts(up_tables, down_tables, xs, exp_gate[0], exp_up[0], exp_down[0])
    y1 = _rows(ys, pos[:, 0])
    y2 = _rows(ys, pos[:, 1])
    y_ctx, y_lat = _combine(x1, y1, y2, w1, w2, mod[1], norm_final[None, :])

    y_prompt = y_ctx.reshape(BATCH, SEQ, D_MODEL)
    y_sample = y_lat.reshape(DEC_BATCH, DEC_SEQ, D_MODEL)
    new_state_a = s_fin[:BATCH].reshape(BATCH, 1, 2, H_A, DK_A, DV_A)
    new_b_k = kb_ctx.reshape(BATCH, 1, SEQ, H_B, 2, DH_B)
    new_b_v = vb_ctx.reshape(BATCH, 1, SEQ, H_B, DV_B)
    new_c_k = kc_ctx.reshape(BATCH, 1, SEQ, HKV_C, DH_C)
    new_c_v = vc_ctx.reshape(BATCH, 1, SEQ, HKV_C, DH_C)
    return (y_prompt, y_sample, new_state_a, new_b_k, new_b_v, new_c_k, new_c_v)
```

```python
import functools
import math

import numpy as np
import jax
import jax.numpy as jnp
from jax import lax
from jax.experimental import pallas as pl
from jax.experimental.pallas import tpu as pltpu

D_MODEL = 1024
BATCH = 32
SEQ = 256
DEPTH = 2
DEC_BATCH = 2
DEC_SEQ = 2048
PAST_LEN = 512
GRID_W = 64
H_A, DK_A, DV_A = 4, 64, 128
GATE_RANK = 16
GATE_TAU = 16.0
H_B, DH_B = 4, 64
DV_B = 2 * DH_B
H_C, HKV_C, DH_C = 8, 2, 128
D_FF = 2816
N_EXPERTS = 8
D_FF_E = 3584
ROPE_THETA = 10000.0
EPS = 1e-6
A_QK = H_A * DK_A
A_V = H_A * DV_A
B_QK = H_B * 2 * DH_B
B_V = H_B * DV_B

F32 = jnp.float32
BF16 = jnp.bfloat16

V7X_VMEM_BYTES = 64 * 1024 * 1024
VMEM_LIMIT = V7X_VMEM_BYTES * 7 // 8
LANES = 128
MXU_N = 256
TM = 256
N_CTX = BATCH * SEQ
N_LAT = DEC_BATCH * DEC_SEQ
M_TOK = N_CTX + N_LAT
NB = M_TOK // TM
NCB = N_CTX // TM
LBB = DEC_SEQ // TM
N_COND = 8
GLA_C = 64
GLA_SB = 8
TME = 512
NT_E = 2 * M_TOK // TME + N_EXPERTS
NC_E = 2
TF_E = D_FF_E // NC_E
TQ_LAT = 256

assert SEQ == TM and DEC_SEQ % TM == 0 and DEPTH == 2


def _cparams(sem):
    return pltpu.CompilerParams(dimension_semantics=sem, vmem_limit_bytes=VMEM_LIMIT)


def _group(i):
    return jnp.where(i < NCB, 0, 1 + (i - NCB) // LBB)


def _pos_block(i):
    return jnp.where(i < NCB, 0, (i - NCB) % LBB)


def _mod_spec(chunk):
    return pl.BlockSpec((None, 1, D_MODEL), lambda i, c=chunk: (_group(i), 0, c))


def _row_spec(width, col=0):
    return pl.BlockSpec((TM, width), lambda i, c=col: (i, c))


def _ctx_spec(width):
    return pl.BlockSpec((TM, width), lambda i: (jnp.minimum(i, NCB - 1), 0))


def _ctx_spare_spec(width):
    return pl.BlockSpec((TM, width), lambda i: (jnp.minimum(i, NCB), 0))


def _lat_spec(width):
    return pl.BlockSpec((TM, width), lambda i: (jnp.maximum(i - NCB, 0), 0))


def _const_spec(shape):
    nd = len(shape)
    return pl.BlockSpec(shape, lambda i, nd=nd: (0,) * nd)


def _dot(a, b):
    return jnp.dot(a, b, preferred_element_type=F32)


def _dot_nt(a, b):
    return lax.dot_general(a, b, (((1,), (1,)), ((), ())), preferred_element_type=F32)


def _split2(a):
    hi = a.astype(BF16)
    lo = (a - hi.astype(F32)).astype(BF16)
    return hi, lo


def _dot_hi(a, b):
    a_hi, a_lo = _split2(a)
    b_hi, b_lo = _split2(b)
    return _dot(a_hi, b_hi) + (_dot(a_hi, b_lo) + _dot(a_lo, b_hi))


def _silu(x):
    return (0.5 * x) * (1.0 + jnp.tanh(0.5 * x))


def _log_sigmoid(x):
    return jnp.minimum(x, 0.0) - jnp.log(1.0 + jnp.exp(-jnp.abs(x)))


def _rms(x, g):
    return x * lax.rsqrt(jnp.mean(x * x, axis=-1, keepdims=True) + EPS) * g


def _norm_mod(x, gamma, sc, sh):
    return _rms(x, gamma) * (1.0 + sc) + sh


def _rope(x, cos, sin, half):
    lane = lax.broadcasted_iota(jnp.int32, x.shape, 1)
    first = (lane % (2 * half)) < half
    swapped = jnp.where(first, pltpu.roll(x, LANES - half, 1), pltpu.roll(x, half, 1))
    return x * cos + swapped * sin


def _mod_kernel(cond_ref, w_ref, b_ref, o_ref):
    o_ref[...] = _dot_hi(_silu(cond_ref[...]), w_ref[...]) + b_ref[...]


def _modulation(cond, w_mod, b_mod):
    tn = 1536
    return pl.pallas_call(
        _mod_kernel,
        grid=(DEPTH, 6 * D_MODEL // tn),
        in_specs=[
            pl.BlockSpec((N_COND, D_MODEL), lambda l, j: (0, 0)),
            pl.BlockSpec((None, D_MODEL, tn), lambda l, j: (l, 0, j)),
            pl.BlockSpec((None, 1, tn), lambda l, j: (l, 0, j)),
        ],
        out_specs=pl.BlockSpec((None, N_COND, tn), lambda l, j: (l, 0, j)),
        out_shape=jax.ShapeDtypeStruct((DEPTH, N_COND, 6 * D_MODEL), F32),
        compiler_params=_cparams(("arbitrary", "arbitrary")),
        name="modulation",
    )(cond, w_mod, b_mod.reshape(DEPTH, 1, 6 * D_MODEL))


def _inproj_even_kernel(x_ref, gam_ref, sh_ref, sc_ref, w_ref, wg_ref, g2_ref, bg_ref, cos_ref, sin_ref,
                        a_ref, glog_ref, qb_ref, kb_ref, vb_ref, kctx_ref, vctx_ref):
    is_lat = pl.program_id(0) >= NCB
    hb = _norm_mod(x_ref[...], gam_ref[...], sc_ref[...], sh_ref[...]).astype(BF16)
    z = _dot(hb, w_ref[...])
    na = 2 * A_QK + 2 * A_V
    a_ref[...] = z[:, :na]
    gates = _dot(hb, wg_ref[...])
    xg = _dot(gates.astype(BF16), g2_ref[...]) + bg_ref[...]
    glog_ref[...] = _log_sigmoid(xg) * (1.0 / GATE_TAU)
    cos = jnp.where(is_lat, cos_ref[...], 1.0)
    sin = jnp.where(is_lat, sin_ref[...], 0.0)

    kctx_ref[...] = z[:, na + B_QK:na + 2 * B_QK]
    vctx_ref[...] = z[:, na + 2 * B_QK:]
    for j in range(B_QK // LANES):
        lo = na + j * LANES
        qb_ref[:, j * LANES:(j + 1) * LANES] = _rope(z[:, lo:lo + LANES], cos, sin, DH_B // 4).astype(BF16)
        lo = na + B_QK + j * LANES
        kb_ref[:, j * LANES:(j + 1) * LANES] = _rope(z[:, lo:lo + LANES], cos, sin, DH_B // 4).astype(BF16)
    vb_ref[...] = z[:, na + 2 * B_QK:].astype(BF16)


def _inproj_even(x, gamma, mod, w_main, w_gates, g2, bg, cos, sin):
    na = 2 * A_QK + 2 * A_V
    nz = na + 2 * B_QK + B_V
    return pl.pallas_call(
        _inproj_even_kernel,
        grid=(NB,),
        in_specs=[
            _row_spec(D_MODEL), _const_spec((1, D_MODEL)), _mod_spec(0), _mod_spec(1),
            _const_spec((D_MODEL, nz)), _const_spec((D_MODEL, LANES)), _const_spec((LANES, 2 * A_QK)),
            _const_spec((1, 2 * A_QK)),
            pl.BlockSpec((TM, LANES), lambda i: (_pos_block(i), 0)),
            pl.BlockSpec((TM, LANES), lambda i: (_pos_block(i), 0)),
        ],
        out_specs=[_row_spec(na), _row_spec(2 * A_QK), _row_spec(B_QK), _row_spec(B_QK), _row_spec(B_V),
                   _ctx_spare_spec(B_QK), _ctx_spare_spec(B_V)],
        out_shape=[
            jax.ShapeDtypeStruct((M_TOK, na), F32), jax.ShapeDtypeStruct((M_TOK, 2 * A_QK), F32),
            jax.ShapeDtypeStruct((M_TOK, B_QK), BF16), jax.ShapeDtypeStruct((M_TOK, B_QK), BF16),
            jax.ShapeDtypeStruct((M_TOK, B_V), BF16), jax.ShapeDtypeStruct((N_CTX + TM, B_QK), F32),
            jax.ShapeDtypeStruct((N_CTX + TM, B_V), F32),
        ],
        compiler_params=_cparams(("arbitrary",)),
        name="inproj_even",
    )(x, gamma, mod, mod, w_main, w_gates, g2, bg, cos, sin)


def _gla_chunk(q_ref, k_ref, v_ref, g_ref, o_ref, row0, s_ref, a_ref, b_ref, m1_ref, head_mask, same_block, rev):
    rows = pl.ds(row0, GLA_C)
    q, k, v, g = q_ref[rows, :], k_ref[rows, :], v_ref[rows, :], g_ref[rows, :]
    v_b = v.astype(BF16)
    r_i = lax.broadcasted_iota(jnp.int32, (GLA_C, GLA_C), 0)
    c_i = lax.broadcasted_iota(jnp.int32, (GLA_C, GLA_C), 1)
    tri = jnp.where((c_i >= r_i) if rev else (c_i <= r_i), 1.0, 0.0).astype(BF16)
    g_hi = g.astype(BF16)
    g_r = g - g_hi.astype(F32)
    g_mid = g_r.astype(BF16)
    g_lo = (g_r - g_mid.astype(F32)).astype(BF16)
    bc = _dot(tri, g_hi) + (_dot(tri, g_mid) + _dot(tri, g_lo))
    b_ref[...] = bc
    last = bc[0:1] if rev else bc[GLA_C - 1:GLA_C]
    qc = q * jnp.exp(bc)
    kl = k * jnp.exp(last - bc)
    row = lax.broadcasted_iota(jnp.int32, (GLA_C, A_QK), 0)

    def expand(x):
        return jnp.where(head_mask, jnp.concatenate([x] * H_A, axis=0), 0.0).astype(BF16)

    scores = None
    half = GLA_C // 2
    while half >= GLA_SB:
        mids = [bc[b + half:b + half + 1] if rev else bc[b + half - 1:b + half] for b in range(0, GLA_C, 2 * half)]
        ref = mids[0] if len(mids) == 1 else jnp.concatenate(
            [jnp.broadcast_to(m, (2 * half, A_QK)) for m in mids], axis=0)
        upper = (row // half) % 2 == 1
        later, earlier = (jnp.logical_not(upper), upper) if rev else (upper, jnp.logical_not(upper))
        q_l = jnp.where(later, q * jnp.exp(bc - ref), 0.0)
        k_e = jnp.where(earlier, k * jnp.exp(ref - bc), 0.0).astype(BF16)
        term = _dot_nt(expand(q_l), k_e)
        if 2 * half < GLA_C:
            term = jnp.where(same_block[2 * half], term, 0.0)
        scores = term if scores is None else scores + term
        half //= 2
    s_old = s_ref[...]
    inter = _dot(expand(qc), s_old.astype(BF16))
    o_heads = []
    for h in range(H_A):
        hs = slice(h * GLA_C, (h + 1) * GLA_C)
        o_heads.append(inter[hs] + _dot(scores[hs].astype(BF16), v_b[:, h * DV_A:(h + 1) * DV_A]))
    o = jnp.concatenate(o_heads, axis=1)

    t_i = lax.broadcasted_iota(jnp.int32, (GLA_SB, A_QK), 0)
    group = 2 * GLA_SB
    diag = []
    for i0 in range(0, GLA_C, group):
        base = i0 * GLA_SB
        for i in range(i0, i0 + group, GLA_SB):
            q_i = q_ref[pl.ds(row0 + i, GLA_SB), :]
            b_i = b_ref[i:i + GLA_SB, :]
            for j in range(0, GLA_SB, 2):
                pair = []
                for jj in (j, j + 1):
                    k_s = k_ref[pl.ds(row0 + i + jj, 1), :]
                    b_s = b_ref[i + jj:i + jj + 1, :]
                    keep = (t_i <= jj) if rev else (t_i >= jj)
                    pair.append(jnp.where(keep, q_i * k_s * jnp.exp(b_i - b_s), 0.0))
                lo = (i + j) * GLA_SB
                a_ref[lo:lo + 2 * GLA_SB, :] = jnp.concatenate(pair, axis=0).astype(BF16)
        w = _dot(a_ref[base:base + group * GLA_SB, :], m1_ref[...])
        for i in range(i0, i0 + group, GLA_SB):
            acc = None
            for j in range(GLA_SB):
                lo = (i + j) * GLA_SB - base
                term = w[lo:lo + GLA_SB] * v_ref[pl.ds(row0 + i + j, 1), :]
                acc = term if acc is None else acc + term
            diag.append(acc)
    o_ref[rows, :] = o + jnp.concatenate(diag, axis=0)

    t = jnp.transpose(jnp.concatenate([kl, jnp.broadcast_to(last, (GLA_C, A_QK))], axis=0))
    kv = _dot(t[:, :GLA_C].astype(BF16), v_b)
    a_col = jnp.exp(t[:, GLA_C:GLA_C + 1])
    for h in range(H_A):
        hs = slice(h * DK_A, (h + 1) * DK_A)
        s_ref[hs, :] = a_col[hs] * s_old[hs] + kv[hs, h * DV_A:(h + 1) * DV_A]


def _gla_kernel(fblk_ref, bblk_ref, first_ref, init_ref, seq_ref,
                qf_ref, kf_ref, vf_ref, gf_ref, qr_ref, kr_ref, vr_ref, gr_ref, s0_ref, m1_ref,
                of_ref, ob_ref, sfin_ref, sf_ref, sb_ref, af_ref, ab_ref, bf_ref, bb_ref):
    step = pl.program_id(0)

    @pl.when(first_ref[step] == 1)
    def _():
        sf_ref[...] = s0_ref[0]
        sb_ref[...] = s0_ref[1]

    r_h = lax.broadcasted_iota(jnp.int32, (H_A * GLA_C, A_QK), 0) // GLA_C
    c_h = lax.broadcasted_iota(jnp.int32, (H_A * GLA_C, A_QK), 1) // DK_A
    head_mask = r_h == c_h
    t_s = lax.broadcasted_iota(jnp.int32, (H_A * GLA_C, GLA_C), 0) % GLA_C
    s_s = lax.broadcasted_iota(jnp.int32, (H_A * GLA_C, GLA_C), 1)
    same_block = {}
    size = 2 * GLA_SB
    while size < GLA_C:
        same_block[size] = (t_s // size) == (s_s // size)
        size *= 2
    nch = TM // GLA_C

    def body(c, carry):
        _gla_chunk(qf_ref, kf_ref, vf_ref, gf_ref, of_ref, pl.multiple_of(c * GLA_C, GLA_C),
                   sf_ref, af_ref, bf_ref, m1_ref, head_mask, same_block, False)
        _gla_chunk(qr_ref, kr_ref, vr_ref, gr_ref, ob_ref, pl.multiple_of((nch - 1 - c) * GLA_C, GLA_C),
                   sb_ref, ab_ref, bb_ref, m1_ref, head_mask, same_block, True)
        return carry

    lax.fori_loop(0, nch, body, 0)
    sfin_ref[0] = sf_ref[...]
    sfin_ref[1] = sb_ref[...]


def _gla_tables():
    fblk, bblk, first, init, seq = [], [], [], [], []
    for b in range(BATCH):
        fblk.append(b), bblk.append(b), first.append(1), init.append(0), seq.append(b)
    for s in range(DEC_BATCH):
        for j in range(LBB):
            fblk.append(NCB + s * LBB + j)
            bblk.append(NCB + s * LBB + LBB - 1 - j)
            first.append(1 if j == 0 else 0)
            init.append(1 + s)
            seq.append(BATCH + s)
    return [jnp.asarray(np.array(t, np.int32)) for t in (fblk, bblk, first, init, seq)]


def _gla(a_proj, glog, s0_all, m1):
    tables = _gla_tables()
    nsteps = int(tables[0].shape[0])
    nseq = BATCH + DEC_BATCH

    def fmap(col):
        return lambda i, fb, bb, fi, ini, sq: (fb[i], col)

    def rmap(col):
        return lambda i, fb, bb, fi, ini, sq: (bb[i], col)

    gs = pltpu.PrefetchScalarGridSpec(
        num_scalar_prefetch=5,
        grid=(nsteps,),
        in_specs=[
            pl.BlockSpec((TM, A_QK), fmap(0)), pl.BlockSpec((TM, A_QK), fmap(1)),
            pl.BlockSpec((TM, A_V), fmap(1)), pl.BlockSpec((TM, A_QK), fmap(0)),
            pl.BlockSpec((TM, A_QK), rmap(0)), pl.BlockSpec((TM, A_QK), rmap(1)),
            pl.BlockSpec((TM, A_V), rmap(1)), pl.BlockSpec((TM, A_QK), rmap(1)),
            pl.BlockSpec((None, 2, A_QK, DV_A), lambda i, fb, bb, fi, ini, sq: (ini[i], 0, 0, 0)),
            pl.BlockSpec((A_QK, A_V), lambda i, fb, bb, fi, ini, sq: (0, 0)),
        ],
        out_specs=[
            pl.BlockSpec((TM, A_V), fmap(0)), pl.BlockSpec((TM, A_V), rmap(0)),
            pl.BlockSpec((None, 2, A_QK, DV_A), lambda i, fb, bb, fi, ini, sq: (sq[i], 0, 0, 0)),
        ],
        scratch_shapes=[
            pltpu.VMEM((A_QK, DV_A), F32), pltpu.VMEM((A_QK, DV_A), F32),
            pltpu.VMEM((GLA_C * GLA_SB, A_QK), BF16), pltpu.VMEM((GLA_C * GLA_SB, A_QK), BF16),
            pltpu.VMEM((GLA_C, A_QK), F32), pltpu.VMEM((GLA_C, A_QK), F32),
        ],
    )
    return pl.pallas_call(
        _gla_kernel,
        grid_spec=gs,
        out_shape=[
            jax.ShapeDtypeStruct((M_TOK, A_V), F32), jax.ShapeDtypeStruct((M_TOK, A_V), F32),
            jax.ShapeDtypeStruct((nseq, 2, A_QK, DV_A), F32),
        ],
        compiler_params=_cparams(("arbitrary",)),
        name="gla",
    )(*tables, a_proj, a_proj, a_proj, glog, a_proj, a_proj, a_proj, glog, s0_all, m1)


def _softmax_parts(s):
    e = jnp.exp(s - jnp.max(s, axis=-1, keepdims=True))
    return e, 1.0 / jnp.sum(e, axis=-1, keepdims=True)


def _diffattn_body(q_ref, k_ref, v_ref, lq1_ref, lk1_ref, lq2_ref, lk2_ref, gsub_ref, o_ref, lam_init):
    lam = (jnp.exp(jnp.sum(lq1_ref[...] * lk1_ref[...], axis=-1, keepdims=True))
           - jnp.exp(jnp.sum(lq2_ref[...] * lk2_ref[...], axis=-1, keepdims=True)) + lam_init)
    lane = lax.broadcasted_iota(jnp.int32, (q_ref.shape[0], LANES), 1)
    for h in range(H_B):
        hs = slice(h * LANES, (h + 1) * LANES)
        q_h = q_ref[:, hs]
        k_h = k_ref[:, hs]
        zero = jnp.zeros_like(q_h)
        e1, r1 = _softmax_parts(_dot_nt(jnp.where(lane < DH_B, q_h, zero), k_h))
        e2, r2 = _softmax_parts(_dot_nt(jnp.where(lane >= DH_B, q_h, zero), k_h))
        w = e1 * r1 - e2 * (lam * r2)
        o = _dot(w.astype(BF16), v_ref[:, hs])
        o_ref[:, hs] = (_rms(o, gsub_ref[...]) * (1.0 - lam_init)).astype(BF16)


def _diffattn_kernel(q_ref, k_ref, v_ref, lq1, lk1, lq2, lk2, gsub, o_ref, *, lam_init):
    _diffattn_body(q_ref, k_ref, v_ref, lq1, lk1, lq2, lk2, gsub, o_ref, lam_init)


def _diffattn(qb, kb, vb, k_lat, v_lat, lam_params, gsub, lam_init):
    small = [pl.BlockSpec((1, DH_B), lambda *_: (0, 0))] * 4 + [pl.BlockSpec((1, DV_B), lambda *_: (0, 0))]
    body = functools.partial(_diffattn_kernel, lam_init=lam_init)
    ctx = pl.pallas_call(
        body,
        grid=(NCB,),
        in_specs=[_row_spec(B_QK), _row_spec(B_QK), _row_spec(B_V)] + small,
        out_specs=_row_spec(B_V),
        out_shape=jax.ShapeDtypeStruct((N_CTX, B_V), BF16),
        compiler_params=_cparams(("arbitrary",)),
        name="diffattn_ctx",
    )(qb, kb, vb, *lam_params, gsub)
    nk = DEC_SEQ + PAST_LEN
    nqb = DEC_SEQ // TQ_LAT
    off = N_CTX // TQ_LAT
    lat = pl.pallas_call(
        body,
        grid=(DEC_BATCH, nqb),
        in_specs=[
            pl.BlockSpec((TQ_LAT, B_QK), lambda b, j: (off + b * nqb + j, 0)),
            pl.BlockSpec((None, nk, B_QK), lambda b, j: (b, 0, 0)),
            pl.BlockSpec((None, nk, B_V), lambda b, j: (b, 0, 0)),
        ] + small,
        out_specs=pl.BlockSpec((TQ_LAT, B_V), lambda b, j: (b * nqb + j, 0)),
        out_shape=jax.ShapeDtypeStruct((N_LAT, B_V), BF16),
        compiler_params=_cparams(("arbitrary", "arbitrary")),
        name="diffattn_lat",
    )(qb, k_lat, v_lat, *lam_params, gsub)
    return ctx, lat


def _pick_rows(ctx_ref, lat_ref):
    return jnp.where(pl.program_id(0) >= NCB, lat_ref[...], ctx_ref[...])


def _postmix_even_kernel(of_ref, ob_ref, ra_ref, actx_ref, alat_ref, x_ref, g1_ref, ggla_ref, wo_ref, o_ref):
    heads = []
    for h in range(H_A):
        hs = slice(h * DV_A, (h + 1) * DV_A)
        heads.append((_rms(of_ref[:, hs] + ob_ref[:, hs], ggla_ref[...]) * _silu(ra_ref[:, hs])).astype(BF16))
    mix = jnp.concatenate(heads + [_pick_rows(actx_ref, alat_ref)], axis=1)
    o_ref[...] = x_ref[...] + g1_ref[...] * _dot(mix, wo_ref[...])


def _postmix_even(o_f, o_b, a_proj, attn_ctx, attn_lat, x, mod, g_gla, w_out):
    return pl.pallas_call(
        _postmix_even_kernel,
        grid=(NB,),
        in_specs=[_row_spec(A_V), _row_spec(A_V), _row_spec(A_V, 2), _ctx_spec(B_V), _lat_spec(B_V),
                  _row_spec(D_MODEL), _mod_spec(2), _const_spec((1, DV_A)), _const_spec((A_V + B_V, D_MODEL))],
        out_specs=_row_spec(D_MODEL),
        out_shape=jax.ShapeDtypeStruct((M_TOK, D_MODEL), F32),
        compiler_params=_cparams(("arbitrary",)),
        name="postmix_even",
    )(o_f, o_b, a_proj, attn_ctx, attn_lat, x, mod, g_gla, w_out)


def _swiglu_act(xb, wg_ref, wu_ref, act_ref):
    for lo in range(0, act_ref.shape[1], MXU_N):
        cs = slice(lo, lo + MXU_N)
        act_ref[:, cs] = (_silu(_dot(xb, wg_ref[:, cs])) * _dot(xb, wu_ref[:, cs])).astype(BF16)


def _ffn_even_kernel(x_ref, gam_ref, sh_ref, sc_ref, g2_ref, wg_ref, wu_ref, wd_ref, o_ref, act_ref):
    x = x_ref[...]
    hb = _norm_mod(x, gam_ref[...], sc_ref[...], sh_ref[...]).astype(BF16)
    _swiglu_act(hb, wg_ref, wu_ref, act_ref)
    o_ref[...] = x + g2_ref[...] * _dot(act_ref[...], wd_ref[...])


def _ffn_even(x, gamma, mod, w_gate, w_up, w_down):
    return pl.pallas_call(
        _ffn_even_kernel,
        grid=(NB,),
        in_specs=[_row_spec(D_MODEL), _const_spec((1, D_MODEL)), _mod_spec(3), _mod_spec(4), _mod_spec(5),
                  _const_spec((D_MODEL, D_FF)), _const_spec((D_MODEL, D_FF)), _const_spec((D_FF, D_MODEL))],
        out_specs=_row_spec(D_MODEL),
        out_shape=jax.ShapeDtypeStruct((M_TOK, D_MODEL), F32),
        scratch_shapes=[pltpu.VMEM((TM, D_FF), BF16)],
        compiler_params=_cparams(("arbitrary",)),
        name="ffn_even",
    )(x, gamma, mod, mod, mod, w_gate, w_up, w_down)


def _inproj_odd_kernel(x_ref, gam_ref, sh_ref, sc_ref, w_ref, gq_ref, gk_ref, cos_ref, sin_ref,
                       q_ref, k_ref, v_ref, kctx_ref, vctx_ref):
    is_lat = pl.program_id(0) >= NCB
    hb = _norm_mod(x_ref[...], gam_ref[...], sc_ref[...], sh_ref[...]).astype(BF16)
    z = _dot(hb, w_ref[...])
    cos = jnp.where(is_lat, cos_ref[...], 1.0)
    sin = jnp.where(is_lat, sin_ref[...], 0.0)
    scale = DH_C ** -0.5
    for h in range(H_C):
        hs = slice(h * DH_C, (h + 1) * DH_C)
        q_ref[:, hs] = (_rope(_rms(z[:, hs], gq_ref[...]), cos, sin, DH_C // 4) * scale).astype(BF16)
    nq = H_C * DH_C
    for h in range(HKV_C):
        hs = slice(h * DH_C, (h + 1) * DH_C)
        k_n = _rms(z[:, nq + h * DH_C:nq + (h + 1) * DH_C], gk_ref[...])
        kctx_ref[:, hs] = k_n
        k_ref[:, hs] = _rope(k_n, cos, sin, DH_C // 4).astype(BF16)
    nkv = HKV_C * DH_C
    v = z[:, nq + nkv:]
    vctx_ref[...] = v
    v_ref[...] = v.astype(BF16)


def _inproj_odd(x, gamma, mod, w_in, g_q, g_k, cos, sin):
    nq, nkv = H_C * DH_C, HKV_C * DH_C
    return pl.pallas_call(
        _inproj_odd_kernel,
        grid=(NB,),
        in_specs=[
            _row_spec(D_MODEL), _const_spec((1, D_MODEL)), _mod_spec(0), _mod_spec(1),
            _const_spec((D_MODEL, nq + 2 * nkv)), _const_spec((1, DH_C)), _const_spec((1, DH_C)),
            pl.BlockSpec((TM, LANES), lambda i: (_pos_block(i), 0)),
            pl.BlockSpec((TM, LANES), lambda i: (_pos_block(i), 0)),
        ],
        out_specs=[_row_spec(nq), _row_spec(nkv), _row_spec(nkv), _ctx_spare_spec(nkv), _ctx_spare_spec(nkv)],
        out_shape=[
            jax.ShapeDtypeStruct((M_TOK, nq), BF16), jax.ShapeDtypeStruct((M_TOK, nkv), BF16),
            jax.ShapeDtypeStruct((M_TOK, nkv), BF16), jax.ShapeDtypeStruct((N_CTX + TM, nkv), F32),
            jax.ShapeDtypeStruct((N_CTX + TM, nkv), F32),
        ],
        compiler_params=_cparams(("arbitrary",)),
        name="inproj_odd",
    )(x, gamma, mod, mod, w_in, g_q, g_k, cos, sin)


def _gqa_body(q_ref, k_ref, v_ref, o_ref):
    rep = H_C // HKV_C
    tq = q_ref.shape[0]
    for hk in range(HKV_C):
        ks = slice(hk * DH_C, (hk + 1) * DH_C)
        q_g = jnp.concatenate([q_ref[:, (hk * rep + g) * DH_C:(hk * rep + g + 1) * DH_C] for g in range(rep)],
                              axis=0)
        e, r = _softmax_parts(_dot_nt(q_g, k_ref[:, ks]))
        o = _dot((e * r).astype(BF16), v_ref[:, ks])
        for g in range(rep):
            o_ref[:, (hk * rep + g) * DH_C:(hk * rep + g + 1) * DH_C] = o[g * tq:(g + 1) * tq].astype(BF16)


def _gqa(q, k, v, k_lat, v_lat):
    nq, nkv = H_C * DH_C, HKV_C * DH_C
    ctx = pl.pallas_call(
        _gqa_body,
        grid=(NCB,),
        in_specs=[_row_spec(nq), _row_spec(nkv), _row_spec(nkv)],
        out_specs=_row_spec(nq),
        out_shape=jax.ShapeDtypeStruct((N_CTX, nq), BF16),
        compiler_params=_cparams(("arbitrary",)),
        name="gqa_ctx",
    )(q, k, v)
    nk = DEC_SEQ + PAST_LEN
    nqb = DEC_SEQ // TQ_LAT
    off = N_CTX // TQ_LAT
    lat = pl.pallas_call(
        _gqa_body,
        grid=(DEC_BATCH, nqb),
        in_specs=[
            pl.BlockSpec((TQ_LAT, nq), lambda b, j: (off + b * nqb + j, 0)),
            pl.BlockSpec((None, nk, nkv), lambda b, j: (b, 0, 0)),
            pl.BlockSpec((None, nk, nkv), lambda b, j: (b, 0, 0)),
        ],
        out_specs=pl.BlockSpec((TQ_LAT, nq), lambda b, j: (b * nqb + j, 0)),
        out_shape=jax.ShapeDtypeStruct((N_LAT, nq), BF16),
        compiler_params=_cparams(("arbitrary", "arbitrary")),
        name="gqa_lat",
    )(q, k_lat, v_lat)
    return ctx, lat


def _postmix_odd_kernel(actx_ref, alat_ref, x_ref, g1_ref, wo_ref, gam_ref, sh_ref, sc_ref, wr_ref,
                        x1_ref, h_ref, w1_ref, w2_ref, idx_ref, cnt_ref, run_ref):
    @pl.when(pl.program_id(0) == 0)
    def _():
        run_ref[...] = jnp.zeros_like(run_ref)

    x1 = x_ref[...] + g1_ref[...] * _dot(_pick_rows(actx_ref, alat_ref), wo_ref[...])
    x1_ref[...] = x1
    h = _norm_mod(x1, gam_ref[...], sc_ref[...], sh_ref[...])
    h_ref[...] = h
    logits = _dot_hi(h, wr_ref[...])
    lane = lax.broadcasted_iota(jnp.int32, logits.shape, 1)
    lane_f = lane.astype(F32)
    lg = jnp.where(lane < N_EXPERTS, logits, -jnp.inf)
    m1 = jnp.max(lg, axis=-1, keepdims=True)
    i1 = jnp.min(jnp.where(lg == m1, lane_f, float(LANES)), axis=-1, keepdims=True)
    lg2 = jnp.where(lane_f == i1, -jnp.inf, lg)
    m2 = jnp.max(lg2, axis=-1, keepdims=True)
    i2 = jnp.min(jnp.where(lg2 == m2, lane_f, float(LANES)), axis=-1, keepdims=True)
    e = jnp.exp(m2 - m1)
    w1 = 1.0 / (1.0 + e)
    w1_ref[...] = jnp.broadcast_to(w1, logits.shape)
    w2_ref[...] = jnp.broadcast_to(e * w1, logits.shape)
    oh1 = jnp.where(lane_f == i1, 1.0, 0.0)
    oh2 = jnp.where(lane_f == i2, 1.0, 0.0)
    both = oh1 + oh2
    r_i = lax.broadcasted_iota(jnp.int32, (TM, TM), 0)
    c_i = lax.broadcasted_iota(jnp.int32, (TM, TM), 1)
    before = jnp.where(c_i < r_i, 1.0, 0.0).astype(BF16)
    pref = _dot(before, both.astype(BF16)) + run_ref[0:1, :]
    rank1 = jnp.sum(pref * oh1, axis=-1, keepdims=True)
    rank2 = jnp.sum(pref * oh2, axis=-1, keepdims=True)
    run_ref[...] = run_ref[...] + jnp.sum(both, axis=0, keepdims=True)
    cnt_ref[...] = run_ref[...]
    quarter = LANES // 4
    idx_ref[...] = jnp.where(lane < quarter, i1, jnp.where(lane < 2 * quarter, i2, jnp.where(
        lane < 3 * quarter, rank1, rank2))).astype(jnp.int32)


def _postmix_odd(attn_ctx, attn_lat, x, mod, w_out, gamma, w_router):
    return pl.pallas_call(
        _postmix_odd_kernel,
        grid=(NB,),
        in_specs=[_ctx_spec(H_C * DH_C), _lat_spec(H_C * DH_C), _row_spec(D_MODEL), _mod_spec(2),
                  _const_spec((H_C * DH_C, D_MODEL)),
                  _const_spec((1, D_MODEL)), _mod_spec(3), _mod_spec(4), _const_spec((D_MODEL, LANES))],
        out_specs=[_row_spec(D_MODEL), _row_spec(D_MODEL), _row_spec(LANES), _row_spec(LANES), _row_spec(LANES),
                   _const_spec((8, LANES))],
        out_shape=[
            jax.ShapeDtypeStruct((M_TOK, D_MODEL), F32), jax.ShapeDtypeStruct((M_TOK, D_MODEL), F32),
            jax.ShapeDtypeStruct((M_TOK, LANES), F32), jax.ShapeDtypeStruct((M_TOK, LANES), F32),
            jax.ShapeDtypeStruct((M_TOK, LANES), jnp.int32), jax.ShapeDtypeStruct((8, LANES), F32),
        ],
        scratch_shapes=[pltpu.VMEM((8, LANES), F32)],
        compiler_params=_cparams(("arbitrary",)),
        name="postmix_odd",
    )(attn_ctx, attn_lat, x, mod, w_out, gamma, mod, mod, w_router)


def _cast_rows(src_ref, dst_ref, piece):
    def body(r, carry):
        rows = pl.ds(pl.multiple_of(r * piece, piece), piece)
        dst_ref[rows, :] = src_ref[rows, :].astype(BF16)
        return carry

    lax.fori_loop(0, src_ref.shape[0] // piece, body, 0)


def _experts_up_kernel(tile_ref, chunk_ref, exp_ref, first_ref, valid_ref, x_ref, wg_ref, wu_ref, act_ref,
                       wgb_ref, wub_ref):
    s = pl.program_id(0)

    @pl.when(first_ref[s] == 1)
    def _():
        _cast_rows(wg_ref, wgb_ref, 128)
        _cast_rows(wu_ref, wub_ref, 128)

    @pl.when(valid_ref[s] == 1)
    def _():
        _swiglu_act(x_ref[...].astype(BF16), wgb_ref, wub_ref, act_ref)


def _experts_down_kernel(act_tile_ref, exp_ref, first_ref, valid_ref, act_ref, wd_ref, o_ref, wdb_ref):
    i = pl.program_id(0)

    @pl.when(first_ref[i] == 1)
    def _():
        _cast_rows(wd_ref, wdb_ref, 256)

    @pl.when(valid_ref[i] == 1)
    def _():
        o_ref[...] = _dot(act_ref[...], wdb_ref[...])

    @pl.when(valid_ref[i] == 0)
    def _():
        o_ref[...] = jnp.zeros_like(o_ref)


def _experts(up_tables, down_tables, xs, w_gate, w_up, w_down):
    up = pl.pallas_call(
        _experts_up_kernel,
        grid_spec=pltpu.PrefetchScalarGridSpec(
            num_scalar_prefetch=5,
            grid=(NC_E * NT_E,),
            in_specs=[
                pl.BlockSpec((TME, D_MODEL), lambda s, t, c, e, f, v: (t[s], 0)),
                pl.BlockSpec((None, D_MODEL, TF_E), lambda s, t, c, e, f, v: (e[s], 0, c[s])),
                pl.BlockSpec((None, D_MODEL, TF_E), lambda s, t, c, e, f, v: (e[s], 0, c[s])),
            ],
            out_specs=pl.BlockSpec((TME, TF_E), lambda s, t, c, e, f, v: (t[s], c[s])),
            scratch_shapes=[pltpu.VMEM((D_MODEL, TF_E), BF16), pltpu.VMEM((D_MODEL, TF_E), BF16)],
        ),
        out_shape=jax.ShapeDtypeStruct((NT_E * TME, D_FF_E), BF16),
        compiler_params=_cparams(("arbitrary",)),
        name="experts_up",
    )(*up_tables, xs, w_gate, w_up)
    return pl.pallas_call(
        _experts_down_kernel,
        grid_spec=pltpu.PrefetchScalarGridSpec(
            num_scalar_prefetch=4,
            grid=(NT_E,),
            in_specs=[
                pl.BlockSpec((TME, D_FF_E), lambda i, a, e, f, v: (a[i], 0)),
                pl.BlockSpec((None, D_FF_E, D_MODEL), lambda i, a, e, f, v: (e[i], 0, 0)),
            ],
            out_specs=pl.BlockSpec((TME, D_MODEL), lambda i, a, e, f, v: (i, 0)),
            scratch_shapes=[pltpu.VMEM((D_FF_E, D_MODEL), BF16)],
        ),
        out_shape=jax.ShapeDtypeStruct((NT_E * TME, D_MODEL), F32),
        compiler_params=_cparams(("arbitrary",)),
        name="experts_down",
    )(*down_tables, up, w_down)


def _combine_kernel(x_ref, y1_ref, y2_ref, w1_ref, w2_ref, g2_ref, gam_ref, octx_ref, olat_ref):
    rep = D_MODEL // LANES
    w1 = jnp.concatenate([w1_ref[...]] * rep, axis=1)
    w2 = jnp.concatenate([w2_ref[...]] * rep, axis=1)
    x2 = x_ref[...] + g2_ref[...] * (w1 * y1_ref[...] + w2 * y2_ref[...])
    y = _rms(x2, gam_ref[...])
    is_lat = pl.program_id(0) >= NCB

    @pl.when(jnp.logical_not(is_lat))
    def _():
        octx_ref[...] = y

    @pl.when(is_lat)
    def _():
        olat_ref[...] = y


def _combine(x1, y1, y2, w1, w2, mod, gamma):
    return pl.pallas_call(
        _combine_kernel,
        grid=(NB,),
        in_specs=[_row_spec(D_MODEL), _row_spec(D_MODEL), _row_spec(D_MODEL), _row_spec(LANES), _row_spec(LANES),
                  _mod_spec(5), _const_spec((1, D_MODEL))],
        out_specs=[_ctx_spec(D_MODEL), _lat_spec(D_MODEL)],
        out_shape=[jax.ShapeDtypeStruct((N_CTX, D_MODEL), F32), jax.ShapeDtypeStruct((N_LAT, D_MODEL), F32)],
        compiler_params=_cparams(("arbitrary",)),
        name="combine",
    )(x1, y1, y2, w1, w2, mod, gamma)


def _rope_tables(rot_dim):
    n = DEC_SEQ
    rows = jnp.repeat(jnp.arange(n // GRID_W, dtype=F32), GRID_W)
    cols = jnp.tile(jnp.arange(GRID_W, dtype=F32), n // GRID_W)
    half = rot_dim // 2
    freqs = ROPE_THETA ** (-jnp.arange(0, half, 2, dtype=F32) / half)
    ang_r, ang_c = rows[:, None] * freqs, cols[:, None] * freqs
    cos = jnp.concatenate([jnp.cos(ang_r)] * 2 + [jnp.cos(ang_c)] * 2, axis=-1)
    sin = jnp.concatenate([-jnp.sin(ang_r), jnp.sin(ang_r), -jnp.sin(ang_c), jnp.sin(ang_c)], axis=-1)
    rep = LANES // rot_dim
    return jnp.tile(cos, (1, rep)), jnp.tile(sin, (1, rep))


def _rows(a, idx):
    return a.at[idx].get(mode="promise_in_bounds")


def _route(idx, counts_f):
    quarter = LANES // 4
    experts = jnp.arange(N_EXPERTS, dtype=jnp.int32)
    e2 = jnp.stack([idx[:, 0], idx[:, quarter]], axis=1)
    rank2 = jnp.stack([idx[:, 2 * quarter], idx[:, 3 * quarter]], axis=1)
    counts = counts_f[0, :N_EXPERTS].astype(jnp.int32)
    tiles = (counts + TME - 1) // TME
    tile_end = jnp.cumsum(tiles)
    tile_start = tile_end - tiles
    start = jnp.cumsum(counts) - counts
    onehot = e2[:, :, None] == experts[None, None, :]
    pos = jnp.sum(jnp.where(onehot, tile_start[None, None, :] * TME, 0), axis=-1) + rank2
    tile_id = jnp.arange(NT_E, dtype=jnp.int32)
    tile_expert = jnp.minimum(jnp.sum((tile_id[:, None] >= tile_end[None, :]).astype(jnp.int32), axis=1),
                              N_EXPERTS - 1).astype(jnp.int32)
    tile_valid = (tile_id < tile_end[-1]).astype(jnp.int32)
    t_oh = tile_expert[:, None] == experts[None, :]
    t_first = jnp.sum(jnp.where(t_oh, (start - tile_start * TME)[None, :], 0), axis=1) + tile_id * TME
    t_last = jnp.sum(jnp.where(t_oh, (start + counts)[None, :], 0), axis=1)
    order = jnp.argsort(e2.reshape(-1), stable=True).astype(jnp.int32)
    n_assign = order.shape[0]
    g_idx = t_first[:, None] + jnp.arange(TME, dtype=jnp.int32)[None, :]
    live = jnp.logical_and(g_idx < t_last[:, None], tile_valid[:, None] == 1)
    src = _rows(order, jnp.clip(g_idx, 0, n_assign - 1).reshape(-1)) // 2
    src_tok = jnp.where(live.reshape(-1), src, 0).astype(jnp.int32)

    n_tiles = tile_end[-1]
    last_tile = jnp.maximum(n_tiles - 1, 0)
    last_expert = jnp.sum(jnp.where(tile_id == last_tile, tile_expert, 0))
    t_tstart = jnp.sum(jnp.where(t_oh, tile_start[None, :], 0), axis=1)
    down_tables = (
        jnp.where(tile_valid == 1, tile_id, last_tile).astype(jnp.int32),
        tile_expert,
        jnp.logical_and(tile_valid == 1, tile_id == t_tstart).astype(jnp.int32),
        tile_valid,
    )
    s_id = jnp.arange(NC_E * NT_E, dtype=jnp.int32)
    s_exp = jnp.minimum(jnp.sum((s_id[:, None] >= NC_E * tile_end[None, :]).astype(jnp.int32), axis=1),
                        N_EXPERTS - 1)
    s_oh = s_exp[:, None] == experts[None, :]
    s_tstart = jnp.sum(jnp.where(s_oh, tile_start[None, :], 0), axis=1)
    s_ntile = jnp.maximum(jnp.sum(jnp.where(s_oh, tiles[None, :], 0), axis=1), 1)
    rel = s_id - NC_E * s_tstart
    s_valid = s_id < NC_E * n_tiles
    up_tables = (
        jnp.where(s_valid, s_tstart + rel % s_ntile, last_tile).astype(jnp.int32),
        jnp.where(s_valid, rel // s_ntile, NC_E - 1).astype(jnp.int32),
        jnp.where(s_valid, s_exp, last_expert).astype(jnp.int32),
        jnp.logical_and(s_valid, rel % s_ntile == 0).astype(jnp.int32),
        s_valid.astype(jnp.int32),
    )
    return src_tok, pos, up_tables, down_tables


def lambda_init(layer):
    return 0.8 - 0.6 * math.exp(-0.3 * layer)


def kernel(x_prompt, x_sample, state_a, cache_b_k, cache_b_v, cache_c_k, cache_c_v, c, c_ctx, w_mod, b_mod, norm_mix, norm_ffn, w_in_even, w_gate2_a, b_gate_a, g_gla, lam_q1, lam_k1, lam_q2, lam_k2, g_sub_b, w_out_even, w_in_odd, g_q_c, g_k_c, w_out_odd, ffn_gate, ffn_up, ffn_down, w_router, exp_gate, exp_up, exp_down, norm_final):
    x = jnp.concatenate([x_prompt.reshape(N_CTX, D_MODEL), x_sample.reshape(N_LAT, D_MODEL)], axis=0)
    cond = jnp.concatenate([c_ctx[None, :], c, jnp.zeros((N_COND - 1 - DEC_BATCH, D_MODEL), F32)], axis=0)
    mod = _modulation(cond, w_mod, b_mod).reshape(DEPTH, N_COND, 1, 6 * D_MODEL)

    w = w_in_even[0]
    na = 2 * A_QK + 2 * A_V
    gate_lo = na
    gate_hi = na + 2 * GATE_RANK
    col_scale = jnp.concatenate([jnp.full((A_QK,), DK_A ** -0.5, F32), jnp.ones((na - A_QK,), F32),
                                 jnp.full((B_QK,), DH_B ** -0.5, F32), jnp.ones((B_QK + B_V,), F32)])
    w_main = (jnp.concatenate([w[:, :gate_lo], w[:, gate_hi:]], axis=1) * col_scale).astype(BF16)
    w_gates = jnp.pad(w[:, gate_lo:gate_hi], ((0, 0), (0, LANES - 2 * GATE_RANK))).astype(BF16)
    g2 = jnp.zeros((LANES, 2 * A_QK), F32)
    g2 = g2.at[:GATE_RANK, :A_QK].set(w_gate2_a[0, 0]).at[GATE_RANK:2 * GATE_RANK, A_QK:].set(w_gate2_a[0, 1])
    bg = b_gate_a[0].reshape(1, 2 * A_QK)
    cos_b, sin_b = _rope_tables(DH_B)
    a_proj, glog, qb, kb, vb, kb_ctx, vb_ctx = _inproj_even(
        x, norm_mix[0][None, :], mod[0], w_main, w_gates, g2.astype(BF16), bg, cos_b, sin_b)

    s0_all = jnp.concatenate([jnp.zeros((1, 2, A_QK, DV_A), F32),
                              state_a[:, 0].reshape(DEC_BATCH, 2, A_QK, DV_A)], axis=0)
    hd = np.arange(A_QK)[:, None] // DK_A == np.arange(A_V)[None, :] // DV_A
    m1 = jnp.asarray(hd.astype(np.float32)).astype(BF16)
    o_f, o_b, s_fin = _gla(a_proj, glog, s0_all, m1)

    k_lat = jnp.concatenate([kb[N_CTX:].reshape(DEC_BATCH, DEC_SEQ, B_QK),
                             cache_b_k[:, 0].reshape(DEC_BATCH, PAST_LEN, B_QK).astype(BF16)], axis=1)
    v_lat = jnp.concatenate([vb[N_CTX:].reshape(DEC_BATCH, DEC_SEQ, B_V),
                             cache_b_v[:, 0].reshape(DEC_BATCH, PAST_LEN, B_V).astype(BF16)], axis=1)
    lam_params = [p[0][None, :] for p in (lam_q1, lam_k1, lam_q2, lam_k2)]
    attn_b = _diffattn(qb, kb, vb, k_lat, v_lat, lam_params, g_sub_b[0][None, :], lambda_init(0))

    x = _postmix_even(o_f, o_b, a_proj, *attn_b, x, mod[0], g_gla[0][None, :], w_out_even[0].astype(BF16))
    x = _ffn_even(x, norm_ffn[0][None, :], mod[0], ffn_gate[0].astype(BF16), ffn_up[0].astype(BF16),
                  ffn_down[0].astype(BF16))

    cos_c, sin_c = _rope_tables(DH_C)
    q_c, k_c, v_c, kc_ctx, vc_ctx = _inproj_odd(x, norm_mix[1][None, :], mod[1], w_in_odd[0].astype(BF16),
                                     g_q_c[0][None, :], g_k_c[0][None, :], cos_c, sin_c)
    nkv = HKV_C * DH_C
    k_lat = jnp.concatenate([k_c[N_CTX:].reshape(DEC_BATCH, DEC_SEQ, nkv),
                             cache_c_k[:, 0].reshape(DEC_BATCH, PAST_LEN, nkv).astype(BF16)], axis=1)
    v_lat = jnp.concatenate([v_c[N_CTX:].reshape(DEC_BATCH, DEC_SEQ, nkv),
                             cache_c_v[:, 0].reshape(DEC_BATCH, PAST_LEN, nkv).astype(BF16)], axis=1)
    attn_c = _gqa(q_c, k_c, v_c, k_lat, v_lat)
    w_r = jnp.pad(w_router[0], ((0, 0), (0, LANES - N_EXPERTS)))
    x1, h_moe, w1, w2, idx, counts = _postmix_odd(*attn_c, x, mod[1], w_out_odd[0].astype(BF16),
                                                  norm_ffn[1][None, :], w_r)

    src_tok, pos, up_tables, down_tables = _route(idx, counts)
    xs = _rows(h_moe, src_tok)
    ys = _experts(up_tables, down_tables, xs, exp_gate[0], exp_up[0], exp_down[0])
    y1 = _rows(ys, pos[:, 0])
    y2 = _rows(ys, pos[:, 1])
    y_ctx, y_lat = _combine(x1, y1, y2, w1, w2, mod[1], norm_final[None, :])

    y_prompt = y_ctx.reshape(BATCH, SEQ, D_MODEL)
    y_sample = y_lat.reshape(DEC_BATCH, DEC_SEQ, D_MODEL)
    new_state_a = s_fin[:BATCH].reshape(BATCH, 1, 2, H_A, DK_A, DV_A)
    new_b_k = kb_ctx[:N_CTX].reshape(BATCH, 1, SEQ, H_B, 2, DH_B)
    new_b_v = vb_ctx[:N_CTX].reshape(BATCH, 1, SEQ, H_B, DV_B)
    new_c_k = kc_ctx[:N_CTX].reshape(BATCH, 1, SEQ, HKV_C, DH_C)
    new_c_v = vc_ctx[:N_CTX].reshape(BATCH, 1, SEQ, HKV_C, DH_C)
    return (y_prompt, y_sample, new_state_a, new_b_k, new_b_v, new_c_k, new_c_v)
```

```python
import functools
import math

import numpy as np
import jax
import jax.numpy as jnp
from jax import lax
from jax.experimental import pallas as pl
from jax.experimental.pallas import tpu as pltpu

D_MODEL = 1024
BATCH = 32
SEQ = 256
DEPTH = 2
DEC_BATCH = 2
DEC_SEQ = 2048
PAST_LEN = 512
GRID_W = 64
H_A, DK_A, DV_A = 4, 64, 128
GATE_RANK = 16
GATE_TAU = 16.0
H_B, DH_B = 4, 64
DV_B = 2 * DH_B
H_C, HKV_C, DH_C = 8, 2, 128
D_FF = 2816
N_EXPERTS = 8
D_FF_E = 3584
ROPE_THETA = 10000.0
EPS = 1e-6
A_QK = H_A * DK_A
A_V = H_A * DV_A
B_QK = H_B * 2 * DH_B
B_V = H_B * DV_B

F32 = jnp.float32
BF16 = jnp.bfloat16
LOG2E = math.log2(math.e)

V7X_VMEM_BYTES = 64 * 1024 * 1024
VMEM_LIMIT = V7X_VMEM_BYTES * 7 // 8
LANES = 128
MXU_N = 256
TM = 256
N_CTX = BATCH * SEQ
N_LAT = DEC_BATCH * DEC_SEQ
M_TOK = N_CTX + N_LAT
NB = M_TOK // TM
NCB = N_CTX // TM
LBB = DEC_SEQ // TM
N_COND = 8
GLA_C = 64
GLA_SB = 8
TME = 512
NT_E = 2 * M_TOK // TME + N_EXPERTS
NC_E = 2
TF_E = D_FF_E // NC_E
TQ_LAT = 256

assert SEQ == TM and DEC_SEQ % TM == 0 and DEPTH == 2


def _cparams(sem):
    return pltpu.CompilerParams(dimension_semantics=sem, vmem_limit_bytes=VMEM_LIMIT)


def _group(i):
    return jnp.where(i < NCB, 0, 1 + (i - NCB) // LBB)


def _pos_block(i):
    return jnp.where(i < NCB, 0, (i - NCB) % LBB)


def _mod_spec(chunk):
    return pl.BlockSpec((None, 1, D_MODEL), lambda i, c=chunk: (_group(i), 0, c))


def _row_spec(width, col=0):
    return pl.BlockSpec((TM, width), lambda i, c=col: (i, c))


def _ctx_spec(width):
    return pl.BlockSpec((TM, width), lambda i: (jnp.minimum(i, NCB - 1), 0))


def _lat_spec(width):
    return pl.BlockSpec((TM, width), lambda i: (jnp.maximum(i - NCB, 0), 0))


def _const_spec(shape):
    nd = len(shape)
    return pl.BlockSpec(shape, lambda i, nd=nd: (0,) * nd)


def _dot(a, b):
    return jnp.dot(a, b, preferred_element_type=F32)


def _dot_nt(a, b):
    return lax.dot_general(a, b, (((1,), (1,)), ((), ())), preferred_element_type=F32)


def _split2(a):
    hi = a.astype(BF16)
    lo = (a - hi.astype(F32)).astype(BF16)
    return hi, lo


def _dot_hi(a, b):
    a_hi, a_lo = _split2(a)
    b_hi, b_lo = _split2(b)
    return _dot(a_hi, b_hi) + (_dot(a_hi, b_lo) + _dot(a_lo, b_hi))


def _silu(x):
    return (0.5 * x) * (1.0 + jnp.tanh(0.5 * x))


def _log_sigmoid(x):
    return jnp.minimum(x, 0.0) - jnp.log(1.0 + jnp.exp(-jnp.abs(x)))


def _rms(x, g):
    return x * lax.rsqrt(jnp.mean(x * x, axis=-1, keepdims=True) + EPS) * g


def _norm_mod(x, gamma, sc, sh):
    return _rms(x, gamma) * (1.0 + sc) + sh


def _pick_rows(ctx_ref, lat_ref):
    return jnp.where(pl.program_id(0) >= NCB, lat_ref[...], ctx_ref[...])


def _copy_ctx(stage_ref, outs):
    piece = 64

    @pl.when(pl.program_id(0) < NCB)
    def _():
        def body(r, carry):
            rows = pl.ds(pl.multiple_of(r * piece, piece), piece)
            for lo, dst_ref in outs:
                dst_ref[rows, :] = stage_ref[rows, lo:lo + dst_ref.shape[1]]
            return carry

        lax.fori_loop(0, stage_ref.shape[0] // piece, body, 0)


def _rope(x, cos, sin, half):
    lane = lax.broadcasted_iota(jnp.int32, x.shape, 1)
    first = (lane % (2 * half)) < half
    swapped = jnp.where(first, pltpu.roll(x, LANES - half, 1), pltpu.roll(x, half, 1))
    return x * cos + swapped * sin


def _mod_kernel(cond_ref, w_ref, b_ref, o_ref):
    o_ref[...] = _dot_hi(_silu(cond_ref[...]), w_ref[...]) + b_ref[...]


def _modulation(cond, w_mod, b_mod):
    tn = 1536
    return pl.pallas_call(
        _mod_kernel,
        grid=(DEPTH, 6 * D_MODEL // tn),
        in_specs=[
            pl.BlockSpec((N_COND, D_MODEL), lambda l, j: (0, 0)),
            pl.BlockSpec((None, D_MODEL, tn), lambda l, j: (l, 0, j)),
            pl.BlockSpec((None, 1, tn), lambda l, j: (l, 0, j)),
        ],
        out_specs=pl.BlockSpec((None, N_COND, tn), lambda l, j: (l, 0, j)),
        out_shape=jax.ShapeDtypeStruct((DEPTH, N_COND, 6 * D_MODEL), F32),
        compiler_params=_cparams(("arbitrary", "arbitrary")),
        name="modulation",
    )(cond, w_mod, b_mod.reshape(DEPTH, 1, 6 * D_MODEL))


def _inproj_even_kernel(xc_ref, xl_ref, gam_ref, sh_ref, sc_ref, w_ref, wg_ref, g2_ref, bg_ref, cos_ref, sin_ref,
                        a_ref, glog_ref, qb_ref, kb_ref, vb_ref, kctx_ref, vctx_ref, stage_ref):
    is_lat = pl.program_id(0) >= NCB
    hb = _norm_mod(_pick_rows(xc_ref, xl_ref), gam_ref[...], sc_ref[...], sh_ref[...]).astype(BF16)
    z = _dot(hb, w_ref[...])
    na = 2 * A_QK + 2 * A_V
    a_ref[...] = z[:, :na]
    gates = _dot(hb, wg_ref[...])
    xg = _dot(gates.astype(BF16), g2_ref[...]) + bg_ref[...]
    glog_ref[...] = _log_sigmoid(xg) * (1.0 / GATE_TAU)
    cos = jnp.where(is_lat, cos_ref[...], 1.0)
    sin = jnp.where(is_lat, sin_ref[...], 0.0)

    stage_ref[...] = z[:, na + B_QK:]
    _copy_ctx(stage_ref, ((0, kctx_ref), (B_QK, vctx_ref)))
    for j in range(B_QK // LANES):
        lo = na + j * LANES
        qb_ref[:, j * LANES:(j + 1) * LANES] = (_rope(z[:, lo:lo + LANES], cos, sin, DH_B // 4) * LOG2E).astype(BF16)
        lo = na + B_QK + j * LANES
        kb_ref[:, j * LANES:(j + 1) * LANES] = _rope(z[:, lo:lo + LANES], cos, sin, DH_B // 4).astype(BF16)
    vb_ref[...] = z[:, na + 2 * B_QK:].astype(BF16)


def _inproj_even(x_ctx, x_lat, gamma, mod, w_main, w_gates, g2, bg, cos, sin):
    na = 2 * A_QK + 2 * A_V
    nz = na + 2 * B_QK + B_V
    return pl.pallas_call(
        _inproj_even_kernel,
        grid=(NB,),
        in_specs=[
            _ctx_spec(D_MODEL), _lat_spec(D_MODEL), _const_spec((1, D_MODEL)), _mod_spec(0), _mod_spec(1),
            _const_spec((D_MODEL, nz)), _const_spec((D_MODEL, LANES)), _const_spec((LANES, 2 * A_QK)),
            _const_spec((1, 2 * A_QK)),
            pl.BlockSpec((TM, LANES), lambda i: (_pos_block(i), 0)),
            pl.BlockSpec((TM, LANES), lambda i: (_pos_block(i), 0)),
        ],
        out_specs=[_row_spec(na), _row_spec(2 * A_QK), _row_spec(B_QK), _row_spec(B_QK), _row_spec(B_V),
                   _ctx_spec(B_QK), _ctx_spec(B_V)],
        out_shape=[
            jax.ShapeDtypeStruct((M_TOK, na), F32), jax.ShapeDtypeStruct((M_TOK, 2 * A_QK), F32),
            jax.ShapeDtypeStruct((M_TOK, B_QK), BF16), jax.ShapeDtypeStruct((M_TOK, B_QK), BF16),
            jax.ShapeDtypeStruct((M_TOK, B_V), BF16), jax.ShapeDtypeStruct((N_CTX, B_QK), F32),
            jax.ShapeDtypeStruct((N_CTX, B_V), F32),
        ],
        scratch_shapes=[pltpu.VMEM((TM, B_QK + B_V), F32)],
        compiler_params=_cparams(("arbitrary",)),
        name="inproj_even",
    )(x_ctx, x_lat, gamma, mod, mod, w_main, w_gates, g2, bg, cos, sin)


def _gla_chunk(q_ref, k_ref, v_ref, g_ref, o_ref, c, s_ref, a_ref, b_ref, m1_ref, head_mask, same_block, rev):
    q, k, v, g = q_ref[c], k_ref[c], v_ref[c], g_ref[c]
    v_b = v.astype(BF16)
    r_i = lax.broadcasted_iota(jnp.int32, (GLA_C, GLA_C), 0)
    c_i = lax.broadcasted_iota(jnp.int32, (GLA_C, GLA_C), 1)
    tri = jnp.where((c_i >= r_i) if rev else (c_i <= r_i), 1.0, 0.0).astype(BF16)
    g_hi = g.astype(BF16)
    g_r = g - g_hi.astype(F32)
    g_mid = g_r.astype(BF16)
    g_lo = (g_r - g_mid.astype(F32)).astype(BF16)
    bc = _dot(tri, g_hi) + (_dot(tri, g_mid) + _dot(tri, g_lo))
    b_ref[...] = bc
    last = bc[0:1] if rev else bc[GLA_C - 1:GLA_C]
    qc = q * jnp.exp(bc)
    kl = k * jnp.exp(last - bc)
    row = lax.broadcasted_iota(jnp.int32, (GLA_C, A_QK), 0)

    def expand(x):
        return jnp.where(head_mask, jnp.concatenate([x] * H_A, axis=0), 0.0).astype(BF16)

    scores = None
    half = GLA_C // 2
    while half >= GLA_SB:
        mids = [bc[b + half:b + half + 1] if rev else bc[b + half - 1:b + half] for b in range(0, GLA_C, 2 * half)]
        ref = mids[0] if len(mids) == 1 else jnp.concatenate(
            [jnp.broadcast_to(m, (2 * half, A_QK)) for m in mids], axis=0)
        upper = (row // half) % 2 == 1
        later, earlier = (jnp.logical_not(upper), upper) if rev else (upper, jnp.logical_not(upper))
        q_l = jnp.where(later, q * jnp.exp(bc - ref), 0.0)
        k_e = jnp.where(earlier, k * jnp.exp(ref - bc), 0.0).astype(BF16)
        term = _dot_nt(expand(q_l), k_e)
        if 2 * half < GLA_C:
            term = jnp.where(same_block[2 * half], term, 0.0)
        scores = term if scores is None else scores + term
        half //= 2
    s_old = s_ref[...]
    inter = _dot(expand(qc), s_old.astype(BF16))
    o_heads = []
    for h in range(H_A):
        hs = slice(h * GLA_C, (h + 1) * GLA_C)
        o_heads.append(inter[hs] + _dot(scores[hs].astype(BF16), v_b[:, h * DV_A:(h + 1) * DV_A]))
    o = jnp.concatenate(o_heads, axis=1)

    t_i = lax.broadcasted_iota(jnp.int32, (GLA_SB, A_QK), 0)
    group = 2 * GLA_SB
    diag = []
    for i0 in range(0, GLA_C, group):
        base = i0 * GLA_SB
        for i in range(i0, i0 + group, GLA_SB):
            q_i = q_ref[c, i:i + GLA_SB, :]
            b_i = b_ref[i:i + GLA_SB, :]
            for j in range(0, GLA_SB, 2):
                pair = []
                for jj in (j, j + 1):
                    k_s = k_ref[c, i + jj:i + jj + 1, :]
                    b_s = b_ref[i + jj:i + jj + 1, :]
                    keep = (t_i <= jj) if rev else (t_i >= jj)
                    pair.append(jnp.where(keep, q_i * k_s * jnp.exp(b_i - b_s), 0.0))
                lo = (i + j) * GLA_SB
                a_ref[lo:lo + 2 * GLA_SB, :] = jnp.concatenate(pair, axis=0).astype(BF16)
        w = _dot(a_ref[base:base + group * GLA_SB, :], m1_ref[...])
        for i in range(i0, i0 + group, GLA_SB):
            acc = None
            for j in range(GLA_SB):
                lo = (i + j) * GLA_SB - base
                term = w[lo:lo + GLA_SB] * v_ref[c, i + j:i + j + 1, :]
                acc = term if acc is None else acc + term
            diag.append(acc)
    o_ref[c] = o + jnp.concatenate(diag, axis=0)

    t = jnp.transpose(jnp.concatenate([kl, jnp.broadcast_to(last, (GLA_C, A_QK))], axis=0))
    kv = _dot(t[:, :GLA_C].astype(BF16), v_b)
    a_col = jnp.exp(t[:, GLA_C:GLA_C + 1])
    for h in range(H_A):
        hs = slice(h * DK_A, (h + 1) * DK_A)
        s_ref[hs, :] = a_col[hs] * s_old[hs] + kv[hs, h * DV_A:(h + 1) * DV_A]


def _gla_kernel(fblk_ref, bblk_ref, first_ref, init_ref, seq_ref,
                qf_ref, kf_ref, vf_ref, gf_ref, qr_ref, kr_ref, vr_ref, gr_ref, s0_ref, m1_ref,
                of_ref, ob_ref, sfin_ref, sf_ref, sb_ref, af_ref, ab_ref, bf_ref, bb_ref):
    step = pl.program_id(0)

    @pl.when(first_ref[step] == 1)
    def _():
        sf_ref[...] = s0_ref[0]
        sb_ref[...] = s0_ref[1]

    r_h = lax.broadcasted_iota(jnp.int32, (H_A * GLA_C, A_QK), 0) // GLA_C
    c_h = lax.broadcasted_iota(jnp.int32, (H_A * GLA_C, A_QK), 1) // DK_A
    head_mask = r_h == c_h
    t_s = lax.broadcasted_iota(jnp.int32, (H_A * GLA_C, GLA_C), 0) % GLA_C
    s_s = lax.broadcasted_iota(jnp.int32, (H_A * GLA_C, GLA_C), 1)
    same_block = {}
    size = 2 * GLA_SB
    while size < GLA_C:
        same_block[size] = (t_s // size) == (s_s // size)
        size *= 2
    nch = TM // GLA_C

    def body(c, carry):
        _gla_chunk(qf_ref, kf_ref, vf_ref, gf_ref, of_ref, c,
                   sf_ref, af_ref, bf_ref, m1_ref, head_mask, same_block, False)
        _gla_chunk(qr_ref, kr_ref, vr_ref, gr_ref, ob_ref, nch - 1 - c,
                   sb_ref, ab_ref, bb_ref, m1_ref, head_mask, same_block, True)
        return carry

    lax.fori_loop(0, nch, body, 0)
    sfin_ref[0] = sf_ref[...]
    sfin_ref[1] = sb_ref[...]


def _gla_tables():
    fblk, bblk, first, init, seq = [], [], [], [], []
    for b in range(BATCH):
        fblk.append(b), bblk.append(b), first.append(1), init.append(0), seq.append(b)
    for s in range(DEC_BATCH):
        for j in range(LBB):
            fblk.append(NCB + s * LBB + j)
            bblk.append(NCB + s * LBB + LBB - 1 - j)
            first.append(1 if j == 0 else 0)
            init.append(1 + s)
            seq.append(BATCH + s)
    return [jnp.asarray(np.array(t, np.int32)) for t in (fblk, bblk, first, init, seq)]


def _gla(a_proj, glog, s0_all, m1):
    tables = _gla_tables()
    nsteps = int(tables[0].shape[0])
    nseq = BATCH + DEC_BATCH

    def fmap(col):
        return lambda i, fb, bb, fi, ini, sq: (fb[i], 0, col)

    def rmap(col):
        return lambda i, fb, bb, fi, ini, sq: (bb[i], 0, col)

    nch = TM // GLA_C
    a3 = a_proj.reshape(M_TOK // GLA_C, GLA_C, a_proj.shape[1])
    g3 = glog.reshape(M_TOK // GLA_C, GLA_C, glog.shape[1])
    gs = pltpu.PrefetchScalarGridSpec(
        num_scalar_prefetch=5,
        grid=(nsteps,),
        in_specs=[
            pl.BlockSpec((nch, GLA_C, A_QK), fmap(0)), pl.BlockSpec((nch, GLA_C, A_QK), fmap(1)),
            pl.BlockSpec((nch, GLA_C, A_V), fmap(1)), pl.BlockSpec((nch, GLA_C, A_QK), fmap(0)),
            pl.BlockSpec((nch, GLA_C, A_QK), rmap(0)), pl.BlockSpec((nch, GLA_C, A_QK), rmap(1)),
            pl.BlockSpec((nch, GLA_C, A_V), rmap(1)), pl.BlockSpec((nch, GLA_C, A_QK), rmap(1)),
            pl.BlockSpec((None, 2, A_QK, DV_A), lambda i, fb, bb, fi, ini, sq: (ini[i], 0, 0, 0)),
            pl.BlockSpec((A_QK, A_V), lambda i, fb, bb, fi, ini, sq: (0, 0)),
        ],
        out_specs=[
            pl.BlockSpec((nch, GLA_C, A_V), fmap(0)), pl.BlockSpec((nch, GLA_C, A_V), rmap(0)),
            pl.BlockSpec((None, 2, A_QK, DV_A), lambda i, fb, bb, fi, ini, sq: (sq[i], 0, 0, 0)),
        ],
        scratch_shapes=[
            pltpu.VMEM((A_QK, DV_A), F32), pltpu.VMEM((A_QK, DV_A), F32),
            pltpu.VMEM((GLA_C * GLA_SB, A_QK), BF16), pltpu.VMEM((GLA_C * GLA_SB, A_QK), BF16),
            pltpu.VMEM((GLA_C, A_QK), F32), pltpu.VMEM((GLA_C, A_QK), F32),
        ],
    )
    o_f, o_b, s_fin = pl.pallas_call(
        _gla_kernel,
        grid_spec=gs,
        out_shape=[
            jax.ShapeDtypeStruct((M_TOK // GLA_C, GLA_C, A_V), F32),
            jax.ShapeDtypeStruct((M_TOK // GLA_C, GLA_C, A_V), F32),
            jax.ShapeDtypeStruct((nseq, 2, A_QK, DV_A), F32),
        ],
        compiler_params=_cparams(("arbitrary",)),
        name="gla",
    )(*tables, a3, a3, a3, g3, a3, a3, a3, g3, s0_all, m1)
    return o_f.reshape(M_TOK, A_V), o_b.reshape(M_TOK, A_V), s_fin


def _softmax_parts(s2):
    e = jnp.exp2(s2 - jnp.max(s2, axis=-1, keepdims=True))
    return e, 1.0 / jnp.sum(e, axis=-1, keepdims=True)


def _diffattn_body(q_ref, k_ref, v_ref, lq1_ref, lk1_ref, lq2_ref, lk2_ref, gsub_ref, o_ref, lam_init):
    lam = (jnp.exp(jnp.sum(lq1_ref[...] * lk1_ref[...], axis=-1, keepdims=True))
           - jnp.exp(jnp.sum(lq2_ref[...] * lk2_ref[...], axis=-1, keepdims=True)) + lam_init)
    lane = lax.broadcasted_iota(jnp.int32, (q_ref.shape[0], LANES), 1)
    for h in range(H_B):
        hs = slice(h * LANES, (h + 1) * LANES)
        q_h = q_ref[:, hs]
        k_h = k_ref[:, hs]
        zero = jnp.zeros_like(q_h)
        e1, r1 = _softmax_parts(_dot_nt(jnp.where(lane < DH_B, q_h, zero), k_h))
        e2, r2 = _softmax_parts(_dot_nt(jnp.where(lane >= DH_B, q_h, zero), k_h))
        w = e1 * r1 - e2 * (lam * r2)
        o = _dot(w.astype(BF16), v_ref[:, hs])
        o_ref[:, hs] = (_rms(o, gsub_ref[...]) * (1.0 - lam_init)).astype(BF16)


def _diffattn_kernel(q_ref, k_ref, v_ref, lq1, lk1, lq2, lk2, gsub, o_ref, *, lam_init):
    _diffattn_body(q_ref, k_ref, v_ref, lq1, lk1, lq2, lk2, gsub, o_ref, lam_init)


def _diffattn(qb, kb, vb, k_lat, v_lat, lam_params, gsub, lam_init):
    small = [pl.BlockSpec((1, DH_B), lambda *_: (0, 0))] * 4 + [pl.BlockSpec((1, DV_B), lambda *_: (0, 0))]
    body = functools.partial(_diffattn_kernel, lam_init=lam_init)
    ctx = pl.pallas_call(
        body,
        grid=(NCB,),
        in_specs=[_row_spec(B_QK), _row_spec(B_QK), _row_spec(B_V)] + small,
        out_specs=_row_spec(B_V),
        out_shape=jax.ShapeDtypeStruct((N_CTX, B_V), BF16),
        compiler_params=_cparams(("arbitrary",)),
        name="diffattn_ctx",
    )(qb, kb, vb, *lam_params, gsub)
    nk = DEC_SEQ + PAST_LEN
    nqb = DEC_SEQ // TQ_LAT
    off = N_CTX // TQ_LAT
    lat = pl.pallas_call(
        body,
        grid=(DEC_BATCH, nqb),
        in_specs=[
            pl.BlockSpec((TQ_LAT, B_QK), lambda b, j: (off + b * nqb + j, 0)),
            pl.BlockSpec((None, nk, B_QK), lambda b, j: (b, 0, 0)),
            pl.BlockSpec((None, nk, B_V), lambda b, j: (b, 0, 0)),
        ] + small,
        out_specs=pl.BlockSpec((TQ_LAT, B_V), lambda b, j: (b * nqb + j, 0)),
        out_shape=jax.ShapeDtypeStruct((N_LAT, B_V), BF16),
        compiler_params=_cparams(("arbitrary", "arbitrary")),
        name="diffattn_lat",
    )(qb, k_lat, v_lat, *lam_params, gsub)
    return ctx, lat


def _postmix_even_kernel(of_ref, ob_ref, ra_ref, actx_ref, alat_ref, xc_ref, xl_ref, g1_ref, ggla_ref, wo_ref,
                         o_ref):
    heads = []
    for h in range(H_A):
        hs = slice(h * DV_A, (h + 1) * DV_A)
        heads.append((_rms(of_ref[:, hs] + ob_ref[:, hs], ggla_ref[...]) * _silu(ra_ref[:, hs])).astype(BF16))
    mix = jnp.concatenate(heads + [_pick_rows(actx_ref, alat_ref)], axis=1)
    o_ref[...] = _pick_rows(xc_ref, xl_ref) + g1_ref[...] * _dot(mix, wo_ref[...])


def _postmix_even(o_f, o_b, a_proj, attn_ctx, attn_lat, x_ctx, x_lat, mod, g_gla, w_out):
    return pl.pallas_call(
        _postmix_even_kernel,
        grid=(NB,),
        in_specs=[_row_spec(A_V), _row_spec(A_V), _row_spec(A_V, 2), _ctx_spec(B_V), _lat_spec(B_V),
                  _ctx_spec(D_MODEL), _lat_spec(D_MODEL), _mod_spec(2), _const_spec((1, DV_A)),
                  _const_spec((A_V + B_V, D_MODEL))],
        out_specs=_row_spec(D_MODEL),
        out_shape=jax.ShapeDtypeStruct((M_TOK, D_MODEL), F32),
        compiler_params=_cparams(("arbitrary",)),
        name="postmix_even",
    )(o_f, o_b, a_proj, attn_ctx, attn_lat, x_ctx, x_lat, mod, g_gla, w_out)


def _swiglu_act(xb, wg_ref, wu_ref, act_ref):
    for lo in range(0, act_ref.shape[1], MXU_N):
        cs = slice(lo, lo + MXU_N)
        act_ref[:, cs] = (_silu(_dot(xb, wg_ref[:, cs])) * _dot(xb, wu_ref[:, cs])).astype(BF16)


def _ffn_even_kernel(x_ref, gam_ref, sh_ref, sc_ref, g2_ref, wg_ref, wu_ref, wd_ref, o_ref, act_ref):
    x = x_ref[...]
    hb = _norm_mod(x, gam_ref[...], sc_ref[...], sh_ref[...]).astype(BF16)
    _swiglu_act(hb, wg_ref, wu_ref, act_ref)
    o_ref[...] = x + g2_ref[...] * _dot(act_ref[...], wd_ref[...])


def _ffn_even(x, gamma, mod, w_gate, w_up, w_down):
    return pl.pallas_call(
        _ffn_even_kernel,
        grid=(NB,),
        in_specs=[_row_spec(D_MODEL), _const_spec((1, D_MODEL)), _mod_spec(3), _mod_spec(4), _mod_spec(5),
                  _const_spec((D_MODEL, D_FF)), _const_spec((D_MODEL, D_FF)), _const_spec((D_FF, D_MODEL))],
        out_specs=_row_spec(D_MODEL),
        out_shape=jax.ShapeDtypeStruct((M_TOK, D_MODEL), F32),
        scratch_shapes=[pltpu.VMEM((TM, D_FF), BF16)],
        compiler_params=_cparams(("arbitrary",)),
        name="ffn_even",
    )(x, gamma, mod, mod, mod, w_gate, w_up, w_down)


def _inproj_odd_kernel(x_ref, gam_ref, sh_ref, sc_ref, w_ref, gq_ref, gk_ref, cos_ref, sin_ref,
                       q_ref, k_ref, v_ref, kctx_ref, vctx_ref, stage_ref):
    is_lat = pl.program_id(0) >= NCB
    hb = _norm_mod(x_ref[...], gam_ref[...], sc_ref[...], sh_ref[...]).astype(BF16)
    z = _dot(hb, w_ref[...])
    cos = jnp.where(is_lat, cos_ref[...], 1.0)
    sin = jnp.where(is_lat, sin_ref[...], 0.0)
    scale = DH_C ** -0.5 * LOG2E
    for h in range(H_C):
        hs = slice(h * DH_C, (h + 1) * DH_C)
        q_ref[:, hs] = (_rope(_rms(z[:, hs], gq_ref[...]), cos, sin, DH_C // 4) * scale).astype(BF16)
    nq = H_C * DH_C
    for h in range(HKV_C):
        hs = slice(h * DH_C, (h + 1) * DH_C)
        k_n = _rms(z[:, nq + h * DH_C:nq + (h + 1) * DH_C], gk_ref[...])
        stage_ref[:, hs] = k_n
        k_ref[:, hs] = _rope(k_n, cos, sin, DH_C // 4).astype(BF16)
    nkv = HKV_C * DH_C
    v = z[:, nq + nkv:]
    stage_ref[:, nkv:] = v
    _copy_ctx(stage_ref, ((0, kctx_ref), (nkv, vctx_ref)))
    v_ref[...] = v.astype(BF16)


def _inproj_odd(x, gamma, mod, w_in, g_q, g_k, cos, sin):
    nq, nkv = H_C * DH_C, HKV_C * DH_C
    return pl.pallas_call(
        _inproj_odd_kernel,
        grid=(NB,),
        in_specs=[
            _row_spec(D_MODEL), _const_spec((1, D_MODEL)), _mod_spec(0), _mod_spec(1),
            _const_spec((D_MODEL, nq + 2 * nkv)), _const_spec((1, DH_C)), _const_spec((1, DH_C)),
            pl.BlockSpec((TM, LANES), lambda i: (_pos_block(i), 0)),
            pl.BlockSpec((TM, LANES), lambda i: (_pos_block(i), 0)),
        ],
        out_specs=[_row_spec(nq), _row_spec(nkv), _row_spec(nkv), _ctx_spec(nkv), _ctx_spec(nkv)],
        out_shape=[
            jax.ShapeDtypeStruct((M_TOK, nq), BF16), jax.ShapeDtypeStruct((M_TOK, nkv), BF16),
            jax.ShapeDtypeStruct((M_TOK, nkv), BF16), jax.ShapeDtypeStruct((N_CTX, nkv), F32),
            jax.ShapeDtypeStruct((N_CTX, nkv), F32),
        ],
        scratch_shapes=[pltpu.VMEM((TM, 2 * nkv), F32)],
        compiler_params=_cparams(("arbitrary",)),
        name="inproj_odd",
    )(x, gamma, mod, mod, w_in, g_q, g_k, cos, sin)


def _gqa_body(q_ref, k_ref, v_ref, o_ref):
    rep = H_C // HKV_C
    tq = q_ref.shape[0]
    for hk in range(HKV_C):
        ks = slice(hk * DH_C, (hk + 1) * DH_C)
        q_g = jnp.concatenate([q_ref[:, (hk * rep + g) * DH_C:(hk * rep + g + 1) * DH_C] for g in range(rep)],
                              axis=0)
        e, r = _softmax_parts(_dot_nt(q_g, k_ref[:, ks]))
        o = _dot((e * r).astype(BF16), v_ref[:, ks])
        for g in range(rep):
            o_ref[:, (hk * rep + g) * DH_C:(hk * rep + g + 1) * DH_C] = o[g * tq:(g + 1) * tq].astype(BF16)


def _gqa(q, k, v, k_lat, v_lat):
    nq, nkv = H_C * DH_C, HKV_C * DH_C
    ctx = pl.pallas_call(
        _gqa_body,
        grid=(NCB,),
        in_specs=[_row_spec(nq), _row_spec(nkv), _row_spec(nkv)],
        out_specs=_row_spec(nq),
        out_shape=jax.ShapeDtypeStruct((N_CTX, nq), BF16),
        compiler_params=_cparams(("arbitrary",)),
        name="gqa_ctx",
    )(q, k, v)
    nk = DEC_SEQ + PAST_LEN
    nqb = DEC_SEQ // TQ_LAT
    off = N_CTX // TQ_LAT
    lat = pl.pallas_call(
        _gqa_body,
        grid=(DEC_BATCH, nqb),
        in_specs=[
            pl.BlockSpec((TQ_LAT, nq), lambda b, j: (off + b * nqb + j, 0)),
            pl.BlockSpec((None, nk, nkv), lambda b, j: (b, 0, 0)),
            pl.BlockSpec((None, nk, nkv), lambda b, j: (b, 0, 0)),
        ],
        out_specs=pl.BlockSpec((TQ_LAT, nq), lambda b, j: (b * nqb + j, 0)),
        out_shape=jax.ShapeDtypeStruct((N_LAT, nq), BF16),
        compiler_params=_cparams(("arbitrary", "arbitrary")),
        name="gqa_lat",
    )(q, k_lat, v_lat)
    return ctx, lat


def _postmix_odd_kernel(actx_ref, alat_ref, x_ref, g1_ref, wo_ref, gam_ref, sh_ref, sc_ref, wr_ref,
                        x1_ref, h_ref, w1_ref, w2_ref, idx_ref, cnt_ref, run_ref):
    @pl.when(pl.program_id(0) == 0)
    def _():
        run_ref[...] = jnp.zeros_like(run_ref)

    x1 = x_ref[...] + g1_ref[...] * _dot(_pick_rows(actx_ref, alat_ref), wo_ref[...])
    x1_ref[...] = x1
    h = _norm_mod(x1, gam_ref[...], sc_ref[...], sh_ref[...])
    h_ref[...] = h
    logits = _dot_hi(h, wr_ref[...])
    lane = lax.broadcasted_iota(jnp.int32, logits.shape, 1)
    lane_f = lane.astype(F32)
    lg = jnp.where(lane < N_EXPERTS, logits, -jnp.inf)
    m1 = jnp.max(lg, axis=-1, keepdims=True)
    i1 = jnp.min(jnp.where(lg == m1, lane_f, float(LANES)), axis=-1, keepdims=True)
    lg2 = jnp.where(lane_f == i1, -jnp.inf, lg)
    m2 = jnp.max(lg2, axis=-1, keepdims=True)
    i2 = jnp.min(jnp.where(lg2 == m2, lane_f, float(LANES)), axis=-1, keepdims=True)
    e = jnp.exp(m2 - m1)
    w1 = 1.0 / (1.0 + e)
    w1_ref[...] = jnp.broadcast_to(w1, logits.shape)
    w2_ref[...] = jnp.broadcast_to(e * w1, logits.shape)
    oh1 = jnp.where(lane_f == i1, 1.0, 0.0)
    oh2 = jnp.where(lane_f == i2, 1.0, 0.0)
    both = oh1 + oh2
    r_i = lax.broadcasted_iota(jnp.int32, (TM, TM), 0)
    c_i = lax.broadcasted_iota(jnp.int32, (TM, TM), 1)
    before = jnp.where(c_i < r_i, 1.0, 0.0).astype(BF16)
    pref = _dot(before, both.astype(BF16)) + run_ref[0:1, :]
    rank1 = jnp.sum(pref * oh1, axis=-1, keepdims=True)
    rank2 = jnp.sum(pref * oh2, axis=-1, keepdims=True)
    run_ref[...] = run_ref[...] + jnp.sum(both, axis=0, keepdims=True)
    cnt_ref[...] = run_ref[...]
    quarter = LANES // 4
    idx_ref[...] = jnp.where(lane < quarter, i1, jnp.where(lane < 2 * quarter, i2, jnp.where(
        lane < 3 * quarter, rank1, rank2))).astype(jnp.int32)


def _postmix_odd(attn_ctx, attn_lat, x, mod, w_out, gamma, w_router):
    return pl.pallas_call(
        _postmix_odd_kernel,
        grid=(NB,),
        in_specs=[_ctx_spec(H_C * DH_C), _lat_spec(H_C * DH_C), _row_spec(D_MODEL), _mod_spec(2),
                  _const_spec((H_C * DH_C, D_MODEL)),
                  _const_spec((1, D_MODEL)), _mod_spec(3), _mod_spec(4), _const_spec((D_MODEL, LANES))],
        out_specs=[_row_spec(D_MODEL), _row_spec(D_MODEL), _row_spec(LANES), _row_spec(LANES), _row_spec(LANES),
                   _const_spec((8, LANES))],
        out_shape=[
            jax.ShapeDtypeStruct((M_TOK, D_MODEL), F32), jax.ShapeDtypeStruct((M_TOK, D_MODEL), F32),
            jax.ShapeDtypeStruct((M_TOK, LANES), F32), jax.ShapeDtypeStruct((M_TOK, LANES), F32),
            jax.ShapeDtypeStruct((M_TOK, LANES), jnp.int32), jax.ShapeDtypeStruct((8, LANES), F32),
        ],
        scratch_shapes=[pltpu.VMEM((8, LANES), F32)],
        compiler_params=_cparams(("arbitrary",)),
        name="postmix_odd",
    )(attn_ctx, attn_lat, x, mod, w_out, gamma, mod, mod, w_router)


def _cast_rows(src_ref, dst_ref, piece):
    def body(r, carry):
        rows = pl.ds(pl.multiple_of(r * piece, piece), piece)
        dst_ref[rows, :] = src_ref[rows, :].astype(BF16)
        return carry

    lax.fori_loop(0, src_ref.shape[0] // piece, body, 0)


def _experts_up_kernel(tile_ref, chunk_ref, exp_ref, first_ref, valid_ref, x_ref, wg_ref, wu_ref, act_ref,
                       wgb_ref, wub_ref):
    s = pl.program_id(0)

    @pl.when(first_ref[s] == 1)
    def _():
        _cast_rows(wg_ref, wgb_ref, 128)
        _cast_rows(wu_ref, wub_ref, 128)

    @pl.when(valid_ref[s] == 1)
    def _():
        _swiglu_act(x_ref[...].astype(BF16), wgb_ref, wub_ref, act_ref)


def _experts_down_kernel(act_tile_ref, exp_ref, first_ref, valid_ref, act_ref, wd_ref, o_ref, wdb_ref):
    i = pl.program_id(0)

    @pl.when(first_ref[i] == 1)
    def _():
        _cast_rows(wd_ref, wdb_ref, 256)

    @pl.when(valid_ref[i] == 1)
    def _():
        o_ref[...] = _dot(act_ref[...], wdb_ref[...])

    @pl.when(valid_ref[i] == 0)
    def _():
        o_ref[...] = jnp.zeros_like(o_ref)


def _experts(up_tables, down_tables, xs, w_gate, w_up, w_down):
    up = pl.pallas_call(
        _experts_up_kernel,
        grid_spec=pltpu.PrefetchScalarGridSpec(
            num_scalar_prefetch=5,
            grid=(NC_E * NT_E,),
            in_specs=[
                pl.BlockSpec((TME, D_MODEL), lambda s, t, c, e, f, v: (t[s], 0)),
                pl.BlockSpec((None, D_MODEL, TF_E), lambda s, t, c, e, f, v: (e[s], 0, c[s])),
                pl.BlockSpec((None, D_MODEL, TF_E), lambda s, t, c, e, f, v: (e[s], 0, c[s])),
            ],
            out_specs=pl.BlockSpec((TME, TF_E), lambda s, t, c, e, f, v: (t[s], c[s])),
            scratch_shapes=[pltpu.VMEM((D_MODEL, TF_E), BF16), pltpu.VMEM((D_MODEL, TF_E), BF16)],
        ),
        out_shape=jax.ShapeDtypeStruct((NT_E * TME, D_FF_E), BF16),
        compiler_params=_cparams(("arbitrary",)),
        name="experts_up",
    )(*up_tables, xs, w_gate, w_up)
    return pl.pallas_call(
        _experts_down_kernel,
        grid_spec=pltpu.PrefetchScalarGridSpec(
            num_scalar_prefetch=4,
            grid=(NT_E,),
            in_specs=[
                pl.BlockSpec((TME, D_FF_E), lambda i, a, e, f, v: (a[i], 0)),
                pl.BlockSpec((None, D_FF_E, D_MODEL), lambda i, a, e, f, v: (e[i], 0, 0)),
            ],
            out_specs=pl.BlockSpec((TME, D_MODEL), lambda i, a, e, f, v: (i, 0)),
            scratch_shapes=[pltpu.VMEM((D_FF_E, D_MODEL), BF16)],
        ),
        out_shape=jax.ShapeDtypeStruct((NT_E * TME, D_MODEL), F32),
        compiler_params=_cparams(("arbitrary",)),
        name="experts_down",
    )(*down_tables, up, w_down)


def _combine_kernel(x_ref, y1_ref, y2_ref, w1_ref, w2_ref, g2_ref, gam_ref, octx_ref, olat_ref):
    rep = D_MODEL // LANES
    w1 = jnp.concatenate([w1_ref[...]] * rep, axis=1)
    w2 = jnp.concatenate([w2_ref[...]] * rep, axis=1)
    x2 = x_ref[...] + g2_ref[...] * (w1 * y1_ref[...] + w2 * y2_ref[...])
    y = _rms(x2, gam_ref[...])
    is_lat = pl.program_id(0) >= NCB

    @pl.when(jnp.logical_not(is_lat))
    def _():
        octx_ref[...] = y

    @pl.when(is_lat)
    def _():
        olat_ref[...] = y


def _combine(x1, y1, y2, w1, w2, mod, gamma):
    return pl.pallas_call(
        _combine_kernel,
        grid=(NB,),
        in_specs=[_row_spec(D_MODEL), _row_spec(D_MODEL), _row_spec(D_MODEL), _row_spec(LANES), _row_spec(LANES),
                  _mod_spec(5), _const_spec((1, D_MODEL))],
        out_specs=[_ctx_spec(D_MODEL), _lat_spec(D_MODEL)],
        out_shape=[jax.ShapeDtypeStruct((N_CTX, D_MODEL), F32), jax.ShapeDtypeStruct((N_LAT, D_MODEL), F32)],
        compiler_params=_cparams(("arbitrary",)),
        name="combine",
    )(x1, y1, y2, w1, w2, mod, gamma)


def _rope_tables(rot_dim):
    n = DEC_SEQ
    rows = jnp.repeat(jnp.arange(n // GRID_W, dtype=F32), GRID_W)
    cols = jnp.tile(jnp.arange(GRID_W, dtype=F32), n // GRID_W)
    half = rot_dim // 2
    freqs = ROPE_THETA ** (-jnp.arange(0, half, 2, dtype=F32) / half)
    ang_r, ang_c = rows[:, None] * freqs, cols[:, None] * freqs
    cos = jnp.concatenate([jnp.cos(ang_r)] * 2 + [jnp.cos(ang_c)] * 2, axis=-1)
    sin = jnp.concatenate([-jnp.sin(ang_r), jnp.sin(ang_r), -jnp.sin(ang_c), jnp.sin(ang_c)], axis=-1)
    rep = LANES // rot_dim
    return jnp.tile(cos, (1, rep)), jnp.tile(sin, (1, rep))


def _rows(a, idx):
    return a.at[idx].get(mode="promise_in_bounds")


def _route(idx, counts_f):
    quarter = LANES // 4
    experts = jnp.arange(N_EXPERTS, dtype=jnp.int32)
    e2 = jnp.stack([idx[:, 0], idx[:, quarter]], axis=1)
    rank2 = jnp.stack([idx[:, 2 * quarter], idx[:, 3 * quarter]], axis=1)
    counts = counts_f[0, :N_EXPERTS].astype(jnp.int32)
    tiles = (counts + TME - 1) // TME
    tile_end = jnp.cumsum(tiles)
    tile_start = tile_end - tiles
    start = jnp.cumsum(counts) - counts
    onehot = e2[:, :, None] == experts[None, None, :]
    pos = jnp.sum(jnp.where(onehot, tile_start[None, None, :] * TME, 0), axis=-1) + rank2
    tile_id = jnp.arange(NT_E, dtype=jnp.int32)
    tile_expert = jnp.minimum(jnp.sum((tile_id[:, None] >= tile_end[None, :]).astype(jnp.int32), axis=1),
                              N_EXPERTS - 1).astype(jnp.int32)
    tile_valid = (tile_id < tile_end[-1]).astype(jnp.int32)
    t_oh = tile_expert[:, None] == experts[None, :]
    t_first = jnp.sum(jnp.where(t_oh, (start - tile_start * TME)[None, :], 0), axis=1) + tile_id * TME
    t_last = jnp.sum(jnp.where(t_oh, (start + counts)[None, :], 0), axis=1)
    order = jnp.argsort(e2.reshape(-1), stable=True).astype(jnp.int32)
    n_assign = order.shape[0]
    g_idx = t_first[:, None] + jnp.arange(TME, dtype=jnp.int32)[None, :]
    live = jnp.logical_and(g_idx < t_last[:, None], tile_valid[:, None] == 1)
    src = _rows(order, jnp.clip(g_idx, 0, n_assign - 1).reshape(-1)) // 2
    src_tok = jnp.where(live.reshape(-1), src, 0).astype(jnp.int32)

    n_tiles = tile_end[-1]
    last_tile = jnp.maximum(n_tiles - 1, 0)
    last_expert = jnp.sum(jnp.where(tile_id == last_tile, tile_expert, 0))
    t_tstart = jnp.sum(jnp.where(t_oh, tile_start[None, :], 0), axis=1)
    down_tables = (
        jnp.where(tile_valid == 1, tile_id, last_tile).astype(jnp.int32),
        tile_expert,
        jnp.logical_and(tile_valid == 1, tile_id == t_tstart).astype(jnp.int32),
        tile_valid,
    )
    s_id = jnp.arange(NC_E * NT_E, dtype=jnp.int32)
    s_exp = jnp.minimum(jnp.sum((s_id[:, None] >= NC_E * tile_end[None, :]).astype(jnp.int32), axis=1),
                        N_EXPERTS - 1)
    s_oh = s_exp[:, None] == experts[None, :]
    s_tstart = jnp.sum(jnp.where(s_oh, tile_start[None, :], 0), axis=1)
    s_ntile = jnp.maximum(jnp.sum(jnp.where(s_oh, tiles[None, :], 0), axis=1), 1)
    rel = s_id - NC_E * s_tstart
    s_valid = s_id < NC_E * n_tiles
    up_tables = (
        jnp.where(s_valid, s_tstart + rel % s_ntile, last_tile).astype(jnp.int32),
        jnp.where(s_valid, rel // s_ntile, NC_E - 1).astype(jnp.int32),
        jnp.where(s_valid, s_exp, last_expert).astype(jnp.int32),
        jnp.logical_and(s_valid, rel % s_ntile == 0).astype(jnp.int32),
        s_valid.astype(jnp.int32),
    )
    return src_tok, pos, up_tables, down_tables


def lambda_init(layer):
    return 0.8 - 0.6 * math.exp(-0.3 * layer)


def kernel(x_prompt, x_sample, state_a, cache_b_k, cache_b_v, cache_c_k, cache_c_v, c, c_ctx, w_mod, b_mod, norm_mix, norm_ffn, w_in_even, w_gate2_a, b_gate_a, g_gla, lam_q1, lam_k1, lam_q2, lam_k2, g_sub_b, w_out_even, w_in_odd, g_q_c, g_k_c, w_out_odd, ffn_gate, ffn_up, ffn_down, w_router, exp_gate, exp_up, exp_down, norm_final):
    x_ctx, x_lat = x_prompt.reshape(N_CTX, D_MODEL), x_sample.reshape(N_LAT, D_MODEL)
    cond =jnp.concatenate([c_ctx[None, :], c, jnp.zeros((N_COND - 1 - DEC_BATCH, D_MODEL), F32)], axis=0)
    mod = _modulation(cond, w_mod, b_mod).reshape(DEPTH, N_COND, 1, 6 * D_MODEL)

    w = w_in_even[0]
    na = 2 * A_QK + 2 * A_V
    gate_lo = na
    gate_hi = na + 2 * GATE_RANK
    col_scale = jnp.concatenate([jnp.full((A_QK,), DK_A ** -0.5, F32), jnp.ones((na - A_QK,), F32),
                                 jnp.full((B_QK,), DH_B ** -0.5, F32), jnp.ones((B_QK + B_V,), F32)])
    w_main = (jnp.concatenate([w[:, :gate_lo], w[:, gate_hi:]], axis=1) * col_scale).astype(BF16)
    w_gates = jnp.pad(w[:, gate_lo:gate_hi], ((0, 0), (0, LANES - 2 * GATE_RANK))).astype(BF16)
    g2 = jnp.zeros((LANES, 2 * A_QK), F32)
    g2 = g2.at[:GATE_RANK, :A_QK].set(w_gate2_a[0, 0]).at[GATE_RANK:2 * GATE_RANK, A_QK:].set(w_gate2_a[0, 1])
    bg = b_gate_a[0].reshape(1, 2 * A_QK)
    cos_b, sin_b = _rope_tables(DH_B)
    a_proj, glog, qb, kb, vb, kb_ctx, vb_ctx = _inproj_even(
        x_ctx, x_lat, norm_mix[0][None, :], mod[0], w_main, w_gates, g2.astype(BF16), bg, cos_b, sin_b)

    s0_all = jnp.concatenate([jnp.zeros((1, 2, A_QK, DV_A), F32),
                              state_a[:, 0].reshape(DEC_BATCH, 2, A_QK, DV_A)], axis=0)
    hd = np.arange(A_QK)[:, None] // DK_A == np.arange(A_V)[None, :] // DV_A
    m1 = jnp.asarray(hd.astype(np.float32)).astype(BF16)
    o_f, o_b, s_fin = _gla(a_proj, glog, s0_all, m1)

    k_lat = jnp.concatenate([kb[N_CTX:].reshape(DEC_BATCH, DEC_SEQ, B_QK),
                             cache_b_k[:, 0].reshape(DEC_BATCH, PAST_LEN, B_QK).astype(BF16)], axis=1)
    v_lat = jnp.concatenate([vb[N_CTX:].reshape(DEC_BATCH, DEC_SEQ, B_V),
                             cache_b_v[:, 0].reshape(DEC_BATCH, PAST_LEN, B_V).astype(BF16)], axis=1)
    lam_params = [p[0][None, :] for p in (lam_q1, lam_k1, lam_q2, lam_k2)]
    attn_b = _diffattn(qb, kb, vb, k_lat, v_lat, lam_params, g_sub_b[0][None, :], lambda_init(0))

    x = _postmix_even(o_f, o_b, a_proj, *attn_b, x_ctx, x_lat, mod[0], g_gla[0][None, :],
                      w_out_even[0].astype(BF16))
    x = _ffn_even(x, norm_ffn[0][None, :], mod[0], ffn_gate[0].astype(BF16), ffn_up[0].astype(BF16),
                  ffn_down[0].astype(BF16))

    cos_c, sin_c = _rope_tables(DH_C)
    q_c, k_c, v_c, kc_ctx, vc_ctx = _inproj_odd(x, norm_mix[1][None, :], mod[1], w_in_odd[0].astype(BF16),
                                     g_q_c[0][None, :], g_k_c[0][None, :], cos_c, sin_c)
    nkv = HKV_C * DH_C
    k_lat = jnp.concatenate([k_c[N_CTX:].reshape(DEC_BATCH, DEC_SEQ, nkv),
                             cache_c_k[:, 0].reshape(DEC_BATCH, PAST_LEN, nkv).astype(BF16)], axis=1)
    v_lat = jnp.concatenate([v_c[N_CTX:].reshape(DEC_BATCH, DEC_SEQ, nkv),
                             cache_c_v[:, 0].reshape(DEC_BATCH, PAST_LEN, nkv).astype(BF16)], axis=1)
    attn_c = _gqa(q_c, k_c, v_c, k_lat, v_lat)
    w_r = jnp.pad(w_router[0], ((0, 0), (0, LANES - N_EXPERTS)))
    x1, h_moe, w1, w2, idx, counts = _postmix_odd(*attn_c, x, mod[1], w_out_odd[0].astype(BF16),
                                                  norm_ffn[1][None, :], w_r)

    src_tok, pos, up_tables, down_tables = _route(idx, counts)
    xs = _rows(h_moe, src_tok)
    ys = _experts(up_tables, down_tables, xs, exp_gate[0], exp_up[0], exp_down[0])
    y1 = _rows(ys, pos[:, 0])
    y2 = _rows(ys, pos[:, 1])
    y_ctx, y_lat = _combine(x1, y1, y2, w1, w2, mod[1], norm_final[None, :])

    y_prompt = y_ctx.reshape(BATCH, SEQ, D_MODEL)
    y_sample = y_lat.reshape(DEC_BATCH, DEC_SEQ, D_MODEL)
    new_state_a = s_fin[:BATCH].reshape(BATCH, 1, 2, H_A, DK_A, DV_A)
    new_b_k = kb_ctx.reshape(BATCH, 1, SEQ, H_B, 2, DH_B)
    new_b_v = vb_ctx.reshape(BATCH, 1, SEQ, H_B, DV_B)
    new_c_k = kc_ctx.reshape(BATCH, 1, SEQ, HKV_C, DH_C)
    new_c_v = vc_ctx.reshape(BATCH, 1, SEQ, HKV_C, DH_C)
    return (y_prompt, y_sample, new_state_a, new_b_k, new_b_v, new_c_k, new_c_v)
```

```python
import functools
import math

import numpy as np
import jax
import jax.numpy as jnp
from jax import lax
from jax.experimental import pallas as pl
from jax.experimental.pallas import tpu as pltpu

D_MODEL = 1024
BATCH = 32
SEQ = 256
DEPTH = 2
DEC_BATCH = 2
DEC_SEQ = 2048
PAST_LEN = 512
GRID_W = 64
H_A, DK_A, DV_A = 4, 64, 128
GATE_RANK = 16
GATE_TAU = 16.0
H_B, DH_B = 4, 64
DV_B = 2 * DH_B
H_C, HKV_C, DH_C = 8, 2, 128
D_FF = 2816
N_EXPERTS = 8
D_FF_E = 3584
ROPE_THETA = 10000.0
EPS = 1e-6
A_QK = H_A * DK_A
A_V = H_A * DV_A
B_QK = H_B * 2 * DH_B
B_V = H_B * DV_B

F32 = jnp.float32
BF16 = jnp.bfloat16
LOG2E = math.log2(math.e)

V7X_VMEM_BYTES = 64 * 1024 * 1024
VMEM_LIMIT = V7X_VMEM_BYTES * 7 // 8
LANES = 128
MXU_N = 256
TM = 256
N_CTX = BATCH * SEQ
N_LAT = DEC_BATCH * DEC_SEQ
M_TOK = N_CTX + N_LAT
NB = M_TOK // TM
NCB = N_CTX // TM
LBB = DEC_SEQ // TM
N_COND = 8
GLA_C = 64
GLA_SB = 8
TME = 512
NT_E = 2 * M_TOK // TME + N_EXPERTS
NC_E = 2
TF_E = D_FF_E // NC_E
TQ_LAT = 256

assert SEQ == TM and DEC_SEQ % TM == 0 and DEPTH == 2


def _cparams(sem):
    return pltpu.CompilerParams(dimension_semantics=sem, vmem_limit_bytes=VMEM_LIMIT)


def _group(i):
    return jnp.where(i < NCB, 0, 1 + (i - NCB) // LBB)


def _pos_block(i):
    return jnp.where(i < NCB, 0, (i - NCB) % LBB)


def _mod_spec(chunk):
    return pl.BlockSpec((None, 1, D_MODEL), lambda i, c=chunk: (_group(i), 0, c))


def _row_spec(width, col=0):
    return pl.BlockSpec((TM, width), lambda i, c=col: (i, c))


def _ctx_spec(width):
    return pl.BlockSpec((TM, width), lambda i: (jnp.minimum(i, NCB - 1), 0))


def _lat_spec(width):
    return pl.BlockSpec((TM, width), lambda i: (jnp.maximum(i - NCB, 0), 0))


def _const_spec(shape):
    nd = len(shape)
    return pl.BlockSpec(shape, lambda i, nd=nd: (0,) * nd)


def _dot(a, b):
    return jnp.dot(a, b, preferred_element_type=F32)


def _dot_nt(a, b):
    return lax.dot_general(a, b, (((1,), (1,)), ((), ())), preferred_element_type=F32)


def _split2(a):
    hi = a.astype(BF16)
    lo = (a - hi.astype(F32)).astype(BF16)
    return hi, lo


def _dot_hi(a, b):
    a_hi, a_lo = _split2(a)
    b_hi, b_lo = _split2(b)
    return _dot(a_hi, b_hi) + (_dot(a_hi, b_lo) + _dot(a_lo, b_hi))


def _silu(x):
    return (0.5 * x) * (1.0 + jnp.tanh(0.5 * x))


def _log_sigmoid(x):
    return jnp.minimum(x, 0.0) - jnp.log(1.0 + jnp.exp(-jnp.abs(x)))


def _rms(x, g):
    return x * lax.rsqrt(jnp.mean(x * x, axis=-1, keepdims=True) + EPS) * g


def _norm_mod(x, gamma, sc, sh):
    return _rms(x, gamma) * (1.0 + sc) + sh


def _pick_rows(ctx_ref, lat_ref):
    return jnp.where(pl.program_id(0) >= NCB, lat_ref[...], ctx_ref[...])


def _copy_ctx(stage_ref, outs):
    piece = 64

    @pl.when(pl.program_id(0) < NCB)
    def _():
        def body(r, carry):
            rows = pl.ds(pl.multiple_of(r * piece, piece), piece)
            for lo, dst_ref in outs:
                dst_ref[rows, :] = stage_ref[rows, lo:lo + dst_ref.shape[1]]
            return carry

        lax.fori_loop(0, stage_ref.shape[0] // piece, body, 0)


def _rope(x, cos, sin, half):
    lane = lax.broadcasted_iota(jnp.int32, x.shape, 1)
    first = (lane % (2 * half)) < half
    swapped = jnp.where(first, pltpu.roll(x, LANES - half, 1), pltpu.roll(x, half, 1))
    return x * cos + swapped * sin


def _mod_kernel(cond_ref, w_ref, b_ref, o_ref):
    o_ref[...] = _dot_hi(_silu(cond_ref[...]), w_ref[...]) + b_ref[...]


def _modulation(cond, w_mod, b_mod):
    tn = 1536
    return pl.pallas_call(
        _mod_kernel,
        grid=(DEPTH, 6 * D_MODEL // tn),
        in_specs=[
            pl.BlockSpec((N_COND, D_MODEL), lambda l, j: (0, 0)),
            pl.BlockSpec((None, D_MODEL, tn), lambda l, j: (l, 0, j)),
            pl.BlockSpec((None, 1, tn), lambda l, j: (l, 0, j)),
        ],
        out_specs=pl.BlockSpec((None, N_COND, tn), lambda l, j: (l, 0, j)),
        out_shape=jax.ShapeDtypeStruct((DEPTH, N_COND, 6 * D_MODEL), F32),
        compiler_params=_cparams(("arbitrary", "arbitrary")),
        name="modulation",
    )(cond, w_mod, b_mod.reshape(DEPTH, 1, 6 * D_MODEL))


def _inproj_even_kernel(xc_ref, xl_ref, gam_ref, sh_ref, sc_ref, w_ref, wg_ref, g2_ref, bg_ref, cos_ref, sin_ref,
                        a_ref, glog_ref, qb_ref, kb_ref, vb_ref, kctx_ref, vctx_ref, stage_ref):
    is_lat = pl.program_id(0) >= NCB
    hb = _norm_mod(_pick_rows(xc_ref, xl_ref), gam_ref[...], sc_ref[...], sh_ref[...]).astype(BF16)
    cos = jnp.where(is_lat, cos_ref[...], 1.0)
    sin = jnp.where(is_lat, sin_ref[...], 0.0)
    na = 2 * A_QK + 2 * A_V
    for lo in range(0, na, MXU_N):
        a_ref[:, lo:lo + MXU_N] = _dot(hb, w_ref[:, lo:lo + MXU_N])
    for lo in range(0, B_QK, MXU_N):
        zq = _dot(hb, w_ref[:, na + lo:na + lo + MXU_N])
        zk = _dot(hb, w_ref[:, na + B_QK + lo:na + B_QK + lo + MXU_N])
        zv = _dot(hb, w_ref[:, na + 2 * B_QK + lo:na + 2 * B_QK + lo + MXU_N])
        stage_ref[:, lo:lo + MXU_N] = zk
        stage_ref[:, B_QK + lo:B_QK + lo + MXU_N] = zv
        vb_ref[:, lo:lo + MXU_N] = zv.astype(BF16)
        for j in range(0, MXU_N, LANES):
            cs = slice(lo + j, lo + j + LANES)
            qb_ref[:, cs] = (_rope(zq[:, j:j + LANES], cos, sin, DH_B // 4) * LOG2E).astype(BF16)
            kb_ref[:, cs] = _rope(zk[:, j:j + LANES], cos, sin, DH_B // 4).astype(BF16)
    gates = _dot(hb, wg_ref[...])
    xg = _dot(gates.astype(BF16), g2_ref[...]) + bg_ref[...]
    glog_ref[...] = _log_sigmoid(xg) * (1.0 / GATE_TAU)
    _copy_ctx(stage_ref, ((0, kctx_ref), (B_QK, vctx_ref)))


def _inproj_even(x_ctx, x_lat, gamma, mod, w_main, w_gates, g2, bg, cos, sin):
    na = 2 * A_QK + 2 * A_V
    nz = na + 2 * B_QK + B_V
    return pl.pallas_call(
        _inproj_even_kernel,
        grid=(NB,),
        in_specs=[
            _ctx_spec(D_MODEL), _lat_spec(D_MODEL), _const_spec((1, D_MODEL)), _mod_spec(0), _mod_spec(1),
            _const_spec((D_MODEL, nz)), _const_spec((D_MODEL, LANES)), _const_spec((LANES, 2 * A_QK)),
            _const_spec((1, 2 * A_QK)),
            pl.BlockSpec((TM, LANES), lambda i: (_pos_block(i), 0)),
            pl.BlockSpec((TM, LANES), lambda i: (_pos_block(i), 0)),
        ],
        out_specs=[_pair_spec(na), _pair_spec(2 * A_QK), _row_spec(B_QK), _row_spec(B_QK), _row_spec(B_V),
                   _ctx_spec(B_QK), _ctx_spec(B_V)],
        out_shape=[
            jax.ShapeDtypeStruct((M_TOK, na), F32), jax.ShapeDtypeStruct((M_TOK, 2 * A_QK), F32),
            jax.ShapeDtypeStruct((M_TOK, B_QK), BF16), jax.ShapeDtypeStruct((M_TOK, B_QK), BF16),
            jax.ShapeDtypeStruct((M_TOK, B_V), BF16), jax.ShapeDtypeStruct((N_CTX, B_QK), F32),
            jax.ShapeDtypeStruct((N_CTX, B_V), F32),
        ],
        scratch_shapes=[pltpu.VMEM((TM, B_QK + B_V), F32)],
        compiler_params=_cparams(("arbitrary",)),
        name="inproj_even",
    )(x_ctx, x_lat, gamma, mod, mod, w_main, w_gates, g2, bg, cos, sin)


def _gla_chunk(q_ref, k_ref, v_ref, g_ref, o_ref, c, s_ref, a_ref, b_ref, m1_ref, head_mask, same_block, rev):
    q, k, v = q_ref[c], k_ref[c], v_ref[c]
    g = g_ref[c] * LOG2E
    v_b = v.astype(BF16)
    r_i = lax.broadcasted_iota(jnp.int32, (GLA_C, GLA_C), 0)
    c_i = lax.broadcasted_iota(jnp.int32, (GLA_C, GLA_C), 1)
    tri = jnp.where((c_i >= r_i) if rev else (c_i <= r_i), 1.0, 0.0).astype(BF16)
    g_hi = g.astype(BF16)
    g_r = g - g_hi.astype(F32)
    g_mid = g_r.astype(BF16)
    g_lo = (g_r - g_mid.astype(F32)).astype(BF16)
    bc = _dot(tri, g_hi) + (_dot(tri, g_mid) + _dot(tri, g_lo))
    b_ref[...] = bc
    yield
    t_i = lax.broadcasted_iota(jnp.int32, (GLA_SB, A_QK), 0)
    group = 2 * GLA_SB
    for i0 in range(0, GLA_C, group):
        blocks = range(i0, i0 + group, GLA_SB)
        q_i = [q_ref[c, i:i + GLA_SB, :] for i in blocks]
        b_i = [b_ref[i:i + GLA_SB, :] for i in blocks]
        for j in range(GLA_SB):
            keep = (t_i <= j) if rev else (t_i >= j)
            pair = []
            for n, i in enumerate(blocks):
                k_s = k_ref[c, i + j:i + j + 1, :]
                b_s = b_ref[i + j:i + j + 1, :]
                pair.append(jnp.where(keep, q_i[n] * k_s * jnp.exp2(b_i[n] - b_s), 0.0))
            a_ref[j * GLA_C + i0:j * GLA_C + i0 + group, :] = jnp.concatenate(pair, axis=0).astype(BF16)
    yield
    last = bc[0:1] if rev else bc[GLA_C - 1:GLA_C]
    qc = q * jnp.exp2(bc)
    kl = k * jnp.exp2(last - bc)
    row = lax.broadcasted_iota(jnp.int32, (GLA_C, A_QK), 0)

    def expand(x):
        return jnp.where(head_mask, jnp.concatenate([x] * H_A, axis=0), 0.0).astype(BF16)

    scores = None
    half = GLA_C // 2
    while half >= GLA_SB:
        mids = [bc[b + half:b + half + 1] if rev else bc[b + half - 1:b + half] for b in range(0, GLA_C, 2 * half)]
        ref = mids[0] if len(mids) == 1 else jnp.concatenate(
            [jnp.broadcast_to(m, (2 * half, A_QK)) for m in mids], axis=0)
        upper = (row // half) % 2 == 1
        later, earlier = (jnp.logical_not(upper), upper) if rev else (upper, jnp.logical_not(upper))
        q_l = jnp.where(later, q * jnp.exp2(bc - ref), 0.0)
        k_e = jnp.where(earlier, k * jnp.exp2(ref - bc), 0.0).astype(BF16)
        term = _dot_nt(expand(q_l), k_e)
        if 2 * half < GLA_C:
            term = jnp.where(same_block[2 * half], term, 0.0)
        scores = term if scores is None else scores + term
        half //= 2
    s_old = s_ref[...]
    inter = _dot(expand(qc), s_old.astype(BF16))
    sums = _dot(a_ref[...], m1_ref[...])
    yield
    lane_c = lax.broadcasted_iota(jnp.int32, (GLA_C, LANES), 1)
    within = None
    for j in range(GLA_SB):
        part = jnp.where(lane_c % GLA_SB == j, sums[j * GLA_C:(j + 1) * GLA_C], 0.0)
        within = part if within is None else within + part
    lane = lax.broadcasted_iota(jnp.int32, (GLA_SB, LANES), 1)
    per_head = []
    for h in range(H_A):
        rows_h = []
        for i in range(0, GLA_C, GLA_SB):
            moved = pltpu.roll(within[i:i + GLA_SB], (i - h * GLA_SB) % LANES, 1)
            rows_h.append(jnp.where(lane // GLA_SB == i // GLA_SB, moved, 0.0))
        per_head.append(jnp.concatenate(rows_h, axis=0)[:, :GLA_C])
    scores = scores + jnp.concatenate(per_head, axis=0)
    o_heads = []
    for h in range(H_A):
        hs = slice(h * GLA_C, (h + 1) * GLA_C)
        o_heads.append(inter[hs] + _dot(scores[hs].astype(BF16), v_b[:, h * DV_A:(h + 1) * DV_A]))
    o_ref[c] = jnp.concatenate(o_heads, axis=1)
    yield
    t = jnp.transpose(jnp.concatenate([kl, jnp.broadcast_to(last, (GLA_C, A_QK))], axis=0))
    kv = _dot(t[:, :GLA_C].astype(BF16), v_b)
    a_col = jnp.exp2(t[:, GLA_C:GLA_C + 1])
    for h in range(H_A):
        hs = slice(h * DK_A, (h + 1) * DK_A)
        s_ref[hs, :] = a_col[hs] * s_old[hs] + kv[hs, h * DV_A:(h + 1) * DV_A]


def _gla_kernel(fblk_ref, bblk_ref, first_ref, inita_ref, initb_ref,
                qf_ref, kf_ref, vf_ref, gf_ref, qr_ref, kr_ref, vr_ref, gr_ref, s0a_ref, s0b_ref, m1_ref,
                of_ref, ob_ref, sfin_ref, s_refs, a_refs, b_refs):
    step = pl.program_id(0)

    @pl.when(first_ref[step] == 1)
    def _():
        for seq, s0_ref in enumerate((s0a_ref, s0b_ref)):
            for d in range(2):
                s_refs[2 * seq + d] = s0_ref[d]

    r_h = lax.broadcasted_iota(jnp.int32, (H_A * GLA_C, A_QK), 0) // GLA_C
    c_h = lax.broadcasted_iota(jnp.int32, (H_A * GLA_C, A_QK), 1) // DK_A
    head_mask = r_h == c_h
    t_s = lax.broadcasted_iota(jnp.int32, (H_A * GLA_C, GLA_C), 0) % GLA_C
    s_s = lax.broadcasted_iota(jnp.int32, (H_A * GLA_C, GLA_C), 1)
    same_block = {}
    size = 2 * GLA_SB
    while size < GLA_C:
        same_block[size] = (t_s // size) == (s_s // size)
        size *= 2
    nch = TM // GLA_C

    def body(c, carry):
        chains = []
        for seq in range(2):
            chains.append(_gla_chunk(qf_ref, kf_ref, vf_ref, gf_ref, of_ref, seq * nch + c, s_refs.at[2 * seq],
                                     a_refs.at[2 * seq], b_refs.at[2 * seq], m1_ref, head_mask, same_block, False))
            chains.append(_gla_chunk(qr_ref, kr_ref, vr_ref, gr_ref, ob_ref, seq * nch + nch - 1 - c,
                                     s_refs.at[2 * seq + 1], a_refs.at[2 * seq + 1], b_refs.at[2 * seq + 1], m1_ref,
                                     head_mask, same_block, True))
        live = chains
        while live:
            live = [ch for ch in live if next(ch, True) is None]
        return carry

    lax.fori_loop(0, nch, body, 0)
    for seq in range(2):
        for d in range(2):
            sfin_ref[seq, d] = s_refs[2 * seq + d]


def _pair_block(i):
    j = i - NCB
    return jnp.where(i < NCB, i, NCB + 2 * (j % LBB) + j // LBB)


def _pair_spec(width, col=0):
    return pl.BlockSpec((TM, width), lambda i, c=col: (_pair_block(i), c))


def _gla_tables():
    fblk, bblk, first, init_a, init_b = [], [], [], [], []
    for p in range(BATCH // 2):
        fblk.append(p), bblk.append(p), first.append(1), init_a.append(0), init_b.append(0)
    for j in range(LBB):
        fblk.append(NCB // 2 + j)
        bblk.append(NCB // 2 + LBB - 1 - j)
        first.append(1 if j == 0 else 0)
        init_a.append(1), init_b.append(2)
    return [jnp.asarray(np.array(t, np.int32)) for t in (fblk, bblk, first, init_a, init_b)]


def _gla(a_proj, glog, s0_all, m1):
    tables = _gla_tables()
    nsteps = int(tables[0].shape[0])
    nseq = BATCH + DEC_BATCH

    def fmap(col):
        return lambda i, fb, bb, fi, ia, ib: (fb[i], 0, col)

    def rmap(col):
        return lambda i, fb, bb, fi, ia, ib: (bb[i], 0, col)

    nch = 2 * TM // GLA_C
    a3 = a_proj.reshape(M_TOK // GLA_C, GLA_C, a_proj.shape[1])
    g3 = glog.reshape(M_TOK // GLA_C, GLA_C, glog.shape[1])
    gs = pltpu.PrefetchScalarGridSpec(
        num_scalar_prefetch=5,
        grid=(nsteps,),
        in_specs=[
            pl.BlockSpec((nch, GLA_C, A_QK), fmap(0)), pl.BlockSpec((nch, GLA_C, A_QK), fmap(1)),
            pl.BlockSpec((nch, GLA_C, A_V), fmap(1)), pl.BlockSpec((nch, GLA_C, A_QK), fmap(0)),
            pl.BlockSpec((nch, GLA_C, A_QK), rmap(0)), pl.BlockSpec((nch, GLA_C, A_QK), rmap(1)),
            pl.BlockSpec((nch, GLA_C, A_V), rmap(1)), pl.BlockSpec((nch, GLA_C, A_QK), rmap(1)),
            pl.BlockSpec((None, 2, A_QK, DV_A), lambda i, fb, bb, fi, ia, ib: (ia[i], 0, 0, 0)),
            pl.BlockSpec((None, 2, A_QK, DV_A), lambda i, fb, bb, fi, ia, ib: (ib[i], 0, 0, 0)),
            pl.BlockSpec((A_QK, LANES), lambda i, fb, bb, fi, ia, ib: (0, 0)),
        ],
        out_specs=[
            pl.BlockSpec((nch, GLA_C, A_V), fmap(0)), pl.BlockSpec((nch, GLA_C, A_V), rmap(0)),
            pl.BlockSpec((2, 2, A_QK, DV_A), lambda i, fb, bb, fi, ia, ib: (jnp.minimum(fb[i], NCB // 2), 0, 0, 0)),
        ],
        scratch_shapes=[
            pltpu.VMEM((4, A_QK, DV_A), F32),
            pltpu.VMEM((4, GLA_SB * GLA_C, A_QK), BF16),
            pltpu.VMEM((4, GLA_C, A_QK), F32),
        ],
    )
    o_f, o_b, s_fin = pl.pallas_call(
        _gla_kernel,
        grid_spec=gs,
        out_shape=[
            jax.ShapeDtypeStruct((M_TOK // GLA_C, GLA_C, A_V), F32),
            jax.ShapeDtypeStruct((M_TOK // GLA_C, GLA_C, A_V), F32),
            jax.ShapeDtypeStruct((nseq, 2, A_QK, DV_A), F32),
        ],
        compiler_params=_cparams(("arbitrary",)),
        name="gla",
    )(*tables, a3, a3, a3, g3, a3, a3, a3, g3, s0_all, s0_all, m1)
    return o_f.reshape(M_TOK, A_V), o_b.reshape(M_TOK, A_V), s_fin


def _softmax_parts(s2):
    e = jnp.exp2(s2 - jnp.max(s2, axis=-1, keepdims=True))
    return e, 1.0 / jnp.sum(e, axis=-1, keepdims=True)


def _diffattn_body(q_ref, k_ref, v_ref, lq1_ref, lk1_ref, lq2_ref, lk2_ref, gsub_ref, o_ref, lam_init):
    lam = (jnp.exp(jnp.sum(lq1_ref[...] * lk1_ref[...], axis=-1, keepdims=True))
           - jnp.exp(jnp.sum(lq2_ref[...] * lk2_ref[...], axis=-1, keepdims=True)) + lam_init)
    lane = lax.broadcasted_iota(jnp.int32, (q_ref.shape[0], LANES), 1)
    for h in range(H_B):
        hs = slice(h * LANES, (h + 1) * LANES)
        q_h = q_ref[:, hs]
        k_h = k_ref[:, hs]
        zero = jnp.zeros_like(q_h)
        e1, r1 = _softmax_parts(_dot_nt(jnp.where(lane < DH_B, q_h, zero), k_h))
        e2, r2 = _softmax_parts(_dot_nt(jnp.where(lane >= DH_B, q_h, zero), k_h))
        w = e1 * r1 - e2 * (lam * r2)
        o = _dot(w.astype(BF16), v_ref[:, hs])
        o_ref[:, hs] = (_rms(o, gsub_ref[...]) * (1.0 - lam_init)).astype(BF16)


def _diffattn_kernel(q_ref, k_ref, v_ref, lq1, lk1, lq2, lk2, gsub, o_ref, *, lam_init):
    _diffattn_body(q_ref, k_ref, v_ref, lq1, lk1, lq2, lk2, gsub, o_ref, lam_init)


def _diffattn(qb, kb, vb, k_lat, v_lat, lam_params, gsub, lam_init):
    small = [pl.BlockSpec((1, DH_B), lambda *_: (0, 0))] * 4 + [pl.BlockSpec((1, DV_B), lambda *_: (0, 0))]
    body = functools.partial(_diffattn_kernel, lam_init=lam_init)
    ctx = pl.pallas_call(
        body,
        grid=(NCB,),
        in_specs=[_row_spec(B_QK), _row_spec(B_QK), _row_spec(B_V)] + small,
        out_specs=_row_spec(B_V),
        out_shape=jax.ShapeDtypeStruct((N_CTX, B_V), BF16),
        compiler_params=_cparams(("arbitrary",)),
        name="diffattn_ctx",
    )(qb, kb, vb, *lam_params, gsub)
    nk = DEC_SEQ + PAST_LEN
    nqb = DEC_SEQ // TQ_LAT
    off = N_CTX // TQ_LAT
    lat = pl.pallas_call(
        body,
        grid=(DEC_BATCH, nqb),
        in_specs=[
            pl.BlockSpec((TQ_LAT, B_QK), lambda b, j: (off + b * nqb + j, 0)),
            pl.BlockSpec((None, nk, B_QK), lambda b, j: (b, 0, 0)),
            pl.BlockSpec((None, nk, B_V), lambda b, j: (b, 0, 0)),
        ] + small,
        out_specs=pl.BlockSpec((TQ_LAT, B_V), lambda b, j: (b * nqb + j, 0)),
        out_shape=jax.ShapeDtypeStruct((N_LAT, B_V), BF16),
        compiler_params=_cparams(("arbitrary", "arbitrary")),
        name="diffattn_lat",
    )(qb, k_lat, v_lat, *lam_params, gsub)
    return ctx, lat


def _postmix_even_kernel(of_ref, ob_ref, ra_ref, actx_ref, alat_ref, xc_ref, xl_ref, g1_ref, ggla_ref, wo_ref,
                         o_ref):
    heads = []
    for h in range(H_A):
        hs = slice(h * DV_A, (h + 1) * DV_A)
        heads.append((_rms(of_ref[:, hs] + ob_ref[:, hs], ggla_ref[...]) * _silu(ra_ref[:, hs])).astype(BF16))
    mix = jnp.concatenate(heads + [_pick_rows(actx_ref, alat_ref)], axis=1)
    o_ref[...] = _pick_rows(xc_ref, xl_ref) + g1_ref[...] * _dot(mix, wo_ref[...])


def _postmix_even(o_f, o_b, a_proj, attn_ctx, attn_lat, x_ctx, x_lat, mod, g_gla, w_out):
    return pl.pallas_call(
        _postmix_even_kernel,
        grid=(NB,),
        in_specs=[_pair_spec(A_V), _pair_spec(A_V), _pair_spec(A_V, 2), _ctx_spec(B_V), _lat_spec(B_V),
                  _ctx_spec(D_MODEL), _lat_spec(D_MODEL), _mod_spec(2), _const_spec((1, DV_A)),
                  _const_spec((A_V + B_V, D_MODEL))],
        out_specs=_row_spec(D_MODEL),
        out_shape=jax.ShapeDtypeStruct((M_TOK, D_MODEL), F32),
        compiler_params=_cparams(("arbitrary",)),
        name="postmix_even",
    )(o_f, o_b, a_proj, attn_ctx, attn_lat, x_ctx, x_lat, mod, g_gla, w_out)


def _swiglu_act(xb, wg_ref, wu_ref, act_ref):
    for lo in range(0, act_ref.shape[1], MXU_N):
        cs = slice(lo, lo + MXU_N)
        act_ref[:, cs] = (_silu(_dot(xb, wg_ref[:, cs])) * _dot(xb, wu_ref[:, cs])).astype(BF16)


def _ffn_even_kernel(x_ref, gam_ref, sh_ref, sc_ref, g2_ref, wg_ref, wu_ref, wd_ref, o_ref, act_ref):
    x = x_ref[...]
    hb = _norm_mod(x, gam_ref[...], sc_ref[...], sh_ref[...]).astype(BF16)
    _swiglu_act(hb, wg_ref, wu_ref, act_ref)
    o_ref[...] = x + g2_ref[...] * _dot(act_ref[...], wd_ref[...])


def _ffn_even(x, gamma, mod, w_gate, w_up, w_down):
    return pl.pallas_call(
        _ffn_even_kernel,
        grid=(NB,),
        in_specs=[_row_spec(D_MODEL), _const_spec((1, D_MODEL)), _mod_spec(3), _mod_spec(4), _mod_spec(5),
                  _const_spec((D_MODEL, D_FF)), _const_spec((D_MODEL, D_FF)), _const_spec((D_FF, D_MODEL))],
        out_specs=_row_spec(D_MODEL),
        out_shape=jax.ShapeDtypeStruct((M_TOK, D_MODEL), F32),
        scratch_shapes=[pltpu.VMEM((TM, D_FF), BF16)],
        compiler_params=_cparams(("arbitrary",)),
        name="ffn_even",
    )(x, gamma, mod, mod, mod, w_gate, w_up, w_down)


def _inproj_odd_kernel(x_ref, gam_ref, sh_ref, sc_ref, w_ref, gq_ref, gk_ref, cos_ref, sin_ref,
                       q_ref, k_ref, v_ref, kctx_ref, vctx_ref, stage_ref):
    is_lat = pl.program_id(0) >= NCB
    hb = _norm_mod(x_ref[...], gam_ref[...], sc_ref[...], sh_ref[...]).astype(BF16)
    cos = jnp.where(is_lat, cos_ref[...], 1.0)
    sin = jnp.where(is_lat, sin_ref[...], 0.0)
    scale = DH_C ** -0.5 * LOG2E
    nq, nkv = H_C * DH_C, HKV_C * DH_C
    for lo in range(0, nq, MXU_N):
        z = _dot(hb, w_ref[:, lo:lo + MXU_N])
        for j in range(0, MXU_N, DH_C):
            q_ref[:, lo + j:lo + j + DH_C] = (
                _rope(_rms(z[:, j:j + DH_C], gq_ref[...]), cos, sin, DH_C // 4) * scale).astype(BF16)
    z = _dot(hb, w_ref[:, nq:nq + nkv])
    for h in range(HKV_C):
        hs = slice(h * DH_C, (h + 1) * DH_C)
        k_n = _rms(z[:, hs], gk_ref[...])
        stage_ref[:, hs] = k_n
        k_ref[:, hs] = _rope(k_n, cos, sin, DH_C // 4).astype(BF16)
    v = _dot(hb, w_ref[:, nq + nkv:])
    stage_ref[:, nkv:] = v
    v_ref[...] = v.astype(BF16)
    _copy_ctx(stage_ref, ((0, kctx_ref), (nkv, vctx_ref)))


def _inproj_odd(x, gamma, mod, w_in, g_q, g_k, cos, sin):
    nq, nkv = H_C * DH_C, HKV_C * DH_C
    return pl.pallas_call(
        _inproj_odd_kernel,
        grid=(NB,),
        in_specs=[
            _row_spec(D_MODEL), _const_spec((1, D_MODEL)), _mod_spec(0), _mod_spec(1),
            _const_spec((D_MODEL, nq + 2 * nkv)), _const_spec((1, DH_C)), _const_spec((1, DH_C)),
            pl.BlockSpec((TM, LANES), lambda i: (_pos_block(i), 0)),
            pl.BlockSpec((TM, LANES), lambda i: (_pos_block(i), 0)),
        ],
        out_specs=[_row_spec(nq), _row_spec(nkv), _row_spec(nkv), _ctx_spec(nkv), _ctx_spec(nkv)],
        out_shape=[
            jax.ShapeDtypeStruct((M_TOK, nq), BF16), jax.ShapeDtypeStruct((M_TOK, nkv), BF16),
            jax.ShapeDtypeStruct((M_TOK, nkv), BF16), jax.ShapeDtypeStruct((N_CTX, nkv), F32),
            jax.ShapeDtypeStruct((N_CTX, nkv), F32),
        ],
        scratch_shapes=[pltpu.VMEM((TM, 2 * nkv), F32)],
        compiler_params=_cparams(("arbitrary",)),
        name="inproj_odd",
    )(x, gamma, mod, mod, w_in, g_q, g_k, cos, sin)


def _gqa_body(q_ref, k_ref, v_ref, o_ref):
    rep = H_C // HKV_C
    tq = q_ref.shape[0]
    for hk in range(HKV_C):
        ks = slice(hk * DH_C, (hk + 1) * DH_C)
        q_g = jnp.concatenate([q_ref[:, (hk * rep + g) * DH_C:(hk * rep + g + 1) * DH_C] for g in range(rep)],
                              axis=0)
        e, r = _softmax_parts(_dot_nt(q_g, k_ref[:, ks]))
        o = _dot((e * r).astype(BF16), v_ref[:, ks])
        for g in range(rep):
            o_ref[:, (hk * rep + g) * DH_C:(hk * rep + g + 1) * DH_C] = o[g * tq:(g + 1) * tq].astype(BF16)


def _gqa(q, k, v, k_lat, v_lat):
    nq, nkv = H_C * DH_C, HKV_C * DH_C
    ctx = pl.pallas_call(
        _gqa_body,
        grid=(NCB,),
        in_specs=[_row_spec(nq), _row_spec(nkv), _row_spec(nkv)],
        out_specs=_row_spec(nq),
        out_shape=jax.ShapeDtypeStruct((N_CTX, nq), BF16),
        compiler_params=_cparams(("arbitrary",)),
        name="gqa_ctx",
    )(q, k, v)
    nk = DEC_SEQ + PAST_LEN
    nqb = DEC_SEQ // TQ_LAT
    off = N_CTX // TQ_LAT
    lat = pl.pallas_call(
        _gqa_body,
        grid=(DEC_BATCH, nqb),
        in_specs=[
            pl.BlockSpec((TQ_LAT, nq), lambda b, j: (off + b * nqb + j, 0)),
            pl.BlockSpec((None, nk, nkv), lambda b, j: (b, 0, 0)),
            pl.BlockSpec((None, nk, nkv), lambda b, j: (b, 0, 0)),
        ],
        out_specs=pl.BlockSpec((TQ_LAT, nq), lambda b, j: (b * nqb + j, 0)),
        out_shape=jax.ShapeDtypeStruct((N_LAT, nq), BF16),
        compiler_params=_cparams(("arbitrary", "arbitrary")),
        name="gqa_lat",
    )(q, k_lat, v_lat)
    return ctx, lat


def _postmix_odd_kernel(actx_ref, alat_ref, x_ref, g1_ref, wo_ref, gam_ref, sh_ref, sc_ref, wr_ref,
                        x1_ref, h_ref, w1_ref, w2_ref, idx_ref, cnt_ref, run_ref):
    @pl.when(pl.program_id(0) == 0)
    def _():
        run_ref[...] = jnp.zeros_like(run_ref)

    x1 = x_ref[...] + g1_ref[...] * _dot(_pick_rows(actx_ref, alat_ref), wo_ref[...])
    x1_ref[...] = x1
    h = _norm_mod(x1, gam_ref[...], sc_ref[...], sh_ref[...])
    h_ref[...] = h
    logits = _dot_hi(h, wr_ref[...])
    lane = lax.broadcasted_iota(jnp.int32, logits.shape, 1)
    lane_f = lane.astype(F32)
    lg = jnp.where(lane < N_EXPERTS, logits, -jnp.inf)
    m1 = jnp.max(lg, axis=-1, keepdims=True)
    i1 = jnp.min(jnp.where(lg == m1, lane_f, float(LANES)), axis=-1, keepdims=True)
    lg2 = jnp.where(lane_f == i1, -jnp.inf, lg)
    m2 = jnp.max(lg2, axis=-1, keepdims=True)
    i2 = jnp.min(jnp.where(lg2 == m2, lane_f, float(LANES)), axis=-1, keepdims=True)
    e = jnp.exp(m2 - m1)
    w1 = 1.0 / (1.0 + e)
    w1_ref[...] = jnp.broadcast_to(w1, logits.shape)
    w2_ref[...] = jnp.broadcast_to(e * w1, logits.shape)
    oh1 = jnp.where(lane_f == i1, 1.0, 0.0)
    oh2 = jnp.where(lane_f == i2, 1.0, 0.0)
    both = oh1 + oh2
    r_i = lax.broadcasted_iota(jnp.int32, (TM, TM), 0)
    c_i = lax.broadcasted_iota(jnp.int32, (TM, TM), 1)
    before = jnp.where(c_i < r_i, 1.0, 0.0).astype(BF16)
    pref = _dot(before, both.astype(BF16)) + run_ref[0:1, :]
    rank1 = jnp.sum(pref * oh1, axis=-1, keepdims=True)
    rank2 = jnp.sum(pref * oh2, axis=-1, keepdims=True)
    run_ref[...] = run_ref[...] + jnp.sum(both, axis=0, keepdims=True)
    cnt_ref[...] = run_ref[...]
    quarter = LANES // 4
    idx_ref[...] = jnp.where(lane < quarter, i1, jnp.where(lane < 2 * quarter, i2, jnp.where(
        lane < 3 * quarter, rank1, rank2))).astype(jnp.int32)


def _postmix_odd(attn_ctx, attn_lat, x, mod, w_out, gamma, w_router):
    return pl.pallas_call(
        _postmix_odd_kernel,
        grid=(NB,),
        in_specs=[_ctx_spec(H_C * DH_C), _lat_spec(H_C * DH_C), _row_spec(D_MODEL), _mod_spec(2),
                  _const_spec((H_C * DH_C, D_MODEL)),
                  _const_spec((1, D_MODEL)), _mod_spec(3), _mod_spec(4), _const_spec((D_MODEL, LANES))],
        out_specs=[_row_spec(D_MODEL), _row_spec(D_MODEL), _row_spec(LANES), _row_spec(LANES), _row_spec(LANES),
                   _const_spec((8, LANES))],
        out_shape=[
            jax.ShapeDtypeStruct((M_TOK, D_MODEL), F32), jax.ShapeDtypeStruct((M_TOK, D_MODEL), F32),
            jax.ShapeDtypeStruct((M_TOK, LANES), F32), jax.ShapeDtypeStruct((M_TOK, LANES), F32),
            jax.ShapeDtypeStruct((M_TOK, LANES), jnp.int32), jax.ShapeDtypeStruct((8, LANES), F32),
        ],
        scratch_shapes=[pltpu.VMEM((8, LANES), F32)],
        compiler_params=_cparams(("arbitrary",)),
        name="postmix_odd",
    )(attn_ctx, attn_lat, x, mod, w_out, gamma, mod, mod, w_router)


def _cast_rows(src_ref, dst_ref, piece):
    def body(r, carry):
        rows = pl.ds(pl.multiple_of(r * piece, piece), piece)
        dst_ref[rows, :] = src_ref[rows, :].astype(BF16)
        return carry

    lax.fori_loop(0, src_ref.shape[0] // piece, body, 0)


def _experts_up_kernel(tile_ref, chunk_ref, exp_ref, first_ref, valid_ref, x_ref, wg_ref, wu_ref, act_ref,
                       wgb_ref, wub_ref):
    s = pl.program_id(0)

    @pl.when(first_ref[s] == 1)
    def _():
        _cast_rows(wg_ref, wgb_ref, 128)
        _cast_rows(wu_ref, wub_ref, 128)

    @pl.when(valid_ref[s] == 1)
    def _():
        _swiglu_act(x_ref[...].astype(BF16), wgb_ref, wub_ref, act_ref)


def _experts_down_kernel(act_tile_ref, exp_ref, first_ref, valid_ref, act_ref, wd_ref, o_ref, wdb_ref):
    i = pl.program_id(0)

    @pl.when(first_ref[i] == 1)
    def _():
        _cast_rows(wd_ref, wdb_ref, 256)

    @pl.when(valid_ref[i] == 1)
    def _():
        o_ref[...] = _dot(act_ref[...], wdb_ref[...])

    @pl.when(valid_ref[i] == 0)
    def _():
        o_ref[...] = jnp.zeros_like(o_ref)


def _experts(up_tables, down_tables, xs, w_gate, w_up, w_down):
    up = pl.pallas_call(
        _experts_up_kernel,
        grid_spec=pltpu.PrefetchScalarGridSpec(
            num_scalar_prefetch=5,
            grid=(NC_E * NT_E,),
            in_specs=[
                pl.BlockSpec((TME, D_MODEL), lambda s, t, c, e, f, v: (t[s], 0)),
                pl.BlockSpec((None, D_MODEL, TF_E), lambda s, t, c, e, f, v: (e[s], 0, c[s])),
                pl.BlockSpec((None, D_MODEL, TF_E), lambda s, t, c, e, f, v: (e[s], 0, c[s])),
            ],
            out_specs=pl.BlockSpec((TME, TF_E), lambda s, t, c, e, f, v: (t[s], c[s])),
            scratch_shapes=[pltpu.VMEM((D_MODEL, TF_E), BF16), pltpu.VMEM((D_MODEL, TF_E), BF16)],
        ),
        out_shape=jax.ShapeDtypeStruct((NT_E * TME, D_FF_E), BF16),
        compiler_params=_cparams(("arbitrary",)),
        name="experts_up",
    )(*up_tables, xs, w_gate, w_up)
    return pl.pallas_call(
        _experts_down_kernel,
        grid_spec=pltpu.PrefetchScalarGridSpec(
            num_scalar_prefetch=4,
            grid=(NT_E,),
            in_specs=[
                pl.BlockSpec((TME, D_FF_E), lambda i, a, e, f, v: (a[i], 0)),
                pl.BlockSpec((None, D_FF_E, D_MODEL), lambda i, a, e, f, v: (e[i], 0, 0)),
            ],
            out_specs=pl.BlockSpec((TME, D_MODEL), lambda i, a, e, f, v: (i, 0)),
            scratch_shapes=[pltpu.VMEM((D_FF_E, D_MODEL), BF16)],
        ),
        out_shape=jax.ShapeDtypeStruct((NT_E * TME, D_MODEL), F32),
        compiler_params=_cparams(("arbitrary",)),
        name="experts_down",
    )(*down_tables, up, w_down)


def _combine_kernel(x_ref, y1_ref, y2_ref, w1_ref, w2_ref, g2_ref, gam_ref, octx_ref, olat_ref):
    rep = D_MODEL // LANES
    w1 = jnp.concatenate([w1_ref[...]] * rep, axis=1)
    w2 = jnp.concatenate([w2_ref[...]] * rep, axis=1)
    x2 = x_ref[...] + g2_ref[...] * (w1 * y1_ref[...] + w2 * y2_ref[...])
    y = _rms(x2, gam_ref[...])
    is_lat = pl.program_id(0) >= NCB

    @pl.when(jnp.logical_not(is_lat))
    def _():
        octx_ref[...] = y

    @pl.when(is_lat)
    def _():
        olat_ref[...] = y


def _combine(x1, y1, y2, w1, w2, mod, gamma):
    return pl.pallas_call(
        _combine_kernel,
        grid=(NB,),
        in_specs=[_row_spec(D_MODEL), _row_spec(D_MODEL), _row_spec(D_MODEL), _row_spec(LANES), _row_spec(LANES),
                  _mod_spec(5), _const_spec((1, D_MODEL))],
        out_specs=[_ctx_spec(D_MODEL), _lat_spec(D_MODEL)],
        out_shape=[jax.ShapeDtypeStruct((N_CTX, D_MODEL), F32), jax.ShapeDtypeStruct((N_LAT, D_MODEL), F32)],
        compiler_params=_cparams(("arbitrary",)),
        name="combine",
    )(x1, y1, y2, w1, w2, mod, gamma)


def _rope_tables(rot_dim):
    n = DEC_SEQ
    rows = jnp.repeat(jnp.arange(n // GRID_W, dtype=F32), GRID_W)
    cols = jnp.tile(jnp.arange(GRID_W, dtype=F32), n // GRID_W)
    half = rot_dim // 2
    freqs = ROPE_THETA ** (-jnp.arange(0, half, 2, dtype=F32) / half)
    ang_r, ang_c = rows[:, None] * freqs, cols[:, None] * freqs
    cos = jnp.concatenate([jnp.cos(ang_r)] * 2 + [jnp.cos(ang_c)] * 2, axis=-1)
    sin = jnp.concatenate([-jnp.sin(ang_r), jnp.sin(ang_r), -jnp.sin(ang_c), jnp.sin(ang_c)], axis=-1)
    rep = LANES // rot_dim
    return jnp.tile(cos, (1, rep)), jnp.tile(sin, (1, rep))


def _rows(a, idx):
    return a.at[idx].get(mode="promise_in_bounds")


def _route(idx, counts_f):
    quarter = LANES // 4
    experts = jnp.arange(N_EXPERTS, dtype=jnp.int32)
    e2 = jnp.stack([idx[:, 0], idx[:, quarter]], axis=1)
    rank2 = jnp.stack([idx[:, 2 * quarter], idx[:, 3 * quarter]], axis=1)
    counts = counts_f[0, :N_EXPERTS].astype(jnp.int32)
    tiles = (counts + TME - 1) // TME
    tile_end = jnp.cumsum(tiles)
    tile_start = tile_end - tiles
    start = jnp.cumsum(counts) - counts
    onehot = e2[:, :, None] == experts[None, None, :]
    pos = jnp.sum(jnp.where(onehot, tile_start[None, None, :] * TME, 0), axis=-1) + rank2
    tile_id = jnp.arange(NT_E, dtype=jnp.int32)
    tile_expert = jnp.minimum(jnp.sum((tile_id[:, None] >= tile_end[None, :]).astype(jnp.int32), axis=1),
                              N_EXPERTS - 1).astype(jnp.int32)
    tile_valid = (tile_id < tile_end[-1]).astype(jnp.int32)
    t_oh = tile_expert[:, None] == experts[None, :]
    t_first = jnp.sum(jnp.where(t_oh, (start - tile_start * TME)[None, :], 0), axis=1) + tile_id * TME
    t_last = jnp.sum(jnp.where(t_oh, (start + counts)[None, :], 0), axis=1)
    order = jnp.argsort(e2.reshape(-1), stable=True).astype(jnp.int32)
    n_assign = order.shape[0]
    g_idx = t_first[:, None] + jnp.arange(TME, dtype=jnp.int32)[None, :]
    live = jnp.logical_and(g_idx < t_last[:, None], tile_valid[:, None] == 1)
    src = _rows(order, jnp.clip(g_idx, 0, n_assign - 1).reshape(-1)) // 2
    src_tok = jnp.where(live.reshape(-1), src, 0).astype(jnp.int32)

    n_tiles = tile_end[-1]
    last_tile = jnp.maximum(n_tiles - 1, 0)
    last_expert = jnp.sum(jnp.where(tile_id == last_tile, tile_expert, 0))
    t_tstart = jnp.sum(jnp.where(t_oh, tile_start[None, :], 0), axis=1)
    down_tables = (
        jnp.where(tile_valid == 1, tile_id, last_tile).astype(jnp.int32),
        tile_expert,
        jnp.logical_and(tile_valid == 1, tile_id == t_tstart).astype(jnp.int32),
        tile_valid,
    )
    s_id = jnp.arange(NC_E * NT_E, dtype=jnp.int32)
    s_exp = jnp.minimum(jnp.sum((s_id[:, None] >= NC_E * tile_end[None, :]).astype(jnp.int32), axis=1),
                        N_EXPERTS - 1)
    s_oh = s_exp[:, None] == experts[None, :]
    s_tstart = jnp.sum(jnp.where(s_oh, tile_start[None, :], 0), axis=1)
    s_ntile = jnp.maximum(jnp.sum(jnp.where(s_oh, tiles[None, :], 0), axis=1), 1)
    rel = s_id - NC_E * s_tstart
    s_valid = s_id < NC_E * n_tiles
    up_tables = (
        jnp.where(s_valid, s_tstart + rel % s_ntile, last_tile).astype(jnp.int32),
        jnp.where(s_valid, rel // s_ntile, NC_E - 1).astype(jnp.int32),
        jnp.where(s_valid, s_exp, last_expert).astype(jnp.int32),
        jnp.logical_and(s_valid, rel % s_ntile == 0).astype(jnp.int32),
        s_valid.astype(jnp.int32),
    )
    return src_tok, pos, up_tables, down_tables


def lambda_init(layer):
    return 0.8 - 0.6 * math.exp(-0.3 * layer)


def kernel(x_prompt, x_sample, state_a, cache_b_k, cache_b_v, cache_c_k, cache_c_v, c, c_ctx, w_mod, b_mod, norm_mix, norm_ffn, w_in_even, w_gate2_a, b_gate_a, g_gla, lam_q1, lam_k1, lam_q2, lam_k2, g_sub_b, w_out_even, w_in_odd, g_q_c, g_k_c, w_out_odd, ffn_gate, ffn_up, ffn_down, w_router, exp_gate, exp_up, exp_down, norm_final):
    x_ctx, x_lat = x_prompt.reshape(N_CTX, D_MODEL), x_sample.reshape(N_LAT, D_MODEL)
    cond =jnp.concatenate([c_ctx[None, :], c, jnp.zeros((N_COND - 1 - DEC_BATCH, D_MODEL), F32)], axis=0)
    mod = _modulation(cond, w_mod, b_mod).reshape(DEPTH, N_COND, 1, 6 * D_MODEL)

    w = w_in_even[0]
    na = 2 * A_QK + 2 * A_V
    gate_lo = na
    gate_hi = na + 2 * GATE_RANK
    col_scale = jnp.concatenate([jnp.full((A_QK,), DK_A ** -0.5, F32), jnp.ones((na - A_QK,), F32),
                                 jnp.full((B_QK,), DH_B ** -0.5, F32), jnp.ones((B_QK + B_V,), F32)])
    w_main = (jnp.concatenate([w[:, :gate_lo], w[:, gate_hi:]], axis=1) * col_scale).astype(BF16)
    w_gates = jnp.pad(w[:, gate_lo:gate_hi], ((0, 0), (0, LANES - 2 * GATE_RANK))).astype(BF16)
    g2 = jnp.zeros((LANES, 2 * A_QK), F32)
    g2 = g2.at[:GATE_RANK, :A_QK].set(w_gate2_a[0, 0]).at[GATE_RANK:2 * GATE_RANK, A_QK:].set(w_gate2_a[0, 1])
    bg = b_gate_a[0].reshape(1, 2 * A_QK)
    cos_b, sin_b = _rope_tables(DH_B)
    a_proj, glog, qb, kb, vb, kb_ctx, vb_ctx = _inproj_even(
        x_ctx, x_lat, norm_mix[0][None, :], mod[0], w_main, w_gates, g2.astype(BF16), bg, cos_b, sin_b)

    s0_all = jnp.concatenate([jnp.zeros((1, 2, A_QK, DV_A), F32),
                              state_a[:, 0].reshape(DEC_BATCH, 2, A_QK, DV_A)], axis=0)
    sel = np.arange(A_QK)[:, None] // DK_A == np.arange(LANES)[None, :] // GLA_SB
    m1 = jnp.asarray(sel.astype(np.float32)).astype(BF16)
    o_f, o_b, s_fin = _gla(a_proj, glog, s0_all, m1)

    k_lat = jnp.concatenate([kb[N_CTX:].reshape(DEC_BATCH, DEC_SEQ, B_QK),
                             cache_b_k[:, 0].reshape(DEC_BATCH, PAST_LEN, B_QK).astype(BF16)], axis=1)
    v_lat = jnp.concatenate([vb[N_CTX:].reshape(DEC_BATCH, DEC_SEQ, B_V),
                             cache_b_v[:, 0].reshape(DEC_BATCH, PAST_LEN, B_V).astype(BF16)], axis=1)
    lam_params = [p[0][None, :] for p in (lam_q1, lam_k1, lam_q2, lam_k2)]
    attn_b = _diffattn(qb, kb, vb, k_lat, v_lat, lam_params, g_sub_b[0][None, :], lambda_init(0))

    x = _postmix_even(o_f, o_b, a_proj, *attn_b, x_ctx, x_lat, mod[0], g_gla[0][None, :],
                      w_out_even[0].astype(BF16))
    x = _ffn_even(x, norm_ffn[0][None, :], mod[0], ffn_gate[0].astype(BF16), ffn_up[0].astype(BF16),
                  ffn_down[0].astype(BF16))

    cos_c, sin_c = _rope_tables(DH_C)
    q_c, k_c, v_c, kc_ctx, vc_ctx = _inproj_odd(x, norm_mix[1][None, :], mod[1], w_in_odd[0].astype(BF16),
                                     g_q_c[0][None, :], g_k_c[0][None, :], cos_c, sin_c)
    nkv = HKV_C * DH_C
    k_lat = jnp.concatenate([k_c[N_CTX:].reshape(DEC_BATCH, DEC_SEQ, nkv),
                             cache_c_k[:, 0].reshape(DEC_BATCH, PAST_LEN, nkv).astype(BF16)], axis=1)
    v_lat = jnp.concatenate([v_c[N_CTX:].reshape(DEC_BATCH, DEC_SEQ, nkv),
                             cache_c_v[:, 0].reshape(DEC_BATCH, PAST_LEN, nkv).astype(BF16)], axis=1)
    attn_c = _gqa(q_c, k_c, v_c, k_lat, v_lat)
    w_r = jnp.pad(w_router[0], ((0, 0), (0, LANES - N_EXPERTS)))
    x1, h_moe, w1, w2, idx, counts = _postmix_odd(*attn_c, x, mod[1], w_out_odd[0].astype(BF16),
                                                  norm_ffn[1][None, :], w_r)

    src_tok, pos, up_tables, down_tables = _route(idx, counts)
    xs = _rows(h_moe, src_tok)
    ys = _experts(up_tables, down_tables, xs, exp_gate[0], exp_up[0], exp_down[0])
    y1 = _rows(ys, pos[:, 0])
    y2 = _rows(ys, pos[:, 1])
    y_ctx, y_lat = _combine(x1, y1, y2, w1, w2, mod[1], norm_final[None, :])

    y_prompt = y_ctx.reshape(BATCH, SEQ, D_MODEL)
    y_sample = y_lat.reshape(DEC_BATCH, DEC_SEQ, D_MODEL)
    new_state_a = s_fin[:BATCH].reshape(BATCH, 1, 2, H_A, DK_A, DV_A)
    new_b_k = kb_ctx.reshape(BATCH, 1, SEQ, H_B, 2, DH_B)
    new_b_v = vb_ctx.reshape(BATCH, 1, SEQ, H_B, DV_B)
    new_c_k = kc_ctx.reshape(BATCH, 1, SEQ, HKV_C, DH_C)
    new_c_v = vc_ctx.reshape(BATCH, 1, SEQ, HKV_C, DH_C)
    return (y_prompt, y_sample, new_state_a, new_b_k, new_b_v, new_c_k, new_c_v)
```

```python
import functools
import math

import numpy as np
import jax
import jax.numpy as jnp
from jax import lax
from jax.experimental import pallas as pl
from jax.experimental.pallas import tpu as pltpu

D_MODEL = 1024
BATCH = 32
SEQ = 256
DEPTH = 2
DEC_BATCH = 2
DEC_SEQ = 2048
PAST_LEN = 512
GRID_W = 64
H_A, DK_A, DV_A = 4, 64, 128
GATE_RANK = 16
GATE_TAU = 16.0
H_B, DH_B = 4, 64
DV_B = 2 * DH_B
H_C, HKV_C, DH_C = 8, 2, 128
D_FF = 2816
N_EXPERTS = 8
D_FF_E = 3584
ROPE_THETA = 10000.0
EPS = 1e-6
A_QK = H_A * DK_A
A_V = H_A * DV_A
B_QK = H_B * 2 * DH_B
B_V = H_B * DV_B

F32 = jnp.float32
BF16 = jnp.bfloat16
LOG2E = math.log2(math.e)

V7X_VMEM_BYTES = 64 * 1024 * 1024
VMEM_LIMIT = V7X_VMEM_BYTES * 7 // 8
LANES = 128
MXU_N = 256
TM = 256
N_CTX = BATCH * SEQ
N_LAT = DEC_BATCH * DEC_SEQ
M_TOK = N_CTX + N_LAT
NB = M_TOK // TM
NCB = N_CTX // TM
LBB = DEC_SEQ // TM
N_COND = 8
GLA_C = 64
GLA_SB = 8
TME = 512
NT_E = 2 * M_TOK // TME + N_EXPERTS
NC_E = 2
TF_E = D_FF_E // NC_E
TQ_LAT = 256

assert SEQ == TM and DEC_SEQ % TM == 0 and DEPTH == 2


def _cparams(sem):
    return pltpu.CompilerParams(dimension_semantics=sem, vmem_limit_bytes=VMEM_LIMIT)


def _group(i):
    return jnp.where(i < NCB, 0, 1 + (i - NCB) // LBB)


def _pos_block(i):
    return jnp.where(i < NCB, 0, (i - NCB) % LBB)


def _mod_spec(chunk):
    return pl.BlockSpec((None, 1, D_MODEL), lambda i, c=chunk: (_group(i), 0, c))


def _row_spec(width, col=0):
    return pl.BlockSpec((TM, width), lambda i, c=col: (i, c))


def _ctx_spec(width):
    return pl.BlockSpec((TM, width), lambda i: (jnp.minimum(i, NCB - 1), 0))


def _lat_spec(width):
    return pl.BlockSpec((TM, width), lambda i: (jnp.maximum(i - NCB, 0), 0))


def _const_spec(shape):
    nd = len(shape)
    return pl.BlockSpec(shape, lambda i, nd=nd: (0,) * nd)


def _dot(a, b):
    return jnp.dot(a, b, preferred_element_type=F32)


def _dot_nt(a, b):
    return lax.dot_general(a, b, (((1,), (1,)), ((), ())), preferred_element_type=F32)


def _split2(a):
    hi = a.astype(BF16)
    lo = (a - hi.astype(F32)).astype(BF16)
    return hi, lo


def _dot_hi(a, b):
    a_hi, a_lo = _split2(a)
    b_hi, b_lo = _split2(b)
    return _dot(a_hi, b_hi) + (_dot(a_hi, b_lo) + _dot(a_lo, b_hi))


def _silu(x):
    return (0.5 * x) * (1.0 + jnp.tanh(0.5 * x))


def _log_sigmoid(x):
    return jnp.minimum(x, 0.0) - jnp.log(1.0 + jnp.exp(-jnp.abs(x)))


def _rms(x, g):
    return x * lax.rsqrt(jnp.mean(x * x, axis=-1, keepdims=True) + EPS) * g


def _norm_mod(x, gamma, sc, sh):
    return _rms(x, gamma) * (1.0 + sc) + sh


def _pick_rows(ctx_ref, lat_ref):
    return jnp.where(pl.program_id(0) >= NCB, lat_ref[...], ctx_ref[...])


def _copy_ctx(stage_ref, outs):
    piece = 64

    @pl.when(pl.program_id(0) < NCB)
    def _():
        def body(r, carry):
            rows = pl.ds(pl.multiple_of(r * piece, piece), piece)
            for lo, dst_ref in outs:
                dst_ref[rows, :] = stage_ref[rows, lo:lo + dst_ref.shape[1]]
            return carry

        lax.fori_loop(0, stage_ref.shape[0] // piece, body, 0)


def _rope(x, cos, sin, half):
    lane = lax.broadcasted_iota(jnp.int32, x.shape, 1)
    first = (lane % (2 * half)) < half
    swapped = jnp.where(first, pltpu.roll(x, LANES - half, 1), pltpu.roll(x, half, 1))
    return x * cos + swapped * sin


def _mod_kernel(cond_ref, w_ref, b_ref, o_ref):
    o_ref[...] = _dot_hi(_silu(cond_ref[...]), w_ref[...]) + b_ref[...]


def _modulation(cond, w_mod, b_mod):
    tn = 1536
    return pl.pallas_call(
        _mod_kernel,
        grid=(DEPTH, 6 * D_MODEL // tn),
        in_specs=[
            pl.BlockSpec((N_COND, D_MODEL), lambda l, j: (0, 0)),
            pl.BlockSpec((None, D_MODEL, tn), lambda l, j: (l, 0, j)),
            pl.BlockSpec((None, 1, tn), lambda l, j: (l, 0, j)),
        ],
        out_specs=pl.BlockSpec((None, N_COND, tn), lambda l, j: (l, 0, j)),
        out_shape=jax.ShapeDtypeStruct((DEPTH, N_COND, 6 * D_MODEL), F32),
        compiler_params=_cparams(("arbitrary", "arbitrary")),
        name="modulation",
    )(cond, w_mod, b_mod.reshape(DEPTH, 1, 6 * D_MODEL))


def _inproj_even_kernel(xc_ref, xl_ref, gam_ref, sh_ref, sc_ref, w_ref, wg_ref, g2_ref, bg_ref, cos_ref, sin_ref,
                        a_ref, glog_ref, qb_ref, kb_ref, vb_ref, kctx_ref, vctx_ref, stage_ref):
    is_lat = pl.program_id(0) >= NCB
    hb = _norm_mod(_pick_rows(xc_ref, xl_ref), gam_ref[...], sc_ref[...], sh_ref[...]).astype(BF16)
    cos = jnp.where(is_lat, cos_ref[...], 1.0)
    sin = jnp.where(is_lat, sin_ref[...], 0.0)
    na = 2 * A_QK + 2 * A_V
    for lo in range(0, na, MXU_N):
        a_ref[:, lo:lo + MXU_N] = _dot(hb, w_ref[:, lo:lo + MXU_N])
    for lo in range(0, B_QK, MXU_N):
        zq = _dot(hb, w_ref[:, na + lo:na + lo + MXU_N])
        zk = _dot(hb, w_ref[:, na + B_QK + lo:na + B_QK + lo + MXU_N])
        zv = _dot(hb, w_ref[:, na + 2 * B_QK + lo:na + 2 * B_QK + lo + MXU_N])
        stage_ref[:, lo:lo + MXU_N] = zk
        stage_ref[:, B_QK + lo:B_QK + lo + MXU_N] = zv
        for j in range(0, MXU_N, LANES):
            cs = slice(lo + j, lo + j + LANES)
            vb_ref[:, 2 * (lo + j):2 * (lo + j) + DV_B] = zv[:, j:j + LANES].astype(BF16)
            vb_ref[:, 2 * (lo + j) + DV_B:2 * (lo + j + LANES)] = jnp.ones((TM, DV_B), BF16)
            qb_ref[:, cs] = (_rope(zq[:, j:j + LANES], cos, sin, DH_B // 4) * LOG2E).astype(BF16)
            kb_ref[:, cs] = _rope(zk[:, j:j + LANES], cos, sin, DH_B // 4).astype(BF16)
    gates = _dot(hb, wg_ref[...])
    xg = _dot(gates.astype(BF16), g2_ref[...]) + bg_ref[...]
    glog_ref[...] = _log_sigmoid(xg) * (1.0 / GATE_TAU)
    _copy_ctx(stage_ref, ((0, kctx_ref), (B_QK, vctx_ref)))


def _inproj_even(x_ctx, x_lat, gamma, mod, w_main, w_gates, g2, bg, cos, sin):
    na = 2 * A_QK + 2 * A_V
    nz = na + 2 * B_QK + B_V
    return pl.pallas_call(
        _inproj_even_kernel,
        grid=(NB,),
        in_specs=[
            _ctx_spec(D_MODEL), _lat_spec(D_MODEL), _const_spec((1, D_MODEL)), _mod_spec(0), _mod_spec(1),
            _const_spec((D_MODEL, nz)), _const_spec((D_MODEL, LANES)), _const_spec((LANES, 2 * A_QK)),
            _const_spec((1, 2 * A_QK)),
            pl.BlockSpec((TM, LANES), lambda i: (_pos_block(i), 0)),
            pl.BlockSpec((TM, LANES), lambda i: (_pos_block(i), 0)),
        ],
        out_specs=[_pair_spec(na), _pair_spec(2 * A_QK), _row_spec(B_QK), _row_spec(B_QK), _row_spec(2 * B_V),
                   _ctx_spec(B_QK), _ctx_spec(B_V)],
        out_shape=[
            jax.ShapeDtypeStruct((M_TOK, na), F32), jax.ShapeDtypeStruct((M_TOK, 2 * A_QK), F32),
            jax.ShapeDtypeStruct((M_TOK, B_QK), BF16), jax.ShapeDtypeStruct((M_TOK, B_QK), BF16),
            jax.ShapeDtypeStruct((M_TOK, 2 * B_V), BF16), jax.ShapeDtypeStruct((N_CTX, B_QK), F32),
            jax.ShapeDtypeStruct((N_CTX, B_V), F32),
        ],
        scratch_shapes=[pltpu.VMEM((TM, B_QK + B_V), F32)],
        compiler_params=_cparams(("arbitrary",)),
        name="inproj_even",
    )(x_ctx, x_lat, gamma, mod, mod, w_main, w_gates, g2, bg, cos, sin)


def _gla_chunk(q_ref, k_ref, v_ref, g_ref, o_ref, c, s_ref, a_ref, b_ref, m1_ref, head_mask, same_block, rev):
    q, k, v = q_ref[c], k_ref[c], v_ref[c]
    g = g_ref[c] * LOG2E
    v_b = v.astype(BF16)
    r_i = lax.broadcasted_iota(jnp.int32, (GLA_C, GLA_C), 0)
    c_i = lax.broadcasted_iota(jnp.int32, (GLA_C, GLA_C), 1)
    tri = jnp.where((c_i >= r_i) if rev else (c_i <= r_i), 1.0, 0.0).astype(BF16)
    g_hi = g.astype(BF16)
    g_r = g - g_hi.astype(F32)
    g_mid = g_r.astype(BF16)
    g_lo = (g_r - g_mid.astype(F32)).astype(BF16)
    bc = _dot(tri, g_hi) + (_dot(tri, g_mid) + _dot(tri, g_lo))
    b_ref[...] = bc
    yield
    t_i = lax.broadcasted_iota(jnp.int32, (GLA_SB, A_QK), 0)
    group = 2 * GLA_SB
    for i0 in range(0, GLA_C, group):
        blocks = range(i0, i0 + group, GLA_SB)
        q_i = [q_ref[c, i:i + GLA_SB, :] for i in blocks]
        b_i = [b_ref[i:i + GLA_SB, :] for i in blocks]
        for j in range(GLA_SB):
            keep = (t_i <= j) if rev else (t_i >= j)
            pair = []
            for n, i in enumerate(blocks):
                k_s = k_ref[c, i + j:i + j + 1, :]
                b_s = b_ref[i + j:i + j + 1, :]
                pair.append(jnp.where(keep, q_i[n] * k_s * jnp.exp2(b_i[n] - b_s), 0.0))
            a_ref[j * GLA_C + i0:j * GLA_C + i0 + group, :] = jnp.concatenate(pair, axis=0).astype(BF16)
    yield
    last = bc[0:1] if rev else bc[GLA_C - 1:GLA_C]
    qc = q * jnp.exp2(bc)
    kl = k * jnp.exp2(last - bc)
    row = lax.broadcasted_iota(jnp.int32, (GLA_C, A_QK), 0)

    def expand(x):
        return jnp.where(head_mask, jnp.concatenate([x] * H_A, axis=0), 0.0).astype(BF16)

    scores = None
    half = GLA_C // 2
    while half >= GLA_SB:
        mids = [bc[b + half:b + half + 1] if rev else bc[b + half - 1:b + half] for b in range(0, GLA_C, 2 * half)]
        ref = mids[0] if len(mids) == 1 else jnp.concatenate(
            [jnp.broadcast_to(m, (2 * half, A_QK)) for m in mids], axis=0)
        upper = (row // half) % 2 == 1
        later, earlier = (jnp.logical_not(upper), upper) if rev else (upper, jnp.logical_not(upper))
        q_l = jnp.where(later, q * jnp.exp2(bc - ref), 0.0)
        k_e = jnp.where(earlier, k * jnp.exp2(ref - bc), 0.0).astype(BF16)
        term = _dot_nt(expand(q_l), k_e)
        if 2 * half < GLA_C:
            term = jnp.where(same_block[2 * half], term, 0.0)
        scores = term if scores is None else scores + term
        half //= 2
    s_old = s_ref[...]
    inter = _dot(expand(qc), s_old.astype(BF16))
    sums = _dot(a_ref[...], m1_ref[...])
    yield
    lane_c = lax.broadcasted_iota(jnp.int32, (GLA_C, LANES), 1)
    within = None
    for j in range(GLA_SB):
        part = jnp.where(lane_c % GLA_SB == j, sums[j * GLA_C:(j + 1) * GLA_C], 0.0)
        within = part if within is None else within + part
    lane = lax.broadcasted_iota(jnp.int32, (GLA_SB, LANES), 1)
    per_head = []
    for h in range(H_A):
        rows_h = []
        for i in range(0, GLA_C, GLA_SB):
            moved = pltpu.roll(within[i:i + GLA_SB], (i - h * GLA_SB) % LANES, 1)
            rows_h.append(jnp.where(lane // GLA_SB == i // GLA_SB, moved, 0.0))
        per_head.append(jnp.concatenate(rows_h, axis=0)[:, :GLA_C])
    scores = scores + jnp.concatenate(per_head, axis=0)
    o_heads = []
    for h in range(H_A):
        hs = slice(h * GLA_C, (h + 1) * GLA_C)
        o_heads.append(inter[hs] + _dot(scores[hs].astype(BF16), v_b[:, h * DV_A:(h + 1) * DV_A]))
    o_ref[c] = jnp.concatenate(o_heads, axis=1)
    yield
    t = jnp.transpose(jnp.concatenate([kl, jnp.broadcast_to(last, (GLA_C, A_QK))], axis=0))
    kv = _dot(t[:, :GLA_C].astype(BF16), v_b)
    a_col = jnp.exp2(t[:, GLA_C:GLA_C + 1])
    for h in range(H_A):
        hs = slice(h * DK_A, (h + 1) * DK_A)
        s_ref[hs, :] = a_col[hs] * s_old[hs] + kv[hs, h * DV_A:(h + 1) * DV_A]


def _gla_kernel(fblk_ref, bblk_ref, first_ref, inita_ref, initb_ref,
                qf_ref, kf_ref, vf_ref, gf_ref, qr_ref, kr_ref, vr_ref, gr_ref, s0a_ref, s0b_ref, m1_ref,
                of_ref, ob_ref, sfin_ref, s_refs, a_refs, b_refs):
    step = pl.program_id(0)

    @pl.when(first_ref[step] == 1)
    def _():
        for seq, s0_ref in enumerate((s0a_ref, s0b_ref)):
            for d in range(2):
                s_refs[2 * seq + d] = s0_ref[d]

    r_h = lax.broadcasted_iota(jnp.int32, (H_A * GLA_C, A_QK), 0) // GLA_C
    c_h = lax.broadcasted_iota(jnp.int32, (H_A * GLA_C, A_QK), 1) // DK_A
    head_mask = r_h == c_h
    t_s = lax.broadcasted_iota(jnp.int32, (H_A * GLA_C, GLA_C), 0) % GLA_C
    s_s = lax.broadcasted_iota(jnp.int32, (H_A * GLA_C, GLA_C), 1)
    same_block = {}
    size = 2 * GLA_SB
    while size < GLA_C:
        same_block[size] = (t_s // size) == (s_s // size)
        size *= 2
    nch = TM // GLA_C

    def body(c, carry):
        chains = []
        for seq in range(2):
            chains.append(_gla_chunk(qf_ref, kf_ref, vf_ref, gf_ref, of_ref, seq * nch + c, s_refs.at[2 * seq],
                                     a_refs.at[2 * seq], b_refs.at[2 * seq], m1_ref, head_mask, same_block, False))
            chains.append(_gla_chunk(qr_ref, kr_ref, vr_ref, gr_ref, ob_ref, seq * nch + nch - 1 - c,
                                     s_refs.at[2 * seq + 1], a_refs.at[2 * seq + 1], b_refs.at[2 * seq + 1], m1_ref,
                                     head_mask, same_block, True))
        live = chains
        while live:
            live = [ch for ch in live if next(ch, True) is None]
        return carry

    lax.fori_loop(0, nch, body, 0)
    for seq in range(2):
        for d in range(2):
            sfin_ref[seq, d] = s_refs[2 * seq + d]


def _pair_block(i):
    j = i - NCB
    return jnp.where(i < NCB, i, NCB + 2 * (j % LBB) + j // LBB)


def _pair_spec(width, col=0):
    return pl.BlockSpec((TM, width), lambda i, c=col: (_pair_block(i), c))


def _gla_tables():
    fblk, bblk, first, init_a, init_b = [], [], [], [], []
    for p in range(BATCH // 2):
        fblk.append(p), bblk.append(p), first.append(1), init_a.append(0), init_b.append(0)
    for j in range(LBB):
        fblk.append(NCB // 2 + j)
        bblk.append(NCB // 2 + LBB - 1 - j)
        first.append(1 if j == 0 else 0)
        init_a.append(1), init_b.append(2)
    return [jnp.asarray(np.array(t, np.int32)) for t in (fblk, bblk, first, init_a, init_b)]


def _gla(a_proj, glog, s0_all, m1):
    tables = _gla_tables()
    nsteps = int(tables[0].shape[0])
    nseq = BATCH + DEC_BATCH

    def fmap(col):
        return lambda i, fb, bb, fi, ia, ib: (fb[i], 0, col)

    def rmap(col):
        return lambda i, fb, bb, fi, ia, ib: (bb[i], 0, col)

    nch = 2 * TM // GLA_C
    a3 = a_proj.reshape(M_TOK // GLA_C, GLA_C, a_proj.shape[1])
    g3 = glog.reshape(M_TOK // GLA_C, GLA_C, glog.shape[1])
    gs = pltpu.PrefetchScalarGridSpec(
        num_scalar_prefetch=5,
        grid=(nsteps,),
        in_specs=[
            pl.BlockSpec((nch, GLA_C, A_QK), fmap(0)), pl.BlockSpec((nch, GLA_C, A_QK), fmap(1)),
            pl.BlockSpec((nch, GLA_C, A_V), fmap(1)), pl.BlockSpec((nch, GLA_C, A_QK), fmap(0)),
            pl.BlockSpec((nch, GLA_C, A_QK), rmap(0)), pl.BlockSpec((nch, GLA_C, A_QK), rmap(1)),
            pl.BlockSpec((nch, GLA_C, A_V), rmap(1)), pl.BlockSpec((nch, GLA_C, A_QK), rmap(1)),
            pl.BlockSpec((None, 2, A_QK, DV_A), lambda i, fb, bb, fi, ia, ib: (ia[i], 0, 0, 0)),
            pl.BlockSpec((None, 2, A_QK, DV_A), lambda i, fb, bb, fi, ia, ib: (ib[i], 0, 0, 0)),
            pl.BlockSpec((A_QK, LANES), lambda i, fb, bb, fi, ia, ib: (0, 0)),
        ],
        out_specs=[
            pl.BlockSpec((nch, GLA_C, A_V), fmap(0)), pl.BlockSpec((nch, GLA_C, A_V), rmap(0)),
            pl.BlockSpec((2, 2, A_QK, DV_A), lambda i, fb, bb, fi, ia, ib: (jnp.minimum(fb[i], NCB // 2), 0, 0, 0)),
        ],
        scratch_shapes=[
            pltpu.VMEM((4, A_QK, DV_A), F32),
            pltpu.VMEM((4, GLA_SB * GLA_C, A_QK), BF16),
            pltpu.VMEM((4, GLA_C, A_QK), F32),
        ],
    )
    o_f, o_b, s_fin = pl.pallas_call(
        _gla_kernel,
        grid_spec=gs,
        out_shape=[
            jax.ShapeDtypeStruct((M_TOK // GLA_C, GLA_C, A_V), F32),
            jax.ShapeDtypeStruct((M_TOK // GLA_C, GLA_C, A_V), F32),
            jax.ShapeDtypeStruct((nseq, 2, A_QK, DV_A), F32),
        ],
        compiler_params=_cparams(("arbitrary",)),
        name="gla",
    )(*tables, a3, a3, a3, g3, a3, a3, a3, g3, s0_all, s0_all, m1)
    return o_f.reshape(M_TOK, A_V), o_b.reshape(M_TOK, A_V), s_fin


def _attend(q_b, k_b, v_ones):
    s2 = _dot_nt(q_b, k_b)
    e = jnp.exp2(s2 - jnp.max(s2, axis=-1, keepdims=True)).astype(BF16)
    both = _dot(e, v_ones)
    dv = v_ones.shape[1] // 2
    return both[:, :dv] * (1.0 / both[:, dv:])


def _diffattn_body(q_ref, k_ref, v_ref, lq1_ref, lk1_ref, lq2_ref, lk2_ref, gsub_ref, o_ref, lam_init):
    lam = (jnp.exp(jnp.sum(lq1_ref[...] * lk1_ref[...], axis=-1, keepdims=True))
           - jnp.exp(jnp.sum(lq2_ref[...] * lk2_ref[...], axis=-1, keepdims=True)) + lam_init)
    lane = lax.broadcasted_iota(jnp.int32, (q_ref.shape[0], LANES), 1)
    for h in range(H_B):
        hs = slice(h * LANES, (h + 1) * LANES)
        q_h = q_ref[:, hs]
        k_h = k_ref[:, hs]
        v_h = v_ref[:, 2 * h * DV_B:2 * (h + 1) * DV_B]
        zero = jnp.zeros_like(q_h)
        o = (_attend(jnp.where(lane < DH_B, q_h, zero), k_h, v_h)
             - lam * _attend(jnp.where(lane >= DH_B, q_h, zero), k_h, v_h))
        o_ref[:, hs] = (_rms(o, gsub_ref[...]) * (1.0 - lam_init)).astype(BF16)


def _diffattn_kernel(q_ref, k_ref, v_ref, lq1, lk1, lq2, lk2, gsub, o_ref, *, lam_init):
    _diffattn_body(q_ref, k_ref, v_ref, lq1, lk1, lq2, lk2, gsub, o_ref, lam_init)


def _diffattn(qb, kb, vb, k_lat, v_lat, lam_params, gsub, lam_init):
    small = [pl.BlockSpec((1, DH_B), lambda *_: (0, 0))] * 4 + [pl.BlockSpec((1, DV_B), lambda *_: (0, 0))]
    body = functools.partial(_diffattn_kernel, lam_init=lam_init)
    ctx = pl.pallas_call(
        body,
        grid=(NCB,),
        in_specs=[_row_spec(B_QK), _row_spec(B_QK), _row_spec(2 * B_V)] + small,
        out_specs=_row_spec(B_V),
        out_shape=jax.ShapeDtypeStruct((N_CTX, B_V), BF16),
        compiler_params=_cparams(("arbitrary",)),
        name="diffattn_ctx",
    )(qb, kb, vb, *lam_params, gsub)
    nk = DEC_SEQ + PAST_LEN
    nqb = DEC_SEQ // TQ_LAT
    off = N_CTX // TQ_LAT
    lat = pl.pallas_call(
        body,
        grid=(DEC_BATCH, nqb),
        in_specs=[
            pl.BlockSpec((TQ_LAT, B_QK), lambda b, j: (off + b * nqb + j, 0)),
            pl.BlockSpec((None, nk, B_QK), lambda b, j: (b, 0, 0)),
            pl.BlockSpec((None, nk, 2 * B_V), lambda b, j: (b, 0, 0)),
        ] + small,
        out_specs=pl.BlockSpec((TQ_LAT, B_V), lambda b, j: (b * nqb + j, 0)),
        out_shape=jax.ShapeDtypeStruct((N_LAT, B_V), BF16),
        compiler_params=_cparams(("arbitrary", "arbitrary")),
        name="diffattn_lat",
    )(qb, k_lat, v_lat, *lam_params, gsub)
    return ctx, lat


def _postmix_even_kernel(of_ref, ob_ref, ra_ref, actx_ref, alat_ref, xc_ref, xl_ref, g1_ref, ggla_ref, wo_ref,
                         o_ref):
    heads = []
    for h in range(H_A):
        hs = slice(h * DV_A, (h + 1) * DV_A)
        heads.append((_rms(of_ref[:, hs] + ob_ref[:, hs], ggla_ref[...]) * _silu(ra_ref[:, hs])).astype(BF16))
    mix = jnp.concatenate(heads + [_pick_rows(actx_ref, alat_ref)], axis=1)
    o_ref[...] = _pick_rows(xc_ref, xl_ref) + g1_ref[...] * _dot(mix, wo_ref[...])


def _postmix_even(o_f, o_b, a_proj, attn_ctx, attn_lat, x_ctx, x_lat, mod, g_gla, w_out):
    return pl.pallas_call(
        _postmix_even_kernel,
        grid=(NB,),
        in_specs=[_pair_spec(A_V), _pair_spec(A_V), _pair_spec(A_V, 2), _ctx_spec(B_V), _lat_spec(B_V),
                  _ctx_spec(D_MODEL), _lat_spec(D_MODEL), _mod_spec(2), _const_spec((1, DV_A)),
                  _const_spec((A_V + B_V, D_MODEL))],
        out_specs=_row_spec(D_MODEL),
        out_shape=jax.ShapeDtypeStruct((M_TOK, D_MODEL), F32),
        compiler_params=_cparams(("arbitrary",)),
        name="postmix_even",
    )(o_f, o_b, a_proj, attn_ctx, attn_lat, x_ctx, x_lat, mod, g_gla, w_out)


def _swiglu_act(xb, wg_ref, wu_ref, act_ref):
    for lo in range(0, act_ref.shape[1], MXU_N):
        cs = slice(lo, lo + MXU_N)
        act_ref[:, cs] = (_silu(_dot(xb, wg_ref[:, cs])) * _dot(xb, wu_ref[:, cs])).astype(BF16)


def _ffn_even_kernel(x_ref, gam_ref, sh_ref, sc_ref, g2_ref, wg_ref, wu_ref, wd_ref, o_ref, act_ref):
    x = x_ref[...]
    hb = _norm_mod(x, gam_ref[...], sc_ref[...], sh_ref[...]).astype(BF16)
    _swiglu_act(hb, wg_ref, wu_ref, act_ref)
    o_ref[...] = x + g2_ref[...] * _dot(act_ref[...], wd_ref[...])


def _ffn_even(x, gamma, mod, w_gate, w_up, w_down):
    return pl.pallas_call(
        _ffn_even_kernel,
        grid=(NB,),
        in_specs=[_row_spec(D_MODEL), _const_spec((1, D_MODEL)), _mod_spec(3), _mod_spec(4), _mod_spec(5),
                  _const_spec((D_MODEL, D_FF)), _const_spec((D_MODEL, D_FF)), _const_spec((D_FF, D_MODEL))],
        out_specs=_row_spec(D_MODEL),
        out_shape=jax.ShapeDtypeStruct((M_TOK, D_MODEL), F32),
        scratch_shapes=[pltpu.VMEM((TM, D_FF), BF16)],
        compiler_params=_cparams(("arbitrary",)),
        name="ffn_even",
    )(x, gamma, mod, mod, mod, w_gate, w_up, w_down)


def _inproj_odd_kernel(x_ref, gam_ref, sh_ref, sc_ref, w_ref, gq_ref, gk_ref, cos_ref, sin_ref,
                       q_ref, k_ref, v_ref, kctx_ref, vctx_ref, stage_ref):
    is_lat = pl.program_id(0) >= NCB
    hb = _norm_mod(x_ref[...], gam_ref[...], sc_ref[...], sh_ref[...]).astype(BF16)
    cos = jnp.where(is_lat, cos_ref[...], 1.0)
    sin = jnp.where(is_lat, sin_ref[...], 0.0)
    scale = DH_C ** -0.5 * LOG2E
    nq, nkv = H_C * DH_C, HKV_C * DH_C
    for lo in range(0, nq, MXU_N):
        z = _dot(hb, w_ref[:, lo:lo + MXU_N])
        for j in range(0, MXU_N, DH_C):
            q_ref[:, lo + j:lo + j + DH_C] = (
                _rope(_rms(z[:, j:j + DH_C], gq_ref[...]), cos, sin, DH_C // 4) * scale).astype(BF16)
    z = _dot(hb, w_ref[:, nq:nq + nkv])
    for h in range(HKV_C):
        hs = slice(h * DH_C, (h + 1) * DH_C)
        k_n = _rms(z[:, hs], gk_ref[...])
        stage_ref[:, hs] = k_n
        k_ref[:, hs] = _rope(k_n, cos, sin, DH_C // 4).astype(BF16)
    v = _dot(hb, w_ref[:, nq + nkv:])
    stage_ref[:, nkv:] = v
    for h in range(HKV_C):
        v_ref[:, 2 * h * DH_C:(2 * h + 1) * DH_C] = v[:, h * DH_C:(h + 1) * DH_C].astype(BF16)
        v_ref[:, (2 * h + 1) * DH_C:2 * (h + 1) * DH_C] = jnp.ones((TM, DH_C), BF16)
    _copy_ctx(stage_ref, ((0, kctx_ref), (nkv, vctx_ref)))


def _inproj_odd(x, gamma, mod, w_in, g_q, g_k, cos, sin):
    nq, nkv = H_C * DH_C, HKV_C * DH_C
    return pl.pallas_call(
        _inproj_odd_kernel,
        grid=(NB,),
        in_specs=[
            _row_spec(D_MODEL), _const_spec((1, D_MODEL)), _mod_spec(0), _mod_spec(1),
            _const_spec((D_MODEL, nq + 2 * nkv)), _const_spec((1, DH_C)), _const_spec((1, DH_C)),
            pl.BlockSpec((TM, LANES), lambda i: (_pos_block(i), 0)),
            pl.BlockSpec((TM, LANES), lambda i: (_pos_block(i), 0)),
        ],
        out_specs=[_row_spec(nq), _row_spec(nkv), _row_spec(2 * nkv), _ctx_spec(nkv), _ctx_spec(nkv)],
        out_shape=[
            jax.ShapeDtypeStruct((M_TOK, nq), BF16), jax.ShapeDtypeStruct((M_TOK, nkv), BF16),
            jax.ShapeDtypeStruct((M_TOK, 2 * nkv), BF16), jax.ShapeDtypeStruct((N_CTX, nkv), F32),
            jax.ShapeDtypeStruct((N_CTX, nkv), F32),
        ],
        scratch_shapes=[pltpu.VMEM((TM, 2 * nkv), F32)],
        compiler_params=_cparams(("arbitrary",)),
        name="inproj_odd",
    )(x, gamma, mod, mod, w_in, g_q, g_k, cos, sin)


def _gqa_body(q_ref, k_ref, v_ref, o_ref):
    rep = H_C // HKV_C
    tq = q_ref.shape[0]
    for hk in range(HKV_C):
        ks = slice(hk * DH_C, (hk + 1) * DH_C)
        q_g = jnp.concatenate([q_ref[:, (hk * rep + g) * DH_C:(hk * rep + g + 1) * DH_C] for g in range(rep)],
                              axis=0)
        o = _attend(q_g, k_ref[:, ks], v_ref[:, 2 * hk * DH_C:2 * (hk + 1) * DH_C])
        for g in range(rep):
            o_ref[:, (hk * rep + g) * DH_C:(hk * rep + g + 1) * DH_C] = o[g * tq:(g + 1) * tq].astype(BF16)


def _gqa(q, k, v, k_lat, v_lat):
    nq, nkv = H_C * DH_C, HKV_C * DH_C
    ctx = pl.pallas_call(
        _gqa_body,
        grid=(NCB,),
        in_specs=[_row_spec(nq), _row_spec(nkv), _row_spec(2 * nkv)],
        out_specs=_row_spec(nq),
        out_shape=jax.ShapeDtypeStruct((N_CTX, nq), BF16),
        compiler_params=_cparams(("arbitrary",)),
        name="gqa_ctx",
    )(q, k, v)
    nk = DEC_SEQ + PAST_LEN
    nqb = DEC_SEQ // TQ_LAT
    off = N_CTX // TQ_LAT
    lat = pl.pallas_call(
        _gqa_body,
        grid=(DEC_BATCH, nqb),
        in_specs=[
            pl.BlockSpec((TQ_LAT, nq), lambda b, j: (off + b * nqb + j, 0)),
            pl.BlockSpec((None, nk, nkv), lambda b, j: (b, 0, 0)),
            pl.BlockSpec((None, nk, 2 * nkv), lambda b, j: (b, 0, 0)),
        ],
        out_specs=pl.BlockSpec((TQ_LAT, nq), lambda b, j: (b * nqb + j, 0)),
        out_shape=jax.ShapeDtypeStruct((N_LAT, nq), BF16),
        compiler_params=_cparams(("arbitrary", "arbitrary")),
        name="gqa_lat",
    )(q, k_lat, v_lat)
    return ctx, lat


def _postmix_odd_kernel(actx_ref, alat_ref, x_ref, g1_ref, wo_ref, gam_ref, sh_ref, sc_ref, wr_ref,
                        x1_ref, h_ref, w1_ref, w2_ref, idx_ref, cnt_ref, run_ref):
    @pl.when(pl.program_id(0) == 0)
    def _():
        run_ref[...] = jnp.zeros_like(run_ref)

    x1 = x_ref[...] + g1_ref[...] * _dot(_pick_rows(actx_ref, alat_ref), wo_ref[...])
    x1_ref[...] = x1
    h = _norm_mod(x1, gam_ref[...], sc_ref[...], sh_ref[...])
    h_ref[...] = h
    logits = _dot_hi(h, wr_ref[...])
    lane = lax.broadcasted_iota(jnp.int32, logits.shape, 1)
    lane_f = lane.astype(F32)
    lg = jnp.where(lane < N_EXPERTS, logits, -jnp.inf)
    m1 = jnp.max(lg, axis=-1, keepdims=True)
    i1 = jnp.min(jnp.where(lg == m1, lane_f, float(LANES)), axis=-1, keepdims=True)
    lg2 = jnp.where(lane_f == i1, -jnp.inf, lg)
    m2 = jnp.max(lg2, axis=-1, keepdims=True)
    i2 = jnp.min(jnp.where(lg2 == m2, lane_f, float(LANES)), axis=-1, keepdims=True)
    e = jnp.exp(m2 - m1)
    w1 = 1.0 / (1.0 + e)
    w1_ref[...] = jnp.broadcast_to(w1, logits.shape)
    w2_ref[...] = jnp.broadcast_to(e * w1, logits.shape)
    oh1 = jnp.where(lane_f == i1, 1.0, 0.0)
    oh2 = jnp.where(lane_f == i2, 1.0, 0.0)
    both = oh1 + oh2
    r_i = lax.broadcasted_iota(jnp.int32, (TM, TM), 0)
    c_i = lax.broadcasted_iota(jnp.int32, (TM, TM), 1)
    before = jnp.where(c_i < r_i, 1.0, 0.0).astype(BF16)
    pref = _dot(before, both.astype(BF16)) + run_ref[0:1, :]
    rank1 = jnp.sum(pref * oh1, axis=-1, keepdims=True)
    rank2 = jnp.sum(pref * oh2, axis=-1, keepdims=True)
    run_ref[...] = run_ref[...] + jnp.sum(both, axis=0, keepdims=True)
    cnt_ref[...] = run_ref[...]
    quarter = LANES // 4
    idx_ref[...] = jnp.where(lane < quarter, i1, jnp.where(lane < 2 * quarter, i2, jnp.where(
        lane < 3 * quarter, rank1, rank2))).astype(jnp.int32)


def _postmix_odd(attn_ctx, attn_lat, x, mod, w_out, gamma, w_router):
    return pl.pallas_call(
        _postmix_odd_kernel,
        grid=(NB,),
        in_specs=[_ctx_spec(H_C * DH_C), _lat_spec(H_C * DH_C), _row_spec(D_MODEL), _mod_spec(2),
                  _const_spec((H_C * DH_C, D_MODEL)),
                  _const_spec((1, D_MODEL)), _mod_spec(3), _mod_spec(4), _const_spec((D_MODEL, LANES))],
        out_specs=[_row_spec(D_MODEL), _row_spec(D_MODEL), _row_spec(LANES), _row_spec(LANES), _row_spec(LANES),
                   _const_spec((8, LANES))],
        out_shape=[
            jax.ShapeDtypeStruct((M_TOK, D_MODEL), F32), jax.ShapeDtypeStruct((M_TOK, D_MODEL), F32),
            jax.ShapeDtypeStruct((M_TOK, LANES), F32), jax.ShapeDtypeStruct((M_TOK, LANES), F32),
            jax.ShapeDtypeStruct((M_TOK, LANES), jnp.int32), jax.ShapeDtypeStruct((8, LANES), F32),
        ],
        scratch_shapes=[pltpu.VMEM((8, LANES), F32)],
        compiler_params=_cparams(("arbitrary",)),
        name="postmix_odd",
    )(attn_ctx, attn_lat, x, mod, w_out, gamma, mod, mod, w_router)


def _cast_rows(src_ref, dst_ref, piece):
    def body(r, carry):
        rows = pl.ds(pl.multiple_of(r * piece, piece), piece)
        dst_ref[rows, :] = src_ref[rows, :].astype(BF16)
        return carry

    lax.fori_loop(0, src_ref.shape[0] // piece, body, 0)


def _experts_up_kernel(tile_ref, chunk_ref, exp_ref, first_ref, valid_ref, x_ref, wg_ref, wu_ref, act_ref,
                       wgb_ref, wub_ref):
    s = pl.program_id(0)

    @pl.when(first_ref[s] == 1)
    def _():
        _cast_rows(wg_ref, wgb_ref, 128)
        _cast_rows(wu_ref, wub_ref, 128)

    @pl.when(valid_ref[s] == 1)
    def _():
        _swiglu_act(x_ref[...].astype(BF16), wgb_ref, wub_ref, act_ref)


def _experts_down_kernel(act_tile_ref, exp_ref, first_ref, valid_ref, act_ref, wd_ref, o_ref, wdb_ref):
    i = pl.program_id(0)

    @pl.when(first_ref[i] == 1)
    def _():
        _cast_rows(wd_ref, wdb_ref, 256)

    @pl.when(valid_ref[i] == 1)
    def _():
        o_ref[...] = _dot(act_ref[...], wdb_ref[...])

    @pl.when(valid_ref[i] == 0)
    def _():
        o_ref[...] = jnp.zeros_like(o_ref)


def _experts(up_tables, down_tables, xs, w_gate, w_up, w_down):
    up = pl.pallas_call(
        _experts_up_kernel,
        grid_spec=pltpu.PrefetchScalarGridSpec(
            num_scalar_prefetch=5,
            grid=(NC_E * NT_E,),
            in_specs=[
                pl.BlockSpec((TME, D_MODEL), lambda s, t, c, e, f, v: (t[s], 0)),
                pl.BlockSpec((None, D_MODEL, TF_E), lambda s, t, c, e, f, v: (e[s], 0, c[s])),
                pl.BlockSpec((None, D_MODEL, TF_E), lambda s, t, c, e, f, v: (e[s], 0, c[s])),
            ],
            out_specs=pl.BlockSpec((TME, TF_E), lambda s, t, c, e, f, v: (t[s], c[s])),
            scratch_shapes=[pltpu.VMEM((D_MODEL, TF_E), BF16), pltpu.VMEM((D_MODEL, TF_E), BF16)],
        ),
        out_shape=jax.ShapeDtypeStruct((NT_E * TME, D_FF_E), BF16),
        compiler_params=_cparams(("arbitrary",)),
        name="experts_up",
    )(*up_tables, xs, w_gate, w_up)
    return pl.pallas_call(
        _experts_down_kernel,
        grid_spec=pltpu.PrefetchScalarGridSpec(
            num_scalar_prefetch=4,
            grid=(NT_E,),
            in_specs=[
                pl.BlockSpec((TME, D_FF_E), lambda i, a, e, f, v: (a[i], 0)),
                pl.BlockSpec((None, D_FF_E, D_MODEL), lambda i, a, e, f, v: (e[i], 0, 0)),
            ],
            out_specs=pl.BlockSpec((TME, D_MODEL), lambda i, a, e, f, v: (i, 0)),
            scratch_shapes=[pltpu.VMEM((D_FF_E, D_MODEL), BF16)],
        ),
        out_shape=jax.ShapeDtypeStruct((NT_E * TME, D_MODEL), F32),
        compiler_params=_cparams(("arbitrary",)),
        name="experts_down",
    )(*down_tables, up, w_down)


def _combine_kernel(x_ref, y1_ref, y2_ref, w1_ref, w2_ref, g2_ref, gam_ref, octx_ref, olat_ref):
    rep = D_MODEL // LANES
    w1 = jnp.concatenate([w1_ref[...]] * rep, axis=1)
    w2 = jnp.concatenate([w2_ref[...]] * rep, axis=1)
    x2 = x_ref[...] + g2_ref[...] * (w1 * y1_ref[...] + w2 * y2_ref[...])
    y = _rms(x2, gam_ref[...])
    is_lat = pl.program_id(0) >= NCB

    @pl.when(jnp.logical_not(is_lat))
    def _():
        octx_ref[...] = y

    @pl.when(is_lat)
    def _():
        olat_ref[...] = y


def _combine(x1, y1, y2, w1, w2, mod, gamma):
    return pl.pallas_call(
        _combine_kernel,
        grid=(NB,),
        in_specs=[_row_spec(D_MODEL), _row_spec(D_MODEL), _row_spec(D_MODEL), _row_spec(LANES), _row_spec(LANES),
                  _mod_spec(5), _const_spec((1, D_MODEL))],
        out_specs=[_ctx_spec(D_MODEL), _lat_spec(D_MODEL)],
        out_shape=[jax.ShapeDtypeStruct((N_CTX, D_MODEL), F32), jax.ShapeDtypeStruct((N_LAT, D_MODEL), F32)],
        compiler_params=_cparams(("arbitrary",)),
        name="combine",
    )(x1, y1, y2, w1, w2, mod, gamma)


def _rope_tables(rot_dim):
    n = DEC_SEQ
    rows = jnp.repeat(jnp.arange(n // GRID_W, dtype=F32), GRID_W)
    cols = jnp.tile(jnp.arange(GRID_W, dtype=F32), n // GRID_W)
    half = rot_dim // 2
    freqs = ROPE_THETA ** (-jnp.arange(0, half, 2, dtype=F32) / half)
    ang_r, ang_c = rows[:, None] * freqs, cols[:, None] * freqs
    cos = jnp.concatenate([jnp.cos(ang_r)] * 2 + [jnp.cos(ang_c)] * 2, axis=-1)
    sin = jnp.concatenate([-jnp.sin(ang_r), jnp.sin(ang_r), -jnp.sin(ang_c), jnp.sin(ang_c)], axis=-1)
    rep = LANES // rot_dim
    return jnp.tile(cos, (1, rep)), jnp.tile(sin, (1, rep))


def _with_ones(v):
    return jnp.concatenate([v, jnp.ones_like(v)], axis=-1)


def _rows(a, idx):
    return a.at[idx].get(mode="promise_in_bounds")


def _route(idx, counts_f):
    quarter = LANES // 4
    experts = jnp.arange(N_EXPERTS, dtype=jnp.int32)
    e2 = jnp.stack([idx[:, 0], idx[:, quarter]], axis=1)
    rank2 = jnp.stack([idx[:, 2 * quarter], idx[:, 3 * quarter]], axis=1)
    counts = counts_f[0, :N_EXPERTS].astype(jnp.int32)
    tiles = (counts + TME - 1) // TME
    tile_end = jnp.cumsum(tiles)
    tile_start = tile_end - tiles
    start = jnp.cumsum(counts) - counts
    onehot = e2[:, :, None] == experts[None, None, :]
    pos = jnp.sum(jnp.where(onehot, tile_start[None, None, :] * TME, 0), axis=-1) + rank2
    tile_id = jnp.arange(NT_E, dtype=jnp.int32)
    tile_expert = jnp.minimum(jnp.sum((tile_id[:, None] >= tile_end[None, :]).astype(jnp.int32), axis=1),
                              N_EXPERTS - 1).astype(jnp.int32)
    tile_valid = (tile_id < tile_end[-1]).astype(jnp.int32)
    t_oh = tile_expert[:, None] == experts[None, :]
    t_first = jnp.sum(jnp.where(t_oh, (start - tile_start * TME)[None, :], 0), axis=1) + tile_id * TME
    t_last = jnp.sum(jnp.where(t_oh, (start + counts)[None, :], 0), axis=1)
    order = jnp.argsort(e2.reshape(-1), stable=True).astype(jnp.int32)
    n_assign = order.shape[0]
    g_idx = t_first[:, None] + jnp.arange(TME, dtype=jnp.int32)[None, :]
    live = jnp.logical_and(g_idx < t_last[:, None], tile_valid[:, None] == 1)
    src = _rows(order, jnp.clip(g_idx, 0, n_assign - 1).reshape(-1)) // 2
    src_tok = jnp.where(live.reshape(-1), src, 0).astype(jnp.int32)

    n_tiles = tile_end[-1]
    last_tile = jnp.maximum(n_tiles - 1, 0)
    last_expert = jnp.sum(jnp.where(tile_id == last_tile, tile_expert, 0))
    t_tstart = jnp.sum(jnp.where(t_oh, tile_start[None, :], 0), axis=1)
    down_tables = (
        jnp.where(tile_valid == 1, tile_id, last_tile).astype(jnp.int32),
        tile_expert,
        jnp.logical_and(tile_valid == 1, tile_id == t_tstart).astype(jnp.int32),
        tile_valid,
    )
    s_id = jnp.arange(NC_E * NT_E, dtype=jnp.int32)
    s_exp = jnp.minimum(jnp.sum((s_id[:, None] >= NC_E * tile_end[None, :]).astype(jnp.int32), axis=1),
                        N_EXPERTS - 1)
    s_oh = s_exp[:, None] == experts[None, :]
    s_tstart = jnp.sum(jnp.where(s_oh, tile_start[None, :], 0), axis=1)
    s_ntile = jnp.maximum(jnp.sum(jnp.where(s_oh, tiles[None, :], 0), axis=1), 1)
    rel = s_id - NC_E * s_tstart
    s_valid = s_id < NC_E * n_tiles
    up_tables = (
        jnp.where(s_valid, s_tstart + rel % s_ntile, last_tile).astype(jnp.int32),
        jnp.where(s_valid, rel // s_ntile, NC_E - 1).astype(jnp.int32),
        jnp.where(s_valid, s_exp, last_expert).astype(jnp.int32),
        jnp.logical_and(s_valid, rel % s_ntile == 0).astype(jnp.int32),
        s_valid.astype(jnp.int32),
    )
    return src_tok, pos, up_tables, down_tables


def lambda_init(layer):
    return 0.8 - 0.6 * math.exp(-0.3 * layer)


def kernel(x_prompt, x_sample, state_a, cache_b_k, cache_b_v, cache_c_k, cache_c_v, c, c_ctx, w_mod, b_mod, norm_mix, norm_ffn, w_in_even, w_gate2_a, b_gate_a, g_gla, lam_q1, lam_k1, lam_q2, lam_k2, g_sub_b, w_out_even, w_in_odd, g_q_c, g_k_c, w_out_odd, ffn_gate, ffn_up, ffn_down, w_router, exp_gate, exp_up, exp_down, norm_final):
    x_ctx, x_lat = x_prompt.reshape(N_CTX, D_MODEL), x_sample.reshape(N_LAT, D_MODEL)
    cond =jnp.concatenate([c_ctx[None, :], c, jnp.zeros((N_COND - 1 - DEC_BATCH, D_MODEL), F32)], axis=0)
    mod = _modulation(cond, w_mod, b_mod).reshape(DEPTH, N_COND, 1, 6 * D_MODEL)

    w = w_in_even[0]
    na = 2 * A_QK + 2 * A_V
    gate_lo = na
    gate_hi = na + 2 * GATE_RANK
    col_scale = jnp.concatenate([jnp.full((A_QK,), DK_A ** -0.5, F32), jnp.ones((na - A_QK,), F32),
                                 jnp.full((B_QK,), DH_B ** -0.5, F32), jnp.ones((B_QK + B_V,), F32)])
    w_main = (jnp.concatenate([w[:, :gate_lo], w[:, gate_hi:]], axis=1) * col_scale).astype(BF16)
    w_gates = jnp.pad(w[:, gate_lo:gate_hi], ((0, 0), (0, LANES - 2 * GATE_RANK))).astype(BF16)
    g2 = jnp.zeros((LANES, 2 * A_QK), F32)
    g2 = g2.at[:GATE_RANK, :A_QK].set(w_gate2_a[0, 0]).at[GATE_RANK:2 * GATE_RANK, A_QK:].set(w_gate2_a[0, 1])
    bg = b_gate_a[0].reshape(1, 2 * A_QK)
    cos_b, sin_b = _rope_tables(DH_B)
    a_proj, glog, qb, kb, vb, kb_ctx, vb_ctx = _inproj_even(
        x_ctx, x_lat, norm_mix[0][None, :], mod[0], w_main, w_gates, g2.astype(BF16), bg, cos_b, sin_b)

    s0_all = jnp.concatenate([jnp.zeros((1, 2, A_QK, DV_A), F32),
                              state_a[:, 0].reshape(DEC_BATCH, 2, A_QK, DV_A)], axis=0)
    sel = np.arange(A_QK)[:, None] // DK_A == np.arange(LANES)[None, :] // GLA_SB
    m1 = jnp.asarray(sel.astype(np.float32)).astype(BF16)
    o_f, o_b, s_fin = _gla(a_proj, glog, s0_all, m1)

    k_lat = jnp.concatenate([kb[N_CTX:].reshape(DEC_BATCH, DEC_SEQ, B_QK),
                             cache_b_k[:, 0].reshape(DEC_BATCH, PAST_LEN, B_QK).astype(BF16)], axis=1)
    v_lat = jnp.concatenate([vb[N_CTX:].reshape(DEC_BATCH, DEC_SEQ, 2 * B_V),
                             _with_ones(cache_b_v[:, 0].astype(BF16)).reshape(DEC_BATCH, PAST_LEN, 2 * B_V)], axis=1)
    lam_params = [p[0][None, :] for p in (lam_q1, lam_k1, lam_q2, lam_k2)]
    attn_b = _diffattn(qb, kb, vb, k_lat, v_lat, lam_params, g_sub_b[0][None, :], lambda_init(0))

    x = _postmix_even(o_f, o_b, a_proj, *attn_b, x_ctx, x_lat, mod[0], g_gla[0][None, :],
                      w_out_even[0].astype(BF16))
    x = _ffn_even(x, norm_ffn[0][None, :], mod[0], ffn_gate[0].astype(BF16), ffn_up[0].astype(BF16),
                  ffn_down[0].astype(BF16))

    cos_c, sin_c = _rope_tables(DH_C)
    q_c, k_c, v_c, kc_ctx, vc_ctx = _inproj_odd(x, norm_mix[1][None, :], mod[1], w_in_odd[0].astype(BF16),
                                     g_q_c[0][None, :], g_k_c[0][None, :], cos_c, sin_c)
    nkv = HKV_C * DH_C
    k_lat = jnp.concatenate([k_c[N_CTX:].reshape(DEC_BATCH, DEC_SEQ, nkv),
                             cache_c_k[:, 0].reshape(DEC_BATCH, PAST_LEN, nkv).astype(BF16)], axis=1)
    v_lat = jnp.concatenate([v_c[N_CTX:].reshape(DEC_BATCH, DEC_SEQ, 2 * nkv),
                             _with_ones(cache_c_v[:, 0].astype(BF16)).reshape(DEC_BATCH, PAST_LEN, 2 * nkv)], axis=1)
    attn_c = _gqa(q_c, k_c, v_c, k_lat, v_lat)
    w_r = jnp.pad(w_router[0], ((0, 0), (0, LANES - N_EXPERTS)))
    x1, h_moe, w1, w2, idx, counts = _postmix_odd(*attn_c, x, mod[1], w_out_odd[0].astype(BF16),
                                                  norm_ffn[1][None, :], w_r)

    src_tok, pos, up_tables, down_tables = _route(idx, counts)
    xs = _rows(h_moe, src_tok)
    ys = _experts(up_tables, down_tables, xs, exp_gate[0], exp_up[0], exp_down[0])
    y1 = _rows(ys, pos[:, 0])
    y2 = _rows(ys, pos[:, 1])
    y_ctx, y_lat = _combine(x1, y1, y2, w1, w2, mod[1], norm_final[None, :])

    y_prompt = y_ctx.reshape(BATCH, SEQ, D_MODEL)
    y_sample = y_lat.reshape(DEC_BATCH, DEC_SEQ, D_MODEL)
    new_state_a = s_fin[:BATCH].reshape(BATCH, 1, 2, H_A, DK_A, DV_A)
    new_b_k = kb_ctx.reshape(BATCH, 1, SEQ, H_B, 2, DH_B)
    new_b_v = vb_ctx.reshape(BATCH, 1, SEQ, H_B, DV_B)
    new_c_k = kc_ctx.reshape(BATCH, 1, SEQ, HKV_C, DH_C)
    new_c_v = vc_ctx.reshape(BATCH, 1, SEQ, HKV_C, DH_C)
    return (y_prompt, y_sample, new_state_a, new_b_k, new_b_v, new_c_k, new_c_v)
```

```python
import functools
import math

import numpy as np
import jax
import jax.numpy as jnp
from jax import lax
from jax.experimental import pallas as pl
from jax.experimental.pallas import tpu as pltpu

D_MODEL = 1024
BATCH = 32
SEQ = 256
DEPTH = 2
DEC_BATCH = 2
DEC_SEQ = 2048
PAST_LEN = 512
GRID_W = 64
H_A, DK_A, DV_A = 4, 64, 128
GATE_RANK = 16
GATE_TAU = 16.0
H_B, DH_B = 4, 64
DV_B = 2 * DH_B
H_C, HKV_C, DH_C = 8, 2, 128
D_FF = 2816
N_EXPERTS = 8
D_FF_E = 3584
ROPE_THETA = 10000.0
EPS = 1e-6
A_QK = H_A * DK_A
A_V = H_A * DV_A
B_QK = H_B * 2 * DH_B
B_V = H_B * DV_B

F32 = jnp.float32
BF16 = jnp.bfloat16
LOG2E = math.log2(math.e)

V7X_VMEM_BYTES = 64 * 1024 * 1024
VMEM_LIMIT = V7X_VMEM_BYTES * 7 // 8
LANES = 128
MXU_N = 256
TM = 256
N_CTX = BATCH * SEQ
N_LAT = DEC_BATCH * DEC_SEQ
M_TOK = N_CTX + N_LAT
NB = M_TOK // TM
NCB = N_CTX // TM
LBB = DEC_SEQ // TM
N_COND = 8
GLA_C = 64
GLA_SB = 8
TME = 512
NT_E = 2 * M_TOK // TME + N_EXPERTS
NC_E = 2
TF_E = D_FF_E // NC_E
TQ_LAT = 256

assert SEQ == TM and DEC_SEQ % TM == 0 and DEPTH == 2


def _cparams(sem):
    return pltpu.CompilerParams(dimension_semantics=sem, vmem_limit_bytes=VMEM_LIMIT)


def _group(i):
    return jnp.where(i < NCB, 0, 1 + (i - NCB) // LBB)


def _pos_block(i):
    return jnp.where(i < NCB, 0, (i - NCB) % LBB)


def _mod_spec(chunk):
    return pl.BlockSpec((None, 1, D_MODEL), lambda i, c=chunk: (_group(i), 0, c))


def _row_spec(width, col=0):
    return pl.BlockSpec((TM, width), lambda i, c=col: (i, c))


def _ctx_spec(width):
    return pl.BlockSpec((TM, width), lambda i: (jnp.minimum(i, NCB - 1), 0))


def _ctx_parts_spec(parts):
    return pl.BlockSpec((TM * parts, LANES), lambda i: (jnp.minimum(i, NCB - 1), 0))


def _lat_spec(width):
    return pl.BlockSpec((TM, width), lambda i: (jnp.maximum(i - NCB, 0), 0))


def _const_spec(shape):
    nd = len(shape)
    return pl.BlockSpec(shape, lambda i, nd=nd: (0,) * nd)


def _dot(a, b):
    return jnp.dot(a, b, preferred_element_type=F32)


def _dot_nt(a, b):
    return lax.dot_general(a, b, (((1,), (1,)), ((), ())), preferred_element_type=F32)


def _split2(a):
    hi = a.astype(BF16)
    lo = (a - hi.astype(F32)).astype(BF16)
    return hi, lo


def _dot_hi(a, b):
    a_hi, a_lo = _split2(a)
    b_hi, b_lo = _split2(b)
    return _dot(a_hi, b_hi) + (_dot(a_hi, b_lo) + _dot(a_lo, b_hi))


def _silu(x):
    return (0.5 * x) * (1.0 + jnp.tanh(0.5 * x))


def _log_sigmoid(x):
    return jnp.minimum(x, 0.0) - jnp.log(1.0 + jnp.exp(-jnp.abs(x)))


def _rms(x, g):
    return x * lax.rsqrt(jnp.mean(x * x, axis=-1, keepdims=True) + EPS) * g


def _norm_mod(x, gamma, sc, sh):
    return _rms(x, gamma) * (1.0 + sc) + sh


def _copy_ctx_transposed(stage3_ref, dst_ref):
    @pl.when(pl.program_id(0) < NCB)
    def _():
        def body(j, carry):
            dst_ref[pl.ds(pl.multiple_of(j * LANES, LANES), LANES), :] = jnp.transpose(stage3_ref[j])
            return carry

        lax.fori_loop(0, stage3_ref.shape[0], body, 0)


def _pick_rows(ctx_ref, lat_ref):
    return jnp.where(pl.program_id(0) >= NCB, lat_ref[...], ctx_ref[...])


def _copy_ctx(stage_ref, outs):
    piece = 64

    @pl.when(pl.program_id(0) < NCB)
    def _():
        def body(r, carry):
            rows = pl.ds(pl.multiple_of(r * piece, piece), piece)
            for lo, width, parts, dst_ref in outs:
                for p in range(parts):
                    val = stage_ref[rows, lo + p * width:lo + (p + 1) * width]
                    if width < LANES:
                        val = jnp.concatenate([val, jnp.zeros((piece, LANES - width), F32)], axis=1)
                    dst_ref[pl.ds(pl.multiple_of(r * piece * parts, piece * parts) + p, piece, stride=parts), :] = val
            return carry

        lax.fori_loop(0, stage_ref.shape[0] // piece, body, 0)


def _rope(x, cos, sin, half):
    lane = lax.broadcasted_iota(jnp.int32, x.shape, 1)
    first = (lane % (2 * half)) < half
    swapped = jnp.where(first, pltpu.roll(x, LANES - half, 1), pltpu.roll(x, half, 1))
    return x * cos + swapped * sin


def _mod_kernel(cond_ref, w_ref, b_ref, o_ref):
    o_ref[...] = _dot_hi(_silu(cond_ref[...]), w_ref[...]) + b_ref[...]


def _modulation(cond, w_mod, b_mod):
    tn = 1536
    return pl.pallas_call(
        _mod_kernel,
        grid=(DEPTH, 6 * D_MODEL // tn),
        in_specs=[
            pl.BlockSpec((N_COND, D_MODEL), lambda l, j: (0, 0)),
            pl.BlockSpec((None, D_MODEL, tn), lambda l, j: (l, 0, j)),
            pl.BlockSpec((None, 1, tn), lambda l, j: (l, 0, j)),
        ],
        out_specs=pl.BlockSpec((None, N_COND, tn), lambda l, j: (l, 0, j)),
        out_shape=jax.ShapeDtypeStruct((DEPTH, N_COND, 6 * D_MODEL), F32),
        compiler_params=_cparams(("arbitrary", "arbitrary")),
        name="modulation",
    )(cond, w_mod, b_mod.reshape(DEPTH, 1, 6 * D_MODEL))


def _inproj_even_kernel(xc_ref, xl_ref, gam_ref, sh_ref, sc_ref, w_ref, wg_ref, g2_ref, bg_ref, cos_ref, sin_ref,
                        a_ref, glog_ref, qb_ref, kb_ref, vb_ref, kctx_ref, vctx_ref, stage_ref, stagek_ref):
    is_lat = pl.program_id(0) >= NCB
    hb = _norm_mod(_pick_rows(xc_ref, xl_ref), gam_ref[...], sc_ref[...], sh_ref[...]).astype(BF16)
    cos = jnp.where(is_lat, cos_ref[...], 1.0)
    sin = jnp.where(is_lat, sin_ref[...], 0.0)
    na = 2 * A_QK + 2 * A_V
    for lo in range(0, na, MXU_N):
        a_ref[:, lo:lo + MXU_N] = _dot(hb, w_ref[:, lo:lo + MXU_N])
    for lo in range(0, B_QK, MXU_N):
        zq = _dot(hb, w_ref[:, na + lo:na + lo + MXU_N])
        zk = _dot(hb, w_ref[:, na + B_QK + lo:na + B_QK + lo + MXU_N])
        zv = _dot(hb, w_ref[:, na + 2 * B_QK + lo:na + 2 * B_QK + lo + MXU_N])
        stage_ref[:, lo:lo + MXU_N] = zv
        for j in range(0, MXU_N, LANES):
            cs = slice(lo + j, lo + j + LANES)
            vb_ref[:, 2 * (lo + j):2 * (lo + j) + DV_B] = zv[:, j:j + LANES].astype(BF16)
            vb_ref[:, 2 * (lo + j) + DV_B:2 * (lo + j + LANES)] = jnp.ones((TM, DV_B), BF16)
            stagek_ref[(lo + j) // LANES] = zk[:, j:j + LANES]
            qb_ref[:, cs] = (_rope(zq[:, j:j + LANES], cos, sin, DH_B // 4) * LOG2E).astype(BF16)
            kb_ref[:, cs] = _rope(zk[:, j:j + LANES], cos, sin, DH_B // 4).astype(BF16)
    gates = _dot(hb, wg_ref[...])
    xg = _dot(gates.astype(BF16), g2_ref[...]) + bg_ref[...]
    glog_ref[...] = _log_sigmoid(xg) * (1.0 / GATE_TAU)
    _copy_ctx(stage_ref, ((0, DV_B, H_B, vctx_ref),))
    _copy_ctx_transposed(stagek_ref, kctx_ref)


def _inproj_even(x_ctx, x_lat, gamma, mod, w_main, w_gates, g2, bg, cos, sin):
    na = 2 * A_QK + 2 * A_V
    nz = na + 2 * B_QK + B_V
    return pl.pallas_call(
        _inproj_even_kernel,
        grid=(NB,),
        in_specs=[
            _ctx_spec(D_MODEL), _lat_spec(D_MODEL), _const_spec((1, D_MODEL)), _mod_spec(0), _mod_spec(1),
            _const_spec((D_MODEL, nz)), _const_spec((D_MODEL, LANES)), _const_spec((LANES, 2 * A_QK)),
            _const_spec((1, 2 * A_QK)),
            pl.BlockSpec((TM, LANES), lambda i: (_pos_block(i), 0)),
            pl.BlockSpec((TM, LANES), lambda i: (_pos_block(i), 0)),
        ],
        out_specs=[_pair_spec(na), _pair_spec(2 * A_QK), _row_spec(B_QK), _row_spec(B_QK), _row_spec(2 * B_V),
                   pl.BlockSpec((None, B_QK, TM), lambda i: (jnp.minimum(i, NCB - 1), 0, 0)), _ctx_parts_spec(H_B)],
        out_shape=[
            jax.ShapeDtypeStruct((M_TOK, na), F32), jax.ShapeDtypeStruct((M_TOK, 2 * A_QK), F32),
            jax.ShapeDtypeStruct((M_TOK, B_QK), BF16), jax.ShapeDtypeStruct((M_TOK, B_QK), BF16),
            jax.ShapeDtypeStruct((M_TOK, 2 * B_V), BF16), jax.ShapeDtypeStruct((BATCH, B_QK, SEQ), F32),
            jax.ShapeDtypeStruct((N_CTX * H_B, LANES), F32),
        ],
        scratch_shapes=[pltpu.VMEM((TM, B_V), F32), pltpu.VMEM((B_QK // LANES, TM, LANES), F32)],
        compiler_params=_cparams(("arbitrary",)),
        name="inproj_even",
    )(x_ctx, x_lat, gamma, mod, mod, w_main, w_gates, g2, bg, cos, sin)


def _gla_chunk(q_ref, k_ref, v_ref, g_ref, o_ref, c, s_ref, a_ref, b_ref, m1_ref, head_mask, same_block, rev):
    q, k, v = q_ref[c], k_ref[c], v_ref[c]
    g = g_ref[c] * LOG2E
    v_b = v.astype(BF16)
    r_i = lax.broadcasted_iota(jnp.int32, (GLA_C, GLA_C), 0)
    c_i = lax.broadcasted_iota(jnp.int32, (GLA_C, GLA_C), 1)
    tri = jnp.where((c_i >= r_i) if rev else (c_i <= r_i), 1.0, 0.0).astype(BF16)
    g_hi = g.astype(BF16)
    g_r = g - g_hi.astype(F32)
    g_mid = g_r.astype(BF16)
    g_lo = (g_r - g_mid.astype(F32)).astype(BF16)
    bc = _dot(tri, g_hi) + (_dot(tri, g_mid) + _dot(tri, g_lo))
    b_ref[...] = bc
    yield
    t_i = lax.broadcasted_iota(jnp.int32, (GLA_SB, A_QK), 0)
    group = 2 * GLA_SB
    for i0 in range(0, GLA_C, group):
        blocks = range(i0, i0 + group, GLA_SB)
        q_i = [q_ref[c, i:i + GLA_SB, :] for i in blocks]
        b_i = [b_ref[i:i + GLA_SB, :] for i in blocks]
        for j in range(GLA_SB):
            keep = (t_i <= j) if rev else (t_i >= j)
            pair = []
            for n, i in enumerate(blocks):
                k_s = k_ref[c, i + j:i + j + 1, :]
                b_s = b_ref[i + j:i + j + 1, :]
                pair.append(jnp.where(keep, q_i[n] * k_s * jnp.exp2(b_i[n] - b_s), 0.0))
            a_ref[j * GLA_C + i0:j * GLA_C + i0 + group, :] = jnp.concatenate(pair, axis=0).astype(BF16)
    yield
    last = bc[0:1] if rev else bc[GLA_C - 1:GLA_C]
    qc = q * jnp.exp2(bc)
    kl = k * jnp.exp2(last - bc)
    row = lax.broadcasted_iota(jnp.int32, (GLA_C, A_QK), 0)

    def expand(x):
        return jnp.where(head_mask, jnp.concatenate([x] * H_A, axis=0), 0.0).astype(BF16)

    scores = None
    half = GLA_C // 2
    while half >= GLA_SB:
        mids = [bc[b + half:b + half + 1] if rev else bc[b + half - 1:b + half] for b in range(0, GLA_C, 2 * half)]
        ref = mids[0] if len(mids) == 1 else jnp.concatenate(
            [jnp.broadcast_to(m, (2 * half, A_QK)) for m in mids], axis=0)
        upper = (row // half) % 2 == 1
        later, earlier = (jnp.logical_not(upper), upper) if rev else (upper, jnp.logical_not(upper))
        q_l = jnp.where(later, q * jnp.exp2(bc - ref), 0.0)
        k_e = jnp.where(earlier, k * jnp.exp2(ref - bc), 0.0).astype(BF16)
        term = _dot_nt(expand(q_l), k_e)
        if 2 * half < GLA_C:
            term = jnp.where(same_block[2 * half], term, 0.0)
        scores = term if scores is None else scores + term
        half //= 2
    s_old = s_ref[...]
    inter = _dot(expand(qc), s_old.astype(BF16))
    sums = _dot(a_ref[...], m1_ref[...])
    yield
    lane_c = lax.broadcasted_iota(jnp.int32, (GLA_C, LANES), 1)
    within = None
    for j in range(GLA_SB):
        part = jnp.where(lane_c % GLA_SB == j, sums[j * GLA_C:(j + 1) * GLA_C], 0.0)
        within = part if within is None else within + part
    lane = lax.broadcasted_iota(jnp.int32, (GLA_SB, LANES), 1)
    per_head = []
    for h in range(H_A):
        rows_h = []
        for i in range(0, GLA_C, GLA_SB):
            moved = pltpu.roll(within[i:i + GLA_SB], (i - h * GLA_SB) % LANES, 1)
            rows_h.append(jnp.where(lane // GLA_SB == i // GLA_SB, moved, 0.0))
        per_head.append(jnp.concatenate(rows_h, axis=0)[:, :GLA_C])
    scores = scores + jnp.concatenate(per_head, axis=0)
    o_heads = []
    for h in range(H_A):
        hs = slice(h * GLA_C, (h + 1) * GLA_C)
        o_heads.append(inter[hs] + _dot(scores[hs].astype(BF16), v_b[:, h * DV_A:(h + 1) * DV_A]))
    o_ref[c] = jnp.concatenate(o_heads, axis=1)
    yield
    t = jnp.transpose(jnp.concatenate([kl, jnp.broadcast_to(last, (GLA_C, A_QK))], axis=0))
    kv = _dot(t[:, :GLA_C].astype(BF16), v_b)
    a_col = jnp.exp2(t[:, GLA_C:GLA_C + 1])
    for h in range(H_A):
        hs = slice(h * DK_A, (h + 1) * DK_A)
        s_ref[hs, :] = a_col[hs] * s_old[hs] + kv[hs, h * DV_A:(h + 1) * DV_A]


def _gla_kernel(fblk_ref, bblk_ref, first_ref, inita_ref, initb_ref,
                qf_ref, kf_ref, vf_ref, gf_ref, qr_ref, kr_ref, vr_ref, gr_ref, s0a_ref, s0b_ref, m1_ref,
                of_ref, ob_ref, sfin_ref, s_refs, a_refs, b_refs):
    step = pl.program_id(0)

    @pl.when(first_ref[step] == 1)
    def _():
        for seq, s0_ref in enumerate((s0a_ref, s0b_ref)):
            for d in range(2):
                s_refs[2 * seq + d] = s0_ref[d]

    r_h = lax.broadcasted_iota(jnp.int32, (H_A * GLA_C, A_QK), 0) // GLA_C
    c_h = lax.broadcasted_iota(jnp.int32, (H_A * GLA_C, A_QK), 1) // DK_A
    head_mask = r_h == c_h
    t_s = lax.broadcasted_iota(jnp.int32, (H_A * GLA_C, GLA_C), 0) % GLA_C
    s_s = lax.broadcasted_iota(jnp.int32, (H_A * GLA_C, GLA_C), 1)
    same_block = {}
    size = 2 * GLA_SB
    while size < GLA_C:
        same_block[size] = (t_s // size) == (s_s // size)
        size *= 2
    nch = TM // GLA_C

    def body(c, carry):
        chains = []
        for seq in range(2):
            chains.append(_gla_chunk(qf_ref, kf_ref, vf_ref, gf_ref, of_ref, seq * nch + c, s_refs.at[2 * seq],
                                     a_refs.at[2 * seq], b_refs.at[2 * seq], m1_ref, head_mask, same_block, False))
            chains.append(_gla_chunk(qr_ref, kr_ref, vr_ref, gr_ref, ob_ref, seq * nch + nch - 1 - c,
                                     s_refs.at[2 * seq + 1], a_refs.at[2 * seq + 1], b_refs.at[2 * seq + 1], m1_ref,
                                     head_mask, same_block, True))
        live = chains
        while live:
            live = [ch for ch in live if next(ch, True) is None]
        return carry

    lax.fori_loop(0, nch, body, 0)
    for seq in range(2):
        for d in range(2):
            sfin_ref[seq, d] = s_refs[2 * seq + d]


def _pair_block(i):
    j = i - NCB
    return jnp.where(i < NCB, i, NCB + 2 * (j % LBB) + j // LBB)


def _pair_spec(width, col=0):
    return pl.BlockSpec((TM, width), lambda i, c=col: (_pair_block(i), c))


def _gla_tables():
    fblk, bblk, first, init_a, init_b = [], [], [], [], []
    for p in range(BATCH // 2):
        fblk.append(p), bblk.append(p), first.append(1), init_a.append(0), init_b.append(0)
    for j in range(LBB):
        fblk.append(NCB // 2 + j)
        bblk.append(NCB // 2 + LBB - 1 - j)
        first.append(1 if j == 0 else 0)
        init_a.append(1), init_b.append(2)
    return [jnp.asarray(np.array(t, np.int32)) for t in (fblk, bblk, first, init_a, init_b)]


def _gla(a_proj, glog, s0_all, m1):
    tables = _gla_tables()
    nsteps = int(tables[0].shape[0])
    nseq = BATCH + DEC_BATCH

    def fmap(col):
        return lambda i, fb, bb, fi, ia, ib: (fb[i], 0, col)

    def rmap(col):
        return lambda i, fb, bb, fi, ia, ib: (bb[i], 0, col)

    nch = 2 * TM // GLA_C
    a3 = a_proj.reshape(M_TOK // GLA_C, GLA_C, a_proj.shape[1])
    g3 = glog.reshape(M_TOK // GLA_C, GLA_C, glog.shape[1])
    gs = pltpu.PrefetchScalarGridSpec(
        num_scalar_prefetch=5,
        grid=(nsteps,),
        in_specs=[
            pl.BlockSpec((nch, GLA_C, A_QK), fmap(0)), pl.BlockSpec((nch, GLA_C, A_QK), fmap(1)),
            pl.BlockSpec((nch, GLA_C, A_V), fmap(1)), pl.BlockSpec((nch, GLA_C, A_QK), fmap(0)),
            pl.BlockSpec((nch, GLA_C, A_QK), rmap(0)), pl.BlockSpec((nch, GLA_C, A_QK), rmap(1)),
            pl.BlockSpec((nch, GLA_C, A_V), rmap(1)), pl.BlockSpec((nch, GLA_C, A_QK), rmap(1)),
            pl.BlockSpec((None, 2, A_QK, DV_A), lambda i, fb, bb, fi, ia, ib: (ia[i], 0, 0, 0)),
            pl.BlockSpec((None, 2, A_QK, DV_A), lambda i, fb, bb, fi, ia, ib: (ib[i], 0, 0, 0)),
            pl.BlockSpec((A_QK, LANES), lambda i, fb, bb, fi, ia, ib: (0, 0)),
        ],
        out_specs=[
            pl.BlockSpec((nch, GLA_C, A_V), fmap(0)), pl.BlockSpec((nch, GLA_C, A_V), rmap(0)),
            pl.BlockSpec((2, 2, A_QK, DV_A), lambda i, fb, bb, fi, ia, ib: (jnp.minimum(fb[i], NCB // 2), 0, 0, 0)),
        ],
        scratch_shapes=[
            pltpu.VMEM((4, A_QK, DV_A), F32),
            pltpu.VMEM((4, GLA_SB * GLA_C, A_QK), BF16),
            pltpu.VMEM((4, GLA_C, A_QK), F32),
        ],
    )
    o_f, o_b, s_fin = pl.pallas_call(
        _gla_kernel,
        grid_spec=gs,
        out_shape=[
            jax.ShapeDtypeStruct((M_TOK // GLA_C, GLA_C, A_V), F32),
            jax.ShapeDtypeStruct((M_TOK // GLA_C, GLA_C, A_V), F32),
            jax.ShapeDtypeStruct((nseq, 2, A_QK, DV_A), F32),
        ],
        compiler_params=_cparams(("arbitrary",)),
        name="gla",
    )(*tables, a3, a3, a3, g3, a3, a3, a3, g3, s0_all, s0_all, m1)
    return o_f.reshape(M_TOK, A_V), o_b.reshape(M_TOK, A_V), s_fin


def _attend(q_b, k_b, v_ones):
    s2 = _dot_nt(q_b, k_b)
    e = jnp.exp2(s2 - jnp.max(s2, axis=-1, keepdims=True)).astype(BF16)
    both = _dot(e, v_ones)
    dv = v_ones.shape[1] // 2
    return both[:, :dv] * (1.0 / both[:, dv:])


def _diffattn_body(q_ref, k_ref, v_ref, lq1_ref, lk1_ref, lq2_ref, lk2_ref, gsub_ref, o_ref, lam_init):
    lam = (jnp.exp(jnp.sum(lq1_ref[...] * lk1_ref[...], axis=-1, keepdims=True))
           - jnp.exp(jnp.sum(lq2_ref[...] * lk2_ref[...], axis=-1, keepdims=True)) + lam_init)
    lane = lax.broadcasted_iota(jnp.int32, (q_ref.shape[0], LANES), 1)
    for h in range(H_B):
        hs = slice(h * LANES, (h + 1) * LANES)
        q_h = q_ref[:, hs]
        k_h = k_ref[:, hs]
        v_h = v_ref[:, 2 * h * DV_B:2 * (h + 1) * DV_B]
        zero = jnp.zeros_like(q_h)
        o = (_attend(jnp.where(lane < DH_B, q_h, zero), k_h, v_h)
             - lam * _attend(jnp.where(lane >= DH_B, q_h, zero), k_h, v_h))
        o_ref[:, hs] = (_rms(o, gsub_ref[...]) * (1.0 - lam_init)).astype(BF16)


def _diffattn_kernel(q_ref, k_ref, v_ref, lq1, lk1, lq2, lk2, gsub, o_ref, *, lam_init):
    _diffattn_body(q_ref, k_ref, v_ref, lq1, lk1, lq2, lk2, gsub, o_ref, lam_init)


def _diffattn(qb, kb, vb, k_lat, v_lat, lam_params, gsub, lam_init):
    small = [pl.BlockSpec((1, DH_B), lambda *_: (0, 0))] * 4 + [pl.BlockSpec((1, DV_B), lambda *_: (0, 0))]
    body = functools.partial(_diffattn_kernel, lam_init=lam_init)
    ctx = pl.pallas_call(
        body,
        grid=(NCB,),
        in_specs=[_row_spec(B_QK), _row_spec(B_QK), _row_spec(2 * B_V)] + small,
        out_specs=_row_spec(B_V),
        out_shape=jax.ShapeDtypeStruct((N_CTX, B_V), BF16),
        compiler_params=_cparams(("arbitrary",)),
        name="diffattn_ctx",
    )(qb, kb, vb, *lam_params, gsub)
    nk = DEC_SEQ + PAST_LEN
    nqb = DEC_SEQ // TQ_LAT
    off = N_CTX // TQ_LAT
    lat = pl.pallas_call(
        body,
        grid=(DEC_BATCH, nqb),
        in_specs=[
            pl.BlockSpec((TQ_LAT, B_QK), lambda b, j: (off + b * nqb + j, 0)),
            pl.BlockSpec((None, nk, B_QK), lambda b, j: (b, 0, 0)),
            pl.BlockSpec((None, nk, 2 * B_V), lambda b, j: (b, 0, 0)),
        ] + small,
        out_specs=pl.BlockSpec((TQ_LAT, B_V), lambda b, j: (b * nqb + j, 0)),
        out_shape=jax.ShapeDtypeStruct((N_LAT, B_V), BF16),
        compiler_params=_cparams(("arbitrary", "arbitrary")),
        name="diffattn_lat",
    )(qb, k_lat, v_lat, *lam_params, gsub)
    return ctx, lat


def _postmix_even_kernel(of_ref, ob_ref, ra_ref, actx_ref, alat_ref, xc_ref, xl_ref, g1_ref, ggla_ref, wo_ref,
                         o_ref):
    heads = []
    for h in range(H_A):
        hs = slice(h * DV_A, (h + 1) * DV_A)
        heads.append((_rms(of_ref[:, hs] + ob_ref[:, hs], ggla_ref[...]) * _silu(ra_ref[:, hs])).astype(BF16))
    mix = jnp.concatenate(heads + [_pick_rows(actx_ref, alat_ref)], axis=1)
    o_ref[...] = _pick_rows(xc_ref, xl_ref) + g1_ref[...] * _dot(mix, wo_ref[...])


def _postmix_even(o_f, o_b, a_proj, attn_ctx, attn_lat, x_ctx, x_lat, mod, g_gla, w_out):
    return pl.pallas_call(
        _postmix_even_kernel,
        grid=(NB,),
        in_specs=[_pair_spec(A_V), _pair_spec(A_V), _pair_spec(A_V, 2), _ctx_spec(B_V), _lat_spec(B_V),
                  _ctx_spec(D_MODEL), _lat_spec(D_MODEL), _mod_spec(2), _const_spec((1, DV_A)),
                  _const_spec((A_V + B_V, D_MODEL))],
        out_specs=_row_spec(D_MODEL),
        out_shape=jax.ShapeDtypeStruct((M_TOK, D_MODEL), F32),
        compiler_params=_cparams(("arbitrary",)),
        name="postmix_even",
    )(o_f, o_b, a_proj, attn_ctx, attn_lat, x_ctx, x_lat, mod, g_gla, w_out)


def _swiglu_act(xb, wg_ref, wu_ref, act_ref):
    for lo in range(0, act_ref.shape[1], MXU_N):
        cs = slice(lo, lo + MXU_N)
        act_ref[:, cs] = (_silu(_dot(xb, wg_ref[:, cs])) * _dot(xb, wu_ref[:, cs])).astype(BF16)


def _ffn_even_kernel(x_ref, gam_ref, sh_ref, sc_ref, g2_ref, wg_ref, wu_ref, wd_ref, o_ref, act_ref):
    x = x_ref[...]
    hb = _norm_mod(x, gam_ref[...], sc_ref[...], sh_ref[...]).astype(BF16)
    _swiglu_act(hb, wg_ref, wu_ref, act_ref)
    o_ref[...] = x + g2_ref[...] * _dot(act_ref[...], wd_ref[...])


def _ffn_even(x, gamma, mod, w_gate, w_up, w_down):
    return pl.pallas_call(
        _ffn_even_kernel,
        grid=(NB,),
        in_specs=[_row_spec(D_MODEL), _const_spec((1, D_MODEL)), _mod_spec(3), _mod_spec(4), _mod_spec(5),
                  _const_spec((D_MODEL, D_FF)), _const_spec((D_MODEL, D_FF)), _const_spec((D_FF, D_MODEL))],
        out_specs=_row_spec(D_MODEL),
        out_shape=jax.ShapeDtypeStruct((M_TOK, D_MODEL), F32),
        scratch_shapes=[pltpu.VMEM((TM, D_FF), BF16)],
        compiler_params=_cparams(("arbitrary",)),
        name="ffn_even",
    )(x, gamma, mod, mod, mod, w_gate, w_up, w_down)


def _inproj_odd_kernel(x_ref, gam_ref, sh_ref, sc_ref, w_ref, gq_ref, gk_ref, cos_ref, sin_ref,
                       q_ref, k_ref, v_ref, kctx_ref, vctx_ref, stage_ref):
    is_lat = pl.program_id(0) >= NCB
    hb = _norm_mod(x_ref[...], gam_ref[...], sc_ref[...], sh_ref[...]).astype(BF16)
    cos = jnp.where(is_lat, cos_ref[...], 1.0)
    sin = jnp.where(is_lat, sin_ref[...], 0.0)
    scale = DH_C ** -0.5 * LOG2E
    nq, nkv = H_C * DH_C, HKV_C * DH_C
    for lo in range(0, nq, MXU_N):
        z = _dot(hb, w_ref[:, lo:lo + MXU_N])
        for j in range(0, MXU_N, DH_C):
            q_ref[:, lo + j:lo + j + DH_C] = (
                _rope(_rms(z[:, j:j + DH_C], gq_ref[...]), cos, sin, DH_C // 4) * scale).astype(BF16)
    z = _dot(hb, w_ref[:, nq:nq + nkv])
    for h in range(HKV_C):
        hs = slice(h * DH_C, (h + 1) * DH_C)
        k_n = _rms(z[:, hs], gk_ref[...])
        stage_ref[:, hs] = k_n
        k_ref[:, hs] = _rope(k_n, cos, sin, DH_C // 4).astype(BF16)
    v = _dot(hb, w_ref[:, nq + nkv:])
    stage_ref[:, nkv:] = v
    for h in range(HKV_C):
        v_ref[:, 2 * h * DH_C:(2 * h + 1) * DH_C] = v[:, h * DH_C:(h + 1) * DH_C].astype(BF16)
        v_ref[:, (2 * h + 1) * DH_C:2 * (h + 1) * DH_C] = jnp.ones((TM, DH_C), BF16)
    _copy_ctx(stage_ref, ((0, DH_C, HKV_C, kctx_ref), (nkv, DH_C, HKV_C, vctx_ref)))


def _inproj_odd(x, gamma, mod, w_in, g_q, g_k, cos, sin):
    nq, nkv = H_C * DH_C, HKV_C * DH_C
    return pl.pallas_call(
        _inproj_odd_kernel,
        grid=(NB,),
        in_specs=[
            _row_spec(D_MODEL), _const_spec((1, D_MODEL)), _mod_spec(0), _mod_spec(1),
            _const_spec((D_MODEL, nq + 2 * nkv)), _const_spec((1, DH_C)), _const_spec((1, DH_C)),
            pl.BlockSpec((TM, LANES), lambda i: (_pos_block(i), 0)),
            pl.BlockSpec((TM, LANES), lambda i: (_pos_block(i), 0)),
        ],
        out_specs=[_row_spec(nq), _row_spec(nkv), _row_spec(2 * nkv), _ctx_parts_spec(HKV_C),
                   _ctx_parts_spec(HKV_C)],
        out_shape=[
            jax.ShapeDtypeStruct((M_TOK, nq), BF16), jax.ShapeDtypeStruct((M_TOK, nkv), BF16),
            jax.ShapeDtypeStruct((M_TOK, 2 * nkv), BF16), jax.ShapeDtypeStruct((N_CTX * HKV_C, LANES), F32),
            jax.ShapeDtypeStruct((N_CTX * HKV_C, LANES), F32),
        ],
        scratch_shapes=[pltpu.VMEM((TM, 2 * nkv), F32)],
        compiler_params=_cparams(("arbitrary",)),
        name="inproj_odd",
    )(x, gamma, mod, mod, w_in, g_q, g_k, cos, sin)


def _gqa_body(q_ref, k_ref, v_ref, o_ref):
    rep = H_C // HKV_C
    tq = q_ref.shape[0]
    for hk in range(HKV_C):
        ks = slice(hk * DH_C, (hk + 1) * DH_C)
        q_g = jnp.concatenate([q_ref[:, (hk * rep + g) * DH_C:(hk * rep + g + 1) * DH_C] for g in range(rep)],
                              axis=0)
        o = _attend(q_g, k_ref[:, ks], v_ref[:, 2 * hk * DH_C:2 * (hk + 1) * DH_C])
        for g in range(rep):
            o_ref[:, (hk * rep + g) * DH_C:(hk * rep + g + 1) * DH_C] = o[g * tq:(g + 1) * tq].astype(BF16)


def _gqa(q, k, v, k_lat, v_lat):
    nq, nkv = H_C * DH_C, HKV_C * DH_C
    ctx = pl.pallas_call(
        _gqa_body,
        grid=(NCB,),
        in_specs=[_row_spec(nq), _row_spec(nkv), _row_spec(2 * nkv)],
        out_specs=_row_spec(nq),
        out_shape=jax.ShapeDtypeStruct((N_CTX, nq), BF16),
        compiler_params=_cparams(("arbitrary",)),
        name="gqa_ctx",
    )(q, k, v)
    nk = DEC_SEQ + PAST_LEN
    nqb = DEC_SEQ // TQ_LAT
    off = N_CTX // TQ_LAT
    lat = pl.pallas_call(
        _gqa_body,
        grid=(DEC_BATCH, nqb),
        in_specs=[
            pl.BlockSpec((TQ_LAT, nq), lambda b, j: (off + b * nqb + j, 0)),
            pl.BlockSpec((None, nk, nkv), lambda b, j: (b, 0, 0)),
            pl.BlockSpec((None, nk, 2 * nkv), lambda b, j: (b, 0, 0)),
        ],
        out_specs=pl.BlockSpec((TQ_LAT, nq), lambda b, j: (b * nqb + j, 0)),
        out_shape=jax.ShapeDtypeStruct((N_LAT, nq), BF16),
        compiler_params=_cparams(("arbitrary", "arbitrary")),
        name="gqa_lat",
    )(q, k_lat, v_lat)
    return ctx, lat


def _postmix_odd_kernel(actx_ref, alat_ref, x_ref, g1_ref, wo_ref, gam_ref, sh_ref, sc_ref, wr_ref,
                        x1_ref, h_ref, w1_ref, w2_ref, idx_ref, cnt_ref, run_ref):
    @pl.when(pl.program_id(0) == 0)
    def _():
        run_ref[...] = jnp.zeros_like(run_ref)

    x1 = x_ref[...] + g1_ref[...] * _dot(_pick_rows(actx_ref, alat_ref), wo_ref[...])
    x1_ref[...] = x1
    h = _norm_mod(x1, gam_ref[...], sc_ref[...], sh_ref[...])
    h_ref[...] = h
    logits = _dot_hi(h, wr_ref[...])
    lane = lax.broadcasted_iota(jnp.int32, logits.shape, 1)
    lane_f = lane.astype(F32)
    lg = jnp.where(lane < N_EXPERTS, logits, -jnp.inf)
    m1 = jnp.max(lg, axis=-1, keepdims=True)
    i1 = jnp.min(jnp.where(lg == m1, lane_f, float(LANES)), axis=-1, keepdims=True)
    lg2 = jnp.where(lane_f == i1, -jnp.inf, lg)
    m2 = jnp.max(lg2, axis=-1, keepdims=True)
    i2 = jnp.min(jnp.where(lg2 == m2, lane_f, float(LANES)), axis=-1, keepdims=True)
    e = jnp.exp(m2 - m1)
    w1 = 1.0 / (1.0 + e)
    w1_ref[...] = jnp.broadcast_to(w1, logits.shape)
    w2_ref[...] = jnp.broadcast_to(e * w1, logits.shape)
    oh1 = jnp.where(lane_f == i1, 1.0, 0.0)
    oh2 = jnp.where(lane_f == i2, 1.0, 0.0)
    both = oh1 + oh2
    r_i = lax.broadcasted_iota(jnp.int32, (TM, TM), 0)
    c_i = lax.broadcasted_iota(jnp.int32, (TM, TM), 1)
    before = jnp.where(c_i < r_i, 1.0, 0.0).astype(BF16)
    pref = _dot(before, both.astype(BF16)) + run_ref[0:1, :]
    rank1 = jnp.sum(pref * oh1, axis=-1, keepdims=True)
    rank2 = jnp.sum(pref * oh2, axis=-1, keepdims=True)
    run_ref[...] = run_ref[...] + jnp.sum(both, axis=0, keepdims=True)
    cnt_ref[...] = run_ref[...]
    quarter = LANES // 4
    idx_ref[...] = jnp.where(lane < quarter, i1, jnp.where(lane < 2 * quarter, i2, jnp.where(
        lane < 3 * quarter, rank1, rank2))).astype(jnp.int32)


def _postmix_odd(attn_ctx, attn_lat, x, mod, w_out, gamma, w_router):
    return pl.pallas_call(
        _postmix_odd_kernel,
        grid=(NB,),
        in_specs=[_ctx_spec(H_C * DH_C), _lat_spec(H_C * DH_C), _row_spec(D_MODEL), _mod_spec(2),
                  _const_spec((H_C * DH_C, D_MODEL)),
                  _const_spec((1, D_MODEL)), _mod_spec(3), _mod_spec(4), _const_spec((D_MODEL, LANES))],
        out_specs=[_row_spec(D_MODEL), _row_spec(D_MODEL), _row_spec(LANES), _row_spec(LANES), _row_spec(LANES),
                   _const_spec((8, LANES))],
        out_shape=[
            jax.ShapeDtypeStruct((M_TOK, D_MODEL), F32), jax.ShapeDtypeStruct((M_TOK, D_MODEL), F32),
            jax.ShapeDtypeStruct((M_TOK, LANES), F32), jax.ShapeDtypeStruct((M_TOK, LANES), F32),
            jax.ShapeDtypeStruct((M_TOK, LANES), jnp.int32), jax.ShapeDtypeStruct((8, LANES), F32),
        ],
        scratch_shapes=[pltpu.VMEM((8, LANES), F32)],
        compiler_params=_cparams(("arbitrary",)),
        name="postmix_odd",
    )(attn_ctx, attn_lat, x, mod, w_out, gamma, mod, mod, w_router)


def _cast_rows(src_ref, dst_ref, piece):
    def body(r, carry):
        rows = pl.ds(pl.multiple_of(r * piece, piece), piece)
        dst_ref[rows, :] = src_ref[rows, :].astype(BF16)
        return carry

    lax.fori_loop(0, src_ref.shape[0] // piece, body, 0)


def _experts_up_kernel(tile_ref, chunk_ref, exp_ref, first_ref, valid_ref, x_ref, wg_ref, wu_ref, act_ref,
                       wgb_ref, wub_ref):
    s = pl.program_id(0)

    @pl.when(first_ref[s] == 1)
    def _():
        _cast_rows(wg_ref, wgb_ref, 128)
        _cast_rows(wu_ref, wub_ref, 128)

    @pl.when(valid_ref[s] == 1)
    def _():
        _swiglu_act(x_ref[...].astype(BF16), wgb_ref, wub_ref, act_ref)


def _experts_down_kernel(act_tile_ref, exp_ref, first_ref, valid_ref, act_ref, wd_ref, o_ref, wdb_ref):
    i = pl.program_id(0)

    @pl.when(first_ref[i] == 1)
    def _():
        _cast_rows(wd_ref, wdb_ref, 256)

    @pl.when(valid_ref[i] == 1)
    def _():
        o_ref[...] = _dot(act_ref[...], wdb_ref[...])

    @pl.when(valid_ref[i] == 0)
    def _():
        o_ref[...] = jnp.zeros_like(o_ref)


def _experts(up_tables, down_tables, xs, w_gate, w_up, w_down):
    up = pl.pallas_call(
        _experts_up_kernel,
        grid_spec=pltpu.PrefetchScalarGridSpec(
            num_scalar_prefetch=5,
            grid=(NC_E * NT_E,),
            in_specs=[
                pl.BlockSpec((TME, D_MODEL), lambda s, t, c, e, f, v: (t[s], 0)),
                pl.BlockSpec((None, D_MODEL, TF_E), lambda s, t, c, e, f, v: (e[s], 0, c[s])),
                pl.BlockSpec((None, D_MODEL, TF_E), lambda s, t, c, e, f, v: (e[s], 0, c[s])),
            ],
            out_specs=pl.BlockSpec((TME, TF_E), lambda s, t, c, e, f, v: (t[s], c[s])),
            scratch_shapes=[pltpu.VMEM((D_MODEL, TF_E), BF16), pltpu.VMEM((D_MODEL, TF_E), BF16)],
        ),
        out_shape=jax.ShapeDtypeStruct((NT_E * TME, D_FF_E), BF16),
        compiler_params=_cparams(("arbitrary",)),
        name="experts_up",
    )(*up_tables, xs, w_gate, w_up)
    return pl.pallas_call(
        _experts_down_kernel,
        grid_spec=pltpu.PrefetchScalarGridSpec(
            num_scalar_prefetch=4,
            grid=(NT_E,),
            in_specs=[
                pl.BlockSpec((TME, D_FF_E), lambda i, a, e, f, v: (a[i], 0)),
                pl.BlockSpec((None, D_FF_E, D_MODEL), lambda i, a, e, f, v: (e[i], 0, 0)),
            ],
            out_specs=pl.BlockSpec((TME, D_MODEL), lambda i, a, e, f, v: (i, 0)),
            scratch_shapes=[pltpu.VMEM((D_FF_E, D_MODEL), BF16)],
        ),
        out_shape=jax.ShapeDtypeStruct((NT_E * TME, D_MODEL), F32),
        compiler_params=_cparams(("arbitrary",)),
        name="experts_down",
    )(*down_tables, up, w_down)


def _combine_kernel(x_ref, y1_ref, y2_ref, w1_ref, w2_ref, g2_ref, gam_ref, octx_ref, olat_ref):
    rep = D_MODEL // LANES
    w1 = jnp.concatenate([w1_ref[...]] * rep, axis=1)
    w2 = jnp.concatenate([w2_ref[...]] * rep, axis=1)
    x2 = x_ref[...] + g2_ref[...] * (w1 * y1_ref[...] + w2 * y2_ref[...])
    y = _rms(x2, gam_ref[...])
    is_lat = pl.program_id(0) >= NCB

    @pl.when(jnp.logical_not(is_lat))
    def _():
        octx_ref[...] = y

    @pl.when(is_lat)
    def _():
        olat_ref[...] = y


def _combine(x1, y1, y2, w1, w2, mod, gamma):
    return pl.pallas_call(
        _combine_kernel,
        grid=(NB,),
        in_specs=[_row_spec(D_MODEL), _row_spec(D_MODEL), _row_spec(D_MODEL), _row_spec(LANES), _row_spec(LANES),
                  _mod_spec(5), _const_spec((1, D_MODEL))],
        out_specs=[_ctx_spec(D_MODEL), _lat_spec(D_MODEL)],
        out_shape=[jax.ShapeDtypeStruct((N_CTX, D_MODEL), F32), jax.ShapeDtypeStruct((N_LAT, D_MODEL), F32)],
        compiler_params=_cparams(("arbitrary",)),
        name="combine",
    )(x1, y1, y2, w1, w2, mod, gamma)


def _rope_tables(rot_dim):
    n = DEC_SEQ
    rows = jnp.repeat(jnp.arange(n // GRID_W, dtype=F32), GRID_W)
    cols = jnp.tile(jnp.arange(GRID_W, dtype=F32), n // GRID_W)
    half = rot_dim // 2
    freqs = ROPE_THETA ** (-jnp.arange(0, half, 2, dtype=F32) / half)
    ang_r, ang_c = rows[:, None] * freqs, cols[:, None] * freqs
    cos = jnp.concatenate([jnp.cos(ang_r)] * 2 + [jnp.cos(ang_c)] * 2, axis=-1)
    sin = jnp.concatenate([-jnp.sin(ang_r), jnp.sin(ang_r), -jnp.sin(ang_c), jnp.sin(ang_c)], axis=-1)
    rep = LANES // rot_dim
    return jnp.tile(cos, (1, rep)), jnp.tile(sin, (1, rep))


def _with_ones(v):
    return jnp.concatenate([v, jnp.ones_like(v)], axis=-1)


def _rows(a, idx):
    return a.at[idx].get(mode="promise_in_bounds")


def _route(idx, counts_f):
    quarter = LANES // 4
    experts = jnp.arange(N_EXPERTS, dtype=jnp.int32)
    e2 = jnp.stack([idx[:, 0], idx[:, quarter]], axis=1)
    rank2 = jnp.stack([idx[:, 2 * quarter], idx[:, 3 * quarter]], axis=1)
    counts = counts_f[0, :N_EXPERTS].astype(jnp.int32)
    tiles = (counts + TME - 1) // TME
    tile_end = jnp.cumsum(tiles)
    tile_start = tile_end - tiles
    start = jnp.cumsum(counts) - counts
    onehot = e2[:, :, None] == experts[None, None, :]
    pos = jnp.sum(jnp.where(onehot, tile_start[None, None, :] * TME, 0), axis=-1) + rank2
    tile_id = jnp.arange(NT_E, dtype=jnp.int32)
    tile_expert = jnp.minimum(jnp.sum((tile_id[:, None] >= tile_end[None, :]).astype(jnp.int32), axis=1),
                              N_EXPERTS - 1).astype(jnp.int32)
    tile_valid = (tile_id < tile_end[-1]).astype(jnp.int32)
    t_oh = tile_expert[:, None] == experts[None, :]
    t_first = jnp.sum(jnp.where(t_oh, (start - tile_start * TME)[None, :], 0), axis=1) + tile_id * TME
    t_last = jnp.sum(jnp.where(t_oh, (start + counts)[None, :], 0), axis=1)
    order = jnp.argsort(e2.reshape(-1), stable=True).astype(jnp.int32)
    n_assign = order.shape[0]
    g_idx = t_first[:, None] + jnp.arange(TME, dtype=jnp.int32)[None, :]
    live = jnp.logical_and(g_idx < t_last[:, None], tile_valid[:, None] == 1)
    src = _rows(order, jnp.clip(g_idx, 0, n_assign - 1).reshape(-1)) // 2
    src_tok = jnp.where(live.reshape(-1), src, 0).astype(jnp.int32)

    n_tiles = tile_end[-1]
    last_tile = jnp.maximum(n_tiles - 1, 0)
    last_expert = jnp.sum(jnp.where(tile_id == last_tile, tile_expert, 0))
    t_tstart = jnp.sum(jnp.where(t_oh, tile_start[None, :], 0), axis=1)
    down_tables = (
        jnp.where(tile_valid == 1, tile_id, last_tile).astype(jnp.int32),
        tile_expert,
        jnp.logical_and(tile_valid == 1, tile_id == t_tstart).astype(jnp.int32),
        tile_valid,
    )
    s_id = jnp.arange(NC_E * NT_E, dtype=jnp.int32)
    s_exp = jnp.minimum(jnp.sum((s_id[:, None] >= NC_E * tile_end[None, :]).astype(jnp.int32), axis=1),
                        N_EXPERTS - 1)
    s_oh = s_exp[:, None] == experts[None, :]
    s_tstart = jnp.sum(jnp.where(s_oh, tile_start[None, :], 0), axis=1)
    s_ntile = jnp.maximum(jnp.sum(jnp.where(s_oh, tiles[None, :], 0), axis=1), 1)
    rel = s_id - NC_E * s_tstart
    s_valid = s_id < NC_E * n_tiles
    up_tables = (
        jnp.where(s_valid, s_tstart + rel % s_ntile, last_tile).astype(jnp.int32),
        jnp.where(s_valid, rel // s_ntile, NC_E - 1).astype(jnp.int32),
        jnp.where(s_valid, s_exp, last_expert).astype(jnp.int32),
        jnp.logical_and(s_valid, rel % s_ntile == 0).astype(jnp.int32),
        s_valid.astype(jnp.int32),
    )
    return src_tok, pos, up_tables, down_tables


def lambda_init(layer):
    return 0.8 - 0.6 * math.exp(-0.3 * layer)


def kernel(x_prompt, x_sample, state_a, cache_b_k, cache_b_v, cache_c_k, cache_c_v, c, c_ctx, w_mod, b_mod, norm_mix, norm_ffn, w_in_even, w_gate2_a, b_gate_a, g_gla, lam_q1, lam_k1, lam_q2, lam_k2, g_sub_b, w_out_even, w_in_odd, g_q_c, g_k_c, w_out_odd, ffn_gate, ffn_up, ffn_down, w_router, exp_gate, exp_up, exp_down, norm_final):
    x_ctx, x_lat = x_prompt.reshape(N_CTX, D_MODEL), x_sample.reshape(N_LAT, D_MODEL)
    cond =jnp.concatenate([c_ctx[None, :], c, jnp.zeros((N_COND - 1 - DEC_BATCH, D_MODEL), F32)], axis=0)
    mod = _modulation(cond, w_mod, b_mod).reshape(DEPTH, N_COND, 1, 6 * D_MODEL)

    w = w_in_even[0]
    na = 2 * A_QK + 2 * A_V
    gate_lo = na
    gate_hi = na + 2 * GATE_RANK
    col_scale = jnp.concatenate([jnp.full((A_QK,), DK_A ** -0.5, F32), jnp.ones((na - A_QK,), F32),
                                 jnp.full((B_QK,), DH_B ** -0.5, F32), jnp.ones((B_QK + B_V,), F32)])
    w_main = (jnp.concatenate([w[:, :gate_lo], w[:, gate_hi:]], axis=1) * col_scale).astype(BF16)
    w_gates = jnp.pad(w[:, gate_lo:gate_hi], ((0, 0), (0, LANES - 2 * GATE_RANK))).astype(BF16)
    g2 = jnp.zeros((LANES, 2 * A_QK), F32)
    g2 = g2.at[:GATE_RANK, :A_QK].set(w_gate2_a[0, 0]).at[GATE_RANK:2 * GATE_RANK, A_QK:].set(w_gate2_a[0, 1])
    bg = b_gate_a[0].reshape(1, 2 * A_QK)
    cos_b, sin_b = _rope_tables(DH_B)
    a_proj, glog, qb, kb, vb, kb_ctx, vb_ctx = _inproj_even(
        x_ctx, x_lat, norm_mix[0][None, :], mod[0], w_main, w_gates, g2.astype(BF16), bg, cos_b, sin_b)

    s0_all = jnp.concatenate([jnp.zeros((1, 2, A_QK, DV_A), F32),
                              state_a[:, 0].reshape(DEC_BATCH, 2, A_QK, DV_A)], axis=0)
    sel = np.arange(A_QK)[:, None] // DK_A == np.arange(LANES)[None, :] // GLA_SB
    m1 = jnp.asarray(sel.astype(np.float32)).astype(BF16)
    o_f, o_b, s_fin = _gla(a_proj, glog, s0_all, m1)

    k_lat = jnp.concatenate([kb[N_CTX:].reshape(DEC_BATCH, DEC_SEQ, B_QK),
                             cache_b_k[:, 0].reshape(DEC_BATCH, PAST_LEN, B_QK).astype(BF16)], axis=1)
    v_lat = jnp.concatenate([vb[N_CTX:].reshape(DEC_BATCH, DEC_SEQ, 2 * B_V),
                             _with_ones(cache_b_v[:, 0].astype(BF16)).reshape(DEC_BATCH, PAST_LEN, 2 * B_V)], axis=1)
    lam_params = [p[0][None, :] for p in (lam_q1, lam_k1, lam_q2, lam_k2)]
    attn_b = _diffattn(qb, kb, vb, k_lat, v_lat, lam_params, g_sub_b[0][None, :], lambda_init(0))

    x = _postmix_even(o_f, o_b, a_proj, *attn_b, x_ctx, x_lat, mod[0], g_gla[0][None, :],
                      w_out_even[0].astype(BF16))
    x = _ffn_even(x, norm_ffn[0][None, :], mod[0], ffn_gate[0].astype(BF16), ffn_up[0].astype(BF16),
                  ffn_down[0].astype(BF16))

    cos_c, sin_c = _rope_tables(DH_C)
    q_c, k_c, v_c, kc_ctx, vc_ctx = _inproj_odd(x, norm_mix[1][None, :], mod[1], w_in_odd[0].astype(BF16),
                                     g_q_c[0][None, :], g_k_c[0][None, :], cos_c, sin_c)
    nkv = HKV_C * DH_C
    k_lat = jnp.concatenate([k_c[N_CTX:].reshape(DEC_BATCH, DEC_SEQ, nkv),
                             cache_c_k[:, 0].reshape(DEC_BATCH, PAST_LEN, nkv).astype(BF16)], axis=1)
    v_lat = jnp.concatenate([v_c[N_CTX:].reshape(DEC_BATCH, DEC_SEQ, 2 * nkv),
                             _with_ones(cache_c_v[:, 0].astype(BF16)).reshape(DEC_BATCH, PAST_LEN, 2 * nkv)], axis=1)
    attn_c = _gqa(q_c, k_c, v_c, k_lat, v_lat)
    w_r = jnp.pad(w_router[0], ((0, 0), (0, LANES - N_EXPERTS)))
    x1, h_moe, w1, w2, idx, counts = _postmix_odd(*attn_c, x, mod[1], w_out_odd[0].astype(BF16),
                                                  norm_ffn[1][None, :], w_r)

    src_tok, pos, up_tables, down_tables = _route(idx, counts)
    xs = _rows(h_moe, src_tok)
    ys = _experts(up_tables, down_tables, xs, exp_gate[0], exp_up[0], exp_down[0])
    y1 = _rows(ys, pos[:, 0])
    y2 = _rows(ys, pos[:, 1])
    y_ctx, y_lat = _combine(x1, y1, y2, w1, w2, mod[1], norm_final[None, :])

    y_prompt = y_ctx.reshape(BATCH, SEQ, D_MODEL)
    y_sample = y_lat.reshape(DEC_BATCH, DEC_SEQ, D_MODEL)
    new_state_a = s_fin[:BATCH].reshape(BATCH, 1, 2, H_A, DK_A, DV_A)
    new_b_k = jnp.transpose(kb_ctx.reshape(BATCH, 1, H_B, 2, DH_B, SEQ), (0, 1, 5, 2, 3, 4))
    new_b_v = vb_ctx.reshape(BATCH, 1, SEQ, H_B, DV_B)
    new_c_k = kc_ctx.reshape(BATCH, 1, SEQ, HKV_C, DH_C)
    new_c_v = vc_ctx.reshape(BATCH, 1, SEQ, HKV_C, DH_C)
    return (y_prompt, y_sample, new_state_a, new_b_k, new_b_v, new_c_k, new_c_v)
```

```python
import functools
import math

import numpy as np
import jax
import jax.numpy as jnp
from jax import lax
from jax.experimental import pallas as pl
from jax.experimental.pallas import tpu as pltpu

D_MODEL = 1024
BATCH = 32
SEQ = 256
DEPTH = 2
DEC_BATCH = 2
DEC_SEQ = 2048
PAST_LEN = 512
GRID_W = 64
H_A, DK_A, DV_A = 4, 64, 128
GATE_RANK = 16
GATE_TAU = 16.0
H_B, DH_B = 4, 64
DV_B = 2 * DH_B
H_C, HKV_C, DH_C = 8, 2, 128
D_FF = 2816
N_EXPERTS = 8
D_FF_E = 3584
ROPE_THETA = 10000.0
EPS = 1e-6
A_QK = H_A * DK_A
A_V = H_A * DV_A
B_QK = H_B * 2 * DH_B
B_V = H_B * DV_B

F32 = jnp.float32
BF16 = jnp.bfloat16
LOG2E = math.log2(math.e)

V7X_VMEM_BYTES = 64 * 1024 * 1024
VMEM_LIMIT = V7X_VMEM_BYTES * 7 // 8
LANES = 128
MXU_N = 256
TM = 256
N_CTX = BATCH * SEQ
N_LAT = DEC_BATCH * DEC_SEQ
M_TOK = N_CTX + N_LAT
NB = M_TOK // TM
NCB = N_CTX // TM
LBB = DEC_SEQ // TM
N_COND = 8
GLA_C = 64
GLA_SB = 8
TME = 512
NT_E = 2 * M_TOK // TME + N_EXPERTS
NC_E = 2
TF_E = D_FF_E // NC_E
TQ_LAT = 256

assert SEQ == TM and DEC_SEQ % TM == 0 and DEPTH == 2


def _cparams(sem):
    return pltpu.CompilerParams(dimension_semantics=sem, vmem_limit_bytes=VMEM_LIMIT)


def _group(i):
    return jnp.where(i < NCB, 0, 1 + (i - NCB) // LBB)


def _pos_block(i):
    return jnp.where(i < NCB, 0, (i - NCB) % LBB)


def _mod_spec(chunk):
    return pl.BlockSpec((None, 1, D_MODEL), lambda i, c=chunk: (_group(i), 0, c))


def _row_spec(width, col=0):
    return pl.BlockSpec((TM, width), lambda i, c=col: (i, c))


def _ctx_spec(width):
    return pl.BlockSpec((TM, width), lambda i: (jnp.minimum(i, NCB - 1), 0))


def _ctx_parts_spec(parts):
    return pl.BlockSpec((TM * parts, LANES), lambda i: (jnp.minimum(i, NCB - 1), 0))


def _lat_spec(width):
    return pl.BlockSpec((TM, width), lambda i: (jnp.maximum(i - NCB, 0), 0))


def _const_spec(shape):
    nd = len(shape)
    return pl.BlockSpec(shape, lambda i, nd=nd: (0,) * nd, pipeline_mode=pl.Buffered(1))


def _dot(a, b):
    return jnp.dot(a, b, preferred_element_type=F32)


def _dot_nt(a, b):
    return lax.dot_general(a, b, (((1,), (1,)), ((), ())), preferred_element_type=F32)


def _split2(a):
    hi = a.astype(BF16)
    lo = (a - hi.astype(F32)).astype(BF16)
    return hi, lo


def _dot_hi(a, b):
    a_hi, a_lo = _split2(a)
    b_hi, b_lo = _split2(b)
    return _dot(a_hi, b_hi) + (_dot(a_hi, b_lo) + _dot(a_lo, b_hi))


def _silu(x):
    return (0.5 * x) * (1.0 + jnp.tanh(0.5 * x))


def _log_sigmoid(x):
    return jnp.minimum(x, 0.0) - jnp.log(1.0 + jnp.exp(-jnp.abs(x)))


def _rms(x, g):
    return x * lax.rsqrt(jnp.mean(x * x, axis=-1, keepdims=True) + EPS) * g


def _norm_mod(x, gamma, sc, sh):
    return _rms(x, gamma) * (1.0 + sc) + sh


def _copy_ctx_transposed(stage3_ref, dst_ref):
    @pl.when(pl.program_id(0) < NCB)
    def _():
        def body(j, carry):
            dst_ref[pl.ds(pl.multiple_of(j * LANES, LANES), LANES), :] = jnp.transpose(stage3_ref[j])
            return carry

        lax.fori_loop(0, stage3_ref.shape[0], body, 0)


def _pick_rows(ctx_ref, lat_ref):
    return jnp.where(pl.program_id(0) >= NCB, lat_ref[...], ctx_ref[...])


def _copy_ctx(stage_ref, outs):
    piece = 64

    @pl.when(pl.program_id(0) < NCB)
    def _():
        def body(r, carry):
            rows = pl.ds(pl.multiple_of(r * piece, piece), piece)
            for lo, width, parts, dst_ref in outs:
                for p in range(parts):
                    val = stage_ref[rows, lo + p * width:lo + (p + 1) * width]
                    if width < LANES:
                        val = jnp.concatenate([val, jnp.zeros((piece, LANES - width), F32)], axis=1)
                    dst_ref[pl.ds(pl.multiple_of(r * piece * parts, piece * parts) + p, piece, stride=parts), :] = val
            return carry

        lax.fori_loop(0, stage_ref.shape[0] // piece, body, 0)


def _rope(x, cos, sin, half):
    lane = lax.broadcasted_iota(jnp.int32, x.shape, 1)
    first = (lane % (2 * half)) < half
    swapped = jnp.where(first, pltpu.roll(x, LANES - half, 1), pltpu.roll(x, half, 1))
    return x * cos + swapped * sin


def _mod_kernel(cond_ref, w_ref, b_ref, o_ref):
    o_ref[...] = _dot_hi(_silu(cond_ref[...]), w_ref[...]) + b_ref[...]


def _modulation(cond, w_mod, b_mod):
    tn = 1536
    return pl.pallas_call(
        _mod_kernel,
        grid=(DEPTH, 6 * D_MODEL // tn),
        in_specs=[
            pl.BlockSpec((N_COND, D_MODEL), lambda l, j: (0, 0)),
            pl.BlockSpec((None, D_MODEL, tn), lambda l, j: (l, 0, j)),
            pl.BlockSpec((None, 1, tn), lambda l, j: (l, 0, j)),
        ],
        out_specs=pl.BlockSpec((None, N_COND, tn), lambda l, j: (l, 0, j)),
        out_shape=jax.ShapeDtypeStruct((DEPTH, N_COND, 6 * D_MODEL), F32),
        compiler_params=_cparams(("arbitrary", "arbitrary")),
        name="modulation",
    )(cond, w_mod, b_mod.reshape(DEPTH, 1, 6 * D_MODEL))


def _inproj_even_kernel(xc_ref, xl_ref, gam_ref, sh_ref, sc_ref, w_ref, wg_ref, g2_ref, bg_ref, cos_ref, sin_ref,
                        a_ref, glog_ref, qb_ref, kb_ref, vb_ref, kctx_ref, vctx_ref, stage_ref, stagek_ref):
    is_lat = pl.program_id(0) >= NCB
    hb = _norm_mod(_pick_rows(xc_ref, xl_ref), gam_ref[...], sc_ref[...], sh_ref[...]).astype(BF16)
    cos = jnp.where(is_lat, cos_ref[...], 1.0)
    sin = jnp.where(is_lat, sin_ref[...], 0.0)
    na = 2 * A_QK + 2 * A_V
    for lo in range(0, na, MXU_N):
        a_ref[:, lo:lo + MXU_N] = _dot(hb, w_ref[:, lo:lo + MXU_N])
    for lo in range(0, B_QK, MXU_N):
        zq = _dot(hb, w_ref[:, na + lo:na + lo + MXU_N])
        zk = _dot(hb, w_ref[:, na + B_QK + lo:na + B_QK + lo + MXU_N])
        zv = _dot(hb, w_ref[:, na + 2 * B_QK + lo:na + 2 * B_QK + lo + MXU_N])
        stage_ref[:, lo:lo + MXU_N] = zv
        for j in range(0, MXU_N, LANES):
            cs = slice(lo + j, lo + j + LANES)
            vb_ref[:, 2 * (lo + j):2 * (lo + j) + DV_B] = zv[:, j:j + LANES].astype(BF16)
            vb_ref[:, 2 * (lo + j) + DV_B:2 * (lo + j + LANES)] = jnp.ones((TM, DV_B), BF16)
            stagek_ref[(lo + j) // LANES] = zk[:, j:j + LANES]
            qb_ref[:, cs] = (_rope(zq[:, j:j + LANES], cos, sin, DH_B // 4) * LOG2E).astype(BF16)
            kb_ref[:, cs] = _rope(zk[:, j:j + LANES], cos, sin, DH_B // 4).astype(BF16)
    gates = _dot(hb, wg_ref[...])
    xg = _dot(gates.astype(BF16), g2_ref[...]) + bg_ref[...]
    glog_ref[...] = _log_sigmoid(xg) * (1.0 / GATE_TAU)
    _copy_ctx(stage_ref, ((0, DV_B, H_B, vctx_ref),))
    _copy_ctx_transposed(stagek_ref, kctx_ref)


def _inproj_even(x_ctx, x_lat, gamma, mod, w_main, w_gates, g2, bg, cos, sin):
    na = 2 * A_QK + 2 * A_V
    nz = na + 2 * B_QK + B_V
    return pl.pallas_call(
        _inproj_even_kernel,
        grid=(NB,),
        in_specs=[
            _ctx_spec(D_MODEL), _lat_spec(D_MODEL), _const_spec((1, D_MODEL)), _mod_spec(0), _mod_spec(1),
            _const_spec((D_MODEL, nz)), _const_spec((D_MODEL, LANES)), _const_spec((LANES, 2 * A_QK)),
            _const_spec((1, 2 * A_QK)),
            pl.BlockSpec((TM, LANES), lambda i: (_pos_block(i), 0)),
            pl.BlockSpec((TM, LANES), lambda i: (_pos_block(i), 0)),
        ],
        out_specs=[_pair_spec(na), _pair_spec(2 * A_QK), _row_spec(B_QK), _row_spec(B_QK), _row_spec(2 * B_V),
                   pl.BlockSpec((None, B_QK, TM), lambda i: (jnp.minimum(i, NCB - 1), 0, 0)), _ctx_parts_spec(H_B)],
        out_shape=[
            jax.ShapeDtypeStruct((M_TOK, na), F32), jax.ShapeDtypeStruct((M_TOK, 2 * A_QK), F32),
            jax.ShapeDtypeStruct((M_TOK, B_QK), BF16), jax.ShapeDtypeStruct((M_TOK, B_QK), BF16),
            jax.ShapeDtypeStruct((M_TOK, 2 * B_V), BF16), jax.ShapeDtypeStruct((BATCH, B_QK, SEQ), F32),
            jax.ShapeDtypeStruct((N_CTX * H_B, LANES), F32),
        ],
        scratch_shapes=[pltpu.VMEM((TM, B_V), F32), pltpu.VMEM((B_QK // LANES, TM, LANES), F32)],
        compiler_params=_cparams(("arbitrary",)),
        name="inproj_even",
    )(x_ctx, x_lat, gamma, mod, mod, w_main, w_gates, g2, bg, cos, sin)


def _gla_chunk(q_ref, k_ref, v_ref, g_ref, o_ref, c, s_ref, a_ref, b_ref, m1_ref, head_mask, same_block, rev):
    q, k, v = q_ref[c], k_ref[c], v_ref[c]
    g = g_ref[c] * LOG2E
    v_b = v.astype(BF16)
    r_i = lax.broadcasted_iota(jnp.int32, (GLA_C, GLA_C), 0)
    c_i = lax.broadcasted_iota(jnp.int32, (GLA_C, GLA_C), 1)
    tri = jnp.where((c_i >= r_i) if rev else (c_i <= r_i), 1.0, 0.0).astype(BF16)
    g_hi = g.astype(BF16)
    g_r = g - g_hi.astype(F32)
    g_mid = g_r.astype(BF16)
    g_lo = (g_r - g_mid.astype(F32)).astype(BF16)
    bc = _dot(tri, g_hi) + (_dot(tri, g_mid) + _dot(tri, g_lo))
    b_ref[...] = bc
    yield
    t_i = lax.broadcasted_iota(jnp.int32, (GLA_SB, A_QK), 0)
    group = 2 * GLA_SB
    for i0 in range(0, GLA_C, group):
        blocks = range(i0, i0 + group, GLA_SB)
        q_i = [q_ref[c, i:i + GLA_SB, :] for i in blocks]
        b_i = [b_ref[i:i + GLA_SB, :] for i in blocks]
        for j in range(GLA_SB):
            keep = (t_i <= j) if rev else (t_i >= j)
            pair = []
            for n, i in enumerate(blocks):
                k_s = k_ref[c, i + j:i + j + 1, :]
                b_s = b_ref[i + j:i + j + 1, :]
                pair.append(jnp.where(keep, q_i[n] * k_s * jnp.exp2(b_i[n] - b_s), 0.0))
            a_ref[j * GLA_C + i0:j * GLA_C + i0 + group, :] = jnp.concatenate(pair, axis=0).astype(BF16)
    yield
    last = bc[0:1] if rev else bc[GLA_C - 1:GLA_C]
    qc = q * jnp.exp2(bc)
    kl = k * jnp.exp2(last - bc)
    row = lax.broadcasted_iota(jnp.int32, (GLA_C, A_QK), 0)

    def expand(x):
        return jnp.where(head_mask, jnp.concatenate([x] * H_A, axis=0), 0.0).astype(BF16)

    scores = None
    half = GLA_C // 2
    while half >= GLA_SB:
        mids = [bc[b + half:b + half + 1] if rev else bc[b + half - 1:b + half] for b in range(0, GLA_C, 2 * half)]
        ref = mids[0] if len(mids) == 1 else jnp.concatenate(
            [jnp.broadcast_to(m, (2 * half, A_QK)) for m in mids], axis=0)
        upper = (row // half) % 2 == 1
        later, earlier = (jnp.logical_not(upper), upper) if rev else (upper, jnp.logical_not(upper))
        q_l = jnp.where(later, q * jnp.exp2(bc - ref), 0.0)
        k_e = jnp.where(earlier, k * jnp.exp2(ref - bc), 0.0).astype(BF16)
        term = _dot_nt(expand(q_l), k_e)
        if 2 * half < GLA_C:
            term = jnp.where(same_block[2 * half], term, 0.0)
        scores = term if scores is None else scores + term
        half //= 2
    s_old = s_ref[...]
    inter = _dot(expand(qc), s_old.astype(BF16))
    sums = _dot(a_ref[...], m1_ref[...])
    yield
    lane_c = lax.broadcasted_iota(jnp.int32, (GLA_C, LANES), 1)
    within = None
    for j in range(GLA_SB):
        part = jnp.where(lane_c % GLA_SB == j, sums[j * GLA_C:(j + 1) * GLA_C], 0.0)
        within = part if within is None else within + part
    lane = lax.broadcasted_iota(jnp.int32, (GLA_SB, LANES), 1)
    per_head = []
    for h in range(H_A):
        rows_h = []
        for i in range(0, GLA_C, GLA_SB):
            moved = pltpu.roll(within[i:i + GLA_SB], (i - h * GLA_SB) % LANES, 1)
            rows_h.append(jnp.where(lane // GLA_SB == i // GLA_SB, moved, 0.0))
        per_head.append(jnp.concatenate(rows_h, axis=0)[:, :GLA_C])
    scores = scores + jnp.concatenate(per_head, axis=0)
    o_heads = []
    for h in range(H_A):
        hs = slice(h * GLA_C, (h + 1) * GLA_C)
        o_heads.append(inter[hs] + _dot(scores[hs].astype(BF16), v_b[:, h * DV_A:(h + 1) * DV_A]))
    o_ref[c] = jnp.concatenate(o_heads, axis=1)
    yield
    t = jnp.transpose(jnp.concatenate([kl, jnp.broadcast_to(last, (GLA_C, A_QK))], axis=0))
    kv = _dot(t[:, :GLA_C].astype(BF16), v_b)
    a_col = jnp.exp2(t[:, GLA_C:GLA_C + 1])
    for h in range(H_A):
        hs = slice(h * DK_A, (h + 1) * DK_A)
        s_ref[hs, :] = a_col[hs] * s_old[hs] + kv[hs, h * DV_A:(h + 1) * DV_A]


def _gla_kernel(fblk_ref, bblk_ref, first_ref, inita_ref, initb_ref,
                qf_ref, kf_ref, vf_ref, gf_ref, qr_ref, kr_ref, vr_ref, gr_ref, s0a_ref, s0b_ref, m1_ref,
                of_ref, ob_ref, sfin_ref, s_refs, a_refs, b_refs):
    step = pl.program_id(0)

    @pl.when(first_ref[step] == 1)
    def _():
        for seq, s0_ref in enumerate((s0a_ref, s0b_ref)):
            for d in range(2):
                s_refs[2 * seq + d] = s0_ref[d]

    r_h = lax.broadcasted_iota(jnp.int32, (H_A * GLA_C, A_QK), 0) // GLA_C
    c_h = lax.broadcasted_iota(jnp.int32, (H_A * GLA_C, A_QK), 1) // DK_A
    head_mask = r_h == c_h
    t_s = lax.broadcasted_iota(jnp.int32, (H_A * GLA_C, GLA_C), 0) % GLA_C
    s_s = lax.broadcasted_iota(jnp.int32, (H_A * GLA_C, GLA_C), 1)
    same_block = {}
    size = 2 * GLA_SB
    while size < GLA_C:
        same_block[size] = (t_s // size) == (s_s // size)
        size *= 2
    nch = TM // GLA_C

    def body(c, carry):
        chains = []
        for seq in range(2):
            chains.append(_gla_chunk(qf_ref, kf_ref, vf_ref, gf_ref, of_ref, seq * nch + c, s_refs.at[2 * seq],
                                     a_refs.at[2 * seq], b_refs.at[2 * seq], m1_ref, head_mask, same_block, False))
            chains.append(_gla_chunk(qr_ref, kr_ref, vr_ref, gr_ref, ob_ref, seq * nch + nch - 1 - c,
                                     s_refs.at[2 * seq + 1], a_refs.at[2 * seq + 1], b_refs.at[2 * seq + 1], m1_ref,
                                     head_mask, same_block, True))
        live = chains
        while live:
            live = [ch for ch in live if next(ch, True) is None]
        return carry

    lax.fori_loop(0, nch, body, 0)

    @pl.when(step < BATCH // 2)
    def _():
        for seq in range(2):
            for d in range(2):
                sfin_ref[seq, d] = s_refs[2 * seq + d]


def _pair_block(i):
    j = i - NCB
    return jnp.where(i < NCB, i, NCB + 2 * (j % LBB) + j // LBB)


def _pair_spec(width, col=0):
    return pl.BlockSpec((TM, width), lambda i, c=col: (_pair_block(i), c))


def _gla_tables():
    fblk, bblk, first, init_a, init_b = [], [], [], [], []
    for p in range(BATCH // 2):
        fblk.append(p), bblk.append(p), first.append(1), init_a.append(0), init_b.append(0)
    for j in range(LBB):
        fblk.append(NCB // 2 + j)
        bblk.append(NCB // 2 + LBB - 1 - j)
        first.append(1 if j == 0 else 0)
        init_a.append(1), init_b.append(2)
    return [jnp.asarray(np.array(t, np.int32)) for t in (fblk, bblk, first, init_a, init_b)]


def _gla(a_proj, glog, s0_all, m1):
    tables = _gla_tables()
    nsteps = int(tables[0].shape[0])
    nseq = BATCH + DEC_BATCH

    def fmap(col):
        return lambda i, fb, bb, fi, ia, ib: (fb[i], 0, col)

    def rmap(col):
        return lambda i, fb, bb, fi, ia, ib: (bb[i], 0, col)

    nch = 2 * TM // GLA_C
    a3 = a_proj.reshape(M_TOK // GLA_C, GLA_C, a_proj.shape[1])
    g3 = glog.reshape(M_TOK // GLA_C, GLA_C, glog.shape[1])
    gs = pltpu.PrefetchScalarGridSpec(
        num_scalar_prefetch=5,
        grid=(nsteps,),
        in_specs=[
            pl.BlockSpec((nch, GLA_C, A_QK), fmap(0)), pl.BlockSpec((nch, GLA_C, A_QK), fmap(1)),
            pl.BlockSpec((nch, GLA_C, A_V), fmap(1)), pl.BlockSpec((nch, GLA_C, A_QK), fmap(0)),
            pl.BlockSpec((nch, GLA_C, A_QK), rmap(0)), pl.BlockSpec((nch, GLA_C, A_QK), rmap(1)),
            pl.BlockSpec((nch, GLA_C, A_V), rmap(1)), pl.BlockSpec((nch, GLA_C, A_QK), rmap(1)),
            pl.BlockSpec((None, 2, A_QK, DV_A), lambda i, fb, bb, fi, ia, ib: (ia[i], 0, 0, 0)),
            pl.BlockSpec((None, 2, A_QK, DV_A), lambda i, fb, bb, fi, ia, ib: (ib[i], 0, 0, 0)),
            pl.BlockSpec((A_QK, LANES), lambda i, fb, bb, fi, ia, ib: (0, 0)),
        ],
        out_specs=[
            pl.BlockSpec((nch, GLA_C, A_V), fmap(0)), pl.BlockSpec((nch, GLA_C, A_V), rmap(0)),
            pl.BlockSpec((2, 2, A_QK, DV_A),
                         lambda i, fb, bb, fi, ia, ib: (jnp.minimum(fb[i], BATCH // 2 - 1), 0, 0, 0)),
        ],
        scratch_shapes=[
            pltpu.VMEM((4, A_QK, DV_A), F32),
            pltpu.VMEM((4, GLA_SB * GLA_C, A_QK), BF16),
            pltpu.VMEM((4, GLA_C, A_QK), F32),
        ],
    )
    o_f, o_b, s_fin = pl.pallas_call(
        _gla_kernel,
        grid_spec=gs,
        out_shape=[
            jax.ShapeDtypeStruct((M_TOK // GLA_C, GLA_C, A_V), F32),
            jax.ShapeDtypeStruct((M_TOK // GLA_C, GLA_C, A_V), F32),
            jax.ShapeDtypeStruct((BATCH, 2, A_QK, DV_A), F32),
        ],
        compiler_params=_cparams(("arbitrary",)),
        name="gla",
    )(*tables, a3, a3, a3, g3, a3, a3, a3, g3, s0_all, s0_all, m1)
    return o_f.reshape(M_TOK, A_V), o_b.reshape(M_TOK, A_V), s_fin


def _attend(q_b, k_b, v_ones):
    s2 = _dot_nt(q_b, k_b)
    e = jnp.exp2(s2 - jnp.max(s2, axis=-1, keepdims=True)).astype(BF16)
    both = _dot(e, v_ones)
    dv = v_ones.shape[1] // 2
    return both[:, :dv] * (1.0 / both[:, dv:])


def _diffattn_body(q_ref, k_ref, v_ref, lq1_ref, lk1_ref, lq2_ref, lk2_ref, gsub_ref, o_ref, lam_init):
    lam = (jnp.exp(jnp.sum(lq1_ref[...] * lk1_ref[...], axis=-1, keepdims=True))
           - jnp.exp(jnp.sum(lq2_ref[...] * lk2_ref[...], axis=-1, keepdims=True)) + lam_init)
    lane = lax.broadcasted_iota(jnp.int32, (q_ref.shape[0], LANES), 1)
    for h in range(H_B):
        hs = slice(h * LANES, (h + 1) * LANES)
        q_h = q_ref[:, hs]
        k_h = k_ref[:, hs]
        v_h = v_ref[:, 2 * h * DV_B:2 * (h + 1) * DV_B]
        zero = jnp.zeros_like(q_h)
        o = (_attend(jnp.where(lane < DH_B, q_h, zero), k_h, v_h)
             - lam * _attend(jnp.where(lane >= DH_B, q_h, zero), k_h, v_h))
        o_ref[:, hs] = (_rms(o, gsub_ref[...]) * (1.0 - lam_init)).astype(BF16)


def _diffattn_kernel(q_ref, k_ref, v_ref, lq1, lk1, lq2, lk2, gsub, o_ref, *, lam_init):
    _diffattn_body(q_ref, k_ref, v_ref, lq1, lk1, lq2, lk2, gsub, o_ref, lam_init)


def _diffattn(qb, kb, vb, k_lat, v_lat, lam_params, gsub, lam_init):
    small = [pl.BlockSpec((1, DH_B), lambda *_: (0, 0))] * 4 + [pl.BlockSpec((1, DV_B), lambda *_: (0, 0))]
    body = functools.partial(_diffattn_kernel, lam_init=lam_init)
    ctx = pl.pallas_call(
        body,
        grid=(NCB,),
        in_specs=[_row_spec(B_QK), _row_spec(B_QK), _row_spec(2 * B_V)] + small,
        out_specs=_row_spec(B_V),
        out_shape=jax.ShapeDtypeStruct((N_CTX, B_V), BF16),
        compiler_params=_cparams(("arbitrary",)),
        name="diffattn_ctx",
    )(qb, kb, vb, *lam_params, gsub)
    nk = DEC_SEQ + PAST_LEN
    nqb = DEC_SEQ // TQ_LAT
    off = N_CTX // TQ_LAT
    lat = pl.pallas_call(
        body,
        grid=(DEC_BATCH, nqb),
        in_specs=[
            pl.BlockSpec((TQ_LAT, B_QK), lambda b, j: (off + b * nqb + j, 0)),
            pl.BlockSpec((None, nk, B_QK), lambda b, j: (b, 0, 0)),
            pl.BlockSpec((None, nk, 2 * B_V), lambda b, j: (b, 0, 0)),
        ] + small,
        out_specs=pl.BlockSpec((TQ_LAT, B_V), lambda b, j: (b * nqb + j, 0)),
        out_shape=jax.ShapeDtypeStruct((N_LAT, B_V), BF16),
        compiler_params=_cparams(("arbitrary", "arbitrary")),
        name="diffattn_lat",
    )(qb, k_lat, v_lat, *lam_params, gsub)
    return ctx, lat


def _swiglu_act(xb, wg_ref, wu_ref, act_ref):
    for lo in range(0, act_ref.shape[1], MXU_N):
        cs = slice(lo, lo + MXU_N)
        act_ref[:, cs] = (_silu(_dot(xb, wg_ref[:, cs])) * _dot(xb, wu_ref[:, cs])).astype(BF16)


def _mix_ffn_even_kernel(of_ref, ob_ref, ra_ref, actx_ref, alat_ref, xc_ref, xl_ref, g1_ref, ggla_ref, wo_ref,
                         gam_ref, sh_ref, sc_ref, g2_ref, wg_ref, wu_ref, wd_ref, o_ref, act_ref):
    heads = []
    for h in range(H_A):
        hs = slice(h * DV_A, (h + 1) * DV_A)
        heads.append((_rms(of_ref[:, hs] + ob_ref[:, hs], ggla_ref[...]) * _silu(ra_ref[:, hs])).astype(BF16))
    mix = jnp.concatenate(heads + [_pick_rows(actx_ref, alat_ref)], axis=1)
    x = _pick_rows(xc_ref, xl_ref) + g1_ref[...] * _dot(mix, wo_ref[...])
    hb = _norm_mod(x, gam_ref[...], sc_ref[...], sh_ref[...]).astype(BF16)
    _swiglu_act(hb, wg_ref, wu_ref, act_ref)
    o_ref[...] = x + g2_ref[...] * _dot(act_ref[...], wd_ref[...])


def _mix_ffn_even(o_f, o_b, a_proj, attn_ctx, attn_lat, x_ctx, x_lat, mod, g_gla, w_out, gamma, w_gate, w_up, w_down):
    return pl.pallas_call(
        _mix_ffn_even_kernel,
        grid=(NB,),
        in_specs=[_pair_spec(A_V), _pair_spec(A_V), _pair_spec(A_V, 2), _ctx_spec(B_V), _lat_spec(B_V),
                  _ctx_spec(D_MODEL), _lat_spec(D_MODEL), _mod_spec(2), _const_spec((1, DV_A)),
                  _const_spec((A_V + B_V, D_MODEL)),
                  _const_spec((1, D_MODEL)), _mod_spec(3), _mod_spec(4), _mod_spec(5),
                  _const_spec((D_MODEL, D_FF)), _const_spec((D_MODEL, D_FF)), _const_spec((D_FF, D_MODEL))],
        out_specs=_row_spec(D_MODEL),
        out_shape=jax.ShapeDtypeStruct((M_TOK, D_MODEL), F32),
        scratch_shapes=[pltpu.VMEM((TM, D_FF), BF16)],
        compiler_params=_cparams(("arbitrary",)),
        name="mix_ffn_even",
    )(o_f, o_b, a_proj, attn_ctx, attn_lat, x_ctx, x_lat, mod, g_gla, w_out, gamma, mod, mod, mod,
      w_gate, w_up, w_down)


def _inproj_odd_kernel(x_ref, gam_ref, sh_ref, sc_ref, w_ref, gq_ref, gk_ref, cos_ref, sin_ref,
                       q_ref, k_ref, v_ref, kctx_ref, vctx_ref, stage_ref):
    is_lat = pl.program_id(0) >= NCB
    hb = _norm_mod(x_ref[...], gam_ref[...], sc_ref[...], sh_ref[...]).astype(BF16)
    cos = jnp.where(is_lat, cos_ref[...], 1.0)
    sin = jnp.where(is_lat, sin_ref[...], 0.0)
    scale = DH_C ** -0.5 * LOG2E
    nq, nkv = H_C * DH_C, HKV_C * DH_C
    for lo in range(0, nq, MXU_N):
        z = _dot(hb, w_ref[:, lo:lo + MXU_N])
        for j in range(0, MXU_N, DH_C):
            q_ref[:, lo + j:lo + j + DH_C] = (
                _rope(_rms(z[:, j:j + DH_C], gq_ref[...]), cos, sin, DH_C // 4) * scale).astype(BF16)
    z = _dot(hb, w_ref[:, nq:nq + nkv])
    for h in range(HKV_C):
        hs = slice(h * DH_C, (h + 1) * DH_C)
        k_n = _rms(z[:, hs], gk_ref[...])
        stage_ref[:, hs] = k_n
        k_ref[:, hs] = _rope(k_n, cos, sin, DH_C // 4).astype(BF16)
    v = _dot(hb, w_ref[:, nq + nkv:])
    stage_ref[:, nkv:] = v
    for h in range(HKV_C):
        v_ref[:, 2 * h * DH_C:(2 * h + 1) * DH_C] = v[:, h * DH_C:(h + 1) * DH_C].astype(BF16)
        v_ref[:, (2 * h + 1) * DH_C:2 * (h + 1) * DH_C] = jnp.ones((TM, DH_C), BF16)
    _copy_ctx(stage_ref, ((0, DH_C, HKV_C, kctx_ref), (nkv, DH_C, HKV_C, vctx_ref)))


def _inproj_odd(x, gamma, mod, w_in, g_q, g_k, cos, sin):
    nq, nkv = H_C * DH_C, HKV_C * DH_C
    return pl.pallas_call(
        _inproj_odd_kernel,
        grid=(NB,),
        in_specs=[
            _row_spec(D_MODEL), _const_spec((1, D_MODEL)), _mod_spec(0), _mod_spec(1),
            _const_spec((D_MODEL, nq + 2 * nkv)), _const_spec((1, DH_C)), _const_spec((1, DH_C)),
            pl.BlockSpec((TM, LANES), lambda i: (_pos_block(i), 0)),
            pl.BlockSpec((TM, LANES), lambda i: (_pos_block(i), 0)),
        ],
        out_specs=[_row_spec(nq), _row_spec(nkv), _row_spec(2 * nkv), _ctx_parts_spec(HKV_C),
                   _ctx_parts_spec(HKV_C)],
        out_shape=[
            jax.ShapeDtypeStruct((M_TOK, nq), BF16), jax.ShapeDtypeStruct((M_TOK, nkv), BF16),
            jax.ShapeDtypeStruct((M_TOK, 2 * nkv), BF16), jax.ShapeDtypeStruct((N_CTX * HKV_C, LANES), F32),
            jax.ShapeDtypeStruct((N_CTX * HKV_C, LANES), F32),
        ],
        scratch_shapes=[pltpu.VMEM((TM, 2 * nkv), F32)],
        compiler_params=_cparams(("arbitrary",)),
        name="inproj_odd",
    )(x, gamma, mod, mod, w_in, g_q, g_k, cos, sin)


def _gqa_body(q_ref, k_ref, v_ref, o_ref):
    rep = H_C // HKV_C
    tq = q_ref.shape[0]
    for hk in range(HKV_C):
        ks = slice(hk * DH_C, (hk + 1) * DH_C)
        q_g = jnp.concatenate([q_ref[:, (hk * rep + g) * DH_C:(hk * rep + g + 1) * DH_C] for g in range(rep)],
                              axis=0)
        o = _attend(q_g, k_ref[:, ks], v_ref[:, 2 * hk * DH_C:2 * (hk + 1) * DH_C])
        for g in range(rep):
            o_ref[:, (hk * rep + g) * DH_C:(hk * rep + g + 1) * DH_C] = o[g * tq:(g + 1) * tq].astype(BF16)


def _gqa(q, k, v, k_lat, v_lat):
    nq, nkv = H_C * DH_C, HKV_C * DH_C
    ctx = pl.pallas_call(
        _gqa_body,
        grid=(NCB,),
        in_specs=[_row_spec(nq), _row_spec(nkv), _row_spec(2 * nkv)],
        out_specs=_row_spec(nq),
        out_shape=jax.ShapeDtypeStruct((N_CTX, nq), BF16),
        compiler_params=_cparams(("arbitrary",)),
        name="gqa_ctx",
    )(q, k, v)
    nk = DEC_SEQ + PAST_LEN
    nqb = DEC_SEQ // TQ_LAT
    off = N_CTX // TQ_LAT
    lat = pl.pallas_call(
        _gqa_body,
        grid=(DEC_BATCH, nqb),
        in_specs=[
            pl.BlockSpec((TQ_LAT, nq), lambda b, j: (off + b * nqb + j, 0)),
            pl.BlockSpec((None, nk, nkv), lambda b, j: (b, 0, 0)),
            pl.BlockSpec((None, nk, 2 * nkv), lambda b, j: (b, 0, 0)),
        ],
        out_specs=pl.BlockSpec((TQ_LAT, nq), lambda b, j: (b * nqb + j, 0)),
        out_shape=jax.ShapeDtypeStruct((N_LAT, nq), BF16),
        compiler_params=_cparams(("arbitrary", "arbitrary")),
        name="gqa_lat",
    )(q, k_lat, v_lat)
    return ctx, lat


def _postmix_odd_kernel(actx_ref, alat_ref, x_ref, g1_ref, wo_ref, gam_ref, sh_ref, sc_ref, wr_ref,
                        x1_ref, h_ref, w1_ref, w2_ref, idx_ref, cnt_ref, run_ref):
    @pl.when(pl.program_id(0) == 0)
    def _():
        run_ref[...] = jnp.zeros_like(run_ref)

    x1 = x_ref[...] + g1_ref[...] * _dot(_pick_rows(actx_ref, alat_ref), wo_ref[...])
    x1_ref[...] = x1
    h = _norm_mod(x1, gam_ref[...], sc_ref[...], sh_ref[...])
    h_ref[...] = h
    logits = _dot_hi(h, wr_ref[...])
    lane = lax.broadcasted_iota(jnp.int32, logits.shape, 1)
    lane_f = lane.astype(F32)
    lg = jnp.where(lane < N_EXPERTS, logits, -jnp.inf)
    m1 = jnp.max(lg, axis=-1, keepdims=True)
    i1 = jnp.min(jnp.where(lg == m1, lane_f, float(LANES)), axis=-1, keepdims=True)
    lg2 = jnp.where(lane_f == i1, -jnp.inf, lg)
    m2 = jnp.max(lg2, axis=-1, keepdims=True)
    i2 = jnp.min(jnp.where(lg2 == m2, lane_f, float(LANES)), axis=-1, keepdims=True)
    e = jnp.exp(m2 - m1)
    w1 = 1.0 / (1.0 + e)
    w1_ref[...] = jnp.broadcast_to(w1, logits.shape)
    w2_ref[...] = jnp.broadcast_to(e * w1, logits.shape)
    oh1 = jnp.where(lane_f == i1, 1.0, 0.0)
    oh2 = jnp.where(lane_f == i2, 1.0, 0.0)
    both = oh1 + oh2
    r_i = lax.broadcasted_iota(jnp.int32, (TM, TM), 0)
    c_i = lax.broadcasted_iota(jnp.int32, (TM, TM), 1)
    before = jnp.where(c_i < r_i, 1.0, 0.0).astype(BF16)
    pref = _dot(before, both.astype(BF16)) + run_ref[0:1, :]
    rank1 = jnp.sum(pref * oh1, axis=-1, keepdims=True)
    rank2 = jnp.sum(pref * oh2, axis=-1, keepdims=True)
    run_ref[...] = run_ref[...] + jnp.sum(both, axis=0, keepdims=True)
    cnt_ref[...] = run_ref[...]
    quarter = LANES // 4
    idx_ref[...] = jnp.where(lane < quarter, i1, jnp.where(lane < 2 * quarter, i2, jnp.where(
        lane < 3 * quarter, rank1, rank2))).astype(jnp.int32)


def _postmix_odd(attn_ctx, attn_lat, x, mod, w_out, gamma, w_router):
    return pl.pallas_call(
        _postmix_odd_kernel,
        grid=(NB,),
        in_specs=[_ctx_spec(H_C * DH_C), _lat_spec(H_C * DH_C), _row_spec(D_MODEL), _mod_spec(2),
                  _const_spec((H_C * DH_C, D_MODEL)),
                  _const_spec((1, D_MODEL)), _mod_spec(3), _mod_spec(4), _const_spec((D_MODEL, LANES))],
        out_specs=[_row_spec(D_MODEL), _row_spec(D_MODEL), _row_spec(LANES), _row_spec(LANES), _row_spec(LANES),
                   _const_spec((8, LANES))],
        out_shape=[
            jax.ShapeDtypeStruct((M_TOK, D_MODEL), F32), jax.ShapeDtypeStruct((M_TOK, D_MODEL), F32),
            jax.ShapeDtypeStruct((M_TOK, LANES), F32), jax.ShapeDtypeStruct((M_TOK, LANES), F32),
            jax.ShapeDtypeStruct((M_TOK, LANES), jnp.int32), jax.ShapeDtypeStruct((8, LANES), F32),
        ],
        scratch_shapes=[pltpu.VMEM((8, LANES), F32)],
        compiler_params=_cparams(("arbitrary",)),
        name="postmix_odd",
    )(attn_ctx, attn_lat, x, mod, w_out, gamma, mod, mod, w_router)


def _cast_rows(src_ref, dst_ref, piece):
    def body(r, carry):
        rows = pl.ds(pl.multiple_of(r * piece, piece), piece)
        dst_ref[rows, :] = src_ref[rows, :].astype(BF16)
        return carry

    lax.fori_loop(0, src_ref.shape[0] // piece, body, 0)


def _experts_up_kernel(tile_ref, chunk_ref, exp_ref, first_ref, valid_ref, x_ref, wg_ref, wu_ref, act_ref,
                       wgb_ref, wub_ref):
    s = pl.program_id(0)

    @pl.when(first_ref[s] == 1)
    def _():
        _cast_rows(wg_ref, wgb_ref, 128)
        _cast_rows(wu_ref, wub_ref, 128)

    @pl.when(valid_ref[s] == 1)
    def _():
        _swiglu_act(x_ref[...].astype(BF16), wgb_ref, wub_ref, act_ref)


def _experts_down_kernel(act_tile_ref, exp_ref, first_ref, valid_ref, act_ref, wd_ref, o_ref, wdb_ref):
    i = pl.program_id(0)

    @pl.when(first_ref[i] == 1)
    def _():
        _cast_rows(wd_ref, wdb_ref, 256)

    @pl.when(valid_ref[i] == 1)
    def _():
        o_ref[...] = _dot(act_ref[...], wdb_ref[...])

    @pl.when(valid_ref[i] == 0)
    def _():
        o_ref[...] = jnp.zeros_like(o_ref)


def _experts(up_tables, down_tables, xs, w_gate, w_up, w_down):
    up = pl.pallas_call(
        _experts_up_kernel,
        grid_spec=pltpu.PrefetchScalarGridSpec(
            num_scalar_prefetch=5,
            grid=(NC_E * NT_E,),
            in_specs=[
                pl.BlockSpec((TME, D_MODEL), lambda s, t, c, e, f, v: (t[s], 0)),
                pl.BlockSpec((None, D_MODEL, TF_E), lambda s, t, c, e, f, v: (e[s], 0, c[s])),
                pl.BlockSpec((None, D_MODEL, TF_E), lambda s, t, c, e, f, v: (e[s], 0, c[s])),
            ],
            out_specs=pl.BlockSpec((TME, TF_E), lambda s, t, c, e, f, v: (t[s], c[s])),
            scratch_shapes=[pltpu.VMEM((D_MODEL, TF_E), BF16), pltpu.VMEM((D_MODEL, TF_E), BF16)],
        ),
        out_shape=jax.ShapeDtypeStruct((NT_E * TME, D_FF_E), BF16),
        compiler_params=_cparams(("arbitrary",)),
        name="experts_up",
    )(*up_tables, xs, w_gate, w_up)
    return pl.pallas_call(
        _experts_down_kernel,
        grid_spec=pltpu.PrefetchScalarGridSpec(
            num_scalar_prefetch=4,
            grid=(NT_E,),
            in_specs=[
                pl.BlockSpec((TME, D_FF_E), lambda i, a, e, f, v: (a[i], 0)),
                pl.BlockSpec((None, D_FF_E, D_MODEL), lambda i, a, e, f, v: (e[i], 0, 0)),
            ],
            out_specs=pl.BlockSpec((TME, D_MODEL), lambda i, a, e, f, v: (i, 0)),
            scratch_shapes=[pltpu.VMEM((D_FF_E, D_MODEL), BF16)],
        ),
        out_shape=jax.ShapeDtypeStruct((NT_E * TME, D_MODEL), F32),
        compiler_params=_cparams(("arbitrary",)),
        name="experts_down",
    )(*down_tables, up, w_down)


def _combine_kernel(x_ref, y1_ref, y2_ref, w1_ref, w2_ref, g2_ref, gam_ref, octx_ref, olat_ref):
    rep = D_MODEL // LANES
    w1 = jnp.concatenate([w1_ref[...]] * rep, axis=1)
    w2 = jnp.concatenate([w2_ref[...]] * rep, axis=1)
    x2 = x_ref[...] + g2_ref[...] * (w1 * y1_ref[...] + w2 * y2_ref[...])
    y = _rms(x2, gam_ref[...])
    is_lat = pl.program_id(0) >= NCB

    @pl.when(jnp.logical_not(is_lat))
    def _():
        octx_ref[...] = y

    @pl.when(is_lat)
    def _():
        olat_ref[...] = y


def _combine(x1, y1, y2, w1, w2, mod, gamma):
    return pl.pallas_call(
        _combine_kernel,
        grid=(NB,),
        in_specs=[_row_spec(D_MODEL), _row_spec(D_MODEL), _row_spec(D_MODEL), _row_spec(LANES), _row_spec(LANES),
                  _mod_spec(5), _const_spec((1, D_MODEL))],
        out_specs=[_ctx_spec(D_MODEL), _lat_spec(D_MODEL)],
        out_shape=[jax.ShapeDtypeStruct((N_CTX, D_MODEL), F32), jax.ShapeDtypeStruct((N_LAT, D_MODEL), F32)],
        compiler_params=_cparams(("arbitrary",)),
        name="combine",
    )(x1, y1, y2, w1, w2, mod, gamma)


def _rope_tables(rot_dim):
    n = DEC_SEQ
    rows = np.repeat(np.arange(n // GRID_W, dtype=np.float64), GRID_W)
    cols = np.tile(np.arange(GRID_W, dtype=np.float64), n // GRID_W)
    half = rot_dim // 2
    freqs = ROPE_THETA ** (-np.arange(0, half, 2, dtype=np.float64) / half)
    ang_r, ang_c = rows[:, None] * freqs, cols[:, None] * freqs
    cos = np.concatenate([np.cos(ang_r)] * 2 + [np.cos(ang_c)] * 2, axis=-1)
    sin = np.concatenate([-np.sin(ang_r), np.sin(ang_r), -np.sin(ang_c), np.sin(ang_c)], axis=-1)
    rep = LANES // rot_dim
    return (jnp.asarray(np.tile(cos, (1, rep)).astype(np.float32)),
            jnp.asarray(np.tile(sin, (1, rep)).astype(np.float32)))


def _with_ones(v):
    return jnp.concatenate([v, jnp.ones_like(v)], axis=-1)


def _rows(a, idx):
    return a.at[idx].get(mode="promise_in_bounds")


def _route(idx, counts_f):
    quarter = LANES // 4
    experts = jnp.arange(N_EXPERTS, dtype=jnp.int32)
    e2 = jnp.stack([idx[:, 0], idx[:, quarter]], axis=1)
    rank2 = jnp.stack([idx[:, 2 * quarter], idx[:, 3 * quarter]], axis=1)
    counts = counts_f[0, :N_EXPERTS].astype(jnp.int32)
    tiles = (counts + TME - 1) // TME
    tile_end = jnp.cumsum(tiles)
    tile_start = tile_end - tiles
    start = jnp.cumsum(counts) - counts
    onehot = e2[:, :, None] == experts[None, None, :]
    pos = jnp.sum(jnp.where(onehot, tile_start[None, None, :] * TME, 0), axis=-1) + rank2
    tile_id = jnp.arange(NT_E, dtype=jnp.int32)
    tile_expert = jnp.minimum(jnp.sum((tile_id[:, None] >= tile_end[None, :]).astype(jnp.int32), axis=1),
                              N_EXPERTS - 1).astype(jnp.int32)
    tile_valid = (tile_id < tile_end[-1]).astype(jnp.int32)
    t_oh = tile_expert[:, None] == experts[None, :]
    t_first = jnp.sum(jnp.where(t_oh, (start - tile_start * TME)[None, :], 0), axis=1) + tile_id * TME
    t_last = jnp.sum(jnp.where(t_oh, (start + counts)[None, :], 0), axis=1)
    order = jnp.argsort(e2.reshape(-1), stable=True).astype(jnp.int32)
    n_assign = order.shape[0]
    g_idx = t_first[:, None] + jnp.arange(TME, dtype=jnp.int32)[None, :]
    live = jnp.logical_and(g_idx < t_last[:, None], tile_valid[:, None] == 1)
    src = _rows(order, jnp.clip(g_idx, 0, n_assign - 1).reshape(-1)) // 2
    src_tok = jnp.where(live.reshape(-1), src, 0).astype(jnp.int32)

    n_tiles = tile_end[-1]
    last_tile = jnp.maximum(n_tiles - 1, 0)
    last_expert = jnp.sum(jnp.where(tile_id == last_tile, tile_expert, 0))
    t_tstart = jnp.sum(jnp.where(t_oh, tile_start[None, :], 0), axis=1)
    down_tables = (
        jnp.where(tile_valid == 1, tile_id, last_tile).astype(jnp.int32),
        tile_expert,
        jnp.logical_and(tile_valid == 1, tile_id == t_tstart).astype(jnp.int32),
        tile_valid,
    )
    s_id = jnp.arange(NC_E * NT_E, dtype=jnp.int32)
    s_exp = jnp.minimum(jnp.sum((s_id[:, None] >= NC_E * tile_end[None, :]).astype(jnp.int32), axis=1),
                        N_EXPERTS - 1)
    s_oh = s_exp[:, None] == experts[None, :]
    s_tstart = jnp.sum(jnp.where(s_oh, tile_start[None, :], 0), axis=1)
    s_ntile = jnp.maximum(jnp.sum(jnp.where(s_oh, tiles[None, :], 0), axis=1), 1)
    rel = s_id - NC_E * s_tstart
    s_valid = s_id < NC_E * n_tiles
    up_tables = (
        jnp.where(s_valid, s_tstart + rel % s_ntile, last_tile).astype(jnp.int32),
        jnp.where(s_valid, rel // s_ntile, NC_E - 1).astype(jnp.int32),
        jnp.where(s_valid, s_exp, last_expert).astype(jnp.int32),
        jnp.logical_and(s_valid, rel % s_ntile == 0).astype(jnp.int32),
        s_valid.astype(jnp.int32),
    )
    return src_tok, pos, up_tables, down_tables


def lambda_init(layer):
    return 0.8 - 0.6 * math.exp(-0.3 * layer)


def kernel(x_prompt, x_sample, state_a, cache_b_k, cache_b_v, cache_c_k, cache_c_v, c, c_ctx, w_mod, b_mod, norm_mix, norm_ffn, w_in_even, w_gate2_a, b_gate_a, g_gla, lam_q1, lam_k1, lam_q2, lam_k2, g_sub_b, w_out_even, w_in_odd, g_q_c, g_k_c, w_out_odd, ffn_gate, ffn_up, ffn_down, w_router, exp_gate, exp_up, exp_down, norm_final):
    x_ctx, x_lat = x_prompt.reshape(N_CTX, D_MODEL), x_sample.reshape(N_LAT, D_MODEL)
    cond =jnp.concatenate([c_ctx[None, :], c, jnp.zeros((N_COND - 1 - DEC_BATCH, D_MODEL), F32)], axis=0)
    mod = _modulation(cond, w_mod, b_mod).reshape(DEPTH, N_COND, 1, 6 * D_MODEL)

    w = w_in_even[0]
    na = 2 * A_QK + 2 * A_V
    gate_lo = na
    gate_hi = na + 2 * GATE_RANK
    col_scale = jnp.concatenate([jnp.full((A_QK,), DK_A ** -0.5, F32), jnp.ones((na - A_QK,), F32),
                                 jnp.full((B_QK,), DH_B ** -0.5, F32), jnp.ones((B_QK + B_V,), F32)])
    w_main = (jnp.concatenate([w[:, :gate_lo], w[:, gate_hi:]], axis=1) * col_scale).astype(BF16)
    w_gates = jnp.pad(w[:, gate_lo:gate_hi], ((0, 0), (0, LANES - 2 * GATE_RANK))).astype(BF16)
    g2 = jnp.zeros((LANES, 2 * A_QK), F32)
    g2 = g2.at[:GATE_RANK, :A_QK].set(w_gate2_a[0, 0]).at[GATE_RANK:2 * GATE_RANK, A_QK:].set(w_gate2_a[0, 1])
    bg = b_gate_a[0].reshape(1, 2 * A_QK)
    cos_b, sin_b = _rope_tables(DH_B)
    a_proj, glog, qb, kb, vb, kb_ctx, vb_ctx = _inproj_even(
        x_ctx, x_lat, norm_mix[0][None, :], mod[0], w_main, w_gates, g2.astype(BF16), bg, cos_b, sin_b)

    s0_all = jnp.concatenate([jnp.zeros((1, 2, A_QK, DV_A), F32),
                              state_a[:, 0].reshape(DEC_BATCH, 2, A_QK, DV_A)], axis=0)
    sel = np.arange(A_QK)[:, None] // DK_A == np.arange(LANES)[None, :] // GLA_SB
    m1 = jnp.asarray(sel.astype(np.float32)).astype(BF16)
    o_f, o_b, s_fin = _gla(a_proj, glog, s0_all, m1)

    k_lat = jnp.concatenate([kb[N_CTX:].reshape(DEC_BATCH, DEC_SEQ, B_QK),
                             cache_b_k[:, 0].reshape(DEC_BATCH, PAST_LEN, B_QK).astype(BF16)], axis=1)
    v_lat = jnp.concatenate([vb[N_CTX:].reshape(DEC_BATCH, DEC_SEQ, 2 * B_V),
                             _with_ones(cache_b_v[:, 0].astype(BF16)).reshape(DEC_BATCH, PAST_LEN, 2 * B_V)], axis=1)
    lam_params = [p[0][None, :] for p in (lam_q1, lam_k1, lam_q2, lam_k2)]
    attn_b = _diffattn(qb, kb, vb, k_lat, v_lat, lam_params, g_sub_b[0][None, :], lambda_init(0))

    x = _mix_ffn_even(o_f, o_b, a_proj, *attn_b, x_ctx, x_lat, mod[0], g_gla[0][None, :],
                      w_out_even[0].astype(BF16), norm_ffn[0][None, :], ffn_gate[0].astype(BF16),
                      ffn_up[0].astype(BF16), ffn_down[0].astype(BF16))

    cos_c, sin_c = _rope_tables(DH_C)
    q_c, k_c, v_c, kc_ctx, vc_ctx = _inproj_odd(x, norm_mix[1][None, :], mod[1], w_in_odd[0].astype(BF16),
                                     g_q_c[0][None, :], g_k_c[0][None, :], cos_c, sin_c)
    nkv = HKV_C * DH_C
    k_lat = jnp.concatenate([k_c[N_CTX:].reshape(DEC_BATCH, DEC_SEQ, nkv),
                             cache_c_k[:, 0].reshape(DEC_BATCH, PAST_LEN, nkv).astype(BF16)], axis=1)
    v_lat = jnp.concatenate([v_c[N_CTX:].reshape(DEC_BATCH, DEC_SEQ, 2 * nkv),
                             _with_ones(cache_c_v[:, 0].astype(BF16)).reshape(DEC_BATCH, PAST_LEN, 2 * nkv)], axis=1)
    attn_c = _gqa(q_c, k_c, v_c, k_lat, v_lat)
    w_r = jnp.pad(w_router[0], ((0, 0), (0, LANES - N_EXPERTS)))
    x1, h_moe, w1, w2, idx, counts = _postmix_odd(*attn_c, x, mod[1], w_out_odd[0].astype(BF16),
                                                  norm_ffn[1][None, :], w_r)

    src_tok, pos, up_tables, down_tables = _route(idx, counts)
    xs = _rows(h_moe, src_tok)
    ys = _experts(up_tables, down_tables, xs, exp_gate[0], exp_up[0], exp_down[0])
    y1 = _rows(ys, pos[:, 0])
    y2 = _rows(ys, pos[:, 1])
    y_ctx, y_lat = _combine(x1, y1, y2, w1, w2, mod[1], norm_final[None, :])

    y_prompt = y_ctx.reshape(BATCH, SEQ, D_MODEL)
    y_sample = y_lat.reshape(DEC_BATCH, DEC_SEQ, D_MODEL)
    new_state_a = s_fin.reshape(BATCH, 1, 2, H_A, DK_A, DV_A)
    new_b_k = jnp.transpose(kb_ctx.reshape(BATCH, 1, H_B, 2, DH_B, SEQ), (0, 1, 5, 2, 3, 4))
    new_b_v = vb_ctx.reshape(BATCH, 1, SEQ, H_B, DV_B)
    new_c_k = kc_ctx.reshape(BATCH, 1, SEQ, HKV_C, DH_C)
    new_c_v = vc_ctx.reshape(BATCH, 1, SEQ, HKV_C, DH_C)
    return (y_prompt, y_sample, new_state_a, new_b_k, new_b_v, new_c_k, new_c_v)
```

```python
import functools
import math

import numpy as np
import jax
import jax.numpy as jnp
from jax import lax
from jax.experimental import pallas as pl
from jax.experimental.pallas import tpu as pltpu

D_MODEL = 1024
BATCH = 32
SEQ = 256
DEPTH = 2
DEC_BATCH = 2
DEC_SEQ = 2048
PAST_LEN = 512
GRID_W = 64
H_A, DK_A, DV_A = 4, 64, 128
GATE_RANK = 16
GATE_TAU = 16.0
H_B, DH_B = 4, 64
DV_B = 2 * DH_B
H_C, HKV_C, DH_C = 8, 2, 128
D_FF = 2816
N_EXPERTS = 8
D_FF_E = 3584
ROPE_THETA = 10000.0
EPS = 1e-6
A_QK = H_A * DK_A
A_V = H_A * DV_A
B_QK = H_B * 2 * DH_B
B_V = H_B * DV_B

F32 = jnp.float32
BF16 = jnp.bfloat16
LOG2E = math.log2(math.e)

V7X_VMEM_BYTES = 64 * 1024 * 1024
VMEM_LIMIT = V7X_VMEM_BYTES * 7 // 8
LANES = 128
MXU_N = 256
TM = 256
N_CTX = BATCH * SEQ
N_LAT = DEC_BATCH * DEC_SEQ
M_TOK = N_CTX + N_LAT
NB = M_TOK // TM
NCB = N_CTX // TM
LBB = DEC_SEQ // TM
N_COND = 8
GLA_C = 64
GLA_SB = 8
TME = 512
NT_E = 2 * M_TOK // TME + N_EXPERTS
NC_E = 2
TF_E = D_FF_E // NC_E
TQ_LAT = 256
TMB = 256

assert SEQ == TM and DEC_SEQ % TM == 0 and DEPTH == 2


def _cparams(sem):
    return pltpu.CompilerParams(dimension_semantics=sem, vmem_limit_bytes=VMEM_LIMIT)


def _group(i, tm=TM):
    return jnp.where(i < N_CTX // tm, 0, 1 + (i - N_CTX // tm) // (DEC_SEQ // tm))


def _pos_block(i, tm=TM):
    return jnp.where(i < N_CTX // tm, 0, (i - N_CTX // tm) % (DEC_SEQ // tm))


def _mod_spec(chunk, tm=TM):
    return pl.BlockSpec((None, 1, D_MODEL), lambda i, c=chunk: (_group(i, tm), 0, c))


def _row_spec(width, col=0, tm=TM):
    return pl.BlockSpec((tm, width), lambda i, c=col: (i, c))


def _ctx_spec(width, tm=TM):
    return pl.BlockSpec((tm, width), lambda i: (jnp.minimum(i, N_CTX // tm - 1), 0))


def _ctx_parts_spec(parts, tm=TM):
    return pl.BlockSpec((tm * parts, LANES), lambda i: (jnp.minimum(i, N_CTX // tm - 1), 0))


def _lat_spec(width, tm=TM):
    return pl.BlockSpec((tm, width), lambda i: (jnp.maximum(i - N_CTX // tm, 0), 0))


def _const_spec(shape):
    nd = len(shape)
    return pl.BlockSpec(shape, lambda i, nd=nd: (0,) * nd, pipeline_mode=pl.Buffered(1))


def _dot(a, b):
    return jnp.dot(a, b, preferred_element_type=F32)


def _dot_nt(a, b):
    return lax.dot_general(a, b, (((1,), (1,)), ((), ())), preferred_element_type=F32)


def _split2(a):
    hi = a.astype(BF16)
    lo = (a - hi.astype(F32)).astype(BF16)
    return hi, lo


def _dot_hi(a, b):
    a_hi, a_lo = _split2(a)
    b_hi, b_lo = _split2(b)
    return _dot(a_hi, b_hi) + (_dot(a_hi, b_lo) + _dot(a_lo, b_hi))


def _silu(x):
    return (0.5 * x) * (1.0 + jnp.tanh(0.5 * x))


def _log_sigmoid(x):
    return jnp.minimum(x, 0.0) - jnp.log(1.0 + jnp.exp(-jnp.abs(x)))


def _rms(x, g):
    return x * lax.rsqrt(jnp.mean(x * x, axis=-1, keepdims=True) + EPS) * g


def _norm_mod(x, gamma, sc, sh):
    return _rms(x, gamma) * (1.0 + sc) + sh


def _copy_ctx_transposed(stage3_ref, dst_ref):
    @pl.when(pl.program_id(0) < NCB)
    def _():
        def body(j, carry):
            dst_ref[pl.ds(pl.multiple_of(j * LANES, LANES), LANES), :] = jnp.transpose(stage3_ref[j])
            return carry

        lax.fori_loop(0, stage3_ref.shape[0], body, 0)


def _is_lat(tm):
    return pl.program_id(0) >= N_CTX // tm


def _pick_rows(ctx_ref, lat_ref):
    return jnp.where(_is_lat(ctx_ref.shape[0]), lat_ref[...], ctx_ref[...])


def _copy_ctx(stage_ref, outs):
    piece = 64

    @pl.when(jnp.logical_not(_is_lat(stage_ref.shape[0])))
    def _():
        def body(r, carry):
            rows = pl.ds(pl.multiple_of(r * piece, piece), piece)
            for lo, width, parts, dst_ref in outs:
                for p in range(parts):
                    val = stage_ref[rows, lo + p * width:lo + (p + 1) * width]
                    if width < LANES:
                        val = jnp.concatenate([val, jnp.zeros((piece, LANES - width), F32)], axis=1)
                    dst_ref[pl.ds(pl.multiple_of(r * piece * parts, piece * parts) + p, piece, stride=parts), :] = val
            return carry

        lax.fori_loop(0, stage_ref.shape[0] // piece, body, 0)


def _rope(x, cos, sin, half):
    lane = lax.broadcasted_iota(jnp.int32, x.shape, 1)
    first = (lane % (2 * half)) < half
    swapped = jnp.where(first, pltpu.roll(x, LANES - half, 1), pltpu.roll(x, half, 1))
    return x * cos + swapped * sin


def _mod_kernel(cond_ref, w_ref, b_ref, o_ref):
    o_ref[...] = _dot_hi(_silu(cond_ref[...]), w_ref[...]) + b_ref[...]


def _modulation(cond, w_mod, b_mod):
    tn = 1536
    return pl.pallas_call(
        _mod_kernel,
        grid=(DEPTH, 6 * D_MODEL // tn),
        in_specs=[
            pl.BlockSpec((N_COND, D_MODEL), lambda l, j: (0, 0)),
            pl.BlockSpec((None, D_MODEL, tn), lambda l, j: (l, 0, j)),
            pl.BlockSpec((None, 1, tn), lambda l, j: (l, 0, j)),
        ],
        out_specs=pl.BlockSpec((None, N_COND, tn), lambda l, j: (l, 0, j)),
        out_shape=jax.ShapeDtypeStruct((DEPTH, N_COND, 6 * D_MODEL), F32),
        compiler_params=_cparams(("arbitrary", "arbitrary")),
        name="modulation",
    )(cond, w_mod, b_mod.reshape(DEPTH, 1, 6 * D_MODEL))


def _inproj_even_kernel(xc_ref, xl_ref, gam_ref, sh_ref, sc_ref, w_ref, wg_ref, g2_ref, bg_ref, cos_ref, sin_ref,
                        a_ref, glog_ref, qb_ref, kb_ref, vb_ref, kctx_ref, vctx_ref, stage_ref, stagek_ref):
    is_lat = pl.program_id(0) >= NCB
    hb = _norm_mod(_pick_rows(xc_ref, xl_ref), gam_ref[...], sc_ref[...], sh_ref[...]).astype(BF16)
    cos = jnp.where(is_lat, cos_ref[...], 1.0)
    sin = jnp.where(is_lat, sin_ref[...], 0.0)
    na = 2 * A_QK + 2 * A_V
    for lo in range(0, na, MXU_N):
        a_ref[:, lo:lo + MXU_N] = _dot(hb, w_ref[:, lo:lo + MXU_N])
    for lo in range(0, B_QK, MXU_N):
        zq = _dot(hb, w_ref[:, na + lo:na + lo + MXU_N])
        zk = _dot(hb, w_ref[:, na + B_QK + lo:na + B_QK + lo + MXU_N])
        zv = _dot(hb, w_ref[:, na + 2 * B_QK + lo:na + 2 * B_QK + lo + MXU_N])
        stage_ref[:, lo:lo + MXU_N] = zv
        for j in range(0, MXU_N, LANES):
            cs = slice(lo + j, lo + j + LANES)
            vb_ref[:, 2 * (lo + j):2 * (lo + j) + DV_B] = zv[:, j:j + LANES].astype(BF16)
            vb_ref[:, 2 * (lo + j) + DV_B:2 * (lo + j + LANES)] = jnp.ones((TM, DV_B), BF16)
            stagek_ref[(lo + j) // LANES] = zk[:, j:j + LANES]
            qb_ref[:, cs] = (_rope(zq[:, j:j + LANES], cos, sin, DH_B // 4) * LOG2E).astype(BF16)
            kb_ref[:, cs] = _rope(zk[:, j:j + LANES], cos, sin, DH_B // 4).astype(BF16)
    gates = _dot(hb, wg_ref[...])
    xg = _dot(gates.astype(BF16), g2_ref[...]) + bg_ref[...]
    glog_ref[...] = _log_sigmoid(xg) * (1.0 / GATE_TAU)
    _copy_ctx(stage_ref, ((0, DV_B, H_B, vctx_ref),))
    _copy_ctx_transposed(stagek_ref, kctx_ref)


def _inproj_even(x_ctx, x_lat, gamma, mod, w_main, w_gates, g2, bg, cos, sin):
    na = 2 * A_QK + 2 * A_V
    nz = na + 2 * B_QK + B_V
    return pl.pallas_call(
        _inproj_even_kernel,
        grid=(NB,),
        in_specs=[
            _ctx_spec(D_MODEL), _lat_spec(D_MODEL), _const_spec((1, D_MODEL)), _mod_spec(0), _mod_spec(1),
            _const_spec((D_MODEL, nz)), _const_spec((D_MODEL, LANES)), _const_spec((LANES, 2 * A_QK)),
            _const_spec((1, 2 * A_QK)),
            pl.BlockSpec((TM, LANES), lambda i: (_pos_block(i), 0)),
            pl.BlockSpec((TM, LANES), lambda i: (_pos_block(i), 0)),
        ],
        out_specs=[_pair_spec(na), _pair_spec(2 * A_QK), _row_spec(B_QK), _row_spec(B_QK), _row_spec(2 * B_V),
                   pl.BlockSpec((None, B_QK, TM), lambda i: (jnp.minimum(i, NCB - 1), 0, 0)), _ctx_parts_spec(H_B)],
        out_shape=[
            jax.ShapeDtypeStruct((M_TOK, na), F32), jax.ShapeDtypeStruct((M_TOK, 2 * A_QK), F32),
            jax.ShapeDtypeStruct((M_TOK, B_QK), BF16), jax.ShapeDtypeStruct((M_TOK, B_QK), BF16),
            jax.ShapeDtypeStruct((M_TOK, 2 * B_V), BF16), jax.ShapeDtypeStruct((BATCH, B_QK, SEQ), F32),
            jax.ShapeDtypeStruct((N_CTX * H_B, LANES), F32),
        ],
        scratch_shapes=[pltpu.VMEM((TM, B_V), F32), pltpu.VMEM((B_QK // LANES, TM, LANES), F32)],
        compiler_params=_cparams(("arbitrary",)),
        name="inproj_even",
    )(x_ctx, x_lat, gamma, mod, mod, w_main, w_gates, g2, bg, cos, sin)


def _gla_chunk(q_ref, k_ref, v_ref, g_ref, o_ref, c, s_ref, a_ref, b_ref, m1_ref, head_mask, same_block, rev):
    q, k, v = q_ref[c], k_ref[c], v_ref[c]
    g = g_ref[c] * LOG2E
    v_b = v.astype(BF16)
    r_i = lax.broadcasted_iota(jnp.int32, (GLA_C, GLA_C), 0)
    c_i = lax.broadcasted_iota(jnp.int32, (GLA_C, GLA_C), 1)
    tri = jnp.where((c_i >= r_i) if rev else (c_i <= r_i), 1.0, 0.0).astype(BF16)
    g_hi = g.astype(BF16)
    g_r = g - g_hi.astype(F32)
    g_mid = g_r.astype(BF16)
    g_lo = (g_r - g_mid.astype(F32)).astype(BF16)
    bc = _dot(tri, g_hi) + (_dot(tri, g_mid) + _dot(tri, g_lo))
    b_ref[...] = bc
    yield
    t_i = lax.broadcasted_iota(jnp.int32, (GLA_SB, A_QK), 0)
    group = 2 * GLA_SB
    for i0 in range(0, GLA_C, group):
        blocks = range(i0, i0 + group, GLA_SB)
        q_i = [q_ref[c, i:i + GLA_SB, :] for i in blocks]
        b_i = [b_ref[i:i + GLA_SB, :] for i in blocks]
        for j in range(GLA_SB):
            keep = (t_i <= j) if rev else (t_i >= j)
            pair = []
            for n, i in enumerate(blocks):
                k_s = k_ref[c, i + j:i + j + 1, :]
                b_s = b_ref[i + j:i + j + 1, :]
                pair.append(jnp.where(keep, q_i[n] * k_s * jnp.exp2(b_i[n] - b_s), 0.0))
            a_ref[j * GLA_C + i0:j * GLA_C + i0 + group, :] = jnp.concatenate(pair, axis=0).astype(BF16)
    yield
    last = bc[0:1] if rev else bc[GLA_C - 1:GLA_C]
    qc = q * jnp.exp2(bc)
    kl = k * jnp.exp2(last - bc)
    row = lax.broadcasted_iota(jnp.int32, (GLA_C, A_QK), 0)

    def expand(x):
        return jnp.where(head_mask, jnp.concatenate([x] * H_A, axis=0), 0.0).astype(BF16)

    scores = None
    half = GLA_C // 2
    while half >= GLA_SB:
        mids = [bc[b + half:b + half + 1] if rev else bc[b + half - 1:b + half] for b in range(0, GLA_C, 2 * half)]
        ref = mids[0] if len(mids) == 1 else jnp.concatenate(
            [jnp.broadcast_to(m, (2 * half, A_QK)) for m in mids], axis=0)
        upper = (row // half) % 2 == 1
        later, earlier = (jnp.logical_not(upper), upper) if rev else (upper, jnp.logical_not(upper))
        q_l = jnp.where(later, q * jnp.exp2(bc - ref), 0.0)
        k_e = jnp.where(earlier, k * jnp.exp2(ref - bc), 0.0).astype(BF16)
        term = _dot_nt(expand(q_l), k_e)
        if 2 * half < GLA_C:
            term = jnp.where(same_block[2 * half], term, 0.0)
        scores = term if scores is None else scores + term
        half //= 2
    s_old = s_ref[...]
    inter = _dot(expand(qc), s_old.astype(BF16))
    sums = _dot(a_ref[...], m1_ref[...])
    yield
    lane_c = lax.broadcasted_iota(jnp.int32, (GLA_C, LANES), 1)
    within = None
    for j in range(GLA_SB):
        part = jnp.where(lane_c % GLA_SB == j, sums[j * GLA_C:(j + 1) * GLA_C], 0.0)
        within = part if within is None else within + part
    lane = lax.broadcasted_iota(jnp.int32, (GLA_SB, LANES), 1)
    per_head = []
    for h in range(H_A):
        rows_h = []
        for i in range(0, GLA_C, GLA_SB):
            moved = pltpu.roll(within[i:i + GLA_SB], (i - h * GLA_SB) % LANES, 1)
            rows_h.append(jnp.where(lane // GLA_SB == i // GLA_SB, moved, 0.0))
        per_head.append(jnp.concatenate(rows_h, axis=0)[:, :GLA_C])
    scores = scores + jnp.concatenate(per_head, axis=0)
    o_heads = []
    for h in range(H_A):
        hs = slice(h * GLA_C, (h + 1) * GLA_C)
        o_heads.append(inter[hs] + _dot(scores[hs].astype(BF16), v_b[:, h * DV_A:(h + 1) * DV_A]))
    o_ref[c] = jnp.concatenate(o_heads, axis=1)
    yield
    t = jnp.transpose(jnp.concatenate([kl, jnp.broadcast_to(last, (GLA_C, A_QK))], axis=0))
    kv = _dot(t[:, :GLA_C].astype(BF16), v_b)
    a_col = jnp.exp2(t[:, GLA_C:GLA_C + 1])
    for h in range(H_A):
        hs = slice(h * DK_A, (h + 1) * DK_A)
        s_ref[hs, :] = a_col[hs] * s_old[hs] + kv[hs, h * DV_A:(h + 1) * DV_A]


def _gla_kernel(fblk_ref, bblk_ref, first_ref, inita_ref, initb_ref,
                qf_ref, kf_ref, vf_ref, gf_ref, qr_ref, kr_ref, vr_ref, gr_ref, s0a_ref, s0b_ref, m1_ref,
                of_ref, ob_ref, sfin_ref, s_refs, a_refs, b_refs):
    step = pl.program_id(0)

    @pl.when(first_ref[step] == 1)
    def _():
        for seq, s0_ref in enumerate((s0a_ref, s0b_ref)):
            for d in range(2):
                s_refs[2 * seq + d] = s0_ref[d]

    r_h = lax.broadcasted_iota(jnp.int32, (H_A * GLA_C, A_QK), 0) // GLA_C
    c_h = lax.broadcasted_iota(jnp.int32, (H_A * GLA_C, A_QK), 1) // DK_A
    head_mask = r_h == c_h
    t_s = lax.broadcasted_iota(jnp.int32, (H_A * GLA_C, GLA_C), 0) % GLA_C
    s_s = lax.broadcasted_iota(jnp.int32, (H_A * GLA_C, GLA_C), 1)
    same_block = {}
    size = 2 * GLA_SB
    while size < GLA_C:
        same_block[size] = (t_s // size) == (s_s // size)
        size *= 2
    nch = TM // GLA_C

    def body(c, carry):
        chains = []
        for seq in range(2):
            chains.append(_gla_chunk(qf_ref, kf_ref, vf_ref, gf_ref, of_ref, seq * nch + c, s_refs.at[2 * seq],
                                     a_refs.at[2 * seq], b_refs.at[2 * seq], m1_ref, head_mask, same_block, False))
            chains.append(_gla_chunk(qr_ref, kr_ref, vr_ref, gr_ref, ob_ref, seq * nch + nch - 1 - c,
                                     s_refs.at[2 * seq + 1], a_refs.at[2 * seq + 1], b_refs.at[2 * seq + 1], m1_ref,
                                     head_mask, same_block, True))
        live = chains
        while live:
            live = [ch for ch in live if next(ch, True) is None]
        return carry

    lax.fori_loop(0, nch, body, 0)

    @pl.when(step < BATCH // 2)
    def _():
        for seq in range(2):
            for d in range(2):
                sfin_ref[seq, d] = s_refs[2 * seq + d]


def _pair_block(i):
    j = i - NCB
    return jnp.where(i < NCB, i, NCB + 2 * (j % LBB) + j // LBB)


def _pair_spec(width, col=0):
    return pl.BlockSpec((TM, width), lambda i, c=col: (_pair_block(i), c))


def _gla_tables():
    fblk, bblk, first, init_a, init_b = [], [], [], [], []
    for p in range(BATCH // 2):
        fblk.append(p), bblk.append(p), first.append(1), init_a.append(0), init_b.append(0)
    for j in range(LBB):
        fblk.append(NCB // 2 + j)
        bblk.append(NCB // 2 + LBB - 1 - j)
        first.append(1 if j == 0 else 0)
        init_a.append(1), init_b.append(2)
    return [jnp.asarray(np.array(t, np.int32)) for t in (fblk, bblk, first, init_a, init_b)]


def _gla(a_proj, glog, s0_all, m1):
    tables = _gla_tables()
    nsteps = int(tables[0].shape[0])
    nseq = BATCH + DEC_BATCH

    def fmap(col):
        return lambda i, fb, bb, fi, ia, ib: (fb[i], 0, col)

    def rmap(col):
        return lambda i, fb, bb, fi, ia, ib: (bb[i], 0, col)

    nch = 2 * TM // GLA_C
    a3 = a_proj.reshape(M_TOK // GLA_C, GLA_C, a_proj.shape[1])
    g3 = glog.reshape(M_TOK // GLA_C, GLA_C, glog.shape[1])
    gs = pltpu.PrefetchScalarGridSpec(
        num_scalar_prefetch=5,
        grid=(nsteps,),
        in_specs=[
            pl.BlockSpec((nch, GLA_C, A_QK), fmap(0)), pl.BlockSpec((nch, GLA_C, A_QK), fmap(1)),
            pl.BlockSpec((nch, GLA_C, A_V), fmap(1)), pl.BlockSpec((nch, GLA_C, A_QK), fmap(0)),
            pl.BlockSpec((nch, GLA_C, A_QK), rmap(0)), pl.BlockSpec((nch, GLA_C, A_QK), rmap(1)),
            pl.BlockSpec((nch, GLA_C, A_V), rmap(1)), pl.BlockSpec((nch, GLA_C, A_QK), rmap(1)),
            pl.BlockSpec((None, 2, A_QK, DV_A), lambda i, fb, bb, fi, ia, ib: (ia[i], 0, 0, 0)),
            pl.BlockSpec((None, 2, A_QK, DV_A), lambda i, fb, bb, fi, ia, ib: (ib[i], 0, 0, 0)),
            pl.BlockSpec((A_QK, LANES), lambda i, fb, bb, fi, ia, ib: (0, 0)),
        ],
        out_specs=[
            pl.BlockSpec((nch, GLA_C, A_V), fmap(0)), pl.BlockSpec((nch, GLA_C, A_V), rmap(0)),
            pl.BlockSpec((2, 2, A_QK, DV_A),
                         lambda i, fb, bb, fi, ia, ib: (jnp.minimum(fb[i], BATCH // 2 - 1), 0, 0, 0)),
        ],
        scratch_shapes=[
            pltpu.VMEM((4, A_QK, DV_A), F32),
            pltpu.VMEM((4, GLA_SB * GLA_C, A_QK), BF16),
            pltpu.VMEM((4, GLA_C, A_QK), F32),
        ],
    )
    o_f, o_b, s_fin = pl.pallas_call(
        _gla_kernel,
        grid_spec=gs,
        out_shape=[
            jax.ShapeDtypeStruct((M_TOK // GLA_C, GLA_C, A_V), F32),
            jax.ShapeDtypeStruct((M_TOK // GLA_C, GLA_C, A_V), F32),
            jax.ShapeDtypeStruct((BATCH, 2, A_QK, DV_A), F32),
        ],
        compiler_params=_cparams(("arbitrary",)),
        name="gla",
    )(*tables, a3, a3, a3, g3, a3, a3, a3, g3, s0_all, s0_all, m1)
    return o_f.reshape(M_TOK, A_V), o_b.reshape(M_TOK, A_V), s_fin


def _attend(q_b, k_b, v_ones):
    s2 = _dot_nt(q_b, k_b)
    e = jnp.exp2(s2 - jnp.max(s2, axis=-1, keepdims=True)).astype(BF16)
    both = _dot(e, v_ones)
    dv = v_ones.shape[1] // 2
    return both[:, :dv] * (1.0 / both[:, dv:])


def _diffattn_body(q_ref, k_ref, v_ref, lq1_ref, lk1_ref, lq2_ref, lk2_ref, gsub_ref, o_ref, lam_init):
    lam = (jnp.exp(jnp.sum(lq1_ref[...] * lk1_ref[...], axis=-1, keepdims=True))
           - jnp.exp(jnp.sum(lq2_ref[...] * lk2_ref[...], axis=-1, keepdims=True)) + lam_init)
    lane = lax.broadcasted_iota(jnp.int32, (q_ref.shape[0], LANES), 1)
    for h in range(H_B):
        hs = slice(h * LANES, (h + 1) * LANES)
        q_h = q_ref[:, hs]
        k_h = k_ref[:, hs]
        v_h = v_ref[:, 2 * h * DV_B:2 * (h + 1) * DV_B]
        zero = jnp.zeros_like(q_h)
        o = (_attend(jnp.where(lane < DH_B, q_h, zero), k_h, v_h)
             - lam * _attend(jnp.where(lane >= DH_B, q_h, zero), k_h, v_h))
        o_ref[:, hs] = (_rms(o, gsub_ref[...]) * (1.0 - lam_init)).astype(BF16)


def _diffattn_kernel(q_ref, k_ref, v_ref, lq1, lk1, lq2, lk2, gsub, o_ref, *, lam_init):
    _diffattn_body(q_ref, k_ref, v_ref, lq1, lk1, lq2, lk2, gsub, o_ref, lam_init)


def _diffattn(qb, kb, vb, k_lat, v_lat, lam_params, gsub, lam_init):
    small = [pl.BlockSpec((1, DH_B), lambda *_: (0, 0))] * 4 + [pl.BlockSpec((1, DV_B), lambda *_: (0, 0))]
    body = functools.partial(_diffattn_kernel, lam_init=lam_init)
    ctx = pl.pallas_call(
        body,
        grid=(NCB,),
        in_specs=[_row_spec(B_QK), _row_spec(B_QK), _row_spec(2 * B_V)] + small,
        out_specs=_row_spec(B_V),
        out_shape=jax.ShapeDtypeStruct((N_CTX, B_V), BF16),
        compiler_params=_cparams(("arbitrary",)),
        name="diffattn_ctx",
    )(qb, kb, vb, *lam_params, gsub)
    nk = DEC_SEQ + PAST_LEN
    nqb = DEC_SEQ // TQ_LAT
    off = N_CTX // TQ_LAT
    lat = pl.pallas_call(
        body,
        grid=(DEC_BATCH, nqb),
        in_specs=[
            pl.BlockSpec((TQ_LAT, B_QK), lambda b, j: (off + b * nqb + j, 0)),
            pl.BlockSpec((None, nk, B_QK), lambda b, j: (b, 0, 0)),
            pl.BlockSpec((None, nk, 2 * B_V), lambda b, j: (b, 0, 0)),
        ] + small,
        out_specs=pl.BlockSpec((TQ_LAT, B_V), lambda b, j: (b * nqb + j, 0)),
        out_shape=jax.ShapeDtypeStruct((N_LAT, B_V), BF16),
        compiler_params=_cparams(("arbitrary", "arbitrary")),
        name="diffattn_lat",
    )(qb, k_lat, v_lat, *lam_params, gsub)
    return ctx, lat


def _swiglu_act(xb, wg_ref, wu_ref, act_ref):
    for lo in range(0, act_ref.shape[1], MXU_N):
        cs = slice(lo, lo + MXU_N)
        act_ref[:, cs] = (_silu(_dot(xb, wg_ref[:, cs])) * _dot(xb, wu_ref[:, cs])).astype(BF16)


def _mix_ffn_even_kernel(of_ref, ob_ref, ra_ref, actx_ref, alat_ref, xc_ref, xl_ref, g1_ref, ggla_ref, wo_ref,
                         gam_ref, sh_ref, sc_ref, g2_ref, wg_ref, wu_ref, wd_ref, o_ref, act_ref):
    heads = []
    for h in range(H_A):
        hs = slice(h * DV_A, (h + 1) * DV_A)
        heads.append((_rms(of_ref[:, hs] + ob_ref[:, hs], ggla_ref[...]) * _silu(ra_ref[:, hs])).astype(BF16))
    mix = jnp.concatenate(heads + [_pick_rows(actx_ref, alat_ref)], axis=1)
    x = _pick_rows(xc_ref, xl_ref) + g1_ref[...] * _dot(mix, wo_ref[...])
    hb = _norm_mod(x, gam_ref[...], sc_ref[...], sh_ref[...]).astype(BF16)
    _swiglu_act(hb, wg_ref, wu_ref, act_ref)
    o_ref[...] = x + g2_ref[...] * _dot(act_ref[...], wd_ref[...])


def _mix_ffn_even(o_f, o_b, a_proj, attn_ctx, attn_lat, x_ctx, x_lat, mod, g_gla, w_out, gamma, w_gate, w_up, w_down):
    return pl.pallas_call(
        _mix_ffn_even_kernel,
        grid=(NB,),
        in_specs=[_pair_spec(A_V), _pair_spec(A_V), _pair_spec(A_V, 2), _ctx_spec(B_V), _lat_spec(B_V),
                  _ctx_spec(D_MODEL), _lat_spec(D_MODEL), _mod_spec(2), _const_spec((1, DV_A)),
                  _const_spec((A_V + B_V, D_MODEL)),
                  _const_spec((1, D_MODEL)), _mod_spec(3), _mod_spec(4), _mod_spec(5),
                  _const_spec((D_MODEL, D_FF)), _const_spec((D_MODEL, D_FF)), _const_spec((D_FF, D_MODEL))],
        out_specs=_row_spec(D_MODEL),
        out_shape=jax.ShapeDtypeStruct((M_TOK, D_MODEL), F32),
        scratch_shapes=[pltpu.VMEM((TM, D_FF), BF16)],
        compiler_params=_cparams(("arbitrary",)),
        name="mix_ffn_even",
    )(o_f, o_b, a_proj, attn_ctx, attn_lat, x_ctx, x_lat, mod, g_gla, w_out, gamma, mod, mod, mod,
      w_gate, w_up, w_down)


def _inproj_odd_kernel(x_ref, gam_ref, sh_ref, sc_ref, w_ref, gq_ref, gk_ref, cos_ref, sin_ref,
                       q_ref, k_ref, v_ref, kctx_ref, vctx_ref, stage_ref):
    is_lat = _is_lat(TMB)
    hb = _norm_mod(x_ref[...], gam_ref[...], sc_ref[...], sh_ref[...]).astype(BF16)
    cos = jnp.where(is_lat, cos_ref[...], 1.0)
    sin = jnp.where(is_lat, sin_ref[...], 0.0)
    scale = DH_C ** -0.5 * LOG2E
    nq, nkv = H_C * DH_C, HKV_C * DH_C
    for lo in range(0, nq, MXU_N):
        z = _dot(hb, w_ref[:, lo:lo + MXU_N])
        for j in range(0, MXU_N, DH_C):
            q_ref[:, lo + j:lo + j + DH_C] = (
                _rope(_rms(z[:, j:j + DH_C], gq_ref[...]), cos, sin, DH_C // 4) * scale).astype(BF16)
    z = _dot(hb, w_ref[:, nq:nq + nkv])
    for h in range(HKV_C):
        hs = slice(h * DH_C, (h + 1) * DH_C)
        k_n = _rms(z[:, hs], gk_ref[...])
        stage_ref[:, hs] = k_n
        k_ref[:, hs] = _rope(k_n, cos, sin, DH_C // 4).astype(BF16)
    v = _dot(hb, w_ref[:, nq + nkv:])
    stage_ref[:, nkv:] = v
    for h in range(HKV_C):
        v_ref[:, 2 * h * DH_C:(2 * h + 1) * DH_C] = v[:, h * DH_C:(h + 1) * DH_C].astype(BF16)
        v_ref[:, (2 * h + 1) * DH_C:2 * (h + 1) * DH_C] = jnp.ones((TMB, DH_C), BF16)
    _copy_ctx(stage_ref, ((0, DH_C, HKV_C, kctx_ref), (nkv, DH_C, HKV_C, vctx_ref)))


def _inproj_odd(x, gamma, mod, w_in, g_q, g_k, cos, sin):
    nq, nkv = H_C * DH_C, HKV_C * DH_C
    return pl.pallas_call(
        _inproj_odd_kernel,
        grid=(M_TOK // TMB,),
        in_specs=[
            _row_spec(D_MODEL, tm=TMB), _const_spec((1, D_MODEL)), _mod_spec(0, TMB), _mod_spec(1, TMB),
            _const_spec((D_MODEL, nq + 2 * nkv)), _const_spec((1, DH_C)), _const_spec((1, DH_C)),
            pl.BlockSpec((TMB, LANES), lambda i: (_pos_block(i, TMB), 0)),
            pl.BlockSpec((TMB, LANES), lambda i: (_pos_block(i, TMB), 0)),
        ],
        out_specs=[_row_spec(nq, tm=TMB), _row_spec(nkv, tm=TMB), _row_spec(2 * nkv, tm=TMB),
                   _ctx_parts_spec(HKV_C, TMB), _ctx_parts_spec(HKV_C, TMB)],
        out_shape=[
            jax.ShapeDtypeStruct((M_TOK, nq), BF16), jax.ShapeDtypeStruct((M_TOK, nkv), BF16),
            jax.ShapeDtypeStruct((M_TOK, 2 * nkv), BF16), jax.ShapeDtypeStruct((N_CTX * HKV_C, LANES), F32),
            jax.ShapeDtypeStruct((N_CTX * HKV_C, LANES), F32),
        ],
        scratch_shapes=[pltpu.VMEM((TMB, 2 * nkv), F32)],
        compiler_params=_cparams(("arbitrary",)),
        name="inproj_odd",
    )(x, gamma, mod, mod, w_in, g_q, g_k, cos, sin)


def _gqa_body(q_ref, k_ref, v_ref, o_ref):
    rep = H_C // HKV_C
    tq = q_ref.shape[0]
    for hk in range(HKV_C):
        ks = slice(hk * DH_C, (hk + 1) * DH_C)
        q_g = jnp.concatenate([q_ref[:, (hk * rep + g) * DH_C:(hk * rep + g + 1) * DH_C] for g in range(rep)],
                              axis=0)
        o = _attend(q_g, k_ref[:, ks], v_ref[:, 2 * hk * DH_C:2 * (hk + 1) * DH_C])
        for g in range(rep):
            o_ref[:, (hk * rep + g) * DH_C:(hk * rep + g + 1) * DH_C] = o[g * tq:(g + 1) * tq].astype(BF16)


def _gqa(q, k, v, k_lat, v_lat):
    nq, nkv = H_C * DH_C, HKV_C * DH_C
    ctx = pl.pallas_call(
        _gqa_body,
        grid=(NCB,),
        in_specs=[_row_spec(nq), _row_spec(nkv), _row_spec(2 * nkv)],
        out_specs=_row_spec(nq),
        out_shape=jax.ShapeDtypeStruct((N_CTX, nq), BF16),
        compiler_params=_cparams(("arbitrary",)),
        name="gqa_ctx",
    )(q, k, v)
    nk = DEC_SEQ + PAST_LEN
    nqb = DEC_SEQ // TQ_LAT
    off = N_CTX // TQ_LAT
    lat = pl.pallas_call(
        _gqa_body,
        grid=(DEC_BATCH, nqb),
        in_specs=[
            pl.BlockSpec((TQ_LAT, nq), lambda b, j: (off + b * nqb + j, 0)),
            pl.BlockSpec((None, nk, nkv), lambda b, j: (b, 0, 0)),
            pl.BlockSpec((None, nk, 2 * nkv), lambda b, j: (b, 0, 0)),
        ],
        out_specs=pl.BlockSpec((TQ_LAT, nq), lambda b, j: (b * nqb + j, 0)),
        out_shape=jax.ShapeDtypeStruct((N_LAT, nq), BF16),
        compiler_params=_cparams(("arbitrary", "arbitrary")),
        name="gqa_lat",
    )(q, k_lat, v_lat)
    return ctx, lat


def _postmix_odd_kernel(actx_ref, alat_ref, x_ref, g1_ref, wo_ref, gam_ref, sh_ref, sc_ref, wrh_ref, wrl_ref,
                        x1_ref, h_ref, w1_ref, w2_ref, idx_ref, cnt_ref, run_ref):
    @pl.when(pl.program_id(0) == 0)
    def _():
        run_ref[...] = jnp.zeros_like(run_ref)

    is_lat = _is_lat(TMB)
    lane = lax.broadcasted_iota(jnp.int32, (TMB, LANES), 1)
    lane_f = lane.astype(F32)
    picked = []

    def part(rows):
        attn = jnp.where(is_lat, alat_ref[rows, :], actx_ref[rows, :])
        x1 = x_ref[rows, :] + g1_ref[...] * _dot(attn, wo_ref[...])
        x1_ref[rows, :] = x1
        yield
        h = _norm_mod(x1, gam_ref[...], sc_ref[...], sh_ref[...])
        h_hi, h_lo = _split2(h)
        h_ref[rows, :] = h_hi
        logits = _dot(h_hi, wrh_ref[...]) + (_dot(h_hi, wrl_ref[...]) + _dot(h_lo, wrh_ref[...]))
        yield
        lane_p = lax.broadcasted_iota(jnp.int32, logits.shape, 1).astype(F32)
        lg = jnp.where(lane_p < N_EXPERTS, logits, -jnp.inf)
        m1 = jnp.max(lg, axis=-1, keepdims=True)
        i1 = jnp.min(jnp.where(lg == m1, lane_p, float(LANES)), axis=-1, keepdims=True)
        lg2 = jnp.where(lane_p == i1, -jnp.inf, lg)
        m2 = jnp.max(lg2, axis=-1, keepdims=True)
        i2 = jnp.min(jnp.where(lg2 == m2, lane_p, float(LANES)), axis=-1, keepdims=True)
        e = jnp.exp(m2 - m1)
        w1 = 1.0 / (1.0 + e)
        w1_ref[rows, :] = jnp.broadcast_to(w1, logits.shape)
        w2_ref[rows, :] = jnp.broadcast_to(e * w1, logits.shape)
        picked.append((jnp.broadcast_to(i1, logits.shape), jnp.broadcast_to(i2, logits.shape)))

    nparts = 2
    live = [part(slice(p * TMB // nparts, (p + 1) * TMB // nparts)) for p in range(nparts)]
    while live:
        live = [g for g in live if next(g, True) is None]
    i1 = jnp.concatenate([p[0] for p in picked], axis=0)
    i2 = jnp.concatenate([p[1] for p in picked], axis=0)
    oh1 = jnp.where(lane_f == i1, 1.0, 0.0)
    oh2 = jnp.where(lane_f == i2, 1.0, 0.0)
    both = oh1 + oh2
    r_i = lax.broadcasted_iota(jnp.int32, (TMB, TMB), 0)
    c_i = lax.broadcasted_iota(jnp.int32, (TMB, TMB), 1)
    before = jnp.where(c_i < r_i, 1.0, 0.0).astype(BF16)
    pref = _dot(before, both.astype(BF16)) + run_ref[0:1, :]
    rank1 = jnp.sum(pref * oh1, axis=-1, keepdims=True)
    rank2 = jnp.sum(pref * oh2, axis=-1, keepdims=True)
    run_ref[...] = run_ref[...] + jnp.sum(both, axis=0, keepdims=True)
    cnt_ref[...] = run_ref[...]
    quarter = LANES // 4
    idx_ref[...] = jnp.where(lane < quarter, i1, jnp.where(lane < 2 * quarter, i2, jnp.where(
        lane < 3 * quarter, rank1, rank2))).astype(jnp.int32)


def _postmix_odd(attn_ctx, attn_lat, x, mod, w_out, gamma, w_router):
    wr_hi = w_router.astype(BF16)
    wr_lo = (w_router - wr_hi.astype(F32)).astype(BF16)
    row = functools.partial(_row_spec, tm=TMB)
    return pl.pallas_call(
        _postmix_odd_kernel,
        grid=(M_TOK // TMB,),
        in_specs=[_ctx_spec(H_C * DH_C, TMB), _lat_spec(H_C * DH_C, TMB), row(D_MODEL), _mod_spec(2, TMB),
                  _const_spec((H_C * DH_C, D_MODEL)),
                  _const_spec((1, D_MODEL)), _mod_spec(3, TMB), _mod_spec(4, TMB), _const_spec((D_MODEL, LANES)),
                  _const_spec((D_MODEL, LANES))],
        out_specs=[row(D_MODEL), row(D_MODEL), row(LANES), row(LANES), row(LANES), _const_spec((8, LANES))],
        out_shape=[
            jax.ShapeDtypeStruct((M_TOK, D_MODEL), F32), jax.ShapeDtypeStruct((M_TOK, D_MODEL), BF16),
            jax.ShapeDtypeStruct((M_TOK, LANES), F32), jax.ShapeDtypeStruct((M_TOK, LANES), F32),
            jax.ShapeDtypeStruct((M_TOK, LANES), jnp.int32), jax.ShapeDtypeStruct((8, LANES), F32),
        ],
        scratch_shapes=[pltpu.VMEM((8, LANES), F32)],
        compiler_params=_cparams(("arbitrary",)),
        name="postmix_odd",
    )(attn_ctx, attn_lat, x, mod, w_out, gamma, mod, mod, wr_hi, wr_lo)


def _cast_rows(src_ref, dst_ref, piece):
    def body(r, carry):
        rows = pl.ds(pl.multiple_of(r * piece, piece), piece)
        dst_ref[rows, :] = src_ref[rows, :].astype(BF16)
        return carry

    lax.fori_loop(0, src_ref.shape[0] // piece, body, 0)


def _experts_up_kernel(tile_ref, chunk_ref, exp_ref, first_ref, valid_ref, x_ref, wg_ref, wu_ref, act_ref,
                       wgb_ref, wub_ref):
    s = pl.program_id(0)

    @pl.when(first_ref[s] == 1)
    def _():
        _cast_rows(wg_ref, wgb_ref, 128)
        _cast_rows(wu_ref, wub_ref, 128)

    @pl.when(valid_ref[s] == 1)
    def _():
        _swiglu_act(x_ref[...].astype(BF16), wgb_ref, wub_ref, act_ref)


def _experts_down_kernel(act_tile_ref, exp_ref, first_ref, valid_ref, act_ref, wd_ref, o_ref, wdb_ref):
    i = pl.program_id(0)

    @pl.when(first_ref[i] == 1)
    def _():
        _cast_rows(wd_ref, wdb_ref, 256)

    @pl.when(valid_ref[i] == 1)
    def _():
        o_ref[...] = _dot(act_ref[...], wdb_ref[...])

    @pl.when(valid_ref[i] == 0)
    def _():
        o_ref[...] = jnp.zeros_like(o_ref)


def _experts(up_tables, down_tables, xs, w_gate, w_up, w_down):
    up = pl.pallas_call(
        _experts_up_kernel,
        grid_spec=pltpu.PrefetchScalarGridSpec(
            num_scalar_prefetch=5,
            grid=(NC_E * NT_E,),
            in_specs=[
                pl.BlockSpec((TME, D_MODEL), lambda s, t, c, e, f, v: (t[s], 0)),
                pl.BlockSpec((None, D_MODEL, TF_E), lambda s, t, c, e, f, v: (e[s], 0, c[s])),
                pl.BlockSpec((None, D_MODEL, TF_E), lambda s, t, c, e, f, v: (e[s], 0, c[s])),
            ],
            out_specs=pl.BlockSpec((TME, TF_E), lambda s, t, c, e, f, v: (t[s], c[s])),
            scratch_shapes=[pltpu.VMEM((D_MODEL, TF_E), BF16), pltpu.VMEM((D_MODEL, TF_E), BF16)],
        ),
        out_shape=jax.ShapeDtypeStruct((NT_E * TME, D_FF_E), BF16),
        compiler_params=_cparams(("arbitrary",)),
        name="experts_up",
    )(*up_tables, xs, w_gate, w_up)
    return pl.pallas_call(
        _experts_down_kernel,
        grid_spec=pltpu.PrefetchScalarGridSpec(
            num_scalar_prefetch=4,
            grid=(NT_E,),
            in_specs=[
                pl.BlockSpec((TME, D_FF_E), lambda i, a, e, f, v: (a[i], 0)),
                pl.BlockSpec((None, D_FF_E, D_MODEL), lambda i, a, e, f, v: (e[i], 0, 0)),
            ],
            out_specs=pl.BlockSpec((TME, D_MODEL), lambda i, a, e, f, v: (i, 0)),
            scratch_shapes=[pltpu.VMEM((D_FF_E, D_MODEL), BF16)],
        ),
        out_shape=jax.ShapeDtypeStruct((NT_E * TME, D_MODEL), F32),
        compiler_params=_cparams(("arbitrary",)),
        name="experts_down",
    )(*down_tables, up, w_down)


def _combine_kernel(x_ref, y1_ref, y2_ref, w1_ref, w2_ref, g2_ref, gam_ref, octx_ref, olat_ref):
    rep = D_MODEL // LANES
    w1 = jnp.concatenate([w1_ref[...]] * rep, axis=1)
    w2 = jnp.concatenate([w2_ref[...]] * rep, axis=1)
    x2 = x_ref[...] + g2_ref[...] * (w1 * y1_ref[...] + w2 * y2_ref[...])
    y = _rms(x2, gam_ref[...])
    is_lat = _is_lat(TMB)

    @pl.when(jnp.logical_not(is_lat))
    def _():
        octx_ref[...] = y

    @pl.when(is_lat)
    def _():
        olat_ref[...] = y


def _combine(x1, y1, y2, w1, w2, mod, gamma):
    return pl.pallas_call(
        _combine_kernel,
        grid=(M_TOK // TMB,),
        in_specs=[_row_spec(D_MODEL, tm=TMB), _row_spec(D_MODEL, tm=TMB), _row_spec(D_MODEL, tm=TMB),
                  _row_spec(LANES, tm=TMB), _row_spec(LANES, tm=TMB), _mod_spec(5, TMB), _const_spec((1, D_MODEL))],
        out_specs=[_ctx_spec(D_MODEL, TMB), _lat_spec(D_MODEL, TMB)],
        out_shape=[jax.ShapeDtypeStruct((N_CTX, D_MODEL), F32), jax.ShapeDtypeStruct((N_LAT, D_MODEL), F32)],
        compiler_params=_cparams(("arbitrary",)),
        name="combine",
    )(x1, y1, y2, w1, w2, mod, gamma)


def _rope_tables(rot_dim):
    n = DEC_SEQ
    rows = np.repeat(np.arange(n // GRID_W, dtype=np.float64), GRID_W)
    cols = np.tile(np.arange(GRID_W, dtype=np.float64), n // GRID_W)
    half = rot_dim // 2
    freqs = ROPE_THETA ** (-np.arange(0, half, 2, dtype=np.float64) / half)
    ang_r, ang_c = rows[:, None] * freqs, cols[:, None] * freqs
    cos = np.concatenate([np.cos(ang_r)] * 2 + [np.cos(ang_c)] * 2, axis=-1)
    sin = np.concatenate([-np.sin(ang_r), np.sin(ang_r), -np.sin(ang_c), np.sin(ang_c)], axis=-1)
    rep = LANES // rot_dim
    return (jnp.asarray(np.tile(cos, (1, rep)).astype(np.float32)),
            jnp.asarray(np.tile(sin, (1, rep)).astype(np.float32)))


def _with_ones(v):
    return jnp.concatenate([v, jnp.ones_like(v)], axis=-1)


def _rows(a, idx):
    return a.at[idx].get(mode="promise_in_bounds")


def _route(idx, counts_f):
    quarter = LANES // 4
    experts = jnp.arange(N_EXPERTS, dtype=jnp.int32)
    e2 = jnp.stack([idx[:, 0], idx[:, quarter]], axis=1)
    rank2 = jnp.stack([idx[:, 2 * quarter], idx[:, 3 * quarter]], axis=1)
    counts = counts_f[0, :N_EXPERTS].astype(jnp.int32)
    tiles = (counts + TME - 1) // TME
    tile_end = jnp.cumsum(tiles)
    tile_start = tile_end - tiles
    start = jnp.cumsum(counts) - counts
    onehot = e2[:, :, None] == experts[None, None, :]
    pos = jnp.sum(jnp.where(onehot, tile_start[None, None, :] * TME, 0), axis=-1) + rank2
    tile_id = jnp.arange(NT_E, dtype=jnp.int32)
    tile_expert = jnp.minimum(jnp.sum((tile_id[:, None] >= tile_end[None, :]).astype(jnp.int32), axis=1),
                              N_EXPERTS - 1).astype(jnp.int32)
    tile_valid = (tile_id < tile_end[-1]).astype(jnp.int32)
    t_oh = tile_expert[:, None] == experts[None, :]
    t_first = jnp.sum(jnp.where(t_oh, (start - tile_start * TME)[None, :], 0), axis=1) + tile_id * TME
    t_last = jnp.sum(jnp.where(t_oh, (start + counts)[None, :], 0), axis=1)
    order = jnp.argsort(e2.reshape(-1), stable=True).astype(jnp.int32)
    n_assign = order.shape[0]
    g_idx = t_first[:, None] + jnp.arange(TME, dtype=jnp.int32)[None, :]
    live = jnp.logical_and(g_idx < t_last[:, None], tile_valid[:, None] == 1)
    src = _rows(order, jnp.clip(g_idx, 0, n_assign - 1).reshape(-1)) // 2
    src_tok = jnp.where(live.reshape(-1), src, 0).astype(jnp.int32)

    n_tiles = tile_end[-1]
    last_tile = jnp.maximum(n_tiles - 1, 0)
    last_expert = jnp.sum(jnp.where(tile_id == last_tile, tile_expert, 0))
    t_tstart = jnp.sum(jnp.where(t_oh, tile_start[None, :], 0), axis=1)
    down_tables = (
        jnp.where(tile_valid == 1, tile_id, last_tile).astype(jnp.int32),
        tile_expert,
        jnp.logical_and(tile_valid == 1, tile_id == t_tstart).astype(jnp.int32),
        tile_valid,
    )
    s_id = jnp.arange(NC_E * NT_E, dtype=jnp.int32)
    s_exp = jnp.minimum(jnp.sum((s_id[:, None] >= NC_E * tile_end[None, :]).astype(jnp.int32), axis=1),
                        N_EXPERTS - 1)
    s_oh = s_exp[:, None] == experts[None, :]
    s_tstart = jnp.sum(jnp.where(s_oh, tile_start[None, :], 0), axis=1)
    s_ntile = jnp.maximum(jnp.sum(jnp.where(s_oh, tiles[None, :], 0), axis=1), 1)
    rel = s_id - NC_E * s_tstart
    s_valid = s_id < NC_E * n_tiles
    up_tables = (
        jnp.where(s_valid, s_tstart + rel % s_ntile, last_tile).astype(jnp.int32),
        jnp.where(s_valid, rel // s_ntile, NC_E - 1).astype(jnp.int32),
        jnp.where(s_valid, s_exp, last_expert).astype(jnp.int32),
        jnp.logical_and(s_valid, rel % s_ntile == 0).astype(jnp.int32),
        s_valid.astype(jnp.int32),
    )
    return src_tok, pos, up_tables, down_tables


def lambda_init(layer):
    return 0.8 - 0.6 * math.exp(-0.3 * layer)


def kernel(x_prompt, x_sample, state_a, cache_b_k, cache_b_v, cache_c_k, cache_c_v, c, c_ctx, w_mod, b_mod, norm_mix, norm_ffn, w_in_even, w_gate2_a, b_gate_a, g_gla, lam_q1, lam_k1, lam_q2, lam_k2, g_sub_b, w_out_even, w_in_odd, g_q_c, g_k_c, w_out_odd, ffn_gate, ffn_up, ffn_down, w_router, exp_gate, exp_up, exp_down, norm_final):
    x_ctx, x_lat = x_prompt.reshape(N_CTX, D_MODEL), x_sample.reshape(N_LAT, D_MODEL)
    cond =jnp.concatenate([c_ctx[None, :], c, jnp.zeros((N_COND - 1 - DEC_BATCH, D_MODEL), F32)], axis=0)
    mod = _modulation(cond, w_mod, b_mod).reshape(DEPTH, N_COND, 1, 6 * D_MODEL)

    w = w_in_even[0]
    na = 2 * A_QK + 2 * A_V
    gate_lo = na
    gate_hi = na + 2 * GATE_RANK
    col_scale = jnp.concatenate([jnp.full((A_QK,), DK_A ** -0.5, F32), jnp.ones((na - A_QK,), F32),
                                 jnp.full((B_QK,), DH_B ** -0.5, F32), jnp.ones((B_QK + B_V,), F32)])
    w_main = (jnp.concatenate([w[:, :gate_lo], w[:, gate_hi:]], axis=1) * col_scale).astype(BF16)
    w_gates = jnp.pad(w[:, gate_lo:gate_hi], ((0, 0), (0, LANES - 2 * GATE_RANK))).astype(BF16)
    g2 = jnp.zeros((LANES, 2 * A_QK), F32)
    g2 = g2.at[:GATE_RANK, :A_QK].set(w_gate2_a[0, 0]).at[GATE_RANK:2 * GATE_RANK, A_QK:].set(w_gate2_a[0, 1])
    bg = b_gate_a[0].reshape(1, 2 * A_QK)
    cos_b, sin_b = _rope_tables(DH_B)
    a_proj, glog, qb, kb, vb, kb_ctx, vb_ctx = _inproj_even(
        x_ctx, x_lat, norm_mix[0][None, :], mod[0], w_main, w_gates, g2.astype(BF16), bg, cos_b, sin_b)

    s0_all = jnp.concatenate([jnp.zeros((1, 2, A_QK, DV_A), F32),
                              state_a[:, 0].reshape(DEC_BATCH, 2, A_QK, DV_A)], axis=0)
    sel = np.arange(A_QK)[:, None] // DK_A == np.arange(LANES)[None, :] // GLA_SB
    m1 = jnp.asarray(sel.astype(np.float32)).astype(BF16)
    o_f, o_b, s_fin = _gla(a_proj, glog, s0_all, m1)

    k_lat = jnp.concatenate([kb[N_CTX:].reshape(DEC_BATCH, DEC_SEQ, B_QK),
                             cache_b_k[:, 0].reshape(DEC_BATCH, PAST_LEN, B_QK).astype(BF16)], axis=1)
    v_lat = jnp.concatenate([vb[N_CTX:].reshape(DEC_BATCH, DEC_SEQ, 2 * B_V),
                             _with_ones(cache_b_v[:, 0].astype(BF16)).reshape(DEC_BATCH, PAST_LEN, 2 * B_V)], axis=1)
    lam_params = [p[0][None, :] for p in (lam_q1, lam_k1, lam_q2, lam_k2)]
    attn_b = _diffattn(qb, kb, vb, k_lat, v_lat, lam_params, g_sub_b[0][None, :], lambda_init(0))

    x = _mix_ffn_even(o_f, o_b, a_proj, *attn_b, x_ctx, x_lat, mod[0], g_gla[0][None, :],
                      w_out_even[0].astype(BF16), norm_ffn[0][None, :], ffn_gate[0].astype(BF16),
                      ffn_up[0].astype(BF16), ffn_down[0].astype(BF16))

    cos_c, sin_c = _rope_tables(DH_C)
    q_c, k_c, v_c, kc_ctx, vc_ctx = _inproj_odd(x, norm_mix[1][None, :], mod[1], w_in_odd[0].astype(BF16),
                                     g_q_c[0][None, :], g_k_c[0][None, :], cos_c, sin_c)
    nkv = HKV_C * DH_C
    k_lat = jnp.concatenate([k_c[N_CTX:].reshape(DEC_BATCH, DEC_SEQ, nkv),
                             cache_c_k[:, 0].reshape(DEC_BATCH, PAST_LEN, nkv).astype(BF16)], axis=1)
    v_lat = jnp.concatenate([v_c[N_CTX:].reshape(DEC_BATCH, DEC_SEQ, 2 * nkv),
                             _with_ones(cache_c_v[:, 0].astype(BF16)).reshape(DEC_BATCH, PAST_LEN, 2 * nkv)], axis=1)
    attn_c = _gqa(q_c, k_c, v_c, k_lat, v_lat)
    w_r = jnp.pad(w_router[0], ((0, 0), (0, LANES - N_EXPERTS)))
    x1, h_moe, w1, w2, idx, counts = _postmix_odd(*attn_c, x, mod[1], w_out_odd[0].astype(BF16),
                                                  norm_ffn[1][None, :], w_r)

    src_tok, pos, up_tables, down_tables = _route(idx, counts)
    xs = _rows(h_moe, src_tok)
    ys = _experts(up_tables, down_tables, xs, exp_gate[0], exp_up[0], exp_down[0])
    y1 = _rows(ys, pos[:, 0])
    y2 = _rows(ys, pos[:, 1])
    y_ctx, y_lat = _combine(x1, y1, y2, w1, w2, mod[1], norm_final[None, :])

    y_prompt = y_ctx.reshape(BATCH, SEQ, D_MODEL)
    y_sample = y_lat.reshape(DEC_BATCH, DEC_SEQ, D_MODEL)
    new_state_a = s_fin.reshape(BATCH, 1, 2, H_A, DK_A, DV_A)
    new_b_k = jnp.transpose(kb_ctx.reshape(BATCH, 1, H_B, 2, DH_B, SEQ), (0, 1, 5, 2, 3, 4))
    new_b_v = vb_ctx.reshape(BATCH, 1, SEQ, H_B, DV_B)
    new_c_k = kc_ctx.reshape(BATCH, 1, SEQ, HKV_C, DH_C)
    new_c_v = vc_ctx.reshape(BATCH, 1, SEQ, HKV_C, DH_C)
    return (y_prompt, y_sample, new_state_a, new_b_k, new_b_v, new_c_k, new_c_v)
```

```python
import functools
import math

import numpy as np
import jax
import jax.numpy as jnp
from jax import lax
from jax.experimental import pallas as pl
from jax.experimental.pallas import tpu as pltpu

D_MODEL = 1024
BATCH = 32
SEQ = 256
DEPTH = 2
DEC_BATCH = 2
DEC_SEQ = 2048
PAST_LEN = 512
GRID_W = 64
H_A, DK_A, DV_A = 4, 64, 128
GATE_RANK = 16
GATE_TAU = 16.0
H_B, DH_B = 4, 64
DV_B = 2 * DH_B
H_C, HKV_C, DH_C = 8, 2, 128
D_FF = 2816
N_EXPERTS = 8
D_FF_E = 3584
ROPE_THETA = 10000.0
EPS = 1e-6
A_QK = H_A * DK_A
A_V = H_A * DV_A
B_QK = H_B * 2 * DH_B
B_V = H_B * DV_B

F32 = jnp.float32
BF16 = jnp.bfloat16
LOG2E = math.log2(math.e)

V7X_VMEM_BYTES = 64 * 1024 * 1024
VMEM_LIMIT = V7X_VMEM_BYTES * 7 // 8
LANES = 128
MXU_N = 256
TM = 256
N_CTX = BATCH * SEQ
N_LAT = DEC_BATCH * DEC_SEQ
M_TOK = N_CTX + N_LAT
NB = M_TOK // TM
NCB = N_CTX // TM
LBB = DEC_SEQ // TM
N_COND = 8
GLA_C = 64
GLA_SB = 8
TME = 512
NT_E = 2 * M_TOK // TME + N_EXPERTS
NC_E = 2
TF_E = D_FF_E // NC_E
TQ_LAT = 256
TMB = 512

assert SEQ == TM and DEC_SEQ % TM == 0 and DEPTH == 2


def _cparams(sem):
    return pltpu.CompilerParams(dimension_semantics=sem, vmem_limit_bytes=VMEM_LIMIT)


def _group(i, tm=TM):
    return jnp.where(i < N_CTX // tm, 0, 1 + (i - N_CTX // tm) // (DEC_SEQ // tm))


def _pos_block(i, tm=TM):
    return jnp.where(i < N_CTX // tm, 0, (i - N_CTX // tm) % (DEC_SEQ // tm))


def _mod_spec(chunk, tm=TM):
    return pl.BlockSpec((None, 1, D_MODEL), lambda i, c=chunk: (_group(i, tm), 0, c))


def _row_spec(width, col=0, tm=TM):
    return pl.BlockSpec((tm, width), lambda i, c=col: (i, c))


def _ctx_spec(width, tm=TM):
    return pl.BlockSpec((tm, width), lambda i: (jnp.minimum(i, N_CTX // tm - 1), 0))


def _ctx_parts_spec(parts, tm=TM):
    return pl.BlockSpec((tm * parts, LANES), lambda i: (jnp.minimum(i, N_CTX // tm - 1), 0))


def _lat_spec(width, tm=TM):
    return pl.BlockSpec((tm, width), lambda i: (jnp.maximum(i - N_CTX // tm, 0), 0))


def _const_spec(shape):
    nd = len(shape)
    return pl.BlockSpec(shape, lambda i, nd=nd: (0,) * nd, pipeline_mode=pl.Buffered(1))


def _dot(a, b):
    return jnp.dot(a, b, preferred_element_type=F32)


def _dot_nt(a, b):
    return lax.dot_general(a, b, (((1,), (1,)), ((), ())), preferred_element_type=F32)


def _split2(a):
    hi = a.astype(BF16)
    lo = (a - hi.astype(F32)).astype(BF16)
    return hi, lo


def _dot_hi(a, b):
    a_hi, a_lo = _split2(a)
    b_hi, b_lo = _split2(b)
    return _dot(a_hi, b_hi) + (_dot(a_hi, b_lo) + _dot(a_lo, b_hi))


def _silu(x):
    return (0.5 * x) * (1.0 + jnp.tanh(0.5 * x))


def _log_sigmoid(x):
    return jnp.minimum(x, 0.0) - jnp.log(1.0 + jnp.exp(-jnp.abs(x)))


def _rms(x, g):
    return x * lax.rsqrt(jnp.mean(x * x, axis=-1, keepdims=True) + EPS) * g


def _norm_mod(x, gamma, sc, sh):
    return _rms(x, gamma) * (1.0 + sc) + sh


def _staged(parts):
    live = list(parts)
    while live:
        live = [g for g in live if next(g, True) is None]


def _copy_ctx_transposed(stage3_ref, dst_ref):
    @pl.when(pl.program_id(0) < NCB)
    def _():
        def body(j, carry):
            dst_ref[pl.ds(pl.multiple_of(j * LANES, LANES), LANES), :] = jnp.transpose(stage3_ref[j])
            return carry

        lax.fori_loop(0, stage3_ref.shape[0], body, 0)


def _is_lat(tm):
    return pl.program_id(0) >= N_CTX // tm


def _pick_rows(ctx_ref, lat_ref):
    return jnp.where(_is_lat(ctx_ref.shape[0]), lat_ref[...], ctx_ref[...])


def _copy_ctx(stage_ref, outs):
    piece = 64

    @pl.when(jnp.logical_not(_is_lat(stage_ref.shape[0])))
    def _():
        def body(r, carry):
            rows = pl.ds(pl.multiple_of(r * piece, piece), piece)
            for lo, width, parts, dst_ref in outs:
                for p in range(parts):
                    val = stage_ref[rows, lo + p * width:lo + (p + 1) * width]
                    if width < LANES:
                        val = jnp.concatenate([val, jnp.zeros((piece, LANES - width), F32)], axis=1)
                    dst_ref[pl.ds(pl.multiple_of(r * piece * parts, piece * parts) + p, piece, stride=parts), :] = val
            return carry

        lax.fori_loop(0, stage_ref.shape[0] // piece, body, 0)


def _rope(x, cos, sin, half):
    lane = lax.broadcasted_iota(jnp.int32, x.shape, 1)
    first = (lane % (2 * half)) < half
    swapped = jnp.where(first, pltpu.roll(x, LANES - half, 1), pltpu.roll(x, half, 1))
    return x * cos + swapped * sin


def _mod_kernel(cond_ref, w_ref, b_ref, o_ref):
    o_ref[...] = _dot_hi(_silu(cond_ref[...]), w_ref[...]) + b_ref[...]


def _modulation(cond, w_mod, b_mod):
    tn = 1536
    return pl.pallas_call(
        _mod_kernel,
        grid=(DEPTH, 6 * D_MODEL // tn),
        in_specs=[
            pl.BlockSpec((N_COND, D_MODEL), lambda l, j: (0, 0)),
            pl.BlockSpec((None, D_MODEL, tn), lambda l, j: (l, 0, j)),
            pl.BlockSpec((None, 1, tn), lambda l, j: (l, 0, j)),
        ],
        out_specs=pl.BlockSpec((None, N_COND, tn), lambda l, j: (l, 0, j)),
        out_shape=jax.ShapeDtypeStruct((DEPTH, N_COND, 6 * D_MODEL), F32),
        compiler_params=_cparams(("arbitrary", "arbitrary")),
        name="modulation",
    )(cond, w_mod, b_mod.reshape(DEPTH, 1, 6 * D_MODEL))


def _inproj_even_kernel(xc_ref, xl_ref, gam_ref, sh_ref, sc_ref, w_ref, wg_ref, g2_ref, bg_ref, cos_ref, sin_ref,
                        a_ref, glog_ref, qb_ref, kb_ref, vb_ref, kctx_ref, vctx_ref, stage_ref, stagek_ref):
    is_lat = pl.program_id(0) >= NCB
    na = 2 * A_QK + 2 * A_V

    def part(rows):
        n = rows.stop - rows.start
        x = jnp.where(is_lat, xl_ref[rows, :], xc_ref[rows, :])
        hb = _norm_mod(x, gam_ref[...], sc_ref[...], sh_ref[...]).astype(BF16)
        cos = jnp.where(is_lat, cos_ref[rows, :], 1.0)
        sin = jnp.where(is_lat, sin_ref[rows, :], 0.0)
        yield
        for lo in range(0, na, MXU_N):
            a_ref[rows, lo:lo + MXU_N] = _dot(hb, w_ref[:, lo:lo + MXU_N])
        for lo in range(0, B_QK, MXU_N):
            zq = _dot(hb, w_ref[:, na + lo:na + lo + MXU_N])
            zk = _dot(hb, w_ref[:, na + B_QK + lo:na + B_QK + lo + MXU_N])
            zv = _dot(hb, w_ref[:, na + 2 * B_QK + lo:na + 2 * B_QK + lo + MXU_N])
            stage_ref[rows, lo:lo + MXU_N] = zv
            for j in range(0, MXU_N, LANES):
                cs = slice(lo + j, lo + j + LANES)
                vb_ref[rows, 2 * (lo + j):2 * (lo + j) + DV_B] = zv[:, j:j + LANES].astype(BF16)
                vb_ref[rows, 2 * (lo + j) + DV_B:2 * (lo + j + LANES)] = jnp.ones((n, DV_B), BF16)
                stagek_ref[(lo + j) // LANES, rows, :] = zk[:, j:j + LANES]
                qb_ref[rows, cs] = (_rope(zq[:, j:j + LANES], cos, sin, DH_B // 4) * LOG2E).astype(BF16)
                kb_ref[rows, cs] = _rope(zk[:, j:j + LANES], cos, sin, DH_B // 4).astype(BF16)
        gates = _dot(hb, wg_ref[...])
        xg = _dot(gates.astype(BF16), g2_ref[...]) + bg_ref[...]
        glog_ref[rows, :] = _log_sigmoid(xg) * (1.0 / GATE_TAU)

    _staged([part(slice(0, TM))])
    _copy_ctx(stage_ref, ((0, DV_B, H_B, vctx_ref),))
    _copy_ctx_transposed(stagek_ref, kctx_ref)


def _inproj_even(x_ctx, x_lat, gamma, mod, w_main, w_gates, g2, bg, cos, sin):
    na = 2 * A_QK + 2 * A_V
    nz = na + 2 * B_QK + B_V
    return pl.pallas_call(
        _inproj_even_kernel,
        grid=(NB,),
        in_specs=[
            _ctx_spec(D_MODEL), _lat_spec(D_MODEL), _const_spec((1, D_MODEL)), _mod_spec(0), _mod_spec(1),
            _const_spec((D_MODEL, nz)), _const_spec((D_MODEL, LANES)), _const_spec((LANES, 2 * A_QK)),
            _const_spec((1, 2 * A_QK)),
            pl.BlockSpec((TM, LANES), lambda i: (_pos_block(i), 0)),
            pl.BlockSpec((TM, LANES), lambda i: (_pos_block(i), 0)),
        ],
        out_specs=[_pair_spec(na), _pair_spec(2 * A_QK), _row_spec(B_QK), _row_spec(B_QK), _row_spec(2 * B_V),
                   pl.BlockSpec((None, B_QK, TM), lambda i: (jnp.minimum(i, NCB - 1), 0, 0)), _ctx_parts_spec(H_B)],
        out_shape=[
            jax.ShapeDtypeStruct((M_TOK, na), F32), jax.ShapeDtypeStruct((M_TOK, 2 * A_QK), F32),
            jax.ShapeDtypeStruct((M_TOK, B_QK), BF16), jax.ShapeDtypeStruct((M_TOK, B_QK), BF16),
            jax.ShapeDtypeStruct((M_TOK, 2 * B_V), BF16), jax.ShapeDtypeStruct((BATCH, B_QK, SEQ), F32),
            jax.ShapeDtypeStruct((N_CTX * H_B, LANES), F32),
        ],
        scratch_shapes=[pltpu.VMEM((TM, B_V), F32), pltpu.VMEM((B_QK // LANES, TM, LANES), F32)],
        compiler_params=_cparams(("arbitrary",)),
        name="inproj_even",
    )(x_ctx, x_lat, gamma, mod, mod, w_main, w_gates, g2, bg, cos, sin)


def _gla_chunk(q_ref, k_ref, v_ref, g_ref, o_ref, c, s_ref, a_ref, b_ref, m1_ref, head_mask, same_block, rev):
    q, k, v = q_ref[c], k_ref[c], v_ref[c]
    g = g_ref[c] * LOG2E
    v_b = v.astype(BF16)
    r_i = lax.broadcasted_iota(jnp.int32, (GLA_C, GLA_C), 0)
    c_i = lax.broadcasted_iota(jnp.int32, (GLA_C, GLA_C), 1)
    tri = jnp.where((c_i >= r_i) if rev else (c_i <= r_i), 1.0, 0.0).astype(BF16)
    g_hi = g.astype(BF16)
    g_r = g - g_hi.astype(F32)
    g_mid = g_r.astype(BF16)
    g_lo = (g_r - g_mid.astype(F32)).astype(BF16)
    bc = _dot(tri, g_hi) + (_dot(tri, g_mid) + _dot(tri, g_lo))
    b_ref[...] = bc
    yield
    t_i = lax.broadcasted_iota(jnp.int32, (GLA_SB, A_QK), 0)
    group = 2 * GLA_SB
    for i0 in range(0, GLA_C, group):
        blocks = range(i0, i0 + group, GLA_SB)
        q_i = [q_ref[c, i:i + GLA_SB, :] for i in blocks]
        b_i = [b_ref[i:i + GLA_SB, :] for i in blocks]
        for j in range(GLA_SB):
            keep = (t_i <= j) if rev else (t_i >= j)
            pair = []
            for n, i in enumerate(blocks):
                k_s = k_ref[c, i + j:i + j + 1, :]
                b_s = b_ref[i + j:i + j + 1, :]
                pair.append(jnp.where(keep, q_i[n] * k_s * jnp.exp2(b_i[n] - b_s), 0.0))
            a_ref[j * GLA_C + i0:j * GLA_C + i0 + group, :] = jnp.concatenate(pair, axis=0).astype(BF16)
    yield
    last = bc[0:1] if rev else bc[GLA_C - 1:GLA_C]
    qc = q * jnp.exp2(bc)
    kl = k * jnp.exp2(last - bc)
    row = lax.broadcasted_iota(jnp.int32, (GLA_C, A_QK), 0)

    def expand(x):
        return jnp.where(head_mask, jnp.concatenate([x] * H_A, axis=0), 0.0).astype(BF16)

    scores = None
    half = GLA_C // 2
    while half >= GLA_SB:
        mids = [bc[b + half:b + half + 1] if rev else bc[b + half - 1:b + half] for b in range(0, GLA_C, 2 * half)]
        ref = mids[0] if len(mids) == 1 else jnp.concatenate(
            [jnp.broadcast_to(m, (2 * half, A_QK)) for m in mids], axis=0)
        upper = (row // half) % 2 == 1
        later, earlier = (jnp.logical_not(upper), upper) if rev else (upper, jnp.logical_not(upper))
        q_l = jnp.where(later, q * jnp.exp2(bc - ref), 0.0)
        k_e = jnp.where(earlier, k * jnp.exp2(ref - bc), 0.0).astype(BF16)
        term = _dot_nt(expand(q_l), k_e)
        if 2 * half < GLA_C:
            term = jnp.where(same_block[2 * half], term, 0.0)
        scores = term if scores is None else scores + term
        half //= 2
    s_old = s_ref[...]
    inter = _dot(expand(qc), s_old.astype(BF16))
    sums = _dot(a_ref[...], m1_ref[...])
    yield
    lane_c = lax.broadcasted_iota(jnp.int32, (GLA_C, LANES), 1)
    within = None
    for j in range(GLA_SB):
        part = jnp.where(lane_c % GLA_SB == j, sums[j * GLA_C:(j + 1) * GLA_C], 0.0)
        within = part if within is None else within + part
    lane = lax.broadcasted_iota(jnp.int32, (GLA_SB, LANES), 1)
    per_head = []
    for h in range(H_A):
        rows_h = []
        for i in range(0, GLA_C, GLA_SB):
            moved = pltpu.roll(within[i:i + GLA_SB], (i - h * GLA_SB) % LANES, 1)
            rows_h.append(jnp.where(lane // GLA_SB == i // GLA_SB, moved, 0.0))
        per_head.append(jnp.concatenate(rows_h, axis=0)[:, :GLA_C])
    scores = scores + jnp.concatenate(per_head, axis=0)
    o_heads = []
    for h in range(H_A):
        hs = slice(h * GLA_C, (h + 1) * GLA_C)
        o_heads.append(inter[hs] + _dot(scores[hs].astype(BF16), v_b[:, h * DV_A:(h + 1) * DV_A]))
    o_ref[c] = jnp.concatenate(o_heads, axis=1)
    yield
    t = jnp.transpose(jnp.concatenate([kl, jnp.broadcast_to(last, (GLA_C, A_QK))], axis=0))
    kv = _dot(t[:, :GLA_C].astype(BF16), v_b)
    a_col = jnp.exp2(t[:, GLA_C:GLA_C + 1])
    for h in range(H_A):
        hs = slice(h * DK_A, (h + 1) * DK_A)
        s_ref[hs, :] = a_col[hs] * s_old[hs] + kv[hs, h * DV_A:(h + 1) * DV_A]


def _gla_kernel(fblk_ref, bblk_ref, first_ref, inita_ref, initb_ref,
                qf_ref, kf_ref, vf_ref, gf_ref, qr_ref, kr_ref, vr_ref, gr_ref, s0a_ref, s0b_ref, m1_ref,
                of_ref, ob_ref, sfin_ref, s_refs, a_refs, b_refs):
    step = pl.program_id(0)

    @pl.when(first_ref[step] == 1)
    def _():
        for seq, s0_ref in enumerate((s0a_ref, s0b_ref)):
            for d in range(2):
                s_refs[2 * seq + d] = s0_ref[d]

    r_h = lax.broadcasted_iota(jnp.int32, (H_A * GLA_C, A_QK), 0) // GLA_C
    c_h = lax.broadcasted_iota(jnp.int32, (H_A * GLA_C, A_QK), 1) // DK_A
    head_mask = r_h == c_h
    t_s = lax.broadcasted_iota(jnp.int32, (H_A * GLA_C, GLA_C), 0) % GLA_C
    s_s = lax.broadcasted_iota(jnp.int32, (H_A * GLA_C, GLA_C), 1)
    same_block = {}
    size = 2 * GLA_SB
    while size < GLA_C:
        same_block[size] = (t_s // size) == (s_s // size)
        size *= 2
    nch = TM // GLA_C

    def body(c, carry):
        chains = []
        for seq in range(2):
            chains.append(_gla_chunk(qf_ref, kf_ref, vf_ref, gf_ref, of_ref, seq * nch + c, s_refs.at[2 * seq],
                                     a_refs.at[2 * seq], b_refs.at[2 * seq], m1_ref, head_mask, same_block, False))
            chains.append(_gla_chunk(qr_ref, kr_ref, vr_ref, gr_ref, ob_ref, seq * nch + nch - 1 - c,
                                     s_refs.at[2 * seq + 1], a_refs.at[2 * seq + 1], b_refs.at[2 * seq + 1], m1_ref,
                                     head_mask, same_block, True))
        _staged(chains)
        return carry

    lax.fori_loop(0, nch, body, 0)

    @pl.when(step < BATCH // 2)
    def _():
        for seq in range(2):
            for d in range(2):
                sfin_ref[seq, d] = s_refs[2 * seq + d]


def _pair_block(i):
    j = i - NCB
    return jnp.where(i < NCB, i, NCB + 2 * (j % LBB) + j // LBB)


def _pair_spec(width, col=0):
    return pl.BlockSpec((TM, width), lambda i, c=col: (_pair_block(i), c))


def _gla_tables():
    fblk, bblk, first, init_a, init_b = [], [], [], [], []
    for p in range(BATCH // 2):
        fblk.append(p), bblk.append(p), first.append(1), init_a.append(0), init_b.append(0)
    for j in range(LBB):
        fblk.append(NCB // 2 + j)
        bblk.append(NCB // 2 + LBB - 1 - j)
        first.append(1 if j == 0 else 0)
        init_a.append(1), init_b.append(2)
    return [jnp.asarray(np.array(t, np.int32)) for t in (fblk, bblk, first, init_a, init_b)]


def _gla(a_proj, glog, s0_all, m1):
    tables = _gla_tables()
    nsteps = int(tables[0].shape[0])
    nseq = BATCH + DEC_BATCH

    def fmap(col):
        return lambda i, fb, bb, fi, ia, ib: (fb[i], 0, col)

    def rmap(col):
        return lambda i, fb, bb, fi, ia, ib: (bb[i], 0, col)

    nch = 2 * TM // GLA_C
    a3 = a_proj.reshape(M_TOK // GLA_C, GLA_C, a_proj.shape[1])
    g3 = glog.reshape(M_TOK // GLA_C, GLA_C, glog.shape[1])
    gs = pltpu.PrefetchScalarGridSpec(
        num_scalar_prefetch=5,
        grid=(nsteps,),
        in_specs=[
            pl.BlockSpec((nch, GLA_C, A_QK), fmap(0)), pl.BlockSpec((nch, GLA_C, A_QK), fmap(1)),
            pl.BlockSpec((nch, GLA_C, A_V), fmap(1)), pl.BlockSpec((nch, GLA_C, A_QK), fmap(0)),
            pl.BlockSpec((nch, GLA_C, A_QK), rmap(0)), pl.BlockSpec((nch, GLA_C, A_QK), rmap(1)),
            pl.BlockSpec((nch, GLA_C, A_V), rmap(1)), pl.BlockSpec((nch, GLA_C, A_QK), rmap(1)),
            pl.BlockSpec((None, 2, A_QK, DV_A), lambda i, fb, bb, fi, ia, ib: (ia[i], 0, 0, 0)),
            pl.BlockSpec((None, 2, A_QK, DV_A), lambda i, fb, bb, fi, ia, ib: (ib[i], 0, 0, 0)),
            pl.BlockSpec((A_QK, LANES), lambda i, fb, bb, fi, ia, ib: (0, 0)),
        ],
        out_specs=[
            pl.BlockSpec((nch, GLA_C, A_V), fmap(0)), pl.BlockSpec((nch, GLA_C, A_V), rmap(0)),
            pl.BlockSpec((2, 2, A_QK, DV_A),
                         lambda i, fb, bb, fi, ia, ib: (jnp.minimum(fb[i], BATCH // 2 - 1), 0, 0, 0)),
        ],
        scratch_shapes=[
            pltpu.VMEM((4, A_QK, DV_A), F32),
            pltpu.VMEM((4, GLA_SB * GLA_C, A_QK), BF16),
            pltpu.VMEM((4, GLA_C, A_QK), F32),
        ],
    )
    o_f, o_b, s_fin = pl.pallas_call(
        _gla_kernel,
        grid_spec=gs,
        out_shape=[
            jax.ShapeDtypeStruct((M_TOK // GLA_C, GLA_C, A_V), F32),
            jax.ShapeDtypeStruct((M_TOK // GLA_C, GLA_C, A_V), F32),
            jax.ShapeDtypeStruct((BATCH, 2, A_QK, DV_A), F32),
        ],
        compiler_params=_cparams(("arbitrary",)),
        name="gla",
    )(*tables, a3, a3, a3, g3, a3, a3, a3, g3, s0_all, s0_all, m1)
    return o_f.reshape(M_TOK, A_V), o_b.reshape(M_TOK, A_V), s_fin


def _attend(q_b, k_b, v_ones):
    s2 = _dot_nt(q_b, k_b)
    e = jnp.exp2(s2 - jnp.max(s2, axis=-1, keepdims=True)).astype(BF16)
    both = _dot(e, v_ones)
    dv = v_ones.shape[1] // 2
    return both[:, :dv] * (1.0 / both[:, dv:])


def _diffattn_body(q_ref, k_ref, v_ref, lq1_ref, lk1_ref, lq2_ref, lk2_ref, gsub_ref, o_ref, lam_init):
    lam = (jnp.exp(jnp.sum(lq1_ref[...] * lk1_ref[...], axis=-1, keepdims=True))
           - jnp.exp(jnp.sum(lq2_ref[...] * lk2_ref[...], axis=-1, keepdims=True)) + lam_init)
    lane = lax.broadcasted_iota(jnp.int32, (q_ref.shape[0], LANES), 1)
    for h in range(H_B):
        hs = slice(h * LANES, (h + 1) * LANES)
        q_h = q_ref[:, hs]
        k_h = k_ref[:, hs]
        v_h = v_ref[:, 2 * h * DV_B:2 * (h + 1) * DV_B]
        zero = jnp.zeros_like(q_h)
        o = (_attend(jnp.where(lane < DH_B, q_h, zero), k_h, v_h)
             - lam * _attend(jnp.where(lane >= DH_B, q_h, zero), k_h, v_h))
        o_ref[:, hs] = (_rms(o, gsub_ref[...]) * (1.0 - lam_init)).astype(BF16)


def _diffattn_kernel(q_ref, k_ref, v_ref, lq1, lk1, lq2, lk2, gsub, o_ref, *, lam_init):
    _diffattn_body(q_ref, k_ref, v_ref, lq1, lk1, lq2, lk2, gsub, o_ref, lam_init)


def _diffattn(qb, kb, vb, k_lat, v_lat, lam_params, gsub, lam_init):
    small = [pl.BlockSpec((1, DH_B), lambda *_: (0, 0))] * 4 + [pl.BlockSpec((1, DV_B), lambda *_: (0, 0))]
    body = functools.partial(_diffattn_kernel, lam_init=lam_init)
    ctx = pl.pallas_call(
        body,
        grid=(NCB,),
        in_specs=[_row_spec(B_QK), _row_spec(B_QK), _row_spec(2 * B_V)] + small,
        out_specs=_row_spec(B_V),
        out_shape=jax.ShapeDtypeStruct((N_CTX, B_V), BF16),
        compiler_params=_cparams(("arbitrary",)),
        name="diffattn_ctx",
    )(qb, kb, vb, *lam_params, gsub)
    nk = DEC_SEQ + PAST_LEN
    nqb = DEC_SEQ // TQ_LAT
    off = N_CTX // TQ_LAT
    lat = pl.pallas_call(
        body,
        grid=(DEC_BATCH, nqb),
        in_specs=[
            pl.BlockSpec((TQ_LAT, B_QK), lambda b, j: (off + b * nqb + j, 0)),
            pl.BlockSpec((None, nk, B_QK), lambda b, j: (b, 0, 0)),
            pl.BlockSpec((None, nk, 2 * B_V), lambda b, j: (b, 0, 0)),
        ] + small,
        out_specs=pl.BlockSpec((TQ_LAT, B_V), lambda b, j: (b * nqb + j, 0)),
        out_shape=jax.ShapeDtypeStruct((N_LAT, B_V), BF16),
        compiler_params=_cparams(("arbitrary", "arbitrary")),
        name="diffattn_lat",
    )(qb, k_lat, v_lat, *lam_params, gsub)
    return ctx, lat


def _swiglu_act(xb, wg_ref, wu_ref, act_ref):
    for lo in range(0, act_ref.shape[1], MXU_N):
        cs = slice(lo, lo + MXU_N)
        act_ref[:, cs] = (_silu(_dot(xb, wg_ref[:, cs])) * _dot(xb, wu_ref[:, cs])).astype(BF16)


def _mix_ffn_even_kernel(of_ref, ob_ref, ra_ref, actx_ref, alat_ref, xc_ref, xl_ref, g1_ref, ggla_ref, wo_ref,
                         gam_ref, sh_ref, sc_ref, g2_ref, wg_ref, wu_ref, wd_ref, o_ref, act_ref):
    is_lat = _is_lat(TM)

    def part(rows):
        heads = []
        for h in range(H_A):
            hs = slice(h * DV_A, (h + 1) * DV_A)
            heads.append((_rms(of_ref[rows, hs] + ob_ref[rows, hs], ggla_ref[...])
                          * _silu(ra_ref[rows, hs])).astype(BF16))
        mix = jnp.concatenate(heads + [jnp.where(is_lat, alat_ref[rows, :], actx_ref[rows, :])], axis=1)
        yield
        x = jnp.where(is_lat, xl_ref[rows, :], xc_ref[rows, :]) + g1_ref[...] * _dot(mix, wo_ref[...])
        hb = _norm_mod(x, gam_ref[...], sc_ref[...], sh_ref[...]).astype(BF16)
        yield
        for lo in range(0, D_FF, MXU_N):
            cs = slice(lo, lo + MXU_N)
            act_ref[rows, cs] = (_silu(_dot(hb, wg_ref[:, cs])) * _dot(hb, wu_ref[:, cs])).astype(BF16)
        yield
        o_ref[rows, :] = x + g2_ref[...] * _dot(act_ref[rows, :], wd_ref[...])

    _staged([part(slice(r, r + TM // 2)) for r in range(0, TM, TM // 2)])


def _mix_ffn_even(o_f, o_b, a_proj, attn_ctx, attn_lat, x_ctx, x_lat, mod, g_gla, w_out, gamma, w_gate, w_up, w_down):
    return pl.pallas_call(
        _mix_ffn_even_kernel,
        grid=(NB,),
        in_specs=[_pair_spec(A_V), _pair_spec(A_V), _pair_spec(A_V, 2), _ctx_spec(B_V), _lat_spec(B_V),
                  _ctx_spec(D_MODEL), _lat_spec(D_MODEL), _mod_spec(2), _const_spec((1, DV_A)),
                  _const_spec((A_V + B_V, D_MODEL)),
                  _const_spec((1, D_MODEL)), _mod_spec(3), _mod_spec(4), _mod_spec(5),
                  _const_spec((D_MODEL, D_FF)), _const_spec((D_MODEL, D_FF)), _const_spec((D_FF, D_MODEL))],
        out_specs=_row_spec(D_MODEL),
        out_shape=jax.ShapeDtypeStruct((M_TOK, D_MODEL), F32),
        scratch_shapes=[pltpu.VMEM((TM, D_FF), BF16)],
        compiler_params=_cparams(("arbitrary",)),
        name="mix_ffn_even",
    )(o_f, o_b, a_proj, attn_ctx, attn_lat, x_ctx, x_lat, mod, g_gla, w_out, gamma, mod, mod, mod,
      w_gate, w_up, w_down)


def _inproj_odd_kernel(x_ref, gam_ref, sh_ref, sc_ref, w_ref, gq_ref, gk_ref, cos_ref, sin_ref,
                       q_ref, k_ref, v_ref, kctx_ref, vctx_ref, stage_ref):
    is_lat = _is_lat(TMB)
    scale = DH_C ** -0.5 * LOG2E
    nq, nkv = H_C * DH_C, HKV_C * DH_C

    def part(rows):
        n = rows.stop - rows.start
        hb = _norm_mod(x_ref[rows, :], gam_ref[...], sc_ref[...], sh_ref[...]).astype(BF16)
        cos = jnp.where(is_lat, cos_ref[rows, :], 1.0)
        sin = jnp.where(is_lat, sin_ref[rows, :], 0.0)
        yield
        z = _dot(hb, w_ref[:, nq:nq + nkv])
        for h in range(HKV_C):
            hs = slice(h * DH_C, (h + 1) * DH_C)
            k_n = _rms(z[:, hs], gk_ref[...])
            stage_ref[rows, hs] = k_n
            k_ref[rows, hs] = _rope(k_n, cos, sin, DH_C // 4).astype(BF16)
        v = _dot(hb, w_ref[:, nq + nkv:])
        stage_ref[rows, nkv:] = v
        for h in range(HKV_C):
            v_ref[rows, 2 * h * DH_C:(2 * h + 1) * DH_C] = v[:, h * DH_C:(h + 1) * DH_C].astype(BF16)
            v_ref[rows, (2 * h + 1) * DH_C:2 * (h + 1) * DH_C] = jnp.ones((n, DH_C), BF16)
        for lo in range(0, nq, MXU_N):
            z = _dot(hb, w_ref[:, lo:lo + MXU_N])
            for j in range(0, MXU_N, DH_C):
                q_ref[rows, lo + j:lo + j + DH_C] = (
                    _rope(_rms(z[:, j:j + DH_C], gq_ref[...]), cos, sin, DH_C // 4) * scale).astype(BF16)

    _staged([part(slice(r, r + TM)) for r in range(0, TMB, TM)])
    _copy_ctx(stage_ref, ((0, DH_C, HKV_C, kctx_ref), (nkv, DH_C, HKV_C, vctx_ref)))


def _inproj_odd(x, gamma, mod, w_in, g_q, g_k, cos, sin):
    nq, nkv = H_C * DH_C, HKV_C * DH_C
    return pl.pallas_call(
        _inproj_odd_kernel,
        grid=(M_TOK // TMB,),
        in_specs=[
            _row_spec(D_MODEL, tm=TMB), _const_spec((1, D_MODEL)), _mod_spec(0, TMB), _mod_spec(1, TMB),
            _const_spec((D_MODEL, nq + 2 * nkv)), _const_spec((1, DH_C)), _const_spec((1, DH_C)),
            pl.BlockSpec((TMB, LANES), lambda i: (_pos_block(i, TMB), 0)),
            pl.BlockSpec((TMB, LANES), lambda i: (_pos_block(i, TMB), 0)),
        ],
        out_specs=[_row_spec(nq, tm=TMB), _row_spec(nkv, tm=TMB), _row_spec(2 * nkv, tm=TMB),
                   _ctx_parts_spec(HKV_C, TMB), _ctx_parts_spec(HKV_C, TMB)],
        out_shape=[
            jax.ShapeDtypeStruct((M_TOK, nq), BF16), jax.ShapeDtypeStruct((M_TOK, nkv), BF16),
            jax.ShapeDtypeStruct((M_TOK, 2 * nkv), BF16), jax.ShapeDtypeStruct((N_CTX * HKV_C, LANES), F32),
            jax.ShapeDtypeStruct((N_CTX * HKV_C, LANES), F32),
        ],
        scratch_shapes=[pltpu.VMEM((TMB, 2 * nkv), F32)],
        compiler_params=_cparams(("arbitrary",)),
        name="inproj_odd",
    )(x, gamma, mod, mod, w_in, g_q, g_k, cos, sin)


def _gqa_body(q_ref, k_ref, v_ref, o_ref):
    rep = H_C // HKV_C
    tq = q_ref.shape[0]
    for hk in range(HKV_C):
        ks = slice(hk * DH_C, (hk + 1) * DH_C)
        q_g = jnp.concatenate([q_ref[:, (hk * rep + g) * DH_C:(hk * rep + g + 1) * DH_C] for g in range(rep)],
                              axis=0)
        o = _attend(q_g, k_ref[:, ks], v_ref[:, 2 * hk * DH_C:2 * (hk + 1) * DH_C])
        for g in range(rep):
            o_ref[:, (hk * rep + g) * DH_C:(hk * rep + g + 1) * DH_C] = o[g * tq:(g + 1) * tq].astype(BF16)


def _gqa(q, k, v, k_lat, v_lat):
    nq, nkv = H_C * DH_C, HKV_C * DH_C
    ctx = pl.pallas_call(
        _gqa_body,
        grid=(NCB,),
        in_specs=[_row_spec(nq), _row_spec(nkv), _row_spec(2 * nkv)],
        out_specs=_row_spec(nq),
        out_shape=jax.ShapeDtypeStruct((N_CTX, nq), BF16),
        compiler_params=_cparams(("arbitrary",)),
        name="gqa_ctx",
    )(q, k, v)
    nk = DEC_SEQ + PAST_LEN
    nqb = DEC_SEQ // TQ_LAT
    off = N_CTX // TQ_LAT
    lat = pl.pallas_call(
        _gqa_body,
        grid=(DEC_BATCH, nqb),
        in_specs=[
            pl.BlockSpec((TQ_LAT, nq), lambda b, j: (off + b * nqb + j, 0)),
            pl.BlockSpec((None, nk, nkv), lambda b, j: (b, 0, 0)),
            pl.BlockSpec((None, nk, 2 * nkv), lambda b, j: (b, 0, 0)),
        ],
        out_specs=pl.BlockSpec((TQ_LAT, nq), lambda b, j: (b * nqb + j, 0)),
        out_shape=jax.ShapeDtypeStruct((N_LAT, nq), BF16),
        compiler_params=_cparams(("arbitrary", "arbitrary")),
        name="gqa_lat",
    )(q, k_lat, v_lat)
    return ctx, lat


def _postmix_odd_kernel(actx_ref, alat_ref, x_ref, g1_ref, wo_ref, gam_ref, sh_ref, sc_ref, wrh_ref, wrl_ref,
                        x1_ref, h_ref, w1_ref, w2_ref, idx_ref, cnt_ref, run_ref):
    @pl.when(pl.program_id(0) == 0)
    def _():
        run_ref[...] = jnp.zeros_like(run_ref)

    is_lat = _is_lat(TMB)
    lane = lax.broadcasted_iota(jnp.int32, (TMB, LANES), 1)
    lane_f = lane.astype(F32)
    picked = []

    def part(rows):
        attn = jnp.where(is_lat, alat_ref[rows, :], actx_ref[rows, :])
        x1 = x_ref[rows, :] + g1_ref[...] * _dot(attn, wo_ref[...])
        x1_ref[rows, :] = x1
        yield
        h = _norm_mod(x1, gam_ref[...], sc_ref[...], sh_ref[...])
        h_hi, h_lo = _split2(h)
        h_ref[rows, :] = h_hi
        logits = _dot(h_hi, wrh_ref[...]) + (_dot(h_hi, wrl_ref[...]) + _dot(h_lo, wrh_ref[...]))
        yield
        lane_p = lax.broadcasted_iota(jnp.int32, logits.shape, 1).astype(F32)
        lg = jnp.where(lane_p < N_EXPERTS, logits, -jnp.inf)
        m1 = jnp.max(lg, axis=-1, keepdims=True)
        i1 = jnp.min(jnp.where(lg == m1, lane_p, float(LANES)), axis=-1, keepdims=True)
        lg2 = jnp.where(lane_p == i1, -jnp.inf, lg)
        m2 = jnp.max(lg2, axis=-1, keepdims=True)
        i2 = jnp.min(jnp.where(lg2 == m2, lane_p, float(LANES)), axis=-1, keepdims=True)
        e = jnp.exp(m2 - m1)
        w1 = 1.0 / (1.0 + e)
        w1_ref[rows, :] = jnp.broadcast_to(w1, logits.shape)
        w2_ref[rows, :] = jnp.broadcast_to(e * w1, logits.shape)
        picked.append((jnp.broadcast_to(i1, logits.shape), jnp.broadcast_to(i2, logits.shape)))

    _staged([part(slice(r, r + TM)) for r in range(0, TMB, TM)])
    i1 = jnp.concatenate([p[0] for p in picked], axis=0)
    i2 = jnp.concatenate([p[1] for p in picked], axis=0)
    oh1 = jnp.where(lane_f == i1, 1.0, 0.0)
    oh2 = jnp.where(lane_f == i2, 1.0, 0.0)
    both = oh1 + oh2
    r_i = lax.broadcasted_iota(jnp.int32, (TMB, TMB), 0)
    c_i = lax.broadcasted_iota(jnp.int32, (TMB, TMB), 1)
    before = jnp.where(c_i < r_i, 1.0, 0.0).astype(BF16)
    pref = _dot(before, both.astype(BF16)) + run_ref[0:1, :]
    rank1 = jnp.sum(pref * oh1, axis=-1, keepdims=True)
    rank2 = jnp.sum(pref * oh2, axis=-1, keepdims=True)
    run_ref[...] = run_ref[...] + jnp.sum(both, axis=0, keepdims=True)
    cnt_ref[...] = run_ref[...]
    quarter = LANES // 4
    idx_ref[...] = jnp.where(lane < quarter, i1, jnp.where(lane < 2 * quarter, i2, jnp.where(
        lane < 3 * quarter, rank1, rank2))).astype(jnp.int32)


def _postmix_odd(attn_ctx, attn_lat, x, mod, w_out, gamma, w_router):
    wr_hi = w_router.astype(BF16)
    wr_lo = (w_router - wr_hi.astype(F32)).astype(BF16)
    row = functools.partial(_row_spec, tm=TMB)
    return pl.pallas_call(
        _postmix_odd_kernel,
        grid=(M_TOK // TMB,),
        in_specs=[_ctx_spec(H_C * DH_C, TMB), _lat_spec(H_C * DH_C, TMB), row(D_MODEL), _mod_spec(2, TMB),
                  _const_spec((H_C * DH_C, D_MODEL)),
                  _const_spec((1, D_MODEL)), _mod_spec(3, TMB), _mod_spec(4, TMB), _const_spec((D_MODEL, LANES)),
                  _const_spec((D_MODEL, LANES))],
        out_specs=[row(D_MODEL), row(D_MODEL), row(LANES), row(LANES), row(LANES), _const_spec((8, LANES))],
        out_shape=[
            jax.ShapeDtypeStruct((M_TOK, D_MODEL), F32), jax.ShapeDtypeStruct((M_TOK, D_MODEL), BF16),
            jax.ShapeDtypeStruct((M_TOK, LANES), F32), jax.ShapeDtypeStruct((M_TOK, LANES), F32),
            jax.ShapeDtypeStruct((M_TOK, LANES), jnp.int32), jax.ShapeDtypeStruct((8, LANES), F32),
        ],
        scratch_shapes=[pltpu.VMEM((8, LANES), F32)],
        compiler_params=_cparams(("arbitrary",)),
        name="postmix_odd",
    )(attn_ctx, attn_lat, x, mod, w_out, gamma, mod, mod, wr_hi, wr_lo)


def _cast_rows(src_ref, dst_ref, piece):
    def body(r, carry):
        rows = pl.ds(pl.multiple_of(r * piece, piece), piece)
        dst_ref[rows, :] = src_ref[rows, :].astype(BF16)
        return carry

    lax.fori_loop(0, src_ref.shape[0] // piece, body, 0)


def _experts_up_kernel(tile_ref, chunk_ref, exp_ref, first_ref, valid_ref, x_ref, wg_ref, wu_ref, act_ref,
                       wgb_ref, wub_ref):
    s = pl.program_id(0)

    @pl.when(first_ref[s] == 1)
    def _():
        _cast_rows(wg_ref, wgb_ref, 128)
        _cast_rows(wu_ref, wub_ref, 128)

    @pl.when(valid_ref[s] == 1)
    def _():
        _swiglu_act(x_ref[...].astype(BF16), wgb_ref, wub_ref, act_ref)


def _experts_down_kernel(act_tile_ref, exp_ref, first_ref, valid_ref, act_ref, wd_ref, o_ref, wdb_ref):
    i = pl.program_id(0)

    @pl.when(first_ref[i] == 1)
    def _():
        _cast_rows(wd_ref, wdb_ref, 256)

    @pl.when(valid_ref[i] == 1)
    def _():
        o_ref[...] = _dot(act_ref[...], wdb_ref[...])

    @pl.when(valid_ref[i] == 0)
    def _():
        o_ref[...] = jnp.zeros_like(o_ref)


def _experts(up_tables, down_tables, xs, w_gate, w_up, w_down):
    up = pl.pallas_call(
        _experts_up_kernel,
        grid_spec=pltpu.PrefetchScalarGridSpec(
            num_scalar_prefetch=5,
            grid=(NC_E * NT_E,),
            in_specs=[
                pl.BlockSpec((TME, D_MODEL), lambda s, t, c, e, f, v: (t[s], 0)),
                pl.BlockSpec((None, D_MODEL, TF_E), lambda s, t, c, e, f, v: (e[s], 0, c[s])),
                pl.BlockSpec((None, D_MODEL, TF_E), lambda s, t, c, e, f, v: (e[s], 0, c[s])),
            ],
            out_specs=pl.BlockSpec((TME, TF_E), lambda s, t, c, e, f, v: (t[s], c[s])),
            scratch_shapes=[pltpu.VMEM((D_MODEL, TF_E), BF16), pltpu.VMEM((D_MODEL, TF_E), BF16)],
        ),
        out_shape=jax.ShapeDtypeStruct((NT_E * TME, D_FF_E), BF16),
        compiler_params=_cparams(("arbitrary",)),
        name="experts_up",
    )(*up_tables, xs, w_gate, w_up)
    return pl.pallas_call(
        _experts_down_kernel,
        grid_spec=pltpu.PrefetchScalarGridSpec(
            num_scalar_prefetch=4,
            grid=(NT_E,),
            in_specs=[
                pl.BlockSpec((TME, D_FF_E), lambda i, a, e, f, v: (a[i], 0)),
                pl.BlockSpec((None, D_FF_E, D_MODEL), lambda i, a, e, f, v: (e[i], 0, 0)),
            ],
            out_specs=pl.BlockSpec((TME, D_MODEL), lambda i, a, e, f, v: (i, 0)),
            scratch_shapes=[pltpu.VMEM((D_FF_E, D_MODEL), BF16)],
        ),
        out_shape=jax.ShapeDtypeStruct((NT_E * TME, D_MODEL), F32),
        compiler_params=_cparams(("arbitrary",)),
        name="experts_down",
    )(*down_tables, up, w_down)


def _combine_kernel(x_ref, y1_ref, y2_ref, w1_ref, w2_ref, g2_ref, gam_ref, octx_ref, olat_ref):
    rep = D_MODEL // LANES
    w1 = jnp.concatenate([w1_ref[...]] * rep, axis=1)
    w2 = jnp.concatenate([w2_ref[...]] * rep, axis=1)
    x2 = x_ref[...] + g2_ref[...] * (w1 * y1_ref[...] + w2 * y2_ref[...])
    y = _rms(x2, gam_ref[...])
    is_lat = _is_lat(TMB)

    @pl.when(jnp.logical_not(is_lat))
    def _():
        octx_ref[...] = y

    @pl.when(is_lat)
    def _():
        olat_ref[...] = y


def _combine(x1, y1, y2, w1, w2, mod, gamma):
    return pl.pallas_call(
        _combine_kernel,
        grid=(M_TOK // TMB,),
        in_specs=[_row_spec(D_MODEL, tm=TMB), _row_spec(D_MODEL, tm=TMB), _row_spec(D_MODEL, tm=TMB),
                  _row_spec(LANES, tm=TMB), _row_spec(LANES, tm=TMB), _mod_spec(5, TMB), _const_spec((1, D_MODEL))],
        out_specs=[_ctx_spec(D_MODEL, TMB), _lat_spec(D_MODEL, TMB)],
        out_shape=[jax.ShapeDtypeStruct((N_CTX, D_MODEL), F32), jax.ShapeDtypeStruct((N_LAT, D_MODEL), F32)],
        compiler_params=_cparams(("arbitrary",)),
        name="combine",
    )(x1, y1, y2, w1, w2, mod, gamma)


def _rope_tables(rot_dim):
    n = DEC_SEQ
    rows = np.repeat(np.arange(n // GRID_W, dtype=np.float64), GRID_W)
    cols = np.tile(np.arange(GRID_W, dtype=np.float64), n // GRID_W)
    half = rot_dim // 2
    freqs = ROPE_THETA ** (-np.arange(0, half, 2, dtype=np.float64) / half)
    ang_r, ang_c = rows[:, None] * freqs, cols[:, None] * freqs
    cos = np.concatenate([np.cos(ang_r)] * 2 + [np.cos(ang_c)] * 2, axis=-1)
    sin = np.concatenate([-np.sin(ang_r), np.sin(ang_r), -np.sin(ang_c), np.sin(ang_c)], axis=-1)
    rep = LANES // rot_dim
    return (jnp.asarray(np.tile(cos, (1, rep)).astype(np.float32)),
            jnp.asarray(np.tile(sin, (1, rep)).astype(np.float32)))


def _with_ones(v):
    return jnp.concatenate([v, jnp.ones_like(v)], axis=-1)


def _rows(a, idx):
    return a.at[idx].get(mode="promise_in_bounds")


def _route(idx, counts_f):
    quarter = LANES // 4
    experts = jnp.arange(N_EXPERTS, dtype=jnp.int32)
    e2 = jnp.stack([idx[:, 0], idx[:, quarter]], axis=1)
    rank2 = jnp.stack([idx[:, 2 * quarter], idx[:, 3 * quarter]], axis=1)
    counts = counts_f[0, :N_EXPERTS].astype(jnp.int32)
    tiles = (counts + TME - 1) // TME
    tile_end = jnp.cumsum(tiles)
    tile_start = tile_end - tiles
    start = jnp.cumsum(counts) - counts
    onehot = e2[:, :, None] == experts[None, None, :]
    pos = jnp.sum(jnp.where(onehot, tile_start[None, None, :] * TME, 0), axis=-1) + rank2
    tile_id = jnp.arange(NT_E, dtype=jnp.int32)
    tile_expert = jnp.minimum(jnp.sum((tile_id[:, None] >= tile_end[None, :]).astype(jnp.int32), axis=1),
                              N_EXPERTS - 1).astype(jnp.int32)
    tile_valid = (tile_id < tile_end[-1]).astype(jnp.int32)
    t_oh = tile_expert[:, None] == experts[None, :]
    t_first = jnp.sum(jnp.where(t_oh, (start - tile_start * TME)[None, :], 0), axis=1) + tile_id * TME
    t_last = jnp.sum(jnp.where(t_oh, (start + counts)[None, :], 0), axis=1)
    order = jnp.argsort(e2.reshape(-1), stable=True).astype(jnp.int32)
    n_assign = order.shape[0]
    g_idx = t_first[:, None] + jnp.arange(TME, dtype=jnp.int32)[None, :]
    live = jnp.logical_and(g_idx < t_last[:, None], tile_valid[:, None] == 1)
    src = _rows(order, jnp.clip(g_idx, 0, n_assign - 1).reshape(-1)) // 2
    src_tok = jnp.where(live.reshape(-1), src, 0).astype(jnp.int32)

    n_tiles = tile_end[-1]
    last_tile = jnp.maximum(n_tiles - 1, 0)
    last_expert = jnp.sum(jnp.where(tile_id == last_tile, tile_expert, 0))
    t_tstart = jnp.sum(jnp.where(t_oh, tile_start[None, :], 0), axis=1)
    down_tables = (
        jnp.where(tile_valid == 1, tile_id, last_tile).astype(jnp.int32),
        tile_expert,
        jnp.logical_and(tile_valid == 1, tile_id == t_tstart).astype(jnp.int32),
        tile_valid,
    )
    s_id = jnp.arange(NC_E * NT_E, dtype=jnp.int32)
    s_exp = jnp.minimum(jnp.sum((s_id[:, None] >= NC_E * tile_end[None, :]).astype(jnp.int32), axis=1),
                        N_EXPERTS - 1)
    s_oh = s_exp[:, None] == experts[None, :]
    s_tstart = jnp.sum(jnp.where(s_oh, tile_start[None, :], 0), axis=1)
    s_ntile = jnp.maximum(jnp.sum(jnp.where(s_oh, tiles[None, :], 0), axis=1), 1)
    rel = s_id - NC_E * s_tstart
    s_valid = s_id < NC_E * n_tiles
    up_tables = (
        jnp.where(s_valid, s_tstart + rel % s_ntile, last_tile).astype(jnp.int32),
        jnp.where(s_valid, rel // s_ntile, NC_E - 1).astype(jnp.int32),
        jnp.where(s_valid, s_exp, last_expert).astype(jnp.int32),
        jnp.logical_and(s_valid, rel % s_ntile == 0).astype(jnp.int32),
        s_valid.astype(jnp.int32),
    )
    return src_tok, pos, up_tables, down_tables


def lambda_init(layer):
    return 0.8 - 0.6 * math.exp(-0.3 * layer)


def kernel(x_prompt, x_sample, state_a, cache_b_k, cache_b_v, cache_c_k, cache_c_v, c, c_ctx, w_mod, b_mod, norm_mix, norm_ffn, w_in_even, w_gate2_a, b_gate_a, g_gla, lam_q1, lam_k1, lam_q2, lam_k2, g_sub_b, w_out_even, w_in_odd, g_q_c, g_k_c, w_out_odd, ffn_gate, ffn_up, ffn_down, w_router, exp_gate, exp_up, exp_down, norm_final):
    x_ctx, x_lat = x_prompt.reshape(N_CTX, D_MODEL), x_sample.reshape(N_LAT, D_MODEL)
    cond =jnp.concatenate([c_ctx[None, :], c, jnp.zeros((N_COND - 1 - DEC_BATCH, D_MODEL), F32)], axis=0)
    mod = _modulation(cond, w_mod, b_mod).reshape(DEPTH, N_COND, 1, 6 * D_MODEL)

    w = w_in_even[0]
    na = 2 * A_QK + 2 * A_V
    gate_lo = na
    gate_hi = na + 2 * GATE_RANK
    col_scale = jnp.concatenate([jnp.full((A_QK,), DK_A ** -0.5, F32), jnp.ones((na - A_QK,), F32),
                                 jnp.full((B_QK,), DH_B ** -0.5, F32), jnp.ones((B_QK + B_V,), F32)])
    w_main = (jnp.concatenate([w[:, :gate_lo], w[:, gate_hi:]], axis=1) * col_scale).astype(BF16)
    w_gates = jnp.pad(w[:, gate_lo:gate_hi], ((0, 0), (0, LANES - 2 * GATE_RANK))).astype(BF16)
    g2 = jnp.zeros((LANES, 2 * A_QK), F32)
    g2 = g2.at[:GATE_RANK, :A_QK].set(w_gate2_a[0, 0]).at[GATE_RANK:2 * GATE_RANK, A_QK:].set(w_gate2_a[0, 1])
    bg = b_gate_a[0].reshape(1, 2 * A_QK)
    cos_b, sin_b = _rope_tables(DH_B)
    a_proj, glog, qb, kb, vb, kb_ctx, vb_ctx = _inproj_even(
        x_ctx, x_lat, norm_mix[0][None, :], mod[0], w_main, w_gates, g2.astype(BF16), bg, cos_b, sin_b)

    s0_all = jnp.concatenate([jnp.zeros((1, 2, A_QK, DV_A), F32),
                              state_a[:, 0].reshape(DEC_BATCH, 2, A_QK, DV_A)], axis=0)
    sel = np.arange(A_QK)[:, None] // DK_A == np.arange(LANES)[None, :] // GLA_SB
    m1 = jnp.asarray(sel.astype(np.float32)).astype(BF16)
    o_f, o_b, s_fin = _gla(a_proj, glog, s0_all, m1)

    k_lat = jnp.concatenate([kb[N_CTX:].reshape(DEC_BATCH, DEC_SEQ, B_QK),
                             cache_b_k[:, 0].reshape(DEC_BATCH, PAST_LEN, B_QK).astype(BF16)], axis=1)
    v_lat = jnp.concatenate([vb[N_CTX:].reshape(DEC_BATCH, DEC_SEQ, 2 * B_V),
                             _with_ones(cache_b_v[:, 0].astype(BF16)).reshape(DEC_BATCH, PAST_LEN, 2 * B_V)], axis=1)
    lam_params = [p[0][None, :] for p in (lam_q1, lam_k1, lam_q2, lam_k2)]
    attn_b = _diffattn(qb, kb, vb, k_lat, v_lat, lam_params, g_sub_b[0][None, :], lambda_init(0))

    x = _mix_ffn_even(o_f, o_b, a_proj, *attn_b, x_ctx, x_lat, mod[0], g_gla[0][None, :],
                      w_out_even[0].astype(BF16), norm_ffn[0][None, :], ffn_gate[0].astype(BF16),
                      ffn_up[0].astype(BF16), ffn_down[0].astype(BF16))

    cos_c, sin_c = _rope_tables(DH_C)
    q_c, k_c, v_c, kc_ctx, vc_ctx = _inproj_odd(x, norm_mix[1][None, :], mod[1], w_in_odd[0].astype(BF16),
                                     g_q_c[0][None, :], g_k_c[0][None, :], cos_c, sin_c)
    nkv = HKV_C * DH_C
    k_lat = jnp.concatenate([k_c[N_CTX:].reshape(DEC_BATCH, DEC_SEQ, nkv),
                             cache_c_k[:, 0].reshape(DEC_BATCH, PAST_LEN, nkv).astype(BF16)], axis=1)
    v_lat = jnp.concatenate([v_c[N_CTX:].reshape(DEC_BATCH, DEC_SEQ, 2 * nkv),
                             _with_ones(cache_c_v[:, 0].astype(BF16)).reshape(DEC_BATCH, PAST_LEN, 2 * nkv)], axis=1)
    attn_c = _gqa(q_c, k_c, v_c, k_lat, v_lat)
    w_r = jnp.pad(w_router[0], ((0, 0), (0, LANES - N_EXPERTS)))
    x1, h_moe, w1, w2, idx, counts = _postmix_odd(*attn_c, x, mod[1], w_out_odd[0].astype(BF16),
                                                  norm_ffn[1][None, :], w_r)

    src_tok, pos, up_tables, down_tables = _route(idx, counts)
    xs = _rows(h_moe, src_tok)
    ys = _experts(up_tables, down_tables, xs, exp_gate[0], exp_up[0], exp_down[0])
    y1 = _rows(ys, pos[:, 0])
    y2 = _rows(ys, pos[:, 1])
    y_ctx, y_lat = _combine(x1, y1, y2, w1, w2, mod[1], norm_final[None, :])

    y_prompt = y_ctx.reshape(BATCH, SEQ, D_MODEL)
    y_sample = y_lat.reshape(DEC_BATCH, DEC_SEQ, D_MODEL)
    new_state_a = s_fin.reshape(BATCH, 1, 2, H_A, DK_A, DV_A)
    new_b_k = jnp.transpose(kb_ctx.reshape(BATCH, 1, H_B, 2, DH_B, SEQ), (0, 1, 5, 2, 3, 4))
    new_b_v = vb_ctx.reshape(BATCH, 1, SEQ, H_B, DV_B)
    new_c_k = kc_ctx.reshape(BATCH, 1, SEQ, HKV_C, DH_C)
    new_c_v = vc_ctx.reshape(BATCH, 1, SEQ, HKV_C, DH_C)
    return (y_prompt, y_sample, new_state_a, new_b_k, new_b_v, new_c_k, new_c_v)
```

```python
import functools
import math

import numpy as np
import jax
import jax.numpy as jnp
from jax import lax
from jax.experimental import pallas as pl
from jax.experimental.pallas import tpu as pltpu

D_MODEL = 1024
BATCH = 32
SEQ = 256
DEPTH = 2
DEC_BATCH = 2
DEC_SEQ = 2048
PAST_LEN = 512
GRID_W = 64
H_A, DK_A, DV_A = 4, 64, 128
GATE_RANK = 16
GATE_TAU = 16.0
H_B, DH_B = 4, 64
DV_B = 2 * DH_B
H_C, HKV_C, DH_C = 8, 2, 128
D_FF = 2816
N_EXPERTS = 8
D_FF_E = 3584
ROPE_THETA = 10000.0
EPS = 1e-6
A_QK = H_A * DK_A
A_V = H_A * DV_A
B_QK = H_B * 2 * DH_B
B_V = H_B * DV_B

F32 = jnp.float32
BF16 = jnp.bfloat16
LOG2E = math.log2(math.e)

V7X_VMEM_BYTES = 64 * 1024 * 1024
VMEM_LIMIT = V7X_VMEM_BYTES * 7 // 8
LANES = 128
MXU_N = 256
TM = 256
N_CTX = BATCH * SEQ
N_LAT = DEC_BATCH * DEC_SEQ
M_TOK = N_CTX + N_LAT
NB = M_TOK // TM
NCB = N_CTX // TM
LBB = DEC_SEQ // TM
N_COND = 8
GLA_C = 64
GLA_SB = 8
TME = 512
NT_E = 2 * M_TOK // TME + N_EXPERTS
NC_E = 2
TF_E = D_FF_E // NC_E
TQ_LAT = 256
TMB = 512

assert SEQ == TM and DEC_SEQ % TM == 0 and DEPTH == 2


def _cparams(sem):
    return pltpu.CompilerParams(dimension_semantics=sem, vmem_limit_bytes=VMEM_LIMIT)


def _group(i, tm=TM):
    return jnp.where(i < N_CTX // tm, 0, 1 + (i - N_CTX // tm) // (DEC_SEQ // tm))


def _pos_block(i, tm=TM):
    return jnp.where(i < N_CTX // tm, 0, (i - N_CTX // tm) % (DEC_SEQ // tm))


def _mod_spec(chunk, tm=TM):
    return pl.BlockSpec((None, 1, D_MODEL), lambda i, c=chunk: (_group(i, tm), 0, c))


def _row_spec(width, col=0, tm=TM):
    return pl.BlockSpec((tm, width), lambda i, c=col: (i, c))


def _ctx_spec(width, tm=TM):
    return pl.BlockSpec((tm, width), lambda i: (jnp.minimum(i, N_CTX // tm - 1), 0))


def _ctx_parts_spec(parts, tm=TM):
    return pl.BlockSpec((tm * parts, LANES), lambda i: (jnp.minimum(i, N_CTX // tm - 1), 0))


def _lat_spec(width, tm=TM):
    return pl.BlockSpec((tm, width), lambda i: (jnp.maximum(i - N_CTX // tm, 0), 0))


def _const_spec(shape):
    nd = len(shape)
    return pl.BlockSpec(shape, lambda i, nd=nd: (0,) * nd, pipeline_mode=pl.Buffered(1))


def _dot(a, b):
    return jnp.dot(a, b, preferred_element_type=F32)


def _dot_nt(a, b):
    return lax.dot_general(a, b, (((1,), (1,)), ((), ())), preferred_element_type=F32)


def _split2(a):
    hi = a.astype(BF16)
    lo = (a - hi.astype(F32)).astype(BF16)
    return hi, lo


def _dot_hi(a, b):
    a_hi, a_lo = _split2(a)
    b_hi, b_lo = _split2(b)
    return _dot(a_hi, b_hi) + (_dot(a_hi, b_lo) + _dot(a_lo, b_hi))


def _silu(x):
    return (0.5 * x) * (1.0 + jnp.tanh(0.5 * x))


def _log_sigmoid(x):
    return jnp.minimum(x, 0.0) - jnp.log(1.0 + jnp.exp(-jnp.abs(x)))


def _rms(x, g):
    return x * lax.rsqrt(jnp.mean(x * x, axis=-1, keepdims=True) + EPS) * g


def _norm_mod(x, gamma, sc, sh):
    return _rms(x, gamma) * (1.0 + sc) + sh


def _staged(parts):
    live = list(parts)
    while live:
        live = [g for g in live if next(g, True) is None]


def _copy_ctx_transposed(stage3_ref, dst_ref):
    @pl.when(pl.program_id(0) < NCB)
    def _():
        def body(j, carry):
            dst_ref[pl.ds(pl.multiple_of(j * LANES, LANES), LANES), :] = jnp.transpose(stage3_ref[j])
            return carry

        lax.fori_loop(0, stage3_ref.shape[0], body, 0)


def _is_lat(tm):
    return pl.program_id(0) >= N_CTX // tm


def _pick_rows(ctx_ref, lat_ref):
    return jnp.where(_is_lat(ctx_ref.shape[0]), lat_ref[...], ctx_ref[...])


def _copy_ctx(stage_ref, outs):
    piece = 64

    @pl.when(jnp.logical_not(_is_lat(stage_ref.shape[0])))
    def _():
        def body(r, carry):
            rows = pl.ds(pl.multiple_of(r * piece, piece), piece)
            for lo, width, parts, dst_ref in outs:
                for p in range(parts):
                    val = stage_ref[rows, lo + p * width:lo + (p + 1) * width]
                    if width < LANES:
                        val = jnp.concatenate([val, jnp.zeros((piece, LANES - width), F32)], axis=1)
                    dst_ref[pl.ds(pl.multiple_of(r * piece * parts, piece * parts) + p, piece, stride=parts), :] = val
            return carry

        lax.fori_loop(0, stage_ref.shape[0] // piece, body, 0)


def _rope(x, cos, sin, half):
    lane = lax.broadcasted_iota(jnp.int32, x.shape, 1)
    first = (lane % (2 * half)) < half
    swapped = jnp.where(first, pltpu.roll(x, LANES - half, 1), pltpu.roll(x, half, 1))
    return x * cos + swapped * sin


def _mod_kernel(cond_ref, w_ref, b_ref, o_ref):
    o_ref[...] = _dot_hi(_silu(cond_ref[...]), w_ref[...]) + b_ref[...]


def _modulation(cond, w_mod, b_mod):
    tn = 1536
    return pl.pallas_call(
        _mod_kernel,
        grid=(DEPTH, 6 * D_MODEL // tn),
        in_specs=[
            pl.BlockSpec((N_COND, D_MODEL), lambda l, j: (0, 0)),
            pl.BlockSpec((None, D_MODEL, tn), lambda l, j: (l, 0, j)),
            pl.BlockSpec((None, 1, tn), lambda l, j: (l, 0, j)),
        ],
        out_specs=pl.BlockSpec((None, N_COND, tn), lambda l, j: (l, 0, j)),
        out_shape=jax.ShapeDtypeStruct((DEPTH, N_COND, 6 * D_MODEL), F32),
        compiler_params=_cparams(("arbitrary", "arbitrary")),
        name="modulation",
    )(cond, w_mod, b_mod.reshape(DEPTH, 1, 6 * D_MODEL))


def _inproj_even_kernel(xc_ref, xl_ref, gam_ref, sh_ref, sc_ref, w_ref, wg_ref, g2_ref, bg_ref, cos_ref, sin_ref,
                        a_ref, glog_ref, qb_ref, kb_ref, vb_ref, kctx_ref, vctx_ref, stage_ref, stagek_ref):
    is_lat = pl.program_id(0) >= NCB
    na = 2 * A_QK + 2 * A_V

    def part(rows):
        n = rows.stop - rows.start
        x = jnp.where(is_lat, xl_ref[rows, :], xc_ref[rows, :])
        hb = _norm_mod(x, gam_ref[...], sc_ref[...], sh_ref[...]).astype(BF16)
        cos = jnp.where(is_lat, cos_ref[rows, :], 1.0)
        sin = jnp.where(is_lat, sin_ref[rows, :], 0.0)
        yield
        for lo in range(0, na, MXU_N):
            a_ref[rows, lo:lo + MXU_N] = _dot(hb, w_ref[:, lo:lo + MXU_N])
        for lo in range(0, B_QK, MXU_N):
            zq = _dot(hb, w_ref[:, na + lo:na + lo + MXU_N])
            zk = _dot(hb, w_ref[:, na + B_QK + lo:na + B_QK + lo + MXU_N])
            zv = _dot(hb, w_ref[:, na + 2 * B_QK + lo:na + 2 * B_QK + lo + MXU_N])
            stage_ref[rows, lo:lo + MXU_N] = zv
            for j in range(0, MXU_N, LANES):
                cs = slice(lo + j, lo + j + LANES)
                vb_ref[rows, 2 * (lo + j):2 * (lo + j) + DV_B] = zv[:, j:j + LANES].astype(BF16)
                vb_ref[rows, 2 * (lo + j) + DV_B:2 * (lo + j + LANES)] = jnp.ones((n, DV_B), BF16)
                stagek_ref[(lo + j) // LANES, rows, :] = zk[:, j:j + LANES]
                qb_ref[rows, cs] = (_rope(zq[:, j:j + LANES], cos, sin, DH_B // 4) * LOG2E).astype(BF16)
                kb_ref[rows, cs] = _rope(zk[:, j:j + LANES], cos, sin, DH_B // 4).astype(BF16)
        gates = _dot(hb, wg_ref[...])
        xg = _dot(gates.astype(BF16), g2_ref[...]) + bg_ref[...]
        glog_ref[rows, :] = _log_sigmoid(xg) * (1.0 / GATE_TAU)

    _staged([part(slice(0, TM))])
    _copy_ctx(stage_ref, ((0, DV_B, H_B, vctx_ref),))
    _copy_ctx_transposed(stagek_ref, kctx_ref)


def _inproj_even(x_ctx, x_lat, gamma, mod, w_main, w_gates, g2, bg, cos, sin):
    na = 2 * A_QK + 2 * A_V
    nz = na + 2 * B_QK + B_V
    return pl.pallas_call(
        _inproj_even_kernel,
        grid=(NB,),
        in_specs=[
            _ctx_spec(D_MODEL), _lat_spec(D_MODEL), _const_spec((1, D_MODEL)), _mod_spec(0), _mod_spec(1),
            _const_spec((D_MODEL, nz)), _const_spec((D_MODEL, LANES)), _const_spec((LANES, 2 * A_QK)),
            _const_spec((1, 2 * A_QK)),
            pl.BlockSpec((TM, LANES), lambda i: (_pos_block(i), 0)),
            pl.BlockSpec((TM, LANES), lambda i: (_pos_block(i), 0)),
        ],
        out_specs=[_pair_spec(na), _pair_spec(2 * A_QK), _row_spec(B_QK), _row_spec(B_QK), _row_spec(2 * B_V),
                   pl.BlockSpec((None, B_QK, TM), lambda i: (jnp.minimum(i, NCB - 1), 0, 0)), _ctx_parts_spec(H_B)],
        out_shape=[
            jax.ShapeDtypeStruct((M_TOK, na), F32), jax.ShapeDtypeStruct((M_TOK, 2 * A_QK), F32),
            jax.ShapeDtypeStruct((M_TOK, B_QK), BF16), jax.ShapeDtypeStruct((M_TOK, B_QK), BF16),
            jax.ShapeDtypeStruct((M_TOK, 2 * B_V), BF16), jax.ShapeDtypeStruct((BATCH, B_QK, SEQ), F32),
            jax.ShapeDtypeStruct((N_CTX * H_B, LANES), F32),
        ],
        scratch_shapes=[pltpu.VMEM((TM, B_V), F32), pltpu.VMEM((B_QK // LANES, TM, LANES), F32)],
        compiler_params=_cparams(("arbitrary",)),
        name="inproj_even",
    )(x_ctx, x_lat, gamma, mod, mod, w_main, w_gates, g2, bg, cos, sin)


def _gla_chunk(q_ref, k_ref, v_ref, g_ref, o_ref, c, s_ref, a_ref, b_ref, m1_ref, head_mask, same_block, rev):
    q, k, v = q_ref[c], k_ref[c], v_ref[c]
    g = g_ref[c] * LOG2E
    v_b = v.astype(BF16)
    r_i = lax.broadcasted_iota(jnp.int32, (GLA_C, GLA_C), 0)
    c_i = lax.broadcasted_iota(jnp.int32, (GLA_C, GLA_C), 1)
    tri = jnp.where((c_i >= r_i) if rev else (c_i <= r_i), 1.0, 0.0).astype(BF16)
    g_hi = g.astype(BF16)
    g_r = g - g_hi.astype(F32)
    g_mid = g_r.astype(BF16)
    g_lo = (g_r - g_mid.astype(F32)).astype(BF16)
    bc = _dot(tri, g_hi) + (_dot(tri, g_mid) + _dot(tri, g_lo))
    b_ref[...] = bc
    yield
    t_i = lax.broadcasted_iota(jnp.int32, (GLA_SB, A_QK), 0)
    group = 2 * GLA_SB
    for i0 in range(0, GLA_C, group):
        blocks = range(i0, i0 + group, GLA_SB)
        q_i = [q_ref[c, i:i + GLA_SB, :] for i in blocks]
        b_i = [b_ref[i:i + GLA_SB, :] for i in blocks]
        for j in range(GLA_SB):
            keep = (t_i <= j) if rev else (t_i >= j)
            pair = []
            for n, i in enumerate(blocks):
                k_s = k_ref[c, i + j:i + j + 1, :]
                b_s = b_ref[i + j:i + j + 1, :]
                pair.append(jnp.where(keep, q_i[n] * k_s * jnp.exp2(b_i[n] - b_s), 0.0))
            a_ref[j * GLA_C + i0:j * GLA_C + i0 + group, :] = jnp.concatenate(pair, axis=0).astype(BF16)
    yield
    last = bc[0:1] if rev else bc[GLA_C - 1:GLA_C]
    qc = q * jnp.exp2(bc)
    kl = k * jnp.exp2(last - bc)
    row = lax.broadcasted_iota(jnp.int32, (GLA_C, A_QK), 0)

    def expand(x):
        return jnp.where(head_mask, jnp.concatenate([x] * H_A, axis=0), 0.0).astype(BF16)

    scores = None
    half = GLA_C // 2
    while half >= GLA_SB:
        mids = [bc[b + half:b + half + 1] if rev else bc[b + half - 1:b + half] for b in range(0, GLA_C, 2 * half)]
        ref = mids[0] if len(mids) == 1 else jnp.concatenate(
            [jnp.broadcast_to(m, (2 * half, A_QK)) for m in mids], axis=0)
        upper = (row // half) % 2 == 1
        later, earlier = (jnp.logical_not(upper), upper) if rev else (upper, jnp.logical_not(upper))
        q_l = jnp.where(later, q * jnp.exp2(bc - ref), 0.0)
        k_e = jnp.where(earlier, k * jnp.exp2(ref - bc), 0.0).astype(BF16)
        term = _dot_nt(expand(q_l), k_e)
        if 2 * half < GLA_C:
            term = jnp.where(same_block[2 * half], term, 0.0)
        scores = term if scores is None else scores + term
        half //= 2
    s_old = s_ref[...]
    inter = _dot(expand(qc), s_old.astype(BF16))
    sums = _dot(a_ref[...], m1_ref[...])
    yield
    lane_c = lax.broadcasted_iota(jnp.int32, (GLA_C, LANES), 1)
    within = None
    for j in range(GLA_SB):
        part = jnp.where(lane_c % GLA_SB == j, sums[j * GLA_C:(j + 1) * GLA_C], 0.0)
        within = part if within is None else within + part
    lane = lax.broadcasted_iota(jnp.int32, (GLA_SB, LANES), 1)
    per_head = []
    for h in range(H_A):
        rows_h = []
        for i in range(0, GLA_C, GLA_SB):
            moved = pltpu.roll(within[i:i + GLA_SB], (i - h * GLA_SB) % LANES, 1)
            rows_h.append(jnp.where(lane // GLA_SB == i // GLA_SB, moved, 0.0))
        per_head.append(jnp.concatenate(rows_h, axis=0)[:, :GLA_C])
    scores = scores + jnp.concatenate(per_head, axis=0)
    o_heads = []
    for h in range(H_A):
        hs = slice(h * GLA_C, (h + 1) * GLA_C)
        o_heads.append(inter[hs] + _dot(scores[hs].astype(BF16), v_b[:, h * DV_A:(h + 1) * DV_A]))
    o_ref[c] = jnp.concatenate(o_heads, axis=1)
    yield
    t = jnp.transpose(jnp.concatenate([kl, jnp.broadcast_to(last, (GLA_C, A_QK))], axis=0))
    kv = _dot(t[:, :GLA_C].astype(BF16), v_b)
    a_col = jnp.exp2(t[:, GLA_C:GLA_C + 1])
    for h in range(H_A):
        hs = slice(h * DK_A, (h + 1) * DK_A)
        s_ref[hs, :] = a_col[hs] * s_old[hs] + kv[hs, h * DV_A:(h + 1) * DV_A]


def _gla_kernel(fblk_ref, bblk_ref, first_ref, inita_ref, initb_ref,
                qf_ref, kf_ref, vf_ref, gf_ref, qr_ref, kr_ref, vr_ref, gr_ref, s0a_ref, s0b_ref, m1_ref,
                of_ref, ob_ref, sfin_ref, s_refs, a_refs, b_refs):
    step = pl.program_id(0)

    @pl.when(first_ref[step] == 1)
    def _():
        for seq, s0_ref in enumerate((s0a_ref, s0b_ref)):
            for d in range(2):
                s_refs[2 * seq + d] = s0_ref[d]

    r_h = lax.broadcasted_iota(jnp.int32, (H_A * GLA_C, A_QK), 0) // GLA_C
    c_h = lax.broadcasted_iota(jnp.int32, (H_A * GLA_C, A_QK), 1) // DK_A
    head_mask = r_h == c_h
    t_s = lax.broadcasted_iota(jnp.int32, (H_A * GLA_C, GLA_C), 0) % GLA_C
    s_s = lax.broadcasted_iota(jnp.int32, (H_A * GLA_C, GLA_C), 1)
    same_block = {}
    size = 2 * GLA_SB
    while size < GLA_C:
        same_block[size] = (t_s // size) == (s_s // size)
        size *= 2
    nch = TM // GLA_C

    def body(c, carry):
        chains = []
        for seq in range(2):
            chains.append(_gla_chunk(qf_ref, kf_ref, vf_ref, gf_ref, of_ref, seq * nch + c, s_refs.at[2 * seq],
                                     a_refs.at[2 * seq], b_refs.at[2 * seq], m1_ref, head_mask, same_block, False))
            chains.append(_gla_chunk(qr_ref, kr_ref, vr_ref, gr_ref, ob_ref, seq * nch + nch - 1 - c,
                                     s_refs.at[2 * seq + 1], a_refs.at[2 * seq + 1], b_refs.at[2 * seq + 1], m1_ref,
                                     head_mask, same_block, True))
        _staged(chains)
        return carry

    lax.fori_loop(0, nch, body, 0)

    @pl.when(step < BATCH // 2)
    def _():
        for seq in range(2):
            for d in range(2):
                sfin_ref[seq, d] = s_refs[2 * seq + d]


def _pair_block(i):
    j = i - NCB
    return jnp.where(i < NCB, i, NCB + 2 * (j % LBB) + j // LBB)


def _pair_spec(width, col=0):
    return pl.BlockSpec((TM, width), lambda i, c=col: (_pair_block(i), c))


def _gla_tables():
    fblk, bblk, first, init_a, init_b = [], [], [], [], []
    for p in range(BATCH // 2):
        fblk.append(p), bblk.append(p), first.append(1), init_a.append(0), init_b.append(0)
    for j in range(LBB):
        fblk.append(NCB // 2 + j)
        bblk.append(NCB // 2 + LBB - 1 - j)
        first.append(1 if j == 0 else 0)
        init_a.append(1), init_b.append(2)
    return [jnp.asarray(np.array(t, np.int32)) for t in (fblk, bblk, first, init_a, init_b)]


def _gla(a_proj, glog, s0_all, m1):
    tables = _gla_tables()
    nsteps = int(tables[0].shape[0])
    nseq = BATCH + DEC_BATCH

    def fmap(col):
        return lambda i, fb, bb, fi, ia, ib: (fb[i], 0, col)

    def rmap(col):
        return lambda i, fb, bb, fi, ia, ib: (bb[i], 0, col)

    nch = 2 * TM // GLA_C
    a3 = a_proj.reshape(M_TOK // GLA_C, GLA_C, a_proj.shape[1])
    g3 = glog.reshape(M_TOK // GLA_C, GLA_C, glog.shape[1])
    gs = pltpu.PrefetchScalarGridSpec(
        num_scalar_prefetch=5,
        grid=(nsteps,),
        in_specs=[
            pl.BlockSpec((nch, GLA_C, A_QK), fmap(0)), pl.BlockSpec((nch, GLA_C, A_QK), fmap(1)),
            pl.BlockSpec((nch, GLA_C, A_V), fmap(1)), pl.BlockSpec((nch, GLA_C, A_QK), fmap(0)),
            pl.BlockSpec((nch, GLA_C, A_QK), rmap(0)), pl.BlockSpec((nch, GLA_C, A_QK), rmap(1)),
            pl.BlockSpec((nch, GLA_C, A_V), rmap(1)), pl.BlockSpec((nch, GLA_C, A_QK), rmap(1)),
            pl.BlockSpec((None, 2, A_QK, DV_A), lambda i, fb, bb, fi, ia, ib: (ia[i], 0, 0, 0)),
            pl.BlockSpec((None, 2, A_QK, DV_A), lambda i, fb, bb, fi, ia, ib: (ib[i], 0, 0, 0)),
            pl.BlockSpec((A_QK, LANES), lambda i, fb, bb, fi, ia, ib: (0, 0)),
        ],
        out_specs=[
            pl.BlockSpec((nch, GLA_C, A_V), fmap(0)), pl.BlockSpec((nch, GLA_C, A_V), rmap(0)),
            pl.BlockSpec((2, 2, A_QK, DV_A),
                         lambda i, fb, bb, fi, ia, ib: (jnp.minimum(fb[i], BATCH // 2 - 1), 0, 0, 0)),
        ],
        scratch_shapes=[
            pltpu.VMEM((4, A_QK, DV_A), F32),
            pltpu.VMEM((4, GLA_SB * GLA_C, A_QK), BF16),
            pltpu.VMEM((4, GLA_C, A_QK), F32),
        ],
    )
    o_f, o_b, s_fin = pl.pallas_call(
        _gla_kernel,
        grid_spec=gs,
        out_shape=[
            jax.ShapeDtypeStruct((M_TOK // GLA_C, GLA_C, A_V), F32),
            jax.ShapeDtypeStruct((M_TOK // GLA_C, GLA_C, A_V), F32),
            jax.ShapeDtypeStruct((BATCH, 2, A_QK, DV_A), F32),
        ],
        compiler_params=_cparams(("arbitrary",)),
        name="gla",
    )(*tables, a3, a3, a3, g3, a3, a3, a3, g3, s0_all, s0_all, m1)
    return o_f.reshape(M_TOK, A_V), o_b.reshape(M_TOK, A_V), s_fin


def _attend(q_b, kv):
    scores = [_dot_nt(q_b, k_b) for k_b, _ in kv]
    top = functools.reduce(jnp.maximum, [jnp.max(s2, axis=-1, keepdims=True) for s2 in scores])
    both = None
    for s2, (_, v_ones) in zip(scores, kv):
        part = _dot(jnp.exp2(s2 - top).astype(BF16), v_ones)
        both = part if both is None else both + part
    dv = both.shape[1] // 2
    return both[:, :dv] * (1.0 / both[:, dv:])


def _diffattn_body(q_ref, kv_refs, lq1_ref, lk1_ref, lq2_ref, lk2_ref, gsub_ref, o_ref, lam_init):
    lam = (jnp.exp(jnp.sum(lq1_ref[...] * lk1_ref[...], axis=-1, keepdims=True))
           - jnp.exp(jnp.sum(lq2_ref[...] * lk2_ref[...], axis=-1, keepdims=True)) + lam_init)
    lane = lax.broadcasted_iota(jnp.int32, (q_ref.shape[0], LANES), 1)
    for h in range(H_B):
        hs = slice(h * LANES, (h + 1) * LANES)
        q_h = q_ref[:, hs]
        kv = [(k_ref[:, hs], v_ref[:, 2 * h * DV_B:2 * (h + 1) * DV_B]) for k_ref, v_ref in kv_refs]
        zero = jnp.zeros_like(q_h)
        o = (_attend(jnp.where(lane < DH_B, q_h, zero), kv)
             - lam * _attend(jnp.where(lane >= DH_B, q_h, zero), kv))
        o_ref[:, hs] = (_rms(o, gsub_ref[...]) * (1.0 - lam_init)).astype(BF16)


def _diffattn_kernel(q_ref, k_ref, v_ref, lq1, lk1, lq2, lk2, gsub, o_ref, *, lam_init):
    _diffattn_body(q_ref, [(k_ref, v_ref)], lq1, lk1, lq2, lk2, gsub, o_ref, lam_init)


def _diffattn_cached_kernel(q_ref, k_ref, v_ref, kc_ref, vc_ref, lq1, lk1, lq2, lk2, gsub, o_ref, *, lam_init):
    _diffattn_body(q_ref, [(k_ref, v_ref), (kc_ref, vc_ref)], lq1, lk1, lq2, lk2, gsub, o_ref, lam_init)


def _diffattn(qb, kb, vb, k_cache, v_cache, lam_params, gsub, lam_init):
    small = [pl.BlockSpec((1, DH_B), lambda *_: (0, 0))] * 4 + [pl.BlockSpec((1, DV_B), lambda *_: (0, 0))]
    body = functools.partial(_diffattn_kernel, lam_init=lam_init)
    ctx = pl.pallas_call(
        body,
        grid=(NCB,),
        in_specs=[_row_spec(B_QK), _row_spec(B_QK), _row_spec(2 * B_V)] + small,
        out_specs=_row_spec(B_V),
        out_shape=jax.ShapeDtypeStruct((N_CTX, B_V), BF16),
        compiler_params=_cparams(("arbitrary",)),
        name="diffattn_ctx",
    )(qb, kb, vb, *lam_params, gsub)
    nqb = DEC_SEQ // TQ_LAT
    off = N_CTX // TQ_LAT
    seq0 = N_CTX // DEC_SEQ
    lat = pl.pallas_call(
        functools.partial(_diffattn_cached_kernel, lam_init=lam_init),
        grid=(DEC_BATCH, nqb),
        in_specs=[
            pl.BlockSpec((TQ_LAT, B_QK), lambda b, j: (off + b * nqb + j, 0)),
            pl.BlockSpec((DEC_SEQ, B_QK), lambda b, j: (seq0 + b, 0)),
            pl.BlockSpec((DEC_SEQ, 2 * B_V), lambda b, j: (seq0 + b, 0)),
            pl.BlockSpec((None, PAST_LEN, B_QK), lambda b, j: (b, 0, 0)),
            pl.BlockSpec((None, PAST_LEN, 2 * B_V), lambda b, j: (b, 0, 0)),
        ] + small,
        out_specs=pl.BlockSpec((TQ_LAT, B_V), lambda b, j: (b * nqb + j, 0)),
        out_shape=jax.ShapeDtypeStruct((N_LAT, B_V), BF16),
        compiler_params=_cparams(("arbitrary", "arbitrary")),
        name="diffattn_lat",
    )(qb, kb, vb, k_cache, v_cache, *lam_params, gsub)
    return ctx, lat


def _swiglu_act(xb, wg_ref, wu_ref, act_ref):
    for lo in range(0, act_ref.shape[1], MXU_N):
        cs = slice(lo, lo + MXU_N)
        act_ref[:, cs] = (_silu(_dot(xb, wg_ref[:, cs])) * _dot(xb, wu_ref[:, cs])).astype(BF16)


def _mix_ffn_even_kernel(of_ref, ob_ref, ra_ref, actx_ref, alat_ref, xc_ref, xl_ref, g1_ref, ggla_ref, wo_ref,
                         gam_ref, sh_ref, sc_ref, g2_ref, wg_ref, wu_ref, wd_ref, o_ref, act_ref):
    is_lat = _is_lat(TM)

    def part(rows):
        heads = []
        for h in range(H_A):
            hs = slice(h * DV_A, (h + 1) * DV_A)
            heads.append((_rms(of_ref[rows, hs] + ob_ref[rows, hs], ggla_ref[...])
                          * _silu(ra_ref[rows, hs])).astype(BF16))
        mix = jnp.concatenate(heads + [jnp.where(is_lat, alat_ref[rows, :], actx_ref[rows, :])], axis=1)
        yield
        x = jnp.where(is_lat, xl_ref[rows, :], xc_ref[rows, :]) + g1_ref[...] * _dot(mix, wo_ref[...])
        hb = _norm_mod(x, gam_ref[...], sc_ref[...], sh_ref[...]).astype(BF16)
        yield
        for lo in range(0, D_FF, MXU_N):
            cs = slice(lo, lo + MXU_N)
            act_ref[rows, cs] = (_silu(_dot(hb, wg_ref[:, cs])) * _dot(hb, wu_ref[:, cs])).astype(BF16)
        yield
        o_ref[rows, :] = x + g2_ref[...] * _dot(act_ref[rows, :], wd_ref[...])

    _staged([part(slice(r, r + TM // 2)) for r in range(0, TM, TM // 2)])


def _mix_ffn_even(o_f, o_b, a_proj, attn_ctx, attn_lat, x_ctx, x_lat, mod, g_gla, w_out, gamma, w_gate, w_up, w_down):
    return pl.pallas_call(
        _mix_ffn_even_kernel,
        grid=(NB,),
        in_specs=[_pair_spec(A_V), _pair_spec(A_V), _pair_spec(A_V, 2), _ctx_spec(B_V), _lat_spec(B_V),
                  _ctx_spec(D_MODEL), _lat_spec(D_MODEL), _mod_spec(2), _const_spec((1, DV_A)),
                  _const_spec((A_V + B_V, D_MODEL)),
                  _const_spec((1, D_MODEL)), _mod_spec(3), _mod_spec(4), _mod_spec(5),
                  _const_spec((D_MODEL, D_FF)), _const_spec((D_MODEL, D_FF)), _const_spec((D_FF, D_MODEL))],
        out_specs=_row_spec(D_MODEL),
        out_shape=jax.ShapeDtypeStruct((M_TOK, D_MODEL), F32),
        scratch_shapes=[pltpu.VMEM((TM, D_FF), BF16)],
        compiler_params=_cparams(("arbitrary",)),
        name="mix_ffn_even",
    )(o_f, o_b, a_proj, attn_ctx, attn_lat, x_ctx, x_lat, mod, g_gla, w_out, gamma, mod, mod, mod,
      w_gate, w_up, w_down)


def _inproj_odd_kernel(x_ref, gam_ref, sh_ref, sc_ref, w_ref, gq_ref, gk_ref, cos_ref, sin_ref,
                       q_ref, k_ref, v_ref, kctx_ref, vctx_ref, stage_ref):
    is_lat = _is_lat(TMB)
    scale = DH_C ** -0.5 * LOG2E
    nq, nkv = H_C * DH_C, HKV_C * DH_C

    def part(rows):
        n = rows.stop - rows.start
        hb = _norm_mod(x_ref[rows, :], gam_ref[...], sc_ref[...], sh_ref[...]).astype(BF16)
        cos = jnp.where(is_lat, cos_ref[rows, :], 1.0)
        sin = jnp.where(is_lat, sin_ref[rows, :], 0.0)
        yield
        z = _dot(hb, w_ref[:, nq:nq + nkv])
        for h in range(HKV_C):
            hs = slice(h * DH_C, (h + 1) * DH_C)
            k_n = _rms(z[:, hs], gk_ref[...])
            stage_ref[rows, hs] = k_n
            k_ref[rows, hs] = _rope(k_n, cos, sin, DH_C // 4).astype(BF16)
        v = _dot(hb, w_ref[:, nq + nkv:])
        stage_ref[rows, nkv:] = v
        for h in range(HKV_C):
            v_ref[rows, 2 * h * DH_C:(2 * h + 1) * DH_C] = v[:, h * DH_C:(h + 1) * DH_C].astype(BF16)
            v_ref[rows, (2 * h + 1) * DH_C:2 * (h + 1) * DH_C] = jnp.ones((n, DH_C), BF16)
        for lo in range(0, nq, MXU_N):
            z = _dot(hb, w_ref[:, lo:lo + MXU_N])
            for j in range(0, MXU_N, DH_C):
                q_ref[rows, lo + j:lo + j + DH_C] = (
                    _rope(_rms(z[:, j:j + DH_C], gq_ref[...]), cos, sin, DH_C // 4) * scale).astype(BF16)

    _staged([part(slice(r, r + TM)) for r in range(0, TMB, TM)])
    _copy_ctx(stage_ref, ((0, DH_C, HKV_C, kctx_ref), (nkv, DH_C, HKV_C, vctx_ref)))


def _inproj_odd(x, gamma, mod, w_in, g_q, g_k, cos, sin):
    nq, nkv = H_C * DH_C, HKV_C * DH_C
    return pl.pallas_call(
        _inproj_odd_kernel,
        grid=(M_TOK // TMB,),
        in_specs=[
            _row_spec(D_MODEL, tm=TMB), _const_spec((1, D_MODEL)), _mod_spec(0, TMB), _mod_spec(1, TMB),
            _const_spec((D_MODEL, nq + 2 * nkv)), _const_spec((1, DH_C)), _const_spec((1, DH_C)),
            pl.BlockSpec((TMB, LANES), lambda i: (_pos_block(i, TMB), 0)),
            pl.BlockSpec((TMB, LANES), lambda i: (_pos_block(i, TMB), 0)),
        ],
        out_specs=[_row_spec(nq, tm=TMB), _row_spec(nkv, tm=TMB), _row_spec(2 * nkv, tm=TMB),
                   _ctx_parts_spec(HKV_C, TMB), _ctx_parts_spec(HKV_C, TMB)],
        out_shape=[
            jax.ShapeDtypeStruct((M_TOK, nq), BF16), jax.ShapeDtypeStruct((M_TOK, nkv), BF16),
            jax.ShapeDtypeStruct((M_TOK, 2 * nkv), BF16), jax.ShapeDtypeStruct((N_CTX * HKV_C, LANES), F32),
            jax.ShapeDtypeStruct((N_CTX * HKV_C, LANES), F32),
        ],
        scratch_shapes=[pltpu.VMEM((TMB, 2 * nkv), F32)],
        compiler_params=_cparams(("arbitrary",)),
        name="inproj_odd",
    )(x, gamma, mod, mod, w_in, g_q, g_k, cos, sin)


def _gqa_body(q_ref, kv_refs, o_ref):
    rep = H_C // HKV_C
    tq = q_ref.shape[0]
    for hk in range(HKV_C):
        ks = slice(hk * DH_C, (hk + 1) * DH_C)
        q_g = jnp.concatenate([q_ref[:, (hk * rep + g) * DH_C:(hk * rep + g + 1) * DH_C] for g in range(rep)],
                              axis=0)
        o = _attend(q_g, [(k_ref[:, ks], v_ref[:, 2 * hk * DH_C:2 * (hk + 1) * DH_C]) for k_ref, v_ref in kv_refs])
        for g in range(rep):
            o_ref[:, (hk * rep + g) * DH_C:(hk * rep + g + 1) * DH_C] = o[g * tq:(g + 1) * tq].astype(BF16)


def _gqa_kernel(q_ref, k_ref, v_ref, o_ref):
    _gqa_body(q_ref, [(k_ref, v_ref)], o_ref)


def _gqa_cached_kernel(q_ref, k_ref, v_ref, kc_ref, vc_ref, o_ref):
    _gqa_body(q_ref, [(k_ref, v_ref), (kc_ref, vc_ref)], o_ref)


def _gqa(q, k, v, k_cache, v_cache):
    nq, nkv = H_C * DH_C, HKV_C * DH_C
    ctx = pl.pallas_call(
        _gqa_kernel,
        grid=(NCB,),
        in_specs=[_row_spec(nq), _row_spec(nkv), _row_spec(2 * nkv)],
        out_specs=_row_spec(nq),
        out_shape=jax.ShapeDtypeStruct((N_CTX, nq), BF16),
        compiler_params=_cparams(("arbitrary",)),
        name="gqa_ctx",
    )(q, k, v)
    nqb = DEC_SEQ // TQ_LAT
    off = N_CTX // TQ_LAT
    seq0 = N_CTX // DEC_SEQ
    lat = pl.pallas_call(
        _gqa_cached_kernel,
        grid=(DEC_BATCH, nqb),
        in_specs=[
            pl.BlockSpec((TQ_LAT, nq), lambda b, j: (off + b * nqb + j, 0)),
            pl.BlockSpec((DEC_SEQ, nkv), lambda b, j: (seq0 + b, 0)),
            pl.BlockSpec((DEC_SEQ, 2 * nkv), lambda b, j: (seq0 + b, 0)),
            pl.BlockSpec((None, PAST_LEN, nkv), lambda b, j: (b, 0, 0)),
            pl.BlockSpec((None, PAST_LEN, 2 * nkv), lambda b, j: (b, 0, 0)),
        ],
        out_specs=pl.BlockSpec((TQ_LAT, nq), lambda b, j: (b * nqb + j, 0)),
        out_shape=jax.ShapeDtypeStruct((N_LAT, nq), BF16),
        compiler_params=_cparams(("arbitrary", "arbitrary")),
        name="gqa_lat",
    )(q, k, v, k_cache, v_cache)
    return ctx, lat


def _postmix_odd_kernel(actx_ref, alat_ref, x_ref, g1_ref, wo_ref, gam_ref, sh_ref, sc_ref, wrh_ref, wrl_ref,
                        x1_ref, h_ref, w1_ref, w2_ref, idx_ref, cnt_ref, run0_ref, run1_ref):
    @pl.when(pl.program_id(0) == 0)
    def _():
        run0_ref[...] = jnp.zeros_like(run0_ref)
        run1_ref[...] = jnp.zeros_like(run1_ref)

    is_lat = _is_lat(TMB)
    lane = lax.broadcasted_iota(jnp.int32, (TMB, LANES), 1)
    lane_f = lane.astype(F32)
    picked = []

    def part(rows):
        attn = jnp.where(is_lat, alat_ref[rows, :], actx_ref[rows, :])
        x1 = x_ref[rows, :] + g1_ref[...] * _dot(attn, wo_ref[...])
        x1_ref[rows, :] = x1
        yield
        h = _norm_mod(x1, gam_ref[...], sc_ref[...], sh_ref[...])
        h_hi, h_lo = _split2(h)
        h_ref[rows, :] = h_hi
        logits = _dot(h_hi, wrh_ref[...]) + (_dot(h_hi, wrl_ref[...]) + _dot(h_lo, wrh_ref[...]))
        yield
        lane_p = lax.broadcasted_iota(jnp.int32, logits.shape, 1).astype(F32)
        lg = jnp.where(lane_p < N_EXPERTS, logits, -jnp.inf)
        m1 = jnp.max(lg, axis=-1, keepdims=True)
        i1 = jnp.min(jnp.where(lg == m1, lane_p, float(LANES)), axis=-1, keepdims=True)
        lg2 = jnp.where(lane_p == i1, -jnp.inf, lg)
        m2 = jnp.max(lg2, axis=-1, keepdims=True)
        i2 = jnp.min(jnp.where(lg2 == m2, lane_p, float(LANES)), axis=-1, keepdims=True)
        e = jnp.exp(m2 - m1)
        w1 = 1.0 / (1.0 + e)
        w1_ref[rows, :] = jnp.broadcast_to(w1, logits.shape)
        w2_ref[rows, :] = jnp.broadcast_to(e * w1, logits.shape)
        picked.append((jnp.broadcast_to(i1, logits.shape), jnp.broadcast_to(i2, logits.shape)))

    _staged([part(slice(r, r + TM)) for r in range(0, TMB, TM)])
    i1 = jnp.concatenate([p[0] for p in picked], axis=0)
    i2 = jnp.concatenate([p[1] for p in picked], axis=0)
    oh1 = jnp.where(lane_f == i1, 1.0, 0.0)
    oh2 = jnp.where(lane_f == i2, 1.0, 0.0)
    r_i = lax.broadcasted_iota(jnp.int32, (TMB, TMB), 0)
    c_i = lax.broadcasted_iota(jnp.int32, (TMB, TMB), 1)
    before = jnp.where(c_i < r_i, 1.0, 0.0).astype(BF16)
    rank1 = jnp.sum((_dot(before, oh1.astype(BF16)) + run0_ref[0:1, :]) * oh1, axis=-1, keepdims=True)
    rank2 = jnp.sum((_dot(before, oh2.astype(BF16)) + run1_ref[0:1, :]) * oh2, axis=-1, keepdims=True)
    run0_ref[...] = run0_ref[...] + jnp.sum(oh1, axis=0, keepdims=True)
    run1_ref[...] = run1_ref[...] + jnp.sum(oh2, axis=0, keepdims=True)
    sub = lax.broadcasted_iota(jnp.int32, cnt_ref.shape, 0)
    cnt_ref[...] = jnp.where(sub == 0, run0_ref[...], run1_ref[...])
    quarter = LANES // 4
    cols = jnp.where(lane < quarter, i1, jnp.where(lane < 2 * quarter, i2, jnp.where(
        lane < 3 * quarter, rank1, rank2)))
    rows_t = jnp.transpose(cols)
    idx_ref[...] = jnp.concatenate(
        [rows_t[q * quarter:q * quarter + 1] for q in range(4)] + [jnp.zeros((4, TMB), F32)],
        axis=0).astype(jnp.int32)


def _postmix_odd(attn_ctx, attn_lat, x, mod, w_out, gamma, w_router):
    wr_hi = w_router.astype(BF16)
    wr_lo = (w_router - wr_hi.astype(F32)).astype(BF16)
    row = functools.partial(_row_spec, tm=TMB)
    return pl.pallas_call(
        _postmix_odd_kernel,
        grid=(M_TOK // TMB,),
        in_specs=[_ctx_spec(H_C * DH_C, TMB), _lat_spec(H_C * DH_C, TMB), row(D_MODEL), _mod_spec(2, TMB),
                  _const_spec((H_C * DH_C, D_MODEL)),
                  _const_spec((1, D_MODEL)), _mod_spec(3, TMB), _mod_spec(4, TMB), _const_spec((D_MODEL, LANES)),
                  _const_spec((D_MODEL, LANES))],
        out_specs=[row(D_MODEL), row(D_MODEL), row(LANES), row(LANES), pl.BlockSpec((8, TMB), lambda i: (0, i)),
                   _const_spec((8, LANES))],
        out_shape=[
            jax.ShapeDtypeStruct((M_TOK, D_MODEL), F32), jax.ShapeDtypeStruct((M_TOK, D_MODEL), BF16),
            jax.ShapeDtypeStruct((M_TOK, LANES), F32), jax.ShapeDtypeStruct((M_TOK, LANES), F32),
            jax.ShapeDtypeStruct((8, M_TOK), jnp.int32), jax.ShapeDtypeStruct((8, LANES), F32),
        ],
        scratch_shapes=[pltpu.VMEM((8, LANES), F32), pltpu.VMEM((8, LANES), F32)],
        compiler_params=_cparams(("arbitrary",)),
        name="postmix_odd",
    )(attn_ctx, attn_lat, x, mod, w_out, gamma, mod, mod, wr_hi, wr_lo)


def _cast_rows(src_ref, dst_ref, piece):
    def body(r, carry):
        rows = pl.ds(pl.multiple_of(r * piece, piece), piece)
        dst_ref[rows, :] = src_ref[rows, :].astype(BF16)
        return carry

    lax.fori_loop(0, src_ref.shape[0] // piece, body, 0)


def _experts_up_kernel(tile_ref, chunk_ref, exp_ref, first_ref, valid_ref, x_ref, wg_ref, wu_ref, act_ref,
                       wgb_ref, wub_ref):
    s = pl.program_id(0)

    @pl.when(first_ref[s] == 1)
    def _():
        _cast_rows(wg_ref, wgb_ref, 128)
        _cast_rows(wu_ref, wub_ref, 128)

    @pl.when(valid_ref[s] == 1)
    def _():
        _swiglu_act(x_ref[...].astype(BF16), wgb_ref, wub_ref, act_ref)


def _experts_down_kernel(act_tile_ref, exp_ref, first_ref, valid_ref, act_ref, wd_ref, o_ref, wdb_ref):
    i = pl.program_id(0)

    @pl.when(first_ref[i] == 1)
    def _():
        _cast_rows(wd_ref, wdb_ref, 256)

    @pl.when(valid_ref[i] == 1)
    def _():
        o_ref[...] = _dot(act_ref[...], wdb_ref[...])

    @pl.when(valid_ref[i] == 0)
    def _():
        o_ref[...] = jnp.zeros_like(o_ref)


def _experts(up_tables, down_tables, xs, w_gate, w_up, w_down):
    up = pl.pallas_call(
        _experts_up_kernel,
        grid_spec=pltpu.PrefetchScalarGridSpec(
            num_scalar_prefetch=5,
            grid=(NC_E * NT_E,),
            in_specs=[
                pl.BlockSpec((TME, D_MODEL), lambda s, t, c, e, f, v: (t[s], 0)),
                pl.BlockSpec((None, D_MODEL, TF_E), lambda s, t, c, e, f, v: (e[s], 0, c[s])),
                pl.BlockSpec((None, D_MODEL, TF_E), lambda s, t, c, e, f, v: (e[s], 0, c[s])),
            ],
            out_specs=pl.BlockSpec((TME, TF_E), lambda s, t, c, e, f, v: (t[s], c[s])),
            scratch_shapes=[pltpu.VMEM((D_MODEL, TF_E), BF16), pltpu.VMEM((D_MODEL, TF_E), BF16)],
        ),
        out_shape=jax.ShapeDtypeStruct((NT_E * TME, D_FF_E), BF16),
        compiler_params=_cparams(("arbitrary",)),
        name="experts_up",
    )(*up_tables, xs, w_gate, w_up)
    return pl.pallas_call(
        _experts_down_kernel,
        grid_spec=pltpu.PrefetchScalarGridSpec(
            num_scalar_prefetch=4,
            grid=(NT_E,),
            in_specs=[
                pl.BlockSpec((TME, D_FF_E), lambda i, a, e, f, v: (a[i], 0)),
                pl.BlockSpec((None, D_FF_E, D_MODEL), lambda i, a, e, f, v: (e[i], 0, 0)),
            ],
            out_specs=pl.BlockSpec((TME, D_MODEL), lambda i, a, e, f, v: (i, 0)),
            scratch_shapes=[pltpu.VMEM((D_FF_E, D_MODEL), BF16)],
        ),
        out_shape=jax.ShapeDtypeStruct((NT_E * TME, D_MODEL), F32),
        compiler_params=_cparams(("arbitrary",)),
        name="experts_down",
    )(*down_tables, up, w_down)


def _combine_kernel(x_ref, y1_ref, y2_ref, w1_ref, w2_ref, g2_ref, gam_ref, octx_ref, olat_ref):
    rep = D_MODEL // LANES
    w1 = jnp.concatenate([w1_ref[...]] * rep, axis=1)
    w2 = jnp.concatenate([w2_ref[...]] * rep, axis=1)
    x2 = x_ref[...] + g2_ref[...] * (w1 * y1_ref[...] + w2 * y2_ref[...])
    y = _rms(x2, gam_ref[...])
    is_lat = _is_lat(TMB)

    @pl.when(jnp.logical_not(is_lat))
    def _():
        octx_ref[...] = y

    @pl.when(is_lat)
    def _():
        olat_ref[...] = y


def _combine(x1, y1, y2, w1, w2, mod, gamma):
    return pl.pallas_call(
        _combine_kernel,
        grid=(M_TOK // TMB,),
        in_specs=[_row_spec(D_MODEL, tm=TMB), _row_spec(D_MODEL, tm=TMB), _row_spec(D_MODEL, tm=TMB),
                  _row_spec(LANES, tm=TMB), _row_spec(LANES, tm=TMB), _mod_spec(5, TMB), _const_spec((1, D_MODEL))],
        out_specs=[_ctx_spec(D_MODEL, TMB), _lat_spec(D_MODEL, TMB)],
        out_shape=[jax.ShapeDtypeStruct((N_CTX, D_MODEL), F32), jax.ShapeDtypeStruct((N_LAT, D_MODEL), F32)],
        compiler_params=_cparams(("arbitrary",)),
        name="combine",
    )(x1, y1, y2, w1, w2, mod, gamma)


def _rope_tables(rot_dim):
    n = DEC_SEQ
    rows = np.repeat(np.arange(n // GRID_W, dtype=np.float64), GRID_W)
    cols = np.tile(np.arange(GRID_W, dtype=np.float64), n // GRID_W)
    half = rot_dim // 2
    freqs = ROPE_THETA ** (-np.arange(0, half, 2, dtype=np.float64) / half)
    ang_r, ang_c = rows[:, None] * freqs, cols[:, None] * freqs
    cos = np.concatenate([np.cos(ang_r)] * 2 + [np.cos(ang_c)] * 2, axis=-1)
    sin = np.concatenate([-np.sin(ang_r), np.sin(ang_r), -np.sin(ang_c), np.sin(ang_c)], axis=-1)
    rep = LANES // rot_dim
    return (jnp.asarray(np.tile(cos, (1, rep)).astype(np.float32)),
            jnp.asarray(np.tile(sin, (1, rep)).astype(np.float32)))


def _with_ones(v):
    return jnp.concatenate([v, jnp.ones_like(v)], axis=-1)


def _rows(a, idx):
    return a.at[idx].get(mode="promise_in_bounds")


def _route(idx, counts_f):
    experts = jnp.arange(N_EXPERTS, dtype=jnp.int32)
    e2 = idx[0:2]
    first_total = counts_f[0, :N_EXPERTS].astype(jnp.int32)
    counts = first_total + counts_f[1, :N_EXPERTS].astype(jnp.int32)
    tiles = (counts + TME - 1) // TME
    tile_end = jnp.cumsum(tiles)
    tile_start = tile_end - tiles
    start = jnp.cumsum(counts) - counts

    def lookup(table, keys):
        return sum(jnp.where(keys == k, table[k], 0) for k in range(N_EXPERTS))

    rank2 = jnp.stack([idx[2], idx[3] + lookup(first_total, e2[1])], axis=0)
    pos = lookup(tile_start * TME, e2) + rank2
    tile_id = jnp.arange(NT_E, dtype=jnp.int32)
    tile_expert = jnp.minimum(jnp.sum((tile_id[:, None] >= tile_end[None, :]).astype(jnp.int32), axis=1),
                              N_EXPERTS - 1).astype(jnp.int32)
    tile_valid = (tile_id < tile_end[-1]).astype(jnp.int32)
    t_oh = tile_expert[:, None] == experts[None, :]
    t_first = jnp.sum(jnp.where(t_oh, (start - tile_start * TME)[None, :], 0), axis=1) + tile_id * TME
    t_last = jnp.sum(jnp.where(t_oh, (start + counts)[None, :], 0), axis=1)
    order = jnp.argsort(e2.reshape(-1), stable=True).astype(jnp.int32)
    n_assign = order.shape[0]
    g_idx = t_first[:, None] + jnp.arange(TME, dtype=jnp.int32)[None, :]
    live = jnp.logical_and(g_idx < t_last[:, None], tile_valid[:, None] == 1)
    src = _rows(order, jnp.clip(g_idx, 0, n_assign - 1).reshape(-1)) % M_TOK
    src_tok = jnp.where(live.reshape(-1), src, 0).astype(jnp.int32)

    n_tiles = tile_end[-1]
    last_tile = jnp.maximum(n_tiles - 1, 0)
    last_expert = jnp.sum(jnp.where(tile_id == last_tile, tile_expert, 0))
    t_tstart = jnp.sum(jnp.where(t_oh, tile_start[None, :], 0), axis=1)
    down_tables = (
        jnp.where(tile_valid == 1, tile_id, last_tile).astype(jnp.int32),
        tile_expert,
        jnp.logical_and(tile_valid == 1, tile_id == t_tstart).astype(jnp.int32),
        tile_valid,
    )
    s_id = jnp.arange(NC_E * NT_E, dtype=jnp.int32)
    s_exp = jnp.minimum(jnp.sum((s_id[:, None] >= NC_E * tile_end[None, :]).astype(jnp.int32), axis=1),
                        N_EXPERTS - 1)
    s_oh = s_exp[:, None] == experts[None, :]
    s_tstart = jnp.sum(jnp.where(s_oh, tile_start[None, :], 0), axis=1)
    s_ntile = jnp.maximum(jnp.sum(jnp.where(s_oh, tiles[None, :], 0), axis=1), 1)
    rel = s_id - NC_E * s_tstart
    s_valid = s_id < NC_E * n_tiles
    up_tables = (
        jnp.where(s_valid, s_tstart + rel % s_ntile, last_tile).astype(jnp.int32),
        jnp.where(s_valid, rel // s_ntile, NC_E - 1).astype(jnp.int32),
        jnp.where(s_valid, s_exp, last_expert).astype(jnp.int32),
        jnp.logical_and(s_valid, rel % s_ntile == 0).astype(jnp.int32),
        s_valid.astype(jnp.int32),
    )
    return src_tok, pos, up_tables, down_tables


def lambda_init(layer):
    return 0.8 - 0.6 * math.exp(-0.3 * layer)


def kernel(x_prompt, x_sample, state_a, cache_b_k, cache_b_v, cache_c_k, cache_c_v, c, c_ctx, w_mod, b_mod, norm_mix, norm_ffn, w_in_even, w_gate2_a, b_gate_a, g_gla, lam_q1, lam_k1, lam_q2, lam_k2, g_sub_b, w_out_even, w_in_odd, g_q_c, g_k_c, w_out_odd, ffn_gate, ffn_up, ffn_down, w_router, exp_gate, exp_up, exp_down, norm_final):
    x_ctx, x_lat = x_prompt.reshape(N_CTX, D_MODEL), x_sample.reshape(N_LAT, D_MODEL)
    cond =jnp.concatenate([c_ctx[None, :], c, jnp.zeros((N_COND - 1 - DEC_BATCH, D_MODEL), F32)], axis=0)
    mod = _modulation(cond, w_mod, b_mod).reshape(DEPTH, N_COND, 1, 6 * D_MODEL)

    w = w_in_even[0]
    na = 2 * A_QK + 2 * A_V
    gate_lo = na
    gate_hi = na + 2 * GATE_RANK
    col_scale = jnp.concatenate([jnp.full((A_QK,), DK_A ** -0.5, F32), jnp.ones((na - A_QK,), F32),
                                 jnp.full((B_QK,), DH_B ** -0.5, F32), jnp.ones((B_QK + B_V,), F32)])
    w_main = (jnp.concatenate([w[:, :gate_lo], w[:, gate_hi:]], axis=1) * col_scale).astype(BF16)
    w_gates = jnp.pad(w[:, gate_lo:gate_hi], ((0, 0), (0, LANES - 2 * GATE_RANK))).astype(BF16)
    g2 = jnp.zeros((LANES, 2 * A_QK), F32)
    g2 = g2.at[:GATE_RANK, :A_QK].set(w_gate2_a[0, 0]).at[GATE_RANK:2 * GATE_RANK, A_QK:].set(w_gate2_a[0, 1])
    bg = b_gate_a[0].reshape(1, 2 * A_QK)
    cos_b, sin_b = _rope_tables(DH_B)
    a_proj, glog, qb, kb, vb, kb_ctx, vb_ctx = _inproj_even(
        x_ctx, x_lat, norm_mix[0][None, :], mod[0], w_main, w_gates, g2.astype(BF16), bg, cos_b, sin_b)

    s0_all = jnp.concatenate([jnp.zeros((1, 2, A_QK, DV_A), F32),
                              state_a[:, 0].reshape(DEC_BATCH, 2, A_QK, DV_A)], axis=0)
    sel = np.arange(A_QK)[:, None] // DK_A == np.arange(LANES)[None, :] // GLA_SB
    m1 = jnp.asarray(sel.astype(np.float32)).astype(BF16)
    o_f, o_b, s_fin = _gla(a_proj, glog, s0_all, m1)

    k_cache = cache_b_k[:, 0].reshape(DEC_BATCH, PAST_LEN, B_QK).astype(BF16)
    v_cache = _with_ones(cache_b_v[:, 0].astype(BF16)).reshape(DEC_BATCH, PAST_LEN, 2 * B_V)
    lam_params = [p[0][None, :] for p in (lam_q1, lam_k1, lam_q2, lam_k2)]
    attn_b = _diffattn(qb, kb, vb, k_cache, v_cache, lam_params, g_sub_b[0][None, :], lambda_init(0))

    x = _mix_ffn_even(o_f, o_b, a_proj, *attn_b, x_ctx, x_lat, mod[0], g_gla[0][None, :],
                      w_out_even[0].astype(BF16), norm_ffn[0][None, :], ffn_gate[0].astype(BF16),
                      ffn_up[0].astype(BF16), ffn_down[0].astype(BF16))

    cos_c, sin_c = _rope_tables(DH_C)
    q_c, k_c, v_c, kc_ctx, vc_ctx = _inproj_odd(x, norm_mix[1][None, :], mod[1], w_in_odd[0].astype(BF16),
                                     g_q_c[0][None, :], g_k_c[0][None, :], cos_c, sin_c)
    nkv = HKV_C * DH_C
    k_cache = cache_c_k[:, 0].reshape(DEC_BATCH, PAST_LEN, nkv).astype(BF16)
    v_cache = _with_ones(cache_c_v[:, 0].astype(BF16)).reshape(DEC_BATCH, PAST_LEN, 2 * nkv)
    attn_c = _gqa(q_c, k_c, v_c, k_cache, v_cache)
    w_r = jnp.pad(w_router[0], ((0, 0), (0, LANES - N_EXPERTS)))
    x1, h_moe, w1, w2, idx, counts = _postmix_odd(*attn_c, x, mod[1], w_out_odd[0].astype(BF16),
                                                  norm_ffn[1][None, :], w_r)

    src_tok, pos, up_tables, down_tables = _route(idx, counts)
    xs = _rows(h_moe, src_tok)
    ys = _experts(up_tables, down_tables, xs, exp_gate[0], exp_up[0], exp_down[0])
    y1 = _rows(ys, pos[0])
    y2 = _rows(ys, pos[1])
    y_ctx, y_lat = _combine(x1, y1, y2, w1, w2, mod[1], norm_final[None, :])

    y_prompt = y_ctx.reshape(BATCH, SEQ, D_MODEL)
    y_sample = y_lat.reshape(DEC_BATCH, DEC_SEQ, D_MODEL)
    new_state_a = s_fin.reshape(BATCH, 1, 2, H_A, DK_A, DV_A)
    new_b_k = jnp.transpose(kb_ctx.reshape(BATCH, 1, H_B, 2, DH_B, SEQ), (0, 1, 5, 2, 3, 4))
    new_b_v = vb_ctx.reshape(BATCH, 1, SEQ, H_B, DV_B)
    new_c_k = kc_ctx.reshape(BATCH, 1, SEQ, HKV_C, DH_C)
    new_c_v = vc_ctx.reshape(BATCH, 1, SEQ, HKV_C, DH_C)
    return (y_prompt, y_sample, new_state_a, new_b_k, new_b_v, new_c_k, new_c_v)
```

```python
import functools
import math

import numpy as np
import jax
import jax.numpy as jnp
from jax import lax
from jax.experimental import pallas as pl
from jax.experimental.pallas import tpu as pltpu

D_MODEL = 1024
BATCH = 32
SEQ = 256
DEPTH = 2
DEC_BATCH = 2
DEC_SEQ = 2048
PAST_LEN = 512
GRID_W = 64
H_A, DK_A, DV_A = 4, 64, 128
GATE_RANK = 16
GATE_TAU = 16.0
H_B, DH_B = 4, 64
DV_B = 2 * DH_B
H_C, HKV_C, DH_C = 8, 2, 128
D_FF = 2816
N_EXPERTS = 8
D_FF_E = 3584
ROPE_THETA = 10000.0
EPS = 1e-6
A_QK = H_A * DK_A
A_V = H_A * DV_A
B_QK = H_B * 2 * DH_B
B_V = H_B * DV_B

F32 = jnp.float32
BF16 = jnp.bfloat16
LOG2E = math.log2(math.e)

V7X_VMEM_BYTES = 64 * 1024 * 1024
VMEM_LIMIT = V7X_VMEM_BYTES * 7 // 8
LANES = 128
MXU_N = 256
TM = 256
N_CTX = BATCH * SEQ
N_LAT = DEC_BATCH * DEC_SEQ
M_TOK = N_CTX + N_LAT
NB = M_TOK // TM
NCB = N_CTX // TM
LBB = DEC_SEQ // TM
N_COND = 8
GLA_C = 64
GLA_SB = 8
TME = 512
NT_E = 2 * M_TOK // TME + N_EXPERTS
NC_E = 2
TF_E = D_FF_E // NC_E
TQ_LAT = 256
TMB = 512

assert SEQ == TM and DEC_SEQ % TM == 0 and DEPTH == 2


def _cparams(sem):
    return pltpu.CompilerParams(dimension_semantics=sem, vmem_limit_bytes=VMEM_LIMIT)


def _group(i, tm=TM):
    return jnp.where(i < N_CTX // tm, 0, 1 + (i - N_CTX // tm) // (DEC_SEQ // tm))


def _pos_block(i, tm=TM):
    return jnp.where(i < N_CTX // tm, 0, (i - N_CTX // tm) % (DEC_SEQ // tm))


def _mod_spec(chunk, tm=TM):
    return pl.BlockSpec((None, 1, D_MODEL), lambda i, c=chunk: (_group(i, tm), 0, c))


def _row_spec(width, col=0, tm=TM):
    return pl.BlockSpec((tm, width), lambda i, c=col: (i, c))


def _ctx_spec(width, tm=TM):
    return pl.BlockSpec((tm, width), lambda i: (jnp.minimum(i, N_CTX // tm - 1), 0))


def _ctx_parts_spec(parts, tm=TM):
    return pl.BlockSpec((tm * parts, LANES), lambda i: (jnp.minimum(i, N_CTX // tm - 1), 0))


def _lat_spec(width, tm=TM):
    return pl.BlockSpec((tm, width), lambda i: (jnp.maximum(i - N_CTX // tm, 0), 0))


def _const_spec(shape):
    nd = len(shape)
    return pl.BlockSpec(shape, lambda i, nd=nd: (0,) * nd, pipeline_mode=pl.Buffered(1))


def _dot(a, b):
    return jnp.dot(a, b, preferred_element_type=F32)


def _dot_nt(a, b):
    return lax.dot_general(a, b, (((1,), (1,)), ((), ())), preferred_element_type=F32)


def _split2(a):
    hi = a.astype(BF16)
    lo = (a - hi.astype(F32)).astype(BF16)
    return hi, lo


def _dot_hi(a, b):
    a_hi, a_lo = _split2(a)
    b_hi, b_lo = _split2(b)
    return _dot(a_hi, b_hi) + (_dot(a_hi, b_lo) + _dot(a_lo, b_hi))


def _silu(x):
    return (0.5 * x) * (1.0 + jnp.tanh(0.5 * x))


def _log_sigmoid(x):
    return jnp.minimum(x, 0.0) - jnp.log(1.0 + jnp.exp(-jnp.abs(x)))


def _rms(x, g):
    return x * lax.rsqrt(jnp.mean(x * x, axis=-1, keepdims=True) + EPS) * g


def _norm_mod(x, gamma, sc, sh):
    return _rms(x, gamma) * (1.0 + sc) + sh


def _staged(parts):
    live = list(parts)
    while live:
        live = [g for g in live if next(g, True) is None]


def _copy_ctx_transposed(stage3_ref, dst_ref):
    @pl.when(pl.program_id(0) < NCB)
    def _():
        def body(j, carry):
            dst_ref[pl.ds(pl.multiple_of(j * LANES, LANES), LANES), :] = jnp.transpose(stage3_ref[j])
            return carry

        lax.fori_loop(0, stage3_ref.shape[0], body, 0)


def _is_lat(tm):
    return pl.program_id(0) >= N_CTX // tm


def _pick_rows(ctx_ref, lat_ref):
    return jnp.where(_is_lat(ctx_ref.shape[0]), lat_ref[...], ctx_ref[...])


def _copy_ctx(stage_ref, outs):
    piece = 64

    @pl.when(jnp.logical_not(_is_lat(stage_ref.shape[0])))
    def _():
        def body(r, carry):
            rows = pl.ds(pl.multiple_of(r * piece, piece), piece)
            for lo, width, parts, dst_ref in outs:
                for p in range(parts):
                    val = stage_ref[rows, lo + p * width:lo + (p + 1) * width]
                    if width < LANES:
                        val = jnp.concatenate([val, jnp.zeros((piece, LANES - width), F32)], axis=1)
                    dst_ref[pl.ds(pl.multiple_of(r * piece * parts, piece * parts) + p, piece, stride=parts), :] = val
            return carry

        lax.fori_loop(0, stage_ref.shape[0] // piece, body, 0)


def _rope(x, cos, sin, half):
    lane = lax.broadcasted_iota(jnp.int32, x.shape, 1)
    first = (lane % (2 * half)) < half
    swapped = jnp.where(first, pltpu.roll(x, LANES - half, 1), pltpu.roll(x, half, 1))
    return x * cos + swapped * sin


def _mod_kernel(cond_ref, w_ref, b_ref, o_ref):
    o_ref[...] = _dot_hi(_silu(cond_ref[...]), w_ref[...]) + b_ref[...]


def _modulation(cond, w_mod, b_mod):
    tn = 1536
    return pl.pallas_call(
        _mod_kernel,
        grid=(DEPTH, 6 * D_MODEL // tn),
        in_specs=[
            pl.BlockSpec((N_COND, D_MODEL), lambda l, j: (0, 0)),
            pl.BlockSpec((None, D_MODEL, tn), lambda l, j: (l, 0, j)),
            pl.BlockSpec((None, 1, tn), lambda l, j: (l, 0, j)),
        ],
        out_specs=pl.BlockSpec((None, N_COND, tn), lambda l, j: (l, 0, j)),
        out_shape=jax.ShapeDtypeStruct((DEPTH, N_COND, 6 * D_MODEL), F32),
        compiler_params=_cparams(("arbitrary", "arbitrary")),
        name="modulation",
    )(cond, w_mod, b_mod.reshape(DEPTH, 1, 6 * D_MODEL))


def _inproj_even_kernel(xc_ref, xl_ref, gam_ref, sh_ref, sc_ref, w_ref, wg_ref, g2_ref, bg_ref, cos_ref, sin_ref,
                        a_ref, glog_ref, qb_ref, kb_ref, vb_ref, kctx_ref, vctx_ref, stage_ref, stagek_ref):
    is_lat = pl.program_id(0) >= NCB
    na = 2 * A_QK + 2 * A_V

    def part(rows):
        n = rows.stop - rows.start
        x = jnp.where(is_lat, xl_ref[rows, :], xc_ref[rows, :])
        hb = _norm_mod(x, gam_ref[...], sc_ref[...], sh_ref[...]).astype(BF16)
        cos = jnp.where(is_lat, cos_ref[rows, :], 1.0)
        sin = jnp.where(is_lat, sin_ref[rows, :], 0.0)
        yield
        for lo in range(0, na, MXU_N):
            a_ref[rows, lo:lo + MXU_N] = _dot(hb, w_ref[:, lo:lo + MXU_N])
        for lo in range(0, B_QK, MXU_N):
            zq = _dot(hb, w_ref[:, na + lo:na + lo + MXU_N])
            zk = _dot(hb, w_ref[:, na + B_QK + lo:na + B_QK + lo + MXU_N])
            zv = _dot(hb, w_ref[:, na + 2 * B_QK + lo:na + 2 * B_QK + lo + MXU_N])
            stage_ref[rows, lo:lo + MXU_N] = zv
            for j in range(0, MXU_N, LANES):
                cs = slice(lo + j, lo + j + LANES)
                vb_ref[rows, 2 * (lo + j):2 * (lo + j) + DV_B] = zv[:, j:j + LANES].astype(BF16)
                vb_ref[rows, 2 * (lo + j) + DV_B:2 * (lo + j + LANES)] = jnp.ones((n, DV_B), BF16)
                stagek_ref[(lo + j) // LANES, rows, :] = zk[:, j:j + LANES]
                qb_ref[rows, cs] = (_rope(zq[:, j:j + LANES], cos, sin, DH_B // 4) * LOG2E).astype(BF16)
                kb_ref[rows, cs] = _rope(zk[:, j:j + LANES], cos, sin, DH_B // 4).astype(BF16)
        gates = _dot(hb, wg_ref[...])
        xg = _dot(gates.astype(BF16), g2_ref[...]) + bg_ref[...]
        glog_ref[rows, :] = _log_sigmoid(xg) * (1.0 / GATE_TAU)

    _staged([part(slice(0, TM))])
    _copy_ctx(stage_ref, ((0, DV_B, H_B, vctx_ref),))
    _copy_ctx_transposed(stagek_ref, kctx_ref)


def _inproj_even(x_ctx, x_lat, gamma, mod, w_main, w_gates, g2, bg, cos, sin):
    na = 2 * A_QK + 2 * A_V
    nz = na + 2 * B_QK + B_V
    return pl.pallas_call(
        _inproj_even_kernel,
        grid=(NB,),
        in_specs=[
            _ctx_spec(D_MODEL), _lat_spec(D_MODEL), _const_spec((1, D_MODEL)), _mod_spec(0), _mod_spec(1),
            _const_spec((D_MODEL, nz)), _const_spec((D_MODEL, LANES)), _const_spec((LANES, 2 * A_QK)),
            _const_spec((1, 2 * A_QK)),
            pl.BlockSpec((TM, LANES), lambda i: (_pos_block(i), 0)),
            pl.BlockSpec((TM, LANES), lambda i: (_pos_block(i), 0)),
        ],
        out_specs=[_pair_spec(na), _pair_spec(2 * A_QK), _row_spec(B_QK), _row_spec(B_QK), _row_spec(2 * B_V),
                   pl.BlockSpec((None, B_QK, TM), lambda i: (jnp.minimum(i, NCB - 1), 0, 0)), _ctx_parts_spec(H_B)],
        out_shape=[
            jax.ShapeDtypeStruct((M_TOK, na), F32), jax.ShapeDtypeStruct((M_TOK, 2 * A_QK), F32),
            jax.ShapeDtypeStruct((M_TOK, B_QK), BF16), jax.ShapeDtypeStruct((M_TOK, B_QK), BF16),
            jax.ShapeDtypeStruct((M_TOK, 2 * B_V), BF16), jax.ShapeDtypeStruct((BATCH, B_QK, SEQ), F32),
            jax.ShapeDtypeStruct((N_CTX * H_B, LANES), F32),
        ],
        scratch_shapes=[pltpu.VMEM((TM, B_V), F32), pltpu.VMEM((B_QK // LANES, TM, LANES), F32)],
        compiler_params=_cparams(("arbitrary",)),
        name="inproj_even",
    )(x_ctx, x_lat, gamma, mod, mod, w_main, w_gates, g2, bg, cos, sin)


def _gla_chunk(q_ref, k_ref, v_ref, g_ref, o_ref, c, s_ref, a_ref, b_ref, m1_ref, head_mask, same_block, rev):
    q, k, v = q_ref[c], k_ref[c], v_ref[c]
    g = g_ref[c] * LOG2E
    v_b = v.astype(BF16)
    r_i = lax.broadcasted_iota(jnp.int32, (GLA_C, GLA_C), 0)
    c_i = lax.broadcasted_iota(jnp.int32, (GLA_C, GLA_C), 1)
    tri = jnp.where((c_i >= r_i) if rev else (c_i <= r_i), 1.0, 0.0).astype(BF16)
    g_hi = g.astype(BF16)
    g_r = g - g_hi.astype(F32)
    g_mid = g_r.astype(BF16)
    g_lo = (g_r - g_mid.astype(F32)).astype(BF16)
    bc = _dot(tri, g_hi) + (_dot(tri, g_mid) + _dot(tri, g_lo))
    b_ref[...] = bc
    yield
    t_i = lax.broadcasted_iota(jnp.int32, (GLA_SB, A_QK), 0)
    group = 2 * GLA_SB
    for i0 in range(0, GLA_C, group):
        blocks = range(i0, i0 + group, GLA_SB)
        q_i = [q_ref[c, i:i + GLA_SB, :] for i in blocks]
        b_i = [b_ref[i:i + GLA_SB, :] for i in blocks]
        for j in range(GLA_SB):
            keep = (t_i <= j) if rev else (t_i >= j)
            pair = []
            for n, i in enumerate(blocks):
                k_s = k_ref[c, i + j:i + j + 1, :]
                b_s = b_ref[i + j:i + j + 1, :]
                pair.append(jnp.where(keep, q_i[n] * k_s * jnp.exp2(b_i[n] - b_s), 0.0))
            a_ref[j * GLA_C + i0:j * GLA_C + i0 + group, :] = jnp.concatenate(pair, axis=0).astype(BF16)
    yield
    last = bc[0:1] if rev else bc[GLA_C - 1:GLA_C]
    qc = q * jnp.exp2(bc)
    kl = k * jnp.exp2(last - bc)
    row = lax.broadcasted_iota(jnp.int32, (GLA_C, A_QK), 0)

    def expand(x):
        return jnp.where(head_mask, jnp.concatenate([x] * H_A, axis=0), 0.0).astype(BF16)

    scores = None
    half = GLA_C // 2
    while half >= GLA_SB:
        mids = [bc[b + half:b + half + 1] if rev else bc[b + half - 1:b + half] for b in range(0, GLA_C, 2 * half)]
        ref = mids[0] if len(mids) == 1 else jnp.concatenate(
            [jnp.broadcast_to(m, (2 * half, A_QK)) for m in mids], axis=0)
        upper = (row // half) % 2 == 1
        later, earlier = (jnp.logical_not(upper), upper) if rev else (upper, jnp.logical_not(upper))
        q_l = jnp.where(later, q * jnp.exp2(bc - ref), 0.0)
        k_e = jnp.where(earlier, k * jnp.exp2(ref - bc), 0.0).astype(BF16)
        term = _dot_nt(expand(q_l), k_e)
        if 2 * half < GLA_C:
            term = jnp.where(same_block[2 * half], term, 0.0)
        scores = term if scores is None else scores + term
        half //= 2
    s_old = s_ref[...]
    inter = _dot(expand(qc), s_old.astype(BF16))
    sums = _dot(a_ref[...], m1_ref[...])
    yield
    lane_c = lax.broadcasted_iota(jnp.int32, (GLA_C, LANES), 1)
    within = None
    for j in range(GLA_SB):
        part = jnp.where(lane_c % GLA_SB == j, sums[j * GLA_C:(j + 1) * GLA_C], 0.0)
        within = part if within is None else within + part
    lane = lax.broadcasted_iota(jnp.int32, (GLA_SB, LANES), 1)
    per_head = []
    for h in range(H_A):
        rows_h = []
        for i in range(0, GLA_C, GLA_SB):
            moved = pltpu.roll(within[i:i + GLA_SB], (i - h * GLA_SB) % LANES, 1)
            rows_h.append(jnp.where(lane // GLA_SB == i // GLA_SB, moved, 0.0))
        per_head.append(jnp.concatenate(rows_h, axis=0)[:, :GLA_C])
    scores = scores + jnp.concatenate(per_head, axis=0)
    o_heads = []
    for h in range(H_A):
        hs = slice(h * GLA_C, (h + 1) * GLA_C)
        o_heads.append(inter[hs] + _dot(scores[hs].astype(BF16), v_b[:, h * DV_A:(h + 1) * DV_A]))
    o_ref[c] = jnp.concatenate(o_heads, axis=1)
    yield
    t = jnp.transpose(jnp.concatenate([kl, jnp.broadcast_to(last, (GLA_C, A_QK))], axis=0))
    kv = _dot(t[:, :GLA_C].astype(BF16), v_b)
    a_col = jnp.exp2(t[:, GLA_C:GLA_C + 1])
    for h in range(H_A):
        hs = slice(h * DK_A, (h + 1) * DK_A)
        s_ref[hs, :] = a_col[hs] * s_old[hs] + kv[hs, h * DV_A:(h + 1) * DV_A]


def _gla_kernel(fblk_ref, bblk_ref, first_ref, inita_ref, initb_ref,
                qf_ref, kf_ref, vf_ref, gf_ref, qr_ref, kr_ref, vr_ref, gr_ref, s0a_ref, s0b_ref, m1_ref,
                of_ref, ob_ref, sfin_ref, s_refs, a_refs, b_refs):
    step = pl.program_id(0)

    @pl.when(first_ref[step] == 1)
    def _():
        for seq, s0_ref in enumerate((s0a_ref, s0b_ref)):
            for d in range(2):
                s_refs[2 * seq + d] = s0_ref[d]

    r_h = lax.broadcasted_iota(jnp.int32, (H_A * GLA_C, A_QK), 0) // GLA_C
    c_h = lax.broadcasted_iota(jnp.int32, (H_A * GLA_C, A_QK), 1) // DK_A
    head_mask = r_h == c_h
    t_s = lax.broadcasted_iota(jnp.int32, (H_A * GLA_C, GLA_C), 0) % GLA_C
    s_s = lax.broadcasted_iota(jnp.int32, (H_A * GLA_C, GLA_C), 1)
    same_block = {}
    size = 2 * GLA_SB
    while size < GLA_C:
        same_block[size] = (t_s // size) == (s_s // size)
        size *= 2
    nch = TM // GLA_C

    def body(c, carry):
        chains = []
        for seq in range(2):
            chains.append(_gla_chunk(qf_ref, kf_ref, vf_ref, gf_ref, of_ref, seq * nch + c, s_refs.at[2 * seq],
                                     a_refs.at[2 * seq], b_refs.at[2 * seq], m1_ref, head_mask, same_block, False))
            chains.append(_gla_chunk(qr_ref, kr_ref, vr_ref, gr_ref, ob_ref, seq * nch + nch - 1 - c,
                                     s_refs.at[2 * seq + 1], a_refs.at[2 * seq + 1], b_refs.at[2 * seq + 1], m1_ref,
                                     head_mask, same_block, True))
        _staged(chains)
        return carry

    lax.fori_loop(0, nch, body, 0)

    @pl.when(step < BATCH // 2)
    def _():
        for seq in range(2):
            for d in range(2):
                sfin_ref[seq, d] = s_refs[2 * seq + d]


def _pair_block(i):
    j = i - NCB
    return jnp.where(i < NCB, i, NCB + 2 * (j % LBB) + j // LBB)


def _pair_spec(width, col=0):
    return pl.BlockSpec((TM, width), lambda i, c=col: (_pair_block(i), c))


def _gla_tables():
    fblk, bblk, first, init_a, init_b = [], [], [], [], []
    for p in range(BATCH // 2):
        fblk.append(p), bblk.append(p), first.append(1), init_a.append(0), init_b.append(0)
    for j in range(LBB):
        fblk.append(NCB // 2 + j)
        bblk.append(NCB // 2 + LBB - 1 - j)
        first.append(1 if j == 0 else 0)
        init_a.append(1), init_b.append(2)
    return [jnp.asarray(np.array(t, np.int32)) for t in (fblk, bblk, first, init_a, init_b)]


def _gla(a_proj, glog, s0_all, m1):
    tables = _gla_tables()
    nsteps = int(tables[0].shape[0])
    nseq = BATCH + DEC_BATCH

    def fmap(col):
        return lambda i, fb, bb, fi, ia, ib: (fb[i], 0, col)

    def rmap(col):
        return lambda i, fb, bb, fi, ia, ib: (bb[i], 0, col)

    nch = 2 * TM // GLA_C
    a3 = a_proj.reshape(M_TOK // GLA_C, GLA_C, a_proj.shape[1])
    g3 = glog.reshape(M_TOK // GLA_C, GLA_C, glog.shape[1])
    gs = pltpu.PrefetchScalarGridSpec(
        num_scalar_prefetch=5,
        grid=(nsteps,),
        in_specs=[
            pl.BlockSpec((nch, GLA_C, A_QK), fmap(0)), pl.BlockSpec((nch, GLA_C, A_QK), fmap(1)),
            pl.BlockSpec((nch, GLA_C, A_V), fmap(1)), pl.BlockSpec((nch, GLA_C, A_QK), fmap(0)),
            pl.BlockSpec((nch, GLA_C, A_QK), rmap(0)), pl.BlockSpec((nch, GLA_C, A_QK), rmap(1)),
            pl.BlockSpec((nch, GLA_C, A_V), rmap(1)), pl.BlockSpec((nch, GLA_C, A_QK), rmap(1)),
            pl.BlockSpec((None, 2, A_QK, DV_A), lambda i, fb, bb, fi, ia, ib: (ia[i], 0, 0, 0)),
            pl.BlockSpec((None, 2, A_QK, DV_A), lambda i, fb, bb, fi, ia, ib: (ib[i], 0, 0, 0)),
            pl.BlockSpec((A_QK, LANES), lambda i, fb, bb, fi, ia, ib: (0, 0)),
        ],
        out_specs=[
            pl.BlockSpec((nch, GLA_C, A_V), fmap(0)), pl.BlockSpec((nch, GLA_C, A_V), rmap(0)),
            pl.BlockSpec((2, 2, A_QK, DV_A),
                         lambda i, fb, bb, fi, ia, ib: (jnp.minimum(fb[i], BATCH // 2 - 1), 0, 0, 0)),
        ],
        scratch_shapes=[
            pltpu.VMEM((4, A_QK, DV_A), F32),
            pltpu.VMEM((4, GLA_SB * GLA_C, A_QK), BF16),
            pltpu.VMEM((4, GLA_C, A_QK), F32),
        ],
    )
    o_f, o_b, s_fin = pl.pallas_call(
        _gla_kernel,
        grid_spec=gs,
        out_shape=[
            jax.ShapeDtypeStruct((M_TOK // GLA_C, GLA_C, A_V), F32),
            jax.ShapeDtypeStruct((M_TOK // GLA_C, GLA_C, A_V), F32),
            jax.ShapeDtypeStruct((BATCH, 2, A_QK, DV_A), F32),
        ],
        compiler_params=_cparams(("arbitrary",)),
        name="gla",
    )(*tables, a3, a3, a3, g3, a3, a3, a3, g3, s0_all, s0_all, m1)
    return o_f.reshape(M_TOK, A_V), o_b.reshape(M_TOK, A_V), s_fin


def _attend(q_b, kv, out):
    scores = [_dot_nt(q_b, k_b) for k_b, _ in kv]
    yield
    top = functools.reduce(jnp.maximum, [jnp.max(s2, axis=-1, keepdims=True) for s2 in scores])
    both = None
    for s2, (_, v_ones) in zip(scores, kv):
        part = _dot(jnp.exp2(s2 - top).astype(BF16), v_ones)
        both = part if both is None else both + part
    yield
    dv = both.shape[1] // 2
    out.append(both[:, :dv] * (1.0 / both[:, dv:]))


def _diffattn_body(q_ref, kv_refs, lq1_ref, lk1_ref, lq2_ref, lk2_ref, gsub_ref, o_ref, lam_init):
    lam = (jnp.exp(jnp.sum(lq1_ref[...] * lk1_ref[...], axis=-1, keepdims=True))
           - jnp.exp(jnp.sum(lq2_ref[...] * lk2_ref[...], axis=-1, keepdims=True)) + lam_init)
    lane = lax.broadcasted_iota(jnp.int32, (q_ref.shape[0], LANES), 1)
    group = H_B
    for h0 in range(0, H_B, group):
        outs, parts = [], []
        for h in range(h0, h0 + group):
            hs = slice(h * LANES, (h + 1) * LANES)
            q_h = q_ref[:, hs]
            kv = [(k_ref[:, hs], v_ref[:, 2 * h * DV_B:2 * (h + 1) * DV_B]) for k_ref, v_ref in kv_refs]
            zero = jnp.zeros_like(q_h)
            outs.append(([], []))
            parts.append(_attend(jnp.where(lane < DH_B, q_h, zero), kv, outs[-1][0]))
            parts.append(_attend(jnp.where(lane >= DH_B, q_h, zero), kv, outs[-1][1]))
        _staged(parts)
        for h, (o1, o2) in zip(range(h0, h0 + group), outs):
            o = o1[0] - lam * o2[0]
            o_ref[:, h * LANES:(h + 1) * LANES] = (_rms(o, gsub_ref[...]) * (1.0 - lam_init)).astype(BF16)


def _diffattn_kernel(q_ref, k_ref, v_ref, lq1, lk1, lq2, lk2, gsub, o_ref, *, lam_init):
    _diffattn_body(q_ref, [(k_ref, v_ref)], lq1, lk1, lq2, lk2, gsub, o_ref, lam_init)


def _diffattn_cached_kernel(q_ref, k_ref, v_ref, kc_ref, vc_ref, lq1, lk1, lq2, lk2, gsub, o_ref, *, lam_init):
    _diffattn_body(q_ref, [(k_ref, v_ref), (kc_ref, vc_ref)], lq1, lk1, lq2, lk2, gsub, o_ref, lam_init)


def _diffattn(qb, kb, vb, k_cache, v_cache, lam_params, gsub, lam_init):
    small = [pl.BlockSpec((1, DH_B), lambda *_: (0, 0))] * 4 + [pl.BlockSpec((1, DV_B), lambda *_: (0, 0))]
    body = functools.partial(_diffattn_kernel, lam_init=lam_init)
    ctx = pl.pallas_call(
        body,
        grid=(NCB,),
        in_specs=[_row_spec(B_QK), _row_spec(B_QK), _row_spec(2 * B_V)] + small,
        out_specs=_row_spec(B_V),
        out_shape=jax.ShapeDtypeStruct((N_CTX, B_V), BF16),
        compiler_params=_cparams(("arbitrary",)),
        name="diffattn_ctx",
    )(qb, kb, vb, *lam_params, gsub)
    nqb = DEC_SEQ // TQ_LAT
    off = N_CTX // TQ_LAT
    seq0 = N_CTX // DEC_SEQ
    lat = pl.pallas_call(
        functools.partial(_diffattn_cached_kernel, lam_init=lam_init),
        grid=(DEC_BATCH, nqb),
        in_specs=[
            pl.BlockSpec((TQ_LAT, B_QK), lambda b, j: (off + b * nqb + j, 0)),
            pl.BlockSpec((DEC_SEQ, B_QK), lambda b, j: (seq0 + b, 0)),
            pl.BlockSpec((DEC_SEQ, 2 * B_V), lambda b, j: (seq0 + b, 0)),
            pl.BlockSpec((None, PAST_LEN, B_QK), lambda b, j: (b, 0, 0)),
            pl.BlockSpec((None, PAST_LEN, 2 * B_V), lambda b, j: (b, 0, 0)),
        ] + small,
        out_specs=pl.BlockSpec((TQ_LAT, B_V), lambda b, j: (b * nqb + j, 0)),
        out_shape=jax.ShapeDtypeStruct((N_LAT, B_V), BF16),
        compiler_params=_cparams(("arbitrary", "arbitrary")),
        name="diffattn_lat",
    )(qb, kb, vb, k_cache, v_cache, *lam_params, gsub)
    return ctx, lat


def _swiglu_act(xb, wg_ref, wu_ref, act_ref):
    for lo in range(0, act_ref.shape[1], MXU_N):
        cs = slice(lo, lo + MXU_N)
        act_ref[:, cs] = (_silu(_dot(xb, wg_ref[:, cs])) * _dot(xb, wu_ref[:, cs])).astype(BF16)


def _mix_ffn_even_kernel(of_ref, ob_ref, ra_ref, actx_ref, alat_ref, xc_ref, xl_ref, g1_ref, ggla_ref, wo_ref,
                         gam_ref, sh_ref, sc_ref, g2_ref, wg_ref, wu_ref, wd_ref, o_ref, act_ref):
    is_lat = _is_lat(TM)

    def part(rows):
        heads = []
        for h in range(H_A):
            hs = slice(h * DV_A, (h + 1) * DV_A)
            heads.append((_rms(of_ref[rows, hs] + ob_ref[rows, hs], ggla_ref[...])
                          * _silu(ra_ref[rows, hs])).astype(BF16))
        mix = jnp.concatenate(heads + [jnp.where(is_lat, alat_ref[rows, :], actx_ref[rows, :])], axis=1)
        yield
        x = jnp.where(is_lat, xl_ref[rows, :], xc_ref[rows, :]) + g1_ref[...] * _dot(mix, wo_ref[...])
        hb = _norm_mod(x, gam_ref[...], sc_ref[...], sh_ref[...]).astype(BF16)
        yield
        for lo in range(0, D_FF, MXU_N):
            cs = slice(lo, lo + MXU_N)
            act_ref[rows, cs] = (_silu(_dot(hb, wg_ref[:, cs])) * _dot(hb, wu_ref[:, cs])).astype(BF16)
        yield
        o_ref[rows, :] = x + g2_ref[...] * _dot(act_ref[rows, :], wd_ref[...])

    _staged([part(slice(r, r + TM // 2)) for r in range(0, TM, TM // 2)])


def _mix_ffn_even(o_f, o_b, a_proj, attn_ctx, attn_lat, x_ctx, x_lat, mod, g_gla, w_out, gamma, w_gate, w_up, w_down):
    return pl.pallas_call(
        _mix_ffn_even_kernel,
        grid=(NB,),
        in_specs=[_pair_spec(A_V), _pair_spec(A_V), _pair_spec(A_V, 2), _ctx_spec(B_V), _lat_spec(B_V),
                  _ctx_spec(D_MODEL), _lat_spec(D_MODEL), _mod_spec(2), _const_spec((1, DV_A)),
                  _const_spec((A_V + B_V, D_MODEL)),
                  _const_spec((1, D_MODEL)), _mod_spec(3), _mod_spec(4), _mod_spec(5),
                  _const_spec((D_MODEL, D_FF)), _const_spec((D_MODEL, D_FF)), _const_spec((D_FF, D_MODEL))],
        out_specs=_row_spec(D_MODEL),
        out_shape=jax.ShapeDtypeStruct((M_TOK, D_MODEL), F32),
        scratch_shapes=[pltpu.VMEM((TM, D_FF), BF16)],
        compiler_params=_cparams(("arbitrary",)),
        name="mix_ffn_even",
    )(o_f, o_b, a_proj, attn_ctx, attn_lat, x_ctx, x_lat, mod, g_gla, w_out, gamma, mod, mod, mod,
      w_gate, w_up, w_down)


def _inproj_odd_kernel(x_ref, gam_ref, sh_ref, sc_ref, w_ref, gq_ref, gk_ref, cos_ref, sin_ref,
                       q_ref, k_ref, v_ref, kctx_ref, vctx_ref, stage_ref):
    is_lat = _is_lat(TMB)
    scale = DH_C ** -0.5 * LOG2E
    nq, nkv = H_C * DH_C, HKV_C * DH_C

    def part(rows):
        n = rows.stop - rows.start
        hb = _norm_mod(x_ref[rows, :], gam_ref[...], sc_ref[...], sh_ref[...]).astype(BF16)
        cos = jnp.where(is_lat, cos_ref[rows, :], 1.0)
        sin = jnp.where(is_lat, sin_ref[rows, :], 0.0)
        yield
        z = _dot(hb, w_ref[:, nq:nq + nkv])
        for h in range(HKV_C):
            hs = slice(h * DH_C, (h + 1) * DH_C)
            k_n = _rms(z[:, hs], gk_ref[...])
            stage_ref[rows, hs] = k_n
            k_ref[rows, hs] = _rope(k_n, cos, sin, DH_C // 4).astype(BF16)
        v = _dot(hb, w_ref[:, nq + nkv:])
        stage_ref[rows, nkv:] = v
        for h in range(HKV_C):
            v_ref[rows, 2 * h * DH_C:(2 * h + 1) * DH_C] = v[:, h * DH_C:(h + 1) * DH_C].astype(BF16)
            v_ref[rows, (2 * h + 1) * DH_C:2 * (h + 1) * DH_C] = jnp.ones((n, DH_C), BF16)
        for lo in range(0, nq, MXU_N):
            z = _dot(hb, w_ref[:, lo:lo + MXU_N])
            for j in range(0, MXU_N, DH_C):
                q_ref[rows, lo + j:lo + j + DH_C] = (
                    _rope(_rms(z[:, j:j + DH_C], gq_ref[...]), cos, sin, DH_C // 4) * scale).astype(BF16)

    _staged([part(slice(r, r + TM)) for r in range(0, TMB, TM)])
    _copy_ctx(stage_ref, ((0, DH_C, HKV_C, kctx_ref), (nkv, DH_C, HKV_C, vctx_ref)))


def _inproj_odd(x, gamma, mod, w_in, g_q, g_k, cos, sin):
    nq, nkv = H_C * DH_C, HKV_C * DH_C
    return pl.pallas_call(
        _inproj_odd_kernel,
        grid=(M_TOK // TMB,),
        in_specs=[
            _row_spec(D_MODEL, tm=TMB), _const_spec((1, D_MODEL)), _mod_spec(0, TMB), _mod_spec(1, TMB),
            _const_spec((D_MODEL, nq + 2 * nkv)), _const_spec((1, DH_C)), _const_spec((1, DH_C)),
            pl.BlockSpec((TMB, LANES), lambda i: (_pos_block(i, TMB), 0)),
            pl.BlockSpec((TMB, LANES), lambda i: (_pos_block(i, TMB), 0)),
        ],
        out_specs=[_row_spec(nq, tm=TMB), _row_spec(nkv, tm=TMB), _row_spec(2 * nkv, tm=TMB),
                   _ctx_parts_spec(HKV_C, TMB), _ctx_parts_spec(HKV_C, TMB)],
        out_shape=[
            jax.ShapeDtypeStruct((M_TOK, nq), BF16), jax.ShapeDtypeStruct((M_TOK, nkv), BF16),
            jax.ShapeDtypeStruct((M_TOK, 2 * nkv), BF16), jax.ShapeDtypeStruct((N_CTX * HKV_C, LANES), F32),
            jax.ShapeDtypeStruct((N_CTX * HKV_C, LANES), F32),
        ],
        scratch_shapes=[pltpu.VMEM((TMB, 2 * nkv), F32)],
        compiler_params=_cparams(("arbitrary",)),
        name="inproj_odd",
    )(x, gamma, mod, mod, w_in, g_q, g_k, cos, sin)


def _gqa_body(q_ref, kv_refs, o_ref):
    rep = H_C // HKV_C
    tq = q_ref.shape[0]
    outs, parts = [], []
    for hk in range(HKV_C):
        ks = slice(hk * DH_C, (hk + 1) * DH_C)
        q_g = jnp.concatenate([q_ref[:, (hk * rep + g) * DH_C:(hk * rep + g + 1) * DH_C] for g in range(rep)],
                              axis=0)
        kv = [(k_ref[:, ks], v_ref[:, 2 * hk * DH_C:2 * (hk + 1) * DH_C]) for k_ref, v_ref in kv_refs]
        outs.append([])
        parts.append(_attend(q_g, kv, outs[-1]))
    _staged(parts)
    for hk in range(HKV_C):
        o = outs[hk][0]
        for g in range(rep):
            o_ref[:, (hk * rep + g) * DH_C:(hk * rep + g + 1) * DH_C] = o[g * tq:(g + 1) * tq].astype(BF16)


def _gqa_kernel(q_ref, k_ref, v_ref, o_ref):
    _gqa_body(q_ref, [(k_ref, v_ref)], o_ref)


def _gqa_cached_kernel(q_ref, k_ref, v_ref, kc_ref, vc_ref, o_ref):
    _gqa_body(q_ref, [(k_ref, v_ref), (kc_ref, vc_ref)], o_ref)


def _gqa(q, k, v, k_cache, v_cache):
    nq, nkv = H_C * DH_C, HKV_C * DH_C
    ctx = pl.pallas_call(
        _gqa_kernel,
        grid=(NCB,),
        in_specs=[_row_spec(nq), _row_spec(nkv), _row_spec(2 * nkv)],
        out_specs=_row_spec(nq),
        out_shape=jax.ShapeDtypeStruct((N_CTX, nq), BF16),
        compiler_params=_cparams(("arbitrary",)),
        name="gqa_ctx",
    )(q, k, v)
    nqb = DEC_SEQ // TQ_LAT
    off = N_CTX // TQ_LAT
    seq0 = N_CTX // DEC_SEQ
    lat = pl.pallas_call(
        _gqa_cached_kernel,
        grid=(DEC_BATCH, nqb),
        in_specs=[
            pl.BlockSpec((TQ_LAT, nq), lambda b, j: (off + b * nqb + j, 0)),
            pl.BlockSpec((DEC_SEQ, nkv), lambda b, j: (seq0 + b, 0)),
            pl.BlockSpec((DEC_SEQ, 2 * nkv), lambda b, j: (seq0 + b, 0)),
            pl.BlockSpec((None, PAST_LEN, nkv), lambda b, j: (b, 0, 0)),
            pl.BlockSpec((None, PAST_LEN, 2 * nkv), lambda b, j: (b, 0, 0)),
        ],
        out_specs=pl.BlockSpec((TQ_LAT, nq), lambda b, j: (b * nqb + j, 0)),
        out_shape=jax.ShapeDtypeStruct((N_LAT, nq), BF16),
        compiler_params=_cparams(("arbitrary", "arbitrary")),
        name="gqa_lat",
    )(q, k, v, k_cache, v_cache)
    return ctx, lat


def _postmix_odd_kernel(actx_ref, alat_ref, x_ref, g1_ref, wo_ref, gam_ref, sh_ref, sc_ref, wrh_ref, wrl_ref,
                        x1_ref, h_ref, w1_ref, w2_ref, idx_ref, cnt_ref, run0_ref, run1_ref):
    @pl.when(pl.program_id(0) == 0)
    def _():
        run0_ref[...] = jnp.zeros_like(run0_ref)
        run1_ref[...] = jnp.zeros_like(run1_ref)

    is_lat = _is_lat(TMB)
    lane = lax.broadcasted_iota(jnp.int32, (TMB, LANES), 1)
    lane_f = lane.astype(F32)
    picked = []

    def part(rows):
        attn = jnp.where(is_lat, alat_ref[rows, :], actx_ref[rows, :])
        x1 = x_ref[rows, :] + g1_ref[...] * _dot(attn, wo_ref[...])
        x1_ref[rows, :] = x1
        yield
        h = _norm_mod(x1, gam_ref[...], sc_ref[...], sh_ref[...])
        h_hi, h_lo = _split2(h)
        h_ref[rows, :] = h_hi
        logits = _dot(h_hi, wrh_ref[...]) + (_dot(h_hi, wrl_ref[...]) + _dot(h_lo, wrh_ref[...]))
        yield
        lane_p = lax.broadcasted_iota(jnp.int32, logits.shape, 1).astype(F32)
        lg = jnp.where(lane_p < N_EXPERTS, logits, -jnp.inf)
        m1 = jnp.max(lg, axis=-1, keepdims=True)
        i1 = jnp.min(jnp.where(lg == m1, lane_p, float(LANES)), axis=-1, keepdims=True)
        lg2 = jnp.where(lane_p == i1, -jnp.inf, lg)
        m2 = jnp.max(lg2, axis=-1, keepdims=True)
        i2 = jnp.min(jnp.where(lg2 == m2, lane_p, float(LANES)), axis=-1, keepdims=True)
        e = jnp.exp(m2 - m1)
        w1 = 1.0 / (1.0 + e)
        w1_ref[rows, :] = jnp.broadcast_to(w1, logits.shape)
        w2_ref[rows, :] = jnp.broadcast_to(e * w1, logits.shape)
        picked.append((jnp.broadcast_to(i1, logits.shape), jnp.broadcast_to(i2, logits.shape)))

    _staged([part(slice(r, r + TM)) for r in range(0, TMB, TM)])
    i1 = jnp.concatenate([p[0] for p in picked], axis=0)
    i2 = jnp.concatenate([p[1] for p in picked], axis=0)
    oh1 = jnp.where(lane_f == i1, 1.0, 0.0)
    oh2 = jnp.where(lane_f == i2, 1.0, 0.0)
    r_i = lax.broadcasted_iota(jnp.int32, (TMB, TMB), 0)
    c_i = lax.broadcasted_iota(jnp.int32, (TMB, TMB), 1)
    before = jnp.where(c_i < r_i, 1.0, 0.0).astype(BF16)
    rank1 = jnp.sum((_dot(before, oh1.astype(BF16)) + run0_ref[0:1, :]) * oh1, axis=-1, keepdims=True)
    rank2 = jnp.sum((_dot(before, oh2.astype(BF16)) + run1_ref[0:1, :]) * oh2, axis=-1, keepdims=True)
    run0_ref[...] = run0_ref[...] + jnp.sum(oh1, axis=0, keepdims=True)
    run1_ref[...] = run1_ref[...] + jnp.sum(oh2, axis=0, keepdims=True)
    sub = lax.broadcasted_iota(jnp.int32, cnt_ref.shape, 0)
    cnt_ref[...] = jnp.where(sub == 0, run0_ref[...], run1_ref[...])
    quarter = LANES // 4
    cols = jnp.where(lane < quarter, i1, jnp.where(lane < 2 * quarter, i2, jnp.where(
        lane < 3 * quarter, rank1, rank2)))
    rows_t = jnp.transpose(cols)
    idx_ref[...] = jnp.concatenate(
        [rows_t[q * quarter:q * quarter + 1] for q in range(4)] + [jnp.zeros((4, TMB), F32)],
        axis=0).astype(jnp.int32)


def _postmix_odd(attn_ctx, attn_lat, x, mod, w_out, gamma, w_router):
    wr_hi = w_router.astype(BF16)
    wr_lo = (w_router - wr_hi.astype(F32)).astype(BF16)
    row = functools.partial(_row_spec, tm=TMB)
    return pl.pallas_call(
        _postmix_odd_kernel,
        grid=(M_TOK // TMB,),
        in_specs=[_ctx_spec(H_C * DH_C, TMB), _lat_spec(H_C * DH_C, TMB), row(D_MODEL), _mod_spec(2, TMB),
                  _const_spec((H_C * DH_C, D_MODEL)),
                  _const_spec((1, D_MODEL)), _mod_spec(3, TMB), _mod_spec(4, TMB), _const_spec((D_MODEL, LANES)),
                  _const_spec((D_MODEL, LANES))],
        out_specs=[row(D_MODEL), row(D_MODEL), row(LANES), row(LANES), pl.BlockSpec((8, TMB), lambda i: (0, i)),
                   _const_spec((8, LANES))],
        out_shape=[
            jax.ShapeDtypeStruct((M_TOK, D_MODEL), F32), jax.ShapeDtypeStruct((M_TOK, D_MODEL), BF16),
            jax.ShapeDtypeStruct((M_TOK, LANES), F32), jax.ShapeDtypeStruct((M_TOK, LANES), F32),
            jax.ShapeDtypeStruct((8, M_TOK), jnp.int32), jax.ShapeDtypeStruct((8, LANES), F32),
        ],
        scratch_shapes=[pltpu.VMEM((8, LANES), F32), pltpu.VMEM((8, LANES), F32)],
        compiler_params=_cparams(("arbitrary",)),
        name="postmix_odd",
    )(attn_ctx, attn_lat, x, mod, w_out, gamma, mod, mod, wr_hi, wr_lo)


def _cast_rows(src_ref, dst_ref, piece):
    def body(r, carry):
        rows = pl.ds(pl.multiple_of(r * piece, piece), piece)
        dst_ref[rows, :] = src_ref[rows, :].astype(BF16)
        return carry

    lax.fori_loop(0, src_ref.shape[0] // piece, body, 0)


def _experts_up_kernel(tile_ref, chunk_ref, exp_ref, first_ref, valid_ref, x_ref, wg_ref, wu_ref, act_ref,
                       wgb_ref, wub_ref):
    s = pl.program_id(0)

    @pl.when(first_ref[s] == 1)
    def _():
        _cast_rows(wg_ref, wgb_ref, 128)
        _cast_rows(wu_ref, wub_ref, 128)

    @pl.when(valid_ref[s] == 1)
    def _():
        _swiglu_act(x_ref[...].astype(BF16), wgb_ref, wub_ref, act_ref)


def _experts_down_kernel(act_tile_ref, exp_ref, first_ref, valid_ref, act_ref, wd_ref, o_ref, wdb_ref):
    i = pl.program_id(0)

    @pl.when(first_ref[i] == 1)
    def _():
        _cast_rows(wd_ref, wdb_ref, 256)

    @pl.when(valid_ref[i] == 1)
    def _():
        o_ref[...] = _dot(act_ref[...], wdb_ref[...])

    @pl.when(valid_ref[i] == 0)
    def _():
        o_ref[...] = jnp.zeros_like(o_ref)


def _experts(up_tables, down_tables, xs, w_gate, w_up, w_down):
    up = pl.pallas_call(
        _experts_up_kernel,
        grid_spec=pltpu.PrefetchScalarGridSpec(
            num_scalar_prefetch=5,
            grid=(NC_E * NT_E,),
            in_specs=[
                pl.BlockSpec((TME, D_MODEL), lambda s, t, c, e, f, v: (t[s], 0)),
                pl.BlockSpec((None, D_MODEL, TF_E), lambda s, t, c, e, f, v: (e[s], 0, c[s])),
                pl.BlockSpec((None, D_MODEL, TF_E), lambda s, t, c, e, f, v: (e[s], 0, c[s])),
            ],
            out_specs=pl.BlockSpec((TME, TF_E), lambda s, t, c, e, f, v: (t[s], c[s])),
            scratch_shapes=[pltpu.VMEM((D_MODEL, TF_E), BF16), pltpu.VMEM((D_MODEL, TF_E), BF16)],
        ),
        out_shape=jax.ShapeDtypeStruct((NT_E * TME, D_FF_E), BF16),
        compiler_params=_cparams(("arbitrary",)),
        name="experts_up",
    )(*up_tables, xs, w_gate, w_up)
    return pl.pallas_call(
        _experts_down_kernel,
        grid_spec=pltpu.PrefetchScalarGridSpec(
            num_scalar_prefetch=4,
            grid=(NT_E,),
            in_specs=[
                pl.BlockSpec((TME, D_FF_E), lambda i, a, e, f, v: (a[i], 0)),
                pl.BlockSpec((None, D_FF_E, D_MODEL), lambda i, a, e, f, v: (e[i], 0, 0)),
            ],
            out_specs=pl.BlockSpec((TME, D_MODEL), lambda i, a, e, f, v: (i, 0)),
            scratch_shapes=[pltpu.VMEM((D_FF_E, D_MODEL), BF16)],
        ),
        out_shape=jax.ShapeDtypeStruct((NT_E * TME, D_MODEL), F32),
        compiler_params=_cparams(("arbitrary",)),
        name="experts_down",
    )(*down_tables, up, w_down)


def _combine_kernel(x_ref, y1_ref, y2_ref, w1_ref, w2_ref, g2_ref, gam_ref, octx_ref, olat_ref):
    rep = D_MODEL // LANES
    w1 = jnp.concatenate([w1_ref[...]] * rep, axis=1)
    w2 = jnp.concatenate([w2_ref[...]] * rep, axis=1)
    x2 = x_ref[...] + g2_ref[...] * (w1 * y1_ref[...] + w2 * y2_ref[...])
    y = _rms(x2, gam_ref[...])
    is_lat = _is_lat(TMB)

    @pl.when(jnp.logical_not(is_lat))
    def _():
        octx_ref[...] = y

    @pl.when(is_lat)
    def _():
        olat_ref[...] = y


def _combine(x1, y1, y2, w1, w2, mod, gamma):
    return pl.pallas_call(
        _combine_kernel,
        grid=(M_TOK // TMB,),
        in_specs=[_row_spec(D_MODEL, tm=TMB), _row_spec(D_MODEL, tm=TMB), _row_spec(D_MODEL, tm=TMB),
                  _row_spec(LANES, tm=TMB), _row_spec(LANES, tm=TMB), _mod_spec(5, TMB), _const_spec((1, D_MODEL))],
        out_specs=[_ctx_spec(D_MODEL, TMB), _lat_spec(D_MODEL, TMB)],
        out_shape=[jax.ShapeDtypeStruct((N_CTX, D_MODEL), F32), jax.ShapeDtypeStruct((N_LAT, D_MODEL), F32)],
        compiler_params=_cparams(("arbitrary",)),
        name="combine",
    )(x1, y1, y2, w1, w2, mod, gamma)


def _rope_tables(rot_dim):
    n = DEC_SEQ
    rows = np.repeat(np.arange(n // GRID_W, dtype=np.float64), GRID_W)
    cols = np.tile(np.arange(GRID_W, dtype=np.float64), n // GRID_W)
    half = rot_dim // 2
    freqs = ROPE_THETA ** (-np.arange(0, half, 2, dtype=np.float64) / half)
    ang_r, ang_c = rows[:, None] * freqs, cols[:, None] * freqs
    cos = np.concatenate([np.cos(ang_r)] * 2 + [np.cos(ang_c)] * 2, axis=-1)
    sin = np.concatenate([-np.sin(ang_r), np.sin(ang_r), -np.sin(ang_c), np.sin(ang_c)], axis=-1)
    rep = LANES // rot_dim
    return (jnp.asarray(np.tile(cos, (1, rep)).astype(np.float32)),
            jnp.asarray(np.tile(sin, (1, rep)).astype(np.float32)))


def _with_ones(v):
    return jnp.concatenate([v, jnp.ones_like(v)], axis=-1)


def _rows(a, idx):
    return a.at[idx].get(mode="promise_in_bounds")


def _route(idx, counts_f):
    experts = jnp.arange(N_EXPERTS, dtype=jnp.int32)
    e2 = idx[0:2]
    first_total = counts_f[0, :N_EXPERTS].astype(jnp.int32)
    counts = first_total + counts_f[1, :N_EXPERTS].astype(jnp.int32)
    tiles = (counts + TME - 1) // TME
    tile_end = jnp.cumsum(tiles)
    tile_start = tile_end - tiles
    start = jnp.cumsum(counts) - counts

    def lookup(table, keys):
        return sum(jnp.where(keys == k, table[k], 0) for k in range(N_EXPERTS))

    rank2 = jnp.stack([idx[2], idx[3] + lookup(first_total, e2[1])], axis=0)
    pos = lookup(tile_start * TME, e2) + rank2
    tile_id = jnp.arange(NT_E, dtype=jnp.int32)
    tile_expert = jnp.minimum(jnp.sum((tile_id[:, None] >= tile_end[None, :]).astype(jnp.int32), axis=1),
                              N_EXPERTS - 1).astype(jnp.int32)
    tile_valid = (tile_id < tile_end[-1]).astype(jnp.int32)
    t_oh = tile_expert[:, None] == experts[None, :]
    t_first = jnp.sum(jnp.where(t_oh, (start - tile_start * TME)[None, :], 0), axis=1) + tile_id * TME
    t_last = jnp.sum(jnp.where(t_oh, (start + counts)[None, :], 0), axis=1)
    order = jnp.argsort(e2.reshape(-1), stable=True).astype(jnp.int32)
    n_assign = order.shape[0]
    g_idx = t_first[:, None] + jnp.arange(TME, dtype=jnp.int32)[None, :]
    live = jnp.logical_and(g_idx < t_last[:, None], tile_valid[:, None] == 1)
    src = _rows(order, jnp.clip(g_idx, 0, n_assign - 1).reshape(-1)) % M_TOK
    src_tok = jnp.where(live.reshape(-1), src, 0).astype(jnp.int32)

    n_tiles = tile_end[-1]
    last_tile = jnp.maximum(n_tiles - 1, 0)
    last_expert = jnp.sum(jnp.where(tile_id == last_tile, tile_expert, 0))
    t_tstart = jnp.sum(jnp.where(t_oh, tile_start[None, :], 0), axis=1)
    down_tables = (
        jnp.where(tile_valid == 1, tile_id, last_tile).astype(jnp.int32),
        tile_expert,
        jnp.logical_and(tile_valid == 1, tile_id == t_tstart).astype(jnp.int32),
        tile_valid,
    )
    s_id = jnp.arange(NC_E * NT_E, dtype=jnp.int32)
    s_exp = jnp.minimum(jnp.sum((s_id[:, None] >= NC_E * tile_end[None, :]).astype(jnp.int32), axis=1),
                        N_EXPERTS - 1)
    s_oh = s_exp[:, None] == experts[None, :]
    s_tstart = jnp.sum(jnp.where(s_oh, tile_start[None, :], 0), axis=1)
    s_ntile = jnp.maximum(jnp.sum(jnp.where(s_oh, tiles[None, :], 0), axis=1), 1)
    rel = s_id - NC_E * s_tstart
    s_valid = s_id < NC_E * n_tiles
    up_tables = (
        jnp.where(s_valid, s_tstart + rel % s_ntile, last_tile).astype(jnp.int32),
        jnp.where(s_valid, rel // s_ntile, NC_E - 1).astype(jnp.int32),
        jnp.where(s_valid, s_exp, last_expert).astype(jnp.int32),
        jnp.logical_and(s_valid, rel % s_ntile == 0).astype(jnp.int32),
        s_valid.astype(jnp.int32),
    )
    return src_tok, pos, up_tables, down_tables


def lambda_init(layer):
    return 0.8 - 0.6 * math.exp(-0.3 * layer)


def kernel(x_prompt, x_sample, state_a, cache_b_k, cache_b_v, cache_c_k, cache_c_v, c, c_ctx, w_mod, b_mod, norm_mix, norm_ffn, w_in_even, w_gate2_a, b_gate_a, g_gla, lam_q1, lam_k1, lam_q2, lam_k2, g_sub_b, w_out_even, w_in_odd, g_q_c, g_k_c, w_out_odd, ffn_gate, ffn_up, ffn_down, w_router, exp_gate, exp_up, exp_down, norm_final):
    x_ctx, x_lat = x_prompt.reshape(N_CTX, D_MODEL), x_sample.reshape(N_LAT, D_MODEL)
    cond =jnp.concatenate([c_ctx[None, :], c, jnp.zeros((N_COND - 1 - DEC_BATCH, D_MODEL), F32)], axis=0)
    mod = _modulation(cond, w_mod, b_mod).reshape(DEPTH, N_COND, 1, 6 * D_MODEL)

    w = w_in_even[0]
    na = 2 * A_QK + 2 * A_V
    gate_lo = na
    gate_hi = na + 2 * GATE_RANK
    col_scale = jnp.concatenate([jnp.full((A_QK,), DK_A ** -0.5, F32), jnp.ones((na - A_QK,), F32),
                                 jnp.full((B_QK,), DH_B ** -0.5, F32), jnp.ones((B_QK + B_V,), F32)])
    w_main = (jnp.concatenate([w[:, :gate_lo], w[:, gate_hi:]], axis=1) * col_scale).astype(BF16)
    w_gates = jnp.pad(w[:, gate_lo:gate_hi], ((0, 0), (0, LANES - 2 * GATE_RANK))).astype(BF16)
    g2 = jnp.zeros((LANES, 2 * A_QK), F32)
    g2 = g2.at[:GATE_RANK, :A_QK].set(w_gate2_a[0, 0]).at[GATE_RANK:2 * GATE_RANK, A_QK:].set(w_gate2_a[0, 1])
    bg = b_gate_a[0].reshape(1, 2 * A_QK)
    cos_b, sin_b = _rope_tables(DH_B)
    a_proj, glog, qb, kb, vb, kb_ctx, vb_ctx = _inproj_even(
        x_ctx, x_lat, norm_mix[0][None, :], mod[0], w_main, w_gates, g2.astype(BF16), bg, cos_b, sin_b)

    s0_all = jnp.concatenate([jnp.zeros((1, 2, A_QK, DV_A), F32),
                              state_a[:, 0].reshape(DEC_BATCH, 2, A_QK, DV_A)], axis=0)
    sel = np.arange(A_QK)[:, None] // DK_A == np.arange(LANES)[None, :] // GLA_SB
    m1 = jnp.asarray(sel.astype(np.float32)).astype(BF16)
    o_f, o_b, s_fin = _gla(a_proj, glog, s0_all, m1)

    k_cache = cache_b_k[:, 0].reshape(DEC_BATCH, PAST_LEN, B_QK).astype(BF16)
    v_cache = _with_ones(cache_b_v[:, 0].astype(BF16)).reshape(DEC_BATCH, PAST_LEN, 2 * B_V)
    lam_params = [p[0][None, :] for p in (lam_q1, lam_k1, lam_q2, lam_k2)]
    attn_b = _diffattn(qb, kb, vb, k_cache, v_cache, lam_params, g_sub_b[0][None, :], lambda_init(0))

    x = _mix_ffn_even(o_f, o_b, a_proj, *attn_b, x_ctx, x_lat, mod[0], g_gla[0][None, :],
                      w_out_even[0].astype(BF16), norm_ffn[0][None, :], ffn_gate[0].astype(BF16),
                      ffn_up[0].astype(BF16), ffn_down[0].astype(BF16))

    cos_c, sin_c = _rope_tables(DH_C)
    q_c, k_c, v_c, kc_ctx, vc_ctx = _inproj_odd(x, norm_mix[1][None, :], mod[1], w_in_odd[0].astype(BF16),
                                     g_q_c[0][None, :], g_k_c[0][None, :], cos_c, sin_c)
    nkv = HKV_C * DH_C
    k_cache = cache_c_k[:, 0].reshape(DEC_BATCH, PAST_LEN, nkv).astype(BF16)
    v_cache = _with_ones(cache_c_v[:, 0].astype(BF16)).reshape(DEC_BATCH, PAST_LEN, 2 * nkv)
    attn_c = _gqa(q_c, k_c, v_c, k_cache, v_cache)
    w_r = jnp.pad(w_router[0], ((0, 0), (0, LANES - N_EXPERTS)))
    x1, h_moe, w1, w2, idx, counts = _postmix_odd(*attn_c, x, mod[1], w_out_odd[0].astype(BF16),
                                                  norm_ffn[1][None, :], w_r)

    src_tok, pos, up_tables, down_tables = _route(idx, counts)
    xs = _rows(h_moe, src_tok)
    ys = _experts(up_tables, down_tables, xs, exp_gate[0], exp_up[0], exp_down[0])
    y1 = _rows(ys, pos[0])
    y2 = _rows(ys, pos[1])
    y_ctx, y_lat = _combine(x1, y1, y2, w1, w2, mod[1], norm_final[None, :])

    y_prompt = y_ctx.reshape(BATCH, SEQ, D_MODEL)
    y_sample = y_lat.reshape(DEC_BATCH, DEC_SEQ, D_MODEL)
    new_state_a = s_fin.reshape(BATCH, 1, 2, H_A, DK_A, DV_A)
    new_b_k = jnp.transpose(kb_ctx.reshape(BATCH, 1, H_B, 2, DH_B, SEQ), (0, 1, 5, 2, 3, 4))
    new_b_v = vb_ctx.reshape(BATCH, 1, SEQ, H_B, DV_B)
    new_c_k = kc_ctx.reshape(BATCH, 1, SEQ, HKV_C, DH_C)
    new_c_v = vc_ctx.reshape(BATCH, 1, SEQ, HKV_C, DH_C)
    return (y_prompt, y_sample, new_state_a, new_b_k, new_b_v, new_c_k, new_c_v)
```

```python
import functools
import math

import numpy as np
import jax
import jax.numpy as jnp
from jax import lax
from jax.experimental import pallas as pl
from jax.experimental.pallas import tpu as pltpu

D_MODEL = 1024
BATCH = 32
SEQ = 256
DEPTH = 2
DEC_BATCH = 2
DEC_SEQ = 2048
PAST_LEN = 512
GRID_W = 64
H_A, DK_A, DV_A = 4, 64, 128
GATE_RANK = 16
GATE_TAU = 16.0
H_B, DH_B = 4, 64
DV_B = 2 * DH_B
H_C, HKV_C, DH_C = 8, 2, 128
D_FF = 2816
N_EXPERTS = 8
D_FF_E = 3584
ROPE_THETA = 10000.0
EPS = 1e-6
A_QK = H_A * DK_A
A_V = H_A * DV_A
B_QK = H_B * 2 * DH_B
B_V = H_B * DV_B

F32 = jnp.float32
BF16 = jnp.bfloat16
LOG2E = math.log2(math.e)

V7X_VMEM_BYTES = 64 * 1024 * 1024
VMEM_LIMIT = V7X_VMEM_BYTES * 7 // 8
LANES = 128
MXU_N = 256
TM = 256
N_CTX = BATCH * SEQ
N_LAT = DEC_BATCH * DEC_SEQ
M_TOK = N_CTX + N_LAT
NB = M_TOK // TM
NCB = N_CTX // TM
LBB = DEC_SEQ // TM
N_COND = 8
GLA_C = 64
GLA_SB = 8
TME = 512
NT_E = 2 * M_TOK // TME + N_EXPERTS
NC_E = 2
TF_E = D_FF_E // NC_E
TQ_LAT = 256
TMC = 2 * SEQ
TMB = 512

assert SEQ == TM and DEC_SEQ % TM == 0 and DEPTH == 2


def _cparams(sem):
    return pltpu.CompilerParams(dimension_semantics=sem, vmem_limit_bytes=VMEM_LIMIT)


def _group(i, tm=TM):
    return jnp.where(i < N_CTX // tm, 0, 1 + (i - N_CTX // tm) // (DEC_SEQ // tm))


def _pos_block(i, tm=TM):
    return jnp.where(i < N_CTX // tm, 0, (i - N_CTX // tm) % (DEC_SEQ // tm))


def _mod_spec(chunk, tm=TM):
    return pl.BlockSpec((None, 1, D_MODEL), lambda i, c=chunk: (_group(i, tm), 0, c))


def _row_spec(width, col=0, tm=TM):
    return pl.BlockSpec((tm, width), lambda i, c=col: (i, c))


def _ctx_spec(width, tm=TM):
    return pl.BlockSpec((tm, width), lambda i: (jnp.minimum(i, N_CTX // tm - 1), 0))


def _ctx_parts_spec(parts, tm=TM):
    return pl.BlockSpec((tm * parts, LANES), lambda i: (jnp.minimum(i, N_CTX // tm - 1), 0))


def _lat_spec(width, tm=TM):
    return pl.BlockSpec((tm, width), lambda i: (jnp.maximum(i - N_CTX // tm, 0), 0))


def _const_spec(shape):
    nd = len(shape)
    return pl.BlockSpec(shape, lambda i, nd=nd: (0,) * nd, pipeline_mode=pl.Buffered(1))


def _dot(a, b):
    return jnp.dot(a, b, preferred_element_type=F32)


def _dot_nt(a, b):
    return lax.dot_general(a, b, (((1,), (1,)), ((), ())), preferred_element_type=F32)


def _split2(a):
    hi = a.astype(BF16)
    lo = (a - hi.astype(F32)).astype(BF16)
    return hi, lo


def _dot_hi(a, b):
    a_hi, a_lo = _split2(a)
    b_hi, b_lo = _split2(b)
    return _dot(a_hi, b_hi) + (_dot(a_hi, b_lo) + _dot(a_lo, b_hi))


def _silu(x):
    return (0.5 * x) * (1.0 + jnp.tanh(0.5 * x))


def _log_sigmoid(x):
    return jnp.minimum(x, 0.0) - jnp.log(1.0 + jnp.exp(-jnp.abs(x)))


def _rms(x, g):
    return x * lax.rsqrt(jnp.mean(x * x, axis=-1, keepdims=True) + EPS) * g


def _norm_mod(x, gamma, sc, sh):
    return _rms(x, gamma) * (1.0 + sc) + sh


def _staged(parts):
    live = list(parts)
    while live:
        live = [g for g in live if next(g, True) is None]


def _copy_ctx_transposed(stage3_ref, dst_ref):
    @pl.when(pl.program_id(0) < NCB)
    def _():
        def body(j, carry):
            dst_ref[pl.ds(pl.multiple_of(j * LANES, LANES), LANES), :] = jnp.transpose(stage3_ref[j])
            return carry

        lax.fori_loop(0, stage3_ref.shape[0], body, 0)


def _is_lat(tm):
    return pl.program_id(0) >= N_CTX // tm


def _pick_rows(ctx_ref, lat_ref):
    return jnp.where(_is_lat(ctx_ref.shape[0]), lat_ref[...], ctx_ref[...])


def _copy_ctx(stage_ref, outs):
    piece = 64

    @pl.when(jnp.logical_not(_is_lat(stage_ref.shape[0])))
    def _():
        def body(r, carry):
            rows = pl.ds(pl.multiple_of(r * piece, piece), piece)
            for lo, width, parts, dst_ref in outs:
                for p in range(parts):
                    val = stage_ref[rows, lo + p * width:lo + (p + 1) * width]
                    if width < LANES:
                        val = jnp.concatenate([val, jnp.zeros((piece, LANES - width), F32)], axis=1)
                    dst_ref[pl.ds(pl.multiple_of(r * piece * parts, piece * parts) + p, piece, stride=parts), :] = val
            return carry

        lax.fori_loop(0, stage_ref.shape[0] // piece, body, 0)


def _rope(x, cos, sin, half):
    lane = lax.broadcasted_iota(jnp.int32, x.shape, 1)
    first = (lane % (2 * half)) < half
    swapped = jnp.where(first, pltpu.roll(x, LANES - half, 1), pltpu.roll(x, half, 1))
    return x * cos + swapped * sin


def _mod_kernel(cond_ref, w_ref, b_ref, o_ref):
    o_ref[...] = _dot_hi(_silu(cond_ref[...]), w_ref[...]) + b_ref[...]


def _modulation(cond, w_mod, b_mod):
    tn = 1536
    return pl.pallas_call(
        _mod_kernel,
        grid=(DEPTH, 6 * D_MODEL // tn),
        in_specs=[
            pl.BlockSpec((N_COND, D_MODEL), lambda l, j: (0, 0)),
            pl.BlockSpec((None, D_MODEL, tn), lambda l, j: (l, 0, j)),
            pl.BlockSpec((None, 1, tn), lambda l, j: (l, 0, j)),
        ],
        out_specs=pl.BlockSpec((None, N_COND, tn), lambda l, j: (l, 0, j)),
        out_shape=jax.ShapeDtypeStruct((DEPTH, N_COND, 6 * D_MODEL), F32),
        compiler_params=_cparams(("arbitrary", "arbitrary")),
        name="modulation",
    )(cond, w_mod, b_mod.reshape(DEPTH, 1, 6 * D_MODEL))


def _inproj_even_kernel(xc_ref, xl_ref, gam_ref, sh_ref, sc_ref, w_ref, wg_ref, g2_ref, bg_ref, cos_ref, sin_ref,
                        a_ref, glog_ref, qb_ref, kb_ref, vb_ref, kctx_ref, vctx_ref, stage_ref, stagek_ref):
    is_lat = pl.program_id(0) >= NCB
    na = 2 * A_QK + 2 * A_V

    def part(rows):
        n = rows.stop - rows.start
        x = jnp.where(is_lat, xl_ref[rows, :], xc_ref[rows, :])
        hb = _norm_mod(x, gam_ref[...], sc_ref[...], sh_ref[...]).astype(BF16)
        cos = jnp.where(is_lat, cos_ref[rows, :], 1.0)
        sin = jnp.where(is_lat, sin_ref[rows, :], 0.0)
        yield
        for lo in range(0, na, MXU_N):
            a_ref[rows, lo:lo + MXU_N] = _dot(hb, w_ref[:, lo:lo + MXU_N])
        for lo in range(0, B_QK, MXU_N):
            zq = _dot(hb, w_ref[:, na + lo:na + lo + MXU_N])
            zk = _dot(hb, w_ref[:, na + B_QK + lo:na + B_QK + lo + MXU_N])
            zv = _dot(hb, w_ref[:, na + 2 * B_QK + lo:na + 2 * B_QK + lo + MXU_N])
            stage_ref[rows, lo:lo + MXU_N] = zv
            for j in range(0, MXU_N, LANES):
                cs = slice(lo + j, lo + j + LANES)
                vb_ref[rows, 2 * (lo + j):2 * (lo + j) + DV_B] = zv[:, j:j + LANES].astype(BF16)
                vb_ref[rows, 2 * (lo + j) + DV_B:2 * (lo + j + LANES)] = jnp.ones((n, DV_B), BF16)
                stagek_ref[(lo + j) // LANES, rows, :] = zk[:, j:j + LANES]
                qb_ref[rows, cs] = (_rope(zq[:, j:j + LANES], cos, sin, DH_B // 4) * LOG2E).astype(BF16)
                kb_ref[rows, cs] = _rope(zk[:, j:j + LANES], cos, sin, DH_B // 4).astype(BF16)
        gates = _dot(hb, wg_ref[...])
        xg = _dot(gates.astype(BF16), g2_ref[...]) + bg_ref[...]
        glog_ref[rows, :] = _log_sigmoid(xg) * (1.0 / GATE_TAU)

    _staged([part(slice(0, TM))])
    _copy_ctx(stage_ref, ((0, DV_B, H_B, vctx_ref),))
    _copy_ctx_transposed(stagek_ref, kctx_ref)


def _inproj_even(x_ctx, x_lat, gamma, mod, w_main, w_gates, g2, bg, cos, sin):
    na = 2 * A_QK + 2 * A_V
    nz = na + 2 * B_QK + B_V
    return pl.pallas_call(
        _inproj_even_kernel,
        grid=(NB,),
        in_specs=[
            _ctx_spec(D_MODEL), _lat_spec(D_MODEL), _const_spec((1, D_MODEL)), _mod_spec(0), _mod_spec(1),
            _const_spec((D_MODEL, nz)), _const_spec((D_MODEL, LANES)), _const_spec((LANES, 2 * A_QK)),
            _const_spec((1, 2 * A_QK)),
            pl.BlockSpec((TM, LANES), lambda i: (_pos_block(i), 0)),
            pl.BlockSpec((TM, LANES), lambda i: (_pos_block(i), 0)),
        ],
        out_specs=[_pair_spec(na), _pair_spec(2 * A_QK), _row_spec(B_QK), _row_spec(B_QK), _row_spec(2 * B_V),
                   pl.BlockSpec((None, B_QK, TM), lambda i: (jnp.minimum(i, NCB - 1), 0, 0)), _ctx_parts_spec(H_B)],
        out_shape=[
            jax.ShapeDtypeStruct((M_TOK, na), F32), jax.ShapeDtypeStruct((M_TOK, 2 * A_QK), F32),
            jax.ShapeDtypeStruct((M_TOK, B_QK), BF16), jax.ShapeDtypeStruct((M_TOK, B_QK), BF16),
            jax.ShapeDtypeStruct((M_TOK, 2 * B_V), BF16), jax.ShapeDtypeStruct((BATCH, B_QK, SEQ), F32),
            jax.ShapeDtypeStruct((N_CTX * H_B, LANES), F32),
        ],
        scratch_shapes=[pltpu.VMEM((TM, B_V), F32), pltpu.VMEM((B_QK // LANES, TM, LANES), F32)],
        compiler_params=_cparams(("arbitrary",)),
        name="inproj_even",
    )(x_ctx, x_lat, gamma, mod, mod, w_main, w_gates, g2, bg, cos, sin)


def _gla_chunk(q_ref, k_ref, v_ref, g_ref, o_ref, c, s_ref, a_ref, b_ref, m1_ref, head_mask, same_block, rev):
    q, k, v = q_ref[c], k_ref[c], v_ref[c]
    g = g_ref[c] * LOG2E
    v_b = v.astype(BF16)
    r_i = lax.broadcasted_iota(jnp.int32, (GLA_C, GLA_C), 0)
    c_i = lax.broadcasted_iota(jnp.int32, (GLA_C, GLA_C), 1)
    tri = jnp.where((c_i >= r_i) if rev else (c_i <= r_i), 1.0, 0.0).astype(BF16)
    g_hi = g.astype(BF16)
    g_r = g - g_hi.astype(F32)
    g_mid = g_r.astype(BF16)
    g_lo = (g_r - g_mid.astype(F32)).astype(BF16)
    bc = _dot(tri, g_hi) + (_dot(tri, g_mid) + _dot(tri, g_lo))
    b_ref[...] = bc
    yield
    t_i = lax.broadcasted_iota(jnp.int32, (GLA_SB, A_QK), 0)
    group = 2 * GLA_SB
    for i0 in range(0, GLA_C, group):
        blocks = range(i0, i0 + group, GLA_SB)
        q_i = [q_ref[c, i:i + GLA_SB, :] for i in blocks]
        b_i = [b_ref[i:i + GLA_SB, :] for i in blocks]
        for j in range(GLA_SB):
            keep = (t_i <= j) if rev else (t_i >= j)
            pair = []
            for n, i in enumerate(blocks):
                k_s = k_ref[c, i + j:i + j + 1, :]
                b_s = b_ref[i + j:i + j + 1, :]
                pair.append(jnp.where(keep, q_i[n] * k_s * jnp.exp2(b_i[n] - b_s), 0.0))
            a_ref[j * GLA_C + i0:j * GLA_C + i0 + group, :] = jnp.concatenate(pair, axis=0).astype(BF16)
    yield
    last = bc[0:1] if rev else bc[GLA_C - 1:GLA_C]
    qc = q * jnp.exp2(bc)
    kl = k * jnp.exp2(last - bc)
    row = lax.broadcasted_iota(jnp.int32, (GLA_C, A_QK), 0)

    def expand(x):
        return jnp.where(head_mask, jnp.concatenate([x] * H_A, axis=0), 0.0).astype(BF16)

    scores = None
    half = GLA_C // 2
    while half >= GLA_SB:
        mids = [bc[b + half:b + half + 1] if rev else bc[b + half - 1:b + half] for b in range(0, GLA_C, 2 * half)]
        ref = mids[0] if len(mids) == 1 else jnp.concatenate(
            [jnp.broadcast_to(m, (2 * half, A_QK)) for m in mids], axis=0)
        upper = (row // half) % 2 == 1
        later, earlier = (jnp.logical_not(upper), upper) if rev else (upper, jnp.logical_not(upper))
        q_l = jnp.where(later, q * jnp.exp2(bc - ref), 0.0)
        k_e = jnp.where(earlier, k * jnp.exp2(ref - bc), 0.0).astype(BF16)
        term = _dot_nt(expand(q_l), k_e)
        if 2 * half < GLA_C:
            term = jnp.where(same_block[2 * half], term, 0.0)
        scores = term if scores is None else scores + term
        half //= 2
    s_old = s_ref[...]
    inter = _dot(expand(qc), s_old.astype(BF16))
    sums = _dot(a_ref[...], m1_ref[...])
    yield
    lane_c = lax.broadcasted_iota(jnp.int32, (GLA_C, LANES), 1)
    within = None
    for j in range(GLA_SB):
        part = jnp.where(lane_c % GLA_SB == j, sums[j * GLA_C:(j + 1) * GLA_C], 0.0)
        within = part if within is None else within + part
    lane = lax.broadcasted_iota(jnp.int32, (GLA_SB, LANES), 1)
    per_head = []
    for h in range(H_A):
        rows_h = []
        for i in range(0, GLA_C, GLA_SB):
            moved = pltpu.roll(within[i:i + GLA_SB], (i - h * GLA_SB) % LANES, 1)
            rows_h.append(jnp.where(lane // GLA_SB == i // GLA_SB, moved, 0.0))
        per_head.append(jnp.concatenate(rows_h, axis=0)[:, :GLA_C])
    scores = scores + jnp.concatenate(per_head, axis=0)
    o_heads = []
    for h in range(H_A):
        hs = slice(h * GLA_C, (h + 1) * GLA_C)
        o_heads.append(inter[hs] + _dot(scores[hs].astype(BF16), v_b[:, h * DV_A:(h + 1) * DV_A]))
    o_ref[c] = jnp.concatenate(o_heads, axis=1)
    yield
    t = jnp.transpose(jnp.concatenate([kl, jnp.broadcast_to(last, (GLA_C, A_QK))], axis=0))
    kv = _dot(t[:, :GLA_C].astype(BF16), v_b)
    a_col = jnp.exp2(t[:, GLA_C:GLA_C + 1])
    for h in range(H_A):
        hs = slice(h * DK_A, (h + 1) * DK_A)
        s_ref[hs, :] = a_col[hs] * s_old[hs] + kv[hs, h * DV_A:(h + 1) * DV_A]


def _gla_kernel(fblk_ref, bblk_ref, first_ref, inita_ref, initb_ref,
                qf_ref, kf_ref, vf_ref, gf_ref, qr_ref, kr_ref, vr_ref, gr_ref, s0a_ref, s0b_ref, m1_ref,
                of_ref, ob_ref, sfin_ref, s_refs, a_refs, b_refs):
    step = pl.program_id(0)

    @pl.when(first_ref[step] == 1)
    def _():
        for seq, s0_ref in enumerate((s0a_ref, s0b_ref)):
            for d in range(2):
                s_refs[2 * seq + d] = s0_ref[d]

    r_h = lax.broadcasted_iota(jnp.int32, (H_A * GLA_C, A_QK), 0) // GLA_C
    c_h = lax.broadcasted_iota(jnp.int32, (H_A * GLA_C, A_QK), 1) // DK_A
    head_mask = r_h == c_h
    t_s = lax.broadcasted_iota(jnp.int32, (H_A * GLA_C, GLA_C), 0) % GLA_C
    s_s = lax.broadcasted_iota(jnp.int32, (H_A * GLA_C, GLA_C), 1)
    same_block = {}
    size = 2 * GLA_SB
    while size < GLA_C:
        same_block[size] = (t_s // size) == (s_s // size)
        size *= 2
    nch = TM // GLA_C

    def body(c, carry):
        chains = []
        for seq in range(2):
            chains.append(_gla_chunk(qf_ref, kf_ref, vf_ref, gf_ref, of_ref, seq * nch + c, s_refs.at[2 * seq],
                                     a_refs.at[2 * seq], b_refs.at[2 * seq], m1_ref, head_mask, same_block, False))
            chains.append(_gla_chunk(qr_ref, kr_ref, vr_ref, gr_ref, ob_ref, seq * nch + nch - 1 - c,
                                     s_refs.at[2 * seq + 1], a_refs.at[2 * seq + 1], b_refs.at[2 * seq + 1], m1_ref,
                                     head_mask, same_block, True))
        _staged(chains)
        return carry

    lax.fori_loop(0, nch, body, 0)

    @pl.when(step < BATCH // 2)
    def _():
        for seq in range(2):
            for d in range(2):
                sfin_ref[seq, d] = s_refs[2 * seq + d]


def _pair_block(i):
    j = i - NCB
    return jnp.where(i < NCB, i, NCB + 2 * (j % LBB) + j // LBB)


def _pair_spec(width, col=0):
    return pl.BlockSpec((TM, width), lambda i, c=col: (_pair_block(i), c))


def _gla_tables():
    fblk, bblk, first, init_a, init_b = [], [], [], [], []
    for p in range(BATCH // 2):
        fblk.append(p), bblk.append(p), first.append(1), init_a.append(0), init_b.append(0)
    for j in range(LBB):
        fblk.append(NCB // 2 + j)
        bblk.append(NCB // 2 + LBB - 1 - j)
        first.append(1 if j == 0 else 0)
        init_a.append(1), init_b.append(2)
    return [jnp.asarray(np.array(t, np.int32)) for t in (fblk, bblk, first, init_a, init_b)]


def _gla(a_proj, glog, s0_all, m1):
    tables = _gla_tables()
    nsteps = int(tables[0].shape[0])
    nseq = BATCH + DEC_BATCH

    def fmap(col):
        return lambda i, fb, bb, fi, ia, ib: (fb[i], 0, col)

    def rmap(col):
        return lambda i, fb, bb, fi, ia, ib: (bb[i], 0, col)

    nch = 2 * TM // GLA_C
    a3 = a_proj.reshape(M_TOK // GLA_C, GLA_C, a_proj.shape[1])
    g3 = glog.reshape(M_TOK // GLA_C, GLA_C, glog.shape[1])
    gs = pltpu.PrefetchScalarGridSpec(
        num_scalar_prefetch=5,
        grid=(nsteps,),
        in_specs=[
            pl.BlockSpec((nch, GLA_C, A_QK), fmap(0)), pl.BlockSpec((nch, GLA_C, A_QK), fmap(1)),
            pl.BlockSpec((nch, GLA_C, A_V), fmap(1)), pl.BlockSpec((nch, GLA_C, A_QK), fmap(0)),
            pl.BlockSpec((nch, GLA_C, A_QK), rmap(0)), pl.BlockSpec((nch, GLA_C, A_QK), rmap(1)),
            pl.BlockSpec((nch, GLA_C, A_V), rmap(1)), pl.BlockSpec((nch, GLA_C, A_QK), rmap(1)),
            pl.BlockSpec((None, 2, A_QK, DV_A), lambda i, fb, bb, fi, ia, ib: (ia[i], 0, 0, 0)),
            pl.BlockSpec((None, 2, A_QK, DV_A), lambda i, fb, bb, fi, ia, ib: (ib[i], 0, 0, 0)),
            pl.BlockSpec((A_QK, LANES), lambda i, fb, bb, fi, ia, ib: (0, 0)),
        ],
        out_specs=[
            pl.BlockSpec((nch, GLA_C, A_V), fmap(0)), pl.BlockSpec((nch, GLA_C, A_V), rmap(0)),
            pl.BlockSpec((2, 2, A_QK, DV_A),
                         lambda i, fb, bb, fi, ia, ib: (jnp.minimum(fb[i], BATCH // 2 - 1), 0, 0, 0)),
        ],
        scratch_shapes=[
            pltpu.VMEM((4, A_QK, DV_A), F32),
            pltpu.VMEM((4, GLA_SB * GLA_C, A_QK), BF16),
            pltpu.VMEM((4, GLA_C, A_QK), F32),
        ],
    )
    o_f, o_b, s_fin = pl.pallas_call(
        _gla_kernel,
        grid_spec=gs,
        out_shape=[
            jax.ShapeDtypeStruct((M_TOK // GLA_C, GLA_C, A_V), F32),
            jax.ShapeDtypeStruct((M_TOK // GLA_C, GLA_C, A_V), F32),
            jax.ShapeDtypeStruct((BATCH, 2, A_QK, DV_A), F32),
        ],
        compiler_params=_cparams(("arbitrary",)),
        name="gla",
    )(*tables, a3, a3, a3, g3, a3, a3, a3, g3, s0_all, s0_all, m1)
    return o_f.reshape(M_TOK, A_V), o_b.reshape(M_TOK, A_V), s_fin


def _attend(q_b, kv, out):
    scores = [_dot_nt(q_b, k_b) for k_b, _ in kv]
    yield
    top = functools.reduce(jnp.maximum, [jnp.max(s2, axis=-1, keepdims=True) for s2 in scores])
    both = None
    for s2, (_, v_ones) in zip(scores, kv):
        part = _dot(jnp.exp2(s2 - top).astype(BF16), v_ones)
        both = part if both is None else both + part
    yield
    dv = both.shape[1] // 2
    out.append(both[:, :dv] * (1.0 / both[:, dv:]))


def _seq_rows(ref):
    return [slice(r, r + SEQ) for r in range(0, ref.shape[0], SEQ)]


def _diffattn_body(q_ref, kv_refs, lq1_ref, lk1_ref, lq2_ref, lk2_ref, gsub_ref, o_ref, lam_init, seqs):
    lam = (jnp.exp(jnp.sum(lq1_ref[...] * lk1_ref[...], axis=-1, keepdims=True))
           - jnp.exp(jnp.sum(lq2_ref[...] * lk2_ref[...], axis=-1, keepdims=True)) + lam_init)
    outs, parts = [], []
    for rows in seqs:
        lane = lax.broadcasted_iota(jnp.int32, q_ref[rows, :LANES].shape, 1)
        for h in range(H_B):
            hs = slice(h * LANES, (h + 1) * LANES)
            q_h = q_ref[rows, hs]
            kv = [(k_ref[rows if n == 0 else slice(None), hs],
                   v_ref[rows if n == 0 else slice(None), 2 * h * DV_B:2 * (h + 1) * DV_B])
                  for n, (k_ref, v_ref) in enumerate(kv_refs)]
            zero = jnp.zeros_like(q_h)
            outs.append((rows, h, [], []))
            parts.append(_attend(jnp.where(lane < DH_B, q_h, zero), kv, outs[-1][2]))
            parts.append(_attend(jnp.where(lane >= DH_B, q_h, zero), kv, outs[-1][3]))
    _staged(parts)
    for rows, h, o1, o2 in outs:
        o = o1[0] - lam * o2[0]
        o_ref[rows, h * LANES:(h + 1) * LANES] = (_rms(o, gsub_ref[...]) * (1.0 - lam_init)).astype(BF16)


def _diffattn_kernel(q_ref, k_ref, v_ref, lq1, lk1, lq2, lk2, gsub, o_ref, *, lam_init):
    _diffattn_body(q_ref, [(k_ref, v_ref)], lq1, lk1, lq2, lk2, gsub, o_ref, lam_init, _seq_rows(q_ref))


def _diffattn_cached_kernel(q_ref, k_ref, v_ref, kc_ref, vc_ref, lq1, lk1, lq2, lk2, gsub, o_ref, *, lam_init):
    _diffattn_body(q_ref, [(k_ref, v_ref), (kc_ref, vc_ref)], lq1, lk1, lq2, lk2, gsub, o_ref, lam_init,
                   [slice(None)])


def _diffattn(qb, kb, vb, k_cache, v_cache, lam_params, gsub, lam_init):
    small = [pl.BlockSpec((1, DH_B), lambda *_: (0, 0))] * 4 + [pl.BlockSpec((1, DV_B), lambda *_: (0, 0))]
    body = functools.partial(_diffattn_kernel, lam_init=lam_init)
    ctx = pl.pallas_call(
        body,
        grid=(N_CTX // TMC,),
        in_specs=[_row_spec(B_QK, tm=TMC), _row_spec(B_QK, tm=TMC), _row_spec(2 * B_V, tm=TMC)] + small,
        out_specs=_row_spec(B_V, tm=TMC),
        out_shape=jax.ShapeDtypeStruct((N_CTX, B_V), BF16),
        compiler_params=_cparams(("arbitrary",)),
        name="diffattn_ctx",
    )(qb, kb, vb, *lam_params, gsub)
    nqb = DEC_SEQ // TQ_LAT
    off = N_CTX // TQ_LAT
    seq0 = N_CTX // DEC_SEQ
    lat = pl.pallas_call(
        functools.partial(_diffattn_cached_kernel, lam_init=lam_init),
        grid=(DEC_BATCH, nqb),
        in_specs=[
            pl.BlockSpec((TQ_LAT, B_QK), lambda b, j: (off + b * nqb + j, 0)),
            pl.BlockSpec((DEC_SEQ, B_QK), lambda b, j: (seq0 + b, 0)),
            pl.BlockSpec((DEC_SEQ, 2 * B_V), lambda b, j: (seq0 + b, 0)),
            pl.BlockSpec((None, PAST_LEN, B_QK), lambda b, j: (b, 0, 0)),
            pl.BlockSpec((None, PAST_LEN, 2 * B_V), lambda b, j: (b, 0, 0)),
        ] + small,
        out_specs=pl.BlockSpec((TQ_LAT, B_V), lambda b, j: (b * nqb + j, 0)),
        out_shape=jax.ShapeDtypeStruct((N_LAT, B_V), BF16),
        compiler_params=_cparams(("arbitrary", "arbitrary")),
        name="diffattn_lat",
    )(qb, kb, vb, k_cache, v_cache, *lam_params, gsub)
    return ctx, lat


def _swiglu_act(xb, wg_ref, wu_ref, act_ref):
    for lo in range(0, act_ref.shape[1], MXU_N):
        cs = slice(lo, lo + MXU_N)
        act_ref[:, cs] = (_silu(_dot(xb, wg_ref[:, cs])) * _dot(xb, wu_ref[:, cs])).astype(BF16)


def _mix_ffn_even_kernel(of_ref, ob_ref, ra_ref, actx_ref, alat_ref, xc_ref, xl_ref, g1_ref, ggla_ref, wo_ref,
                         gam_ref, sh_ref, sc_ref, g2_ref, wg_ref, wu_ref, wd_ref, o_ref, act_ref):
    is_lat = _is_lat(TM)

    def part(rows):
        heads = []
        for h in range(H_A):
            hs = slice(h * DV_A, (h + 1) * DV_A)
            heads.append((_rms(of_ref[rows, hs] + ob_ref[rows, hs], ggla_ref[...])
                          * _silu(ra_ref[rows, hs])).astype(BF16))
        mix = jnp.concatenate(heads + [jnp.where(is_lat, alat_ref[rows, :], actx_ref[rows, :])], axis=1)
        yield
        x = jnp.where(is_lat, xl_ref[rows, :], xc_ref[rows, :]) + g1_ref[...] * _dot(mix, wo_ref[...])
        hb = _norm_mod(x, gam_ref[...], sc_ref[...], sh_ref[...]).astype(BF16)
        yield
        for lo in range(0, D_FF, MXU_N):
            cs = slice(lo, lo + MXU_N)
            act_ref[rows, cs] = (_silu(_dot(hb, wg_ref[:, cs])) * _dot(hb, wu_ref[:, cs])).astype(BF16)
        yield
        o_ref[rows, :] = x + g2_ref[...] * _dot(act_ref[rows, :], wd_ref[...])

    _staged([part(slice(r, r + TM // 2)) for r in range(0, TM, TM // 2)])


def _mix_ffn_even(o_f, o_b, a_proj, attn_ctx, attn_lat, x_ctx, x_lat, mod, g_gla, w_out, gamma, w_gate, w_up, w_down):
    return pl.pallas_call(
        _mix_ffn_even_kernel,
        grid=(NB,),
        in_specs=[_pair_spec(A_V), _pair_spec(A_V), _pair_spec(A_V, 2), _ctx_spec(B_V), _lat_spec(B_V),
                  _ctx_spec(D_MODEL), _lat_spec(D_MODEL), _mod_spec(2), _const_spec((1, DV_A)),
                  _const_spec((A_V + B_V, D_MODEL)),
                  _const_spec((1, D_MODEL)), _mod_spec(3), _mod_spec(4), _mod_spec(5),
                  _const_spec((D_MODEL, D_FF)), _const_spec((D_MODEL, D_FF)), _const_spec((D_FF, D_MODEL))],
        out_specs=_row_spec(D_MODEL),
        out_shape=jax.ShapeDtypeStruct((M_TOK, D_MODEL), F32),
        scratch_shapes=[pltpu.VMEM((TM, D_FF), BF16)],
        compiler_params=_cparams(("arbitrary",)),
        name="mix_ffn_even",
    )(o_f, o_b, a_proj, attn_ctx, attn_lat, x_ctx, x_lat, mod, g_gla, w_out, gamma, mod, mod, mod,
      w_gate, w_up, w_down)


def _inproj_odd_kernel(x_ref, gam_ref, sh_ref, sc_ref, w_ref, gq_ref, gk_ref, cos_ref, sin_ref,
                       q_ref, k_ref, v_ref, kctx_ref, vctx_ref, stage_ref):
    is_lat = _is_lat(TMB)
    scale = DH_C ** -0.5 * LOG2E
    nq, nkv = H_C * DH_C, HKV_C * DH_C

    def part(rows):
        n = rows.stop - rows.start
        hb = _norm_mod(x_ref[rows, :], gam_ref[...], sc_ref[...], sh_ref[...]).astype(BF16)
        cos = jnp.where(is_lat, cos_ref[rows, :], 1.0)
        sin = jnp.where(is_lat, sin_ref[rows, :], 0.0)
        yield
        z = _dot(hb, w_ref[:, nq:nq + nkv])
        for h in range(HKV_C):
            hs = slice(h * DH_C, (h + 1) * DH_C)
            k_n = _rms(z[:, hs], gk_ref[...])
            stage_ref[rows, hs] = k_n
            k_ref[rows, hs] = _rope(k_n, cos, sin, DH_C // 4).astype(BF16)
        v = _dot(hb, w_ref[:, nq + nkv:])
        stage_ref[rows, nkv:] = v
        for h in range(HKV_C):
            v_ref[rows, 2 * h * DH_C:(2 * h + 1) * DH_C] = v[:, h * DH_C:(h + 1) * DH_C].astype(BF16)
            v_ref[rows, (2 * h + 1) * DH_C:2 * (h + 1) * DH_C] = jnp.ones((n, DH_C), BF16)
        for lo in range(0, nq, MXU_N):
            z = _dot(hb, w_ref[:, lo:lo + MXU_N])
            for j in range(0, MXU_N, DH_C):
                q_ref[rows, lo + j:lo + j + DH_C] = (
                    _rope(_rms(z[:, j:j + DH_C], gq_ref[...]), cos, sin, DH_C // 4) * scale).astype(BF16)

    _staged([part(slice(r, r + TM)) for r in range(0, TMB, TM)])
    _copy_ctx(stage_ref, ((0, DH_C, HKV_C, kctx_ref), (nkv, DH_C, HKV_C, vctx_ref)))


def _inproj_odd(x, gamma, mod, w_in, g_q, g_k, cos, sin):
    nq, nkv = H_C * DH_C, HKV_C * DH_C
    return pl.pallas_call(
        _inproj_odd_kernel,
        grid=(M_TOK // TMB,),
        in_specs=[
            _row_spec(D_MODEL, tm=TMB), _const_spec((1, D_MODEL)), _mod_spec(0, TMB), _mod_spec(1, TMB),
            _const_spec((D_MODEL, nq + 2 * nkv)), _const_spec((1, DH_C)), _const_spec((1, DH_C)),
            pl.BlockSpec((TMB, LANES), lambda i: (_pos_block(i, TMB), 0)),
            pl.BlockSpec((TMB, LANES), lambda i: (_pos_block(i, TMB), 0)),
        ],
        out_specs=[_row_spec(nq, tm=TMB), _row_spec(nkv, tm=TMB), _row_spec(2 * nkv, tm=TMB),
                   _ctx_parts_spec(HKV_C, TMB), _ctx_parts_spec(HKV_C, TMB)],
        out_shape=[
            jax.ShapeDtypeStruct((M_TOK, nq), BF16), jax.ShapeDtypeStruct((M_TOK, nkv), BF16),
            jax.ShapeDtypeStruct((M_TOK, 2 * nkv), BF16), jax.ShapeDtypeStruct((N_CTX * HKV_C, LANES), F32),
            jax.ShapeDtypeStruct((N_CTX * HKV_C, LANES), F32),
        ],
        scratch_shapes=[pltpu.VMEM((TMB, 2 * nkv), F32)],
        compiler_params=_cparams(("arbitrary",)),
        name="inproj_odd",
    )(x, gamma, mod, mod, w_in, g_q, g_k, cos, sin)


def _gqa_body(q_ref, kv_refs, o_ref, seqs):
    rep = H_C // HKV_C
    outs, parts = [], []
    for rows in seqs:
        for hk in range(HKV_C):
            ks = slice(hk * DH_C, (hk + 1) * DH_C)
            q_g = jnp.concatenate(
                [q_ref[rows, (hk * rep + g) * DH_C:(hk * rep + g + 1) * DH_C] for g in range(rep)], axis=0)
            kv = [(k_ref[rows if n == 0 else slice(None), ks],
                   v_ref[rows if n == 0 else slice(None), 2 * hk * DH_C:2 * (hk + 1) * DH_C])
                  for n, (k_ref, v_ref) in enumerate(kv_refs)]
            outs.append((rows, hk, []))
            parts.append(_attend(q_g, kv, outs[-1][2]))
    _staged(parts)
    for rows, hk, o in outs:
        tq = o[0].shape[0] // rep
        for g in range(rep):
            o_ref[rows, (hk * rep + g) * DH_C:(hk * rep + g + 1) * DH_C] = o[0][g * tq:(g + 1) * tq].astype(BF16)


def _gqa_kernel(q_ref, k_ref, v_ref, o_ref):
    _gqa_body(q_ref, [(k_ref, v_ref)], o_ref, _seq_rows(q_ref))


def _gqa_cached_kernel(q_ref, k_ref, v_ref, kc_ref, vc_ref, o_ref):
    _gqa_body(q_ref, [(k_ref, v_ref), (kc_ref, vc_ref)], o_ref, [slice(None)])


def _gqa(q, k, v, k_cache, v_cache):
    nq, nkv = H_C * DH_C, HKV_C * DH_C
    ctx = pl.pallas_call(
        _gqa_kernel,
        grid=(N_CTX // TMC,),
        in_specs=[_row_spec(nq, tm=TMC), _row_spec(nkv, tm=TMC), _row_spec(2 * nkv, tm=TMC)],
        out_specs=_row_spec(nq, tm=TMC),
        out_shape=jax.ShapeDtypeStruct((N_CTX, nq), BF16),
        compiler_params=_cparams(("arbitrary",)),
        name="gqa_ctx",
    )(q, k, v)
    nqb = DEC_SEQ // TQ_LAT
    off = N_CTX // TQ_LAT
    seq0 = N_CTX // DEC_SEQ
    lat = pl.pallas_call(
        _gqa_cached_kernel,
        grid=(DEC_BATCH, nqb),
        in_specs=[
            pl.BlockSpec((TQ_LAT, nq), lambda b, j: (off + b * nqb + j, 0)),
            pl.BlockSpec((DEC_SEQ, nkv), lambda b, j: (seq0 + b, 0)),
            pl.BlockSpec((DEC_SEQ, 2 * nkv), lambda b, j: (seq0 + b, 0)),
            pl.BlockSpec((None, PAST_LEN, nkv), lambda b, j: (b, 0, 0)),
            pl.BlockSpec((None, PAST_LEN, 2 * nkv), lambda b, j: (b, 0, 0)),
        ],
        out_specs=pl.BlockSpec((TQ_LAT, nq), lambda b, j: (b * nqb + j, 0)),
        out_shape=jax.ShapeDtypeStruct((N_LAT, nq), BF16),
        compiler_params=_cparams(("arbitrary", "arbitrary")),
        name="gqa_lat",
    )(q, k, v, k_cache, v_cache)
    return ctx, lat


def _postmix_odd_kernel(actx_ref, alat_ref, x_ref, g1_ref, wo_ref, gam_ref, sh_ref, sc_ref, wrh_ref, wrl_ref,
                        x1_ref, h_ref, w1_ref, w2_ref, idx_ref, cnt_ref, run0_ref, run1_ref):
    @pl.when(pl.program_id(0) == 0)
    def _():
        run0_ref[...] = jnp.zeros_like(run0_ref)
        run1_ref[...] = jnp.zeros_like(run1_ref)

    is_lat = _is_lat(TMB)
    lane = lax.broadcasted_iota(jnp.int32, (TMB, LANES), 1)
    lane_f = lane.astype(F32)
    picked = []

    def part(rows):
        attn = jnp.where(is_lat, alat_ref[rows, :], actx_ref[rows, :])
        x1 = x_ref[rows, :] + g1_ref[...] * _dot(attn, wo_ref[...])
        x1_ref[rows, :] = x1
        yield
        h = _norm_mod(x1, gam_ref[...], sc_ref[...], sh_ref[...])
        h_hi, h_lo = _split2(h)
        h_ref[rows, :] = h_hi
        logits = _dot(h_hi, wrh_ref[...]) + (_dot(h_hi, wrl_ref[...]) + _dot(h_lo, wrh_ref[...]))
        yield
        lane_p = lax.broadcasted_iota(jnp.int32, logits.shape, 1).astype(F32)
        lg = jnp.where(lane_p < N_EXPERTS, logits, -jnp.inf)
        m1 = jnp.max(lg, axis=-1, keepdims=True)
        i1 = jnp.min(jnp.where(lg == m1, lane_p, float(LANES)), axis=-1, keepdims=True)
        lg2 = jnp.where(lane_p == i1, -jnp.inf, lg)
        m2 = jnp.max(lg2, axis=-1, keepdims=True)
        i2 = jnp.min(jnp.where(lg2 == m2, lane_p, float(LANES)), axis=-1, keepdims=True)
        e = jnp.exp(m2 - m1)
        w1 = 1.0 / (1.0 + e)
        w1_ref[rows, :] = jnp.broadcast_to(w1, logits.shape)
        w2_ref[rows, :] = jnp.broadcast_to(e * w1, logits.shape)
        picked.append((jnp.broadcast_to(i1, logits.shape), jnp.broadcast_to(i2, logits.shape)))

    _staged([part(slice(r, r + TM)) for r in range(0, TMB, TM)])
    i1 = jnp.concatenate([p[0] for p in picked], axis=0)
    i2 = jnp.concatenate([p[1] for p in picked], axis=0)
    oh1 = jnp.where(lane_f == i1, 1.0, 0.0)
    oh2 = jnp.where(lane_f == i2, 1.0, 0.0)
    r_i = lax.broadcasted_iota(jnp.int32, (TMB, TMB), 0)
    c_i = lax.broadcasted_iota(jnp.int32, (TMB, TMB), 1)
    before = jnp.where(c_i < r_i, 1.0, 0.0).astype(BF16)
    rank1 = jnp.sum((_dot(before, oh1.astype(BF16)) + run0_ref[0:1, :]) * oh1, axis=-1, keepdims=True)
    rank2 = jnp.sum((_dot(before, oh2.astype(BF16)) + run1_ref[0:1, :]) * oh2, axis=-1, keepdims=True)
    run0_ref[...] = run0_ref[...] + jnp.sum(oh1, axis=0, keepdims=True)
    run1_ref[...] = run1_ref[...] + jnp.sum(oh2, axis=0, keepdims=True)
    sub = lax.broadcasted_iota(jnp.int32, cnt_ref.shape, 0)
    cnt_ref[...] = jnp.where(sub == 0, run0_ref[...], run1_ref[...])
    quarter = LANES // 4
    cols = jnp.where(lane < quarter, i1, jnp.where(lane < 2 * quarter, i2, jnp.where(
        lane < 3 * quarter, rank1, rank2)))
    rows_t = jnp.transpose(cols)
    idx_ref[...] = jnp.concatenate(
        [rows_t[q * quarter:q * quarter + 1] for q in range(4)] + [jnp.zeros((4, TMB), F32)],
        axis=0).astype(jnp.int32)


def _postmix_odd(attn_ctx, attn_lat, x, mod, w_out, gamma, w_router):
    wr_hi = w_router.astype(BF16)
    wr_lo = (w_router - wr_hi.astype(F32)).astype(BF16)
    row = functools.partial(_row_spec, tm=TMB)
    return pl.pallas_call(
        _postmix_odd_kernel,
        grid=(M_TOK // TMB,),
        in_specs=[_ctx_spec(H_C * DH_C, TMB), _lat_spec(H_C * DH_C, TMB), row(D_MODEL), _mod_spec(2, TMB),
                  _const_spec((H_C * DH_C, D_MODEL)),
                  _const_spec((1, D_MODEL)), _mod_spec(3, TMB), _mod_spec(4, TMB), _const_spec((D_MODEL, LANES)),
                  _const_spec((D_MODEL, LANES))],
        out_specs=[row(D_MODEL), row(D_MODEL), row(LANES), row(LANES), pl.BlockSpec((8, TMB), lambda i: (0, i)),
                   _const_spec((8, LANES))],
        out_shape=[
            jax.ShapeDtypeStruct((M_TOK, D_MODEL), F32), jax.ShapeDtypeStruct((M_TOK, D_MODEL), BF16),
            jax.ShapeDtypeStruct((M_TOK, LANES), F32), jax.ShapeDtypeStruct((M_TOK, LANES), F32),
            jax.ShapeDtypeStruct((8, M_TOK), jnp.int32), jax.ShapeDtypeStruct((8, LANES), F32),
        ],
        scratch_shapes=[pltpu.VMEM((8, LANES), F32), pltpu.VMEM((8, LANES), F32)],
        compiler_params=_cparams(("arbitrary",)),
        name="postmix_odd",
    )(attn_ctx, attn_lat, x, mod, w_out, gamma, mod, mod, wr_hi, wr_lo)


def _cast_rows(src_ref, dst_ref, piece):
    def body(r, carry):
        rows = pl.ds(pl.multiple_of(r * piece, piece), piece)
        dst_ref[rows, :] = src_ref[rows, :].astype(BF16)
        return carry

    lax.fori_loop(0, src_ref.shape[0] // piece, body, 0)


def _experts_up_kernel(tile_ref, chunk_ref, exp_ref, first_ref, valid_ref, x_ref, wg_ref, wu_ref, act_ref,
                       wgb_ref, wub_ref):
    s = pl.program_id(0)

    @pl.when(first_ref[s] == 1)
    def _():
        _cast_rows(wg_ref, wgb_ref, 128)
        _cast_rows(wu_ref, wub_ref, 128)

    @pl.when(valid_ref[s] == 1)
    def _():
        _swiglu_act(x_ref[...].astype(BF16), wgb_ref, wub_ref, act_ref)

    @pl.when(valid_ref[s] == 0)
    def _():
        piece = 64

        def body(r, carry):
            act_ref[pl.ds(pl.multiple_of(r * piece, piece), piece), :] = jnp.zeros((piece, TF_E), BF16)
            return carry

        lax.fori_loop(0, TME // piece, body, 0)


def _experts_down_kernel(act_tile_ref, exp_ref, first_ref, valid_ref, act_ref, wd_ref, o_ref, wdb_ref):
    i = pl.program_id(0)

    @pl.when(first_ref[i] == 1)
    def _():
        _cast_rows(wd_ref, wdb_ref, 256)

    @pl.when(valid_ref[i] == 1)
    def _():
        o_ref[...] = _dot(act_ref[...], wdb_ref[...])

    @pl.when(valid_ref[i] == 0)
    def _():
        o_ref[...] = jnp.zeros_like(o_ref)


def _experts(up_tables, down_tables, xs, w_gate, w_up, w_down):
    up = pl.pallas_call(
        _experts_up_kernel,
        grid_spec=pltpu.PrefetchScalarGridSpec(
            num_scalar_prefetch=5,
            grid=(NC_E * NT_E,),
            in_specs=[
                pl.BlockSpec((TME, D_MODEL), lambda s, t, c, e, f, v: (t[s], 0)),
                pl.BlockSpec((None, D_MODEL, TF_E), lambda s, t, c, e, f, v: (e[s], 0, c[s])),
                pl.BlockSpec((None, D_MODEL, TF_E), lambda s, t, c, e, f, v: (e[s], 0, c[s])),
            ],
            out_specs=pl.BlockSpec((TME, TF_E), lambda s, t, c, e, f, v: (t[s], c[s])),
            scratch_shapes=[pltpu.VMEM((D_MODEL, TF_E), BF16), pltpu.VMEM((D_MODEL, TF_E), BF16)],
        ),
        out_shape=jax.ShapeDtypeStruct((NT_E * TME, D_FF_E), BF16),
        compiler_params=_cparams(("arbitrary",)),
        name="experts_up",
    )(*up_tables, xs, w_gate, w_up)
    return pl.pallas_call(
        _experts_down_kernel,
        grid_spec=pltpu.PrefetchScalarGridSpec(
            num_scalar_prefetch=4,
            grid=(NT_E,),
            in_specs=[
                pl.BlockSpec((TME, D_FF_E), lambda i, a, e, f, v: (a[i], 0)),
                pl.BlockSpec((None, D_FF_E, D_MODEL), lambda i, a, e, f, v: (e[i], 0, 0)),
            ],
            out_specs=pl.BlockSpec((TME, D_MODEL), lambda i, a, e, f, v: (i, 0)),
            scratch_shapes=[pltpu.VMEM((D_FF_E, D_MODEL), BF16)],
        ),
        out_shape=jax.ShapeDtypeStruct((NT_E * TME, D_MODEL), F32),
        compiler_params=_cparams(("arbitrary",)),
        name="experts_down",
    )(*down_tables, up, w_down)


def _combine_kernel(x_ref, y1_ref, y2_ref, w1_ref, w2_ref, g2_ref, gam_ref, octx_ref, olat_ref):
    rep = D_MODEL // LANES
    w1 = jnp.concatenate([w1_ref[...]] * rep, axis=1)
    w2 = jnp.concatenate([w2_ref[...]] * rep, axis=1)
    x2 = x_ref[...] + g2_ref[...] * (w1 * y1_ref[...] + w2 * y2_ref[...])
    y = _rms(x2, gam_ref[...])
    is_lat = _is_lat(TMB)

    @pl.when(jnp.logical_not(is_lat))
    def _():
        octx_ref[...] = y

    @pl.when(is_lat)
    def _():
        olat_ref[...] = y


def _combine(x1, y1, y2, w1, w2, mod, gamma):
    return pl.pallas_call(
        _combine_kernel,
        grid=(M_TOK // TMB,),
        in_specs=[_row_spec(D_MODEL, tm=TMB), _row_spec(D_MODEL, tm=TMB), _row_spec(D_MODEL, tm=TMB),
                  _row_spec(LANES, tm=TMB), _row_spec(LANES, tm=TMB), _mod_spec(5, TMB), _const_spec((1, D_MODEL))],
        out_specs=[_ctx_spec(D_MODEL, TMB), _lat_spec(D_MODEL, TMB)],
        out_shape=[jax.ShapeDtypeStruct((N_CTX, D_MODEL), F32), jax.ShapeDtypeStruct((N_LAT, D_MODEL), F32)],
        compiler_params=_cparams(("arbitrary",)),
        name="combine",
    )(x1, y1, y2, w1, w2, mod, gamma)


def _rope_tables(rot_dim):
    n = DEC_SEQ
    rows = np.repeat(np.arange(n // GRID_W, dtype=np.float64), GRID_W)
    cols = np.tile(np.arange(GRID_W, dtype=np.float64), n // GRID_W)
    half = rot_dim // 2
    freqs = ROPE_THETA ** (-np.arange(0, half, 2, dtype=np.float64) / half)
    ang_r, ang_c = rows[:, None] * freqs, cols[:, None] * freqs
    cos = np.concatenate([np.cos(ang_r)] * 2 + [np.cos(ang_c)] * 2, axis=-1)
    sin = np.concatenate([-np.sin(ang_r), np.sin(ang_r), -np.sin(ang_c), np.sin(ang_c)], axis=-1)
    rep = LANES // rot_dim
    return (jnp.asarray(np.tile(cos, (1, rep)).astype(np.float32)),
            jnp.asarray(np.tile(sin, (1, rep)).astype(np.float32)))


def _with_ones(v):
    return jnp.concatenate([v, jnp.ones_like(v)], axis=-1)


def _rows(a, idx):
    return a.at[idx].get(mode="promise_in_bounds")


def _route(idx, counts_f):
    experts = jnp.arange(N_EXPERTS, dtype=jnp.int32)
    e2 = idx[0:2]
    first_total = counts_f[0, :N_EXPERTS].astype(jnp.int32)
    counts = first_total + counts_f[1, :N_EXPERTS].astype(jnp.int32)
    tiles = (counts + TME - 1) // TME
    tile_end = jnp.cumsum(tiles)
    tile_start = tile_end - tiles
    start = jnp.cumsum(counts) - counts

    def lookup(table, keys):
        return sum(jnp.where(keys == k, table[k], 0) for k in range(N_EXPERTS))

    rank2 = jnp.stack([idx[2], idx[3] + lookup(first_total, e2[1])], axis=0)
    pos = lookup(tile_start * TME, e2) + rank2
    tile_id = jnp.arange(NT_E, dtype=jnp.int32)
    tile_expert = jnp.minimum(jnp.sum((tile_id[:, None] >= tile_end[None, :]).astype(jnp.int32), axis=1),
                              N_EXPERTS - 1).astype(jnp.int32)
    tile_valid = (tile_id < tile_end[-1]).astype(jnp.int32)
    t_oh = tile_expert[:, None] == experts[None, :]
    t_first = jnp.sum(jnp.where(t_oh, (start - tile_start * TME)[None, :], 0), axis=1) + tile_id * TME
    t_last = jnp.sum(jnp.where(t_oh, (start + counts)[None, :], 0), axis=1)
    order = jnp.argsort(e2.reshape(-1), stable=True).astype(jnp.int32)
    n_assign = order.shape[0]
    g_idx = t_first[:, None] + jnp.arange(TME, dtype=jnp.int32)[None, :]
    live = jnp.logical_and(g_idx < t_last[:, None], tile_valid[:, None] == 1)
    src = _rows(order, jnp.clip(g_idx, 0, n_assign - 1).reshape(-1)) % M_TOK
    src_tok = jnp.where(live.reshape(-1), src, 0).astype(jnp.int32)

    n_tiles = tile_end[-1]
    last_tile = jnp.maximum(n_tiles - 1, 0)
    last_expert = jnp.sum(jnp.where(tile_id == last_tile, tile_expert, 0))
    t_tstart = jnp.sum(jnp.where(t_oh, tile_start[None, :], 0), axis=1)
    down_tables = (
        jnp.where(tile_valid == 1, tile_id, last_tile).astype(jnp.int32),
        tile_expert,
        jnp.logical_and(tile_valid == 1, tile_id == t_tstart).astype(jnp.int32),
        tile_valid,
    )
    s_id = jnp.arange(NC_E * NT_E, dtype=jnp.int32)
    s_exp = jnp.minimum(jnp.sum((s_id[:, None] >= NC_E * tile_end[None, :]).astype(jnp.int32), axis=1),
                        N_EXPERTS - 1)
    s_oh = s_exp[:, None] == experts[None, :]
    s_tstart = jnp.sum(jnp.where(s_oh, tile_start[None, :], 0), axis=1)
    s_ntile = jnp.maximum(jnp.sum(jnp.where(s_oh, tiles[None, :], 0), axis=1), 1)
    rel = s_id - NC_E * s_tstart
    s_valid = s_id < NC_E * n_tiles
    spare = s_id - NC_E * n_tiles
    up_tables = (
        jnp.where(s_valid, s_tstart + rel % s_ntile, n_tiles + spare // NC_E).astype(jnp.int32),
        jnp.where(s_valid, rel // s_ntile, spare % NC_E).astype(jnp.int32),
        jnp.where(s_valid, s_exp, last_expert).astype(jnp.int32),
        jnp.logical_and(s_valid, rel % s_ntile == 0).astype(jnp.int32),
        s_valid.astype(jnp.int32),
    )
    return src_tok, pos, up_tables, down_tables


def lambda_init(layer):
    return 0.8 - 0.6 * math.exp(-0.3 * layer)


def kernel(x_prompt, x_sample, state_a, cache_b_k, cache_b_v, cache_c_k, cache_c_v, c, c_ctx, w_mod, b_mod, norm_mix, norm_ffn, w_in_even, w_gate2_a, b_gate_a, g_gla, lam_q1, lam_k1, lam_q2, lam_k2, g_sub_b, w_out_even, w_in_odd, g_q_c, g_k_c, w_out_odd, ffn_gate, ffn_up, ffn_down, w_router, exp_gate, exp_up, exp_down, norm_final):
    x_ctx, x_lat = x_prompt.reshape(N_CTX, D_MODEL), x_sample.reshape(N_LAT, D_MODEL)
    cond =jnp.concatenate([c_ctx[None, :], c, jnp.zeros((N_COND - 1 - DEC_BATCH, D_MODEL), F32)], axis=0)
    mod = _modulation(cond, w_mod, b_mod).reshape(DEPTH, N_COND, 1, 6 * D_MODEL)

    w = w_in_even[0]
    na = 2 * A_QK + 2 * A_V
    gate_lo = na
    gate_hi = na + 2 * GATE_RANK
    col_scale = jnp.concatenate([jnp.full((A_QK,), DK_A ** -0.5, F32), jnp.ones((na - A_QK,), F32),
                                 jnp.full((B_QK,), DH_B ** -0.5, F32), jnp.ones((B_QK + B_V,), F32)])
    w_main = (jnp.concatenate([w[:, :gate_lo], w[:, gate_hi:]], axis=1) * col_scale).astype(BF16)
    w_gates = jnp.pad(w[:, gate_lo:gate_hi], ((0, 0), (0, LANES - 2 * GATE_RANK))).astype(BF16)
    g2 = jnp.zeros((LANES, 2 * A_QK), F32)
    g2 = g2.at[:GATE_RANK, :A_QK].set(w_gate2_a[0, 0]).at[GATE_RANK:2 * GATE_RANK, A_QK:].set(w_gate2_a[0, 1])
    bg = b_gate_a[0].reshape(1, 2 * A_QK)
    cos_b, sin_b = _rope_tables(DH_B)
    a_proj, glog, qb, kb, vb, kb_ctx, vb_ctx = _inproj_even(
        x_ctx, x_lat, norm_mix[0][None, :], mod[0], w_main, w_gates, g2.astype(BF16), bg, cos_b, sin_b)

    s0_all = jnp.concatenate([jnp.zeros((1, 2, A_QK, DV_A), F32),
                              state_a[:, 0].reshape(DEC_BATCH, 2, A_QK, DV_A)], axis=0)
    sel = np.arange(A_QK)[:, None] // DK_A == np.arange(LANES)[None, :] // GLA_SB
    m1 = jnp.asarray(sel.astype(np.float32)).astype(BF16)
    o_f, o_b, s_fin = _gla(a_proj, glog, s0_all, m1)

    k_cache = cache_b_k[:, 0].reshape(DEC_BATCH, PAST_LEN, B_QK).astype(BF16)
    v_cache = _with_ones(cache_b_v[:, 0].astype(BF16)).reshape(DEC_BATCH, PAST_LEN, 2 * B_V)
    lam_params = [p[0][None, :] for p in (lam_q1, lam_k1, lam_q2, lam_k2)]
    attn_b = _diffattn(qb, kb, vb, k_cache, v_cache, lam_params, g_sub_b[0][None, :], lambda_init(0))

    x = _mix_ffn_even(o_f, o_b, a_proj, *attn_b, x_ctx, x_lat, mod[0], g_gla[0][None, :],
                      w_out_even[0].astype(BF16), norm_ffn[0][None, :], ffn_gate[0].astype(BF16),
                      ffn_up[0].astype(BF16), ffn_down[0].astype(BF16))

    cos_c, sin_c = _rope_tables(DH_C)
    q_c, k_c, v_c, kc_ctx, vc_ctx = _inproj_odd(x, norm_mix[1][None, :], mod[1], w_in_odd[0].astype(BF16),
                                     g_q_c[0][None, :], g_k_c[0][None, :], cos_c, sin_c)
    nkv = HKV_C * DH_C
    k_cache = cache_c_k[:, 0].reshape(DEC_BATCH, PAST_LEN, nkv).astype(BF16)
    v_cache = _with_ones(cache_c_v[:, 0].astype(BF16)).reshape(DEC_BATCH, PAST_LEN, 2 * nkv)
    attn_c = _gqa(q_c, k_c, v_c, k_cache, v_cache)
    w_r = jnp.pad(w_router[0], ((0, 0), (0, LANES - N_EXPERTS)))
    x1, h_moe, w1, w2, idx, counts = _postmix_odd(*attn_c, x, mod[1], w_out_odd[0].astype(BF16),
                                                  norm_ffn[1][None, :], w_r)

    src_tok, pos, up_tables, down_tables = _route(idx, counts)
    xs = _rows(h_moe, src_tok)
    ys = _experts(up_tables, down_tables, xs, exp_gate[0], exp_up[0], exp_down[0])
    y1 = _rows(ys, pos[0])
    y2 = _rows(ys, pos[1])
    y_ctx, y_lat = _combine(x1, y1, y2, w1, w2, mod[1], norm_final[None, :])

    y_prompt = y_ctx.reshape(BATCH, SEQ, D_MODEL)
    y_sample = y_lat.reshape(DEC_BATCH, DEC_SEQ, D_MODEL)
    new_state_a = s_fin.reshape(BATCH, 1, 2, H_A, DK_A, DV_A)
    new_b_k = jnp.transpose(kb_ctx.reshape(BATCH, 1, H_B, 2, DH_B, SEQ), (0, 1, 5, 2, 3, 4))
    new_b_v = vb_ctx.reshape(BATCH, 1, SEQ, H_B, DV_B)
    new_c_k = kc_ctx.reshape(BATCH, 1, SEQ, HKV_C, DH_C)
    new_c_v = vc_ctx.reshape(BATCH, 1, SEQ, HKV_C, DH_C)
    return (y_prompt, y_sample, new_state_a, new_b_k, new_b_v, new_c_k, new_c_v)
```

```python
import functools
import math

import numpy as np
import jax
import jax.numpy as jnp
from jax import lax
from jax.experimental import pallas as pl
from jax.experimental.pallas import tpu as pltpu

D_MODEL = 1024
BATCH = 32
SEQ = 256
DEPTH = 2
DEC_BATCH = 2
DEC_SEQ = 2048
PAST_LEN = 512
GRID_W = 64
H_A, DK_A, DV_A = 4, 64, 128
GATE_RANK = 16
GATE_TAU = 16.0
H_B, DH_B = 4, 64
DV_B = 2 * DH_B
H_C, HKV_C, DH_C = 8, 2, 128
D_FF = 2816
N_EXPERTS = 8
D_FF_E = 3584
ROPE_THETA = 10000.0
EPS = 1e-6
A_QK = H_A * DK_A
A_V = H_A * DV_A
B_QK = H_B * 2 * DH_B
B_V = H_B * DV_B

F32 = jnp.float32
BF16 = jnp.bfloat16
LOG2E = math.log2(math.e)

V7X_VMEM_BYTES = 64 * 1024 * 1024
VMEM_LIMIT = V7X_VMEM_BYTES * 7 // 8
LANES = 128
MXU_N = 256
TM = 256
N_CTX = BATCH * SEQ
N_LAT = DEC_BATCH * DEC_SEQ
M_TOK = N_CTX + N_LAT
NB = M_TOK // TM
NCB = N_CTX // TM
LBB = DEC_SEQ // TM
N_COND = 8
GLA_C = 64
GLA_SB = 8
TME = 512
NT_E = 2 * M_TOK // TME + N_EXPERTS
NC_E = 2
TF_E = D_FF_E // NC_E
TQ_LAT = 256
TMC = 2 * SEQ
TMB = 512

assert SEQ == TM and DEC_SEQ % TM == 0 and DEPTH == 2


def _cparams(sem):
    return pltpu.CompilerParams(dimension_semantics=sem, vmem_limit_bytes=VMEM_LIMIT)


def _group(i, tm=TM):
    return jnp.where(i < N_CTX // tm, 0, 1 + (i - N_CTX // tm) // (DEC_SEQ // tm))


def _pos_block(i, tm=TM):
    return jnp.where(i < N_CTX // tm, 0, (i - N_CTX // tm) % (DEC_SEQ // tm))


def _mod_spec(chunk, tm=TM):
    return pl.BlockSpec((None, 1, D_MODEL), lambda i, c=chunk: (_group(i, tm), 0, c))


def _row_spec(width, col=0, tm=TM):
    return pl.BlockSpec((tm, width), lambda i, c=col: (i, c))


def _ctx_spec(width, tm=TM):
    return pl.BlockSpec((tm, width), lambda i: (jnp.minimum(i, N_CTX // tm - 1), 0))


def _ctx_parts_spec(parts, tm=TM):
    return pl.BlockSpec((tm * parts, LANES), lambda i: (jnp.minimum(i, N_CTX // tm - 1), 0))


def _lat_spec(width, tm=TM):
    return pl.BlockSpec((tm, width), lambda i: (jnp.maximum(i - N_CTX // tm, 0), 0))


def _const_spec(shape):
    nd = len(shape)
    return pl.BlockSpec(shape, lambda i, nd=nd: (0,) * nd, pipeline_mode=pl.Buffered(1))


def _dot(a, b):
    return jnp.dot(a, b, preferred_element_type=F32)


def _dot_nt(a, b):
    return lax.dot_general(a, b, (((1,), (1,)), ((), ())), preferred_element_type=F32)


def _split2(a):
    hi = a.astype(BF16)
    lo = (a - hi.astype(F32)).astype(BF16)
    return hi, lo


def _dot_hi(a, b):
    a_hi, a_lo = _split2(a)
    b_hi, b_lo = _split2(b)
    return _dot(a_hi, b_hi) + (_dot(a_hi, b_lo) + _dot(a_lo, b_hi))


def _silu(x):
    return (0.5 * x) * (1.0 + jnp.tanh(0.5 * x))


def _log_sigmoid(x):
    return jnp.minimum(x, 0.0) - jnp.log(1.0 + jnp.exp(-jnp.abs(x)))


def _rms(x, g):
    return x * lax.rsqrt(jnp.mean(x * x, axis=-1, keepdims=True) + EPS) * g


def _norm_mod(x, gamma, sc, sh):
    return _rms(x, gamma) * (1.0 + sc) + sh


def _staged(parts):
    live = list(parts)
    while live:
        live = [g for g in live if next(g, True) is None]


def _copy_ctx_transposed(stage3_ref, dst_ref):
    @pl.when(pl.program_id(0) < NCB)
    def _():
        def body(j, carry):
            dst_ref[pl.ds(pl.multiple_of(j * LANES, LANES), LANES), :] = jnp.transpose(stage3_ref[j])
            return carry

        lax.fori_loop(0, stage3_ref.shape[0], body, 0)


def _is_lat(tm):
    return pl.program_id(0) >= N_CTX // tm


def _pick_rows(ctx_ref, lat_ref):
    return jnp.where(_is_lat(ctx_ref.shape[0]), lat_ref[...], ctx_ref[...])


def _copy_ctx(stage_ref, outs):
    piece = 64

    @pl.when(jnp.logical_not(_is_lat(stage_ref.shape[0])))
    def _():
        def body(r, carry):
            rows = pl.ds(pl.multiple_of(r * piece, piece), piece)
            for lo, width, parts, dst_ref in outs:
                for p in range(parts):
                    val = stage_ref[rows, lo + p * width:lo + (p + 1) * width]
                    if width < LANES:
                        val = jnp.concatenate([val, jnp.zeros((piece, LANES - width), F32)], axis=1)
                    dst_ref[pl.ds(pl.multiple_of(r * piece * parts, piece * parts) + p, piece, stride=parts), :] = val
            return carry

        lax.fori_loop(0, stage_ref.shape[0] // piece, body, 0)


def _rope(x, cos, sin, half):
    lane = lax.broadcasted_iota(jnp.int32, x.shape, 1)
    first = (lane % (2 * half)) < half
    swapped = jnp.where(first, pltpu.roll(x, LANES - half, 1), pltpu.roll(x, half, 1))
    return x * cos + swapped * sin


def _mod_kernel(cond_ref, w_ref, b_ref, o_ref):
    o_ref[...] = _dot_hi(_silu(cond_ref[...]), w_ref[...]) + b_ref[...]


def _modulation(cond, w_mod, b_mod):
    tn = 1536
    return pl.pallas_call(
        _mod_kernel,
        grid=(DEPTH, 6 * D_MODEL // tn),
        in_specs=[
            pl.BlockSpec((N_COND, D_MODEL), lambda l, j: (0, 0)),
            pl.BlockSpec((None, D_MODEL, tn), lambda l, j: (l, 0, j)),
            pl.BlockSpec((None, 1, tn), lambda l, j: (l, 0, j)),
        ],
        out_specs=pl.BlockSpec((None, N_COND, tn), lambda l, j: (l, 0, j)),
        out_shape=jax.ShapeDtypeStruct((DEPTH, N_COND, 6 * D_MODEL), F32),
        compiler_params=_cparams(("arbitrary", "arbitrary")),
        name="modulation",
    )(cond, w_mod, b_mod.reshape(DEPTH, 1, 6 * D_MODEL))


def _inproj_even_kernel(xc_ref, xl_ref, gam_ref, sh_ref, sc_ref, w_ref, wg_ref, g2_ref, bg_ref, cos_ref, sin_ref,
                        a_ref, glog_ref, qb_ref, kb_ref, vb_ref, kctx_ref, vctx_ref, stage_ref, stagek_ref):
    is_lat = pl.program_id(0) >= NCB
    na = 2 * A_QK + 2 * A_V

    def part(rows):
        n = rows.stop - rows.start
        x = jnp.where(is_lat, xl_ref[rows, :], xc_ref[rows, :])
        hb = _norm_mod(x, gam_ref[...], sc_ref[...], sh_ref[...]).astype(BF16)
        cos = jnp.where(is_lat, cos_ref[rows, :], 1.0)
        sin = jnp.where(is_lat, sin_ref[rows, :], 0.0)
        yield
        for lo in range(0, na, MXU_N):
            a_ref[rows, lo:lo + MXU_N] = _dot(hb, w_ref[:, lo:lo + MXU_N])
        for lo in range(0, B_QK, MXU_N):
            zq = _dot(hb, w_ref[:, na + lo:na + lo + MXU_N])
            zk = _dot(hb, w_ref[:, na + B_QK + lo:na + B_QK + lo + MXU_N])
            zv = _dot(hb, w_ref[:, na + 2 * B_QK + lo:na + 2 * B_QK + lo + MXU_N])
            stage_ref[rows, lo:lo + MXU_N] = zv
            for j in range(0, MXU_N, LANES):
                cs = slice(lo + j, lo + j + LANES)
                vb_ref[rows, 2 * (lo + j):2 * (lo + j) + DV_B] = zv[:, j:j + LANES].astype(BF16)
                vb_ref[rows, 2 * (lo + j) + DV_B:2 * (lo + j + LANES)] = jnp.ones((n, DV_B), BF16)
                stagek_ref[(lo + j) // LANES, rows, :] = zk[:, j:j + LANES]
                qb_ref[rows, cs] = (_rope(zq[:, j:j + LANES], cos, sin, DH_B // 4) * LOG2E).astype(BF16)
                kb_ref[rows, cs] = _rope(zk[:, j:j + LANES], cos, sin, DH_B // 4).astype(BF16)
        gates = _dot(hb, wg_ref[...])
        xg = _dot(gates.astype(BF16), g2_ref[...]) + bg_ref[...]
        glog_ref[rows, :] = _log_sigmoid(xg) * (1.0 / GATE_TAU)

    _staged([part(slice(0, TM))])
    _copy_ctx(stage_ref, ((0, DV_B, H_B, vctx_ref),))
    _copy_ctx_transposed(stagek_ref, kctx_ref)


def _inproj_even(x_ctx, x_lat, gamma, mod, w_main, w_gates, g2, bg, cos, sin):
    na = 2 * A_QK + 2 * A_V
    nz = na + 2 * B_QK + B_V
    return pl.pallas_call(
        _inproj_even_kernel,
        grid=(NB,),
        in_specs=[
            _ctx_spec(D_MODEL), _lat_spec(D_MODEL), _const_spec((1, D_MODEL)), _mod_spec(0), _mod_spec(1),
            _const_spec((D_MODEL, nz)), _const_spec((D_MODEL, LANES)), _const_spec((LANES, 2 * A_QK)),
            _const_spec((1, 2 * A_QK)),
            pl.BlockSpec((TM, LANES), lambda i: (_pos_block(i), 0)),
            pl.BlockSpec((TM, LANES), lambda i: (_pos_block(i), 0)),
        ],
        out_specs=[_pair_spec(na), _pair_spec(2 * A_QK), _row_spec(B_QK), _row_spec(B_QK), _row_spec(2 * B_V),
                   pl.BlockSpec((None, B_QK, TM), lambda i: (jnp.minimum(i, NCB - 1), 0, 0)), _ctx_parts_spec(H_B)],
        out_shape=[
            jax.ShapeDtypeStruct((M_TOK, na), F32), jax.ShapeDtypeStruct((M_TOK, 2 * A_QK), F32),
            jax.ShapeDtypeStruct((M_TOK, B_QK), BF16), jax.ShapeDtypeStruct((M_TOK, B_QK), BF16),
            jax.ShapeDtypeStruct((M_TOK, 2 * B_V), BF16), jax.ShapeDtypeStruct((BATCH, B_QK, SEQ), F32),
            jax.ShapeDtypeStruct((N_CTX * H_B, LANES), F32),
        ],
        scratch_shapes=[pltpu.VMEM((TM, B_V), F32), pltpu.VMEM((B_QK // LANES, TM, LANES), F32)],
        compiler_params=_cparams(("arbitrary",)),
        name="inproj_even",
    )(x_ctx, x_lat, gamma, mod, mod, w_main, w_gates, g2, bg, cos, sin)


def _gla_chunk(q_ref, k_ref, v_ref, g_ref, o_ref, c, s_ref, a_ref, b_ref, m1_ref, head_mask, same_block, rev):
    q, k, v = q_ref[c], k_ref[c], v_ref[c]
    g = g_ref[c] * LOG2E
    v_b = v.astype(BF16)
    r_i = lax.broadcasted_iota(jnp.int32, (GLA_C, GLA_C), 0)
    c_i = lax.broadcasted_iota(jnp.int32, (GLA_C, GLA_C), 1)
    tri = jnp.where((c_i >= r_i) if rev else (c_i <= r_i), 1.0, 0.0).astype(BF16)
    g_hi = g.astype(BF16)
    g_r = g - g_hi.astype(F32)
    g_mid = g_r.astype(BF16)
    g_lo = (g_r - g_mid.astype(F32)).astype(BF16)
    bc = _dot(tri, g_hi) + (_dot(tri, g_mid) + _dot(tri, g_lo))
    b_ref[...] = bc
    yield
    t_i = lax.broadcasted_iota(jnp.int32, (GLA_SB, A_QK), 0)
    group = 2 * GLA_SB
    for i0 in range(0, GLA_C, group):
        blocks = range(i0, i0 + group, GLA_SB)
        q_i = [q_ref[c, i:i + GLA_SB, :] for i in blocks]
        b_i = [b_ref[i:i + GLA_SB, :] for i in blocks]
        for j in range(GLA_SB):
            keep = (t_i <= j) if rev else (t_i >= j)
            pair = []
            for n, i in enumerate(blocks):
                k_s = k_ref[c, i + j:i + j + 1, :]
                b_s = b_ref[i + j:i + j + 1, :]
                pair.append(jnp.where(keep, q_i[n] * k_s * jnp.exp2(b_i[n] - b_s), 0.0))
            a_ref[j * GLA_C + i0:j * GLA_C + i0 + group, :] = jnp.concatenate(pair, axis=0).astype(BF16)
    yield
    last = bc[0:1] if rev else bc[GLA_C - 1:GLA_C]
    qc = q * jnp.exp2(bc)
    kl = k * jnp.exp2(last - bc)
    row = lax.broadcasted_iota(jnp.int32, (GLA_C, A_QK), 0)

    def expand(x):
        return jnp.where(head_mask, jnp.concatenate([x] * H_A, axis=0), 0.0).astype(BF16)

    scores = None
    half = GLA_C // 2
    while half >= GLA_SB:
        mids = [bc[b + half:b + half + 1] if rev else bc[b + half - 1:b + half] for b in range(0, GLA_C, 2 * half)]
        ref = mids[0] if len(mids) == 1 else jnp.concatenate(
            [jnp.broadcast_to(m, (2 * half, A_QK)) for m in mids], axis=0)
        upper = (row // half) % 2 == 1
        later, earlier = (jnp.logical_not(upper), upper) if rev else (upper, jnp.logical_not(upper))
        q_l = jnp.where(later, q * jnp.exp2(bc - ref), 0.0)
        k_e = jnp.where(earlier, k * jnp.exp2(ref - bc), 0.0).astype(BF16)
        term = _dot_nt(expand(q_l), k_e)
        if 2 * half < GLA_C:
            term = jnp.where(same_block[2 * half], term, 0.0)
        scores = term if scores is None else scores + term
        half //= 2
    s_old = s_ref[...]
    inter = _dot(expand(qc), s_old.astype(BF16))
    sums = _dot(a_ref[...], m1_ref[...])
    yield
    lane_c = lax.broadcasted_iota(jnp.int32, (GLA_C, LANES), 1)
    within = None
    for j in range(GLA_SB):
        part = jnp.where(lane_c % GLA_SB == j, sums[j * GLA_C:(j + 1) * GLA_C], 0.0)
        within = part if within is None else within + part
    lane = lax.broadcasted_iota(jnp.int32, (GLA_SB, LANES), 1)
    per_head = []
    for h in range(H_A):
        rows_h = []
        for i in range(0, GLA_C, GLA_SB):
            moved = pltpu.roll(within[i:i + GLA_SB], (i - h * GLA_SB) % LANES, 1)
            rows_h.append(jnp.where(lane // GLA_SB == i // GLA_SB, moved, 0.0))
        per_head.append(jnp.concatenate(rows_h, axis=0)[:, :GLA_C])
    scores = scores + jnp.concatenate(per_head, axis=0)
    o_heads = []
    for h in range(H_A):
        hs = slice(h * GLA_C, (h + 1) * GLA_C)
        o_heads.append(inter[hs] + _dot(scores[hs].astype(BF16), v_b[:, h * DV_A:(h + 1) * DV_A]))
    o_ref[c] = jnp.concatenate(o_heads, axis=1)
    yield
    t = jnp.transpose(jnp.concatenate([kl, jnp.broadcast_to(last, (GLA_C, A_QK))], axis=0))
    kv = _dot(t[:, :GLA_C].astype(BF16), v_b)
    a_col = jnp.exp2(t[:, GLA_C:GLA_C + 1])
    for h in range(H_A):
        hs = slice(h * DK_A, (h + 1) * DK_A)
        s_ref[hs, :] = a_col[hs] * s_old[hs] + kv[hs, h * DV_A:(h + 1) * DV_A]


def _gla_kernel(fblk_ref, bblk_ref, first_ref, inita_ref, initb_ref,
                qf_ref, kf_ref, vf_ref, gf_ref, qr_ref, kr_ref, vr_ref, gr_ref, s0a_ref, s0b_ref, m1_ref,
                of_ref, ob_ref, sfin_ref, s_refs, a_refs, b_refs):
    step = pl.program_id(0)

    @pl.when(first_ref[step] == 1)
    def _():
        for seq, s0_ref in enumerate((s0a_ref, s0b_ref)):
            for d in range(2):
                s_refs[2 * seq + d] = s0_ref[d]

    r_h = lax.broadcasted_iota(jnp.int32, (H_A * GLA_C, A_QK), 0) // GLA_C
    c_h = lax.broadcasted_iota(jnp.int32, (H_A * GLA_C, A_QK), 1) // DK_A
    head_mask = r_h == c_h
    t_s = lax.broadcasted_iota(jnp.int32, (H_A * GLA_C, GLA_C), 0) % GLA_C
    s_s = lax.broadcasted_iota(jnp.int32, (H_A * GLA_C, GLA_C), 1)
    same_block = {}
    size = 2 * GLA_SB
    while size < GLA_C:
        same_block[size] = (t_s // size) == (s_s // size)
        size *= 2
    nch = TM // GLA_C

    def body(c, carry):
        chains = []
        for seq in range(2):
            chains.append(_gla_chunk(qf_ref, kf_ref, vf_ref, gf_ref, of_ref, seq * nch + c, s_refs.at[2 * seq],
                                     a_refs.at[2 * seq], b_refs.at[2 * seq], m1_ref, head_mask, same_block, False))
            chains.append(_gla_chunk(qr_ref, kr_ref, vr_ref, gr_ref, ob_ref, seq * nch + nch - 1 - c,
                                     s_refs.at[2 * seq + 1], a_refs.at[2 * seq + 1], b_refs.at[2 * seq + 1], m1_ref,
                                     head_mask, same_block, True))
        _staged(chains)
        return carry

    lax.fori_loop(0, nch, body, 0)

    @pl.when(step < BATCH // 2)
    def _():
        for seq in range(2):
            for d in range(2):
                sfin_ref[seq, d] = s_refs[2 * seq + d]


def _pair_block(i):
    j = i - NCB
    return jnp.where(i < NCB, i, NCB + 2 * (j % LBB) + j // LBB)


def _pair_spec(width, col=0):
    return pl.BlockSpec((TM, width), lambda i, c=col: (_pair_block(i), c))


def _gla_tables():
    fblk, bblk, first, init_a, init_b = [], [], [], [], []
    for p in range(BATCH // 2):
        fblk.append(p), bblk.append(p), first.append(1), init_a.append(0), init_b.append(0)
    for j in range(LBB):
        fblk.append(NCB // 2 + j)
        bblk.append(NCB // 2 + LBB - 1 - j)
        first.append(1 if j == 0 else 0)
        init_a.append(1), init_b.append(2)
    return [jnp.asarray(np.array(t, np.int32)) for t in (fblk, bblk, first, init_a, init_b)]


def _gla(a_proj, glog, s0_all, m1):
    tables = _gla_tables()
    nsteps = int(tables[0].shape[0])
    nseq = BATCH + DEC_BATCH

    def fmap(col):
        return lambda i, fb, bb, fi, ia, ib: (fb[i], 0, col)

    def rmap(col):
        return lambda i, fb, bb, fi, ia, ib: (bb[i], 0, col)

    nch = 2 * TM // GLA_C
    a3 = a_proj.reshape(M_TOK // GLA_C, GLA_C, a_proj.shape[1])
    g3 = glog.reshape(M_TOK // GLA_C, GLA_C, glog.shape[1])
    gs = pltpu.PrefetchScalarGridSpec(
        num_scalar_prefetch=5,
        grid=(nsteps,),
        in_specs=[
            pl.BlockSpec((nch, GLA_C, A_QK), fmap(0)), pl.BlockSpec((nch, GLA_C, A_QK), fmap(1)),
            pl.BlockSpec((nch, GLA_C, A_V), fmap(1)), pl.BlockSpec((nch, GLA_C, A_QK), fmap(0)),
            pl.BlockSpec((nch, GLA_C, A_QK), rmap(0)), pl.BlockSpec((nch, GLA_C, A_QK), rmap(1)),
            pl.BlockSpec((nch, GLA_C, A_V), rmap(1)), pl.BlockSpec((nch, GLA_C, A_QK), rmap(1)),
            pl.BlockSpec((None, 2, A_QK, DV_A), lambda i, fb, bb, fi, ia, ib: (ia[i], 0, 0, 0)),
            pl.BlockSpec((None, 2, A_QK, DV_A), lambda i, fb, bb, fi, ia, ib: (ib[i], 0, 0, 0)),
            pl.BlockSpec((A_QK, LANES), lambda i, fb, bb, fi, ia, ib: (0, 0)),
        ],
        out_specs=[
            pl.BlockSpec((nch, GLA_C, A_V), fmap(0)), pl.BlockSpec((nch, GLA_C, A_V), rmap(0)),
            pl.BlockSpec((2, 2, A_QK, DV_A),
                         lambda i, fb, bb, fi, ia, ib: (jnp.minimum(fb[i], BATCH // 2 - 1), 0, 0, 0)),
        ],
        scratch_shapes=[
            pltpu.VMEM((4, A_QK, DV_A), F32),
            pltpu.VMEM((4, GLA_SB * GLA_C, A_QK), BF16),
            pltpu.VMEM((4, GLA_C, A_QK), F32),
        ],
    )
    o_f, o_b, s_fin = pl.pallas_call(
        _gla_kernel,
        grid_spec=gs,
        out_shape=[
            jax.ShapeDtypeStruct((M_TOK // GLA_C, GLA_C, A_V), F32),
            jax.ShapeDtypeStruct((M_TOK // GLA_C, GLA_C, A_V), F32),
            jax.ShapeDtypeStruct((BATCH, 2, A_QK, DV_A), F32),
        ],
        compiler_params=_cparams(("arbitrary",)),
        name="gla",
    )(*tables, a3, a3, a3, g3, a3, a3, a3, g3, s0_all, s0_all, m1)
    return o_f.reshape(M_TOK, A_V), o_b.reshape(M_TOK, A_V), s_fin


def _attend(q_b, kv, out):
    scores = [_dot_nt(q_b, k_b) for k_b, _ in kv]
    yield
    top = functools.reduce(jnp.maximum, [jnp.max(s2, axis=-1, keepdims=True) for s2 in scores])
    both = None
    for s2, (_, v_ones) in zip(scores, kv):
        part = _dot(jnp.exp2(s2 - top).astype(BF16), v_ones)
        both = part if both is None else both + part
    yield
    dv = both.shape[1] // 2
    out.append(both[:, :dv] * (1.0 / both[:, dv:]))


def _seq_rows(ref):
    return [slice(r, r + SEQ) for r in range(0, ref.shape[0], SEQ)]


def _diffattn_body(q_ref, kv_refs, lq1_ref, lk1_ref, lq2_ref, lk2_ref, gsub_ref, o_ref, lam_init, seqs):
    lam = (jnp.exp(jnp.sum(lq1_ref[...] * lk1_ref[...], axis=-1, keepdims=True))
           - jnp.exp(jnp.sum(lq2_ref[...] * lk2_ref[...], axis=-1, keepdims=True)) + lam_init)
    outs, parts = [], []
    for rows in seqs:
        lane = lax.broadcasted_iota(jnp.int32, q_ref[rows, :LANES].shape, 1)
        for h in range(H_B):
            hs = slice(h * LANES, (h + 1) * LANES)
            q_h = q_ref[rows, hs]
            kv = [(k_ref[rows if n == 0 else slice(None), hs],
                   v_ref[rows if n == 0 else slice(None), 2 * h * DV_B:2 * (h + 1) * DV_B])
                  for n, (k_ref, v_ref) in enumerate(kv_refs)]
            zero = jnp.zeros_like(q_h)
            outs.append((rows, h, [], []))
            parts.append(_attend(jnp.where(lane < DH_B, q_h, zero), kv, outs[-1][2]))
            parts.append(_attend(jnp.where(lane >= DH_B, q_h, zero), kv, outs[-1][3]))
    _staged(parts)
    for rows, h, o1, o2 in outs:
        o = o1[0] - lam * o2[0]
        o_ref[rows, h * LANES:(h + 1) * LANES] = (_rms(o, gsub_ref[...]) * (1.0 - lam_init)).astype(BF16)


def _diffattn_kernel(q_ref, k_ref, v_ref, lq1, lk1, lq2, lk2, gsub, o_ref, *, lam_init):
    _diffattn_body(q_ref, [(k_ref, v_ref)], lq1, lk1, lq2, lk2, gsub, o_ref, lam_init, _seq_rows(q_ref))


def _diffattn_cached_kernel(q_ref, k_ref, v_ref, kc_ref, vc_ref, lq1, lk1, lq2, lk2, gsub, o_ref, *, lam_init):
    _diffattn_body(q_ref, [(k_ref, v_ref), (kc_ref, vc_ref)], lq1, lk1, lq2, lk2, gsub, o_ref, lam_init,
                   [slice(None)])


def _diffattn(qb, kb, vb, k_cache, v_cache, lam_params, gsub, lam_init):
    small = [pl.BlockSpec((1, DH_B), lambda *_: (0, 0))] * 4 + [pl.BlockSpec((1, DV_B), lambda *_: (0, 0))]
    body = functools.partial(_diffattn_kernel, lam_init=lam_init)
    ctx = pl.pallas_call(
        body,
        grid=(N_CTX // TMC,),
        in_specs=[_row_spec(B_QK, tm=TMC), _row_spec(B_QK, tm=TMC), _row_spec(2 * B_V, tm=TMC)] + small,
        out_specs=_row_spec(B_V, tm=TMC),
        out_shape=jax.ShapeDtypeStruct((N_CTX, B_V), BF16),
        compiler_params=_cparams(("arbitrary",)),
        name="diffattn_ctx",
    )(qb, kb, vb, *lam_params, gsub)
    nqb = DEC_SEQ // TQ_LAT
    off = N_CTX // TQ_LAT
    seq0 = N_CTX // DEC_SEQ
    lat = pl.pallas_call(
        functools.partial(_diffattn_cached_kernel, lam_init=lam_init),
        grid=(DEC_BATCH, nqb),
        in_specs=[
            pl.BlockSpec((TQ_LAT, B_QK), lambda b, j: (off + b * nqb + j, 0)),
            pl.BlockSpec((DEC_SEQ, B_QK), lambda b, j: (seq0 + b, 0)),
            pl.BlockSpec((DEC_SEQ, 2 * B_V), lambda b, j: (seq0 + b, 0)),
            pl.BlockSpec((None, PAST_LEN, B_QK), lambda b, j: (b, 0, 0)),
            pl.BlockSpec((None, PAST_LEN, 2 * B_V), lambda b, j: (b, 0, 0)),
        ] + small,
        out_specs=pl.BlockSpec((TQ_LAT, B_V), lambda b, j: (b * nqb + j, 0)),
        out_shape=jax.ShapeDtypeStruct((N_LAT, B_V), BF16),
        compiler_params=_cparams(("arbitrary", "arbitrary")),
        name="diffattn_lat",
    )(qb, kb, vb, k_cache, v_cache, *lam_params, gsub)
    return ctx, lat


def _swiglu_act(xb, wg_ref, wu_ref, act_ref):
    for lo in range(0, act_ref.shape[1], MXU_N):
        cs = slice(lo, lo + MXU_N)
        act_ref[:, cs] = (_silu(_dot(xb, wg_ref[:, cs])) * _dot(xb, wu_ref[:, cs])).astype(BF16)


def _mix_ffn_even_kernel(of_ref, ob_ref, ra_ref, actx_ref, alat_ref, xc_ref, xl_ref, g1_ref, ggla_ref, wo_ref,
                         gam_ref, sh_ref, sc_ref, g2_ref, wg_ref, wu_ref, wd_ref, o_ref, act_ref):
    is_lat = _is_lat(TM)

    def part(rows):
        heads = []
        for h in range(H_A):
            hs = slice(h * DV_A, (h + 1) * DV_A)
            heads.append((_rms(of_ref[rows, hs] + ob_ref[rows, hs], ggla_ref[...])
                          * _silu(ra_ref[rows, hs])).astype(BF16))
        mix = jnp.concatenate(heads + [jnp.where(is_lat, alat_ref[rows, :], actx_ref[rows, :])], axis=1)
        yield
        x = jnp.where(is_lat, xl_ref[rows, :], xc_ref[rows, :]) + g1_ref[...] * _dot(mix, wo_ref[...])
        hb = _norm_mod(x, gam_ref[...], sc_ref[...], sh_ref[...]).astype(BF16)
        yield
        for lo in range(0, D_FF, MXU_N):
            cs = slice(lo, lo + MXU_N)
            act_ref[rows, cs] = (_silu(_dot(hb, wg_ref[:, cs])) * _dot(hb, wu_ref[:, cs])).astype(BF16)
        yield
        o_ref[rows, :] = x + g2_ref[...] * _dot(act_ref[rows, :], wd_ref[...])

    _staged([part(slice(r, r + TM // 2)) for r in range(0, TM, TM // 2)])


def _mix_ffn_even(o_f, o_b, a_proj, attn_ctx, attn_lat, x_ctx, x_lat, mod, g_gla, w_out, gamma, w_gate, w_up, w_down):
    return pl.pallas_call(
        _mix_ffn_even_kernel,
        grid=(NB,),
        in_specs=[_pair_spec(A_V), _pair_spec(A_V), _pair_spec(A_V, 2), _ctx_spec(B_V), _lat_spec(B_V),
                  _ctx_spec(D_MODEL), _lat_spec(D_MODEL), _mod_spec(2), _const_spec((1, DV_A)),
                  _const_spec((A_V + B_V, D_MODEL)),
                  _const_spec((1, D_MODEL)), _mod_spec(3), _mod_spec(4), _mod_spec(5),
                  _const_spec((D_MODEL, D_FF)), _const_spec((D_MODEL, D_FF)), _const_spec((D_FF, D_MODEL))],
        out_specs=_row_spec(D_MODEL),
        out_shape=jax.ShapeDtypeStruct((M_TOK, D_MODEL), F32),
        scratch_shapes=[pltpu.VMEM((TM, D_FF), BF16)],
        compiler_params=_cparams(("arbitrary",)),
        name="mix_ffn_even",
    )(o_f, o_b, a_proj, attn_ctx, attn_lat, x_ctx, x_lat, mod, g_gla, w_out, gamma, mod, mod, mod,
      w_gate, w_up, w_down)


def _inproj_odd_kernel(x_ref, gam_ref, sh_ref, sc_ref, w_ref, gq_ref, gk_ref, cos_ref, sin_ref,
                       q_ref, k_ref, v_ref, kctx_ref, vctx_ref, stage_ref):
    is_lat = _is_lat(TMB)
    scale = DH_C ** -0.5 * LOG2E
    nq, nkv = H_C * DH_C, HKV_C * DH_C

    def part(rows):
        n = rows.stop - rows.start
        hb = _norm_mod(x_ref[rows, :], gam_ref[...], sc_ref[...], sh_ref[...]).astype(BF16)
        cos = jnp.where(is_lat, cos_ref[rows, :], 1.0)
        sin = jnp.where(is_lat, sin_ref[rows, :], 0.0)
        yield
        z = _dot(hb, w_ref[:, nq:nq + nkv])
        for h in range(HKV_C):
            hs = slice(h * DH_C, (h + 1) * DH_C)
            k_n = _rms(z[:, hs], gk_ref[...])
            stage_ref[rows, hs] = k_n
            k_ref[rows, hs] = _rope(k_n, cos, sin, DH_C // 4).astype(BF16)
        v = _dot(hb, w_ref[:, nq + nkv:])
        stage_ref[rows, nkv:] = v
        for h in range(HKV_C):
            v_ref[rows, 2 * h * DH_C:(2 * h + 1) * DH_C] = v[:, h * DH_C:(h + 1) * DH_C].astype(BF16)
            v_ref[rows, (2 * h + 1) * DH_C:2 * (h + 1) * DH_C] = jnp.ones((n, DH_C), BF16)
        for lo in range(0, nq, MXU_N):
            z = _dot(hb, w_ref[:, lo:lo + MXU_N])
            for j in range(0, MXU_N, DH_C):
                q_ref[rows, lo + j:lo + j + DH_C] = (
                    _rope(_rms(z[:, j:j + DH_C], gq_ref[...]), cos, sin, DH_C // 4) * scale).astype(BF16)

    _staged([part(slice(r, r + TM)) for r in range(0, TMB, TM)])
    _copy_ctx(stage_ref, ((0, DH_C, HKV_C, kctx_ref), (nkv, DH_C, HKV_C, vctx_ref)))


def _inproj_odd(x, gamma, mod, w_in, g_q, g_k, cos, sin):
    nq, nkv = H_C * DH_C, HKV_C * DH_C
    return pl.pallas_call(
        _inproj_odd_kernel,
        grid=(M_TOK // TMB,),
        in_specs=[
            _row_spec(D_MODEL, tm=TMB), _const_spec((1, D_MODEL)), _mod_spec(0, TMB), _mod_spec(1, TMB),
            _const_spec((D_MODEL, nq + 2 * nkv)), _const_spec((1, DH_C)), _const_spec((1, DH_C)),
            pl.BlockSpec((TMB, LANES), lambda i: (_pos_block(i, TMB), 0)),
            pl.BlockSpec((TMB, LANES), lambda i: (_pos_block(i, TMB), 0)),
        ],
        out_specs=[_row_spec(nq, tm=TMB), _row_spec(nkv, tm=TMB), _row_spec(2 * nkv, tm=TMB),
                   _ctx_parts_spec(HKV_C, TMB), _ctx_parts_spec(HKV_C, TMB)],
        out_shape=[
            jax.ShapeDtypeStruct((M_TOK, nq), BF16), jax.ShapeDtypeStruct((M_TOK, nkv), BF16),
            jax.ShapeDtypeStruct((M_TOK, 2 * nkv), BF16), jax.ShapeDtypeStruct((N_CTX * HKV_C, LANES), F32),
            jax.ShapeDtypeStruct((N_CTX * HKV_C, LANES), F32),
        ],
        scratch_shapes=[pltpu.VMEM((TMB, 2 * nkv), F32)],
        compiler_params=_cparams(("arbitrary",)),
        name="inproj_odd",
    )(x, gamma, mod, mod, w_in, g_q, g_k, cos, sin)


def _gqa_body(q_ref, kv_refs, o_ref, seqs):
    rep = H_C // HKV_C
    outs, parts = [], []
    for rows in seqs:
        for hk in range(HKV_C):
            ks = slice(hk * DH_C, (hk + 1) * DH_C)
            q_g = jnp.concatenate(
                [q_ref[rows, (hk * rep + g) * DH_C:(hk * rep + g + 1) * DH_C] for g in range(rep)], axis=0)
            kv = [(k_ref[rows if n == 0 else slice(None), ks],
                   v_ref[rows if n == 0 else slice(None), 2 * hk * DH_C:2 * (hk + 1) * DH_C])
                  for n, (k_ref, v_ref) in enumerate(kv_refs)]
            outs.append((rows, hk, []))
            parts.append(_attend(q_g, kv, outs[-1][2]))
    _staged(parts)
    for rows, hk, o in outs:
        tq = o[0].shape[0] // rep
        for g in range(rep):
            o_ref[rows, (hk * rep + g) * DH_C:(hk * rep + g + 1) * DH_C] = o[0][g * tq:(g + 1) * tq].astype(BF16)


def _gqa_kernel(q_ref, k_ref, v_ref, o_ref):
    _gqa_body(q_ref, [(k_ref, v_ref)], o_ref, _seq_rows(q_ref))


def _gqa_cached_kernel(q_ref, k_ref, v_ref, kc_ref, vc_ref, o_ref):
    _gqa_body(q_ref, [(k_ref, v_ref), (kc_ref, vc_ref)], o_ref, [slice(None)])


def _gqa(q, k, v, k_cache, v_cache):
    nq, nkv = H_C * DH_C, HKV_C * DH_C
    ctx = pl.pallas_call(
        _gqa_kernel,
        grid=(N_CTX // TMC,),
        in_specs=[_row_spec(nq, tm=TMC), _row_spec(nkv, tm=TMC), _row_spec(2 * nkv, tm=TMC)],
        out_specs=_row_spec(nq, tm=TMC),
        out_shape=jax.ShapeDtypeStruct((N_CTX, nq), BF16),
        compiler_params=_cparams(("arbitrary",)),
        name="gqa_ctx",
    )(q, k, v)
    nqb = DEC_SEQ // TQ_LAT
    off = N_CTX // TQ_LAT
    seq0 = N_CTX // DEC_SEQ
    lat = pl.pallas_call(
        _gqa_cached_kernel,
        grid=(DEC_BATCH, nqb),
        in_specs=[
            pl.BlockSpec((TQ_LAT, nq), lambda b, j: (off + b * nqb + j, 0)),
            pl.BlockSpec((DEC_SEQ, nkv), lambda b, j: (seq0 + b, 0)),
            pl.BlockSpec((DEC_SEQ, 2 * nkv), lambda b, j: (seq0 + b, 0)),
            pl.BlockSpec((None, PAST_LEN, nkv), lambda b, j: (b, 0, 0)),
            pl.BlockSpec((None, PAST_LEN, 2 * nkv), lambda b, j: (b, 0, 0)),
        ],
        out_specs=pl.BlockSpec((TQ_LAT, nq), lambda b, j: (b * nqb + j, 0)),
        out_shape=jax.ShapeDtypeStruct((N_LAT, nq), BF16),
        compiler_params=_cparams(("arbitrary", "arbitrary")),
        name="gqa_lat",
    )(q, k, v, k_cache, v_cache)
    return ctx, lat


def _postmix_odd_kernel(actx_ref, alat_ref, x_ref, g1_ref, wo_ref, gam_ref, sh_ref, sc_ref, wrh_ref, wrl_ref,
                        x1_ref, h_ref, w1_ref, w2_ref, idx_ref, cnt_ref, run0_ref, run1_ref):
    @pl.when(pl.program_id(0) == 0)
    def _():
        run0_ref[...] = jnp.zeros_like(run0_ref)
        run1_ref[...] = jnp.zeros_like(run1_ref)

    is_lat = _is_lat(TMB)
    lane = lax.broadcasted_iota(jnp.int32, (TMB, LANES), 1)
    lane_f = lane.astype(F32)
    picked = []

    def part(rows):
        attn = jnp.where(is_lat, alat_ref[rows, :], actx_ref[rows, :])
        x1 = x_ref[rows, :] + g1_ref[...] * _dot(attn, wo_ref[...])
        x1_ref[rows, :] = x1
        yield
        h = _norm_mod(x1, gam_ref[...], sc_ref[...], sh_ref[...])
        h_hi, h_lo = _split2(h)
        h_ref[rows, :] = h_hi
        logits = _dot(h_hi, wrh_ref[...]) + (_dot(h_hi, wrl_ref[...]) + _dot(h_lo, wrh_ref[...]))
        yield
        lane_p = lax.broadcasted_iota(jnp.int32, logits.shape, 1).astype(F32)
        lg = jnp.where(lane_p < N_EXPERTS, logits, -jnp.inf)
        m1 = jnp.max(lg, axis=-1, keepdims=True)
        i1 = jnp.min(jnp.where(lg == m1, lane_p, float(LANES)), axis=-1, keepdims=True)
        lg2 = jnp.where(lane_p == i1, -jnp.inf, lg)
        m2 = jnp.max(lg2, axis=-1, keepdims=True)
        i2 = jnp.min(jnp.where(lg2 == m2, lane_p, float(LANES)), axis=-1, keepdims=True)
        e = jnp.exp(m2 - m1)
        w1 = 1.0 / (1.0 + e)
        w1_ref[rows, :] = jnp.broadcast_to(w1, logits.shape)
        w2_ref[rows, :] = jnp.broadcast_to(e * w1, logits.shape)
        picked.append((jnp.broadcast_to(i1, logits.shape), jnp.broadcast_to(i2, logits.shape)))

    _staged([part(slice(r, r + TM)) for r in range(0, TMB, TM)])
    i1 = jnp.concatenate([p[0] for p in picked], axis=0)
    i2 = jnp.concatenate([p[1] for p in picked], axis=0)
    oh1 = jnp.where(lane_f == i1, 1.0, 0.0)
    oh2 = jnp.where(lane_f == i2, 1.0, 0.0)
    r_i = lax.broadcasted_iota(jnp.int32, (TMB, TMB), 0)
    c_i = lax.broadcasted_iota(jnp.int32, (TMB, TMB), 1)
    before = jnp.where(c_i < r_i, 1.0, 0.0).astype(BF16)
    rank1 = jnp.sum((_dot(before, oh1.astype(BF16)) + run0_ref[0:1, :]) * oh1, axis=-1, keepdims=True)
    rank2 = jnp.sum((_dot(before, oh2.astype(BF16)) + run1_ref[0:1, :]) * oh2, axis=-1, keepdims=True)
    run0_ref[...] = run0_ref[...] + jnp.sum(oh1, axis=0, keepdims=True)
    run1_ref[...] = run1_ref[...] + jnp.sum(oh2, axis=0, keepdims=True)
    sub = lax.broadcasted_iota(jnp.int32, cnt_ref.shape, 0)
    cnt_ref[...] = jnp.where(sub == 0, run0_ref[...], run1_ref[...])
    quarter = LANES // 4
    cols = jnp.where(lane < quarter, i1, jnp.where(lane < 2 * quarter, i2, jnp.where(
        lane < 3 * quarter, rank1, rank2)))
    rows_t = jnp.transpose(cols)
    idx_ref[...] = jnp.concatenate(
        [rows_t[q * quarter:q * quarter + 1] for q in range(4)] + [jnp.zeros((4, TMB), F32)],
        axis=0).astype(jnp.int32)


def _postmix_odd(attn_ctx, attn_lat, x, mod, w_out, gamma, w_router):
    wr_hi = w_router.astype(BF16)
    wr_lo = (w_router - wr_hi.astype(F32)).astype(BF16)
    row = functools.partial(_row_spec, tm=TMB)
    return pl.pallas_call(
        _postmix_odd_kernel,
        grid=(M_TOK // TMB,),
        in_specs=[_ctx_spec(H_C * DH_C, TMB), _lat_spec(H_C * DH_C, TMB), row(D_MODEL), _mod_spec(2, TMB),
                  _const_spec((H_C * DH_C, D_MODEL)),
                  _const_spec((1, D_MODEL)), _mod_spec(3, TMB), _mod_spec(4, TMB), _const_spec((D_MODEL, LANES)),
                  _const_spec((D_MODEL, LANES))],
        out_specs=[row(D_MODEL), row(D_MODEL), row(LANES), row(LANES), pl.BlockSpec((8, TMB), lambda i: (0, i)),
                   _const_spec((8, LANES))],
        out_shape=[
            jax.ShapeDtypeStruct((M_TOK, D_MODEL), F32), jax.ShapeDtypeStruct((M_TOK, D_MODEL), BF16),
            jax.ShapeDtypeStruct((M_TOK, LANES), F32), jax.ShapeDtypeStruct((M_TOK, LANES), F32),
            jax.ShapeDtypeStruct((8, M_TOK), jnp.int32), jax.ShapeDtypeStruct((8, LANES), F32),
        ],
        scratch_shapes=[pltpu.VMEM((8, LANES), F32), pltpu.VMEM((8, LANES), F32)],
        compiler_params=_cparams(("arbitrary",)),
        name="postmix_odd",
    )(attn_ctx, attn_lat, x, mod, w_out, gamma, mod, mod, wr_hi, wr_lo)


def _cast_rows(src_ref, dst_ref, piece):
    def body(r, carry):
        rows = pl.ds(pl.multiple_of(r * piece, piece), piece)
        dst_ref[rows, :] = src_ref[rows, :].astype(BF16)
        return carry

    lax.fori_loop(0, src_ref.shape[0] // piece, body, 0)


def _experts_up_kernel(tile_ref, chunk_ref, exp_ref, first_ref, valid_ref, x_ref, wg_ref, wu_ref, act_ref,
                       wgb_ref, wub_ref):
    s = pl.program_id(0)

    @pl.when(first_ref[s] == 1)
    def _():
        _cast_rows(wg_ref, wgb_ref, 128)
        _cast_rows(wu_ref, wub_ref, 128)

    @pl.when(valid_ref[s] == 1)
    def _():
        _swiglu_act(x_ref[...].astype(BF16), wgb_ref, wub_ref, act_ref)

    @pl.when(valid_ref[s] == 0)
    def _():
        piece = 64

        def body(r, carry):
            act_ref[pl.ds(pl.multiple_of(r * piece, piece), piece), :] = jnp.zeros((piece, TF_E), BF16)
            return carry

        lax.fori_loop(0, TME // piece, body, 0)


def _experts_down_kernel(act_tile_ref, exp_ref, first_ref, valid_ref, act_ref, wd_ref, o_ref, wdb_ref):
    i = pl.program_id(0)

    @pl.when(first_ref[i] == 1)
    def _():
        _cast_rows(wd_ref, wdb_ref, 256)

    @pl.when(valid_ref[i] == 1)
    def _():
        o_ref[...] = _dot(act_ref[...], wdb_ref[...])

    @pl.when(valid_ref[i] == 0)
    def _():
        o_ref[...] = jnp.zeros_like(o_ref)


def _experts(up_tables, down_tables, xs, w_gate, w_up, w_down):
    def w_map(s, t, c, e, f, v):
        return e[s], 0, jnp.where(v[s] == 1, c[s], NC_E - 1)

    up = pl.pallas_call(
        _experts_up_kernel,
        grid_spec=pltpu.PrefetchScalarGridSpec(
            num_scalar_prefetch=5,
            grid=(NC_E * NT_E,),
            in_specs=[
                pl.BlockSpec((TME, D_MODEL), lambda s, t, c, e, f, v: (t[s], 0)),
                pl.BlockSpec((None, D_MODEL, TF_E), w_map), pl.BlockSpec((None, D_MODEL, TF_E), w_map),
            ],
            out_specs=pl.BlockSpec((TME, TF_E), lambda s, t, c, e, f, v: (t[s], c[s])),
            scratch_shapes=[pltpu.VMEM((D_MODEL, TF_E), BF16), pltpu.VMEM((D_MODEL, TF_E), BF16)],
        ),
        out_shape=jax.ShapeDtypeStruct((NT_E * TME, D_FF_E), BF16),
        compiler_params=_cparams(("arbitrary",)),
        name="experts_up",
    )(*up_tables, xs, w_gate, w_up)
    return pl.pallas_call(
        _experts_down_kernel,
        grid_spec=pltpu.PrefetchScalarGridSpec(
            num_scalar_prefetch=4,
            grid=(NT_E,),
            in_specs=[
                pl.BlockSpec((TME, D_FF_E), lambda i, a, e, f, v: (a[i], 0)),
                pl.BlockSpec((None, D_FF_E, D_MODEL), lambda i, a, e, f, v: (e[i], 0, 0)),
            ],
            out_specs=pl.BlockSpec((TME, D_MODEL), lambda i, a, e, f, v: (i, 0)),
            scratch_shapes=[pltpu.VMEM((D_FF_E, D_MODEL), BF16)],
        ),
        out_shape=jax.ShapeDtypeStruct((NT_E * TME, D_MODEL), F32),
        compiler_params=_cparams(("arbitrary",)),
        name="experts_down",
    )(*down_tables, up, w_down)


def _combine_kernel(x_ref, y1_ref, y2_ref, w1_ref, w2_ref, g2_ref, gam_ref, octx_ref, olat_ref):
    rep = D_MODEL // LANES
    w1 = jnp.concatenate([w1_ref[...]] * rep, axis=1)
    w2 = jnp.concatenate([w2_ref[...]] * rep, axis=1)
    x2 = x_ref[...] + g2_ref[...] * (w1 * y1_ref[...] + w2 * y2_ref[...])
    y = _rms(x2, gam_ref[...])
    is_lat = _is_lat(TMB)

    @pl.when(jnp.logical_not(is_lat))
    def _():
        octx_ref[...] = y

    @pl.when(is_lat)
    def _():
        olat_ref[...] = y


def _combine(x1, y1, y2, w1, w2, mod, gamma):
    return pl.pallas_call(
        _combine_kernel,
        grid=(M_TOK // TMB,),
        in_specs=[_row_spec(D_MODEL, tm=TMB), _row_spec(D_MODEL, tm=TMB), _row_spec(D_MODEL, tm=TMB),
                  _row_spec(LANES, tm=TMB), _row_spec(LANES, tm=TMB), _mod_spec(5, TMB), _const_spec((1, D_MODEL))],
        out_specs=[_ctx_spec(D_MODEL, TMB), _lat_spec(D_MODEL, TMB)],
        out_shape=[jax.ShapeDtypeStruct((N_CTX, D_MODEL), F32), jax.ShapeDtypeStruct((N_LAT, D_MODEL), F32)],
        compiler_params=_cparams(("arbitrary",)),
        name="combine",
    )(x1, y1, y2, w1, w2, mod, gamma)


def _rope_tables(rot_dim):
    n = DEC_SEQ
    rows = np.repeat(np.arange(n // GRID_W, dtype=np.float64), GRID_W)
    cols = np.tile(np.arange(GRID_W, dtype=np.float64), n // GRID_W)
    half = rot_dim // 2
    freqs = ROPE_THETA ** (-np.arange(0, half, 2, dtype=np.float64) / half)
    ang_r, ang_c = rows[:, None] * freqs, cols[:, None] * freqs
    cos = np.concatenate([np.cos(ang_r)] * 2 + [np.cos(ang_c)] * 2, axis=-1)
    sin = np.concatenate([-np.sin(ang_r), np.sin(ang_r), -np.sin(ang_c), np.sin(ang_c)], axis=-1)
    rep = LANES // rot_dim
    return (jnp.asarray(np.tile(cos, (1, rep)).astype(np.float32)),
            jnp.asarray(np.tile(sin, (1, rep)).astype(np.float32)))


def _with_ones(v):
    return jnp.concatenate([v, jnp.ones_like(v)], axis=-1)


def _rows(a, idx):
    return a.at[idx].get(mode="promise_in_bounds")


def _route(idx, counts_f):
    experts = jnp.arange(N_EXPERTS, dtype=jnp.int32)
    e2 = idx[0:2]
    first_total = counts_f[0, :N_EXPERTS].astype(jnp.int32)
    counts = first_total + counts_f[1, :N_EXPERTS].astype(jnp.int32)
    tiles = (counts + TME - 1) // TME
    tile_end = jnp.cumsum(tiles)
    tile_start = tile_end - tiles
    start = jnp.cumsum(counts) - counts

    def lookup(table, keys):
        return sum(jnp.where(keys == k, table[k], 0) for k in range(N_EXPERTS))

    rank2 = jnp.stack([idx[2], idx[3] + lookup(first_total, e2[1])], axis=0)
    pos = lookup(tile_start * TME, e2) + rank2
    tile_id = jnp.arange(NT_E, dtype=jnp.int32)
    tile_expert = jnp.minimum(jnp.sum((tile_id[:, None] >= tile_end[None, :]).astype(jnp.int32), axis=1),
                              N_EXPERTS - 1).astype(jnp.int32)
    tile_valid = (tile_id < tile_end[-1]).astype(jnp.int32)
    t_oh = tile_expert[:, None] == experts[None, :]
    t_first = jnp.sum(jnp.where(t_oh, (start - tile_start * TME)[None, :], 0), axis=1) + tile_id * TME
    t_last = jnp.sum(jnp.where(t_oh, (start + counts)[None, :], 0), axis=1)
    order = jnp.argsort(e2.reshape(-1), stable=True).astype(jnp.int32)
    n_assign = order.shape[0]
    g_idx = t_first[:, None] + jnp.arange(TME, dtype=jnp.int32)[None, :]
    live = jnp.logical_and(g_idx < t_last[:, None], tile_valid[:, None] == 1)
    src = _rows(order, jnp.clip(g_idx, 0, n_assign - 1).reshape(-1)) % M_TOK
    src_tok = jnp.where(live.reshape(-1), src, 0).astype(jnp.int32)

    n_tiles = tile_end[-1]
    last_tile = jnp.maximum(n_tiles - 1, 0)
    last_expert = jnp.sum(jnp.where(tile_id == last_tile, tile_expert, 0))
    t_tstart = jnp.sum(jnp.where(t_oh, tile_start[None, :], 0), axis=1)
    down_tables = (
        jnp.where(tile_valid == 1, tile_id, last_tile).astype(jnp.int32),
        tile_expert,
        jnp.logical_and(tile_valid == 1, tile_id == t_tstart).astype(jnp.int32),
        tile_valid,
    )
    s_id = jnp.arange(NC_E * NT_E, dtype=jnp.int32)
    s_exp = jnp.minimum(jnp.sum((s_id[:, None] >= NC_E * tile_end[None, :]).astype(jnp.int32), axis=1),
                        N_EXPERTS - 1)
    s_oh = s_exp[:, None] == experts[None, :]
    s_tstart = jnp.sum(jnp.where(s_oh, tile_start[None, :], 0), axis=1)
    s_ntile = jnp.maximum(jnp.sum(jnp.where(s_oh, tiles[None, :], 0), axis=1), 1)
    rel = s_id - NC_E * s_tstart
    s_valid = s_id < NC_E * n_tiles
    spare = s_id - NC_E * n_tiles
    up_tables = (
        jnp.where(s_valid, s_tstart + rel % s_ntile, n_tiles + spare // NC_E).astype(jnp.int32),
        jnp.where(s_valid, rel // s_ntile, spare % NC_E).astype(jnp.int32),
        jnp.where(s_valid, s_exp, last_expert).astype(jnp.int32),
        jnp.logical_and(s_valid, rel % s_ntile == 0).astype(jnp.int32),
        s_valid.astype(jnp.int32),
    )
    return src_tok, pos, up_tables, down_tables


def lambda_init(layer):
    return 0.8 - 0.6 * math.exp(-0.3 * layer)


def kernel(x_prompt, x_sample, state_a, cache_b_k, cache_b_v, cache_c_k, cache_c_v, c, c_ctx, w_mod, b_mod, norm_mix, norm_ffn, w_in_even, w_gate2_a, b_gate_a, g_gla, lam_q1, lam_k1, lam_q2, lam_k2, g_sub_b, w_out_even, w_in_odd, g_q_c, g_k_c, w_out_odd, ffn_gate, ffn_up, ffn_down, w_router, exp_gate, exp_up, exp_down, norm_final):
    x_ctx, x_lat = x_prompt.reshape(N_CTX, D_MODEL), x_sample.reshape(N_LAT, D_MODEL)
    cond =jnp.concatenate([c_ctx[None, :], c, jnp.zeros((N_COND - 1 - DEC_BATCH, D_MODEL), F32)], axis=0)
    mod = _modulation(cond, w_mod, b_mod).reshape(DEPTH, N_COND, 1, 6 * D_MODEL)

    w = w_in_even[0]
    na = 2 * A_QK + 2 * A_V
    gate_lo = na
    gate_hi = na + 2 * GATE_RANK
    col_scale = jnp.concatenate([jnp.full((A_QK,), DK_A ** -0.5, F32), jnp.ones((na - A_QK,), F32),
                                 jnp.full((B_QK,), DH_B ** -0.5, F32), jnp.ones((B_QK + B_V,), F32)])
    w_main = (jnp.concatenate([w[:, :gate_lo], w[:, gate_hi:]], axis=1) * col_scale).astype(BF16)
    w_gates = jnp.pad(w[:, gate_lo:gate_hi], ((0, 0), (0, LANES - 2 * GATE_RANK))).astype(BF16)
    g2 = jnp.zeros((LANES, 2 * A_QK), F32)
    g2 = g2.at[:GATE_RANK, :A_QK].set(w_gate2_a[0, 0]).at[GATE_RANK:2 * GATE_RANK, A_QK:].set(w_gate2_a[0, 1])
    bg = b_gate_a[0].reshape(1, 2 * A_QK)
    cos_b, sin_b = _rope_tables(DH_B)
    a_proj, glog, qb, kb, vb, kb_ctx, vb_ctx = _inproj_even(
        x_ctx, x_lat, norm_mix[0][None, :], mod[0], w_main, w_gates, g2.astype(BF16), bg, cos_b, sin_b)

    s0_all = jnp.concatenate([jnp.zeros((1, 2, A_QK, DV_A), F32),
                              state_a[:, 0].reshape(DEC_BATCH, 2, A_QK, DV_A)], axis=0)
    sel = np.arange(A_QK)[:, None] // DK_A == np.arange(LANES)[None, :] // GLA_SB
    m1 = jnp.asarray(sel.astype(np.float32)).astype(BF16)
    o_f, o_b, s_fin = _gla(a_proj, glog, s0_all, m1)

    k_cache = cache_b_k[:, 0].reshape(DEC_BATCH, PAST_LEN, B_QK).astype(BF16)
    v_cache = _with_ones(cache_b_v[:, 0].astype(BF16)).reshape(DEC_BATCH, PAST_LEN, 2 * B_V)
    lam_params = [p[0][None, :] for p in (lam_q1, lam_k1, lam_q2, lam_k2)]
    attn_b = _diffattn(qb, kb, vb, k_cache, v_cache, lam_params, g_sub_b[0][None, :], lambda_init(0))

    x = _mix_ffn_even(o_f, o_b, a_proj, *attn_b, x_ctx, x_lat, mod[0], g_gla[0][None, :],
                      w_out_even[0].astype(BF16), norm_ffn[0][None, :], ffn_gate[0].astype(BF16),
                      ffn_up[0].astype(BF16), ffn_down[0].astype(BF16))

    cos_c, sin_c = _rope_tables(DH_C)
    q_c, k_c, v_c, kc_ctx, vc_ctx = _inproj_odd(x, norm_mix[1][None, :], mod[1], w_in_odd[0].astype(BF16),
                                     g_q_c[0][None, :], g_k_c[0][None, :], cos_c, sin_c)
    nkv = HKV_C * DH_C
    k_cache = cache_c_k[:, 0].reshape(DEC_BATCH, PAST_LEN, nkv).astype(BF16)
    v_cache = _with_ones(cache_c_v[:, 0].astype(BF16)).reshape(DEC_BATCH, PAST_LEN, 2 * nkv)
    attn_c = _gqa(q_c, k_c, v_c, k_cache, v_cache)
    w_r = jnp.pad(w_router[0], ((0, 0), (0, LANES - N_EXPERTS)))
    x1, h_moe, w1, w2, idx, counts = _postmix_odd(*attn_c, x, mod[1], w_out_odd[0].astype(BF16),
                                                  norm_ffn[1][None, :], w_r)

    src_tok, pos, up_tables, down_tables = _route(idx, counts)
    xs = _rows(h_moe, src_tok)
    ys = _experts(up_tables, down_tables, xs, exp_gate[0], exp_up[0], exp_down[0])
    y1 = _rows(ys, pos[0])
    y2 = _rows(ys, pos[1])
    y_ctx, y_lat = _combine(x1, y1, y2, w1, w2, mod[1], norm_final[None, :])

    y_prompt = y_ctx.reshape(BATCH, SEQ, D_MODEL)
    y_sample = y_lat.reshape(DEC_BATCH, DEC_SEQ, D_MODEL)
    new_state_a = s_fin.reshape(BATCH, 1, 2, H_A, DK_A, DV_A)
    new_b_k = jnp.transpose(kb_ctx.reshape(BATCH, 1, H_B, 2, DH_B, SEQ), (0, 1, 5, 2, 3, 4))
    new_b_v = vb_ctx.reshape(BATCH, 1, SEQ, H_B, DV_B)
    new_c_k = kc_ctx.reshape(BATCH, 1, SEQ, HKV_C, DH_C)
    new_c_v = vc_ctx.reshape(BATCH, 1, SEQ, HKV_C, DH_C)
    return (y_prompt, y_sample, new_state_a, new_b_k, new_b_v, new_c_k, new_c_v)
```

```python
import functools
import math

import numpy as np
import jax
import jax.numpy as jnp
from jax import lax
from jax.experimental import pallas as pl
from jax.experimental.pallas import tpu as pltpu

D_MODEL = 1024
BATCH = 32
SEQ = 256
DEPTH = 2
DEC_BATCH = 2
DEC_SEQ = 2048
PAST_LEN = 512
GRID_W = 64
H_A, DK_A, DV_A = 4, 64, 128
GATE_RANK = 16
GATE_TAU = 16.0
H_B, DH_B = 4, 64
DV_B = 2 * DH_B
H_C, HKV_C, DH_C = 8, 2, 128
D_FF = 2816
N_EXPERTS = 8
D_FF_E = 3584
ROPE_THETA = 10000.0
EPS = 1e-6
A_QK = H_A * DK_A
A_V = H_A * DV_A
B_QK = H_B * 2 * DH_B
B_V = H_B * DV_B

F32 = jnp.float32
BF16 = jnp.bfloat16
LOG2E = math.log2(math.e)

V7X_VMEM_BYTES = 64 * 1024 * 1024
VMEM_LIMIT = V7X_VMEM_BYTES * 7 // 8
LANES = 128
MXU_N = 256
TM = 256
N_CTX = BATCH * SEQ
N_LAT = DEC_BATCH * DEC_SEQ
M_TOK = N_CTX + N_LAT
NB = M_TOK // TM
NCB = N_CTX // TM
LBB = DEC_SEQ // TM
N_COND = 8
GLA_C = 64
GLA_SB = 8
TME = 512
NT_E = 2 * M_TOK // TME + N_EXPERTS
NC_E = 2
TF_E = D_FF_E // NC_E
TQ_LAT = 256
TMC = 2 * SEQ
TMB = 512

assert SEQ == TM and DEC_SEQ % TM == 0 and DEPTH == 2


def _cparams(sem):
    return pltpu.CompilerParams(dimension_semantics=sem, vmem_limit_bytes=VMEM_LIMIT)


def _group(i, tm=TM):
    return jnp.where(i < N_CTX // tm, 0, 1 + (i - N_CTX // tm) // (DEC_SEQ // tm))


def _pos_block(i, tm=TM):
    return jnp.where(i < N_CTX // tm, 0, (i - N_CTX // tm) % (DEC_SEQ // tm))


def _mod_spec(chunk, tm=TM):
    return pl.BlockSpec((None, 1, D_MODEL), lambda i, c=chunk: (_group(i, tm), 0, c))


def _row_spec(width, col=0, tm=TM):
    return pl.BlockSpec((tm, width), lambda i, c=col: (i, c))


def _ctx_spec(width, tm=TM):
    return pl.BlockSpec((tm, width), lambda i: (jnp.minimum(i, N_CTX // tm - 1), 0))


def _ctx_parts_spec(parts, tm=TM):
    return pl.BlockSpec((tm * parts, LANES), lambda i: (jnp.minimum(i, N_CTX // tm - 1), 0))


def _lat_spec(width, tm=TM):
    return pl.BlockSpec((tm, width), lambda i: (jnp.maximum(i - N_CTX // tm, 0), 0))


def _const_spec(shape):
    nd = len(shape)
    return pl.BlockSpec(shape, lambda i, nd=nd: (0,) * nd, pipeline_mode=pl.Buffered(1))


def _dot(a, b):
    return jnp.dot(a, b, preferred_element_type=F32)


def _dot_nt(a, b):
    return lax.dot_general(a, b, (((1,), (1,)), ((), ())), preferred_element_type=F32)


def _split2(a):
    hi = a.astype(BF16)
    lo = (a - hi.astype(F32)).astype(BF16)
    return hi, lo


def _dot_hi(a, b):
    a_hi, a_lo = _split2(a)
    b_hi, b_lo = _split2(b)
    return _dot(a_hi, b_hi) + (_dot(a_hi, b_lo) + _dot(a_lo, b_hi))


def _silu(x):
    return (0.5 * x) * (1.0 + jnp.tanh(0.5 * x))


def _log_sigmoid(x):
    return jnp.minimum(x, 0.0) - jnp.log(1.0 + jnp.exp(-jnp.abs(x)))


def _rms(x, g):
    return x * lax.rsqrt(jnp.mean(x * x, axis=-1, keepdims=True) + EPS) * g


def _norm_mod(x, gamma, sc, sh):
    return _rms(x, gamma) * (1.0 + sc) + sh


def _staged(parts):
    live = list(parts)
    while live:
        live = [g for g in live if next(g, True) is None]


def _copy_ctx_transposed(stage3_ref, dst_ref):
    @pl.when(pl.program_id(0) < NCB)
    def _():
        def body(j, carry):
            dst_ref[pl.ds(pl.multiple_of(j * LANES, LANES), LANES), :] = jnp.transpose(stage3_ref[j])
            return carry

        lax.fori_loop(0, stage3_ref.shape[0], body, 0)


def _is_lat(tm):
    return pl.program_id(0) >= N_CTX // tm


def _pick_rows(ctx_ref, lat_ref):
    return jnp.where(_is_lat(ctx_ref.shape[0]), lat_ref[...], ctx_ref[...])


def _copy_ctx(stage_ref, outs):
    piece = 64

    @pl.when(jnp.logical_not(_is_lat(stage_ref.shape[0])))
    def _():
        def body(r, carry):
            rows = pl.ds(pl.multiple_of(r * piece, piece), piece)
            for lo, width, parts, dst_ref in outs:
                for p in range(parts):
                    val = stage_ref[rows, lo + p * width:lo + (p + 1) * width]
                    if width < LANES:
                        val = jnp.concatenate([val, jnp.zeros((piece, LANES - width), F32)], axis=1)
                    dst_ref[pl.ds(pl.multiple_of(r * piece * parts, piece * parts) + p, piece, stride=parts), :] = val
            return carry

        lax.fori_loop(0, stage_ref.shape[0] // piece, body, 0)


def _rope(x, cos, sin, half):
    lane = lax.broadcasted_iota(jnp.int32, x.shape, 1)
    first = (lane % (2 * half)) < half
    swapped = jnp.where(first, pltpu.roll(x, LANES - half, 1), pltpu.roll(x, half, 1))
    return x * cos + swapped * sin


def _mod_kernel(cond_ref, w_ref, b_ref, o_ref):
    o_ref[...] = _dot_hi(_silu(cond_ref[...]), w_ref[...]) + b_ref[...]


def _modulation(cond, w_mod, b_mod):
    tn = 1536
    return pl.pallas_call(
        _mod_kernel,
        grid=(DEPTH, 6 * D_MODEL // tn),
        in_specs=[
            pl.BlockSpec((N_COND, D_MODEL), lambda l, j: (0, 0)),
            pl.BlockSpec((None, D_MODEL, tn), lambda l, j: (l, 0, j)),
            pl.BlockSpec((None, 1, tn), lambda l, j: (l, 0, j)),
        ],
        out_specs=pl.BlockSpec((None, N_COND, tn), lambda l, j: (l, 0, j)),
        out_shape=jax.ShapeDtypeStruct((DEPTH, N_COND, 6 * D_MODEL), F32),
        compiler_params=_cparams(("arbitrary", "arbitrary")),
        name="modulation",
    )(cond, w_mod, b_mod.reshape(DEPTH, 1, 6 * D_MODEL))


def _inproj_even_kernel(xc_ref, xl_ref, gam_ref, sh_ref, sc_ref, w_ref, wg_ref, g2_ref, bg_ref, cos_ref, sin_ref,
                        a_ref, glog_ref, qb_ref, kb_ref, vb_ref, kctx_ref, vctx_ref, stage_ref, stagek_ref):
    is_lat = pl.program_id(0) >= NCB
    na = 2 * A_QK + 2 * A_V

    def part(rows):
        n = rows.stop - rows.start
        x = jnp.where(is_lat, xl_ref[rows, :], xc_ref[rows, :])
        hb = _norm_mod(x, gam_ref[...], sc_ref[...], sh_ref[...]).astype(BF16)
        cos = jnp.where(is_lat, cos_ref[rows, :], 1.0)
        sin = jnp.where(is_lat, sin_ref[rows, :], 0.0)
        yield
        for lo in range(0, na, MXU_N):
            a_ref[rows, lo:lo + MXU_N] = _dot(hb, w_ref[:, lo:lo + MXU_N])
        for lo in range(0, B_QK, MXU_N):
            zq = _dot(hb, w_ref[:, na + lo:na + lo + MXU_N])
            zk = _dot(hb, w_ref[:, na + B_QK + lo:na + B_QK + lo + MXU_N])
            zv = _dot(hb, w_ref[:, na + 2 * B_QK + lo:na + 2 * B_QK + lo + MXU_N])
            stage_ref[rows, lo:lo + MXU_N] = zv
            for j in range(0, MXU_N, LANES):
                cs = slice(lo + j, lo + j + LANES)
                vb_ref[rows, 2 * (lo + j):2 * (lo + j) + DV_B] = zv[:, j:j + LANES].astype(BF16)
                vb_ref[rows, 2 * (lo + j) + DV_B:2 * (lo + j + LANES)] = jnp.ones((n, DV_B), BF16)
                stagek_ref[(lo + j) // LANES, rows, :] = zk[:, j:j + LANES]
                qb_ref[rows, cs] = (_rope(zq[:, j:j + LANES], cos, sin, DH_B // 4) * LOG2E).astype(BF16)
                kb_ref[rows, cs] = _rope(zk[:, j:j + LANES], cos, sin, DH_B // 4).astype(BF16)
        gates = _dot(hb, wg_ref[...])
        xg = _dot(gates.astype(BF16), g2_ref[...]) + bg_ref[...]
        glog_ref[rows, :] = _log_sigmoid(xg) * (1.0 / GATE_TAU)

    _staged([part(slice(0, TM))])
    _copy_ctx(stage_ref, ((0, DV_B, H_B, vctx_ref),))
    _copy_ctx_transposed(stagek_ref, kctx_ref)


def _inproj_even(x_ctx, x_lat, gamma, mod, w_main, w_gates, g2, bg, cos, sin):
    na = 2 * A_QK + 2 * A_V
    nz = na + 2 * B_QK + B_V
    return pl.pallas_call(
        _inproj_even_kernel,
        grid=(NB,),
        in_specs=[
            _ctx_spec(D_MODEL), _lat_spec(D_MODEL), _const_spec((1, D_MODEL)), _mod_spec(0), _mod_spec(1),
            _const_spec((D_MODEL, nz)), _const_spec((D_MODEL, LANES)), _const_spec((LANES, 2 * A_QK)),
            _const_spec((1, 2 * A_QK)),
            pl.BlockSpec((TM, LANES), lambda i: (_pos_block(i), 0)),
            pl.BlockSpec((TM, LANES), lambda i: (_pos_block(i), 0)),
        ],
        out_specs=[_pair_spec(na), _pair_spec(2 * A_QK), _row_spec(B_QK), _row_spec(B_QK), _row_spec(2 * B_V),
                   pl.BlockSpec((None, B_QK, TM), lambda i: (jnp.minimum(i, NCB - 1), 0, 0)), _ctx_parts_spec(H_B)],
        out_shape=[
            jax.ShapeDtypeStruct((M_TOK, na), F32), jax.ShapeDtypeStruct((M_TOK, 2 * A_QK), F32),
            jax.ShapeDtypeStruct((M_TOK, B_QK), BF16), jax.ShapeDtypeStruct((M_TOK, B_QK), BF16),
            jax.ShapeDtypeStruct((M_TOK, 2 * B_V), BF16), jax.ShapeDtypeStruct((BATCH, B_QK, SEQ), F32),
            jax.ShapeDtypeStruct((N_CTX * H_B, LANES), F32),
        ],
        scratch_shapes=[pltpu.VMEM((TM, B_V), F32), pltpu.VMEM((B_QK // LANES, TM, LANES), F32)],
        compiler_params=_cparams(("arbitrary",)),
        name="inproj_even",
    )(x_ctx, x_lat, gamma, mod, mod, w_main, w_gates, g2, bg, cos, sin)


def _gla_chunk(q_ref, k_ref, v_ref, g_ref, o_ref, c, s_ref, a_ref, b_ref, m1_ref, head_mask, same_block, rev):
    q, k, v = q_ref[c], k_ref[c], v_ref[c]
    g = g_ref[c] * LOG2E
    v_b = v.astype(BF16)
    r_i = lax.broadcasted_iota(jnp.int32, (GLA_C, GLA_C), 0)
    c_i = lax.broadcasted_iota(jnp.int32, (GLA_C, GLA_C), 1)
    tri = jnp.where((c_i >= r_i) if rev else (c_i <= r_i), 1.0, 0.0).astype(BF16)
    g_hi = g.astype(BF16)
    g_r = g - g_hi.astype(F32)
    g_mid = g_r.astype(BF16)
    g_lo = (g_r - g_mid.astype(F32)).astype(BF16)
    bc = _dot(tri, g_hi) + (_dot(tri, g_mid) + _dot(tri, g_lo))
    b_ref[...] = bc
    yield
    t_i = lax.broadcasted_iota(jnp.int32, (GLA_SB, A_QK), 0)
    group = 2 * GLA_SB
    for i0 in range(0, GLA_C, group):
        blocks = range(i0, i0 + group, GLA_SB)
        q_i = [q_ref[c, i:i + GLA_SB, :] for i in blocks]
        b_i = [b_ref[i:i + GLA_SB, :] for i in blocks]
        for j in range(GLA_SB):
            keep = (t_i <= j) if rev else (t_i >= j)
            pair = []
            for n, i in enumerate(blocks):
                k_s = k_ref[c, i + j:i + j + 1, :]
                b_s = b_ref[i + j:i + j + 1, :]
                pair.append(jnp.where(keep, q_i[n] * k_s * jnp.exp2(b_i[n] - b_s), 0.0))
            a_ref[j * GLA_C + i0:j * GLA_C + i0 + group, :] = jnp.concatenate(pair, axis=0).astype(BF16)
    yield
    last = bc[0:1] if rev else bc[GLA_C - 1:GLA_C]
    qc = q * jnp.exp2(bc)
    kl = k * jnp.exp2(last - bc)
    row = lax.broadcasted_iota(jnp.int32, (GLA_C, A_QK), 0)

    def expand(x):
        return jnp.where(head_mask, jnp.concatenate([x] * H_A, axis=0), 0.0).astype(BF16)

    scores = None
    half = GLA_C // 2
    while half >= GLA_SB:
        mids = [bc[b + half:b + half + 1] if rev else bc[b + half - 1:b + half] for b in range(0, GLA_C, 2 * half)]
        ref = mids[0] if len(mids) == 1 else jnp.concatenate(
            [jnp.broadcast_to(m, (2 * half, A_QK)) for m in mids], axis=0)
        upper = (row // half) % 2 == 1
        later, earlier = (jnp.logical_not(upper), upper) if rev else (upper, jnp.logical_not(upper))
        q_l = jnp.where(later, q * jnp.exp2(bc - ref), 0.0)
        k_e = jnp.where(earlier, k * jnp.exp2(ref - bc), 0.0).astype(BF16)
        term = _dot_nt(expand(q_l), k_e)
        if 2 * half < GLA_C:
            term = jnp.where(same_block[2 * half], term, 0.0)
        scores = term if scores is None else scores + term
        half //= 2
    s_old = s_ref[...]
    inter = _dot(expand(qc), s_old.astype(BF16))
    sums = _dot(a_ref[...], m1_ref[...])
    yield
    lane_c = lax.broadcasted_iota(jnp.int32, (GLA_C, LANES), 1)
    within = None
    for j in range(GLA_SB):
        part = jnp.where(lane_c % GLA_SB == j, sums[j * GLA_C:(j + 1) * GLA_C], 0.0)
        within = part if within is None else within + part
    lane = lax.broadcasted_iota(jnp.int32, (GLA_SB, LANES), 1)
    per_head = []
    for h in range(H_A):
        rows_h = []
        for i in range(0, GLA_C, GLA_SB):
            moved = pltpu.roll(within[i:i + GLA_SB], (i - h * GLA_SB) % LANES, 1)
            rows_h.append(jnp.where(lane // GLA_SB == i // GLA_SB, moved, 0.0))
        per_head.append(jnp.concatenate(rows_h, axis=0)[:, :GLA_C])
    scores = scores + jnp.concatenate(per_head, axis=0)
    o_heads = []
    for h in range(H_A):
        hs = slice(h * GLA_C, (h + 1) * GLA_C)
        o_heads.append(inter[hs] + _dot(scores[hs].astype(BF16), v_b[:, h * DV_A:(h + 1) * DV_A]))
    o_ref[c] = jnp.concatenate(o_heads, axis=1)
    yield
    t = jnp.transpose(jnp.concatenate([kl, jnp.broadcast_to(last, (GLA_C, A_QK))], axis=0))
    kv = _dot(t[:, :GLA_C].astype(BF16), v_b)
    a_col = jnp.exp2(t[:, GLA_C:GLA_C + 1])
    for h in range(H_A):
        hs = slice(h * DK_A, (h + 1) * DK_A)
        s_ref[hs, :] = a_col[hs] * s_old[hs] + kv[hs, h * DV_A:(h + 1) * DV_A]


def _gla_kernel(fblk_ref, bblk_ref, first_ref, inita_ref, initb_ref,
                qf_ref, kf_ref, vf_ref, gf_ref, qr_ref, kr_ref, vr_ref, gr_ref, s0a_ref, s0b_ref, m1_ref,
                of_ref, ob_ref, sfin_ref, s_refs, a_refs, b_refs):
    step = pl.program_id(0)

    @pl.when(first_ref[step] == 1)
    def _():
        for seq, s0_ref in enumerate((s0a_ref, s0b_ref)):
            for d in range(2):
                s_refs[2 * seq + d] = s0_ref[d]

    r_h = lax.broadcasted_iota(jnp.int32, (H_A * GLA_C, A_QK), 0) // GLA_C
    c_h = lax.broadcasted_iota(jnp.int32, (H_A * GLA_C, A_QK), 1) // DK_A
    head_mask = r_h == c_h
    t_s = lax.broadcasted_iota(jnp.int32, (H_A * GLA_C, GLA_C), 0) % GLA_C
    s_s = lax.broadcasted_iota(jnp.int32, (H_A * GLA_C, GLA_C), 1)
    same_block = {}
    size = 2 * GLA_SB
    while size < GLA_C:
        same_block[size] = (t_s // size) == (s_s // size)
        size *= 2
    nch = TM // GLA_C

    def body(c, carry):
        chains = []
        for seq in range(2):
            chains.append(_gla_chunk(qf_ref, kf_ref, vf_ref, gf_ref, of_ref, seq * nch + c, s_refs.at[2 * seq],
                                     a_refs.at[2 * seq], b_refs.at[2 * seq], m1_ref, head_mask, same_block, False))
            chains.append(_gla_chunk(qr_ref, kr_ref, vr_ref, gr_ref, ob_ref, seq * nch + nch - 1 - c,
                                     s_refs.at[2 * seq + 1], a_refs.at[2 * seq + 1], b_refs.at[2 * seq + 1], m1_ref,
                                     head_mask, same_block, True))
        _staged(chains)
        return carry

    lax.fori_loop(0, nch, body, 0)

    @pl.when(step < BATCH // 2)
    def _():
        for seq in range(2):
            for d in range(2):
                sfin_ref[seq, d] = s_refs[2 * seq + d]


def _pair_block(i):
    j = i - NCB
    return jnp.where(i < NCB, i, NCB + 2 * (j % LBB) + j // LBB)


def _pair_spec(width, col=0):
    return pl.BlockSpec((TM, width), lambda i, c=col: (_pair_block(i), c))


def _gla_tables():
    fblk, bblk, first, init_a, init_b = [], [], [], [], []
    for p in range(BATCH // 2):
        fblk.append(p), bblk.append(p), first.append(1), init_a.append(0), init_b.append(0)
    for j in range(LBB):
        fblk.append(NCB // 2 + j)
        bblk.append(NCB // 2 + LBB - 1 - j)
        first.append(1 if j == 0 else 0)
        init_a.append(1), init_b.append(2)
    return [jnp.asarray(np.array(t, np.int32)) for t in (fblk, bblk, first, init_a, init_b)]


def _gla(a_proj, glog, s0_all, m1):
    tables = _gla_tables()
    nsteps = int(tables[0].shape[0])
    nseq = BATCH + DEC_BATCH

    def fmap(col):
        return lambda i, fb, bb, fi, ia, ib: (fb[i], 0, col)

    def rmap(col):
        return lambda i, fb, bb, fi, ia, ib: (bb[i], 0, col)

    nch = 2 * TM // GLA_C
    a3 = a_proj.reshape(M_TOK // GLA_C, GLA_C, a_proj.shape[1])
    g3 = glog.reshape(M_TOK // GLA_C, GLA_C, glog.shape[1])
    gs = pltpu.PrefetchScalarGridSpec(
        num_scalar_prefetch=5,
        grid=(nsteps,),
        in_specs=[
            pl.BlockSpec((nch, GLA_C, A_QK), fmap(0)), pl.BlockSpec((nch, GLA_C, A_QK), fmap(1)),
            pl.BlockSpec((nch, GLA_C, A_V), fmap(1)), pl.BlockSpec((nch, GLA_C, A_QK), fmap(0)),
            pl.BlockSpec((nch, GLA_C, A_QK), rmap(0)), pl.BlockSpec((nch, GLA_C, A_QK), rmap(1)),
            pl.BlockSpec((nch, GLA_C, A_V), rmap(1)), pl.BlockSpec((nch, GLA_C, A_QK), rmap(1)),
            pl.BlockSpec((None, 2, A_QK, DV_A), lambda i, fb, bb, fi, ia, ib: (ia[i], 0, 0, 0)),
            pl.BlockSpec((None, 2, A_QK, DV_A), lambda i, fb, bb, fi, ia, ib: (ib[i], 0, 0, 0)),
            pl.BlockSpec((A_QK, LANES), lambda i, fb, bb, fi, ia, ib: (0, 0)),
        ],
        out_specs=[
            pl.BlockSpec((nch, GLA_C, A_V), fmap(0)), pl.BlockSpec((nch, GLA_C, A_V), rmap(0)),
            pl.BlockSpec((2, 2, A_QK, DV_A),
                         lambda i, fb, bb, fi, ia, ib: (jnp.minimum(fb[i], BATCH // 2 - 1), 0, 0, 0)),
        ],
        scratch_shapes=[
            pltpu.VMEM((4, A_QK, DV_A), F32),
            pltpu.VMEM((4, GLA_SB * GLA_C, A_QK), BF16),
            pltpu.VMEM((4, GLA_C, A_QK), F32),
        ],
    )
    o_f, o_b, s_fin = pl.pallas_call(
        _gla_kernel,
        grid_spec=gs,
        out_shape=[
            jax.ShapeDtypeStruct((M_TOK // GLA_C, GLA_C, A_V), F32),
            jax.ShapeDtypeStruct((M_TOK // GLA_C, GLA_C, A_V), F32),
            jax.ShapeDtypeStruct((BATCH, 2, A_QK, DV_A), F32),
        ],
        compiler_params=_cparams(("arbitrary",)),
        name="gla",
    )(*tables, a3, a3, a3, g3, a3, a3, a3, g3, s0_all, s0_all, m1)
    return o_f.reshape(M_TOK, A_V), o_b.reshape(M_TOK, A_V), s_fin


def _attend(q_b, kv, out):
    scores = [_dot_nt(q_b, k_b) for k_b, _ in kv]
    yield
    top = functools.reduce(jnp.maximum, [jnp.max(s2, axis=-1, keepdims=True) for s2 in scores])
    both = None
    for s2, (_, v_ones) in zip(scores, kv):
        part = _dot(jnp.exp2(s2 - top).astype(BF16), v_ones)
        both = part if both is None else both + part
    yield
    dv = both.shape[1] // 2
    out.append(both[:, :dv] * (1.0 / both[:, dv:]))


def _seq_rows(ref):
    return [slice(r, r + SEQ) for r in range(0, ref.shape[0], SEQ)]


def _diffattn_body(q_ref, kv_refs, lq1_ref, lk1_ref, lq2_ref, lk2_ref, gsub_ref, o_ref, lam_init, seqs):
    lam = (jnp.exp(jnp.sum(lq1_ref[...] * lk1_ref[...], axis=-1, keepdims=True))
           - jnp.exp(jnp.sum(lq2_ref[...] * lk2_ref[...], axis=-1, keepdims=True)) + lam_init)
    outs, parts = [], []
    for rows in seqs:
        lane = lax.broadcasted_iota(jnp.int32, q_ref[rows, :LANES].shape, 1)
        for h in range(H_B):
            hs = slice(h * LANES, (h + 1) * LANES)
            q_h = q_ref[rows, hs]
            kv = [(k_ref[rows if n == 0 else slice(None), hs],
                   v_ref[rows if n == 0 else slice(None), 2 * h * DV_B:2 * (h + 1) * DV_B])
                  for n, (k_ref, v_ref) in enumerate(kv_refs)]
            zero = jnp.zeros_like(q_h)
            outs.append((rows, h, [], []))
            parts.append(_attend(jnp.where(lane < DH_B, q_h, zero), kv, outs[-1][2]))
            parts.append(_attend(jnp.where(lane >= DH_B, q_h, zero), kv, outs[-1][3]))
    _staged(parts)
    for rows, h, o1, o2 in outs:
        o = o1[0] - lam * o2[0]
        o_ref[rows, h * LANES:(h + 1) * LANES] = (_rms(o, gsub_ref[...]) * (1.0 - lam_init)).astype(BF16)


def _diffattn_kernel(q_ref, k_ref, v_ref, lq1, lk1, lq2, lk2, gsub, o_ref, *, lam_init):
    _diffattn_body(q_ref, [(k_ref, v_ref)], lq1, lk1, lq2, lk2, gsub, o_ref, lam_init, _seq_rows(q_ref))


def _diffattn_cached_kernel(q_ref, k_ref, v_ref, kc_ref, vc_ref, lq1, lk1, lq2, lk2, gsub, o_ref, *, lam_init):
    _diffattn_body(q_ref, [(k_ref, v_ref), (kc_ref, vc_ref)], lq1, lk1, lq2, lk2, gsub, o_ref, lam_init,
                   [slice(None)])


def _diffattn(qb, kb, vb, k_cache, v_cache, lam_params, gsub, lam_init):
    small = [pl.BlockSpec((1, DH_B), lambda *_: (0, 0))] * 4 + [pl.BlockSpec((1, DV_B), lambda *_: (0, 0))]
    body = functools.partial(_diffattn_kernel, lam_init=lam_init)
    ctx = pl.pallas_call(
        body,
        grid=(N_CTX // TMC,),
        in_specs=[_row_spec(B_QK, tm=TMC), _row_spec(B_QK, tm=TMC), _row_spec(2 * B_V, tm=TMC)] + small,
        out_specs=_row_spec(B_V, tm=TMC),
        out_shape=jax.ShapeDtypeStruct((N_CTX, B_V), BF16),
        compiler_params=_cparams(("arbitrary",)),
        name="diffattn_ctx",
    )(qb, kb, vb, *lam_params, gsub)
    nqb = DEC_SEQ // TQ_LAT
    off = N_CTX // TQ_LAT
    seq0 = N_CTX // DEC_SEQ
    lat = pl.pallas_call(
        functools.partial(_diffattn_cached_kernel, lam_init=lam_init),
        grid=(DEC_BATCH, nqb),
        in_specs=[
            pl.BlockSpec((TQ_LAT, B_QK), lambda b, j: (off + b * nqb + j, 0)),
            pl.BlockSpec((DEC_SEQ, B_QK), lambda b, j: (seq0 + b, 0)),
            pl.BlockSpec((DEC_SEQ, 2 * B_V), lambda b, j: (seq0 + b, 0)),
            pl.BlockSpec((None, PAST_LEN, B_QK), lambda b, j: (b, 0, 0)),
            pl.BlockSpec((None, PAST_LEN, 2 * B_V), lambda b, j: (b, 0, 0)),
        ] + small,
        out_specs=pl.BlockSpec((TQ_LAT, B_V), lambda b, j: (b * nqb + j, 0)),
        out_shape=jax.ShapeDtypeStruct((N_LAT, B_V), BF16),
        compiler_params=_cparams(("arbitrary", "arbitrary")),
        name="diffattn_lat",
    )(qb, kb, vb, k_cache, v_cache, *lam_params, gsub)
    return ctx, lat


def _swiglu_act(xb, wg_ref, wu_ref, act_ref):
    for lo in range(0, act_ref.shape[1], MXU_N):
        cs = slice(lo, lo + MXU_N)
        act_ref[:, cs] = (_silu(_dot(xb, wg_ref[:, cs])) * _dot(xb, wu_ref[:, cs])).astype(BF16)


def _mix_ffn_even_kernel(of_ref, ob_ref, ra_ref, actx_ref, alat_ref, xc_ref, xl_ref, g1_ref, ggla_ref, wo_ref,
                         gam_ref, sh_ref, sc_ref, g2_ref, wg_ref, wu_ref, wd_ref, o_ref, act_ref):
    is_lat = _is_lat(TM)

    def part(rows):
        heads = []
        for h in range(H_A):
            hs = slice(h * DV_A, (h + 1) * DV_A)
            heads.append((_rms(of_ref[rows, hs] + ob_ref[rows, hs], ggla_ref[...])
                          * _silu(ra_ref[rows, hs])).astype(BF16))
        mix = jnp.concatenate(heads + [jnp.where(is_lat, alat_ref[rows, :], actx_ref[rows, :])], axis=1)
        yield
        x = jnp.where(is_lat, xl_ref[rows, :], xc_ref[rows, :]) + g1_ref[...] * _dot(mix, wo_ref[...])
        hb = _norm_mod(x, gam_ref[...], sc_ref[...], sh_ref[...]).astype(BF16)
        yield
        for lo in range(0, D_FF, MXU_N):
            cs = slice(lo, lo + MXU_N)
            act_ref[rows, cs] = (_silu(_dot(hb, wg_ref[:, cs])) * _dot(hb, wu_ref[:, cs])).astype(BF16)
        yield
        o_ref[rows, :] = x + g2_ref[...] * _dot(act_ref[rows, :], wd_ref[...])

    _staged([part(slice(r, r + TM // 2)) for r in range(0, TM, TM // 2)])


def _mix_ffn_even(o_f, o_b, a_proj, attn_ctx, attn_lat, x_ctx, x_lat, mod, g_gla, w_out, gamma, w_gate, w_up, w_down):
    return pl.pallas_call(
        _mix_ffn_even_kernel,
        grid=(NB,),
        in_specs=[_pair_spec(A_V), _pair_spec(A_V), _pair_spec(A_V, 2), _ctx_spec(B_V), _lat_spec(B_V),
                  _ctx_spec(D_MODEL), _lat_spec(D_MODEL), _mod_spec(2), _const_spec((1, DV_A)),
                  _const_spec((A_V + B_V, D_MODEL)),
                  _const_spec((1, D_MODEL)), _mod_spec(3), _mod_spec(4), _mod_spec(5),
                  _const_spec((D_MODEL, D_FF)), _const_spec((D_MODEL, D_FF)), _const_spec((D_FF, D_MODEL))],
        out_specs=_row_spec(D_MODEL),
        out_shape=jax.ShapeDtypeStruct((M_TOK, D_MODEL), F32),
        scratch_shapes=[pltpu.VMEM((TM, D_FF), BF16)],
        compiler_params=_cparams(("arbitrary",)),
        name="mix_ffn_even",
    )(o_f, o_b, a_proj, attn_ctx, attn_lat, x_ctx, x_lat, mod, g_gla, w_out, gamma, mod, mod, mod,
      w_gate, w_up, w_down)


def _inproj_odd_kernel(x_ref, gam_ref, sh_ref, sc_ref, w_ref, gq_ref, gk_ref, cos_ref, sin_ref,
                       q_ref, k_ref, v_ref, kctx_ref, vctx_ref, stage_ref):
    is_lat = _is_lat(TMB)
    scale = DH_C ** -0.5 * LOG2E
    nq, nkv = H_C * DH_C, HKV_C * DH_C

    def part(rows):
        n = rows.stop - rows.start
        hb = _norm_mod(x_ref[rows, :], gam_ref[...], sc_ref[...], sh_ref[...]).astype(BF16)
        cos = jnp.where(is_lat, cos_ref[rows, :], 1.0)
        sin = jnp.where(is_lat, sin_ref[rows, :], 0.0)
        yield
        z = _dot(hb, w_ref[:, nq:nq + nkv])
        for h in range(HKV_C):
            hs = slice(h * DH_C, (h + 1) * DH_C)
            k_n = _rms(z[:, hs], gk_ref[...])
            stage_ref[rows, hs] = k_n
            k_ref[rows, hs] = _rope(k_n, cos, sin, DH_C // 4).astype(BF16)
        v = _dot(hb, w_ref[:, nq + nkv:])
        stage_ref[rows, nkv:] = v
        for h in range(HKV_C):
            v_ref[rows, 2 * h * DH_C:(2 * h + 1) * DH_C] = v[:, h * DH_C:(h + 1) * DH_C].astype(BF16)
            v_ref[rows, (2 * h + 1) * DH_C:2 * (h + 1) * DH_C] = jnp.ones((n, DH_C), BF16)
        for lo in range(0, nq, MXU_N):
            z = _dot(hb, w_ref[:, lo:lo + MXU_N])
            for j in range(0, MXU_N, DH_C):
                q_ref[rows, lo + j:lo + j + DH_C] = (
                    _rope(_rms(z[:, j:j + DH_C], gq_ref[...]), cos, sin, DH_C // 4) * scale).astype(BF16)

    _staged([part(slice(r, r + TM)) for r in range(0, TMB, TM)])
    _copy_ctx(stage_ref, ((0, DH_C, HKV_C, kctx_ref), (nkv, DH_C, HKV_C, vctx_ref)))


def _inproj_odd(x, gamma, mod, w_in, g_q, g_k, cos, sin):
    nq, nkv = H_C * DH_C, HKV_C * DH_C
    return pl.pallas_call(
        _inproj_odd_kernel,
        grid=(M_TOK // TMB,),
        in_specs=[
            _row_spec(D_MODEL, tm=TMB), _const_spec((1, D_MODEL)), _mod_spec(0, TMB), _mod_spec(1, TMB),
            _const_spec((D_MODEL, nq + 2 * nkv)), _const_spec((1, DH_C)), _const_spec((1, DH_C)),
            pl.BlockSpec((TMB, LANES), lambda i: (_pos_block(i, TMB), 0)),
            pl.BlockSpec((TMB, LANES), lambda i: (_pos_block(i, TMB), 0)),
        ],
        out_specs=[_row_spec(nq, tm=TMB), _row_spec(nkv, tm=TMB), _row_spec(2 * nkv, tm=TMB),
                   _ctx_parts_spec(HKV_C, TMB), _ctx_parts_spec(HKV_C, TMB)],
        out_shape=[
            jax.ShapeDtypeStruct((M_TOK, nq), BF16), jax.ShapeDtypeStruct((M_TOK, nkv), BF16),
            jax.ShapeDtypeStruct((M_TOK, 2 * nkv), BF16), jax.ShapeDtypeStruct((N_CTX * HKV_C, LANES), F32),
            jax.ShapeDtypeStruct((N_CTX * HKV_C, LANES), F32),
        ],
        scratch_shapes=[pltpu.VMEM((TMB, 2 * nkv), F32)],
        compiler_params=_cparams(("arbitrary",)),
        name="inproj_odd",
    )(x, gamma, mod, mod, w_in, g_q, g_k, cos, sin)


def _gqa_body(q_ref, kv_refs, o_ref, seqs):
    rep = H_C // HKV_C
    outs, parts = [], []
    for rows in seqs:
        for hk in range(HKV_C):
            ks = slice(hk * DH_C, (hk + 1) * DH_C)
            q_g = jnp.concatenate(
                [q_ref[rows, (hk * rep + g) * DH_C:(hk * rep + g + 1) * DH_C] for g in range(rep)], axis=0)
            kv = [(k_ref[rows if n == 0 else slice(None), ks],
                   v_ref[rows if n == 0 else slice(None), 2 * hk * DH_C:2 * (hk + 1) * DH_C])
                  for n, (k_ref, v_ref) in enumerate(kv_refs)]
            outs.append((rows, hk, []))
            parts.append(_attend(q_g, kv, outs[-1][2]))
    _staged(parts)
    for rows, hk, o in outs:
        tq = o[0].shape[0] // rep
        for g in range(rep):
            o_ref[rows, (hk * rep + g) * DH_C:(hk * rep + g + 1) * DH_C] = o[0][g * tq:(g + 1) * tq].astype(BF16)


def _gqa_kernel(q_ref, k_ref, v_ref, o_ref):
    _gqa_body(q_ref, [(k_ref, v_ref)], o_ref, _seq_rows(q_ref))


def _gqa_cached_kernel(q_ref, k_ref, v_ref, kc_ref, vc_ref, o_ref):
    _gqa_body(q_ref, [(k_ref, v_ref), (kc_ref, vc_ref)], o_ref, [slice(None)])


def _gqa(q, k, v, k_cache, v_cache):
    nq, nkv = H_C * DH_C, HKV_C * DH_C
    ctx = pl.pallas_call(
        _gqa_kernel,
        grid=(N_CTX // TMC,),
        in_specs=[_row_spec(nq, tm=TMC), _row_spec(nkv, tm=TMC), _row_spec(2 * nkv, tm=TMC)],
        out_specs=_row_spec(nq, tm=TMC),
        out_shape=jax.ShapeDtypeStruct((N_CTX, nq), BF16),
        compiler_params=_cparams(("arbitrary",)),
        name="gqa_ctx",
    )(q, k, v)
    nqb = DEC_SEQ // TQ_LAT
    off = N_CTX // TQ_LAT
    seq0 = N_CTX // DEC_SEQ
    lat = pl.pallas_call(
        _gqa_cached_kernel,
        grid=(DEC_BATCH, nqb),
        in_specs=[
            pl.BlockSpec((TQ_LAT, nq), lambda b, j: (off + b * nqb + j, 0)),
            pl.BlockSpec((DEC_SEQ, nkv), lambda b, j: (seq0 + b, 0)),
            pl.BlockSpec((DEC_SEQ, 2 * nkv), lambda b, j: (seq0 + b, 0)),
            pl.BlockSpec((None, PAST_LEN, nkv), lambda b, j: (b, 0, 0)),
            pl.BlockSpec((None, PAST_LEN, 2 * nkv), lambda b, j: (b, 0, 0)),
        ],
        out_specs=pl.BlockSpec((TQ_LAT, nq), lambda b, j: (b * nqb + j, 0)),
        out_shape=jax.ShapeDtypeStruct((N_LAT, nq), BF16),
        compiler_params=_cparams(("arbitrary", "arbitrary")),
        name="gqa_lat",
    )(q, k, v, k_cache, v_cache)
    return ctx, lat


def _postmix_odd_kernel(actx_ref, alat_ref, x_ref, g1_ref, wo_ref, gam_ref, sh_ref, sc_ref, wrh_ref, wrl_ref,
                        x1_ref, h_ref, w1_ref, w2_ref, idx_ref, cnt_ref, run0_ref, run1_ref):
    @pl.when(pl.program_id(0) == 0)
    def _():
        run0_ref[...] = jnp.zeros_like(run0_ref)
        run1_ref[...] = jnp.zeros_like(run1_ref)

    is_lat = _is_lat(TMB)
    lane = lax.broadcasted_iota(jnp.int32, (TMB, LANES), 1)
    lane_f = lane.astype(F32)
    picked = []

    def part(rows):
        attn = jnp.where(is_lat, alat_ref[rows, :], actx_ref[rows, :])
        x1 = x_ref[rows, :] + g1_ref[...] * _dot(attn, wo_ref[...])
        x1_ref[rows, :] = x1
        yield
        h = _norm_mod(x1, gam_ref[...], sc_ref[...], sh_ref[...])
        h_hi, h_lo = _split2(h)
        h_ref[rows, :] = h_hi
        logits = _dot(h_hi, wrh_ref[...]) + (_dot(h_hi, wrl_ref[...]) + _dot(h_lo, wrh_ref[...]))
        yield
        lane_p = lax.broadcasted_iota(jnp.int32, logits.shape, 1).astype(F32)
        lg = jnp.where(lane_p < N_EXPERTS, logits, -jnp.inf)
        m1 = jnp.max(lg, axis=-1, keepdims=True)
        i1 = jnp.min(jnp.where(lg == m1, lane_p, float(LANES)), axis=-1, keepdims=True)
        lg2 = jnp.where(lane_p == i1, -jnp.inf, lg)
        m2 = jnp.max(lg2, axis=-1, keepdims=True)
        i2 = jnp.min(jnp.where(lg2 == m2, lane_p, float(LANES)), axis=-1, keepdims=True)
        e = jnp.exp(m2 - m1)
        w1 = 1.0 / (1.0 + e)
        w1_ref[rows, :] = jnp.broadcast_to(w1, logits.shape)
        w2_ref[rows, :] = jnp.broadcast_to(e * w1, logits.shape)
        picked.append((jnp.broadcast_to(i1, logits.shape), jnp.broadcast_to(i2, logits.shape)))

    _staged([part(slice(r, r + TM)) for r in range(0, TMB, TM)])
    i1 = jnp.concatenate([p[0] for p in picked], axis=0)
    i2 = jnp.concatenate([p[1] for p in picked], axis=0)
    oh1 = jnp.where(lane_f == i1, 1.0, 0.0)
    oh2 = jnp.where(lane_f == i2, 1.0, 0.0)
    r_i = lax.broadcasted_iota(jnp.int32, (TMB, TMB), 0)
    c_i = lax.broadcasted_iota(jnp.int32, (TMB, TMB), 1)
    before = jnp.where(c_i < r_i, 1.0, 0.0).astype(BF16)
    rank1 = jnp.sum((_dot(before, oh1.astype(BF16)) + run0_ref[0:1, :]) * oh1, axis=-1, keepdims=True)
    rank2 = jnp.sum((_dot(before, oh2.astype(BF16)) + run1_ref[0:1, :]) * oh2, axis=-1, keepdims=True)
    run0_ref[...] = run0_ref[...] + jnp.sum(oh1, axis=0, keepdims=True)
    run1_ref[...] = run1_ref[...] + jnp.sum(oh2, axis=0, keepdims=True)
    sub = lax.broadcasted_iota(jnp.int32, cnt_ref.shape, 0)
    cnt_ref[...] = jnp.where(sub == 0, run0_ref[...], run1_ref[...])
    quarter = LANES // 4
    cols = jnp.where(lane < quarter, i1, jnp.where(lane < 2 * quarter, i2, jnp.where(
        lane < 3 * quarter, rank1, rank2)))
    rows_t = jnp.transpose(cols)
    idx_ref[...] = jnp.concatenate(
        [rows_t[q * quarter:q * quarter + 1] for q in range(4)] + [jnp.zeros((4, TMB), F32)],
        axis=0).astype(jnp.int32)


def _postmix_odd(attn_ctx, attn_lat, x, mod, w_out, gamma, w_router):
    wr_hi = w_router.astype(BF16)
    wr_lo = (w_router - wr_hi.astype(F32)).astype(BF16)
    row = functools.partial(_row_spec, tm=TMB)
    return pl.pallas_call(
        _postmix_odd_kernel,
        grid=(M_TOK // TMB,),
        in_specs=[_ctx_spec(H_C * DH_C, TMB), _lat_spec(H_C * DH_C, TMB), row(D_MODEL), _mod_spec(2, TMB),
                  _const_spec((H_C * DH_C, D_MODEL)),
                  _const_spec((1, D_MODEL)), _mod_spec(3, TMB), _mod_spec(4, TMB), _const_spec((D_MODEL, LANES)),
                  _const_spec((D_MODEL, LANES))],
        out_specs=[row(D_MODEL), row(D_MODEL), row(LANES), row(LANES), pl.BlockSpec((8, TMB), lambda i: (0, i)),
                   _const_spec((8, LANES))],
        out_shape=[
            jax.ShapeDtypeStruct((M_TOK, D_MODEL), F32), jax.ShapeDtypeStruct((M_TOK, D_MODEL), BF16),
            jax.ShapeDtypeStruct((M_TOK, LANES), F32), jax.ShapeDtypeStruct((M_TOK, LANES), F32),
            jax.ShapeDtypeStruct((8, M_TOK), jnp.int32), jax.ShapeDtypeStruct((8, LANES), F32),
        ],
        scratch_shapes=[pltpu.VMEM((8, LANES), F32), pltpu.VMEM((8, LANES), F32)],
        compiler_params=_cparams(("arbitrary",)),
        name="postmix_odd",
    )(attn_ctx, attn_lat, x, mod, w_out, gamma, mod, mod, wr_hi, wr_lo)


def _cast_rows(src_ref, dst_ref, piece):
    def body(r, carry):
        rows = pl.ds(pl.multiple_of(r * piece, piece), piece)
        dst_ref[rows, :] = src_ref[rows, :].astype(BF16)
        return carry

    lax.fori_loop(0, src_ref.shape[0] // piece, body, 0)


def _experts_up_kernel(tile_ref, chunk_ref, exp_ref, first_ref, valid_ref, x_ref, wg_ref, wu_ref, act_ref,
                       wgb_ref, wub_ref):
    s = pl.program_id(0)

    @pl.when(first_ref[s] == 1)
    def _():
        _cast_rows(wg_ref, wgb_ref, 128)
        _cast_rows(wu_ref, wub_ref, 128)

    @pl.when(valid_ref[s] == 1)
    def _():
        _swiglu_act(x_ref[...].astype(BF16), wgb_ref, wub_ref, act_ref)

    @pl.when(valid_ref[s] == 0)
    def _():
        piece = 64

        def body(r, carry):
            act_ref[pl.ds(pl.multiple_of(r * piece, piece), piece), :] = jnp.zeros((piece, TF_E), BF16)
            return carry

        lax.fori_loop(0, TME // piece, body, 0)


def _experts_down_kernel(act_tile_ref, exp_ref, first_ref, valid_ref, act_ref, wd_ref, o_ref, wdb_ref):
    i = pl.program_id(0)

    @pl.when(first_ref[i] == 1)
    def _():
        _cast_rows(wd_ref, wdb_ref, 256)

    @pl.when(valid_ref[i] == 1)
    def _():
        o_ref[...] = _dot(act_ref[...], wdb_ref[...])

    @pl.when(valid_ref[i] == 0)
    def _():
        o_ref[...] = jnp.zeros_like(o_ref)


def _experts(up_tables, down_tables, xs, w_gate, w_up, w_down):
    def w_map(s, t, c, e, f, v):
        return e[s], 0, jnp.where(v[s] == 1, c[s], NC_E - 1)

    up = pl.pallas_call(
        _experts_up_kernel,
        grid_spec=pltpu.PrefetchScalarGridSpec(
            num_scalar_prefetch=5,
            grid=(NC_E * NT_E,),
            in_specs=[
                pl.BlockSpec((TME, D_MODEL), lambda s, t, c, e, f, v: (t[s], 0)),
                pl.BlockSpec((None, D_MODEL, TF_E), w_map), pl.BlockSpec((None, D_MODEL, TF_E), w_map),
            ],
            out_specs=pl.BlockSpec((TME, TF_E), lambda s, t, c, e, f, v: (t[s], c[s])),
            scratch_shapes=[pltpu.VMEM((D_MODEL, TF_E), BF16), pltpu.VMEM((D_MODEL, TF_E), BF16)],
        ),
        out_shape=jax.ShapeDtypeStruct((NT_E * TME, D_FF_E), BF16),
        compiler_params=_cparams(("arbitrary",)),
        name="experts_up",
    )(*up_tables, xs, w_gate, w_up)
    return pl.pallas_call(
        _experts_down_kernel,
        grid_spec=pltpu.PrefetchScalarGridSpec(
            num_scalar_prefetch=4,
            grid=(NT_E,),
            in_specs=[
                pl.BlockSpec((TME, D_FF_E), lambda i, a, e, f, v: (a[i], 0)),
                pl.BlockSpec((None, D_FF_E, D_MODEL), lambda i, a, e, f, v: (e[i], 0, 0)),
            ],
            out_specs=pl.BlockSpec((TME, D_MODEL), lambda i, a, e, f, v: (i, 0)),
            scratch_shapes=[pltpu.VMEM((D_FF_E, D_MODEL), BF16)],
        ),
        out_shape=jax.ShapeDtypeStruct((NT_E * TME, D_MODEL), F32),
        compiler_params=_cparams(("arbitrary",)),
        name="experts_down",
    )(*down_tables, up, w_down)


def _combine_kernel(x_ref, y1_ref, y2_ref, w1_ref, w2_ref, g2_ref, gam_ref, o_ref):
    rep = D_MODEL // LANES
    w1 = jnp.concatenate([w1_ref[...]] * rep, axis=1)
    w2 = jnp.concatenate([w2_ref[...]] * rep, axis=1)
    x2 = x_ref[...] + g2_ref[...] * (w1 * y1_ref[...] + w2 * y2_ref[...])
    o_ref[...] = _rms(x2, gam_ref[...])


def _combine(x1, y1, y2, w1, w2, mod, gamma, row0):
    nrows = y1.shape[0]
    blk0 = row0 // TMB

    def rows(width):
        return pl.BlockSpec((TMB, width), lambda i: (blk0 + i, 0))

    return pl.pallas_call(
        _combine_kernel,
        grid=(nrows // TMB,),
        in_specs=[rows(D_MODEL), _row_spec(D_MODEL, tm=TMB), _row_spec(D_MODEL, tm=TMB), rows(LANES), rows(LANES),
                  pl.BlockSpec((None, 1, D_MODEL), lambda i: (_group(blk0 + i, TMB), 0, 5)),
                  _const_spec((1, D_MODEL))],
        out_specs=_row_spec(D_MODEL, tm=TMB),
        out_shape=jax.ShapeDtypeStruct((nrows, D_MODEL), F32),
        compiler_params=_cparams(("arbitrary",)),
        name="combine",
    )(x1, y1, y2, w1, w2, mod, gamma)


def _rope_tables(rot_dim):
    n = DEC_SEQ
    rows = np.repeat(np.arange(n // GRID_W, dtype=np.float64), GRID_W)
    cols = np.tile(np.arange(GRID_W, dtype=np.float64), n // GRID_W)
    half = rot_dim // 2
    freqs = ROPE_THETA ** (-np.arange(0, half, 2, dtype=np.float64) / half)
    ang_r, ang_c = rows[:, None] * freqs, cols[:, None] * freqs
    cos = np.concatenate([np.cos(ang_r)] * 2 + [np.cos(ang_c)] * 2, axis=-1)
    sin = np.concatenate([-np.sin(ang_r), np.sin(ang_r), -np.sin(ang_c), np.sin(ang_c)], axis=-1)
    rep = LANES // rot_dim
    return (jnp.asarray(np.tile(cos, (1, rep)).astype(np.float32)),
            jnp.asarray(np.tile(sin, (1, rep)).astype(np.float32)))


def _with_ones(v):
    return jnp.concatenate([v, jnp.ones_like(v)], axis=-1)


def _rows(a, idx):
    return a.at[idx].get(mode="promise_in_bounds")


def _route(idx, counts_f):
    experts = jnp.arange(N_EXPERTS, dtype=jnp.int32)
    e2 = idx[0:2]
    first_total = counts_f[0, :N_EXPERTS].astype(jnp.int32)
    counts = first_total + counts_f[1, :N_EXPERTS].astype(jnp.int32)
    tiles = (counts + TME - 1) // TME
    tile_end = jnp.cumsum(tiles)
    tile_start = tile_end - tiles
    start = jnp.cumsum(counts) - counts

    def lookup(table, keys):
        return sum(jnp.where(keys == k, table[k], 0) for k in range(N_EXPERTS))

    rank2 = jnp.stack([idx[2], idx[3] + lookup(first_total, e2[1])], axis=0)
    pos = lookup(tile_start * TME, e2) + rank2
    tile_id = jnp.arange(NT_E, dtype=jnp.int32)
    tile_expert = jnp.minimum(jnp.sum((tile_id[:, None] >= tile_end[None, :]).astype(jnp.int32), axis=1),
                              N_EXPERTS - 1).astype(jnp.int32)
    tile_valid = (tile_id < tile_end[-1]).astype(jnp.int32)
    t_oh = tile_expert[:, None] == experts[None, :]
    t_first = jnp.sum(jnp.where(t_oh, (start - tile_start * TME)[None, :], 0), axis=1) + tile_id * TME
    t_last = jnp.sum(jnp.where(t_oh, (start + counts)[None, :], 0), axis=1)
    order = jnp.argsort(e2.reshape(-1), stable=True).astype(jnp.int32)
    n_assign = order.shape[0]
    g_idx = t_first[:, None] + jnp.arange(TME, dtype=jnp.int32)[None, :]
    live = jnp.logical_and(g_idx < t_last[:, None], tile_valid[:, None] == 1)
    src = _rows(order, jnp.clip(g_idx, 0, n_assign - 1).reshape(-1)) % M_TOK
    src_tok = jnp.where(live.reshape(-1), src, 0).astype(jnp.int32)

    n_tiles = tile_end[-1]
    last_tile = jnp.maximum(n_tiles - 1, 0)
    last_expert = jnp.sum(jnp.where(tile_id == last_tile, tile_expert, 0))
    t_tstart = jnp.sum(jnp.where(t_oh, tile_start[None, :], 0), axis=1)
    down_tables = (
        jnp.where(tile_valid == 1, tile_id, last_tile).astype(jnp.int32),
        tile_expert,
        jnp.logical_and(tile_valid == 1, tile_id == t_tstart).astype(jnp.int32),
        tile_valid,
    )
    s_id = jnp.arange(NC_E * NT_E, dtype=jnp.int32)
    s_exp = jnp.minimum(jnp.sum((s_id[:, None] >= NC_E * tile_end[None, :]).astype(jnp.int32), axis=1),
                        N_EXPERTS - 1)
    s_oh = s_exp[:, None] == experts[None, :]
    s_tstart = jnp.sum(jnp.where(s_oh, tile_start[None, :], 0), axis=1)
    s_ntile = jnp.maximum(jnp.sum(jnp.where(s_oh, tiles[None, :], 0), axis=1), 1)
    rel = s_id - NC_E * s_tstart
    s_valid = s_id < NC_E * n_tiles
    spare = s_id - NC_E * n_tiles
    up_tables = (
        jnp.where(s_valid, s_tstart + rel % s_ntile, n_tiles + spare // NC_E).astype(jnp.int32),
        jnp.where(s_valid, rel // s_ntile, spare % NC_E).astype(jnp.int32),
        jnp.where(s_valid, s_exp, last_expert).astype(jnp.int32),
        jnp.logical_and(s_valid, rel % s_ntile == 0).astype(jnp.int32),
        s_valid.astype(jnp.int32),
    )
    return src_tok, pos, up_tables, down_tables


def lambda_init(layer):
    return 0.8 - 0.6 * math.exp(-0.3 * layer)


def kernel(x_prompt, x_sample, state_a, cache_b_k, cache_b_v, cache_c_k, cache_c_v, c, c_ctx, w_mod, b_mod, norm_mix, norm_ffn, w_in_even, w_gate2_a, b_gate_a, g_gla, lam_q1, lam_k1, lam_q2, lam_k2, g_sub_b, w_out_even, w_in_odd, g_q_c, g_k_c, w_out_odd, ffn_gate, ffn_up, ffn_down, w_router, exp_gate, exp_up, exp_down, norm_final):
    x_ctx, x_lat = x_prompt.reshape(N_CTX, D_MODEL), x_sample.reshape(N_LAT, D_MODEL)
    cond =jnp.concatenate([c_ctx[None, :], c, jnp.zeros((N_COND - 1 - DEC_BATCH, D_MODEL), F32)], axis=0)
    mod = _modulation(cond, w_mod, b_mod).reshape(DEPTH, N_COND, 1, 6 * D_MODEL)

    w = w_in_even[0]
    na = 2 * A_QK + 2 * A_V
    gate_lo = na
    gate_hi = na + 2 * GATE_RANK
    col_scale = jnp.concatenate([jnp.full((A_QK,), DK_A ** -0.5, F32), jnp.ones((na - A_QK,), F32),
                                 jnp.full((B_QK,), DH_B ** -0.5, F32), jnp.ones((B_QK + B_V,), F32)])
    w_main = (jnp.concatenate([w[:, :gate_lo], w[:, gate_hi:]], axis=1) * col_scale).astype(BF16)
    w_gates = jnp.pad(w[:, gate_lo:gate_hi], ((0, 0), (0, LANES - 2 * GATE_RANK))).astype(BF16)
    g2 = jnp.zeros((LANES, 2 * A_QK), F32)
    g2 = g2.at[:GATE_RANK, :A_QK].set(w_gate2_a[0, 0]).at[GATE_RANK:2 * GATE_RANK, A_QK:].set(w_gate2_a[0, 1])
    bg = b_gate_a[0].reshape(1, 2 * A_QK)
    cos_b, sin_b = _rope_tables(DH_B)
    a_proj, glog, qb, kb, vb, kb_ctx, vb_ctx = _inproj_even(
        x_ctx, x_lat, norm_mix[0][None, :], mod[0], w_main, w_gates, g2.astype(BF16), bg, cos_b, sin_b)

    s0_all = jnp.concatenate([jnp.zeros((1, 2, A_QK, DV_A), F32),
                              state_a[:, 0].reshape(DEC_BATCH, 2, A_QK, DV_A)], axis=0)
    sel = np.arange(A_QK)[:, None] // DK_A == np.arange(LANES)[None, :] // GLA_SB
    m1 = jnp.asarray(sel.astype(np.float32)).astype(BF16)
    o_f, o_b, s_fin = _gla(a_proj, glog, s0_all, m1)

    k_cache = cache_b_k[:, 0].reshape(DEC_BATCH, PAST_LEN, B_QK).astype(BF16)
    v_cache = _with_ones(cache_b_v[:, 0].astype(BF16)).reshape(DEC_BATCH, PAST_LEN, 2 * B_V)
    lam_params = [p[0][None, :] for p in (lam_q1, lam_k1, lam_q2, lam_k2)]
    attn_b = _diffattn(qb, kb, vb, k_cache, v_cache, lam_params, g_sub_b[0][None, :], lambda_init(0))

    x = _mix_ffn_even(o_f, o_b, a_proj, *attn_b, x_ctx, x_lat, mod[0], g_gla[0][None, :],
                      w_out_even[0].astype(BF16), norm_ffn[0][None, :], ffn_gate[0].astype(BF16),
                      ffn_up[0].astype(BF16), ffn_down[0].astype(BF16))

    cos_c, sin_c = _rope_tables(DH_C)
    q_c, k_c, v_c, kc_ctx, vc_ctx = _inproj_odd(x, norm_mix[1][None, :], mod[1], w_in_odd[0].astype(BF16),
                                     g_q_c[0][None, :], g_k_c[0][None, :], cos_c, sin_c)
    nkv = HKV_C * DH_C
    k_cache = cache_c_k[:, 0].reshape(DEC_BATCH, PAST_LEN, nkv).astype(BF16)
    v_cache = _with_ones(cache_c_v[:, 0].astype(BF16)).reshape(DEC_BATCH, PAST_LEN, 2 * nkv)
    attn_c = _gqa(q_c, k_c, v_c, k_cache, v_cache)
    w_r = jnp.pad(w_router[0], ((0, 0), (0, LANES - N_EXPERTS)))
    x1, h_moe, w1, w2, idx, counts = _postmix_odd(*attn_c, x, mod[1], w_out_odd[0].astype(BF16),
                                                  norm_ffn[1][None, :], w_r)

    src_tok, pos, up_tables, down_tables = _route(idx, counts)
    xs = _rows(h_moe, src_tok)
    ys = _experts(up_tables, down_tables, xs, exp_gate[0], exp_up[0], exp_down[0])
    y_ctx = _combine(x1, _rows(ys, pos[0, :N_CTX]), _rows(ys, pos[1, :N_CTX]), w1, w2, mod[1],
                     norm_final[None, :], 0)
    y_lat = _combine(x1, _rows(ys, pos[0, N_CTX:]), _rows(ys, pos[1, N_CTX:]), w1, w2, mod[1],
                     norm_final[None, :], N_CTX)

    y_prompt = y_ctx.reshape(BATCH, SEQ, D_MODEL)
    y_sample = y_lat.reshape(DEC_BATCH, DEC_SEQ, D_MODEL)
    new_state_a = s_fin.reshape(BATCH, 1, 2, H_A, DK_A, DV_A)
    new_b_k = jnp.transpose(kb_ctx.reshape(BATCH, 1, H_B, 2, DH_B, SEQ), (0, 1, 5, 2, 3, 4))
    new_b_v = vb_ctx.reshape(BATCH, 1, SEQ, H_B, DV_B)
    new_c_k = kc_ctx.reshape(BATCH, 1, SEQ, HKV_C, DH_C)
    new_c_v = vc_ctx.reshape(BATCH, 1, SEQ, HKV_C, DH_C)
    return (y_prompt, y_sample, new_state_a, new_b_k, new_b_v, new_c_k, new_c_v)
```

```python
import functools
import math

import numpy as np
import jax
import jax.numpy as jnp
from jax import lax
from jax.experimental import pallas as pl
from jax.experimental.pallas import tpu as pltpu

D_MODEL = 1024
BATCH = 32
SEQ = 256
DEPTH = 2
DEC_BATCH = 2
DEC_SEQ = 2048
PAST_LEN = 512
GRID_W = 64
H_A, DK_A, DV_A = 4, 64, 128
GATE_RANK = 16
GATE_TAU = 16.0
H_B, DH_B = 4, 64
DV_B = 2 * DH_B
H_C, HKV_C, DH_C = 8, 2, 128
D_FF = 2816
N_EXPERTS = 8
D_FF_E = 3584
ROPE_THETA = 10000.0
EPS = 1e-6
A_QK = H_A * DK_A
A_V = H_A * DV_A
B_QK = H_B * 2 * DH_B
B_V = H_B * DV_B

F32 = jnp.float32
BF16 = jnp.bfloat16
LOG2E = math.log2(math.e)

V7X_VMEM_BYTES = 64 * 1024 * 1024
VMEM_LIMIT = V7X_VMEM_BYTES * 7 // 8
LANES = 128
MXU_N = 256
TM = 256
N_CTX = BATCH * SEQ
N_LAT = DEC_BATCH * DEC_SEQ
M_TOK = N_CTX + N_LAT
NB = M_TOK // TM
NCB = N_CTX // TM
LBB = DEC_SEQ // TM
N_COND = 8
GLA_C = 64
GLA_SB = 8
TME = 512
NT_E = 2 * M_TOK // TME + N_EXPERTS
NC_E = 2
TF_E = D_FF_E // NC_E
TQ_LAT = 256
TMC = 2 * SEQ
TMB = 512

assert SEQ == TM and DEC_SEQ % TM == 0 and DEPTH == 2


def _cparams(sem):
    return pltpu.CompilerParams(dimension_semantics=sem, vmem_limit_bytes=VMEM_LIMIT)


def _group(i, tm=TM):
    return jnp.where(i < N_CTX // tm, 0, 1 + (i - N_CTX // tm) // (DEC_SEQ // tm))


def _pos_block(i, tm=TM):
    return jnp.where(i < N_CTX // tm, 0, (i - N_CTX // tm) % (DEC_SEQ // tm))


def _mod_spec(chunk, tm=TM):
    return pl.BlockSpec((None, 1, D_MODEL), lambda i, c=chunk: (_group(i, tm), 0, c))


def _row_spec(width, col=0, tm=TM):
    return pl.BlockSpec((tm, width), lambda i, c=col: (i, c))


def _ctx_spec(width, tm=TM):
    return pl.BlockSpec((tm, width), lambda i: (jnp.minimum(i, N_CTX // tm - 1), 0))


def _ctx_parts_spec(parts, tm=TM):
    return pl.BlockSpec((tm * parts, LANES), lambda i: (jnp.minimum(i, N_CTX // tm - 1), 0))


def _lat_spec(width, tm=TM):
    return pl.BlockSpec((tm, width), lambda i: (jnp.maximum(i - N_CTX // tm, 0), 0))


def _const_spec(shape):
    nd = len(shape)
    return pl.BlockSpec(shape, lambda i, nd=nd: (0,) * nd, pipeline_mode=pl.Buffered(1))


def _dot(a, b):
    return jnp.dot(a, b, preferred_element_type=F32)


def _dot_nt(a, b):
    return lax.dot_general(a, b, (((1,), (1,)), ((), ())), preferred_element_type=F32)


def _split2(a):
    hi = a.astype(BF16)
    lo = (a - hi.astype(F32)).astype(BF16)
    return hi, lo


def _dot_hi(a, b):
    a_hi, a_lo = _split2(a)
    b_hi, b_lo = _split2(b)
    return _dot(a_hi, b_hi) + (_dot(a_hi, b_lo) + _dot(a_lo, b_hi))


def _silu(x):
    return (0.5 * x) * (1.0 + jnp.tanh(0.5 * x))


def _log_sigmoid(x):
    return jnp.minimum(x, 0.0) - jnp.log(1.0 + jnp.exp(-jnp.abs(x)))


def _rms(x, g):
    return x * lax.rsqrt(jnp.mean(x * x, axis=-1, keepdims=True) + EPS) * g


def _norm_mod(x, gamma, sc, sh):
    return _rms(x, gamma) * (1.0 + sc) + sh


def _staged(parts):
    live = list(parts)
    while live:
        live = [g for g in live if next(g, True) is None]


def _copy_ctx_transposed(stage3_ref, dst_ref):
    @pl.when(pl.program_id(0) < NCB)
    def _():
        def body(j, carry):
            dst_ref[pl.ds(pl.multiple_of(j * LANES, LANES), LANES), :] = jnp.transpose(stage3_ref[j])
            return carry

        lax.fori_loop(0, stage3_ref.shape[0], body, 0)


def _is_lat(tm):
    return pl.program_id(0) >= N_CTX // tm


def _pick_rows(ctx_ref, lat_ref):
    return jnp.where(_is_lat(ctx_ref.shape[0]), lat_ref[...], ctx_ref[...])


def _copy_ctx(stage_ref, outs):
    piece = 64

    @pl.when(jnp.logical_not(_is_lat(stage_ref.shape[0])))
    def _():
        def body(r, carry):
            rows = pl.ds(pl.multiple_of(r * piece, piece), piece)
            for lo, width, parts, dst_ref in outs:
                for p in range(parts):
                    val = stage_ref[rows, lo + p * width:lo + (p + 1) * width]
                    if width < LANES:
                        val = jnp.concatenate([val, jnp.zeros((piece, LANES - width), F32)], axis=1)
                    dst_ref[pl.ds(pl.multiple_of(r * piece * parts, piece * parts) + p, piece, stride=parts), :] = val
            return carry

        lax.fori_loop(0, stage_ref.shape[0] // piece, body, 0)


def _rope(x, cos, sin, half):
    lane = lax.broadcasted_iota(jnp.int32, x.shape, 1)
    first = (lane % (2 * half)) < half
    swapped = jnp.where(first, pltpu.roll(x, LANES - half, 1), pltpu.roll(x, half, 1))
    return x * cos + swapped * sin


def _mod_kernel(cond_ref, w_ref, b_ref, o_ref):
    o_ref[...] = _dot_hi(_silu(cond_ref[...]), w_ref[...]) + b_ref[...]


def _modulation(cond, w_mod, b_mod):
    tn = 1536
    return pl.pallas_call(
        _mod_kernel,
        grid=(DEPTH, 6 * D_MODEL // tn),
        in_specs=[
            pl.BlockSpec((N_COND, D_MODEL), lambda l, j: (0, 0)),
            pl.BlockSpec((None, D_MODEL, tn), lambda l, j: (l, 0, j)),
            pl.BlockSpec((None, 1, tn), lambda l, j: (l, 0, j)),
        ],
        out_specs=pl.BlockSpec((None, N_COND, tn), lambda l, j: (l, 0, j)),
        out_shape=jax.ShapeDtypeStruct((DEPTH, N_COND, 6 * D_MODEL), F32),
        compiler_params=_cparams(("arbitrary", "arbitrary")),
        name="modulation",
    )(cond, w_mod, b_mod.reshape(DEPTH, 1, 6 * D_MODEL))


def _inproj_even_kernel(xc_ref, xl_ref, gam_ref, sh_ref, sc_ref, w_ref, wg_ref, g2_ref, bg_ref, cos_ref, sin_ref,
                        a_ref, glog_ref, qb_ref, kb_ref, vb_ref, kctx_ref, vctx_ref, stage_ref, stagek_ref):
    is_lat = pl.program_id(0) >= NCB
    na = 2 * A_QK + 2 * A_V

    def part(rows):
        n = rows.stop - rows.start
        x = jnp.where(is_lat, xl_ref[rows, :], xc_ref[rows, :])
        hb = _norm_mod(x, gam_ref[...], sc_ref[...], sh_ref[...]).astype(BF16)
        cos = jnp.where(is_lat, cos_ref[rows, :], 1.0)
        sin = jnp.where(is_lat, sin_ref[rows, :], 0.0)
        yield
        gates = _dot(hb, wg_ref[...])
        xg = _dot(gates.astype(BF16), g2_ref[...]) + bg_ref[...]
        glog_ref[rows, :] = _log_sigmoid(xg) * (1.0 / GATE_TAU)
        for lo in range(0, B_QK, MXU_N):
            zq = _dot(hb, w_ref[:, na + lo:na + lo + MXU_N])
            zk = _dot(hb, w_ref[:, na + B_QK + lo:na + B_QK + lo + MXU_N])
            zv = _dot(hb, w_ref[:, na + 2 * B_QK + lo:na + 2 * B_QK + lo + MXU_N])
            stage_ref[rows, lo:lo + MXU_N] = zv
            for j in range(0, MXU_N, LANES):
                cs = slice(lo + j, lo + j + LANES)
                vb_ref[rows, 2 * (lo + j):2 * (lo + j) + DV_B] = zv[:, j:j + LANES].astype(BF16)
                vb_ref[rows, 2 * (lo + j) + DV_B:2 * (lo + j + LANES)] = jnp.ones((n, DV_B), BF16)
                stagek_ref[(lo + j) // LANES, rows, :] = zk[:, j:j + LANES]
                qb_ref[rows, cs] = (_rope(zq[:, j:j + LANES], cos, sin, DH_B // 4) * LOG2E).astype(BF16)
                kb_ref[rows, cs] = _rope(zk[:, j:j + LANES], cos, sin, DH_B // 4).astype(BF16)
        for lo in range(0, na, MXU_N):
            a_ref[rows, lo:lo + MXU_N] = _dot(hb, w_ref[:, lo:lo + MXU_N])

    _staged([part(slice(0, TM))])
    _copy_ctx(stage_ref, ((0, DV_B, H_B, vctx_ref),))
    _copy_ctx_transposed(stagek_ref, kctx_ref)


def _inproj_even(x_ctx, x_lat, gamma, mod, w_main, w_gates, g2, bg, cos, sin):
    na = 2 * A_QK + 2 * A_V
    nz = na + 2 * B_QK + B_V
    return pl.pallas_call(
        _inproj_even_kernel,
        grid=(NB,),
        in_specs=[
            _ctx_spec(D_MODEL), _lat_spec(D_MODEL), _const_spec((1, D_MODEL)), _mod_spec(0), _mod_spec(1),
            _const_spec((D_MODEL, nz)), _const_spec((D_MODEL, LANES)), _const_spec((LANES, 2 * A_QK)),
            _const_spec((1, 2 * A_QK)),
            pl.BlockSpec((TM, LANES), lambda i: (_pos_block(i), 0)),
            pl.BlockSpec((TM, LANES), lambda i: (_pos_block(i), 0)),
        ],
        out_specs=[_pair_spec(na), _pair_spec(2 * A_QK), _row_spec(B_QK), _row_spec(B_QK), _row_spec(2 * B_V),
                   pl.BlockSpec((None, B_QK, TM), lambda i: (jnp.minimum(i, NCB - 1), 0, 0)), _ctx_parts_spec(H_B)],
        out_shape=[
            jax.ShapeDtypeStruct((M_TOK, na), F32), jax.ShapeDtypeStruct((M_TOK, 2 * A_QK), F32),
            jax.ShapeDtypeStruct((M_TOK, B_QK), BF16), jax.ShapeDtypeStruct((M_TOK, B_QK), BF16),
            jax.ShapeDtypeStruct((M_TOK, 2 * B_V), BF16), jax.ShapeDtypeStruct((BATCH, B_QK, SEQ), F32),
            jax.ShapeDtypeStruct((N_CTX * H_B, LANES), F32),
        ],
        scratch_shapes=[pltpu.VMEM((TM, B_V), F32), pltpu.VMEM((B_QK // LANES, TM, LANES), F32)],
        compiler_params=_cparams(("arbitrary",)),
        name="inproj_even",
    )(x_ctx, x_lat, gamma, mod, mod, w_main, w_gates, g2, bg, cos, sin)


def _gla_chunk(q_ref, k_ref, v_ref, g_ref, o_ref, c, s_ref, a_ref, b_ref, m1_ref, head_mask, same_block, rev):
    q, k, v = q_ref[c], k_ref[c], v_ref[c]
    g = g_ref[c] * LOG2E
    v_b = v.astype(BF16)
    r_i = lax.broadcasted_iota(jnp.int32, (GLA_C, GLA_C), 0)
    c_i = lax.broadcasted_iota(jnp.int32, (GLA_C, GLA_C), 1)
    tri = jnp.where((c_i >= r_i) if rev else (c_i <= r_i), 1.0, 0.0).astype(BF16)
    g_hi = g.astype(BF16)
    g_r = g - g_hi.astype(F32)
    g_mid = g_r.astype(BF16)
    g_lo = (g_r - g_mid.astype(F32)).astype(BF16)
    bc = _dot(tri, g_hi) + (_dot(tri, g_mid) + _dot(tri, g_lo))
    b_ref[...] = bc
    yield
    t_i = lax.broadcasted_iota(jnp.int32, (GLA_SB, A_QK), 0)
    group = 2 * GLA_SB
    for i0 in range(0, GLA_C, group):
        blocks = range(i0, i0 + group, GLA_SB)
        q_i = [q_ref[c, i:i + GLA_SB, :] for i in blocks]
        b_i = [b_ref[i:i + GLA_SB, :] for i in blocks]
        for j in range(GLA_SB):
            keep = (t_i <= j) if rev else (t_i >= j)
            pair = []
            for n, i in enumerate(blocks):
                k_s = k_ref[c, i + j:i + j + 1, :]
                b_s = b_ref[i + j:i + j + 1, :]
                pair.append(jnp.where(keep, q_i[n] * k_s * jnp.exp2(b_i[n] - b_s), 0.0))
            a_ref[j * GLA_C + i0:j * GLA_C + i0 + group, :] = jnp.concatenate(pair, axis=0).astype(BF16)
    yield
    last = bc[0:1] if rev else bc[GLA_C - 1:GLA_C]
    qc = q * jnp.exp2(bc)
    kl = k * jnp.exp2(last - bc)
    row = lax.broadcasted_iota(jnp.int32, (GLA_C, A_QK), 0)

    def expand(x):
        return jnp.where(head_mask, jnp.concatenate([x] * H_A, axis=0), 0.0).astype(BF16)

    scores = None
    half = GLA_C // 2
    while half >= GLA_SB:
        mids = [bc[b + half:b + half + 1] if rev else bc[b + half - 1:b + half] for b in range(0, GLA_C, 2 * half)]
        ref = mids[0] if len(mids) == 1 else jnp.concatenate(
            [jnp.broadcast_to(m, (2 * half, A_QK)) for m in mids], axis=0)
        upper = (row // half) % 2 == 1
        later, earlier = (jnp.logical_not(upper), upper) if rev else (upper, jnp.logical_not(upper))
        q_l = jnp.where(later, q * jnp.exp2(bc - ref), 0.0)
        k_e = jnp.where(earlier, k * jnp.exp2(ref - bc), 0.0).astype(BF16)
        term = _dot_nt(expand(q_l), k_e)
        if 2 * half < GLA_C:
            term = jnp.where(same_block[2 * half], term, 0.0)
        scores = term if scores is None else scores + term
        half //= 2
    s_old = s_ref[...]
    inter = _dot(expand(qc), s_old.astype(BF16))
    sums = _dot(a_ref[...], m1_ref[...])
    yield
    lane_c = lax.broadcasted_iota(jnp.int32, (GLA_C, LANES), 1)
    within = None
    for j in range(GLA_SB):
        part = jnp.where(lane_c % GLA_SB == j, sums[j * GLA_C:(j + 1) * GLA_C], 0.0)
        within = part if within is None else within + part
    lane = lax.broadcasted_iota(jnp.int32, (GLA_SB, LANES), 1)
    per_head = []
    for h in range(H_A):
        rows_h = []
        for i in range(0, GLA_C, GLA_SB):
            moved = pltpu.roll(within[i:i + GLA_SB], (i - h * GLA_SB) % LANES, 1)
            rows_h.append(jnp.where(lane // GLA_SB == i // GLA_SB, moved, 0.0))
        per_head.append(jnp.concatenate(rows_h, axis=0)[:, :GLA_C])
    scores = scores + jnp.concatenate(per_head, axis=0)
    o_heads = []
    for h in range(H_A):
        hs = slice(h * GLA_C, (h + 1) * GLA_C)
        o_heads.append(inter[hs] + _dot(scores[hs].astype(BF16), v_b[:, h * DV_A:(h + 1) * DV_A]))
    o_ref[c] = jnp.concatenate(o_heads, axis=1)
    yield
    t = jnp.transpose(jnp.concatenate([kl, jnp.broadcast_to(last, (GLA_C, A_QK))], axis=0))
    kv = _dot(t[:, :GLA_C].astype(BF16), v_b)
    a_col = jnp.exp2(t[:, GLA_C:GLA_C + 1])
    for h in range(H_A):
        hs = slice(h * DK_A, (h + 1) * DK_A)
        s_ref[hs, :] = a_col[hs] * s_old[hs] + kv[hs, h * DV_A:(h + 1) * DV_A]


def _gla_kernel(fblk_ref, bblk_ref, first_ref, inita_ref, initb_ref,
                qf_ref, kf_ref, vf_ref, gf_ref, qr_ref, kr_ref, vr_ref, gr_ref, s0a_ref, s0b_ref, m1_ref,
                of_ref, ob_ref, sfin_ref, s_refs, a_refs, b_refs):
    step = pl.program_id(0)

    @pl.when(first_ref[step] == 1)
    def _():
        for seq, s0_ref in enumerate((s0a_ref, s0b_ref)):
            for d in range(2):
                s_refs[2 * seq + d] = s0_ref[d]

    r_h = lax.broadcasted_iota(jnp.int32, (H_A * GLA_C, A_QK), 0) // GLA_C
    c_h = lax.broadcasted_iota(jnp.int32, (H_A * GLA_C, A_QK), 1) // DK_A
    head_mask = r_h == c_h
    t_s = lax.broadcasted_iota(jnp.int32, (H_A * GLA_C, GLA_C), 0) % GLA_C
    s_s = lax.broadcasted_iota(jnp.int32, (H_A * GLA_C, GLA_C), 1)
    same_block = {}
    size = 2 * GLA_SB
    while size < GLA_C:
        same_block[size] = (t_s // size) == (s_s // size)
        size *= 2
    nch = TM // GLA_C

    def body(c, carry):
        chains = []
        for seq in range(2):
            chains.append(_gla_chunk(qf_ref, kf_ref, vf_ref, gf_ref, of_ref, seq * nch + c, s_refs.at[2 * seq],
                                     a_refs.at[2 * seq], b_refs.at[2 * seq], m1_ref, head_mask, same_block, False))
            chains.append(_gla_chunk(qr_ref, kr_ref, vr_ref, gr_ref, ob_ref, seq * nch + nch - 1 - c,
                                     s_refs.at[2 * seq + 1], a_refs.at[2 * seq + 1], b_refs.at[2 * seq + 1], m1_ref,
                                     head_mask, same_block, True))
        _staged(chains)
        return carry

    lax.fori_loop(0, nch, body, 0)

    @pl.when(step < BATCH // 2)
    def _():
        for seq in range(2):
            for d in range(2):
                sfin_ref[seq, d] = s_refs[2 * seq + d]


def _pair_block(i):
    j = i - NCB
    return jnp.where(i < NCB, i, NCB + 2 * (j % LBB) + j // LBB)


def _pair_spec(width, col=0):
    return pl.BlockSpec((TM, width), lambda i, c=col: (_pair_block(i), c))


def _gla_tables():
    fblk, bblk, first, init_a, init_b = [], [], [], [], []
    for p in range(BATCH // 2):
        fblk.append(p), bblk.append(p), first.append(1), init_a.append(0), init_b.append(0)
    for j in range(LBB):
        fblk.append(NCB // 2 + j)
        bblk.append(NCB // 2 + LBB - 1 - j)
        first.append(1 if j == 0 else 0)
        init_a.append(1), init_b.append(2)
    return [jnp.asarray(np.array(t, np.int32)) for t in (fblk, bblk, first, init_a, init_b)]


def _gla(a_proj, glog, s0_all, m1):
    tables = _gla_tables()
    nsteps = int(tables[0].shape[0])
    nseq = BATCH + DEC_BATCH

    def fmap(col):
        return lambda i, fb, bb, fi, ia, ib: (fb[i], 0, col)

    def rmap(col):
        return lambda i, fb, bb, fi, ia, ib: (bb[i], 0, col)

    nch = 2 * TM // GLA_C
    a3 = a_proj.reshape(M_TOK // GLA_C, GLA_C, a_proj.shape[1])
    g3 = glog.reshape(M_TOK // GLA_C, GLA_C, glog.shape[1])
    gs = pltpu.PrefetchScalarGridSpec(
        num_scalar_prefetch=5,
        grid=(nsteps,),
        in_specs=[
            pl.BlockSpec((nch, GLA_C, A_QK), fmap(0)), pl.BlockSpec((nch, GLA_C, A_QK), fmap(1)),
            pl.BlockSpec((nch, GLA_C, A_V), fmap(1)), pl.BlockSpec((nch, GLA_C, A_QK), fmap(0)),
            pl.BlockSpec((nch, GLA_C, A_QK), rmap(0)), pl.BlockSpec((nch, GLA_C, A_QK), rmap(1)),
            pl.BlockSpec((nch, GLA_C, A_V), rmap(1)), pl.BlockSpec((nch, GLA_C, A_QK), rmap(1)),
            pl.BlockSpec((None, 2, A_QK, DV_A), lambda i, fb, bb, fi, ia, ib: (ia[i], 0, 0, 0)),
            pl.BlockSpec((None, 2, A_QK, DV_A), lambda i, fb, bb, fi, ia, ib: (ib[i], 0, 0, 0)),
            pl.BlockSpec((A_QK, LANES), lambda i, fb, bb, fi, ia, ib: (0, 0)),
        ],
        out_specs=[
            pl.BlockSpec((nch, GLA_C, A_V), fmap(0)), pl.BlockSpec((nch, GLA_C, A_V), rmap(0)),
            pl.BlockSpec((2, 2, A_QK, DV_A),
                         lambda i, fb, bb, fi, ia, ib: (jnp.minimum(fb[i], BATCH // 2 - 1), 0, 0, 0)),
        ],
        scratch_shapes=[
            pltpu.VMEM((4, A_QK, DV_A), F32),
            pltpu.VMEM((4, GLA_SB * GLA_C, A_QK), BF16),
            pltpu.VMEM((4, GLA_C, A_QK), F32),
        ],
    )
    o_f, o_b, s_fin = pl.pallas_call(
        _gla_kernel,
        grid_spec=gs,
        out_shape=[
            jax.ShapeDtypeStruct((M_TOK // GLA_C, GLA_C, A_V), F32),
            jax.ShapeDtypeStruct((M_TOK // GLA_C, GLA_C, A_V), F32),
            jax.ShapeDtypeStruct((BATCH, 2, A_QK, DV_A), F32),
        ],
        compiler_params=_cparams(("arbitrary",)),
        name="gla",
    )(*tables, a3, a3, a3, g3, a3, a3, a3, g3, s0_all, s0_all, m1)
    return o_f.reshape(M_TOK, A_V), o_b.reshape(M_TOK, A_V), s_fin


def _attend(q_b, kv, out):
    scores = [_dot_nt(q_b, k_b) for k_b, _ in kv]
    yield
    top = functools.reduce(jnp.maximum, [jnp.max(s2, axis=-1, keepdims=True) for s2 in scores])
    both = None
    for s2, (_, v_ones) in zip(scores, kv):
        part = _dot(jnp.exp2(s2 - top).astype(BF16), v_ones)
        both = part if both is None else both + part
    yield
    dv = both.shape[1] // 2
    out.append(both[:, :dv] * (1.0 / both[:, dv:]))


def _seq_rows(ref):
    return [slice(r, r + SEQ) for r in range(0, ref.shape[0], SEQ)]


def _diffattn_body(q_ref, kv_refs, lq1_ref, lk1_ref, lq2_ref, lk2_ref, gsub_ref, o_ref, lam_init, seqs):
    lam = (jnp.exp(jnp.sum(lq1_ref[...] * lk1_ref[...], axis=-1, keepdims=True))
           - jnp.exp(jnp.sum(lq2_ref[...] * lk2_ref[...], axis=-1, keepdims=True)) + lam_init)
    outs, parts = [], []
    for rows in seqs:
        lane = lax.broadcasted_iota(jnp.int32, q_ref[rows, :LANES].shape, 1)
        for h in range(H_B):
            hs = slice(h * LANES, (h + 1) * LANES)
            q_h = q_ref[rows, hs]
            kv = [(k_ref[rows if n == 0 else slice(None), hs],
                   v_ref[rows if n == 0 else slice(None), 2 * h * DV_B:2 * (h + 1) * DV_B])
                  for n, (k_ref, v_ref) in enumerate(kv_refs)]
            zero = jnp.zeros_like(q_h)
            outs.append((rows, h, [], []))
            parts.append(_attend(jnp.where(lane < DH_B, q_h, zero), kv, outs[-1][2]))
            parts.append(_attend(jnp.where(lane >= DH_B, q_h, zero), kv, outs[-1][3]))
    _staged(parts)
    for rows, h, o1, o2 in outs:
        o = o1[0] - lam * o2[0]
        o_ref[rows, h * LANES:(h + 1) * LANES] = (_rms(o, gsub_ref[...]) * (1.0 - lam_init)).astype(BF16)


def _diffattn_kernel(q_ref, k_ref, v_ref, lq1, lk1, lq2, lk2, gsub, o_ref, *, lam_init):
    _diffattn_body(q_ref, [(k_ref, v_ref)], lq1, lk1, lq2, lk2, gsub, o_ref, lam_init, _seq_rows(q_ref))


def _diffattn_cached_kernel(q_ref, k_ref, v_ref, kc_ref, vc_ref, lq1, lk1, lq2, lk2, gsub, o_ref, *, lam_init):
    _diffattn_body(q_ref, [(k_ref, v_ref), (kc_ref, vc_ref)], lq1, lk1, lq2, lk2, gsub, o_ref, lam_init,
                   [slice(None)])


def _diffattn(qb, kb, vb, k_cache, v_cache, lam_params, gsub, lam_init):
    small = [pl.BlockSpec((1, DH_B), lambda *_: (0, 0))] * 4 + [pl.BlockSpec((1, DV_B), lambda *_: (0, 0))]
    body = functools.partial(_diffattn_kernel, lam_init=lam_init)
    ctx = pl.pallas_call(
        body,
        grid=(N_CTX // TMC,),
        in_specs=[_row_spec(B_QK, tm=TMC), _row_spec(B_QK, tm=TMC), _row_spec(2 * B_V, tm=TMC)] + small,
        out_specs=_row_spec(B_V, tm=TMC),
        out_shape=jax.ShapeDtypeStruct((N_CTX, B_V), BF16),
        compiler_params=_cparams(("arbitrary",)),
        name="diffattn_ctx",
    )(qb, kb, vb, *lam_params, gsub)
    nqb = DEC_SEQ // TQ_LAT
    off = N_CTX // TQ_LAT
    seq0 = N_CTX // DEC_SEQ
    lat = pl.pallas_call(
        functools.partial(_diffattn_cached_kernel, lam_init=lam_init),
        grid=(DEC_BATCH, nqb),
        in_specs=[
            pl.BlockSpec((TQ_LAT, B_QK), lambda b, j: (off + b * nqb + j, 0)),
            pl.BlockSpec((DEC_SEQ, B_QK), lambda b, j: (seq0 + b, 0)),
            pl.BlockSpec((DEC_SEQ, 2 * B_V), lambda b, j: (seq0 + b, 0)),
            pl.BlockSpec((None, PAST_LEN, B_QK), lambda b, j: (b, 0, 0)),
            pl.BlockSpec((None, PAST_LEN, 2 * B_V), lambda b, j: (b, 0, 0)),
        ] + small,
        out_specs=pl.BlockSpec((TQ_LAT, B_V), lambda b, j: (b * nqb + j, 0)),
        out_shape=jax.ShapeDtypeStruct((N_LAT, B_V), BF16),
        compiler_params=_cparams(("arbitrary", "arbitrary")),
        name="diffattn_lat",
    )(qb, kb, vb, k_cache, v_cache, *lam_params, gsub)
    return ctx, lat


def _swiglu_act(xb, wg_ref, wu_ref, act_ref):
    for lo in range(0, act_ref.shape[1], MXU_N):
        cs = slice(lo, lo + MXU_N)
        act_ref[:, cs] = (_silu(_dot(xb, wg_ref[:, cs])) * _dot(xb, wu_ref[:, cs])).astype(BF16)


def _mix_ffn_even_kernel(of_ref, ob_ref, ra_ref, actx_ref, alat_ref, xc_ref, xl_ref, g1_ref, ggla_ref, wo_ref,
                         gam_ref, sh_ref, sc_ref, g2_ref, wg_ref, wu_ref, wd_ref, o_ref, act_ref):
    is_lat = _is_lat(TM)

    def part(rows):
        heads = []
        for h in range(H_A):
            hs = slice(h * DV_A, (h + 1) * DV_A)
            heads.append((_rms(of_ref[rows, hs] + ob_ref[rows, hs], ggla_ref[...])
                          * _silu(ra_ref[rows, hs])).astype(BF16))
        mix = jnp.concatenate(heads + [jnp.where(is_lat, alat_ref[rows, :], actx_ref[rows, :])], axis=1)
        yield
        x = jnp.where(is_lat, xl_ref[rows, :], xc_ref[rows, :]) + g1_ref[...] * _dot(mix, wo_ref[...])
        hb = _norm_mod(x, gam_ref[...], sc_ref[...], sh_ref[...]).astype(BF16)
        yield
        for lo in range(0, D_FF, MXU_N):
            cs = slice(lo, lo + MXU_N)
            act_ref[rows, cs] = (_silu(_dot(hb, wg_ref[:, cs])) * _dot(hb, wu_ref[:, cs])).astype(BF16)
        yield
        o_ref[rows, :] = x + g2_ref[...] * _dot(act_ref[rows, :], wd_ref[...])

    _staged([part(slice(r, r + TM // 2)) for r in range(0, TM, TM // 2)])


def _mix_ffn_even(o_f, o_b, a_proj, attn_ctx, attn_lat, x_ctx, x_lat, mod, g_gla, w_out, gamma, w_gate, w_up, w_down):
    return pl.pallas_call(
        _mix_ffn_even_kernel,
        grid=(NB,),
        in_specs=[_pair_spec(A_V), _pair_spec(A_V), _pair_spec(A_V, 2), _ctx_spec(B_V), _lat_spec(B_V),
                  _ctx_spec(D_MODEL), _lat_spec(D_MODEL), _mod_spec(2), _const_spec((1, DV_A)),
                  _const_spec((A_V + B_V, D_MODEL)),
                  _const_spec((1, D_MODEL)), _mod_spec(3), _mod_spec(4), _mod_spec(5),
                  _const_spec((D_MODEL, D_FF)), _const_spec((D_MODEL, D_FF)), _const_spec((D_FF, D_MODEL))],
        out_specs=_row_spec(D_MODEL),
        out_shape=jax.ShapeDtypeStruct((M_TOK, D_MODEL), F32),
        scratch_shapes=[pltpu.VMEM((TM, D_FF), BF16)],
        compiler_params=_cparams(("arbitrary",)),
        name="mix_ffn_even",
    )(o_f, o_b, a_proj, attn_ctx, attn_lat, x_ctx, x_lat, mod, g_gla, w_out, gamma, mod, mod, mod,
      w_gate, w_up, w_down)


def _inproj_odd_kernel(x_ref, gam_ref, sh_ref, sc_ref, w_ref, gq_ref, gk_ref, cos_ref, sin_ref,
                       q_ref, k_ref, v_ref, kctx_ref, vctx_ref, stage_ref):
    is_lat = _is_lat(TMB)
    scale = DH_C ** -0.5 * LOG2E
    nq, nkv = H_C * DH_C, HKV_C * DH_C

    def part(rows):
        n = rows.stop - rows.start
        hb = _norm_mod(x_ref[rows, :], gam_ref[...], sc_ref[...], sh_ref[...]).astype(BF16)
        cos = jnp.where(is_lat, cos_ref[rows, :], 1.0)
        sin = jnp.where(is_lat, sin_ref[rows, :], 0.0)
        yield
        z = _dot(hb, w_ref[:, nq:nq + nkv])
        for h in range(HKV_C):
            hs = slice(h * DH_C, (h + 1) * DH_C)
            k_n = _rms(z[:, hs], gk_ref[...])
            stage_ref[rows, hs] = k_n
            k_ref[rows, hs] = _rope(k_n, cos, sin, DH_C // 4).astype(BF16)
        v = _dot(hb, w_ref[:, nq + nkv:])
        stage_ref[rows, nkv:] = v
        for h in range(HKV_C):
            v_ref[rows, 2 * h * DH_C:(2 * h + 1) * DH_C] = v[:, h * DH_C:(h + 1) * DH_C].astype(BF16)
            v_ref[rows, (2 * h + 1) * DH_C:2 * (h + 1) * DH_C] = jnp.ones((n, DH_C), BF16)
        for lo in range(0, nq, MXU_N):
            z = _dot(hb, w_ref[:, lo:lo + MXU_N])
            for j in range(0, MXU_N, DH_C):
                q_ref[rows, lo + j:lo + j + DH_C] = (
                    _rope(_rms(z[:, j:j + DH_C], gq_ref[...]), cos, sin, DH_C // 4) * scale).astype(BF16)

    _staged([part(slice(r, r + TM)) for r in range(0, TMB, TM)])
    _copy_ctx(stage_ref, ((0, DH_C, HKV_C, kctx_ref), (nkv, DH_C, HKV_C, vctx_ref)))


def _inproj_odd(x, gamma, mod, w_in, g_q, g_k, cos, sin):
    nq, nkv = H_C * DH_C, HKV_C * DH_C
    return pl.pallas_call(
        _inproj_odd_kernel,
        grid=(M_TOK // TMB,),
        in_specs=[
            _row_spec(D_MODEL, tm=TMB), _const_spec((1, D_MODEL)), _mod_spec(0, TMB), _mod_spec(1, TMB),
            _const_spec((D_MODEL, nq + 2 * nkv)), _const_spec((1, DH_C)), _const_spec((1, DH_C)),
            pl.BlockSpec((TMB, LANES), lambda i: (_pos_block(i, TMB), 0)),
            pl.BlockSpec((TMB, LANES), lambda i: (_pos_block(i, TMB), 0)),
        ],
        out_specs=[_row_spec(nq, tm=TMB), _row_spec(nkv, tm=TMB), _row_spec(2 * nkv, tm=TMB),
                   _ctx_parts_spec(HKV_C, TMB), _ctx_parts_spec(HKV_C, TMB)],
        out_shape=[
            jax.ShapeDtypeStruct((M_TOK, nq), BF16), jax.ShapeDtypeStruct((M_TOK, nkv), BF16),
            jax.ShapeDtypeStruct((M_TOK, 2 * nkv), BF16), jax.ShapeDtypeStruct((N_CTX * HKV_C, LANES), F32),
            jax.ShapeDtypeStruct((N_CTX * HKV_C, LANES), F32),
        ],
        scratch_shapes=[pltpu.VMEM((TMB, 2 * nkv), F32)],
        compiler_params=_cparams(("arbitrary",)),
        name="inproj_odd",
    )(x, gamma, mod, mod, w_in, g_q, g_k, cos, sin)


def _gqa_body(q_ref, kv_refs, o_ref, seqs):
    rep = H_C // HKV_C
    outs, parts = [], []
    for rows in seqs:
        for hk in range(HKV_C):
            ks = slice(hk * DH_C, (hk + 1) * DH_C)
            q_g = jnp.concatenate(
                [q_ref[rows, (hk * rep + g) * DH_C:(hk * rep + g + 1) * DH_C] for g in range(rep)], axis=0)
            kv = [(k_ref[rows if n == 0 else slice(None), ks],
                   v_ref[rows if n == 0 else slice(None), 2 * hk * DH_C:2 * (hk + 1) * DH_C])
                  for n, (k_ref, v_ref) in enumerate(kv_refs)]
            outs.append((rows, hk, []))
            parts.append(_attend(q_g, kv, outs[-1][2]))
    _staged(parts)
    for rows, hk, o in outs:
        tq = o[0].shape[0] // rep
        for g in range(rep):
            o_ref[rows, (hk * rep + g) * DH_C:(hk * rep + g + 1) * DH_C] = o[0][g * tq:(g + 1) * tq].astype(BF16)


def _gqa_kernel(q_ref, k_ref, v_ref, o_ref):
    _gqa_body(q_ref, [(k_ref, v_ref)], o_ref, _seq_rows(q_ref))


def _gqa_cached_kernel(q_ref, k_ref, v_ref, kc_ref, vc_ref, o_ref):
    _gqa_body(q_ref, [(k_ref, v_ref), (kc_ref, vc_ref)], o_ref, [slice(None)])


def _gqa(q, k, v, k_cache, v_cache):
    nq, nkv = H_C * DH_C, HKV_C * DH_C
    ctx = pl.pallas_call(
        _gqa_kernel,
        grid=(N_CTX // TMC,),
        in_specs=[_row_spec(nq, tm=TMC), _row_spec(nkv, tm=TMC), _row_spec(2 * nkv, tm=TMC)],
        out_specs=_row_spec(nq, tm=TMC),
        out_shape=jax.ShapeDtypeStruct((N_CTX, nq), BF16),
        compiler_params=_cparams(("arbitrary",)),
        name="gqa_ctx",
    )(q, k, v)
    nqb = DEC_SEQ // TQ_LAT
    off = N_CTX // TQ_LAT
    seq0 = N_CTX // DEC_SEQ
    lat = pl.pallas_call(
        _gqa_cached_kernel,
        grid=(DEC_BATCH, nqb),
        in_specs=[
            pl.BlockSpec((TQ_LAT, nq), lambda b, j: (off + b * nqb + j, 0)),
            pl.BlockSpec((DEC_SEQ, nkv), lambda b, j: (seq0 + b, 0)),
            pl.BlockSpec((DEC_SEQ, 2 * nkv), lambda b, j: (seq0 + b, 0)),
            pl.BlockSpec((None, PAST_LEN, nkv), lambda b, j: (b, 0, 0)),
            pl.BlockSpec((None, PAST_LEN, 2 * nkv), lambda b, j: (b, 0, 0)),
        ],
        out_specs=pl.BlockSpec((TQ_LAT, nq), lambda b, j: (b * nqb + j, 0)),
        out_shape=jax.ShapeDtypeStruct((N_LAT, nq), BF16),
        compiler_params=_cparams(("arbitrary", "arbitrary")),
        name="gqa_lat",
    )(q, k, v, k_cache, v_cache)
    return ctx, lat


def _postmix_odd_kernel(actx_ref, alat_ref, x_ref, g1_ref, wo_ref, gam_ref, sh_ref, sc_ref, wrh_ref, wrl_ref,
                        x1_ref, h_ref, w1_ref, w2_ref, idx_ref, cnt_ref, run0_ref, run1_ref):
    @pl.when(pl.program_id(0) == 0)
    def _():
        run0_ref[...] = jnp.zeros_like(run0_ref)
        run1_ref[...] = jnp.zeros_like(run1_ref)

    is_lat = _is_lat(TMB)
    lane = lax.broadcasted_iota(jnp.int32, (TMB, LANES), 1)
    lane_f = lane.astype(F32)
    picked = []

    def part(rows):
        attn = jnp.where(is_lat, alat_ref[rows, :], actx_ref[rows, :])
        x1 = x_ref[rows, :] + g1_ref[...] * _dot(attn, wo_ref[...])
        x1_ref[rows, :] = x1
        yield
        h = _norm_mod(x1, gam_ref[...], sc_ref[...], sh_ref[...])
        h_hi, h_lo = _split2(h)
        h_ref[rows, :] = h_hi
        logits = _dot(h_hi, wrh_ref[...]) + (_dot(h_hi, wrl_ref[...]) + _dot(h_lo, wrh_ref[...]))
        yield
        lane_p = lax.broadcasted_iota(jnp.int32, logits.shape, 1).astype(F32)
        lg = jnp.where(lane_p < N_EXPERTS, logits, -jnp.inf)
        m1 = jnp.max(lg, axis=-1, keepdims=True)
        i1 = jnp.min(jnp.where(lg == m1, lane_p, float(LANES)), axis=-1, keepdims=True)
        lg2 = jnp.where(lane_p == i1, -jnp.inf, lg)
        m2 = jnp.max(lg2, axis=-1, keepdims=True)
        i2 = jnp.min(jnp.where(lg2 == m2, lane_p, float(LANES)), axis=-1, keepdims=True)
        e = jnp.exp(m2 - m1)
        w1 = 1.0 / (1.0 + e)
        w1_ref[rows, :] = jnp.broadcast_to(w1, logits.shape)
        w2_ref[rows, :] = jnp.broadcast_to(e * w1, logits.shape)
        picked.append((jnp.broadcast_to(i1, logits.shape), jnp.broadcast_to(i2, logits.shape)))

    _staged([part(slice(r, r + TM)) for r in range(0, TMB, TM)])
    i1 = jnp.concatenate([p[0] for p in picked], axis=0)
    i2 = jnp.concatenate([p[1] for p in picked], axis=0)
    oh1 = jnp.where(lane_f == i1, 1.0, 0.0)
    oh2 = jnp.where(lane_f == i2, 1.0, 0.0)
    r_i = lax.broadcasted_iota(jnp.int32, (TMB, TMB), 0)
    c_i = lax.broadcasted_iota(jnp.int32, (TMB, TMB), 1)
    before = jnp.where(c_i < r_i, 1.0, 0.0).astype(BF16)
    rank1 = jnp.sum((_dot(before, oh1.astype(BF16)) + run0_ref[0:1, :]) * oh1, axis=-1, keepdims=True)
    rank2 = jnp.sum((_dot(before, oh2.astype(BF16)) + run1_ref[0:1, :]) * oh2, axis=-1, keepdims=True)
    run0_ref[...] = run0_ref[...] + jnp.sum(oh1, axis=0, keepdims=True)
    run1_ref[...] = run1_ref[...] + jnp.sum(oh2, axis=0, keepdims=True)
    sub = lax.broadcasted_iota(jnp.int32, cnt_ref.shape, 0)
    cnt_ref[...] = jnp.where(sub == 0, run0_ref[...], run1_ref[...])
    quarter = LANES // 4
    cols = jnp.where(lane < quarter, i1, jnp.where(lane < 2 * quarter, i2, jnp.where(
        lane < 3 * quarter, rank1, rank2)))
    rows_t = jnp.transpose(cols)
    idx_ref[...] = jnp.concatenate(
        [rows_t[q * quarter:q * quarter + 1] for q in range(4)] + [jnp.zeros((4, TMB), F32)],
        axis=0).astype(jnp.int32)


def _postmix_odd(attn_ctx, attn_lat, x, mod, w_out, gamma, w_router):
    wr_hi = w_router.astype(BF16)
    wr_lo = (w_router - wr_hi.astype(F32)).astype(BF16)
    row = functools.partial(_row_spec, tm=TMB)
    return pl.pallas_call(
        _postmix_odd_kernel,
        grid=(M_TOK // TMB,),
        in_specs=[_ctx_spec(H_C * DH_C, TMB), _lat_spec(H_C * DH_C, TMB), row(D_MODEL), _mod_spec(2, TMB),
                  _const_spec((H_C * DH_C, D_MODEL)),
                  _const_spec((1, D_MODEL)), _mod_spec(3, TMB), _mod_spec(4, TMB), _const_spec((D_MODEL, LANES)),
                  _const_spec((D_MODEL, LANES))],
        out_specs=[row(D_MODEL), row(D_MODEL), row(LANES), row(LANES), pl.BlockSpec((8, TMB), lambda i: (0, i)),
                   _const_spec((8, LANES))],
        out_shape=[
            jax.ShapeDtypeStruct((M_TOK, D_MODEL), F32), jax.ShapeDtypeStruct((M_TOK, D_MODEL), BF16),
            jax.ShapeDtypeStruct((M_TOK, LANES), F32), jax.ShapeDtypeStruct((M_TOK, LANES), F32),
            jax.ShapeDtypeStruct((8, M_TOK), jnp.int32), jax.ShapeDtypeStruct((8, LANES), F32),
        ],
        scratch_shapes=[pltpu.VMEM((8, LANES), F32), pltpu.VMEM((8, LANES), F32)],
        compiler_params=_cparams(("arbitrary",)),
        name="postmix_odd",
    )(attn_ctx, attn_lat, x, mod, w_out, gamma, mod, mod, wr_hi, wr_lo)


def _cast_rows(src_ref, dst_ref, piece):
    def body(r, carry):
        rows = pl.ds(pl.multiple_of(r * piece, piece), piece)
        dst_ref[rows, :] = src_ref[rows, :].astype(BF16)
        return carry

    lax.fori_loop(0, src_ref.shape[0] // piece, body, 0)


def _zero_rows(dst_ref):
    piece = 64

    def body(r, carry):
        dst_ref[pl.ds(pl.multiple_of(r * piece, piece), piece), :] = jnp.zeros((piece, dst_ref.shape[1]),
                                                                               dst_ref.dtype)
        return carry

    lax.fori_loop(0, dst_ref.shape[0] // piece, body, 0)


def _experts_up_kernel(tile_ref, chunk_ref, exp_ref, first_ref, valid_ref, x_ref, wg_ref, wu_ref, act_ref,
                       wgb_ref, wub_ref):
    s = pl.program_id(0)

    @pl.when(first_ref[s] == 1)
    def _():
        _cast_rows(wg_ref, wgb_ref, 128)
        _cast_rows(wu_ref, wub_ref, 128)

    @pl.when(valid_ref[s] == 1)
    def _():
        _swiglu_act(x_ref[...].astype(BF16), wgb_ref, wub_ref, act_ref)

    @pl.when(valid_ref[s] == 0)
    def _():
        _zero_rows(act_ref)


def _experts_down_kernel(act_tile_ref, exp_ref, first_ref, valid_ref, act_ref, wd_ref, o_ref, wdb_ref):
    i = pl.program_id(0)

    @pl.when(first_ref[i] == 1)
    def _():
        _cast_rows(wd_ref, wdb_ref, 256)

    @pl.when(valid_ref[i] == 1)
    def _():
        o_ref[...] = _dot(act_ref[...], wdb_ref[...])

    @pl.when(valid_ref[i] == 0)
    def _():
        _zero_rows(o_ref)


def _experts(up_tables, down_tables, xs, w_gate, w_up, w_down):
    def w_map(s, t, c, e, f, v):
        return e[s], 0, jnp.where(v[s] == 1, c[s], NC_E - 1)

    up = pl.pallas_call(
        _experts_up_kernel,
        grid_spec=pltpu.PrefetchScalarGridSpec(
            num_scalar_prefetch=5,
            grid=(NC_E * NT_E,),
            in_specs=[
                pl.BlockSpec((TME, D_MODEL), lambda s, t, c, e, f, v: (t[s], 0)),
                pl.BlockSpec((None, D_MODEL, TF_E), w_map), pl.BlockSpec((None, D_MODEL, TF_E), w_map),
            ],
            out_specs=pl.BlockSpec((TME, TF_E), lambda s, t, c, e, f, v: (t[s], c[s])),
            scratch_shapes=[pltpu.VMEM((D_MODEL, TF_E), BF16), pltpu.VMEM((D_MODEL, TF_E), BF16)],
        ),
        out_shape=jax.ShapeDtypeStruct((NT_E * TME, D_FF_E), BF16),
        compiler_params=_cparams(("arbitrary",)),
        name="experts_up",
    )(*up_tables, xs, w_gate, w_up)
    return pl.pallas_call(
        _experts_down_kernel,
        grid_spec=pltpu.PrefetchScalarGridSpec(
            num_scalar_prefetch=4,
            grid=(NT_E,),
            in_specs=[
                pl.BlockSpec((TME, D_FF_E), lambda i, a, e, f, v: (a[i], 0)),
                pl.BlockSpec((None, D_FF_E, D_MODEL), lambda i, a, e, f, v: (e[i], 0, 0)),
            ],
            out_specs=pl.BlockSpec((TME, D_MODEL), lambda i, a, e, f, v: (i, 0)),
            scratch_shapes=[pltpu.VMEM((D_FF_E, D_MODEL), BF16)],
        ),
        out_shape=jax.ShapeDtypeStruct((NT_E * TME, D_MODEL), F32),
        compiler_params=_cparams(("arbitrary",)),
        name="experts_down",
    )(*down_tables, up, w_down)


def _combine_kernel(x_ref, y1_ref, y2_ref, w1_ref, w2_ref, g2_ref, gam_ref, o_ref):
    rep = D_MODEL // LANES
    w1 = jnp.concatenate([w1_ref[...]] * rep, axis=1)
    w2 = jnp.concatenate([w2_ref[...]] * rep, axis=1)
    x2 = x_ref[...] + g2_ref[...] * (w1 * y1_ref[...] + w2 * y2_ref[...])
    o_ref[...] = _rms(x2, gam_ref[...])


def _combine(x1, y1, y2, w1, w2, mod, gamma, row0):
    nrows = y1.shape[0]
    blk0 = row0 // TMB

    def rows(width):
        return pl.BlockSpec((TMB, width), lambda i: (blk0 + i, 0))

    return pl.pallas_call(
        _combine_kernel,
        grid=(nrows // TMB,),
        in_specs=[rows(D_MODEL), _row_spec(D_MODEL, tm=TMB), _row_spec(D_MODEL, tm=TMB), rows(LANES), rows(LANES),
                  pl.BlockSpec((None, 1, D_MODEL), lambda i: (_group(blk0 + i, TMB), 0, 5)),
                  _const_spec((1, D_MODEL))],
        out_specs=_row_spec(D_MODEL, tm=TMB),
        out_shape=jax.ShapeDtypeStruct((nrows, D_MODEL), F32),
        compiler_params=_cparams(("arbitrary",)),
        name="combine",
    )(x1, y1, y2, w1, w2, mod, gamma)


def _rope_tables(rot_dim):
    n = DEC_SEQ
    rows = np.repeat(np.arange(n // GRID_W, dtype=np.float64), GRID_W)
    cols = np.tile(np.arange(GRID_W, dtype=np.float64), n // GRID_W)
    half = rot_dim // 2
    freqs = ROPE_THETA ** (-np.arange(0, half, 2, dtype=np.float64) / half)
    ang_r, ang_c = rows[:, None] * freqs, cols[:, None] * freqs
    cos = np.concatenate([np.cos(ang_r)] * 2 + [np.cos(ang_c)] * 2, axis=-1)
    sin = np.concatenate([-np.sin(ang_r), np.sin(ang_r), -np.sin(ang_c), np.sin(ang_c)], axis=-1)
    rep = LANES // rot_dim
    return (jnp.asarray(np.tile(cos, (1, rep)).astype(np.float32)),
            jnp.asarray(np.tile(sin, (1, rep)).astype(np.float32)))


def _with_ones(v):
    return jnp.concatenate([v, jnp.ones_like(v)], axis=-1)


def _rows(a, idx):
    return a.at[idx].get(mode="promise_in_bounds")


def _route(idx, counts_f):
    experts = jnp.arange(N_EXPERTS, dtype=jnp.int32)
    e2 = idx[0:2]
    first_total = counts_f[0, :N_EXPERTS].astype(jnp.int32)
    counts = first_total + counts_f[1, :N_EXPERTS].astype(jnp.int32)
    tiles = (counts + TME - 1) // TME
    tile_end = jnp.cumsum(tiles)
    tile_start = tile_end - tiles
    start = jnp.cumsum(counts) - counts

    def lookup(table, keys):
        return sum(jnp.where(keys == k, table[k], 0) for k in range(N_EXPERTS))

    rank2 = jnp.stack([idx[2], idx[3] + lookup(first_total, e2[1])], axis=0)
    pos = lookup(tile_start * TME, e2) + rank2
    tile_id = jnp.arange(NT_E, dtype=jnp.int32)
    tile_expert = jnp.minimum(jnp.sum((tile_id[:, None] >= tile_end[None, :]).astype(jnp.int32), axis=1),
                              N_EXPERTS - 1).astype(jnp.int32)
    tile_valid = (tile_id < tile_end[-1]).astype(jnp.int32)
    t_oh = tile_expert[:, None] == experts[None, :]
    t_first = jnp.sum(jnp.where(t_oh, (start - tile_start * TME)[None, :], 0), axis=1) + tile_id * TME
    t_last = jnp.sum(jnp.where(t_oh, (start + counts)[None, :], 0), axis=1)
    order = jnp.argsort(e2.reshape(-1), stable=True).astype(jnp.int32)
    n_assign = order.shape[0]
    g_idx = t_first[:, None] + jnp.arange(TME, dtype=jnp.int32)[None, :]
    live = jnp.logical_and(g_idx < t_last[:, None], tile_valid[:, None] == 1)
    src = _rows(order, jnp.clip(g_idx, 0, n_assign - 1).reshape(-1)) % M_TOK
    src_tok = jnp.where(live.reshape(-1), src, 0).astype(jnp.int32)

    n_tiles = tile_end[-1]
    last_tile = jnp.maximum(n_tiles - 1, 0)
    last_expert = jnp.sum(jnp.where(tile_id == last_tile, tile_expert, 0))
    t_tstart = jnp.sum(jnp.where(t_oh, tile_start[None, :], 0), axis=1)
    down_tables = (
        jnp.where(tile_valid == 1, tile_id, last_tile).astype(jnp.int32),
        tile_expert,
        jnp.logical_and(tile_valid == 1, tile_id == t_tstart).astype(jnp.int32),
        tile_valid,
    )
    s_id = jnp.arange(NC_E * NT_E, dtype=jnp.int32)
    s_exp = jnp.minimum(jnp.sum((s_id[:, None] >= NC_E * tile_end[None, :]).astype(jnp.int32), axis=1),
                        N_EXPERTS - 1)
    s_oh = s_exp[:, None] == experts[None, :]
    s_tstart = jnp.sum(jnp.where(s_oh, tile_start[None, :], 0), axis=1)
    s_ntile = jnp.maximum(jnp.sum(jnp.where(s_oh, tiles[None, :], 0), axis=1), 1)
    rel = s_id - NC_E * s_tstart
    s_valid = s_id < NC_E * n_tiles
    spare = s_id - NC_E * n_tiles
    up_tables = (
        jnp.where(s_valid, s_tstart + rel % s_ntile, n_tiles + spare // NC_E).astype(jnp.int32),
        jnp.where(s_valid, rel // s_ntile, spare % NC_E).astype(jnp.int32),
        jnp.where(s_valid, s_exp, last_expert).astype(jnp.int32),
        jnp.logical_and(s_valid, rel % s_ntile == 0).astype(jnp.int32),
        s_valid.astype(jnp.int32),
    )
    return src_tok, pos, up_tables, down_tables


def lambda_init(layer):
    return 0.8 - 0.6 * math.exp(-0.3 * layer)


def kernel(x_prompt, x_sample, state_a, cache_b_k, cache_b_v, cache_c_k, cache_c_v, c, c_ctx, w_mod, b_mod, norm_mix, norm_ffn, w_in_even, w_gate2_a, b_gate_a, g_gla, lam_q1, lam_k1, lam_q2, lam_k2, g_sub_b, w_out_even, w_in_odd, g_q_c, g_k_c, w_out_odd, ffn_gate, ffn_up, ffn_down, w_router, exp_gate, exp_up, exp_down, norm_final):
    x_ctx, x_lat = x_prompt.reshape(N_CTX, D_MODEL), x_sample.reshape(N_LAT, D_MODEL)
    cond =jnp.concatenate([c_ctx[None, :], c, jnp.zeros((N_COND - 1 - DEC_BATCH, D_MODEL), F32)], axis=0)
    mod = _modulation(cond, w_mod, b_mod).reshape(DEPTH, N_COND, 1, 6 * D_MODEL)

    w = w_in_even[0]
    na = 2 * A_QK + 2 * A_V
    gate_lo = na
    gate_hi = na + 2 * GATE_RANK
    col_scale = jnp.concatenate([jnp.full((A_QK,), DK_A ** -0.5, F32), jnp.ones((na - A_QK,), F32),
                                 jnp.full((B_QK,), DH_B ** -0.5, F32), jnp.ones((B_QK + B_V,), F32)])
    w_main = (jnp.concatenate([w[:, :gate_lo], w[:, gate_hi:]], axis=1) * col_scale).astype(BF16)
    w_gates = jnp.pad(w[:, gate_lo:gate_hi], ((0, 0), (0, LANES - 2 * GATE_RANK))).astype(BF16)
    g2 = jnp.zeros((LANES, 2 * A_QK), F32)
    g2 = g2.at[:GATE_RANK, :A_QK].set(w_gate2_a[0, 0]).at[GATE_RANK:2 * GATE_RANK, A_QK:].set(w_gate2_a[0, 1])
    bg = b_gate_a[0].reshape(1, 2 * A_QK)
    cos_b, sin_b = _rope_tables(DH_B)
    a_proj, glog, qb, kb, vb, kb_ctx, vb_ctx = _inproj_even(
        x_ctx, x_lat, norm_mix[0][None, :], mod[0], w_main, w_gates, g2.astype(BF16), bg, cos_b, sin_b)

    s0_all = jnp.concatenate([jnp.zeros((1, 2, A_QK, DV_A), F32),
                              state_a[:, 0].reshape(DEC_BATCH, 2, A_QK, DV_A)], axis=0)
    sel = np.arange(A_QK)[:, None] // DK_A == np.arange(LANES)[None, :] // GLA_SB
    m1 = jnp.asarray(sel.astype(np.float32)).astype(BF16)
    o_f, o_b, s_fin = _gla(a_proj, glog, s0_all, m1)

    k_cache = cache_b_k[:, 0].reshape(DEC_BATCH, PAST_LEN, B_QK).astype(BF16)
    v_cache = _with_ones(cache_b_v[:, 0].astype(BF16)).reshape(DEC_BATCH, PAST_LEN, 2 * B_V)
    lam_params = [p[0][None, :] for p in (lam_q1, lam_k1, lam_q2, lam_k2)]
    attn_b = _diffattn(qb, kb, vb, k_cache, v_cache, lam_params, g_sub_b[0][None, :], lambda_init(0))

    x = _mix_ffn_even(o_f, o_b, a_proj, *attn_b, x_ctx, x_lat, mod[0], g_gla[0][None, :],
                      w_out_even[0].astype(BF16), norm_ffn[0][None, :], ffn_gate[0].astype(BF16),
                      ffn_up[0].astype(BF16), ffn_down[0].astype(BF16))

    cos_c, sin_c = _rope_tables(DH_C)
    q_c, k_c, v_c, kc_ctx, vc_ctx = _inproj_odd(x, norm_mix[1][None, :], mod[1], w_in_odd[0].astype(BF16),
                                     g_q_c[0][None, :], g_k_c[0][None, :], cos_c, sin_c)
    nkv = HKV_C * DH_C
    k_cache = cache_c_k[:, 0].reshape(DEC_BATCH, PAST_LEN, nkv).astype(BF16)
    v_cache = _with_ones(cache_c_v[:, 0].astype(BF16)).reshape(DEC_BATCH, PAST_LEN, 2 * nkv)
    attn_c = _gqa(q_c, k_c, v_c, k_cache, v_cache)
    w_r = jnp.pad(w_router[0], ((0, 0), (0, LANES - N_EXPERTS)))
    x1, h_moe, w1, w2, idx, counts = _postmix_odd(*attn_c, x, mod[1], w_out_odd[0].astype(BF16),
                                                  norm_ffn[1][None, :], w_r)

    src_tok, pos, up_tables, down_tables = _route(idx, counts)
    xs = _rows(h_moe, src_tok)
    ys = _experts(up_tables, down_tables, xs, exp_gate[0], exp_up[0], exp_down[0])
    y_ctx = _combine(x1, _rows(ys, pos[0, :N_CTX]), _rows(ys, pos[1, :N_CTX]), w1, w2, mod[1],
                     norm_final[None, :], 0)
    y_lat = _combine(x1, _rows(ys, pos[0, N_CTX:]), _rows(ys, pos[1, N_CTX:]), w1, w2, mod[1],
                     norm_final[None, :], N_CTX)

    y_prompt = y_ctx.reshape(BATCH, SEQ, D_MODEL)
    y_sample = y_lat.reshape(DEC_BATCH, DEC_SEQ, D_MODEL)
    new_state_a = s_fin.reshape(BATCH, 1, 2, H_A, DK_A, DV_A)
    new_b_k = jnp.transpose(kb_ctx.reshape(BATCH, 1, H_B, 2, DH_B, SEQ), (0, 1, 5, 2, 3, 4))
    new_b_v = vb_ctx.reshape(BATCH, 1, SEQ, H_B, DV_B)
    new_c_k = kc_ctx.reshape(BATCH, 1, SEQ, HKV_C, DH_C)
    new_c_v = vc_ctx.reshape(BATCH, 1, SEQ, HKV_C, DH_C)
    return (y_prompt, y_sample, new_state_a, new_b_k, new_b_v, new_c_k, new_c_v)
```

```python
import functools
import math

import numpy as np
import jax
import jax.numpy as jnp
from jax import lax
from jax.experimental import pallas as pl
from jax.experimental.pallas import tpu as pltpu

D_MODEL = 1024
BATCH = 32
SEQ = 256
DEPTH = 2
DEC_BATCH = 2
DEC_SEQ = 2048
PAST_LEN = 512
GRID_W = 64
H_A, DK_A, DV_A = 4, 64, 128
GATE_RANK = 16
GATE_TAU = 16.0
H_B, DH_B = 4, 64
DV_B = 2 * DH_B
H_C, HKV_C, DH_C = 8, 2, 128
D_FF = 2816
N_EXPERTS = 8
D_FF_E = 3584
ROPE_THETA = 10000.0
EPS = 1e-6
A_QK = H_A * DK_A
A_V = H_A * DV_A
B_QK = H_B * 2 * DH_B
B_V = H_B * DV_B

F32 = jnp.float32
BF16 = jnp.bfloat16
LOG2E = math.log2(math.e)

V7X_VMEM_BYTES = 64 * 1024 * 1024
VMEM_LIMIT = V7X_VMEM_BYTES * 7 // 8
LANES = 128
MXU_N = 256
TM = 256
N_CTX = BATCH * SEQ
N_LAT = DEC_BATCH * DEC_SEQ
M_TOK = N_CTX + N_LAT
NB = M_TOK // TM
NCB = N_CTX // TM
LBB = DEC_SEQ // TM
N_COND = 8
GLA_C = 64
GLA_SB = 8
TME = 512
NT_E = 2 * M_TOK // TME + N_EXPERTS
NC_E = 2
TF_E = D_FF_E // NC_E
TQ_LAT = 256
TMC = 2 * SEQ
TMB = 512

assert SEQ == TM and DEC_SEQ % TM == 0 and DEPTH == 2


def _cparams(sem):
    return pltpu.CompilerParams(dimension_semantics=sem, vmem_limit_bytes=VMEM_LIMIT)


def _group(i, tm=TM):
    return jnp.where(i < N_CTX // tm, 0, 1 + (i - N_CTX // tm) // (DEC_SEQ // tm))


def _pos_block(i, tm=TM):
    return jnp.where(i < N_CTX // tm, 0, (i - N_CTX // tm) % (DEC_SEQ // tm))


def _mod_spec(chunk, tm=TM):
    return pl.BlockSpec((None, 1, D_MODEL), lambda i, c=chunk: (_group(i, tm), 0, c))


def _row_spec(width, col=0, tm=TM):
    return pl.BlockSpec((tm, width), lambda i, c=col: (i, c))


def _ctx_spec(width, tm=TM):
    return pl.BlockSpec((tm, width), lambda i: (jnp.minimum(i, N_CTX // tm - 1), 0))


def _ctx_parts_spec(parts, tm=TM):
    return pl.BlockSpec((tm * parts, LANES), lambda i: (jnp.minimum(i, N_CTX // tm - 1), 0))


def _lat_spec(width, tm=TM):
    return pl.BlockSpec((tm, width), lambda i: (jnp.maximum(i - N_CTX // tm, 0), 0))


def _const_spec(shape):
    nd = len(shape)
    return pl.BlockSpec(shape, lambda i, nd=nd: (0,) * nd, pipeline_mode=pl.Buffered(1))


def _dot(a, b):
    return jnp.dot(a, b, preferred_element_type=F32)


def _dot_nt(a, b):
    return lax.dot_general(a, b, (((1,), (1,)), ((), ())), preferred_element_type=F32)


def _split2(a):
    hi = a.astype(BF16)
    lo = (a - hi.astype(F32)).astype(BF16)
    return hi, lo


def _dot_hi(a, b):
    a_hi, a_lo = _split2(a)
    b_hi, b_lo = _split2(b)
    return _dot(a_hi, b_hi) + (_dot(a_hi, b_lo) + _dot(a_lo, b_hi))


def _silu(x):
    return (0.5 * x) * (1.0 + jnp.tanh(0.5 * x))


def _log_sigmoid(x):
    return jnp.minimum(x, 0.0) - jnp.log(1.0 + jnp.exp(-jnp.abs(x)))


def _rms(x, g):
    return x * lax.rsqrt(jnp.mean(x * x, axis=-1, keepdims=True) + EPS) * g


def _norm_mod(x, gamma, sc, sh):
    return _rms(x, gamma) * (1.0 + sc) + sh


def _staged(parts):
    live = list(parts)
    while live:
        live = [g for g in live if next(g, True) is None]


def _copy_ctx_transposed(stage3_ref, dst_ref):
    @pl.when(pl.program_id(0) < NCB)
    def _():
        def body(j, carry):
            dst_ref[pl.ds(pl.multiple_of(j * LANES, LANES), LANES), :] = jnp.transpose(stage3_ref[j])
            return carry

        lax.fori_loop(0, stage3_ref.shape[0], body, 0)


def _is_lat(tm):
    return pl.program_id(0) >= N_CTX // tm


def _pick_rows(ctx_ref, lat_ref):
    return jnp.where(_is_lat(ctx_ref.shape[0]), lat_ref[...], ctx_ref[...])


def _copy_ctx(stage_ref, outs):
    piece = 64

    @pl.when(jnp.logical_not(_is_lat(stage_ref.shape[0])))
    def _():
        def body(r, carry):
            rows = pl.ds(pl.multiple_of(r * piece, piece), piece)
            for lo, width, parts, dst_ref in outs:
                for p in range(parts):
                    val = stage_ref[rows, lo + p * width:lo + (p + 1) * width]
                    if width < LANES:
                        val = jnp.concatenate([val, jnp.zeros((piece, LANES - width), F32)], axis=1)
                    dst_ref[pl.ds(pl.multiple_of(r * piece * parts, piece * parts) + p, piece, stride=parts), :] = val
            return carry

        lax.fori_loop(0, stage_ref.shape[0] // piece, body, 0)


def _rope(x, cos, sin, half):
    lane = lax.broadcasted_iota(jnp.int32, x.shape, 1)
    first = (lane % (2 * half)) < half
    swapped = jnp.where(first, pltpu.roll(x, LANES - half, 1), pltpu.roll(x, half, 1))
    return x * cos + swapped * sin


def _mod_kernel(cond_ref, w_ref, b_ref, o_ref):
    o_ref[...] = _dot_hi(_silu(cond_ref[...]), w_ref[...]) + b_ref[...]


def _modulation(cond, w_mod, b_mod):
    tn = 1536
    return pl.pallas_call(
        _mod_kernel,
        grid=(DEPTH, 6 * D_MODEL // tn),
        in_specs=[
            pl.BlockSpec((N_COND, D_MODEL), lambda l, j: (0, 0)),
            pl.BlockSpec((None, D_MODEL, tn), lambda l, j: (l, 0, j)),
            pl.BlockSpec((None, 1, tn), lambda l, j: (l, 0, j)),
        ],
        out_specs=pl.BlockSpec((None, N_COND, tn), lambda l, j: (l, 0, j)),
        out_shape=jax.ShapeDtypeStruct((DEPTH, N_COND, 6 * D_MODEL), F32),
        compiler_params=_cparams(("arbitrary", "arbitrary")),
        name="modulation",
    )(cond, w_mod, b_mod.reshape(DEPTH, 1, 6 * D_MODEL))


def _inproj_even_kernel(xc_ref, xl_ref, gam_ref, sh_ref, sc_ref, w_ref, wg_ref, g2_ref, bg_ref, cos_ref, sin_ref,
                        a_ref, glog_ref, qb_ref, kb_ref, vb_ref, kctx_ref, vctx_ref, stage_ref, stagek_ref):
    is_lat = pl.program_id(0) >= NCB
    na = 2 * A_QK + 2 * A_V

    def part(rows):
        n = rows.stop - rows.start
        x = jnp.where(is_lat, xl_ref[rows, :], xc_ref[rows, :])
        hb = _norm_mod(x, gam_ref[...], sc_ref[...], sh_ref[...]).astype(BF16)
        cos = jnp.where(is_lat, cos_ref[rows, :], 1.0)
        sin = jnp.where(is_lat, sin_ref[rows, :], 0.0)
        yield
        for lo in range(0, na, MXU_N):
            a_ref[rows, lo:lo + MXU_N] = _dot(hb, w_ref[:, lo:lo + MXU_N])
        for lo in range(0, B_QK, MXU_N):
            zq = _dot(hb, w_ref[:, na + lo:na + lo + MXU_N])
            zk = _dot(hb, w_ref[:, na + B_QK + lo:na + B_QK + lo + MXU_N])
            zv = _dot(hb, w_ref[:, na + 2 * B_QK + lo:na + 2 * B_QK + lo + MXU_N])
            stage_ref[rows, lo:lo + MXU_N] = zv
            for j in range(0, MXU_N, LANES):
                cs = slice(lo + j, lo + j + LANES)
                vb_ref[rows, 2 * (lo + j):2 * (lo + j) + DV_B] = zv[:, j:j + LANES].astype(BF16)
                vb_ref[rows, 2 * (lo + j) + DV_B:2 * (lo + j + LANES)] = jnp.ones((n, DV_B), BF16)
                stagek_ref[(lo + j) // LANES, rows, :] = zk[:, j:j + LANES]
                qb_ref[rows, cs] = (_rope(zq[:, j:j + LANES], cos, sin, DH_B // 4) * LOG2E).astype(BF16)
                kb_ref[rows, cs] = _rope(zk[:, j:j + LANES], cos, sin, DH_B // 4).astype(BF16)
        gates = _dot(hb, wg_ref[...])
        xg = _dot(gates.astype(BF16), g2_ref[...]) + bg_ref[...]
        glog_ref[rows, :] = _log_sigmoid(xg) * (1.0 / GATE_TAU)

    _staged([part(slice(0, TM))])
    _copy_ctx(stage_ref, ((0, DV_B, H_B, vctx_ref),))
    _copy_ctx_transposed(stagek_ref, kctx_ref)


def _inproj_even(x_ctx, x_lat, gamma, mod, w_main, w_gates, g2, bg, cos, sin):
    na = 2 * A_QK + 2 * A_V
    nz = na + 2 * B_QK + B_V
    return pl.pallas_call(
        _inproj_even_kernel,
        grid=(NB,),
        in_specs=[
            _ctx_spec(D_MODEL), _lat_spec(D_MODEL), _const_spec((1, D_MODEL)), _mod_spec(0), _mod_spec(1),
            _const_spec((D_MODEL, nz)), _const_spec((D_MODEL, LANES)), _const_spec((LANES, 2 * A_QK)),
            _const_spec((1, 2 * A_QK)),
            pl.BlockSpec((TM, LANES), lambda i: (_pos_block(i), 0)),
            pl.BlockSpec((TM, LANES), lambda i: (_pos_block(i), 0)),
        ],
        out_specs=[_pair_spec(na), _pair_spec(2 * A_QK), _row_spec(B_QK), _row_spec(B_QK), _row_spec(2 * B_V),
                   pl.BlockSpec((None, B_QK, TM), lambda i: (jnp.minimum(i, NCB - 1), 0, 0)), _ctx_parts_spec(H_B)],
        out_shape=[
            jax.ShapeDtypeStruct((M_TOK, na), F32), jax.ShapeDtypeStruct((M_TOK, 2 * A_QK), F32),
            jax.ShapeDtypeStruct((M_TOK, B_QK), BF16), jax.ShapeDtypeStruct((M_TOK, B_QK), BF16),
            jax.ShapeDtypeStruct((M_TOK, 2 * B_V), BF16), jax.ShapeDtypeStruct((BATCH, B_QK, SEQ), F32),
            jax.ShapeDtypeStruct((N_CTX * H_B, LANES), F32),
        ],
        scratch_shapes=[pltpu.VMEM((TM, B_V), F32), pltpu.VMEM((B_QK // LANES, TM, LANES), F32)],
        compiler_params=_cparams(("arbitrary",)),
        name="inproj_even",
    )(x_ctx, x_lat, gamma, mod, mod, w_main, w_gates, g2, bg, cos, sin)


def _gla_chunk(q_ref, k_ref, v_ref, g_ref, o_ref, c, s_ref, a_ref, b_ref, m1_ref, head_mask, same_block, rev):
    q, k, v = q_ref[c], k_ref[c], v_ref[c]
    g = g_ref[c] * LOG2E
    v_b = v.astype(BF16)
    r_i = lax.broadcasted_iota(jnp.int32, (GLA_C, GLA_C), 0)
    c_i = lax.broadcasted_iota(jnp.int32, (GLA_C, GLA_C), 1)
    tri = jnp.where((c_i >= r_i) if rev else (c_i <= r_i), 1.0, 0.0).astype(BF16)
    g_hi = g.astype(BF16)
    g_r = g - g_hi.astype(F32)
    g_mid = g_r.astype(BF16)
    g_lo = (g_r - g_mid.astype(F32)).astype(BF16)
    bc = _dot(tri, g_hi) + (_dot(tri, g_mid) + _dot(tri, g_lo))
    b_ref[...] = bc
    yield
    t_i = lax.broadcasted_iota(jnp.int32, (GLA_SB, A_QK), 0)
    group = 2 * GLA_SB
    for i0 in range(0, GLA_C, group):
        blocks = range(i0, i0 + group, GLA_SB)
        q_i = [q_ref[c, i:i + GLA_SB, :] for i in blocks]
        b_i = [b_ref[i:i + GLA_SB, :] for i in blocks]
        for j in range(GLA_SB):
            keep = (t_i <= j) if rev else (t_i >= j)
            pair = []
            for n, i in enumerate(blocks):
                k_s = k_ref[c, i + j:i + j + 1, :]
                b_s = b_ref[i + j:i + j + 1, :]
                pair.append(jnp.where(keep, q_i[n] * k_s * jnp.exp2(b_i[n] - b_s), 0.0))
            a_ref[j * GLA_C + i0:j * GLA_C + i0 + group, :] = jnp.concatenate(pair, axis=0).astype(BF16)
    yield
    last = bc[0:1] if rev else bc[GLA_C - 1:GLA_C]
    qc = q * jnp.exp2(bc)
    kl = k * jnp.exp2(last - bc)
    row = lax.broadcasted_iota(jnp.int32, (GLA_C, A_QK), 0)

    def expand(x):
        return jnp.where(head_mask, jnp.concatenate([x] * H_A, axis=0), 0.0).astype(BF16)

    scores = None
    half = GLA_C // 2
    while half >= GLA_SB:
        mids = [bc[b + half:b + half + 1] if rev else bc[b + half - 1:b + half] for b in range(0, GLA_C, 2 * half)]
        ref = mids[0] if len(mids) == 1 else jnp.concatenate(
            [jnp.broadcast_to(m, (2 * half, A_QK)) for m in mids], axis=0)
        upper = (row // half) % 2 == 1
        later, earlier = (jnp.logical_not(upper), upper) if rev else (upper, jnp.logical_not(upper))
        q_l = jnp.where(later, q * jnp.exp2(bc - ref), 0.0)
        k_e = jnp.where(earlier, k * jnp.exp2(ref - bc), 0.0).astype(BF16)
        term = _dot_nt(expand(q_l), k_e)
        if 2 * half < GLA_C:
            term = jnp.where(same_block[2 * half], term, 0.0)
        scores = term if scores is None else scores + term
        half //= 2
    s_old = s_ref[...]
    inter = _dot(expand(qc), s_old.astype(BF16))
    sums = _dot(a_ref[...], m1_ref[...])
    yield
    lane_c = lax.broadcasted_iota(jnp.int32, (GLA_C, LANES), 1)
    within = None
    for j in range(GLA_SB):
        part = jnp.where(lane_c % GLA_SB == j, sums[j * GLA_C:(j + 1) * GLA_C], 0.0)
        within = part if within is None else within + part
    lane = lax.broadcasted_iota(jnp.int32, (GLA_SB, LANES), 1)
    per_head = []
    for h in range(H_A):
        rows_h = []
        for i in range(0, GLA_C, GLA_SB):
            moved = pltpu.roll(within[i:i + GLA_SB], (i - h * GLA_SB) % LANES, 1)
            rows_h.append(jnp.where(lane // GLA_SB == i // GLA_SB, moved, 0.0))
        per_head.append(jnp.concatenate(rows_h, axis=0)[:, :GLA_C])
    scores = scores + jnp.concatenate(per_head, axis=0)
    o_heads = []
    for h in range(H_A):
        hs = slice(h * GLA_C, (h + 1) * GLA_C)
        o_heads.append(inter[hs] + _dot(scores[hs].astype(BF16), v_b[:, h * DV_A:(h + 1) * DV_A]))
    o_ref[c] = jnp.concatenate(o_heads, axis=1)
    yield
    t = jnp.transpose(jnp.concatenate([kl, jnp.broadcast_to(last, (GLA_C, A_QK))], axis=0))
    kv = _dot(t[:, :GLA_C].astype(BF16), v_b)
    a_col = jnp.exp2(t[:, GLA_C:GLA_C + 1])
    for h in range(H_A):
        hs = slice(h * DK_A, (h + 1) * DK_A)
        s_ref[hs, :] = a_col[hs] * s_old[hs] + kv[hs, h * DV_A:(h + 1) * DV_A]


def _gla_kernel(fblk_ref, bblk_ref, first_ref, inita_ref, initb_ref,
                qf_ref, kf_ref, vf_ref, gf_ref, qr_ref, kr_ref, vr_ref, gr_ref, s0a_ref, s0b_ref, m1_ref,
                of_ref, ob_ref, sfin_ref, s_refs, a_refs, b_refs):
    step = pl.program_id(0)

    @pl.when(first_ref[step] == 1)
    def _():
        for seq, s0_ref in enumerate((s0a_ref, s0b_ref)):
            for d in range(2):
                s_refs[2 * seq + d] = s0_ref[d]

    r_h = lax.broadcasted_iota(jnp.int32, (H_A * GLA_C, A_QK), 0) // GLA_C
    c_h = lax.broadcasted_iota(jnp.int32, (H_A * GLA_C, A_QK), 1) // DK_A
    head_mask = r_h == c_h
    t_s = lax.broadcasted_iota(jnp.int32, (H_A * GLA_C, GLA_C), 0) % GLA_C
    s_s = lax.broadcasted_iota(jnp.int32, (H_A * GLA_C, GLA_C), 1)
    same_block = {}
    size = 2 * GLA_SB
    while size < GLA_C:
        same_block[size] = (t_s // size) == (s_s // size)
        size *= 2
    nch = TM // GLA_C

    def body(c, carry):
        chains = []
        for seq in range(2):
            chains.append(_gla_chunk(qf_ref, kf_ref, vf_ref, gf_ref, of_ref, seq * nch + c, s_refs.at[2 * seq],
                                     a_refs.at[2 * seq], b_refs.at[2 * seq], m1_ref, head_mask, same_block, False))
            chains.append(_gla_chunk(qr_ref, kr_ref, vr_ref, gr_ref, ob_ref, seq * nch + nch - 1 - c,
                                     s_refs.at[2 * seq + 1], a_refs.at[2 * seq + 1], b_refs.at[2 * seq + 1], m1_ref,
                                     head_mask, same_block, True))
        _staged(chains)
        return carry

    lax.fori_loop(0, nch, body, 0)

    @pl.when(step < BATCH // 2)
    def _():
        for seq in range(2):
            for d in range(2):
                sfin_ref[seq, d] = s_refs[2 * seq + d]


def _pair_block(i):
    j = i - NCB
    return jnp.where(i < NCB, i, NCB + 2 * (j % LBB) + j // LBB)


def _pair_spec(width, col=0):
    return pl.BlockSpec((TM, width), lambda i, c=col: (_pair_block(i), c))


def _gla_tables():
    fblk, bblk, first, init_a, init_b = [], [], [], [], []
    for p in range(BATCH // 2):
        fblk.append(p), bblk.append(p), first.append(1), init_a.append(0), init_b.append(0)
    for j in range(LBB):
        fblk.append(NCB // 2 + j)
        bblk.append(NCB // 2 + LBB - 1 - j)
        first.append(1 if j == 0 else 0)
        init_a.append(1), init_b.append(2)
    return [jnp.asarray(np.array(t, np.int32)) for t in (fblk, bblk, first, init_a, init_b)]


def _gla(a_proj, glog, s0_all, m1):
    tables = _gla_tables()
    nsteps = int(tables[0].shape[0])
    nseq = BATCH + DEC_BATCH

    def fmap(col):
        return lambda i, fb, bb, fi, ia, ib: (fb[i], 0, col)

    def rmap(col):
        return lambda i, fb, bb, fi, ia, ib: (bb[i], 0, col)

    nch = 2 * TM // GLA_C
    a3 = a_proj.reshape(M_TOK // GLA_C, GLA_C, a_proj.shape[1])
    g3 = glog.reshape(M_TOK // GLA_C, GLA_C, glog.shape[1])
    gs = pltpu.PrefetchScalarGridSpec(
        num_scalar_prefetch=5,
        grid=(nsteps,),
        in_specs=[
            pl.BlockSpec((nch, GLA_C, A_QK), fmap(0)), pl.BlockSpec((nch, GLA_C, A_QK), fmap(1)),
            pl.BlockSpec((nch, GLA_C, A_V), fmap(1)), pl.BlockSpec((nch, GLA_C, A_QK), fmap(0)),
            pl.BlockSpec((nch, GLA_C, A_QK), rmap(0)), pl.BlockSpec((nch, GLA_C, A_QK), rmap(1)),
            pl.BlockSpec((nch, GLA_C, A_V), rmap(1)), pl.BlockSpec((nch, GLA_C, A_QK), rmap(1)),
            pl.BlockSpec((None, 2, A_QK, DV_A), lambda i, fb, bb, fi, ia, ib: (ia[i], 0, 0, 0)),
            pl.BlockSpec((None, 2, A_QK, DV_A), lambda i, fb, bb, fi, ia, ib: (ib[i], 0, 0, 0)),
            pl.BlockSpec((A_QK, LANES), lambda i, fb, bb, fi, ia, ib: (0, 0)),
        ],
        out_specs=[
            pl.BlockSpec((nch, GLA_C, A_V), fmap(0)), pl.BlockSpec((nch, GLA_C, A_V), rmap(0)),
            pl.BlockSpec((2, 2, A_QK, DV_A),
                         lambda i, fb, bb, fi, ia, ib: (jnp.minimum(fb[i], BATCH // 2 - 1), 0, 0, 0)),
        ],
        scratch_shapes=[
            pltpu.VMEM((4, A_QK, DV_A), F32),
            pltpu.VMEM((4, GLA_SB * GLA_C, A_QK), BF16),
            pltpu.VMEM((4, GLA_C, A_QK), F32),
        ],
    )
    o_f, o_b, s_fin = pl.pallas_call(
        _gla_kernel,
        grid_spec=gs,
        out_shape=[
            jax.ShapeDtypeStruct((M_TOK // GLA_C, GLA_C, A_V), F32),
            jax.ShapeDtypeStruct((M_TOK // GLA_C, GLA_C, A_V), F32),
            jax.ShapeDtypeStruct((BATCH, 2, A_QK, DV_A), F32),
        ],
        compiler_params=_cparams(("arbitrary",)),
        name="gla",
    )(*tables, a3, a3, a3, g3, a3, a3, a3, g3, s0_all, s0_all, m1)
    return o_f.reshape(M_TOK, A_V), o_b.reshape(M_TOK, A_V), s_fin


def _attend(q_b, kv, out):
    scores = [_dot_nt(q_b, k_b) for k_b, _ in kv]
    yield
    top = functools.reduce(jnp.maximum, [jnp.max(s2, axis=-1, keepdims=True) for s2 in scores])
    both = None
    for s2, (_, v_ones) in zip(scores, kv):
        part = _dot(jnp.exp2(s2 - top).astype(BF16), v_ones)
        both = part if both is None else both + part
    yield
    dv = both.shape[1] // 2
    out.append(both[:, :dv] * (1.0 / both[:, dv:]))


def _seq_rows(ref):
    return [slice(r, r + SEQ) for r in range(0, ref.shape[0], SEQ)]


def _diffattn_body(q_ref, kv_refs, lq1_ref, lk1_ref, lq2_ref, lk2_ref, gsub_ref, o_ref, lam_init, seqs):
    lam = (jnp.exp(jnp.sum(lq1_ref[...] * lk1_ref[...], axis=-1, keepdims=True))
           - jnp.exp(jnp.sum(lq2_ref[...] * lk2_ref[...], axis=-1, keepdims=True)) + lam_init)
    outs, parts = [], []
    for rows in seqs:
        lane = lax.broadcasted_iota(jnp.int32, q_ref[rows, :LANES].shape, 1)
        for h in range(H_B):
            hs = slice(h * LANES, (h + 1) * LANES)
            q_h = q_ref[rows, hs]
            kv = [(k_ref[rows if n == 0 else slice(None), hs],
                   v_ref[rows if n == 0 else slice(None), 2 * h * DV_B:2 * (h + 1) * DV_B])
                  for n, (k_ref, v_ref) in enumerate(kv_refs)]
            zero = jnp.zeros_like(q_h)
            outs.append((rows, h, [], []))
            parts.append(_attend(jnp.where(lane < DH_B, q_h, zero), kv, outs[-1][2]))
            parts.append(_attend(jnp.where(lane >= DH_B, q_h, zero), kv, outs[-1][3]))
    _staged(parts)
    for rows, h, o1, o2 in outs:
        o = o1[0] - lam * o2[0]
        o_ref[rows, h * LANES:(h + 1) * LANES] = (_rms(o, gsub_ref[...]) * (1.0 - lam_init)).astype(BF16)


def _diffattn_kernel(q_ref, k_ref, v_ref, lq1, lk1, lq2, lk2, gsub, o_ref, *, lam_init):
    _diffattn_body(q_ref, [(k_ref, v_ref)], lq1, lk1, lq2, lk2, gsub, o_ref, lam_init, _seq_rows(q_ref))


def _diffattn_cached_kernel(q_ref, k_ref, v_ref, kc_ref, vc_ref, lq1, lk1, lq2, lk2, gsub, o_ref, *, lam_init):
    _diffattn_body(q_ref, [(k_ref, v_ref), (kc_ref, vc_ref)], lq1, lk1, lq2, lk2, gsub, o_ref, lam_init,
                   [slice(None)])


def _diffattn(qb, kb, vb, k_cache, v_cache, lam_params, gsub, lam_init):
    small = [pl.BlockSpec((1, DH_B), lambda *_: (0, 0))] * 4 + [pl.BlockSpec((1, DV_B), lambda *_: (0, 0))]
    body = functools.partial(_diffattn_kernel, lam_init=lam_init)
    ctx = pl.pallas_call(
        body,
        grid=(N_CTX // TMC,),
        in_specs=[_row_spec(B_QK, tm=TMC), _row_spec(B_QK, tm=TMC), _row_spec(2 * B_V, tm=TMC)] + small,
        out_specs=_row_spec(B_V, tm=TMC),
        out_shape=jax.ShapeDtypeStruct((N_CTX, B_V), BF16),
        compiler_params=_cparams(("arbitrary",)),
        name="diffattn_ctx",
    )(qb, kb, vb, *lam_params, gsub)
    nqb = DEC_SEQ // TQ_LAT
    off = N_CTX // TQ_LAT
    seq0 = N_CTX // DEC_SEQ
    lat = pl.pallas_call(
        functools.partial(_diffattn_cached_kernel, lam_init=lam_init),
        grid=(DEC_BATCH, nqb),
        in_specs=[
            pl.BlockSpec((TQ_LAT, B_QK), lambda b, j: (off + b * nqb + j, 0)),
            pl.BlockSpec((DEC_SEQ, B_QK), lambda b, j: (seq0 + b, 0)),
            pl.BlockSpec((DEC_SEQ, 2 * B_V), lambda b, j: (seq0 + b, 0)),
            pl.BlockSpec((None, PAST_LEN, B_QK), lambda b, j: (b, 0, 0)),
            pl.BlockSpec((None, PAST_LEN, 2 * B_V), lambda b, j: (b, 0, 0)),
        ] + small,
        out_specs=pl.BlockSpec((TQ_LAT, B_V), lambda b, j: (b * nqb + j, 0)),
        out_shape=jax.ShapeDtypeStruct((N_LAT, B_V), BF16),
        compiler_params=_cparams(("arbitrary", "arbitrary")),
        name="diffattn_lat",
    )(qb, kb, vb, k_cache, v_cache, *lam_params, gsub)
    return ctx, lat


def _swiglu_act(xb, wg_ref, wu_ref, act_ref):
    for lo in range(0, act_ref.shape[1], MXU_N):
        cs = slice(lo, lo + MXU_N)
        act_ref[:, cs] = (_silu(_dot(xb, wg_ref[:, cs])) * _dot(xb, wu_ref[:, cs])).astype(BF16)


def _mix_ffn_even_kernel(of_ref, ob_ref, ra_ref, actx_ref, alat_ref, xc_ref, xl_ref, g1_ref, ggla_ref, wo_ref,
                         gam_ref, sh_ref, sc_ref, g2_ref, wg_ref, wu_ref, wd_ref, o_ref, act_ref):
    is_lat = _is_lat(TM)

    def part(rows):
        heads = []
        for h in range(H_A):
            hs = slice(h * DV_A, (h + 1) * DV_A)
            heads.append((_rms(of_ref[rows, hs] + ob_ref[rows, hs], ggla_ref[...])
                          * _silu(ra_ref[rows, hs])).astype(BF16))
        mix = jnp.concatenate(heads + [jnp.where(is_lat, alat_ref[rows, :], actx_ref[rows, :])], axis=1)
        yield
        x = jnp.where(is_lat, xl_ref[rows, :], xc_ref[rows, :]) + g1_ref[...] * _dot(mix, wo_ref[...])
        hb = _norm_mod(x, gam_ref[...], sc_ref[...], sh_ref[...]).astype(BF16)
        yield
        for lo in range(0, D_FF, MXU_N):
            cs = slice(lo, lo + MXU_N)
            act_ref[rows, cs] = (_silu(_dot(hb, wg_ref[:, cs])) * _dot(hb, wu_ref[:, cs])).astype(BF16)
        yield
        o_ref[rows, :] = x + g2_ref[...] * _dot(act_ref[rows, :], wd_ref[...])

    _staged([part(slice(r, r + TM // 2)) for r in range(0, TM, TM // 2)])


def _mix_ffn_even(o_f, o_b, a_proj, attn_ctx, attn_lat, x_ctx, x_lat, mod, g_gla, w_out, gamma, w_gate, w_up, w_down):
    return pl.pallas_call(
        _mix_ffn_even_kernel,
        grid=(NB,),
        in_specs=[_pair_spec(A_V), _pair_spec(A_V), _pair_spec(A_V, 2), _ctx_spec(B_V), _lat_spec(B_V),
                  _ctx_spec(D_MODEL), _lat_spec(D_MODEL), _mod_spec(2), _const_spec((1, DV_A)),
                  _const_spec((A_V + B_V, D_MODEL)),
                  _const_spec((1, D_MODEL)), _mod_spec(3), _mod_spec(4), _mod_spec(5),
                  _const_spec((D_MODEL, D_FF)), _const_spec((D_MODEL, D_FF)), _const_spec((D_FF, D_MODEL))],
        out_specs=_row_spec(D_MODEL),
        out_shape=jax.ShapeDtypeStruct((M_TOK, D_MODEL), F32),
        scratch_shapes=[pltpu.VMEM((TM, D_FF), BF16)],
        compiler_params=_cparams(("arbitrary",)),
        name="mix_ffn_even",
    )(o_f, o_b, a_proj, attn_ctx, attn_lat, x_ctx, x_lat, mod, g_gla, w_out, gamma, mod, mod, mod,
      w_gate, w_up, w_down)


def _inproj_odd_kernel(x_ref, gam_ref, sh_ref, sc_ref, w_ref, gq_ref, gk_ref, cos_ref, sin_ref,
                       q_ref, k_ref, v_ref, kctx_ref, vctx_ref, stage_ref):
    is_lat = _is_lat(TMB)
    scale = DH_C ** -0.5 * LOG2E
    nq, nkv = H_C * DH_C, HKV_C * DH_C

    def part(rows):
        n = rows.stop - rows.start
        hb = _norm_mod(x_ref[rows, :], gam_ref[...], sc_ref[...], sh_ref[...]).astype(BF16)
        cos = jnp.where(is_lat, cos_ref[rows, :], 1.0)
        sin = jnp.where(is_lat, sin_ref[rows, :], 0.0)
        yield
        z = _dot(hb, w_ref[:, nq:nq + nkv])
        for h in range(HKV_C):
            hs = slice(h * DH_C, (h + 1) * DH_C)
            k_n = _rms(z[:, hs], gk_ref[...])
            stage_ref[rows, hs] = k_n
            k_ref[rows, hs] = _rope(k_n, cos, sin, DH_C // 4).astype(BF16)
        v = _dot(hb, w_ref[:, nq + nkv:])
        stage_ref[rows, nkv:] = v
        for h in range(HKV_C):
            v_ref[rows, 2 * h * DH_C:(2 * h + 1) * DH_C] = v[:, h * DH_C:(h + 1) * DH_C].astype(BF16)
            v_ref[rows, (2 * h + 1) * DH_C:2 * (h + 1) * DH_C] = jnp.ones((n, DH_C), BF16)
        for lo in range(0, nq, MXU_N):
            z = _dot(hb, w_ref[:, lo:lo + MXU_N])
            for j in range(0, MXU_N, DH_C):
                q_ref[rows, lo + j:lo + j + DH_C] = (
                    _rope(_rms(z[:, j:j + DH_C], gq_ref[...]), cos, sin, DH_C // 4) * scale).astype(BF16)

    _staged([part(slice(r, r + TM)) for r in range(0, TMB, TM)])
    _copy_ctx(stage_ref, ((0, DH_C, HKV_C, kctx_ref), (nkv, DH_C, HKV_C, vctx_ref)))


def _inproj_odd(x, gamma, mod, w_in, g_q, g_k, cos, sin):
    nq, nkv = H_C * DH_C, HKV_C * DH_C
    return pl.pallas_call(
        _inproj_odd_kernel,
        grid=(M_TOK // TMB,),
        in_specs=[
            _row_spec(D_MODEL, tm=TMB), _const_spec((1, D_MODEL)), _mod_spec(0, TMB), _mod_spec(1, TMB),
            _const_spec((D_MODEL, nq + 2 * nkv)), _const_spec((1, DH_C)), _const_spec((1, DH_C)),
            pl.BlockSpec((TMB, LANES), lambda i: (_pos_block(i, TMB), 0)),
            pl.BlockSpec((TMB, LANES), lambda i: (_pos_block(i, TMB), 0)),
        ],
        out_specs=[_row_spec(nq, tm=TMB), _row_spec(nkv, tm=TMB), _row_spec(2 * nkv, tm=TMB),
                   _ctx_parts_spec(HKV_C, TMB), _ctx_parts_spec(HKV_C, TMB)],
        out_shape=[
            jax.ShapeDtypeStruct((M_TOK, nq), BF16), jax.ShapeDtypeStruct((M_TOK, nkv), BF16),
            jax.ShapeDtypeStruct((M_TOK, 2 * nkv), BF16), jax.ShapeDtypeStruct((N_CTX * HKV_C, LANES), F32),
            jax.ShapeDtypeStruct((N_CTX * HKV_C, LANES), F32),
        ],
        scratch_shapes=[pltpu.VMEM((TMB, 2 * nkv), F32)],
        compiler_params=_cparams(("arbitrary",)),
        name="inproj_odd",
    )(x, gamma, mod, mod, w_in, g_q, g_k, cos, sin)


def _gqa_body(q_ref, kv_refs, o_ref, seqs):
    rep = H_C // HKV_C
    outs, parts = [], []
    for rows in seqs:
        for hk in range(HKV_C):
            ks = slice(hk * DH_C, (hk + 1) * DH_C)
            q_g = jnp.concatenate(
                [q_ref[rows, (hk * rep + g) * DH_C:(hk * rep + g + 1) * DH_C] for g in range(rep)], axis=0)
            kv = [(k_ref[rows if n == 0 else slice(None), ks],
                   v_ref[rows if n == 0 else slice(None), 2 * hk * DH_C:2 * (hk + 1) * DH_C])
                  for n, (k_ref, v_ref) in enumerate(kv_refs)]
            outs.append((rows, hk, []))
            parts.append(_attend(q_g, kv, outs[-1][2]))
    _staged(parts)
    for rows, hk, o in outs:
        tq = o[0].shape[0] // rep
        for g in range(rep):
            o_ref[rows, (hk * rep + g) * DH_C:(hk * rep + g + 1) * DH_C] = o[0][g * tq:(g + 1) * tq].astype(BF16)


def _gqa_kernel(q_ref, k_ref, v_ref, o_ref):
    _gqa_body(q_ref, [(k_ref, v_ref)], o_ref, _seq_rows(q_ref))


def _gqa_cached_kernel(q_ref, k_ref, v_ref, kc_ref, vc_ref, o_ref):
    _gqa_body(q_ref, [(k_ref, v_ref), (kc_ref, vc_ref)], o_ref, [slice(None)])


def _gqa(q, k, v, k_cache, v_cache):
    nq, nkv = H_C * DH_C, HKV_C * DH_C
    ctx = pl.pallas_call(
        _gqa_kernel,
        grid=(N_CTX // TMC,),
        in_specs=[_row_spec(nq, tm=TMC), _row_spec(nkv, tm=TMC), _row_spec(2 * nkv, tm=TMC)],
        out_specs=_row_spec(nq, tm=TMC),
        out_shape=jax.ShapeDtypeStruct((N_CTX, nq), BF16),
        compiler_params=_cparams(("arbitrary",)),
        name="gqa_ctx",
    )(q, k, v)
    nqb = DEC_SEQ // TQ_LAT
    off = N_CTX // TQ_LAT
    seq0 = N_CTX // DEC_SEQ
    lat = pl.pallas_call(
        _gqa_cached_kernel,
        grid=(DEC_BATCH, nqb),
        in_specs=[
            pl.BlockSpec((TQ_LAT, nq), lambda b, j: (off + b * nqb + j, 0)),
            pl.BlockSpec((DEC_SEQ, nkv), lambda b, j: (seq0 + b, 0)),
            pl.BlockSpec((DEC_SEQ, 2 * nkv), lambda b, j: (seq0 + b, 0)),
            pl.BlockSpec((None, PAST_LEN, nkv), lambda b, j: (b, 0, 0)),
            pl.BlockSpec((None, PAST_LEN, 2 * nkv), lambda b, j: (b, 0, 0)),
        ],
        out_specs=pl.BlockSpec((TQ_LAT, nq), lambda b, j: (b * nqb + j, 0)),
        out_shape=jax.ShapeDtypeStruct((N_LAT, nq), BF16),
        compiler_params=_cparams(("arbitrary", "arbitrary")),
        name="gqa_lat",
    )(q, k, v, k_cache, v_cache)
    return ctx, lat


def _postmix_odd_kernel(actx_ref, alat_ref, x_ref, g1_ref, wo_ref, gam_ref, sh_ref, sc_ref, wrh_ref, wrl_ref,
                        x1_ref, h_ref, hf_ref, w1_ref, w2_ref, idx_ref, cnt_ref, run0_ref, run1_ref):
    @pl.when(pl.program_id(0) == 0)
    def _():
        run0_ref[...] = jnp.zeros_like(run0_ref)
        run1_ref[...] = jnp.zeros_like(run1_ref)

    is_lat = _is_lat(TMB)
    lane = lax.broadcasted_iota(jnp.int32, (TMB, LANES), 1)
    lane_f = lane.astype(F32)
    picked = []

    def part(rows):
        attn = jnp.where(is_lat, alat_ref[rows, :], actx_ref[rows, :])
        x1 = x_ref[rows, :] + g1_ref[...] * _dot(attn, wo_ref[...])
        x1_ref[rows, :] = x1
        yield
        h = _norm_mod(x1, gam_ref[...], sc_ref[...], sh_ref[...])
        h_hi, h_lo = _split2(h)
        h_ref[rows, :] = h_hi
        hf_ref[rows, :] = h_hi.astype(F32)
        logits = _dot(h_hi, wrh_ref[...]) + (_dot(h_hi, wrl_ref[...]) + _dot(h_lo, wrh_ref[...]))
        yield
        lane_p = lax.broadcasted_iota(jnp.int32, logits.shape, 1).astype(F32)
        lg = jnp.where(lane_p < N_EXPERTS, logits, -jnp.inf)
        m1 = jnp.max(lg, axis=-1, keepdims=True)
        i1 = jnp.min(jnp.where(lg == m1, lane_p, float(LANES)), axis=-1, keepdims=True)
        lg2 = jnp.where(lane_p == i1, -jnp.inf, lg)
        m2 = jnp.max(lg2, axis=-1, keepdims=True)
        i2 = jnp.min(jnp.where(lg2 == m2, lane_p, float(LANES)), axis=-1, keepdims=True)
        e = jnp.exp(m2 - m1)
        w1 = 1.0 / (1.0 + e)
        w1_ref[rows, :] = jnp.broadcast_to(w1, logits.shape)
        w2_ref[rows, :] = jnp.broadcast_to(e * w1, logits.shape)
        picked.append((jnp.broadcast_to(i1, logits.shape), jnp.broadcast_to(i2, logits.shape)))

    _staged([part(slice(r, r + TM)) for r in range(0, TMB, TM)])
    i1 = jnp.concatenate([p[0] for p in picked], axis=0)
    i2 = jnp.concatenate([p[1] for p in picked], axis=0)
    oh1 = jnp.where(lane_f == i1, 1.0, 0.0)
    oh2 = jnp.where(lane_f == i2, 1.0, 0.0)
    r_i = lax.broadcasted_iota(jnp.int32, (TMB, TMB), 0)
    c_i = lax.broadcasted_iota(jnp.int32, (TMB, TMB), 1)
    before = jnp.where(c_i < r_i, 1.0, 0.0).astype(BF16)
    rank1 = jnp.sum((_dot(before, oh1.astype(BF16)) + run0_ref[0:1, :]) * oh1, axis=-1, keepdims=True)
    rank2 = jnp.sum((_dot(before, oh2.astype(BF16)) + run1_ref[0:1, :]) * oh2, axis=-1, keepdims=True)
    run0_ref[...] = run0_ref[...] + jnp.sum(oh1, axis=0, keepdims=True)
    run1_ref[...] = run1_ref[...] + jnp.sum(oh2, axis=0, keepdims=True)
    sub = lax.broadcasted_iota(jnp.int32, cnt_ref.shape, 0)
    cnt_ref[...] = jnp.where(sub == 0, run0_ref[...], run1_ref[...])
    quarter = LANES // 4
    cols = jnp.where(lane < quarter, i1, jnp.where(lane < 2 * quarter, i2, jnp.where(
        lane < 3 * quarter, rank1, rank2)))
    rows_t = jnp.transpose(cols)
    idx_ref[...] = jnp.concatenate(
        [rows_t[q * quarter:q * quarter + 1] for q in range(4)] + [jnp.zeros((4, TMB), F32)],
        axis=0).astype(jnp.int32)


def _postmix_odd(attn_ctx, attn_lat, x, mod, w_out, gamma, w_router):
    wr_hi = w_router.astype(BF16)
    wr_lo = (w_router - wr_hi.astype(F32)).astype(BF16)
    row = functools.partial(_row_spec, tm=TMB)
    return pl.pallas_call(
        _postmix_odd_kernel,
        grid=(M_TOK // TMB,),
        in_specs=[_ctx_spec(H_C * DH_C, TMB), _lat_spec(H_C * DH_C, TMB), row(D_MODEL), _mod_spec(2, TMB),
                  _const_spec((H_C * DH_C, D_MODEL)),
                  _const_spec((1, D_MODEL)), _mod_spec(3, TMB), _mod_spec(4, TMB), _const_spec((D_MODEL, LANES)),
                  _const_spec((D_MODEL, LANES))],
        out_specs=[row(D_MODEL), row(D_MODEL), row(D_MODEL), row(LANES), row(LANES),
                   pl.BlockSpec((8, TMB), lambda i: (0, i)),
                   _const_spec((8, LANES))],
        out_shape=[
            jax.ShapeDtypeStruct((M_TOK, D_MODEL), F32), jax.ShapeDtypeStruct((M_TOK, D_MODEL), BF16),
            jax.ShapeDtypeStruct((M_TOK, D_MODEL), F32),
            jax.ShapeDtypeStruct((M_TOK, LANES), F32), jax.ShapeDtypeStruct((M_TOK, LANES), F32),
            jax.ShapeDtypeStruct((8, M_TOK), jnp.int32), jax.ShapeDtypeStruct((8, LANES), F32),
        ],
        scratch_shapes=[pltpu.VMEM((8, LANES), F32), pltpu.VMEM((8, LANES), F32)],
        compiler_params=_cparams(("arbitrary",)),
        name="postmix_odd",
    )(attn_ctx, attn_lat, x, mod, w_out, gamma, mod, mod, wr_hi, wr_lo)


def _cast_rows(src_ref, dst_ref, piece):
    def body(r, carry):
        rows = pl.ds(pl.multiple_of(r * piece, piece), piece)
        dst_ref[rows, :] = src_ref[rows, :].astype(BF16)
        return carry

    lax.fori_loop(0, src_ref.shape[0] // piece, body, 0)


def _experts_up_kernel(tile_ref, chunk_ref, exp_ref, first_ref, valid_ref, xa_ref, xb_ref, wg_ref, wu_ref, act_ref,
                       wgb_ref, wub_ref):
    s = pl.program_id(0)

    @pl.when(first_ref[s] == 1)
    def _():
        _cast_rows(wg_ref, wgb_ref, 128)
        _cast_rows(wu_ref, wub_ref, 128)

    @pl.when(valid_ref[s] == 1)
    def _():
        x = jnp.where(tile_ref[s] >= NT_E // 2, xb_ref[...].astype(BF16), xa_ref[...])
        _swiglu_act(x, wgb_ref, wub_ref, act_ref)

    @pl.when(valid_ref[s] == 0)
    def _():
        piece = 64

        def body(r, carry):
            act_ref[pl.ds(pl.multiple_of(r * piece, piece), piece), :] = jnp.zeros((piece, TF_E), BF16)
            return carry

        lax.fori_loop(0, TME // piece, body, 0)


def _experts_down_kernel(act_tile_ref, exp_ref, first_ref, valid_ref, act_ref, wd_ref, o_ref, wdb_ref):
    i = pl.program_id(0)

    @pl.when(first_ref[i] == 1)
    def _():
        _cast_rows(wd_ref, wdb_ref, 256)

    @pl.when(valid_ref[i] == 1)
    def _():
        o_ref[...] = _dot(act_ref[...], wdb_ref[...])

    @pl.when(valid_ref[i] == 0)
    def _():
        o_ref[...] = jnp.zeros_like(o_ref)


def _experts(up_tables, down_tables, xs, w_gate, w_up, w_down):
    def w_map(s, t, c, e, f, v):
        return e[s], 0, jnp.where(v[s] == 1, c[s], NC_E - 1)

    up = pl.pallas_call(
        _experts_up_kernel,
        grid_spec=pltpu.PrefetchScalarGridSpec(
            num_scalar_prefetch=5,
            grid=(NC_E * NT_E,),
            in_specs=[
                pl.BlockSpec((TME, D_MODEL), lambda s, t, c, e, f, v: (jnp.minimum(t[s], NT_E // 2 - 1), 0)),
                pl.BlockSpec((TME, D_MODEL), lambda s, t, c, e, f, v: (jnp.maximum(t[s] - NT_E // 2, 0), 0)),
                pl.BlockSpec((None, D_MODEL, TF_E), w_map), pl.BlockSpec((None, D_MODEL, TF_E), w_map),
            ],
            out_specs=pl.BlockSpec((TME, TF_E), lambda s, t, c, e, f, v: (t[s], c[s])),
            scratch_shapes=[pltpu.VMEM((D_MODEL, TF_E), BF16), pltpu.VMEM((D_MODEL, TF_E), BF16)],
        ),
        out_shape=jax.ShapeDtypeStruct((NT_E * TME, D_FF_E), BF16),
        compiler_params=_cparams(("arbitrary",)),
        name="experts_up",
    )(*up_tables, *xs, w_gate, w_up)
    return pl.pallas_call(
        _experts_down_kernel,
        grid_spec=pltpu.PrefetchScalarGridSpec(
            num_scalar_prefetch=4,
            grid=(NT_E,),
            in_specs=[
                pl.BlockSpec((TME, D_FF_E), lambda i, a, e, f, v: (a[i], 0)),
                pl.BlockSpec((None, D_FF_E, D_MODEL), lambda i, a, e, f, v: (e[i], 0, 0)),
            ],
            out_specs=pl.BlockSpec((TME, D_MODEL), lambda i, a, e, f, v: (i, 0)),
            scratch_shapes=[pltpu.VMEM((D_FF_E, D_MODEL), BF16)],
        ),
        out_shape=jax.ShapeDtypeStruct((NT_E * TME, D_MODEL), F32),
        compiler_params=_cparams(("arbitrary",)),
        name="experts_down",
    )(*down_tables, up, w_down)


def _combine_kernel(x_ref, y1_ref, y2_ref, w1_ref, w2_ref, g2_ref, gam_ref, octx_ref, olat_ref):
    rep = D_MODEL // LANES
    w1 = jnp.concatenate([w1_ref[...]] * rep, axis=1)
    w2 = jnp.concatenate([w2_ref[...]] * rep, axis=1)
    x2 = x_ref[...] + g2_ref[...] * (w1 * y1_ref[...] + w2 * y2_ref[...])
    y = _rms(x2, gam_ref[...])
    is_lat = _is_lat(TMB)

    @pl.when(jnp.logical_not(is_lat))
    def _():
        octx_ref[...] = y

    @pl.when(is_lat)
    def _():
        olat_ref[...] = y


def _combine(x1, y1, y2, w1, w2, mod, gamma):
    return pl.pallas_call(
        _combine_kernel,
        grid=(M_TOK // TMB,),
        in_specs=[_row_spec(D_MODEL, tm=TMB), _row_spec(D_MODEL, tm=TMB), _row_spec(D_MODEL, tm=TMB),
                  _row_spec(LANES, tm=TMB), _row_spec(LANES, tm=TMB), _mod_spec(5, TMB), _const_spec((1, D_MODEL))],
        out_specs=[_ctx_spec(D_MODEL, TMB), _lat_spec(D_MODEL, TMB)],
        out_shape=[jax.ShapeDtypeStruct((N_CTX, D_MODEL), F32), jax.ShapeDtypeStruct((N_LAT, D_MODEL), F32)],
        compiler_params=_cparams(("arbitrary",)),
        name="combine",
    )(x1, y1, y2, w1, w2, mod, gamma)


def _rope_tables(rot_dim):
    n = DEC_SEQ
    rows = np.repeat(np.arange(n // GRID_W, dtype=np.float64), GRID_W)
    cols = np.tile(np.arange(GRID_W, dtype=np.float64), n // GRID_W)
    half = rot_dim // 2
    freqs = ROPE_THETA ** (-np.arange(0, half, 2, dtype=np.float64) / half)
    ang_r, ang_c = rows[:, None] * freqs, cols[:, None] * freqs
    cos = np.concatenate([np.cos(ang_r)] * 2 + [np.cos(ang_c)] * 2, axis=-1)
    sin = np.concatenate([-np.sin(ang_r), np.sin(ang_r), -np.sin(ang_c), np.sin(ang_c)], axis=-1)
    rep = LANES // rot_dim
    return (jnp.asarray(np.tile(cos, (1, rep)).astype(np.float32)),
            jnp.asarray(np.tile(sin, (1, rep)).astype(np.float32)))


def _with_ones(v):
    return jnp.concatenate([v, jnp.ones_like(v)], axis=-1)


def _rows(a, idx):
    return a.at[idx].get(mode="promise_in_bounds")


def _route(idx, counts_f):
    experts = jnp.arange(N_EXPERTS, dtype=jnp.int32)
    e2 = idx[0:2]
    first_total = counts_f[0, :N_EXPERTS].astype(jnp.int32)
    counts = first_total + counts_f[1, :N_EXPERTS].astype(jnp.int32)
    tiles = (counts + TME - 1) // TME
    tile_end = jnp.cumsum(tiles)
    tile_start = tile_end - tiles
    start = jnp.cumsum(counts) - counts

    def lookup(table, keys):
        return sum(jnp.where(keys == k, table[k], 0) for k in range(N_EXPERTS))

    rank2 = jnp.stack([idx[2], idx[3] + lookup(first_total, e2[1])], axis=0)
    pos = lookup(tile_start * TME, e2) + rank2
    tile_id = jnp.arange(NT_E, dtype=jnp.int32)
    tile_expert = jnp.minimum(jnp.sum((tile_id[:, None] >= tile_end[None, :]).astype(jnp.int32), axis=1),
                              N_EXPERTS - 1).astype(jnp.int32)
    tile_valid = (tile_id < tile_end[-1]).astype(jnp.int32)
    t_oh = tile_expert[:, None] == experts[None, :]
    t_first = jnp.sum(jnp.where(t_oh, (start - tile_start * TME)[None, :], 0), axis=1) + tile_id * TME
    t_last = jnp.sum(jnp.where(t_oh, (start + counts)[None, :], 0), axis=1)
    order = jnp.argsort(e2.reshape(-1), stable=True).astype(jnp.int32)
    n_assign = order.shape[0]
    g_idx = t_first[:, None] + jnp.arange(TME, dtype=jnp.int32)[None, :]
    live = jnp.logical_and(g_idx < t_last[:, None], tile_valid[:, None] == 1)
    src = _rows(order, jnp.clip(g_idx, 0, n_assign - 1).reshape(-1)) % M_TOK
    src_tok = jnp.where(live.reshape(-1), src, 0).astype(jnp.int32)

    n_tiles = tile_end[-1]
    last_tile = jnp.maximum(n_tiles - 1, 0)
    last_expert = jnp.sum(jnp.where(tile_id == last_tile, tile_expert, 0))
    t_tstart = jnp.sum(jnp.where(t_oh, tile_start[None, :], 0), axis=1)
    down_tables = (
        jnp.where(tile_valid == 1, tile_id, last_tile).astype(jnp.int32),
        tile_expert,
        jnp.logical_and(tile_valid == 1, tile_id == t_tstart).astype(jnp.int32),
        tile_valid,
    )
    s_id = jnp.arange(NC_E * NT_E, dtype=jnp.int32)
    s_exp = jnp.minimum(jnp.sum((s_id[:, None] >= NC_E * tile_end[None, :]).astype(jnp.int32), axis=1),
                        N_EXPERTS - 1)
    s_oh = s_exp[:, None] == experts[None, :]
    s_tstart = jnp.sum(jnp.where(s_oh, tile_start[None, :], 0), axis=1)
    s_ntile = jnp.maximum(jnp.sum(jnp.where(s_oh, tiles[None, :], 0), axis=1), 1)
    rel = s_id - NC_E * s_tstart
    s_valid = s_id < NC_E * n_tiles
    spare = s_id - NC_E * n_tiles
    up_tables = (
        jnp.where(s_valid, s_tstart + rel % s_ntile, n_tiles + spare // NC_E).astype(jnp.int32),
        jnp.where(s_valid, rel // s_ntile, spare % NC_E).astype(jnp.int32),
        jnp.where(s_valid, s_exp, last_expert).astype(jnp.int32),
        jnp.logical_and(s_valid, rel % s_ntile == 0).astype(jnp.int32),
        s_valid.astype(jnp.int32),
    )
    return src_tok, pos, up_tables, down_tables


def lambda_init(layer):
    return 0.8 - 0.6 * math.exp(-0.3 * layer)


def kernel(x_prompt, x_sample, state_a, cache_b_k, cache_b_v, cache_c_k, cache_c_v, c, c_ctx, w_mod, b_mod, norm_mix, norm_ffn, w_in_even, w_gate2_a, b_gate_a, g_gla, lam_q1, lam_k1, lam_q2, lam_k2, g_sub_b, w_out_even, w_in_odd, g_q_c, g_k_c, w_out_odd, ffn_gate, ffn_up, ffn_down, w_router, exp_gate, exp_up, exp_down, norm_final):
    x_ctx, x_lat = x_prompt.reshape(N_CTX, D_MODEL), x_sample.reshape(N_LAT, D_MODEL)
    cond =jnp.concatenate([c_ctx[None, :], c, jnp.zeros((N_COND - 1 - DEC_BATCH, D_MODEL), F32)], axis=0)
    mod = _modulation(cond, w_mod, b_mod).reshape(DEPTH, N_COND, 1, 6 * D_MODEL)

    w = w_in_even[0]
    na = 2 * A_QK + 2 * A_V
    gate_lo = na
    gate_hi = na + 2 * GATE_RANK
    col_scale = jnp.concatenate([jnp.full((A_QK,), DK_A ** -0.5, F32), jnp.ones((na - A_QK,), F32),
                                 jnp.full((B_QK,), DH_B ** -0.5, F32), jnp.ones((B_QK + B_V,), F32)])
    w_main = (jnp.concatenate([w[:, :gate_lo], w[:, gate_hi:]], axis=1) * col_scale).astype(BF16)
    w_gates = jnp.pad(w[:, gate_lo:gate_hi], ((0, 0), (0, LANES - 2 * GATE_RANK))).astype(BF16)
    g2 = jnp.zeros((LANES, 2 * A_QK), F32)
    g2 = g2.at[:GATE_RANK, :A_QK].set(w_gate2_a[0, 0]).at[GATE_RANK:2 * GATE_RANK, A_QK:].set(w_gate2_a[0, 1])
    bg = b_gate_a[0].reshape(1, 2 * A_QK)
    cos_b, sin_b = _rope_tables(DH_B)
    a_proj, glog, qb, kb, vb, kb_ctx, vb_ctx = _inproj_even(
        x_ctx, x_lat, norm_mix[0][None, :], mod[0], w_main, w_gates, g2.astype(BF16), bg, cos_b, sin_b)

    s0_all = jnp.concatenate([jnp.zeros((1, 2, A_QK, DV_A), F32),
                              state_a[:, 0].reshape(DEC_BATCH, 2, A_QK, DV_A)], axis=0)
    sel = np.arange(A_QK)[:, None] // DK_A == np.arange(LANES)[None, :] // GLA_SB
    m1 = jnp.asarray(sel.astype(np.float32)).astype(BF16)
    o_f, o_b, s_fin = _gla(a_proj, glog, s0_all, m1)

    k_cache = cache_b_k[:, 0].reshape(DEC_BATCH, PAST_LEN, B_QK).astype(BF16)
    v_cache = _with_ones(cache_b_v[:, 0].astype(BF16)).reshape(DEC_BATCH, PAST_LEN, 2 * B_V)
    lam_params = [p[0][None, :] for p in (lam_q1, lam_k1, lam_q2, lam_k2)]
    attn_b = _diffattn(qb, kb, vb, k_cache, v_cache, lam_params, g_sub_b[0][None, :], lambda_init(0))

    x = _mix_ffn_even(o_f, o_b, a_proj, *attn_b, x_ctx, x_lat, mod[0], g_gla[0][None, :],
                      w_out_even[0].astype(BF16), norm_ffn[0][None, :], ffn_gate[0].astype(BF16),
                      ffn_up[0].astype(BF16), ffn_down[0].astype(BF16))

    cos_c, sin_c = _rope_tables(DH_C)
    q_c, k_c, v_c, kc_ctx, vc_ctx = _inproj_odd(x, norm_mix[1][None, :], mod[1], w_in_odd[0].astype(BF16),
                                     g_q_c[0][None, :], g_k_c[0][None, :], cos_c, sin_c)
    nkv = HKV_C * DH_C
    k_cache = cache_c_k[:, 0].reshape(DEC_BATCH, PAST_LEN, nkv).astype(BF16)
    v_cache = _with_ones(cache_c_v[:, 0].astype(BF16)).reshape(DEC_BATCH, PAST_LEN, 2 * nkv)
    attn_c = _gqa(q_c, k_c, v_c, k_cache, v_cache)
    w_r = jnp.pad(w_router[0], ((0, 0), (0, LANES - N_EXPERTS)))
    x1, h_moe, h_f32, w1, w2, idx, counts =_postmix_odd(*attn_c, x, mod[1], w_out_odd[0].astype(BF16),
                                                  norm_ffn[1][None, :], w_r)

    src_tok, pos, up_tables, down_tables = _route(idx, counts)
    half = NT_E * TME // 2
    xs = (_rows(h_moe, src_tok[:half]), _rows(h_f32, src_tok[half:]))
    ys = _experts(up_tables, down_tables, xs, exp_gate[0], exp_up[0], exp_down[0])
    y1 = _rows(ys, pos[0])
    y2 = _rows(ys, pos[1])
    y_ctx, y_lat = _combine(x1, y1, y2, w1, w2, mod[1], norm_final[None, :])

    y_prompt = y_ctx.reshape(BATCH, SEQ, D_MODEL)
    y_sample = y_lat.reshape(DEC_BATCH, DEC_SEQ, D_MODEL)
    new_state_a = s_fin.reshape(BATCH, 1, 2, H_A, DK_A, DV_A)
    new_b_k = jnp.transpose(kb_ctx.reshape(BATCH, 1, H_B, 2, DH_B, SEQ), (0, 1, 5, 2, 3, 4))
    new_b_v = vb_ctx.reshape(BATCH, 1, SEQ, H_B, DV_B)
    new_c_k = kc_ctx.reshape(BATCH, 1, SEQ, HKV_C, DH_C)
    new_c_v = vc_ctx.reshape(BATCH, 1, SEQ, HKV_C, DH_C)
    return (y_prompt, y_sample, new_state_a, new_b_k, new_b_v, new_c_k, new_c_v)
```

```python
import functools
import math

import numpy as np
import jax
import jax.numpy as jnp
from jax import lax
from jax.experimental import pallas as pl
from jax.experimental.pallas import tpu as pltpu

D_MODEL = 1024
BATCH = 32
SEQ = 256
DEPTH = 2
DEC_BATCH = 2
DEC_SEQ = 2048
PAST_LEN = 512
GRID_W = 64
H_A, DK_A, DV_A = 4, 64, 128
GATE_RANK = 16
GATE_TAU = 16.0
H_B, DH_B = 4, 64
DV_B = 2 * DH_B
H_C, HKV_C, DH_C = 8, 2, 128
D_FF = 2816
N_EXPERTS = 8
D_FF_E = 3584
ROPE_THETA = 10000.0
EPS = 1e-6
A_QK = H_A * DK_A
A_V = H_A * DV_A
B_QK = H_B * 2 * DH_B
B_V = H_B * DV_B

F32 = jnp.float32
BF16 = jnp.bfloat16
LOG2E = math.log2(math.e)

V7X_VMEM_BYTES = 64 * 1024 * 1024
VMEM_LIMIT = V7X_VMEM_BYTES * 7 // 8
LANES = 128
MXU_N = 256
TM = 256
N_CTX = BATCH * SEQ
N_LAT = DEC_BATCH * DEC_SEQ
M_TOK = N_CTX + N_LAT
NB = M_TOK // TM
NCB = N_CTX // TM
LBB = DEC_SEQ // TM
N_COND = 8
GLA_C = 64
GLA_SB = 8
TME = 512
NT_E = 2 * M_TOK // TME + N_EXPERTS
NC_E = 2
TF_E = D_FF_E // NC_E
TQ_LAT = 256
TMC = 4 * SEQ
TMB = 512

assert SEQ == TM and DEC_SEQ % TM == 0 and DEPTH == 2


def _cparams(sem):
    return pltpu.CompilerParams(dimension_semantics=sem, vmem_limit_bytes=VMEM_LIMIT)


def _group(i, tm=TM):
    return jnp.where(i < N_CTX // tm, 0, 1 + (i - N_CTX // tm) // (DEC_SEQ // tm))


def _pos_block(i, tm=TM):
    return jnp.where(i < N_CTX // tm, 0, (i - N_CTX // tm) % (DEC_SEQ // tm))


def _mod_spec(chunk, tm=TM):
    return pl.BlockSpec((None, 1, D_MODEL), lambda i, c=chunk: (_group(i, tm), 0, c))


def _row_spec(width, col=0, tm=TM):
    return pl.BlockSpec((tm, width), lambda i, c=col: (i, c))


def _ctx_spec(width, tm=TM):
    return pl.BlockSpec((tm, width), lambda i: (jnp.minimum(i, N_CTX // tm - 1), 0))


def _ctx_parts_spec(parts, tm=TM):
    return pl.BlockSpec((tm * parts, LANES), lambda i: (jnp.minimum(i, N_CTX // tm - 1), 0))


def _lat_spec(width, tm=TM):
    return pl.BlockSpec((tm, width), lambda i: (jnp.maximum(i - N_CTX // tm, 0), 0))


def _const_spec(shape):
    nd = len(shape)
    return pl.BlockSpec(shape, lambda i, nd=nd: (0,) * nd, pipeline_mode=pl.Buffered(1))


def _dot(a, b):
    return jnp.dot(a, b, preferred_element_type=F32)


def _dot_nt(a, b):
    return lax.dot_general(a, b, (((1,), (1,)), ((), ())), preferred_element_type=F32)


def _split2(a):
    hi = a.astype(BF16)
    lo = (a - hi.astype(F32)).astype(BF16)
    return hi, lo


def _dot_hi(a, b):
    a_hi, a_lo = _split2(a)
    b_hi, b_lo = _split2(b)
    return _dot(a_hi, b_hi) + (_dot(a_hi, b_lo) + _dot(a_lo, b_hi))


def _silu(x):
    return (0.5 * x) * (1.0 + jnp.tanh(0.5 * x))


def _log_sigmoid(x):
    return jnp.minimum(x, 0.0) - jnp.log(1.0 + jnp.exp(-jnp.abs(x)))


def _rms(x, g):
    return x * lax.rsqrt(jnp.mean(x * x, axis=-1, keepdims=True) + EPS) * g


def _norm_mod(x, gamma, sc, sh):
    return _rms(x, gamma) * (1.0 + sc) + sh


def _staged(parts):
    live = list(parts)
    while live:
        live = [g for g in live if next(g, True) is None]


def _copy_ctx_transposed(stage3_ref, dst_ref):
    @pl.when(pl.program_id(0) < NCB)
    def _():
        def body(j, carry):
            dst_ref[pl.ds(pl.multiple_of(j * LANES, LANES), LANES), :] = jnp.transpose(stage3_ref[j])
            return carry

        lax.fori_loop(0, stage3_ref.shape[0], body, 0)


def _is_lat(tm):
    return pl.program_id(0) >= N_CTX // tm


def _pick_rows(ctx_ref, lat_ref):
    return jnp.where(_is_lat(ctx_ref.shape[0]), lat_ref[...], ctx_ref[...])


def _copy_ctx(stage_ref, outs):
    piece = 64

    @pl.when(jnp.logical_not(_is_lat(stage_ref.shape[0])))
    def _():
        def body(r, carry):
            rows = pl.ds(pl.multiple_of(r * piece, piece), piece)
            for lo, width, parts, dst_ref in outs:
                for p in range(parts):
                    val = stage_ref[rows, lo + p * width:lo + (p + 1) * width]
                    if width < LANES:
                        val = jnp.concatenate([val, jnp.zeros((piece, LANES - width), F32)], axis=1)
                    dst_ref[pl.ds(pl.multiple_of(r * piece * parts, piece * parts) + p, piece, stride=parts), :] = val
            return carry

        lax.fori_loop(0, stage_ref.shape[0] // piece, body, 0)


def _rope(x, cos, sin, half):
    lane = lax.broadcasted_iota(jnp.int32, x.shape, 1)
    first = (lane % (2 * half)) < half
    swapped = jnp.where(first, pltpu.roll(x, LANES - half, 1), pltpu.roll(x, half, 1))
    return x * cos + swapped * sin


def _mod_kernel(cond_ref, w_ref, b_ref, o_ref):
    o_ref[...] = _dot_hi(_silu(cond_ref[...]), w_ref[...]) + b_ref[...]


def _modulation(cond, w_mod, b_mod):
    tn = 1536
    return pl.pallas_call(
        _mod_kernel,
        grid=(DEPTH, 6 * D_MODEL // tn),
        in_specs=[
            pl.BlockSpec((N_COND, D_MODEL), lambda l, j: (0, 0)),
            pl.BlockSpec((None, D_MODEL, tn), lambda l, j: (l, 0, j)),
            pl.BlockSpec((None, 1, tn), lambda l, j: (l, 0, j)),
        ],
        out_specs=pl.BlockSpec((None, N_COND, tn), lambda l, j: (l, 0, j)),
        out_shape=jax.ShapeDtypeStruct((DEPTH, N_COND, 6 * D_MODEL), F32),
        compiler_params=_cparams(("arbitrary", "arbitrary")),
        name="modulation",
    )(cond, w_mod, b_mod.reshape(DEPTH, 1, 6 * D_MODEL))


def _inproj_even_kernel(xc_ref, xl_ref, gam_ref, sh_ref, sc_ref, w_ref, wg_ref, g2_ref, bg_ref, cos_ref, sin_ref,
                        a_ref, glog_ref, qb_ref, kb_ref, vb_ref, kctx_ref, vctx_ref, stage_ref, stagek_ref):
    is_lat = pl.program_id(0) >= NCB
    na = 2 * A_QK + 2 * A_V

    def part(rows):
        n = rows.stop - rows.start
        x = jnp.where(is_lat, xl_ref[rows, :], xc_ref[rows, :])
        hb = _norm_mod(x, gam_ref[...], sc_ref[...], sh_ref[...]).astype(BF16)
        cos = jnp.where(is_lat, cos_ref[rows, :], 1.0)
        sin = jnp.where(is_lat, sin_ref[rows, :], 0.0)
        yield
        for lo in range(0, na, MXU_N):
            a_ref[rows, lo:lo + MXU_N] = _dot(hb, w_ref[:, lo:lo + MXU_N])
        for lo in range(0, B_QK, MXU_N):
            zq = _dot(hb, w_ref[:, na + lo:na + lo + MXU_N])
            zk = _dot(hb, w_ref[:, na + B_QK + lo:na + B_QK + lo + MXU_N])
            zv = _dot(hb, w_ref[:, na + 2 * B_QK + lo:na + 2 * B_QK + lo + MXU_N])
            stage_ref[rows, lo:lo + MXU_N] = zv
            for j in range(0, MXU_N, LANES):
                cs = slice(lo + j, lo + j + LANES)
                vb_ref[rows, 2 * (lo + j):2 * (lo + j) + DV_B] = zv[:, j:j + LANES].astype(BF16)
                vb_ref[rows, 2 * (lo + j) + DV_B:2 * (lo + j + LANES)] = jnp.ones((n, DV_B), BF16)
                stagek_ref[(lo + j) // LANES, rows, :] = zk[:, j:j + LANES]
                qb_ref[rows, cs] = (_rope(zq[:, j:j + LANES], cos, sin, DH_B // 4) * LOG2E).astype(BF16)
                kb_ref[rows, cs] = _rope(zk[:, j:j + LANES], cos, sin, DH_B // 4).astype(BF16)
        gates = _dot(hb, wg_ref[...])
        xg = _dot(gates.astype(BF16), g2_ref[...]) + bg_ref[...]
        glog_ref[rows, :] = _log_sigmoid(xg) * (1.0 / GATE_TAU)

    _staged([part(slice(0, TM))])
    _copy_ctx(stage_ref, ((0, DV_B, H_B, vctx_ref),))
    _copy_ctx_transposed(stagek_ref, kctx_ref)


def _inproj_even(x_ctx, x_lat, gamma, mod, w_main, w_gates, g2, bg, cos, sin):
    na = 2 * A_QK + 2 * A_V
    nz = na + 2 * B_QK + B_V
    return pl.pallas_call(
        _inproj_even_kernel,
        grid=(NB,),
        in_specs=[
            _ctx_spec(D_MODEL), _lat_spec(D_MODEL), _const_spec((1, D_MODEL)), _mod_spec(0), _mod_spec(1),
            _const_spec((D_MODEL, nz)), _const_spec((D_MODEL, LANES)), _const_spec((LANES, 2 * A_QK)),
            _const_spec((1, 2 * A_QK)),
            pl.BlockSpec((TM, LANES), lambda i: (_pos_block(i), 0)),
            pl.BlockSpec((TM, LANES), lambda i: (_pos_block(i), 0)),
        ],
        out_specs=[_pair_spec(na), _pair_spec(2 * A_QK), _row_spec(B_QK), _row_spec(B_QK), _row_spec(2 * B_V),
                   pl.BlockSpec((None, B_QK, TM), lambda i: (jnp.minimum(i, NCB - 1), 0, 0)), _ctx_parts_spec(H_B)],
        out_shape=[
            jax.ShapeDtypeStruct((M_TOK, na), F32), jax.ShapeDtypeStruct((M_TOK, 2 * A_QK), F32),
            jax.ShapeDtypeStruct((M_TOK, B_QK), BF16), jax.ShapeDtypeStruct((M_TOK, B_QK), BF16),
            jax.ShapeDtypeStruct((M_TOK, 2 * B_V), BF16), jax.ShapeDtypeStruct((BATCH, B_QK, SEQ), F32),
            jax.ShapeDtypeStruct((N_CTX * H_B, LANES), F32),
        ],
        scratch_shapes=[pltpu.VMEM((TM, B_V), F32), pltpu.VMEM((B_QK // LANES, TM, LANES), F32)],
        compiler_params=_cparams(("arbitrary",)),
        name="inproj_even",
    )(x_ctx, x_lat, gamma, mod, mod, w_main, w_gates, g2, bg, cos, sin)


def _gla_chunk(q_ref, k_ref, v_ref, g_ref, o_ref, c, s_ref, a_ref, b_ref, m1_ref, head_mask, same_block, rev):
    q, k, v = q_ref[c], k_ref[c], v_ref[c]
    g = g_ref[c] * LOG2E
    v_b = v.astype(BF16)
    r_i = lax.broadcasted_iota(jnp.int32, (GLA_C, GLA_C), 0)
    c_i = lax.broadcasted_iota(jnp.int32, (GLA_C, GLA_C), 1)
    tri = jnp.where((c_i >= r_i) if rev else (c_i <= r_i), 1.0, 0.0).astype(BF16)
    g_hi = g.astype(BF16)
    g_r = g - g_hi.astype(F32)
    g_mid = g_r.astype(BF16)
    g_lo = (g_r - g_mid.astype(F32)).astype(BF16)
    bc = _dot(tri, g_hi) + (_dot(tri, g_mid) + _dot(tri, g_lo))
    b_ref[...] = bc
    yield
    t_i = lax.broadcasted_iota(jnp.int32, (GLA_SB, A_QK), 0)
    group = 2 * GLA_SB
    for i0 in range(0, GLA_C, group):
        blocks = range(i0, i0 + group, GLA_SB)
        q_i = [q_ref[c, i:i + GLA_SB, :] for i in blocks]
        b_i = [b_ref[i:i + GLA_SB, :] for i in blocks]
        for j in range(GLA_SB):
            keep = (t_i <= j) if rev else (t_i >= j)
            pair = []
            for n, i in enumerate(blocks):
                k_s = k_ref[c, i + j:i + j + 1, :]
                b_s = b_ref[i + j:i + j + 1, :]
                pair.append(jnp.where(keep, q_i[n] * k_s * jnp.exp2(b_i[n] - b_s), 0.0))
            a_ref[j * GLA_C + i0:j * GLA_C + i0 + group, :] = jnp.concatenate(pair, axis=0).astype(BF16)
    yield
    last = bc[0:1] if rev else bc[GLA_C - 1:GLA_C]
    qc = q * jnp.exp2(bc)
    kl = k * jnp.exp2(last - bc)
    row = lax.broadcasted_iota(jnp.int32, (GLA_C, A_QK), 0)

    def expand(x):
        return jnp.where(head_mask, jnp.concatenate([x] * H_A, axis=0), 0.0).astype(BF16)

    scores = None
    half = GLA_C // 2
    while half >= GLA_SB:
        mids = [bc[b + half:b + half + 1] if rev else bc[b + half - 1:b + half] for b in range(0, GLA_C, 2 * half)]
        ref = mids[0] if len(mids) == 1 else jnp.concatenate(
            [jnp.broadcast_to(m, (2 * half, A_QK)) for m in mids], axis=0)
        upper = (row // half) % 2 == 1
        later, earlier = (jnp.logical_not(upper), upper) if rev else (upper, jnp.logical_not(upper))
        q_l = jnp.where(later, q * jnp.exp2(bc - ref), 0.0)
        k_e = jnp.where(earlier, k * jnp.exp2(ref - bc), 0.0).astype(BF16)
        term = _dot_nt(expand(q_l), k_e)
        if 2 * half < GLA_C:
            term = jnp.where(same_block[2 * half], term, 0.0)
        scores = term if scores is None else scores + term
        half //= 2
    s_old = s_ref[...]
    inter = _dot(expand(qc), s_old.astype(BF16))
    sums = _dot(a_ref[...], m1_ref[...])
    yield
    lane_c = lax.broadcasted_iota(jnp.int32, (GLA_C, LANES), 1)
    within = None
    for j in range(GLA_SB):
        part = jnp.where(lane_c % GLA_SB == j, sums[j * GLA_C:(j + 1) * GLA_C], 0.0)
        within = part if within is None else within + part
    lane = lax.broadcasted_iota(jnp.int32, (GLA_SB, LANES), 1)
    per_head = []
    for h in range(H_A):
        rows_h = []
        for i in range(0, GLA_C, GLA_SB):
            moved = pltpu.roll(within[i:i + GLA_SB], (i - h * GLA_SB) % LANES, 1)
            rows_h.append(jnp.where(lane // GLA_SB == i // GLA_SB, moved, 0.0))
        per_head.append(jnp.concatenate(rows_h, axis=0)[:, :GLA_C])
    scores = scores + jnp.concatenate(per_head, axis=0)
    o_heads = []
    for h in range(H_A):
        hs = slice(h * GLA_C, (h + 1) * GLA_C)
        o_heads.append(inter[hs] + _dot(scores[hs].astype(BF16), v_b[:, h * DV_A:(h + 1) * DV_A]))
    o_ref[c] = jnp.concatenate(o_heads, axis=1)
    yield
    t = jnp.transpose(jnp.concatenate([kl, jnp.broadcast_to(last, (GLA_C, A_QK))], axis=0))
    kv = _dot(t[:, :GLA_C].astype(BF16), v_b)
    a_col = jnp.exp2(t[:, GLA_C:GLA_C + 1])
    for h in range(H_A):
        hs = slice(h * DK_A, (h + 1) * DK_A)
        s_ref[hs, :] = a_col[hs] * s_old[hs] + kv[hs, h * DV_A:(h + 1) * DV_A]


def _gla_kernel(fblk_ref, bblk_ref, first_ref, inita_ref, initb_ref,
                qf_ref, kf_ref, vf_ref, gf_ref, qr_ref, kr_ref, vr_ref, gr_ref, s0a_ref, s0b_ref, m1_ref,
                of_ref, ob_ref, sfin_ref, s_refs, a_refs, b_refs):
    step = pl.program_id(0)

    @pl.when(first_ref[step] == 1)
    def _():
        for seq, s0_ref in enumerate((s0a_ref, s0b_ref)):
            for d in range(2):
                s_refs[2 * seq + d] = s0_ref[d]

    r_h = lax.broadcasted_iota(jnp.int32, (H_A * GLA_C, A_QK), 0) // GLA_C
    c_h = lax.broadcasted_iota(jnp.int32, (H_A * GLA_C, A_QK), 1) // DK_A
    head_mask = r_h == c_h
    t_s = lax.broadcasted_iota(jnp.int32, (H_A * GLA_C, GLA_C), 0) % GLA_C
    s_s = lax.broadcasted_iota(jnp.int32, (H_A * GLA_C, GLA_C), 1)
    same_block = {}
    size = 2 * GLA_SB
    while size < GLA_C:
        same_block[size] = (t_s // size) == (s_s // size)
        size *= 2
    nch = TM // GLA_C

    def body(c, carry):
        chains = []
        for seq in range(2):
            chains.append(_gla_chunk(qf_ref, kf_ref, vf_ref, gf_ref, of_ref, seq * nch + c, s_refs.at[2 * seq],
                                     a_refs.at[2 * seq], b_refs.at[2 * seq], m1_ref, head_mask, same_block, False))
            chains.append(_gla_chunk(qr_ref, kr_ref, vr_ref, gr_ref, ob_ref, seq * nch + nch - 1 - c,
                                     s_refs.at[2 * seq + 1], a_refs.at[2 * seq + 1], b_refs.at[2 * seq + 1], m1_ref,
                                     head_mask, same_block, True))
        _staged(chains)
        return carry

    lax.fori_loop(0, nch, body, 0)

    @pl.when(step < BATCH // 2)
    def _():
        for seq in range(2):
            for d in range(2):
                sfin_ref[seq, d] = s_refs[2 * seq + d]


def _pair_block(i):
    j = i - NCB
    return jnp.where(i < NCB, i, NCB + 2 * (j % LBB) + j // LBB)


def _pair_spec(width, col=0):
    return pl.BlockSpec((TM, width), lambda i, c=col: (_pair_block(i), c))


def _gla_tables():
    fblk, bblk, first, init_a, init_b = [], [], [], [], []
    for p in range(BATCH // 2):
        fblk.append(p), bblk.append(p), first.append(1), init_a.append(0), init_b.append(0)
    for j in range(LBB):
        fblk.append(NCB // 2 + j)
        bblk.append(NCB // 2 + LBB - 1 - j)
        first.append(1 if j == 0 else 0)
        init_a.append(1), init_b.append(2)
    return [jnp.asarray(np.array(t, np.int32)) for t in (fblk, bblk, first, init_a, init_b)]


def _gla(a_proj, glog, s0_all, m1):
    tables = _gla_tables()
    nsteps = int(tables[0].shape[0])
    nseq = BATCH + DEC_BATCH

    def fmap(col):
        return lambda i, fb, bb, fi, ia, ib: (fb[i], 0, col)

    def rmap(col):
        return lambda i, fb, bb, fi, ia, ib: (bb[i], 0, col)

    nch = 2 * TM // GLA_C
    a3 = a_proj.reshape(M_TOK // GLA_C, GLA_C, a_proj.shape[1])
    g3 = glog.reshape(M_TOK // GLA_C, GLA_C, glog.shape[1])
    gs = pltpu.PrefetchScalarGridSpec(
        num_scalar_prefetch=5,
        grid=(nsteps,),
        in_specs=[
            pl.BlockSpec((nch, GLA_C, A_QK), fmap(0)), pl.BlockSpec((nch, GLA_C, A_QK), fmap(1)),
            pl.BlockSpec((nch, GLA_C, A_V), fmap(1)), pl.BlockSpec((nch, GLA_C, A_QK), fmap(0)),
            pl.BlockSpec((nch, GLA_C, A_QK), rmap(0)), pl.BlockSpec((nch, GLA_C, A_QK), rmap(1)),
            pl.BlockSpec((nch, GLA_C, A_V), rmap(1)), pl.BlockSpec((nch, GLA_C, A_QK), rmap(1)),
            pl.BlockSpec((None, 2, A_QK, DV_A), lambda i, fb, bb, fi, ia, ib: (ia[i], 0, 0, 0)),
            pl.BlockSpec((None, 2, A_QK, DV_A), lambda i, fb, bb, fi, ia, ib: (ib[i], 0, 0, 0)),
            pl.BlockSpec((A_QK, LANES), lambda i, fb, bb, fi, ia, ib: (0, 0)),
        ],
        out_specs=[
            pl.BlockSpec((nch, GLA_C, A_V), fmap(0)), pl.BlockSpec((nch, GLA_C, A_V), rmap(0)),
            pl.BlockSpec((2, 2, A_QK, DV_A),
                         lambda i, fb, bb, fi, ia, ib: (jnp.minimum(fb[i], BATCH // 2 - 1), 0, 0, 0)),
        ],
        scratch_shapes=[
            pltpu.VMEM((4, A_QK, DV_A), F32),
            pltpu.VMEM((4, GLA_SB * GLA_C, A_QK), BF16),
            pltpu.VMEM((4, GLA_C, A_QK), F32),
        ],
    )
    o_f, o_b, s_fin = pl.pallas_call(
        _gla_kernel,
        grid_spec=gs,
        out_shape=[
            jax.ShapeDtypeStruct((M_TOK // GLA_C, GLA_C, A_V), F32),
            jax.ShapeDtypeStruct((M_TOK // GLA_C, GLA_C, A_V), F32),
            jax.ShapeDtypeStruct((BATCH, 2, A_QK, DV_A), F32),
        ],
        compiler_params=_cparams(("arbitrary",)),
        name="gla",
    )(*tables, a3, a3, a3, g3, a3, a3, a3, g3, s0_all, s0_all, m1)
    return o_f.reshape(M_TOK, A_V), o_b.reshape(M_TOK, A_V), s_fin


def _attend(q_b, kv, out):
    scores = [_dot_nt(q_b, k_b) for k_b, _ in kv]
    yield
    top = functools.reduce(jnp.maximum, [jnp.max(s2, axis=-1, keepdims=True) for s2 in scores])
    both = None
    for s2, (_, v_ones) in zip(scores, kv):
        part = _dot(jnp.exp2(s2 - top).astype(BF16), v_ones)
        both = part if both is None else both + part
    yield
    dv = both.shape[1] // 2
    out.append(both[:, :dv] * (1.0 / both[:, dv:]))


def _seq_rows(ref):
    return [slice(r, r + SEQ) for r in range(0, ref.shape[0], SEQ)]


def _diffattn_body(q_ref, kv_refs, lq1_ref, lk1_ref, lq2_ref, lk2_ref, gsub_ref, o_ref, lam_init, seqs):
    lam = (jnp.exp(jnp.sum(lq1_ref[...] * lk1_ref[...], axis=-1, keepdims=True))
           - jnp.exp(jnp.sum(lq2_ref[...] * lk2_ref[...], axis=-1, keepdims=True)) + lam_init)
    outs, parts = [], []
    for rows in seqs:
        lane = lax.broadcasted_iota(jnp.int32, q_ref[rows, :LANES].shape, 1)
        for h in range(H_B):
            hs = slice(h * LANES, (h + 1) * LANES)
            q_h = q_ref[rows, hs]
            kv = [(k_ref[rows if n == 0 else slice(None), hs],
                   v_ref[rows if n == 0 else slice(None), 2 * h * DV_B:2 * (h + 1) * DV_B])
                  for n, (k_ref, v_ref) in enumerate(kv_refs)]
            zero = jnp.zeros_like(q_h)
            outs.append((rows, h, [], []))
            parts.append(_attend(jnp.where(lane < DH_B, q_h, zero), kv, outs[-1][2]))
            parts.append(_attend(jnp.where(lane >= DH_B, q_h, zero), kv, outs[-1][3]))
    _staged(parts)
    for rows, h, o1, o2 in outs:
        o = o1[0] - lam * o2[0]
        o_ref[rows, h * LANES:(h + 1) * LANES] = (_rms(o, gsub_ref[...]) * (1.0 - lam_init)).astype(BF16)


def _diffattn_kernel(q_ref, k_ref, v_ref, lq1, lk1, lq2, lk2, gsub, o_ref, *, lam_init):
    _diffattn_body(q_ref, [(k_ref, v_ref)], lq1, lk1, lq2, lk2, gsub, o_ref, lam_init, _seq_rows(q_ref))


def _diffattn_cached_kernel(q_ref, k_ref, v_ref, kc_ref, vc_ref, lq1, lk1, lq2, lk2, gsub, o_ref, *, lam_init):
    _diffattn_body(q_ref, [(k_ref, v_ref), (kc_ref, vc_ref)], lq1, lk1, lq2, lk2, gsub, o_ref, lam_init,
                   [slice(None)])


def _diffattn(qb, kb, vb, k_cache, v_cache, lam_params, gsub, lam_init):
    small = [pl.BlockSpec((1, DH_B), lambda *_: (0, 0))] * 4 + [pl.BlockSpec((1, DV_B), lambda *_: (0, 0))]
    body = functools.partial(_diffattn_kernel, lam_init=lam_init)
    ctx = pl.pallas_call(
        body,
        grid=(N_CTX // TMC,),
        in_specs=[_row_spec(B_QK, tm=TMC), _row_spec(B_QK, tm=TMC), _row_spec(2 * B_V, tm=TMC)] + small,
        out_specs=_row_spec(B_V, tm=TMC),
        out_shape=jax.ShapeDtypeStruct((N_CTX, B_V), BF16),
        compiler_params=_cparams(("arbitrary",)),
        name="diffattn_ctx",
    )(qb, kb, vb, *lam_params, gsub)
    nqb = DEC_SEQ // TQ_LAT
    off = N_CTX // TQ_LAT
    seq0 = N_CTX // DEC_SEQ
    lat = pl.pallas_call(
        functools.partial(_diffattn_cached_kernel, lam_init=lam_init),
        grid=(DEC_BATCH, nqb),
        in_specs=[
            pl.BlockSpec((TQ_LAT, B_QK), lambda b, j: (off + b * nqb + j, 0)),
            pl.BlockSpec((DEC_SEQ, B_QK), lambda b, j: (seq0 + b, 0)),
            pl.BlockSpec((DEC_SEQ, 2 * B_V), lambda b, j: (seq0 + b, 0)),
            pl.BlockSpec((None, PAST_LEN, B_QK), lambda b, j: (b, 0, 0)),
            pl.BlockSpec((None, PAST_LEN, 2 * B_V), lambda b, j: (b, 0, 0)),
        ] + small,
        out_specs=pl.BlockSpec((TQ_LAT, B_V), lambda b, j: (b * nqb + j, 0)),
        out_shape=jax.ShapeDtypeStruct((N_LAT, B_V), BF16),
        compiler_params=_cparams(("arbitrary", "arbitrary")),
        name="diffattn_lat",
    )(qb, kb, vb, k_cache, v_cache, *lam_params, gsub)
    return ctx, lat


def _swiglu_act(xb, wg_ref, wu_ref, act_ref):
    for lo in range(0, act_ref.shape[1], MXU_N):
        cs = slice(lo, lo + MXU_N)
        act_ref[:, cs] = (_silu(_dot(xb, wg_ref[:, cs])) * _dot(xb, wu_ref[:, cs])).astype(BF16)


def _mix_ffn_even_kernel(of_ref, ob_ref, ra_ref, actx_ref, alat_ref, xc_ref, xl_ref, g1_ref, ggla_ref, wo_ref,
                         gam_ref, sh_ref, sc_ref, g2_ref, wg_ref, wu_ref, wd_ref, o_ref, act_ref):
    is_lat = _is_lat(TM)

    def part(rows):
        heads = []
        for h in range(H_A):
            hs = slice(h * DV_A, (h + 1) * DV_A)
            heads.append((_rms(of_ref[rows, hs] + ob_ref[rows, hs], ggla_ref[...])
                          * _silu(ra_ref[rows, hs])).astype(BF16))
        mix = jnp.concatenate(heads + [jnp.where(is_lat, alat_ref[rows, :], actx_ref[rows, :])], axis=1)
        yield
        x = jnp.where(is_lat, xl_ref[rows, :], xc_ref[rows, :]) + g1_ref[...] * _dot(mix, wo_ref[...])
        hb = _norm_mod(x, gam_ref[...], sc_ref[...], sh_ref[...]).astype(BF16)
        yield
        for lo in range(0, D_FF, MXU_N):
            cs = slice(lo, lo + MXU_N)
            act_ref[rows, cs] = (_silu(_dot(hb, wg_ref[:, cs])) * _dot(hb, wu_ref[:, cs])).astype(BF16)
        yield
        o_ref[rows, :] = x + g2_ref[...] * _dot(act_ref[rows, :], wd_ref[...])

    _staged([part(slice(r, r + TM // 2)) for r in range(0, TM, TM // 2)])


def _mix_ffn_even(o_f, o_b, a_proj, attn_ctx, attn_lat, x_ctx, x_lat, mod, g_gla, w_out, gamma, w_gate, w_up, w_down):
    return pl.pallas_call(
        _mix_ffn_even_kernel,
        grid=(NB,),
        in_specs=[_pair_spec(A_V), _pair_spec(A_V), _pair_spec(A_V, 2), _ctx_spec(B_V), _lat_spec(B_V),
                  _ctx_spec(D_MODEL), _lat_spec(D_MODEL), _mod_spec(2), _const_spec((1, DV_A)),
                  _const_spec((A_V + B_V, D_MODEL)),
                  _const_spec((1, D_MODEL)), _mod_spec(3), _mod_spec(4), _mod_spec(5),
                  _const_spec((D_MODEL, D_FF)), _const_spec((D_MODEL, D_FF)), _const_spec((D_FF, D_MODEL))],
        out_specs=_row_spec(D_MODEL),
        out_shape=jax.ShapeDtypeStruct((M_TOK, D_MODEL), F32),
        scratch_shapes=[pltpu.VMEM((TM, D_FF), BF16)],
        compiler_params=_cparams(("arbitrary",)),
        name="mix_ffn_even",
    )(o_f, o_b, a_proj, attn_ctx, attn_lat, x_ctx, x_lat, mod, g_gla, w_out, gamma, mod, mod, mod,
      w_gate, w_up, w_down)


def _inproj_odd_kernel(x_ref, gam_ref, sh_ref, sc_ref, w_ref, gq_ref, gk_ref, cos_ref, sin_ref,
                       q_ref, k_ref, v_ref, kctx_ref, vctx_ref, stage_ref):
    is_lat = _is_lat(TMB)
    scale = DH_C ** -0.5 * LOG2E
    nq, nkv = H_C * DH_C, HKV_C * DH_C

    def part(rows):
        n = rows.stop - rows.start
        hb = _norm_mod(x_ref[rows, :], gam_ref[...], sc_ref[...], sh_ref[...]).astype(BF16)
        cos = jnp.where(is_lat, cos_ref[rows, :], 1.0)
        sin = jnp.where(is_lat, sin_ref[rows, :], 0.0)
        yield
        z = _dot(hb, w_ref[:, nq:nq + nkv])
        for h in range(HKV_C):
            hs = slice(h * DH_C, (h + 1) * DH_C)
            k_n = _rms(z[:, hs], gk_ref[...])
            stage_ref[rows, hs] = k_n
            k_ref[rows, hs] = _rope(k_n, cos, sin, DH_C // 4).astype(BF16)
        v = _dot(hb, w_ref[:, nq + nkv:])
        stage_ref[rows, nkv:] = v
        for h in range(HKV_C):
            v_ref[rows, 2 * h * DH_C:(2 * h + 1) * DH_C] = v[:, h * DH_C:(h + 1) * DH_C].astype(BF16)
            v_ref[rows, (2 * h + 1) * DH_C:2 * (h + 1) * DH_C] = jnp.ones((n, DH_C), BF16)
        for lo in range(0, nq, MXU_N):
            z = _dot(hb, w_ref[:, lo:lo + MXU_N])
            for j in range(0, MXU_N, DH_C):
                q_ref[rows, lo + j:lo + j + DH_C] = (
                    _rope(_rms(z[:, j:j + DH_C], gq_ref[...]), cos, sin, DH_C // 4) * scale).astype(BF16)

    _staged([part(slice(r, r + TM)) for r in range(0, TMB, TM)])
    _copy_ctx(stage_ref, ((0, DH_C, HKV_C, kctx_ref), (nkv, DH_C, HKV_C, vctx_ref)))


def _inproj_odd(x, gamma, mod, w_in, g_q, g_k, cos, sin):
    nq, nkv = H_C * DH_C, HKV_C * DH_C
    return pl.pallas_call(
        _inproj_odd_kernel,
        grid=(M_TOK // TMB,),
        in_specs=[
            _row_spec(D_MODEL, tm=TMB), _const_spec((1, D_MODEL)), _mod_spec(0, TMB), _mod_spec(1, TMB),
            _const_spec((D_MODEL, nq + 2 * nkv)), _const_spec((1, DH_C)), _const_spec((1, DH_C)),
            pl.BlockSpec((TMB, LANES), lambda i: (_pos_block(i, TMB), 0)),
            pl.BlockSpec((TMB, LANES), lambda i: (_pos_block(i, TMB), 0)),
        ],
        out_specs=[_row_spec(nq, tm=TMB), _row_spec(nkv, tm=TMB), _row_spec(2 * nkv, tm=TMB),
                   _ctx_parts_spec(HKV_C, TMB), _ctx_parts_spec(HKV_C, TMB)],
        out_shape=[
            jax.ShapeDtypeStruct((M_TOK, nq), BF16), jax.ShapeDtypeStruct((M_TOK, nkv), BF16),
            jax.ShapeDtypeStruct((M_TOK, 2 * nkv), BF16), jax.ShapeDtypeStruct((N_CTX * HKV_C, LANES), F32),
            jax.ShapeDtypeStruct((N_CTX * HKV_C, LANES), F32),
        ],
        scratch_shapes=[pltpu.VMEM((TMB, 2 * nkv), F32)],
        compiler_params=_cparams(("arbitrary",)),
        name="inproj_odd",
    )(x, gamma, mod, mod, w_in, g_q, g_k, cos, sin)


def _gqa_body(q_ref, kv_refs, o_ref, seqs):
    rep = H_C // HKV_C
    outs, parts = [], []
    for rows in seqs:
        for hk in range(HKV_C):
            ks = slice(hk * DH_C, (hk + 1) * DH_C)
            q_g = jnp.concatenate(
                [q_ref[rows, (hk * rep + g) * DH_C:(hk * rep + g + 1) * DH_C] for g in range(rep)], axis=0)
            kv = [(k_ref[rows if n == 0 else slice(None), ks],
                   v_ref[rows if n == 0 else slice(None), 2 * hk * DH_C:2 * (hk + 1) * DH_C])
                  for n, (k_ref, v_ref) in enumerate(kv_refs)]
            outs.append((rows, hk, []))
            parts.append(_attend(q_g, kv, outs[-1][2]))
    _staged(parts)
    for rows, hk, o in outs:
        tq = o[0].shape[0] // rep
        for g in range(rep):
            o_ref[rows, (hk * rep + g) * DH_C:(hk * rep + g + 1) * DH_C] = o[0][g * tq:(g + 1) * tq].astype(BF16)


def _gqa_kernel(q_ref, k_ref, v_ref, o_ref):
    _gqa_body(q_ref, [(k_ref, v_ref)], o_ref, _seq_rows(q_ref))


def _gqa_cached_kernel(q_ref, k_ref, v_ref, kc_ref, vc_ref, o_ref):
    _gqa_body(q_ref, [(k_ref, v_ref), (kc_ref, vc_ref)], o_ref, [slice(None)])


def _gqa(q, k, v, k_cache, v_cache):
    nq, nkv = H_C * DH_C, HKV_C * DH_C
    ctx = pl.pallas_call(
        _gqa_kernel,
        grid=(N_CTX // TMC,),
        in_specs=[_row_spec(nq, tm=TMC), _row_spec(nkv, tm=TMC), _row_spec(2 * nkv, tm=TMC)],
        out_specs=_row_spec(nq, tm=TMC),
        out_shape=jax.ShapeDtypeStruct((N_CTX, nq), BF16),
        compiler_params=_cparams(("arbitrary",)),
        name="gqa_ctx",
    )(q, k, v)
    nqb = DEC_SEQ // TQ_LAT
    off = N_CTX // TQ_LAT
    seq0 = N_CTX // DEC_SEQ
    lat = pl.pallas_call(
        _gqa_cached_kernel,
        grid=(DEC_BATCH, nqb),
        in_specs=[
            pl.BlockSpec((TQ_LAT, nq), lambda b, j: (off + b * nqb + j, 0)),
            pl.BlockSpec((DEC_SEQ, nkv), lambda b, j: (seq0 + b, 0)),
            pl.BlockSpec((DEC_SEQ, 2 * nkv), lambda b, j: (seq0 + b, 0)),
            pl.BlockSpec((None, PAST_LEN, nkv), lambda b, j: (b, 0, 0)),
            pl.BlockSpec((None, PAST_LEN, 2 * nkv), lambda b, j: (b, 0, 0)),
        ],
        out_specs=pl.BlockSpec((TQ_LAT, nq), lambda b, j: (b * nqb + j, 0)),
        out_shape=jax.ShapeDtypeStruct((N_LAT, nq), BF16),
        compiler_params=_cparams(("arbitrary", "arbitrary")),
        name="gqa_lat",
    )(q, k, v, k_cache, v_cache)
    return ctx, lat


def _postmix_odd_kernel(actx_ref, alat_ref, x_ref, g1_ref, wo_ref, gam_ref, sh_ref, sc_ref, wrh_ref, wrl_ref,
                        x1_ref, h_ref, w1_ref, w2_ref, idx_ref, cnt_ref, run0_ref, run1_ref):
    @pl.when(pl.program_id(0) == 0)
    def _():
        run0_ref[...] = jnp.zeros_like(run0_ref)
        run1_ref[...] = jnp.zeros_like(run1_ref)

    is_lat = _is_lat(TMB)
    lane = lax.broadcasted_iota(jnp.int32, (TMB, LANES), 1)
    lane_f = lane.astype(F32)
    picked = []

    def part(rows):
        attn = jnp.where(is_lat, alat_ref[rows, :], actx_ref[rows, :])
        x1 = x_ref[rows, :] + g1_ref[...] * _dot(attn, wo_ref[...])
        x1_ref[rows, :] = x1
        yield
        h = _norm_mod(x1, gam_ref[...], sc_ref[...], sh_ref[...])
        h_hi, h_lo = _split2(h)
        h_ref[rows, :] = h_hi
        logits = _dot(h_hi, wrh_ref[...]) + (_dot(h_hi, wrl_ref[...]) + _dot(h_lo, wrh_ref[...]))
        yield
        lane_p = lax.broadcasted_iota(jnp.int32, logits.shape, 1).astype(F32)
        lg = jnp.where(lane_p < N_EXPERTS, logits, -jnp.inf)
        m1 = jnp.max(lg, axis=-1, keepdims=True)
        i1 = jnp.min(jnp.where(lg == m1, lane_p, float(LANES)), axis=-1, keepdims=True)
        lg2 = jnp.where(lane_p == i1, -jnp.inf, lg)
        m2 = jnp.max(lg2, axis=-1, keepdims=True)
        i2 = jnp.min(jnp.where(lg2 == m2, lane_p, float(LANES)), axis=-1, keepdims=True)
        e = jnp.exp(m2 - m1)
        w1 = 1.0 / (1.0 + e)
        w1_ref[rows, :] = jnp.broadcast_to(w1, logits.shape)
        w2_ref[rows, :] = jnp.broadcast_to(e * w1, logits.shape)
        picked.append((jnp.broadcast_to(i1, logits.shape), jnp.broadcast_to(i2, logits.shape)))

    _staged([part(slice(r, r + TM)) for r in range(0, TMB, TM)])
    i1 = jnp.concatenate([p[0] for p in picked], axis=0)
    i2 = jnp.concatenate([p[1] for p in picked], axis=0)
    oh1 = jnp.where(lane_f == i1, 1.0, 0.0)
    oh2 = jnp.where(lane_f == i2, 1.0, 0.0)
    r_i = lax.broadcasted_iota(jnp.int32, (TMB, TMB), 0)
    c_i = lax.broadcasted_iota(jnp.int32, (TMB, TMB), 1)
    before = jnp.where(c_i < r_i, 1.0, 0.0).astype(BF16)
    rank1 = jnp.sum((_dot(before, oh1.astype(BF16)) + run0_ref[0:1, :]) * oh1, axis=-1, keepdims=True)
    rank2 = jnp.sum((_dot(before, oh2.astype(BF16)) + run1_ref[0:1, :]) * oh2, axis=-1, keepdims=True)
    run0_ref[...] = run0_ref[...] + jnp.sum(oh1, axis=0, keepdims=True)
    run1_ref[...] = run1_ref[...] + jnp.sum(oh2, axis=0, keepdims=True)
    sub = lax.broadcasted_iota(jnp.int32, cnt_ref.shape, 0)
    cnt_ref[...] = jnp.where(sub == 0, run0_ref[...], run1_ref[...])
    quarter = LANES // 4
    cols = jnp.where(lane < quarter, i1, jnp.where(lane < 2 * quarter, i2, jnp.where(
        lane < 3 * quarter, rank1, rank2)))
    rows_t = jnp.transpose(cols)
    idx_ref[...] = jnp.concatenate(
        [rows_t[q * quarter:q * quarter + 1] for q in range(4)] + [jnp.zeros((4, TMB), F32)],
        axis=0).astype(jnp.int32)


def _postmix_odd(attn_ctx, attn_lat, x, mod, w_out, gamma, w_router):
    wr_hi = w_router.astype(BF16)
    wr_lo = (w_router - wr_hi.astype(F32)).astype(BF16)
    row = functools.partial(_row_spec, tm=TMB)
    return pl.pallas_call(
        _postmix_odd_kernel,
        grid=(M_TOK // TMB,),
        in_specs=[_ctx_spec(H_C * DH_C, TMB), _lat_spec(H_C * DH_C, TMB), row(D_MODEL), _mod_spec(2, TMB),
                  _const_spec((H_C * DH_C, D_MODEL)),
                  _const_spec((1, D_MODEL)), _mod_spec(3, TMB), _mod_spec(4, TMB), _const_spec((D_MODEL, LANES)),
                  _const_spec((D_MODEL, LANES))],
        out_specs=[row(D_MODEL), row(D_MODEL), row(LANES), row(LANES), pl.BlockSpec((8, TMB), lambda i: (0, i)),
                   _const_spec((8, LANES))],
        out_shape=[
            jax.ShapeDtypeStruct((M_TOK, D_MODEL), F32), jax.ShapeDtypeStruct((M_TOK, D_MODEL), BF16),
            jax.ShapeDtypeStruct((M_TOK, LANES), F32), jax.ShapeDtypeStruct((M_TOK, LANES), F32),
            jax.ShapeDtypeStruct((8, M_TOK), jnp.int32), jax.ShapeDtypeStruct((8, LANES), F32),
        ],
        scratch_shapes=[pltpu.VMEM((8, LANES), F32), pltpu.VMEM((8, LANES), F32)],
        compiler_params=_cparams(("arbitrary",)),
        name="postmix_odd",
    )(attn_ctx, attn_lat, x, mod, w_out, gamma, mod, mod, wr_hi, wr_lo)


def _cast_rows(src_ref, dst_ref, piece):
    def body(r, carry):
        rows = pl.ds(pl.multiple_of(r * piece, piece), piece)
        dst_ref[rows, :] = src_ref[rows, :].astype(BF16)
        return carry

    lax.fori_loop(0, src_ref.shape[0] // piece, body, 0)


def _experts_up_kernel(tile_ref, chunk_ref, exp_ref, first_ref, valid_ref, x_ref, wg_ref, wu_ref, act_ref,
                       wgb_ref, wub_ref):
    s = pl.program_id(0)

    @pl.when(first_ref[s] == 1)
    def _():
        _cast_rows(wg_ref, wgb_ref, 128)
        _cast_rows(wu_ref, wub_ref, 128)

    @pl.when(valid_ref[s] == 1)
    def _():
        _swiglu_act(x_ref[...].astype(BF16), wgb_ref, wub_ref, act_ref)

    @pl.when(valid_ref[s] == 0)
    def _():
        piece = 64

        def body(r, carry):
            act_ref[pl.ds(pl.multiple_of(r * piece, piece), piece), :] = jnp.zeros((piece, TF_E), BF16)
            return carry

        lax.fori_loop(0, TME // piece, body, 0)


def _experts_down_kernel(act_tile_ref, exp_ref, first_ref, valid_ref, act_ref, wd_ref, o_ref, wdb_ref):
    i = pl.program_id(0)

    @pl.when(first_ref[i] == 1)
    def _():
        _cast_rows(wd_ref, wdb_ref, 256)

    @pl.when(valid_ref[i] == 1)
    def _():
        o_ref[...] = _dot(act_ref[...], wdb_ref[...])

    @pl.when(valid_ref[i] == 0)
    def _():
        o_ref[...] = jnp.zeros_like(o_ref)


def _experts(up_tables, down_tables, xs, w_gate, w_up, w_down):
    def w_map(s, t, c, e, f, v):
        return e[s], 0, jnp.where(v[s] == 1, c[s], NC_E - 1)

    up = pl.pallas_call(
        _experts_up_kernel,
        grid_spec=pltpu.PrefetchScalarGridSpec(
            num_scalar_prefetch=5,
            grid=(NC_E * NT_E,),
            in_specs=[
                pl.BlockSpec((TME, D_MODEL), lambda s, t, c, e, f, v: (t[s], 0)),
                pl.BlockSpec((None, D_MODEL, TF_E), w_map), pl.BlockSpec((None, D_MODEL, TF_E), w_map),
            ],
            out_specs=pl.BlockSpec((TME, TF_E), lambda s, t, c, e, f, v: (t[s], c[s])),
            scratch_shapes=[pltpu.VMEM((D_MODEL, TF_E), BF16), pltpu.VMEM((D_MODEL, TF_E), BF16)],
        ),
        out_shape=jax.ShapeDtypeStruct((NT_E * TME, D_FF_E), BF16),
        compiler_params=_cparams(("arbitrary",)),
        name="experts_up",
    )(*up_tables, xs, w_gate, w_up)
    return pl.pallas_call(
        _experts_down_kernel,
        grid_spec=pltpu.PrefetchScalarGridSpec(
            num_scalar_prefetch=4,
            grid=(NT_E,),
            in_specs=[
                pl.BlockSpec((TME, D_FF_E), lambda i, a, e, f, v: (a[i], 0)),
                pl.BlockSpec((None, D_FF_E, D_MODEL), lambda i, a, e, f, v: (e[i], 0, 0)),
            ],
            out_specs=pl.BlockSpec((TME, D_MODEL), lambda i, a, e, f, v: (i, 0)),
            scratch_shapes=[pltpu.VMEM((D_FF_E, D_MODEL), BF16)],
        ),
        out_shape=jax.ShapeDtypeStruct((NT_E * TME, D_MODEL), F32),
        compiler_params=_cparams(("arbitrary",)),
        name="experts_down",
    )(*down_tables, up, w_down)


def _combine_kernel(x_ref, y1_ref, y2_ref, w1_ref, w2_ref, g2_ref, gam_ref, octx_ref, olat_ref):
    rep = D_MODEL // LANES
    w1 = jnp.concatenate([w1_ref[...]] * rep, axis=1)
    w2 = jnp.concatenate([w2_ref[...]] * rep, axis=1)
    x2 = x_ref[...] + g2_ref[...] * (w1 * y1_ref[...] + w2 * y2_ref[...])
    y = _rms(x2, gam_ref[...])
    is_lat = _is_lat(TMB)

    @pl.when(jnp.logical_not(is_lat))
    def _():
        octx_ref[...] = y

    @pl.when(is_lat)
    def _():
        olat_ref[...] = y


def _combine(x1, y1, y2, w1, w2, mod, gamma):
    return pl.pallas_call(
        _combine_kernel,
        grid=(M_TOK // TMB,),
        in_specs=[_row_spec(D_MODEL, tm=TMB), _row_spec(D_MODEL, tm=TMB), _row_spec(D_MODEL, tm=TMB),
                  _row_spec(LANES, tm=TMB), _row_spec(LANES, tm=TMB), _mod_spec(5, TMB), _const_spec((1, D_MODEL))],
        out_specs=[_ctx_spec(D_MODEL, TMB), _lat_spec(D_MODEL, TMB)],
        out_shape=[jax.ShapeDtypeStruct((N_CTX, D_MODEL), F32), jax.ShapeDtypeStruct((N_LAT, D_MODEL), F32)],
        compiler_params=_cparams(("arbitrary",)),
        name="combine",
    )(x1, y1, y2, w1, w2, mod, gamma)


def _rope_tables(rot_dim):
    n = DEC_SEQ
    rows = np.repeat(np.arange(n // GRID_W, dtype=np.float64), GRID_W)
    cols = np.tile(np.arange(GRID_W, dtype=np.float64), n // GRID_W)
    half = rot_dim // 2
    freqs = ROPE_THETA ** (-np.arange(0, half, 2, dtype=np.float64) / half)
    ang_r, ang_c = rows[:, None] * freqs, cols[:, None] * freqs
    cos = np.concatenate([np.cos(ang_r)] * 2 + [np.cos(ang_c)] * 2, axis=-1)
    sin = np.concatenate([-np.sin(ang_r), np.sin(ang_r), -np.sin(ang_c), np.sin(ang_c)], axis=-1)
    rep = LANES // rot_dim
    return (jnp.asarray(np.tile(cos, (1, rep)).astype(np.float32)),
            jnp.asarray(np.tile(sin, (1, rep)).astype(np.float32)))


def _with_ones(v):
    return jnp.concatenate([v, jnp.ones_like(v)], axis=-1)


def _rows(a, idx):
    return a.at[idx].get(mode="promise_in_bounds")


def _route(idx, counts_f):
    experts = jnp.arange(N_EXPERTS, dtype=jnp.int32)
    e2 = idx[0:2]
    first_total = counts_f[0, :N_EXPERTS].astype(jnp.int32)
    counts = first_total + counts_f[1, :N_EXPERTS].astype(jnp.int32)
    tiles = (counts + TME - 1) // TME
    tile_end = jnp.cumsum(tiles)
    tile_start = tile_end - tiles
    start = jnp.cumsum(counts) - counts

    def lookup(table, keys):
        return sum(jnp.where(keys == k, table[k], 0) for k in range(N_EXPERTS))

    rank2 = jnp.stack([idx[2], idx[3] + lookup(first_total, e2[1])], axis=0)
    pos = lookup(tile_start * TME, e2) + rank2
    tile_id = jnp.arange(NT_E, dtype=jnp.int32)
    tile_expert = jnp.minimum(jnp.sum((tile_id[:, None] >= tile_end[None, :]).astype(jnp.int32), axis=1),
                              N_EXPERTS - 1).astype(jnp.int32)
    tile_valid = (tile_id < tile_end[-1]).astype(jnp.int32)
    t_oh = tile_expert[:, None] == experts[None, :]
    t_first = jnp.sum(jnp.where(t_oh, (start - tile_start * TME)[None, :], 0), axis=1) + tile_id * TME
    t_last = jnp.sum(jnp.where(t_oh, (start + counts)[None, :], 0), axis=1)
    order = jnp.argsort(e2.reshape(-1), stable=True).astype(jnp.int32)
    n_assign = order.shape[0]
    g_idx = t_first[:, None] + jnp.arange(TME, dtype=jnp.int32)[None, :]
    live = jnp.logical_and(g_idx < t_last[:, None], tile_valid[:, None] == 1)
    src = _rows(order, jnp.clip(g_idx, 0, n_assign - 1).reshape(-1)) % M_TOK
    src_tok = jnp.where(live.reshape(-1), src, 0).astype(jnp.int32)

    n_tiles = tile_end[-1]
    last_tile = jnp.maximum(n_tiles - 1, 0)
    last_expert = jnp.sum(jnp.where(tile_id == last_tile, tile_expert, 0))
    t_tstart = jnp.sum(jnp.where(t_oh, tile_start[None, :], 0), axis=1)
    down_tables = (
        jnp.where(tile_valid == 1, tile_id, last_tile).astype(jnp.int32),
        tile_expert,
        jnp.logical_and(tile_valid == 1, tile_id == t_tstart).astype(jnp.int32),
        tile_valid,
    )
    s_id = jnp.arange(NC_E * NT_E, dtype=jnp.int32)
    s_exp = jnp.minimum(jnp.sum((s_id[:, None] >= NC_E * tile_end[None, :]).astype(jnp.int32), axis=1),
                        N_EXPERTS - 1)
    s_oh = s_exp[:, None] == experts[None, :]
    s_tstart = jnp.sum(jnp.where(s_oh, tile_start[None, :], 0), axis=1)
    s_ntile = jnp.maximum(jnp.sum(jnp.where(s_oh, tiles[None, :], 0), axis=1), 1)
    rel = s_id - NC_E * s_tstart
    s_valid = s_id < NC_E * n_tiles
    spare = s_id - NC_E * n_tiles
    up_tables = (
        jnp.where(s_valid, s_tstart + rel % s_ntile, n_tiles + spare // NC_E).astype(jnp.int32),
        jnp.where(s_valid, rel // s_ntile, spare % NC_E).astype(jnp.int32),
        jnp.where(s_valid, s_exp, last_expert).astype(jnp.int32),
        jnp.logical_and(s_valid, rel % s_ntile == 0).astype(jnp.int32),
        s_valid.astype(jnp.int32),
    )
    return src_tok, pos, up_tables, down_tables


def lambda_init(layer):
    return 0.8 - 0.6 * math.exp(-0.3 * layer)


def kernel(x_prompt, x_sample, state_a, cache_b_k, cache_b_v, cache_c_k, cache_c_v, c, c_ctx, w_mod, b_mod, norm_mix, norm_ffn, w_in_even, w_gate2_a, b_gate_a, g_gla, lam_q1, lam_k1, lam_q2, lam_k2, g_sub_b, w_out_even, w_in_odd, g_q_c, g_k_c, w_out_odd, ffn_gate, ffn_up, ffn_down, w_router, exp_gate, exp_up, exp_down, norm_final):
    x_ctx, x_lat = x_prompt.reshape(N_CTX, D_MODEL), x_sample.reshape(N_LAT, D_MODEL)
    cond =jnp.concatenate([c_ctx[None, :], c, jnp.zeros((N_COND - 1 - DEC_BATCH, D_MODEL), F32)], axis=0)
    mod = _modulation(cond, w_mod, b_mod).reshape(DEPTH, N_COND, 1, 6 * D_MODEL)

    w = w_in_even[0]
    na = 2 * A_QK + 2 * A_V
    gate_lo = na
    gate_hi = na + 2 * GATE_RANK
    col_scale = jnp.concatenate([jnp.full((A_QK,), DK_A ** -0.5, F32), jnp.ones((na - A_QK,), F32),
                                 jnp.full((B_QK,), DH_B ** -0.5, F32), jnp.ones((B_QK + B_V,), F32)])
    w_main = (jnp.concatenate([w[:, :gate_lo], w[:, gate_hi:]], axis=1) * col_scale).astype(BF16)
    w_gates = jnp.pad(w[:, gate_lo:gate_hi], ((0, 0), (0, LANES - 2 * GATE_RANK))).astype(BF16)
    g2 = jnp.zeros((LANES, 2 * A_QK), F32)
    g2 = g2.at[:GATE_RANK, :A_QK].set(w_gate2_a[0, 0]).at[GATE_RANK:2 * GATE_RANK, A_QK:].set(w_gate2_a[0, 1])
    bg = b_gate_a[0].reshape(1, 2 * A_QK)
    cos_b, sin_b = _rope_tables(DH_B)
    a_proj, glog, qb, kb, vb, kb_ctx, vb_ctx = _inproj_even(
        x_ctx, x_lat, norm_mix[0][None, :], mod[0], w_main, w_gates, g2.astype(BF16), bg, cos_b, sin_b)

    s0_all = jnp.concatenate([jnp.zeros((1, 2, A_QK, DV_A), F32),
                              state_a[:, 0].reshape(DEC_BATCH, 2, A_QK, DV_A)], axis=0)
    sel = np.arange(A_QK)[:, None] // DK_A == np.arange(LANES)[None, :] // GLA_SB
    m1 = jnp.asarray(sel.astype(np.float32)).astype(BF16)
    o_f, o_b, s_fin = _gla(a_proj, glog, s0_all, m1)

    k_cache = cache_b_k[:, 0].reshape(DEC_BATCH, PAST_LEN, B_QK).astype(BF16)
    v_cache = _with_ones(cache_b_v[:, 0].astype(BF16)).reshape(DEC_BATCH, PAST_LEN, 2 * B_V)
    lam_params = [p[0][None, :] for p in (lam_q1, lam_k1, lam_q2, lam_k2)]
    attn_b = _diffattn(qb, kb, vb, k_cache, v_cache, lam_params, g_sub_b[0][None, :], lambda_init(0))

    x = _mix_ffn_even(o_f, o_b, a_proj, *attn_b, x_ctx, x_lat, mod[0], g_gla[0][None, :],
                      w_out_even[0].astype(BF16), norm_ffn[0][None, :], ffn_gate[0].astype(BF16),
                      ffn_up[0].astype(BF16), ffn_down[0].astype(BF16))

    cos_c, sin_c = _rope_tables(DH_C)
    q_c, k_c, v_c, kc_ctx, vc_ctx = _inproj_odd(x, norm_mix[1][None, :], mod[1], w_in_odd[0].astype(BF16),
                                     g_q_c[0][None, :], g_k_c[0][None, :], cos_c, sin_c)
    nkv = HKV_C * DH_C
    k_cache = cache_c_k[:, 0].reshape(DEC_BATCH, PAST_LEN, nkv).astype(BF16)
    v_cache = _with_ones(cache_c_v[:, 0].astype(BF16)).reshape(DEC_BATCH, PAST_LEN, 2 * nkv)
    attn_c = _gqa(q_c, k_c, v_c, k_cache, v_cache)
    w_r = jnp.pad(w_router[0], ((0, 0), (0, LANES - N_EXPERTS)))
    x1, h_moe, w1, w2, idx, counts = _postmix_odd(*attn_c, x, mod[1], w_out_odd[0].astype(BF16),
                                                  norm_ffn[1][None, :], w_r)

    src_tok, pos, up_tables, down_tables = _route(idx, counts)
    xs = _rows(h_moe, src_tok)
    ys = _experts(up_tables, down_tables, xs, exp_gate[0], exp_up[0], exp_down[0])
    y1 = _rows(ys, pos[0])
    y2 = _rows(ys, pos[1])
    y_ctx, y_lat = _combine(x1, y1, y2, w1, w2, mod[1], norm_final[None, :])

    y_prompt = y_ctx.reshape(BATCH, SEQ, D_MODEL)
    y_sample = y_lat.reshape(DEC_BATCH, DEC_SEQ, D_MODEL)
    new_state_a = s_fin.reshape(BATCH, 1, 2, H_A, DK_A, DV_A)
    new_b_k = jnp.transpose(kb_ctx.reshape(BATCH, 1, H_B, 2, DH_B, SEQ), (0, 1, 5, 2, 3, 4))
    new_b_v = vb_ctx.reshape(BATCH, 1, SEQ, H_B, DV_B)
    new_c_k = kc_ctx.reshape(BATCH, 1, SEQ, HKV_C, DH_C)
    new_c_v = vc_ctx.reshape(BATCH, 1, SEQ, HKV_C, DH_C)
    return (y_prompt, y_sample, new_state_a, new_b_k, new_b_v, new_c_k, new_c_v)
```

```python
import functools
import math

import numpy as np
import jax
import jax.numpy as jnp
from jax import lax
from jax.experimental import pallas as pl
from jax.experimental.pallas import tpu as pltpu

D_MODEL = 1024
BATCH = 32
SEQ = 256
DEPTH = 2
DEC_BATCH = 2
DEC_SEQ = 2048
PAST_LEN = 512
GRID_W = 64
H_A, DK_A, DV_A = 4, 64, 128
GATE_RANK = 16
GATE_TAU = 16.0
H_B, DH_B = 4, 64
DV_B = 2 * DH_B
H_C, HKV_C, DH_C = 8, 2, 128
D_FF = 2816
N_EXPERTS = 8
D_FF_E = 3584
ROPE_THETA = 10000.0
EPS = 1e-6
A_QK = H_A * DK_A
A_V = H_A * DV_A
B_QK = H_B * 2 * DH_B
B_V = H_B * DV_B

F32 = jnp.float32
BF16 = jnp.bfloat16
LOG2E = math.log2(math.e)

V7X_VMEM_BYTES = 64 * 1024 * 1024
VMEM_LIMIT = V7X_VMEM_BYTES * 7 // 8
LANES = 128
MXU_N = 256
TM = 256
N_CTX = BATCH * SEQ
N_LAT = DEC_BATCH * DEC_SEQ
M_TOK = N_CTX + N_LAT
NB = M_TOK // TM
NCB = N_CTX // TM
LBB = DEC_SEQ // TM
N_COND = 8
GLA_C = 64
GLA_SB = 8
TME = 512
NT_E = 2 * M_TOK // TME + N_EXPERTS
NC_E = 2
TF_E = D_FF_E // NC_E
TQ_LAT = 256
TMC = 4 * SEQ
TMB = 1024

assert SEQ == TM and DEC_SEQ % TM == 0 and DEPTH == 2


def _cparams(sem):
    return pltpu.CompilerParams(dimension_semantics=sem, vmem_limit_bytes=VMEM_LIMIT)


def _group(i, tm=TM):
    return jnp.where(i < N_CTX // tm, 0, 1 + (i - N_CTX // tm) // (DEC_SEQ // tm))


def _pos_block(i, tm=TM):
    return jnp.where(i < N_CTX // tm, 0, (i - N_CTX // tm) % (DEC_SEQ // tm))


def _mod_spec(chunk, tm=TM):
    return pl.BlockSpec((None, 1, D_MODEL), lambda i, c=chunk: (_group(i, tm), 0, c))


def _row_spec(width, col=0, tm=TM):
    return pl.BlockSpec((tm, width), lambda i, c=col: (i, c))


def _ctx_spec(width, tm=TM):
    return pl.BlockSpec((tm, width), lambda i: (jnp.minimum(i, N_CTX // tm - 1), 0))


def _ctx_parts_spec(parts, tm=TM):
    return pl.BlockSpec((tm * parts, LANES), lambda i: (jnp.minimum(i, N_CTX // tm - 1), 0))


def _lat_spec(width, tm=TM):
    return pl.BlockSpec((tm, width), lambda i: (jnp.maximum(i - N_CTX // tm, 0), 0))


def _const_spec(shape):
    nd = len(shape)
    return pl.BlockSpec(shape, lambda i, nd=nd: (0,) * nd, pipeline_mode=pl.Buffered(1))


def _dot(a, b):
    return jnp.dot(a, b, preferred_element_type=F32)


def _dot_nt(a, b):
    return lax.dot_general(a, b, (((1,), (1,)), ((), ())), preferred_element_type=F32)


def _split2(a):
    hi = a.astype(BF16)
    lo = (a - hi.astype(F32)).astype(BF16)
    return hi, lo


def _dot_hi(a, b):
    a_hi, a_lo = _split2(a)
    b_hi, b_lo = _split2(b)
    return _dot(a_hi, b_hi) + (_dot(a_hi, b_lo) + _dot(a_lo, b_hi))


def _silu(x):
    return (0.5 * x) * (1.0 + jnp.tanh(0.5 * x))


def _log_sigmoid(x):
    return jnp.minimum(x, 0.0) - jnp.log(1.0 + jnp.exp(-jnp.abs(x)))


def _rms(x, g):
    return x * lax.rsqrt(jnp.mean(x * x, axis=-1, keepdims=True) + EPS) * g


def _norm_mod(x, gamma, sc, sh):
    return _rms(x, gamma) * (1.0 + sc) + sh


def _staged(parts):
    live = list(parts)
    while live:
        live = [g for g in live if next(g, True) is None]


def _copy_ctx_transposed(stage3_ref, dst_ref):
    @pl.when(pl.program_id(0) < NCB)
    def _():
        def body(j, carry):
            dst_ref[pl.ds(pl.multiple_of(j * LANES, LANES), LANES), :] = jnp.transpose(stage3_ref[j])
            return carry

        lax.fori_loop(0, stage3_ref.shape[0], body, 0)


def _is_lat(tm):
    return pl.program_id(0) >= N_CTX // tm


def _pick_rows(ctx_ref, lat_ref):
    return jnp.where(_is_lat(ctx_ref.shape[0]), lat_ref[...], ctx_ref[...])


def _copy_ctx(stage_ref, outs):
    piece = 64

    @pl.when(jnp.logical_not(_is_lat(stage_ref.shape[0])))
    def _():
        def body(r, carry):
            rows = pl.ds(pl.multiple_of(r * piece, piece), piece)
            for lo, width, parts, dst_ref in outs:
                for p in range(parts):
                    val = stage_ref[rows, lo + p * width:lo + (p + 1) * width]
                    if width < LANES:
                        val = jnp.concatenate([val, jnp.zeros((piece, LANES - width), F32)], axis=1)
                    dst_ref[pl.ds(pl.multiple_of(r * piece * parts, piece * parts) + p, piece, stride=parts), :] = val
            return carry

        lax.fori_loop(0, stage_ref.shape[0] // piece, body, 0)


def _rope(x, cos, sin, half):
    lane = lax.broadcasted_iota(jnp.int32, x.shape, 1)
    first = (lane % (2 * half)) < half
    swapped = jnp.where(first, pltpu.roll(x, LANES - half, 1), pltpu.roll(x, half, 1))
    return x * cos + swapped * sin


def _mod_kernel(cond_ref, w_ref, b_ref, o_ref):
    o_ref[...] = _dot_hi(_silu(cond_ref[...]), w_ref[...]) + b_ref[...]


def _modulation(cond, w_mod, b_mod):
    tn = 1536
    return pl.pallas_call(
        _mod_kernel,
        grid=(DEPTH, 6 * D_MODEL // tn),
        in_specs=[
            pl.BlockSpec((N_COND, D_MODEL), lambda l, j: (0, 0)),
            pl.BlockSpec((None, D_MODEL, tn), lambda l, j: (l, 0, j)),
            pl.BlockSpec((None, 1, tn), lambda l, j: (l, 0, j)),
        ],
        out_specs=pl.BlockSpec((None, N_COND, tn), lambda l, j: (l, 0, j)),
        out_shape=jax.ShapeDtypeStruct((DEPTH, N_COND, 6 * D_MODEL), F32),
        compiler_params=_cparams(("arbitrary", "arbitrary")),
        name="modulation",
    )(cond, w_mod, b_mod.reshape(DEPTH, 1, 6 * D_MODEL))


def _inproj_even_kernel(xc_ref, xl_ref, gam_ref, sh_ref, sc_ref, w_ref, wg_ref, g2_ref, bg_ref, cos_ref, sin_ref,
                        a_ref, glog_ref, qb_ref, kb_ref, vb_ref, kctx_ref, vctx_ref, stage_ref, stagek_ref):
    is_lat = pl.program_id(0) >= NCB
    na = 2 * A_QK + 2 * A_V

    def part(rows):
        n = rows.stop - rows.start
        x = jnp.where(is_lat, xl_ref[rows, :], xc_ref[rows, :])
        hb = _norm_mod(x, gam_ref[...], sc_ref[...], sh_ref[...]).astype(BF16)
        cos = jnp.where(is_lat, cos_ref[rows, :], 1.0)
        sin = jnp.where(is_lat, sin_ref[rows, :], 0.0)
        yield
        for lo in range(0, na, MXU_N):
            a_ref[rows, lo:lo + MXU_N] = _dot(hb, w_ref[:, lo:lo + MXU_N])
        for lo in range(0, B_QK, MXU_N):
            zq = _dot(hb, w_ref[:, na + lo:na + lo + MXU_N])
            zk = _dot(hb, w_ref[:, na + B_QK + lo:na + B_QK + lo + MXU_N])
            zv = _dot(hb, w_ref[:, na + 2 * B_QK + lo:na + 2 * B_QK + lo + MXU_N])
            stage_ref[rows, lo:lo + MXU_N] = zv
            for j in range(0, MXU_N, LANES):
                cs = slice(lo + j, lo + j + LANES)
                vb_ref[rows, 2 * (lo + j):2 * (lo + j) + DV_B] = zv[:, j:j + LANES].astype(BF16)
                vb_ref[rows, 2 * (lo + j) + DV_B:2 * (lo + j + LANES)] = jnp.ones((n, DV_B), BF16)
                stagek_ref[(lo + j) // LANES, rows, :] = zk[:, j:j + LANES]
                qb_ref[rows, cs] = (_rope(zq[:, j:j + LANES], cos, sin, DH_B // 4) * LOG2E).astype(BF16)
                kb_ref[rows, cs] = _rope(zk[:, j:j + LANES], cos, sin, DH_B // 4).astype(BF16)
        gates = _dot(hb, wg_ref[...])
        xg = _dot(gates.astype(BF16), g2_ref[...]) + bg_ref[...]
        glog_ref[rows, :] = _log_sigmoid(xg) * (1.0 / GATE_TAU)

    _staged([part(slice(0, TM))])
    _copy_ctx(stage_ref, ((0, DV_B, H_B, vctx_ref),))
    _copy_ctx_transposed(stagek_ref, kctx_ref)


def _inproj_even(x_ctx, x_lat, gamma, mod, w_main, w_gates, g2, bg, cos, sin):
    na = 2 * A_QK + 2 * A_V
    nz = na + 2 * B_QK + B_V
    return pl.pallas_call(
        _inproj_even_kernel,
        grid=(NB,),
        in_specs=[
            _ctx_spec(D_MODEL), _lat_spec(D_MODEL), _const_spec((1, D_MODEL)), _mod_spec(0), _mod_spec(1),
            _const_spec((D_MODEL, nz)), _const_spec((D_MODEL, LANES)), _const_spec((LANES, 2 * A_QK)),
            _const_spec((1, 2 * A_QK)),
            pl.BlockSpec((TM, LANES), lambda i: (_pos_block(i), 0)),
            pl.BlockSpec((TM, LANES), lambda i: (_pos_block(i), 0)),
        ],
        out_specs=[_pair_spec(na), _pair_spec(2 * A_QK), _row_spec(B_QK), _row_spec(B_QK), _row_spec(2 * B_V),
                   pl.BlockSpec((None, B_QK, TM), lambda i: (jnp.minimum(i, NCB - 1), 0, 0)), _ctx_parts_spec(H_B)],
        out_shape=[
            jax.ShapeDtypeStruct((M_TOK, na), F32), jax.ShapeDtypeStruct((M_TOK, 2 * A_QK), F32),
            jax.ShapeDtypeStruct((M_TOK, B_QK), BF16), jax.ShapeDtypeStruct((M_TOK, B_QK), BF16),
            jax.ShapeDtypeStruct((M_TOK, 2 * B_V), BF16), jax.ShapeDtypeStruct((BATCH, B_QK, SEQ), F32),
            jax.ShapeDtypeStruct((N_CTX * H_B, LANES), F32),
        ],
        scratch_shapes=[pltpu.VMEM((TM, B_V), F32), pltpu.VMEM((B_QK // LANES, TM, LANES), F32)],
        compiler_params=_cparams(("arbitrary",)),
        name="inproj_even",
    )(x_ctx, x_lat, gamma, mod, mod, w_main, w_gates, g2, bg, cos, sin)


def _gla_chunk(q_ref, k_ref, v_ref, g_ref, o_ref, c, s_ref, a_ref, b_ref, m1_ref, head_mask, same_block, rev):
    q, k, v = q_ref[c], k_ref[c], v_ref[c]
    g = g_ref[c] * LOG2E
    v_b = v.astype(BF16)
    r_i = lax.broadcasted_iota(jnp.int32, (GLA_C, GLA_C), 0)
    c_i = lax.broadcasted_iota(jnp.int32, (GLA_C, GLA_C), 1)
    tri = jnp.where((c_i >= r_i) if rev else (c_i <= r_i), 1.0, 0.0).astype(BF16)
    g_hi = g.astype(BF16)
    g_r = g - g_hi.astype(F32)
    g_mid = g_r.astype(BF16)
    g_lo = (g_r - g_mid.astype(F32)).astype(BF16)
    bc = _dot(tri, g_hi) + (_dot(tri, g_mid) + _dot(tri, g_lo))
    b_ref[...] = bc
    yield
    t_i = lax.broadcasted_iota(jnp.int32, (GLA_SB, A_QK), 0)
    group = 2 * GLA_SB
    for i0 in range(0, GLA_C, group):
        blocks = range(i0, i0 + group, GLA_SB)
        q_i = [q_ref[c, i:i + GLA_SB, :] for i in blocks]
        b_i = [b_ref[i:i + GLA_SB, :] for i in blocks]
        for j in range(GLA_SB):
            keep = (t_i <= j) if rev else (t_i >= j)
            pair = []
            for n, i in enumerate(blocks):
                k_s = k_ref[c, i + j:i + j + 1, :]
                b_s = b_ref[i + j:i + j + 1, :]
                pair.append(jnp.where(keep, q_i[n] * k_s * jnp.exp2(b_i[n] - b_s), 0.0))
            a_ref[j * GLA_C + i0:j * GLA_C + i0 + group, :] = jnp.concatenate(pair, axis=0).astype(BF16)
    yield
    last = bc[0:1] if rev else bc[GLA_C - 1:GLA_C]
    qc = q * jnp.exp2(bc)
    kl = k * jnp.exp2(last - bc)
    row = lax.broadcasted_iota(jnp.int32, (GLA_C, A_QK), 0)

    def expand(x):
        return jnp.where(head_mask, jnp.concatenate([x] * H_A, axis=0), 0.0).astype(BF16)

    scores = None
    half = GLA_C // 2
    while half >= GLA_SB:
        mids = [bc[b + half:b + half + 1] if rev else bc[b + half - 1:b + half] for b in range(0, GLA_C, 2 * half)]
        ref = mids[0] if len(mids) == 1 else jnp.concatenate(
            [jnp.broadcast_to(m, (2 * half, A_QK)) for m in mids], axis=0)
        upper = (row // half) % 2 == 1
        later, earlier = (jnp.logical_not(upper), upper) if rev else (upper, jnp.logical_not(upper))
        q_l = jnp.where(later, q * jnp.exp2(bc - ref), 0.0)
        k_e = jnp.where(earlier, k * jnp.exp2(ref - bc), 0.0).astype(BF16)
        term = _dot_nt(expand(q_l), k_e)
        if 2 * half < GLA_C:
            term = jnp.where(same_block[2 * half], term, 0.0)
        scores = term if scores is None else scores + term
        half //= 2
    s_old = s_ref[...]
    inter = _dot(expand(qc), s_old.astype(BF16))
    sums = _dot(a_ref[...], m1_ref[...])
    yield
    lane_c = lax.broadcasted_iota(jnp.int32, (GLA_C, LANES), 1)
    within = None
    for j in range(GLA_SB):
        part = jnp.where(lane_c % GLA_SB == j, sums[j * GLA_C:(j + 1) * GLA_C], 0.0)
        within = part if within is None else within + part
    lane = lax.broadcasted_iota(jnp.int32, (GLA_SB, LANES), 1)
    per_head = []
    for h in range(H_A):
        rows_h = []
        for i in range(0, GLA_C, GLA_SB):
            moved = pltpu.roll(within[i:i + GLA_SB], (i - h * GLA_SB) % LANES, 1)
            rows_h.append(jnp.where(lane // GLA_SB == i // GLA_SB, moved, 0.0))
        per_head.append(jnp.concatenate(rows_h, axis=0)[:, :GLA_C])
    scores = scores + jnp.concatenate(per_head, axis=0)
    o_heads = []
    for h in range(H_A):
        hs = slice(h * GLA_C, (h + 1) * GLA_C)
        o_heads.append(inter[hs] + _dot(scores[hs].astype(BF16), v_b[:, h * DV_A:(h + 1) * DV_A]))
    o_ref[c] = jnp.concatenate(o_heads, axis=1)
    yield
    t = jnp.transpose(jnp.concatenate([kl, jnp.broadcast_to(last, (GLA_C, A_QK))], axis=0))
    kv = _dot(t[:, :GLA_C].astype(BF16), v_b)
    a_col = jnp.exp2(t[:, GLA_C:GLA_C + 1])
    for h in range(H_A):
        hs = slice(h * DK_A, (h + 1) * DK_A)
        s_ref[hs, :] = a_col[hs] * s_old[hs] + kv[hs, h * DV_A:(h + 1) * DV_A]


def _gla_kernel(fblk_ref, bblk_ref, first_ref, inita_ref, initb_ref,
                qf_ref, kf_ref, vf_ref, gf_ref, qr_ref, kr_ref, vr_ref, gr_ref, s0a_ref, s0b_ref, m1_ref,
                of_ref, ob_ref, sfin_ref, s_refs, a_refs, b_refs):
    step = pl.program_id(0)

    @pl.when(first_ref[step] == 1)
    def _():
        for seq, s0_ref in enumerate((s0a_ref, s0b_ref)):
            for d in range(2):
                s_refs[2 * seq + d] = s0_ref[d]

    r_h = lax.broadcasted_iota(jnp.int32, (H_A * GLA_C, A_QK), 0) // GLA_C
    c_h = lax.broadcasted_iota(jnp.int32, (H_A * GLA_C, A_QK), 1) // DK_A
    head_mask = r_h == c_h
    t_s = lax.broadcasted_iota(jnp.int32, (H_A * GLA_C, GLA_C), 0) % GLA_C
    s_s = lax.broadcasted_iota(jnp.int32, (H_A * GLA_C, GLA_C), 1)
    same_block = {}
    size = 2 * GLA_SB
    while size < GLA_C:
        same_block[size] = (t_s // size) == (s_s // size)
        size *= 2
    nch = TM // GLA_C

    def body(c, carry):
        chains = []
        for seq in range(2):
            chains.append(_gla_chunk(qf_ref, kf_ref, vf_ref, gf_ref, of_ref, seq * nch + c, s_refs.at[2 * seq],
                                     a_refs.at[2 * seq], b_refs.at[2 * seq], m1_ref, head_mask, same_block, False))
            chains.append(_gla_chunk(qr_ref, kr_ref, vr_ref, gr_ref, ob_ref, seq * nch + nch - 1 - c,
                                     s_refs.at[2 * seq + 1], a_refs.at[2 * seq + 1], b_refs.at[2 * seq + 1], m1_ref,
                                     head_mask, same_block, True))
        _staged(chains)
        return carry

    lax.fori_loop(0, nch, body, 0)

    @pl.when(step < BATCH // 2)
    def _():
        for seq in range(2):
            for d in range(2):
                sfin_ref[seq, d] = s_refs[2 * seq + d]


def _pair_block(i):
    j = i - NCB
    return jnp.where(i < NCB, i, NCB + 2 * (j % LBB) + j // LBB)


def _pair_spec(width, col=0):
    return pl.BlockSpec((TM, width), lambda i, c=col: (_pair_block(i), c))


def _gla_tables():
    fblk, bblk, first, init_a, init_b = [], [], [], [], []
    for p in range(BATCH // 2):
        fblk.append(p), bblk.append(p), first.append(1), init_a.append(0), init_b.append(0)
    for j in range(LBB):
        fblk.append(NCB // 2 + j)
        bblk.append(NCB // 2 + LBB - 1 - j)
        first.append(1 if j == 0 else 0)
        init_a.append(1), init_b.append(2)
    return [jnp.asarray(np.array(t, np.int32)) for t in (fblk, bblk, first, init_a, init_b)]


def _gla(a_proj, glog, s0_all, m1):
    tables = _gla_tables()
    nsteps = int(tables[0].shape[0])
    nseq = BATCH + DEC_BATCH

    def fmap(col):
        return lambda i, fb, bb, fi, ia, ib: (fb[i], 0, col)

    def rmap(col):
        return lambda i, fb, bb, fi, ia, ib: (bb[i], 0, col)

    nch = 2 * TM // GLA_C
    a3 = a_proj.reshape(M_TOK // GLA_C, GLA_C, a_proj.shape[1])
    g3 = glog.reshape(M_TOK // GLA_C, GLA_C, glog.shape[1])
    gs = pltpu.PrefetchScalarGridSpec(
        num_scalar_prefetch=5,
        grid=(nsteps,),
        in_specs=[
            pl.BlockSpec((nch, GLA_C, A_QK), fmap(0)), pl.BlockSpec((nch, GLA_C, A_QK), fmap(1)),
            pl.BlockSpec((nch, GLA_C, A_V), fmap(1)), pl.BlockSpec((nch, GLA_C, A_QK), fmap(0)),
            pl.BlockSpec((nch, GLA_C, A_QK), rmap(0)), pl.BlockSpec((nch, GLA_C, A_QK), rmap(1)),
            pl.BlockSpec((nch, GLA_C, A_V), rmap(1)), pl.BlockSpec((nch, GLA_C, A_QK), rmap(1)),
            pl.BlockSpec((None, 2, A_QK, DV_A), lambda i, fb, bb, fi, ia, ib: (ia[i], 0, 0, 0)),
            pl.BlockSpec((None, 2, A_QK, DV_A), lambda i, fb, bb, fi, ia, ib: (ib[i], 0, 0, 0)),
            pl.BlockSpec((A_QK, LANES), lambda i, fb, bb, fi, ia, ib: (0, 0)),
        ],
        out_specs=[
            pl.BlockSpec((nch, GLA_C, A_V), fmap(0)), pl.BlockSpec((nch, GLA_C, A_V), rmap(0)),
            pl.BlockSpec((2, 2, A_QK, DV_A),
                         lambda i, fb, bb, fi, ia, ib: (jnp.minimum(fb[i], BATCH // 2 - 1), 0, 0, 0)),
        ],
        scratch_shapes=[
            pltpu.VMEM((4, A_QK, DV_A), F32),
            pltpu.VMEM((4, GLA_SB * GLA_C, A_QK), BF16),
            pltpu.VMEM((4, GLA_C, A_QK), F32),
        ],
    )
    o_f, o_b, s_fin = pl.pallas_call(
        _gla_kernel,
        grid_spec=gs,
        out_shape=[
            jax.ShapeDtypeStruct((M_TOK // GLA_C, GLA_C, A_V), F32),
            jax.ShapeDtypeStruct((M_TOK // GLA_C, GLA_C, A_V), F32),
            jax.ShapeDtypeStruct((BATCH, 2, A_QK, DV_A), F32),
        ],
        compiler_params=_cparams(("arbitrary",)),
        name="gla",
    )(*tables, a3, a3, a3, g3, a3, a3, a3, g3, s0_all, s0_all, m1)
    return o_f.reshape(M_TOK, A_V), o_b.reshape(M_TOK, A_V), s_fin


def _attend(q_b, kv, out):
    scores = [_dot_nt(q_b, k_b) for k_b, _ in kv]
    yield
    top = functools.reduce(jnp.maximum, [jnp.max(s2, axis=-1, keepdims=True) for s2 in scores])
    both = None
    for s2, (_, v_ones) in zip(scores, kv):
        part = _dot(jnp.exp2(s2 - top).astype(BF16), v_ones)
        both = part if both is None else both + part
    yield
    dv = both.shape[1] // 2
    out.append(both[:, :dv] * (1.0 / both[:, dv:]))


def _seq_rows(ref):
    return [slice(r, r + SEQ) for r in range(0, ref.shape[0], SEQ)]


def _diffattn_body(q_ref, kv_refs, lq1_ref, lk1_ref, lq2_ref, lk2_ref, gsub_ref, o_ref, lam_init, seqs):
    lam = (jnp.exp(jnp.sum(lq1_ref[...] * lk1_ref[...], axis=-1, keepdims=True))
           - jnp.exp(jnp.sum(lq2_ref[...] * lk2_ref[...], axis=-1, keepdims=True)) + lam_init)
    outs, parts = [], []
    for rows in seqs:
        lane = lax.broadcasted_iota(jnp.int32, q_ref[rows, :LANES].shape, 1)
        for h in range(H_B):
            hs = slice(h * LANES, (h + 1) * LANES)
            q_h = q_ref[rows, hs]
            kv = [(k_ref[rows if n == 0 else slice(None), hs],
                   v_ref[rows if n == 0 else slice(None), 2 * h * DV_B:2 * (h + 1) * DV_B])
                  for n, (k_ref, v_ref) in enumerate(kv_refs)]
            zero = jnp.zeros_like(q_h)
            outs.append((rows, h, [], []))
            parts.append(_attend(jnp.where(lane < DH_B, q_h, zero), kv, outs[-1][2]))
            parts.append(_attend(jnp.where(lane >= DH_B, q_h, zero), kv, outs[-1][3]))
    _staged(parts)
    for rows, h, o1, o2 in outs:
        o = o1[0] - lam * o2[0]
        o_ref[rows, h * LANES:(h + 1) * LANES] = (_rms(o, gsub_ref[...]) * (1.0 - lam_init)).astype(BF16)


def _diffattn_kernel(q_ref, k_ref, v_ref, lq1, lk1, lq2, lk2, gsub, o_ref, *, lam_init):
    _diffattn_body(q_ref, [(k_ref, v_ref)], lq1, lk1, lq2, lk2, gsub, o_ref, lam_init, _seq_rows(q_ref))


def _diffattn_cached_kernel(q_ref, k_ref, v_ref, kc_ref, vc_ref, lq1, lk1, lq2, lk2, gsub, o_ref, *, lam_init):
    _diffattn_body(q_ref, [(k_ref, v_ref), (kc_ref, vc_ref)], lq1, lk1, lq2, lk2, gsub, o_ref, lam_init,
                   [slice(None)])


def _diffattn(qb, kb, vb, k_cache, v_cache, lam_params, gsub, lam_init):
    small = [pl.BlockSpec((1, DH_B), lambda *_: (0, 0))] * 4 + [pl.BlockSpec((1, DV_B), lambda *_: (0, 0))]
    body = functools.partial(_diffattn_kernel, lam_init=lam_init)
    ctx = pl.pallas_call(
        body,
        grid=(N_CTX // TMC,),
        in_specs=[_row_spec(B_QK, tm=TMC), _row_spec(B_QK, tm=TMC), _row_spec(2 * B_V, tm=TMC)] + small,
        out_specs=_row_spec(B_V, tm=TMC),
        out_shape=jax.ShapeDtypeStruct((N_CTX, B_V), BF16),
        compiler_params=_cparams(("arbitrary",)),
        name="diffattn_ctx",
    )(qb, kb, vb, *lam_params, gsub)
    nqb = DEC_SEQ // TQ_LAT
    off = N_CTX // TQ_LAT
    seq0 = N_CTX // DEC_SEQ
    lat = pl.pallas_call(
        functools.partial(_diffattn_cached_kernel, lam_init=lam_init),
        grid=(DEC_BATCH, nqb),
        in_specs=[
            pl.BlockSpec((TQ_LAT, B_QK), lambda b, j: (off + b * nqb + j, 0)),
            pl.BlockSpec((DEC_SEQ, B_QK), lambda b, j: (seq0 + b, 0)),
            pl.BlockSpec((DEC_SEQ, 2 * B_V), lambda b, j: (seq0 + b, 0)),
            pl.BlockSpec((None, PAST_LEN, B_QK), lambda b, j: (b, 0, 0)),
            pl.BlockSpec((None, PAST_LEN, 2 * B_V), lambda b, j: (b, 0, 0)),
        ] + small,
        out_specs=pl.BlockSpec((TQ_LAT, B_V), lambda b, j: (b * nqb + j, 0)),
        out_shape=jax.ShapeDtypeStruct((N_LAT, B_V), BF16),
        compiler_params=_cparams(("arbitrary", "arbitrary")),
        name="diffattn_lat",
    )(qb, kb, vb, k_cache, v_cache, *lam_params, gsub)
    return ctx, lat


def _swiglu_act(xb, wg_ref, wu_ref, act_ref):
    for lo in range(0, act_ref.shape[1], MXU_N):
        cs = slice(lo, lo + MXU_N)
        act_ref[:, cs] = (_silu(_dot(xb, wg_ref[:, cs])) * _dot(xb, wu_ref[:, cs])).astype(BF16)


def _mix_ffn_even_kernel(of_ref, ob_ref, ra_ref, actx_ref, alat_ref, xc_ref, xl_ref, g1_ref, ggla_ref, wo_ref,
                         gam_ref, sh_ref, sc_ref, g2_ref, wg_ref, wu_ref, wd_ref, o_ref, act_ref):
    is_lat = _is_lat(TM)

    def part(rows):
        heads = []
        for h in range(H_A):
            hs = slice(h * DV_A, (h + 1) * DV_A)
            heads.append((_rms(of_ref[rows, hs] + ob_ref[rows, hs], ggla_ref[...])
                          * _silu(ra_ref[rows, hs])).astype(BF16))
        mix = jnp.concatenate(heads + [jnp.where(is_lat, alat_ref[rows, :], actx_ref[rows, :])], axis=1)
        yield
        x = jnp.where(is_lat, xl_ref[rows, :], xc_ref[rows, :]) + g1_ref[...] * _dot(mix, wo_ref[...])
        hb = _norm_mod(x, gam_ref[...], sc_ref[...], sh_ref[...]).astype(BF16)
        yield
        for lo in range(0, D_FF, MXU_N):
            cs = slice(lo, lo + MXU_N)
            act_ref[rows, cs] = (_silu(_dot(hb, wg_ref[:, cs])) * _dot(hb, wu_ref[:, cs])).astype(BF16)
        yield
        o_ref[rows, :] = x + g2_ref[...] * _dot(act_ref[rows, :], wd_ref[...])

    _staged([part(slice(r, r + TM // 2)) for r in range(0, TM, TM // 2)])


def _mix_ffn_even(o_f, o_b, a_proj, attn_ctx, attn_lat, x_ctx, x_lat, mod, g_gla, w_out, gamma, w_gate, w_up, w_down):
    return pl.pallas_call(
        _mix_ffn_even_kernel,
        grid=(NB,),
        in_specs=[_pair_spec(A_V), _pair_spec(A_V), _pair_spec(A_V, 2), _ctx_spec(B_V), _lat_spec(B_V),
                  _ctx_spec(D_MODEL), _lat_spec(D_MODEL), _mod_spec(2), _const_spec((1, DV_A)),
                  _const_spec((A_V + B_V, D_MODEL)),
                  _const_spec((1, D_MODEL)), _mod_spec(3), _mod_spec(4), _mod_spec(5),
                  _const_spec((D_MODEL, D_FF)), _const_spec((D_MODEL, D_FF)), _const_spec((D_FF, D_MODEL))],
        out_specs=_row_spec(D_MODEL),
        out_shape=jax.ShapeDtypeStruct((M_TOK, D_MODEL), F32),
        scratch_shapes=[pltpu.VMEM((TM, D_FF), BF16)],
        compiler_params=_cparams(("arbitrary",)),
        name="mix_ffn_even",
    )(o_f, o_b, a_proj, attn_ctx, attn_lat, x_ctx, x_lat, mod, g_gla, w_out, gamma, mod, mod, mod,
      w_gate, w_up, w_down)


def _inproj_odd_kernel(x_ref, gam_ref, sh_ref, sc_ref, w_ref, gq_ref, gk_ref, cos_ref, sin_ref,
                       q_ref, k_ref, v_ref, kctx_ref, vctx_ref, stage_ref):
    is_lat = _is_lat(TMB)
    scale = DH_C ** -0.5 * LOG2E
    nq, nkv = H_C * DH_C, HKV_C * DH_C

    def part(rows):
        n = rows.stop - rows.start
        hb = _norm_mod(x_ref[rows, :], gam_ref[...], sc_ref[...], sh_ref[...]).astype(BF16)
        cos = jnp.where(is_lat, cos_ref[rows, :], 1.0)
        sin = jnp.where(is_lat, sin_ref[rows, :], 0.0)
        yield
        z = _dot(hb, w_ref[:, nq:nq + nkv])
        for h in range(HKV_C):
            hs = slice(h * DH_C, (h + 1) * DH_C)
            k_n = _rms(z[:, hs], gk_ref[...])
            stage_ref[rows, hs] = k_n
            k_ref[rows, hs] = _rope(k_n, cos, sin, DH_C // 4).astype(BF16)
        v = _dot(hb, w_ref[:, nq + nkv:])
        stage_ref[rows, nkv:] = v
        for h in range(HKV_C):
            v_ref[rows, 2 * h * DH_C:(2 * h + 1) * DH_C] = v[:, h * DH_C:(h + 1) * DH_C].astype(BF16)
            v_ref[rows, (2 * h + 1) * DH_C:2 * (h + 1) * DH_C] = jnp.ones((n, DH_C), BF16)
        for lo in range(0, nq, MXU_N):
            z = _dot(hb, w_ref[:, lo:lo + MXU_N])
            for j in range(0, MXU_N, DH_C):
                q_ref[rows, lo + j:lo + j + DH_C] = (
                    _rope(_rms(z[:, j:j + DH_C], gq_ref[...]), cos, sin, DH_C // 4) * scale).astype(BF16)

    _staged([part(slice(r, r + TM)) for r in range(0, TMB, TM)])
    _copy_ctx(stage_ref, ((0, DH_C, HKV_C, kctx_ref), (nkv, DH_C, HKV_C, vctx_ref)))


def _inproj_odd(x, gamma, mod, w_in, g_q, g_k, cos, sin):
    nq, nkv = H_C * DH_C, HKV_C * DH_C
    return pl.pallas_call(
        _inproj_odd_kernel,
        grid=(M_TOK // TMB,),
        in_specs=[
            _row_spec(D_MODEL, tm=TMB), _const_spec((1, D_MODEL)), _mod_spec(0, TMB), _mod_spec(1, TMB),
            _const_spec((D_MODEL, nq + 2 * nkv)), _const_spec((1, DH_C)), _const_spec((1, DH_C)),
            pl.BlockSpec((TMB, LANES), lambda i: (_pos_block(i, TMB), 0)),
            pl.BlockSpec((TMB, LANES), lambda i: (_pos_block(i, TMB), 0)),
        ],
        out_specs=[_row_spec(nq, tm=TMB), _row_spec(nkv, tm=TMB), _row_spec(2 * nkv, tm=TMB),
                   _ctx_parts_spec(HKV_C, TMB), _ctx_parts_spec(HKV_C, TMB)],
        out_shape=[
            jax.ShapeDtypeStruct((M_TOK, nq), BF16), jax.ShapeDtypeStruct((M_TOK, nkv), BF16),
            jax.ShapeDtypeStruct((M_TOK, 2 * nkv), BF16), jax.ShapeDtypeStruct((N_CTX * HKV_C, LANES), F32),
            jax.ShapeDtypeStruct((N_CTX * HKV_C, LANES), F32),
        ],
        scratch_shapes=[pltpu.VMEM((TMB, 2 * nkv), F32)],
        compiler_params=_cparams(("arbitrary",)),
        name="inproj_odd",
    )(x, gamma, mod, mod, w_in, g_q, g_k, cos, sin)


def _gqa_body(q_ref, kv_refs, o_ref, seqs):
    rep = H_C // HKV_C
    outs, parts = [], []
    for rows in seqs:
        for hk in range(HKV_C):
            ks = slice(hk * DH_C, (hk + 1) * DH_C)
            q_g = jnp.concatenate(
                [q_ref[rows, (hk * rep + g) * DH_C:(hk * rep + g + 1) * DH_C] for g in range(rep)], axis=0)
            kv = [(k_ref[rows if n == 0 else slice(None), ks],
                   v_ref[rows if n == 0 else slice(None), 2 * hk * DH_C:2 * (hk + 1) * DH_C])
                  for n, (k_ref, v_ref) in enumerate(kv_refs)]
            outs.append((rows, hk, []))
            parts.append(_attend(q_g, kv, outs[-1][2]))
    _staged(parts)
    for rows, hk, o in outs:
        tq = o[0].shape[0] // rep
        for g in range(rep):
            o_ref[rows, (hk * rep + g) * DH_C:(hk * rep + g + 1) * DH_C] = o[0][g * tq:(g + 1) * tq].astype(BF16)


def _gqa_kernel(q_ref, k_ref, v_ref, o_ref):
    _gqa_body(q_ref, [(k_ref, v_ref)], o_ref, _seq_rows(q_ref))


def _gqa_cached_kernel(q_ref, k_ref, v_ref, kc_ref, vc_ref, o_ref):
    _gqa_body(q_ref, [(k_ref, v_ref), (kc_ref, vc_ref)], o_ref, [slice(None)])


def _gqa(q, k, v, k_cache, v_cache):
    nq, nkv = H_C * DH_C, HKV_C * DH_C
    ctx = pl.pallas_call(
        _gqa_kernel,
        grid=(N_CTX // TMC,),
        in_specs=[_row_spec(nq, tm=TMC), _row_spec(nkv, tm=TMC), _row_spec(2 * nkv, tm=TMC)],
        out_specs=_row_spec(nq, tm=TMC),
        out_shape=jax.ShapeDtypeStruct((N_CTX, nq), BF16),
        compiler_params=_cparams(("arbitrary",)),
        name="gqa_ctx",
    )(q, k, v)
    nqb = DEC_SEQ // TQ_LAT
    off = N_CTX // TQ_LAT
    seq0 = N_CTX // DEC_SEQ
    lat = pl.pallas_call(
        _gqa_cached_kernel,
        grid=(DEC_BATCH, nqb),
        in_specs=[
            pl.BlockSpec((TQ_LAT, nq), lambda b, j: (off + b * nqb + j, 0)),
            pl.BlockSpec((DEC_SEQ, nkv), lambda b, j: (seq0 + b, 0)),
            pl.BlockSpec((DEC_SEQ, 2 * nkv), lambda b, j: (seq0 + b, 0)),
            pl.BlockSpec((None, PAST_LEN, nkv), lambda b, j: (b, 0, 0)),
            pl.BlockSpec((None, PAST_LEN, 2 * nkv), lambda b, j: (b, 0, 0)),
        ],
        out_specs=pl.BlockSpec((TQ_LAT, nq), lambda b, j: (b * nqb + j, 0)),
        out_shape=jax.ShapeDtypeStruct((N_LAT, nq), BF16),
        compiler_params=_cparams(("arbitrary", "arbitrary")),
        name="gqa_lat",
    )(q, k, v, k_cache, v_cache)
    return ctx, lat


def _postmix_odd_kernel(actx_ref, alat_ref, x_ref, g1_ref, wo_ref, gam_ref, sh_ref, sc_ref, wrh_ref, wrl_ref,
                        x1_ref, h_ref, w1_ref, w2_ref, idx_ref, cnt_ref, run0_ref, run1_ref):
    @pl.when(pl.program_id(0) == 0)
    def _():
        run0_ref[...] = jnp.zeros_like(run0_ref)
        run1_ref[...] = jnp.zeros_like(run1_ref)

    is_lat = _is_lat(TMB)
    lane = lax.broadcasted_iota(jnp.int32, (TMB, LANES), 1)
    lane_f = lane.astype(F32)
    picked = []

    def part(rows):
        attn = jnp.where(is_lat, alat_ref[rows, :], actx_ref[rows, :])
        x1 = x_ref[rows, :] + g1_ref[...] * _dot(attn, wo_ref[...])
        x1_ref[rows, :] = x1
        yield
        h = _norm_mod(x1, gam_ref[...], sc_ref[...], sh_ref[...])
        h_hi, h_lo = _split2(h)
        h_ref[rows, :] = h_hi
        logits = _dot(h_hi, wrh_ref[...]) + (_dot(h_hi, wrl_ref[...]) + _dot(h_lo, wrh_ref[...]))
        yield
        lane_p = lax.broadcasted_iota(jnp.int32, logits.shape, 1).astype(F32)
        lg = jnp.where(lane_p < N_EXPERTS, logits, -jnp.inf)
        m1 = jnp.max(lg, axis=-1, keepdims=True)
        i1 = jnp.min(jnp.where(lg == m1, lane_p, float(LANES)), axis=-1, keepdims=True)
        lg2 = jnp.where(lane_p == i1, -jnp.inf, lg)
        m2 = jnp.max(lg2, axis=-1, keepdims=True)
        i2 = jnp.min(jnp.where(lg2 == m2, lane_p, float(LANES)), axis=-1, keepdims=True)
        e = jnp.exp(m2 - m1)
        w1 = 1.0 / (1.0 + e)
        w1_ref[rows, :] = jnp.broadcast_to(w1, logits.shape)
        w2_ref[rows, :] = jnp.broadcast_to(e * w1, logits.shape)
        picked.append((jnp.broadcast_to(i1, logits.shape), jnp.broadcast_to(i2, logits.shape)))

    _staged([part(slice(r, r + TM)) for r in range(0, TMB, TM)])
    i1 = jnp.concatenate([p[0] for p in picked], axis=0)
    i2 = jnp.concatenate([p[1] for p in picked], axis=0)
    oh1 = jnp.where(lane_f == i1, 1.0, 0.0)
    oh2 = jnp.where(lane_f == i2, 1.0, 0.0)
    r_i = lax.broadcasted_iota(jnp.int32, (TMB, TMB), 0)
    c_i = lax.broadcasted_iota(jnp.int32, (TMB, TMB), 1)
    before = jnp.where(c_i < r_i, 1.0, 0.0).astype(BF16)
    rank1 = jnp.sum((_dot(before, oh1.astype(BF16)) + run0_ref[0:1, :]) * oh1, axis=-1, keepdims=True)
    rank2 = jnp.sum((_dot(before, oh2.astype(BF16)) + run1_ref[0:1, :]) * oh2, axis=-1, keepdims=True)
    run0_ref[...] = run0_ref[...] + jnp.sum(oh1, axis=0, keepdims=True)
    run1_ref[...] = run1_ref[...] + jnp.sum(oh2, axis=0, keepdims=True)
    sub = lax.broadcasted_iota(jnp.int32, cnt_ref.shape, 0)
    cnt_ref[...] = jnp.where(sub == 0, run0_ref[...], run1_ref[...])
    quarter = LANES // 4
    cols = jnp.where(lane < quarter, i1, jnp.where(lane < 2 * quarter, i2, jnp.where(
        lane < 3 * quarter, rank1, rank2)))
    rows_t = jnp.transpose(cols)
    idx_ref[...] = jnp.concatenate(
        [rows_t[q * quarter:q * quarter + 1] for q in range(4)] + [jnp.zeros((4, TMB), F32)],
        axis=0).astype(jnp.int32)


def _postmix_odd(attn_ctx, attn_lat, x, mod, w_out, gamma, w_router):
    wr_hi = w_router.astype(BF16)
    wr_lo = (w_router - wr_hi.astype(F32)).astype(BF16)
    row = functools.partial(_row_spec, tm=TMB)
    return pl.pallas_call(
        _postmix_odd_kernel,
        grid=(M_TOK // TMB,),
        in_specs=[_ctx_spec(H_C * DH_C, TMB), _lat_spec(H_C * DH_C, TMB), row(D_MODEL), _mod_spec(2, TMB),
                  _const_spec((H_C * DH_C, D_MODEL)),
                  _const_spec((1, D_MODEL)), _mod_spec(3, TMB), _mod_spec(4, TMB), _const_spec((D_MODEL, LANES)),
                  _const_spec((D_MODEL, LANES))],
        out_specs=[row(D_MODEL), row(D_MODEL), row(LANES), row(LANES), pl.BlockSpec((8, TMB), lambda i: (0, i)),
                   _const_spec((8, LANES))],
        out_shape=[
            jax.ShapeDtypeStruct((M_TOK, D_MODEL), F32), jax.ShapeDtypeStruct((M_TOK, D_MODEL), BF16),
            jax.ShapeDtypeStruct((M_TOK, LANES), F32), jax.ShapeDtypeStruct((M_TOK, LANES), F32),
            jax.ShapeDtypeStruct((8, M_TOK), jnp.int32), jax.ShapeDtypeStruct((8, LANES), F32),
        ],
        scratch_shapes=[pltpu.VMEM((8, LANES), F32), pltpu.VMEM((8, LANES), F32)],
        compiler_params=_cparams(("arbitrary",)),
        name="postmix_odd",
    )(attn_ctx, attn_lat, x, mod, w_out, gamma, mod, mod, wr_hi, wr_lo)


def _cast_rows(src_ref, dst_ref, piece):
    def body(r, carry):
        rows = pl.ds(pl.multiple_of(r * piece, piece), piece)
        dst_ref[rows, :] = src_ref[rows, :].astype(BF16)
        return carry

    lax.fori_loop(0, src_ref.shape[0] // piece, body, 0)


def _experts_up_kernel(tile_ref, chunk_ref, exp_ref, first_ref, valid_ref, x_ref, wg_ref, wu_ref, act_ref,
                       wgb_ref, wub_ref):
    s = pl.program_id(0)

    @pl.when(first_ref[s] == 1)
    def _():
        _cast_rows(wg_ref, wgb_ref, 128)
        _cast_rows(wu_ref, wub_ref, 128)

    @pl.when(valid_ref[s] == 1)
    def _():
        _swiglu_act(x_ref[...].astype(BF16), wgb_ref, wub_ref, act_ref)

    @pl.when(valid_ref[s] == 0)
    def _():
        piece = 64

        def body(r, carry):
            act_ref[pl.ds(pl.multiple_of(r * piece, piece), piece), :] = jnp.zeros((piece, TF_E), BF16)
            return carry

        lax.fori_loop(0, TME // piece, body, 0)


def _experts_down_kernel(act_tile_ref, exp_ref, first_ref, valid_ref, act_ref, wd_ref, o_ref, wdb_ref):
    i = pl.program_id(0)

    @pl.when(first_ref[i] == 1)
    def _():
        _cast_rows(wd_ref, wdb_ref, 256)

    @pl.when(valid_ref[i] == 1)
    def _():
        o_ref[...] = _dot(act_ref[...], wdb_ref[...])

    @pl.when(valid_ref[i] == 0)
    def _():
        o_ref[...] = jnp.zeros_like(o_ref)


def _experts(up_tables, down_tables, xs, w_gate, w_up, w_down):
    def w_map(s, t, c, e, f, v):
        return e[s], 0, jnp.where(v[s] == 1, c[s], NC_E - 1)

    up = pl.pallas_call(
        _experts_up_kernel,
        grid_spec=pltpu.PrefetchScalarGridSpec(
            num_scalar_prefetch=5,
            grid=(NC_E * NT_E,),
            in_specs=[
                pl.BlockSpec((TME, D_MODEL), lambda s, t, c, e, f, v: (t[s], 0)),
                pl.BlockSpec((None, D_MODEL, TF_E), w_map), pl.BlockSpec((None, D_MODEL, TF_E), w_map),
            ],
            out_specs=pl.BlockSpec((TME, TF_E), lambda s, t, c, e, f, v: (t[s], c[s])),
            scratch_shapes=[pltpu.VMEM((D_MODEL, TF_E), BF16), pltpu.VMEM((D_MODEL, TF_E), BF16)],
        ),
        out_shape=jax.ShapeDtypeStruct((NT_E * TME, D_FF_E), BF16),
        compiler_params=_cparams(("arbitrary",)),
        name="experts_up",
    )(*up_tables, xs, w_gate, w_up)
    return pl.pallas_call(
        _experts_down_kernel,
        grid_spec=pltpu.PrefetchScalarGridSpec(
            num_scalar_prefetch=4,
            grid=(NT_E,),
            in_specs=[
                pl.BlockSpec((TME, D_FF_E), lambda i, a, e, f, v: (a[i], 0)),
                pl.BlockSpec((None, D_FF_E, D_MODEL), lambda i, a, e, f, v: (e[i], 0, 0)),
            ],
            out_specs=pl.BlockSpec((TME, D_MODEL), lambda i, a, e, f, v: (i, 0)),
            scratch_shapes=[pltpu.VMEM((D_FF_E, D_MODEL), BF16)],
        ),
        out_shape=jax.ShapeDtypeStruct((NT_E * TME, D_MODEL), F32),
        compiler_params=_cparams(("arbitrary",)),
        name="experts_down",
    )(*down_tables, up, w_down)


def _combine_kernel(x_ref, y1_ref, y2_ref, w1_ref, w2_ref, g2_ref, gam_ref, octx_ref, olat_ref):
    rep = D_MODEL // LANES
    w1 = jnp.concatenate([w1_ref[...]] * rep, axis=1)
    w2 = jnp.concatenate([w2_ref[...]] * rep, axis=1)
    x2 = x_ref[...] + g2_ref[...] * (w1 * y1_ref[...] + w2 * y2_ref[...])
    y = _rms(x2, gam_ref[...])
    is_lat = _is_lat(TMB)

    @pl.when(jnp.logical_not(is_lat))
    def _():
        octx_ref[...] = y

    @pl.when(is_lat)
    def _():
        olat_ref[...] = y


def _combine(x1, y1, y2, w1, w2, mod, gamma):
    return pl.pallas_call(
        _combine_kernel,
        grid=(M_TOK // TMB,),
        in_specs=[_row_spec(D_MODEL, tm=TMB), _row_spec(D_MODEL, tm=TMB), _row_spec(D_MODEL, tm=TMB),
                  _row_spec(LANES, tm=TMB), _row_spec(LANES, tm=TMB), _mod_spec(5, TMB), _const_spec((1, D_MODEL))],
        out_specs=[_ctx_spec(D_MODEL, TMB), _lat_spec(D_MODEL, TMB)],
        out_shape=[jax.ShapeDtypeStruct((N_CTX, D_MODEL), F32), jax.ShapeDtypeStruct((N_LAT, D_MODEL), F32)],
        compiler_params=_cparams(("arbitrary",)),
        name="combine",
    )(x1, y1, y2, w1, w2, mod, gamma)


def _rope_tables(rot_dim):
    n = DEC_SEQ
    rows = np.repeat(np.arange(n // GRID_W, dtype=np.float64), GRID_W)
    cols = np.tile(np.arange(GRID_W, dtype=np.float64), n // GRID_W)
    half = rot_dim // 2
    freqs = ROPE_THETA ** (-np.arange(0, half, 2, dtype=np.float64) / half)
    ang_r, ang_c = rows[:, None] * freqs, cols[:, None] * freqs
    cos = np.concatenate([np.cos(ang_r)] * 2 + [np.cos(ang_c)] * 2, axis=-1)
    sin = np.concatenate([-np.sin(ang_r), np.sin(ang_r), -np.sin(ang_c), np.sin(ang_c)], axis=-1)
    rep = LANES // rot_dim
    return (jnp.asarray(np.tile(cos, (1, rep)).astype(np.float32)),
            jnp.asarray(np.tile(sin, (1, rep)).astype(np.float32)))


def _with_ones(v):
    return jnp.concatenate([v, jnp.ones_like(v)], axis=-1)


def _rows(a, idx):
    return a.at[idx].get(mode="promise_in_bounds")


def _route(idx, counts_f):
    experts = jnp.arange(N_EXPERTS, dtype=jnp.int32)
    e2 = idx[0:2]
    first_total = counts_f[0, :N_EXPERTS].astype(jnp.int32)
    counts = first_total + counts_f[1, :N_EXPERTS].astype(jnp.int32)
    tiles = (counts + TME - 1) // TME
    tile_end = jnp.cumsum(tiles)
    tile_start = tile_end - tiles
    start = jnp.cumsum(counts) - counts

    def lookup(table, keys):
        return sum(jnp.where(keys == k, table[k], 0) for k in range(N_EXPERTS))

    rank2 = jnp.stack([idx[2], idx[3] + lookup(first_total, e2[1])], axis=0)
    pos = lookup(tile_start * TME, e2) + rank2
    tile_id = jnp.arange(NT_E, dtype=jnp.int32)
    tile_expert = jnp.minimum(jnp.sum((tile_id[:, None] >= tile_end[None, :]).astype(jnp.int32), axis=1),
                              N_EXPERTS - 1).astype(jnp.int32)
    tile_valid = (tile_id < tile_end[-1]).astype(jnp.int32)
    t_oh = tile_expert[:, None] == experts[None, :]
    t_first = jnp.sum(jnp.where(t_oh, (start - tile_start * TME)[None, :], 0), axis=1) + tile_id * TME
    t_last = jnp.sum(jnp.where(t_oh, (start + counts)[None, :], 0), axis=1)
    order = jnp.argsort(e2.reshape(-1), stable=True).astype(jnp.int32)
    n_assign = order.shape[0]
    g_idx = t_first[:, None] + jnp.arange(TME, dtype=jnp.int32)[None, :]
    live = jnp.logical_and(g_idx < t_last[:, None], tile_valid[:, None] == 1)
    src = _rows(order, jnp.clip(g_idx, 0, n_assign - 1).reshape(-1)) % M_TOK
    src_tok = jnp.where(live.reshape(-1), src, 0).astype(jnp.int32)

    n_tiles = tile_end[-1]
    last_tile = jnp.maximum(n_tiles - 1, 0)
    last_expert = jnp.sum(jnp.where(tile_id == last_tile, tile_expert, 0))
    t_tstart = jnp.sum(jnp.where(t_oh, tile_start[None, :], 0), axis=1)
    down_tables = (
        jnp.where(tile_valid == 1, tile_id, last_tile).astype(jnp.int32),
        tile_expert,
        jnp.logical_and(tile_valid == 1, tile_id == t_tstart).astype(jnp.int32),
        tile_valid,
    )
    s_id = jnp.arange(NC_E * NT_E, dtype=jnp.int32)
    s_exp = jnp.minimum(jnp.sum((s_id[:, None] >= NC_E * tile_end[None, :]).astype(jnp.int32), axis=1),
                        N_EXPERTS - 1)
    s_oh = s_exp[:, None] == experts[None, :]
    s_tstart = jnp.sum(jnp.where(s_oh, tile_start[None, :], 0), axis=1)
    s_ntile = jnp.maximum(jnp.sum(jnp.where(s_oh, tiles[None, :], 0), axis=1), 1)
    rel = s_id - NC_E * s_tstart
    s_valid = s_id < NC_E * n_tiles
    spare = s_id - NC_E * n_tiles
    up_tables = (
        jnp.where(s_valid, s_tstart + rel % s_ntile, n_tiles + spare // NC_E).astype(jnp.int32),
        jnp.where(s_valid, rel // s_ntile, spare % NC_E).astype(jnp.int32),
        jnp.where(s_valid, s_exp, last_expert).astype(jnp.int32),
        jnp.logical_and(s_valid, rel % s_ntile == 0).astype(jnp.int32),
        s_valid.astype(jnp.int32),
    )
    return src_tok, pos, up_tables, down_tables


def lambda_init(layer):
    return 0.8 - 0.6 * math.exp(-0.3 * layer)


def kernel(x_prompt, x_sample, state_a, cache_b_k, cache_b_v, cache_c_k, cache_c_v, c, c_ctx, w_mod, b_mod, norm_mix, norm_ffn, w_in_even, w_gate2_a, b_gate_a, g_gla, lam_q1, lam_k1, lam_q2, lam_k2, g_sub_b, w_out_even, w_in_odd, g_q_c, g_k_c, w_out_odd, ffn_gate, ffn_up, ffn_down, w_router, exp_gate, exp_up, exp_down, norm_final):
    x_ctx, x_lat = x_prompt.reshape(N_CTX, D_MODEL), x_sample.reshape(N_LAT, D_MODEL)
    cond =jnp.concatenate([c_ctx[None, :], c, jnp.zeros((N_COND - 1 - DEC_BATCH, D_MODEL), F32)], axis=0)
    mod = _modulation(cond, w_mod, b_mod).reshape(DEPTH, N_COND, 1, 6 * D_MODEL)

    w = w_in_even[0]
    na = 2 * A_QK + 2 * A_V
    gate_lo = na
    gate_hi = na + 2 * GATE_RANK
    col_scale = jnp.concatenate([jnp.full((A_QK,), DK_A ** -0.5, F32), jnp.ones((na - A_QK,), F32),
                                 jnp.full((B_QK,), DH_B ** -0.5, F32), jnp.ones((B_QK + B_V,), F32)])
    w_main = (jnp.concatenate([w[:, :gate_lo], w[:, gate_hi:]], axis=1) * col_scale).astype(BF16)
    w_gates = jnp.pad(w[:, gate_lo:gate_hi], ((0, 0), (0, LANES - 2 * GATE_RANK))).astype(BF16)
    g2 = jnp.zeros((LANES, 2 * A_QK), F32)
    g2 = g2.at[:GATE_RANK, :A_QK].set(w_gate2_a[0, 0]).at[GATE_RANK:2 * GATE_RANK, A_QK:].set(w_gate2_a[0, 1])
    bg = b_gate_a[0].reshape(1, 2 * A_QK)
    cos_b, sin_b = _rope_tables(DH_B)
    a_proj, glog, qb, kb, vb, kb_ctx, vb_ctx = _inproj_even(
        x_ctx, x_lat, norm_mix[0][None, :], mod[0], w_main, w_gates, g2.astype(BF16), bg, cos_b, sin_b)

    s0_all = jnp.concatenate([jnp.zeros((1, 2, A_QK, DV_A), F32),
                              state_a[:, 0].reshape(DEC_BATCH, 2, A_QK, DV_A)], axis=0)
    sel = np.arange(A_QK)[:, None] // DK_A == np.arange(LANES)[None, :] // GLA_SB
    m1 = jnp.asarray(sel.astype(np.float32)).astype(BF16)
    o_f, o_b, s_fin = _gla(a_proj, glog, s0_all, m1)

    k_cache = cache_b_k[:, 0].reshape(DEC_BATCH, PAST_LEN, B_QK).astype(BF16)
    v_cache = _with_ones(cache_b_v[:, 0].astype(BF16)).reshape(DEC_BATCH, PAST_LEN, 2 * B_V)
    lam_params = [p[0][None, :] for p in (lam_q1, lam_k1, lam_q2, lam_k2)]
    attn_b = _diffattn(qb, kb, vb, k_cache, v_cache, lam_params, g_sub_b[0][None, :], lambda_init(0))

    x = _mix_ffn_even(o_f, o_b, a_proj, *attn_b, x_ctx, x_lat, mod[0], g_gla[0][None, :],
                      w_out_even[0].astype(BF16), norm_ffn[0][None, :], ffn_gate[0].astype(BF16),
                      ffn_up[0].astype(BF16), ffn_down[0].astype(BF16))

    cos_c, sin_c = _rope_tables(DH_C)
    q_c, k_c, v_c, kc_ctx, vc_ctx = _inproj_odd(x, norm_mix[1][None, :], mod[1], w_in_odd[0].astype(BF16),
                                     g_q_c[0][None, :], g_k_c[0][None, :], cos_c, sin_c)
    nkv = HKV_C * DH_C
    k_cache = cache_c_k[:, 0].reshape(DEC_BATCH, PAST_LEN, nkv).astype(BF16)
    v_cache = _with_ones(cache_c_v[:, 0].astype(BF16)).reshape(DEC_BATCH, PAST_LEN, 2 * nkv)
    attn_c = _gqa(q_c, k_c, v_c, k_cache, v_cache)
    w_r = jnp.pad(w_router[0], ((0, 0), (0, LANES - N_EXPERTS)))
    x1, h_moe, w1, w2, idx, counts = _postmix_odd(*attn_c, x, mod[1], w_out_odd[0].astype(BF16),
                                                  norm_ffn[1][None, :], w_r)

    src_tok, pos, up_tables, down_tables = _route(idx, counts)
    xs = _rows(h_moe, src_tok)
    ys = _experts(up_tables, down_tables, xs, exp_gate[0], exp_up[0], exp_down[0])
    y1 = _rows(ys, pos[0])
    y2 = _rows(ys, pos[1])
    y_ctx, y_lat = _combine(x1, y1, y2, w1, w2, mod[1], norm_final[None, :])

    y_prompt = y_ctx.reshape(BATCH, SEQ, D_MODEL)
    y_sample = y_lat.reshape(DEC_BATCH, DEC_SEQ, D_MODEL)
    new_state_a = s_fin.reshape(BATCH, 1, 2, H_A, DK_A, DV_A)
    new_b_k = jnp.transpose(kb_ctx.reshape(BATCH, 1, H_B, 2, DH_B, SEQ), (0, 1, 5, 2, 3, 4))
    new_b_v = vb_ctx.reshape(BATCH, 1, SEQ, H_B, DV_B)
    new_c_k = kc_ctx.reshape(BATCH, 1, SEQ, HKV_C, DH_C)
    new_c_v = vc_ctx.reshape(BATCH, 1, SEQ, HKV_C, DH_C)
    return (y_prompt, y_sample, new_state_a, new_b_k, new_b_v, new_c_k, new_c_v)
```
